```python
import jax, jax.numpy as jnp
from jax import lax
import numpy as np

D_MODEL = 2048
BATCH = 8
SEQ = 8192
DEPTH = 2

EPS = 1e-5
N_MEM = 256
D_FF = 5632
GM_CHUNK = 128
GM_GROUPS = 4
GM_WIDTH = D_MODEL
GM_GDIM = GM_WIDTH // GM_GROUPS
SSD_WIDTH = D_MODEL
SSD_HEAD_DIM = 64
SSD_HEADS = SSD_WIDTH // SSD_HEAD_DIM
SSD_GROUPS = 4
SSD_HPG = SSD_HEADS // SSD_GROUPS
SSD_STATE = 128
SSD_CONV = 4
SSD_CHUNK = 128
SSD_BC = SSD_GROUPS * SSD_STATE
SSD_CONV_DIM = SSD_WIDTH + 2 * SSD_BC
EVEN_IN = 2 * GM_WIDTH + SSD_WIDTH + SSD_CONV_DIM + SSD_HEADS
EVEN_MIX = GM_WIDTH + SSD_WIDTH
ATT_HEADS = 32
ATT_KV_HEADS = 4
ATT_HEAD_DIM = D_MODEL // ATT_HEADS
ATT_REP = ATT_HEADS // ATT_KV_HEADS
WINDOW = 128
ATT_SCALE = ATT_HEAD_DIM ** -0.5
ROT_DIM = ATT_HEAD_DIM // 4
ROPE_THETA = 500000.0
ODD_IN = (ATT_HEADS + 2 * ATT_KV_HEADS) * ATT_HEAD_DIM
X_HEADS = 4
X_HEAD_DIM = 128
X_WIDTH = X_HEADS * X_HEAD_DIM
X_SCALE = X_HEAD_DIM ** -0.5
N_EVEN = (DEPTH + 1) // 2
N_ODD = DEPTH // 2

kernel_name = 'hybrid_gmlp_ssd_swa_macaron'


def rmsnorm(x, g):
    xf = x.astype(jnp.float32)
    y = xf * lax.rsqrt(jnp.mean(xf * xf, -1, keepdims=True) + EPS)
    return (y * g.astype(jnp.float32)).astype(x.dtype)


def swiglu(h, w_gu, w_down):
    g, u = jnp.split(h @ w_gu, 2, axis=-1)
    return (jax.nn.silu(g) * u) @ w_down


def chunked_gmlp(uv, ln_g, ln_b, w_s, b_s):
    bsz, L, _ = uv.shape
    nc = L // GM_CHUNK
    u, v = jnp.split(jax.nn.gelu(uv, approximate=False), 2, axis=-1)
    vf = v.reshape(bsz, nc, GM_CHUNK, GM_GROUPS, GM_GDIM).astype(jnp.float32)
    mu = jnp.mean(vf, -1, keepdims=True)
    var = jnp.mean(jnp.square(vf - mu), -1, keepdims=True)
    vn = ((vf - mu) * lax.rsqrt(var + EPS)).astype(v.dtype)
    vn = vn * ln_g.reshape(GM_GROUPS, GM_GDIM) + ln_b.reshape(GM_GROUPS, GM_GDIM)
    causal = jnp.tril(jnp.ones((GM_CHUNK, GM_CHUNK), dtype=bool))
    ws = jnp.where(causal[None], w_s, 0.0)
    s = jnp.einsum('gij,bcjgd->bcigd', ws, vn) + b_s.T[None, None, :, :, None]
    return u * s.reshape(bsz, L, GM_WIDTH)


def causal_dwconv(x, w, b):
    K = w.shape[0]
    L = x.shape[1]
    xp = jnp.pad(x, ((0, 0), (K - 1, 0), (0, 0)))
    y = xp[:, 0:L] * w[0]
    for k in range(1, K):
        y = y + xp[:, k:k + L] * w[k]
    return y + b


def ssd_scan(xs, dt, A, Bm, Cm):
    bsz, L = xs.shape[:2]
    nc = L // SSD_CHUNK
    Q, G, R, P, N = SSD_CHUNK, SSD_GROUPS, SSD_HPG, SSD_HEAD_DIM, SSD_STATE
    x = xs.astype(jnp.float32).reshape(bsz, nc, Q, G, R, P)
    dtc = dt.reshape(bsz, nc, Q, G, R)
    Bc = Bm.astype(jnp.float32).reshape(bsz, nc, Q, G, N)
    Cc = Cm.astype(jnp.float32).reshape(bsz, nc, Q, G, N)
    a = jnp.moveaxis(dtc * A.reshape(G, R), 2, -1)
    a_cs = jnp.cumsum(a, axis=-1)
    xdt = x * dtc[..., None]
    causal = jnp.tril(jnp.ones((Q, Q), dtype=bool))
    seg = a_cs[..., :, None] - a_cs[..., None, :]
    Lmat = jnp.where(causal, jnp.exp(jnp.where(causal, seg, 0.0)), 0.0)
    cb = jnp.einsum('bcign,bcjgn->bcgij', Cc, Bc)
    y_diag = jnp.einsum('bcgij,bcgrij,bcjgrp->bcigrp', cb, Lmat, xdt)
    decay_states = jnp.exp(a_cs[..., -1:] - a_cs)
    states = jnp.einsum('bcjgn,bcgrj,bcjgrp->bcgrpn', Bc, decay_states, xdt)
    chunk_decay = jnp.exp(a_cs[..., -1])

    def step(h, inp):
        s_c, d_c = inp
        return h * d_c[..., None, None] + s_c, h

    h0 = jnp.zeros((bsz, G, R, P, N), jnp.float32)
    _, prev = lax.scan(step, h0, (jnp.moveaxis(states, 1, 0), jnp.moveaxis(chunk_decay, 1, 0)))
    prev = jnp.moveaxis(prev, 0, 1)
    y_off = jnp.einsum('bcign,bcgrpn,bcgri->bcigrp', Cc, prev, jnp.exp(a_cs))
    return (y_diag + y_off).reshape(bsz, L, G * R * P)


def even_mixer(h, w_in, gm_ln_g, gm_ln_b, gm_ws, gm_bs, conv_w, conv_b, dt_bias, a_log, d_skip, ssd_norm, w_out):
    bsz, L, _ = h.shape
    proj = h @ w_in
    c0 = 2 * GM_WIDTH
    c1 = c0 + SSD_WIDTH
    c2 = c1 + SSD_CONV_DIM
    uv, z, xbc, dt_raw = jnp.split(proj, [c0, c1, c2], axis=-1)
    a_out = chunked_gmlp(uv, gm_ln_g, gm_ln_b, gm_ws, gm_bs)
    xbc = jax.nn.silu(causal_dwconv(xbc, conv_w, conv_b))
    xs, Bm, Cm = jnp.split(xbc, [SSD_WIDTH, SSD_WIDTH + SSD_BC], axis=-1)
    dt = jax.nn.softplus(dt_raw.astype(jnp.float32) + dt_bias.astype(jnp.float32))
    A = -jnp.exp(a_log.astype(jnp.float32))
    xh = xs.reshape(bsz, L, SSD_HEADS, SSD_HEAD_DIM)
    y = ssd_scan(xh, dt, A, Bm.reshape(bsz, L, SSD_GROUPS, SSD_STATE), Cm.reshape(bsz, L, SSD_GROUPS, SSD_STATE))
    y = y + (xh.astype(jnp.float32) * d_skip.astype(jnp.float32)[:, None]).reshape(bsz, L, SSD_WIDTH)
    yg = (y * jax.nn.silu(z.astype(jnp.float32))).reshape(bsz, L, SSD_GROUPS, SSD_WIDTH // SSD_GROUPS)
    yg = yg * lax.rsqrt(jnp.mean(yg * yg, -1, keepdims=True) + EPS)
    b_out = (yg.reshape(bsz, L, SSD_WIDTH) * ssd_norm.astype(jnp.float32)).astype(h.dtype)
    return jnp.concatenate([a_out, b_out], axis=-1) @ w_out


def rope_partial(x, cos, sin):
    half = ROT_DIM // 2
    x1 = x[..., :half]
    x2 = x[..., half:ROT_DIM]
    return jnp.concatenate([x1 * cos - x2 * sin, x2 * cos + x1 * sin, x[..., ROT_DIM:]], axis=-1)


def swa_sinks(h, w_qkv, b_qkv, sinks, w_o, cos, sin):
    bsz, L, _ = h.shape
    nb = L // WINDOW
    W = WINDOW
    qkv = h @ w_qkv + b_qkv
    q, k, v = jnp.split(qkv, [ATT_HEADS * ATT_HEAD_DIM, (ATT_HEADS + ATT_KV_HEADS) * ATT_HEAD_DIM], axis=-1)
    q = rope_partial(q.reshape(bsz, L, ATT_HEADS, ATT_HEAD_DIM), cos, sin)
    k = rope_partial(k.reshape(bsz, L, ATT_KV_HEADS, ATT_HEAD_DIM), cos, sin)
    q = q.reshape(bsz, nb, W, ATT_KV_HEADS, ATT_REP, ATT_HEAD_DIM)
    kb = k.reshape(bsz, nb, W, ATT_KV_HEADS, ATT_HEAD_DIM)
    vb = v.reshape(bsz, nb, W, ATT_KV_HEADS, ATT_HEAD_DIM)
    pad = ((0, 0), (1, 0), (0, 0), (0, 0), (0, 0))
    kcat = jnp.concatenate([jnp.pad(kb, pad)[:, :-1], kb], axis=2)
    vcat = jnp.concatenate([jnp.pad(vb, pad)[:, :-1], vb], axis=2)
    s = jnp.einsum('bnqkrd,bnskd->bnkrqs', q, kcat).astype(jnp.float32) * ATT_SCALE
    iq = jnp.arange(W)[:, None]
    js = jnp.arange(2 * W)[None, :]
    rel = iq + W - js
    band = (rel >= 0) & (rel < WINDOW)
    blk = jnp.arange(nb)[:, None, None]
    mask = band[None] & ((blk > 0) | (js >= W)[None])
    s = jnp.where(mask[None, :, None, None], s, -jnp.inf)
    sink = sinks.astype(jnp.float32).reshape(ATT_KV_HEADS, ATT_REP)[None, None, :, :, None, None]
    m = jnp.maximum(jnp.max(s, -1, keepdims=True), sink)
    p = jnp.exp(s - m)
    pr = (p / (jnp.sum(p, -1, keepdims=True) + jnp.exp(sink - m))).astype(vcat.dtype)
    o = jnp.einsum('bnkrqs,bnskd->bnqkrd', pr, vcat).reshape(bsz, L, ATT_HEADS * ATT_HEAD_DIM)
    return o @ w_o


def mem_cross_attn(h, mem_n, w_q, w_kv, w_o):
    bsz, L, _ = h.shape
    q = (h @ w_q).reshape(bsz, L, X_HEADS, X_HEAD_DIM)
    k, v = jnp.split(mem_n @ w_kv, 2, axis=-1)
    k = k.reshape(bsz, -1, X_HEADS, X_HEAD_DIM)
    v = v.reshape(bsz, -1, X_HEADS, X_HEAD_DIM)
    s = jnp.einsum('blhd,bmhd->bhlm', q, k).astype(jnp.float32) * X_SCALE
    p = jax.nn.softmax(s, axis=-1).astype(v.dtype)
    o = jnp.einsum('bhlm,bmhd->blhd', p, v).reshape(bsz, L, X_WIDTH)
    return o @ w_o


def _fwd_setup_inputs(seed: int = 0) -> dict:
    key = jax.random.key(seed)
    ks = list(jax.random.split(key, 40))
    f32 = jnp.float32

    def nrm(i, shape, scale):
        return jax.random.normal(ks[i], shape, f32) * scale

    def gain(i, shape):
        return 1.0 + 0.02 * jax.random.normal(ks[i], shape, f32)

    x = nrm(0, (BATCH, SEQ, D_MODEL), 1.0)
    mem = nrm(1, (BATCH, N_MEM, D_MODEL), 1.0)
    start = jax.random.randint(ks[2], (BATCH, 1), 0, 4096, dtype=jnp.int32)
    positions = start + jnp.arange(SEQ, dtype=jnp.int32)[None, :]
    dt0 = jnp.exp(jax.random.uniform(ks[20], (N_EVEN, SSD_HEADS), f32, np.log(1e-3), np.log(1e-1)))
    return {
        'x': x,
        'mem': mem,
        'positions': positions,
        'norm_ffn1': gain(3, (DEPTH, D_MODEL)),
        'w_ffn1_gu': nrm(4, (DEPTH, D_MODEL, 2 * D_FF), D_MODEL ** -0.5),
        'w_ffn1_down': nrm(5, (DEPTH, D_FF, D_MODEL), D_FF ** -0.5),
        'norm_mix': gain(6, (DEPTH, D_MODEL)),
        'w_in_even': nrm(7, (N_EVEN, D_MODEL, EVEN_IN), D_MODEL ** -0.5),
        'gm_ln_g': gain(8, (N_EVEN, GM_WIDTH)),
        'gm_ln_b': nrm(9, (N_EVEN, GM_WIDTH), 0.02),
        'gm_ws': nrm(10, (N_EVEN, GM_GROUPS, GM_CHUNK, GM_CHUNK), GM_CHUNK ** -0.5),
        'gm_bs': 1.0 + nrm(11, (N_EVEN, GM_GROUPS, GM_CHUNK), 0.1),
        'conv_w': nrm(12, (N_EVEN, SSD_CONV, SSD_CONV_DIM), SSD_CONV ** -0.5),
        'conv_b': nrm(13, (N_EVEN, SSD_CONV_DIM), 0.02),
        'dt_bias': dt0 + jnp.log(-jnp.expm1(-dt0)),
        'a_log': jnp.log(jax.random.uniform(ks[14], (N_EVEN, SSD_HEADS), f32, 1.0, 16.0)),
        'd_skip': 1.0 + nrm(15, (N_EVEN, SSD_HEADS), 0.1),
        'ssd_norm': gain(16, (N_EVEN, SSD_WIDTH)),
        'w_out_even': nrm(17, (N_EVEN, EVEN_MIX, D_MODEL), EVEN_MIX ** -0.5),
        'w_qkv': nrm(18, (N_ODD, D_MODEL, ODD_IN), D_MODEL ** -0.5),
        'b_qkv': nrm(19, (N_ODD, ODD_IN), 0.02),
        'sinks': nrm(21, (N_ODD, ATT_HEADS), 0.5),
        'w_o_odd': nrm(22, (N_ODD, ATT_HEADS * ATT_HEAD_DIM, D_MODEL), (ATT_HEADS * ATT_HEAD_DIM) ** -0.5),
        'norm_xq': gain(23, (DEPTH, D_MODEL)),
        'norm_mem': gain(24, (DEPTH, D_MODEL)),
        'w_xq': nrm(25, (DEPTH, D_MODEL, X_WIDTH), D_MODEL ** -0.5),
        'w_xkv': nrm(26, (DEPTH, D_MODEL, 2 * X_WIDTH), D_MODEL ** -0.5),
        'w_xo': nrm(27, (DEPTH, X_WIDTH, D_MODEL), X_WIDTH ** -0.5),
        'norm_ffn2': gain(28, (DEPTH, D_MODEL)),
        'w_ffn2_gu': nrm(29, (DEPTH, D_MODEL, 2 * D_FF), D_MODEL ** -0.5),
        'w_ffn2_down': nrm(30, (DEPTH, D_FF, D_MODEL), D_FF ** -0.5),
        'final_norm': gain(31, (D_MODEL,)),
    }


def _fwd_reference(x, mem, positions, norm_ffn1, w_ffn1_gu, w_ffn1_down, norm_mix, w_in_even, gm_ln_g, gm_ln_b,
              gm_ws, gm_bs, conv_w, conv_b, dt_bias, a_log, d_skip, ssd_norm, w_out_even, w_qkv, b_qkv, sinks,
              w_o_odd, norm_xq, norm_mem, w_xq, w_xkv, w_xo, norm_ffn2, w_ffn2_gu, w_ffn2_down, final_norm):
    inv_freq = ROPE_THETA ** (-jnp.arange(0, ROT_DIM, 2, dtype=jnp.float32) / ROT_DIM)
    ang = positions.astype(jnp.float32)[..., None] * inv_freq
    cos = jnp.cos(ang)[:, :, None, :].astype(x.dtype)
    sin = jnp.sin(ang)[:, :, None, :].astype(x.dtype)
    for i in range(DEPTH):
        j = i // 2
        x = x + 0.5 * swiglu(rmsnorm(x, norm_ffn1[i]), w_ffn1_gu[i], w_ffn1_down[i])
        h = rmsnorm(x, norm_mix[i])
        if i % 2 == 0:
            x = x + even_mixer(h, w_in_even[j], gm_ln_g[j], gm_ln_b[j], gm_ws[j], gm_bs[j], conv_w[j], conv_b[j],
                               dt_bias[j], a_log[j], d_skip[j], ssd_norm[j], w_out_even[j])
        else:
            x = x + swa_sinks(h, w_qkv[j], b_qkv[j], sinks[j], w_o_odd[j], cos, sin)
        x = x + mem_cross_attn(rmsnorm(x, norm_xq[i]), rmsnorm(mem, norm_mem[i]), w_xq[i], w_xkv[i], w_xo[i])
        x = x + 0.5 * swiglu(rmsnorm(x, norm_ffn2[i]), w_ffn2_gu[i], w_ffn2_down[i])
    return rmsnorm(x, final_norm)


import jax as _jax
import jax.numpy as _jnp

TWIN_FORMAT = 'train_step'
FWD_PARAMS = ['x', 'mem', 'positions', 'norm_ffn1', 'w_ffn1_gu', 'w_ffn1_down', 'norm_mix', 'w_in_even', 'gm_ln_g', 'gm_ln_b', 'gm_ws', 'gm_bs', 'conv_w', 'conv_b', 'dt_bias', 'a_log', 'd_skip', 'ssd_norm', 'w_out_even', 'w_qkv', 'b_qkv', 'sinks', 'w_o_odd', 'norm_xq', 'norm_mem', 'w_xq', 'w_xkv', 'w_xo', 'norm_ffn2', 'w_ffn2_gu', 'w_ffn2_down', 'final_norm']
TWIN_WEIGHTS = ['norm_ffn1', 'w_ffn1_gu', 'w_ffn1_down', 'norm_mix', 'w_in_even', 'gm_ln_g', 'gm_ln_b', 'gm_ws', 'gm_bs', 'conv_w', 'conv_b', 'dt_bias', 'a_log', 'd_skip', 'ssd_norm', 'w_out_even', 'w_qkv', 'b_qkv', 'sinks', 'w_o_odd', 'norm_xq', 'norm_mem', 'w_xq', 'w_xkv', 'w_xo', 'norm_ffn2', 'w_ffn2_gu', 'w_ffn2_down', 'final_norm']
TWIN_DIFF_INPUT = 'x'
TWIN_INPUTS = ['x', 'mem', 'positions', 'norm_ffn1', 'w_ffn1_gu', 'w_ffn1_down', 'norm_mix', 'w_in_even', 'gm_ln_g', 'gm_ln_b', 'gm_ws', 'gm_bs', 'conv_w', 'conv_b', 'dt_bias', 'a_log', 'd_skip', 'ssd_norm', 'w_out_even', 'w_qkv', 'b_qkv', 'sinks', 'w_o_odd', 'norm_xq', 'norm_mem', 'w_xq', 'w_xkv', 'w_xo', 'norm_ffn2', 'w_ffn2_gu', 'w_ffn2_down', 'final_norm', 'loss_target', 'm_norm_ffn1', 'm_w_ffn1_gu', 'm_w_ffn1_down', 'm_norm_mix', 'm_w_in_even', 'm_gm_ln_g', 'm_gm_ln_b', 'm_gm_ws', 'm_gm_bs', 'm_conv_w', 'm_conv_b', 'm_dt_bias', 'm_a_log', 'm_d_skip', 'm_ssd_norm', 'm_w_out_even', 'm_w_qkv', 'm_b_qkv', 'm_sinks', 'm_w_o_odd', 'm_norm_xq', 'm_norm_mem', 'm_w_xq', 'm_w_xkv', 'm_w_xo', 'm_norm_ffn2', 'm_w_ffn2_gu', 'm_w_ffn2_down', 'm_final_norm', 'v_norm_ffn1', 'v_w_ffn1_gu', 'v_w_ffn1_down', 'v_norm_mix', 'v_w_in_even', 'v_gm_ln_g', 'v_gm_ln_b', 'v_gm_ws', 'v_gm_bs', 'v_conv_w', 'v_conv_b', 'v_dt_bias', 'v_a_log', 'v_d_skip', 'v_ssd_norm', 'v_w_out_even', 'v_w_qkv', 'v_b_qkv', 'v_sinks', 'v_w_o_odd', 'v_norm_xq', 'v_norm_mem', 'v_w_xq', 'v_w_xkv', 'v_w_xo', 'v_norm_ffn2', 'v_w_ffn2_gu', 'v_w_ffn2_down', 'v_final_norm']
TWIN_OUTPUTS = ['loss', 'grad_x', 'grad_norm_ffn1', 'grad_w_ffn1_gu', 'grad_w_ffn1_down', 'grad_norm_mix', 'grad_w_in_even', 'grad_gm_ln_g', 'grad_gm_ln_b', 'grad_gm_ws', 'grad_gm_bs', 'grad_conv_w', 'grad_conv_b', 'grad_dt_bias', 'grad_a_log', 'grad_d_skip', 'grad_ssd_norm', 'grad_w_out_even', 'grad_w_qkv', 'grad_b_qkv', 'grad_sinks', 'grad_w_o_odd', 'grad_norm_xq', 'grad_norm_mem', 'grad_w_xq', 'grad_w_xkv', 'grad_w_xo', 'grad_norm_ffn2', 'grad_w_ffn2_gu', 'grad_w_ffn2_down', 'grad_final_norm', 'delta_norm_ffn1', 'delta_w_ffn1_gu', 'delta_w_ffn1_down', 'delta_norm_mix', 'delta_w_in_even', 'delta_gm_ln_g', 'delta_gm_ln_b', 'delta_gm_ws', 'delta_gm_bs', 'delta_conv_w', 'delta_conv_b', 'delta_dt_bias', 'delta_a_log', 'delta_d_skip', 'delta_ssd_norm', 'delta_w_out_even', 'delta_w_qkv', 'delta_b_qkv', 'delta_sinks', 'delta_w_o_odd', 'delta_norm_xq', 'delta_norm_mem', 'delta_w_xq', 'delta_w_xkv', 'delta_w_xo', 'delta_norm_ffn2', 'delta_w_ffn2_gu', 'delta_w_ffn2_down', 'delta_final_norm', 'new_m_norm_ffn1', 'new_m_w_ffn1_gu', 'new_m_w_ffn1_down', 'new_m_norm_mix', 'new_m_w_in_even', 'new_m_gm_ln_g', 'new_m_gm_ln_b', 'new_m_gm_ws', 'new_m_gm_bs', 'new_m_conv_w', 'new_m_conv_b', 'new_m_dt_bias', 'new_m_a_log', 'new_m_d_skip', 'new_m_ssd_norm', 'new_m_w_out_even', 'new_m_w_qkv', 'new_m_b_qkv', 'new_m_sinks', 'new_m_w_o_odd', 'new_m_norm_xq', 'new_m_norm_mem', 'new_m_w_xq', 'new_m_w_xkv', 'new_m_w_xo', 'new_m_norm_ffn2', 'new_m_w_ffn2_gu', 'new_m_w_ffn2_down', 'new_m_final_norm', 'new_v_norm_ffn1', 'new_v_w_ffn1_gu', 'new_v_w_ffn1_down', 'new_v_norm_mix', 'new_v_w_in_even', 'new_v_gm_ln_g', 'new_v_gm_ln_b', 'new_v_gm_ws', 'new_v_gm_bs', 'new_v_conv_w', 'new_v_conv_b', 'new_v_dt_bias', 'new_v_a_log', 'new_v_d_skip', 'new_v_ssd_norm', 'new_v_w_out_even', 'new_v_w_qkv', 'new_v_b_qkv', 'new_v_sinks', 'new_v_w_o_odd', 'new_v_norm_xq', 'new_v_norm_mem', 'new_v_w_xq', 'new_v_w_xkv', 'new_v_w_xo', 'new_v_norm_ffn2', 'new_v_w_ffn2_gu', 'new_v_w_ffn2_down', 'new_v_final_norm']
TWIN_LEAF_KINDS = {'loss': 'loss', 'grad_x': 'grad_x', 'grad_norm_ffn1': 'grad_w', 'grad_w_ffn1_gu': 'grad_w', 'grad_w_ffn1_down': 'grad_w', 'grad_norm_mix': 'grad_w', 'grad_w_in_even': 'grad_w', 'grad_gm_ln_g': 'grad_w', 'grad_gm_ln_b': 'grad_w', 'grad_gm_ws': 'grad_w', 'grad_gm_bs': 'grad_w', 'grad_conv_w': 'grad_w', 'grad_conv_b': 'grad_w', 'grad_dt_bias': 'grad_w', 'grad_a_log': 'grad_w', 'grad_d_skip': 'grad_w', 'grad_ssd_norm': 'grad_w', 'grad_w_out_even': 'grad_w', 'grad_w_qkv': 'grad_w', 'grad_b_qkv': 'grad_w', 'grad_sinks': 'grad_w', 'grad_w_o_odd': 'grad_w', 'grad_norm_xq': 'grad_w', 'grad_norm_mem': 'grad_w', 'grad_w_xq': 'grad_w', 'grad_w_xkv': 'grad_w', 'grad_w_xo': 'grad_w', 'grad_norm_ffn2': 'grad_w', 'grad_w_ffn2_gu': 'grad_w', 'grad_w_ffn2_down': 'grad_w', 'grad_final_norm': 'grad_w', 'delta_norm_ffn1': 'delta_w', 'delta_w_ffn1_gu': 'delta_w', 'delta_w_ffn1_down': 'delta_w', 'delta_norm_mix': 'delta_w', 'delta_w_in_even': 'delta_w', 'delta_gm_ln_g': 'delta_w', 'delta_gm_ln_b': 'delta_w', 'delta_gm_ws': 'delta_w', 'delta_gm_bs': 'delta_w', 'delta_conv_w': 'delta_w', 'delta_conv_b': 'delta_w', 'delta_dt_bias': 'delta_w', 'delta_a_log': 'delta_w', 'delta_d_skip': 'delta_w', 'delta_ssd_norm': 'delta_w', 'delta_w_out_even': 'delta_w', 'delta_w_qkv': 'delta_w', 'delta_b_qkv': 'delta_w', 'delta_sinks': 'delta_w', 'delta_w_o_odd': 'delta_w', 'delta_norm_xq': 'delta_w', 'delta_norm_mem': 'delta_w', 'delta_w_xq': 'delta_w', 'delta_w_xkv': 'delta_w', 'delta_w_xo': 'delta_w', 'delta_norm_ffn2': 'delta_w', 'delta_w_ffn2_gu': 'delta_w', 'delta_w_ffn2_down': 'delta_w', 'delta_final_norm': 'delta_w', 'new_m_norm_ffn1': 'new_m', 'new_m_w_ffn1_gu': 'new_m', 'new_m_w_ffn1_down': 'new_m', 'new_m_norm_mix': 'new_m', 'new_m_w_in_even': 'new_m', 'new_m_gm_ln_g': 'new_m', 'new_m_gm_ln_b': 'new_m', 'new_m_gm_ws': 'new_m', 'new_m_gm_bs': 'new_m', 'new_m_conv_w': 'new_m', 'new_m_conv_b': 'new_m', 'new_m_dt_bias': 'new_m', 'new_m_a_log': 'new_m', 'new_m_d_skip': 'new_m', 'new_m_ssd_norm': 'new_m', 'new_m_w_out_even': 'new_m', 'new_m_w_qkv': 'new_m', 'new_m_b_qkv': 'new_m', 'new_m_sinks': 'new_m', 'new_m_w_o_odd': 'new_m', 'new_m_norm_xq': 'new_m', 'new_m_norm_mem': 'new_m', 'new_m_w_xq': 'new_m', 'new_m_w_xkv': 'new_m', 'new_m_w_xo': 'new_m', 'new_m_norm_ffn2': 'new_m', 'new_m_w_ffn2_gu': 'new_m', 'new_m_w_ffn2_down': 'new_m', 'new_m_final_norm': 'new_m', 'new_v_norm_ffn1': 'new_v', 'new_v_w_ffn1_gu': 'new_v', 'new_v_w_ffn1_down': 'new_v', 'new_v_norm_mix': 'new_v', 'new_v_w_in_even': 'new_v', 'new_v_gm_ln_g': 'new_v', 'new_v_gm_ln_b': 'new_v', 'new_v_gm_ws': 'new_v', 'new_v_gm_bs': 'new_v', 'new_v_conv_w': 'new_v', 'new_v_conv_b': 'new_v', 'new_v_dt_bias': 'new_v', 'new_v_a_log': 'new_v', 'new_v_d_skip': 'new_v', 'new_v_ssd_norm': 'new_v', 'new_v_w_out_even': 'new_v', 'new_v_w_qkv': 'new_v', 'new_v_b_qkv': 'new_v', 'new_v_sinks': 'new_v', 'new_v_w_o_odd': 'new_v', 'new_v_norm_xq': 'new_v', 'new_v_norm_mem': 'new_v', 'new_v_w_xq': 'new_v', 'new_v_w_xkv': 'new_v', 'new_v_w_xo': 'new_v', 'new_v_norm_ffn2': 'new_v', 'new_v_w_ffn2_gu': 'new_v', 'new_v_w_ffn2_down': 'new_v', 'new_v_final_norm': 'new_v'}


def _forward(args):
    return _fwd_reference(*[args[k] for k in FWD_PARAMS])


def _output_shape():
    def fwd():
        inp = _fwd_setup_inputs(0)
        return _fwd_reference(*[inp[k] for k in FWD_PARAMS])
    out = _jax.eval_shape(fwd)
    return out.shape, out.dtype

N_MICROBATCH = 1
ADAM_LR = 0.001
ADAM_B1 = 0.9
ADAM_B2 = 0.999
ADAM_EPS = 1e-08
ADAM_WD = 0.01
ADAM_STEP = 10
PER_EXAMPLE_BATCH_AXIS = {'x': 0, 'mem': 0, 'positions': 0, 'loss_target': 0}
SHARED_INPUTS = []
_WEIGHT_DTYPES = {'norm_ffn1': _jnp.float32, 'w_ffn1_gu': _jnp.float32, 'w_ffn1_down': _jnp.float32, 'norm_mix': _jnp.float32, 'w_in_even': _jnp.float32, 'gm_ln_g': _jnp.float32, 'gm_ln_b': _jnp.float32, 'gm_ws': _jnp.float32, 'gm_bs': _jnp.float32, 'conv_w': _jnp.float32, 'conv_b': _jnp.float32, 'dt_bias': _jnp.float32, 'a_log': _jnp.float32, 'd_skip': _jnp.float32, 'ssd_norm': _jnp.float32, 'w_out_even': _jnp.float32, 'w_qkv': _jnp.float32, 'b_qkv': _jnp.float32, 'sinks': _jnp.float32, 'w_o_odd': _jnp.float32, 'norm_xq': _jnp.float32, 'norm_mem': _jnp.float32, 'w_xq': _jnp.float32, 'w_xkv': _jnp.float32, 'w_xo': _jnp.float32, 'norm_ffn2': _jnp.float32, 'w_ffn2_gu': _jnp.float32, 'w_ffn2_down': _jnp.float32, 'final_norm': _jnp.float32}
MOMENT_SCALE = {'norm_ffn1': 5.499250e-02, 'w_ffn1_gu': 2.293551e-02, 'w_ffn1_down': 3.743395e-02, 'norm_mix': 9.109020e-02, 'w_in_even': 5.842846e-02, 'gm_ln_g': 3.233218e-02, 'gm_ln_b': 3.347667e-02, 'gm_ws': 6.356798e-02, 'gm_bs': 8.575598e-02, 'conv_w': 6.110804e-02, 'conv_b': 8.043642e-02, 'dt_bias': 1.328918e-01, 'a_log': 3.814071e-01, 'd_skip': 4.130929e-01, 'ssd_norm': 7.018817e-02, 'w_out_even': 9.291201e-02, 'w_qkv': 2.782948e-02, 'b_qkv': 1.208805e-01, 'sinks': 1.740500e-02, 'w_o_odd': 2.313875e-02, 'norm_xq': 9.432881e-03, 'norm_mem': 1.427814e-02, 'w_xq': 1.902307e-02, 'w_xkv': 1.981780e-02, 'w_xo': 1.030404e-02, 'norm_ffn2': 3.843086e-02, 'w_ffn2_gu': 1.611168e-02, 'w_ffn2_down': 2.632177e-02, 'final_norm': 3.196566e+01}


def _to_microbatches(a, axis):
    t = _jnp.moveaxis(a, axis, 0)
    t = t.reshape((N_MICROBATCH, t.shape[0] // N_MICROBATCH) + t.shape[1:])
    return _jnp.moveaxis(t, 1, axis + 1)


def setup_inputs(seed: int = 0) -> dict:
    inp = _fwd_setup_inputs(seed)
    key = _jax.random.fold_in(_jax.random.key(seed), 7919)
    shape, _ = _output_shape()
    out = dict(inp)
    out["loss_target"] = _jax.random.normal(_jax.random.fold_in(key, 0), shape, _jnp.float32)
    for i, name in enumerate(TWIN_WEIGHTS):
        w = inp[name].astype(_jnp.float32)
        if MOMENT_SCALE is None:
            s = _jnp.sqrt(_jnp.mean(_jnp.square(w)) + 1e-30)
        else:
            s = MOMENT_SCALE[name]
        km, kv = _jax.random.split(_jax.random.fold_in(key, i + 1))
        out[name] = w
        out["m_" + name] = s * _jax.random.normal(km, w.shape, _jnp.float32)
        out["v_" + name] = (s * s) * _jax.random.uniform(kv, w.shape, _jnp.float32, 0.5, 1.5)
    if N_MICROBATCH > 1:
        for name, axis in PER_EXAMPLE_BATCH_AXIS.items():
            out[name] = _to_microbatches(out[name], axis)
    return {'x': out['x'], 'mem': out['mem'], 'positions': out['positions'], 'norm_ffn1': out['norm_ffn1'], 'w_ffn1_gu': out['w_ffn1_gu'], 'w_ffn1_down': out['w_ffn1_down'], 'norm_mix': out['norm_mix'], 'w_in_even': out['w_in_even'], 'gm_ln_g': out['gm_ln_g'], 'gm_ln_b': out['gm_ln_b'], 'gm_ws': out['gm_ws'], 'gm_bs': out['gm_bs'], 'conv_w': out['conv_w'], 'conv_b': out['conv_b'], 'dt_bias': out['dt_bias'], 'a_log': out['a_log'], 'd_skip': out['d_skip'], 'ssd_norm': out['ssd_norm'], 'w_out_even': out['w_out_even'], 'w_qkv': out['w_qkv'], 'b_qkv': out['b_qkv'], 'sinks': out['sinks'], 'w_o_odd': out['w_o_odd'], 'norm_xq': out['norm_xq'], 'norm_mem': out['norm_mem'], 'w_xq': out['w_xq'], 'w_xkv': out['w_xkv'], 'w_xo': out['w_xo'], 'norm_ffn2': out['norm_ffn2'], 'w_ffn2_gu': out['w_ffn2_gu'], 'w_ffn2_down': out['w_ffn2_down'], 'final_norm': out['final_norm'], 'loss_target': out['loss_target'], 'm_norm_ffn1': out['m_norm_ffn1'], 'm_w_ffn1_gu': out['m_w_ffn1_gu'], 'm_w_ffn1_down': out['m_w_ffn1_down'], 'm_norm_mix': out['m_norm_mix'], 'm_w_in_even': out['m_w_in_even'], 'm_gm_ln_g': out['m_gm_ln_g'], 'm_gm_ln_b': out['m_gm_ln_b'], 'm_gm_ws': out['m_gm_ws'], 'm_gm_bs': out['m_gm_bs'], 'm_conv_w': out['m_conv_w'], 'm_conv_b': out['m_conv_b'], 'm_dt_bias': out['m_dt_bias'], 'm_a_log': out['m_a_log'], 'm_d_skip': out['m_d_skip'], 'm_ssd_norm': out['m_ssd_norm'], 'm_w_out_even': out['m_w_out_even'], 'm_w_qkv': out['m_w_qkv'], 'm_b_qkv': out['m_b_qkv'], 'm_sinks': out['m_sinks'], 'm_w_o_odd': out['m_w_o_odd'], 'm_norm_xq': out['m_norm_xq'], 'm_norm_mem': out['m_norm_mem'], 'm_w_xq': out['m_w_xq'], 'm_w_xkv': out['m_w_xkv'], 'm_w_xo': out['m_w_xo'], 'm_norm_ffn2': out['m_norm_ffn2'], 'm_w_ffn2_gu': out['m_w_ffn2_gu'], 'm_w_ffn2_down': out['m_w_ffn2_down'], 'm_final_norm': out['m_final_norm'], 'v_norm_ffn1': out['v_norm_ffn1'], 'v_w_ffn1_gu': out['v_w_ffn1_gu'], 'v_w_ffn1_down': out['v_w_ffn1_down'], 'v_norm_mix': out['v_norm_mix'], 'v_w_in_even': out['v_w_in_even'], 'v_gm_ln_g': out['v_gm_ln_g'], 'v_gm_ln_b': out['v_gm_ln_b'], 'v_gm_ws': out['v_gm_ws'], 'v_gm_bs': out['v_gm_bs'], 'v_conv_w': out['v_conv_w'], 'v_conv_b': out['v_conv_b'], 'v_dt_bias': out['v_dt_bias'], 'v_a_log': out['v_a_log'], 'v_d_skip': out['v_d_skip'], 'v_ssd_norm': out['v_ssd_norm'], 'v_w_out_even': out['v_w_out_even'], 'v_w_qkv': out['v_w_qkv'], 'v_b_qkv': out['v_b_qkv'], 'v_sinks': out['v_sinks'], 'v_w_o_odd': out['v_w_o_odd'], 'v_norm_xq': out['v_norm_xq'], 'v_norm_mem': out['v_norm_mem'], 'v_w_xq': out['v_w_xq'], 'v_w_xkv': out['v_w_xkv'], 'v_w_xo': out['v_w_xo'], 'v_norm_ffn2': out['v_norm_ffn2'], 'v_w_ffn2_gu': out['v_w_ffn2_gu'], 'v_w_ffn2_down': out['v_w_ffn2_down'], 'v_final_norm': out['v_final_norm']}


def _loss(weights, diff, rest, loss_target):
    with _jax.named_scope("forward"):
        args = {**rest, TWIN_DIFF_INPUT: diff, **{k: w.astype(_WEIGHT_DTYPES[k]) for k, w in weights.items()}}
        y = _forward(args)
    with _jax.named_scope("loss_head"):
        err = _jnp.square(y.astype(_jnp.float32) - loss_target)
        return 0.5 * _jnp.sum(_jnp.mean(err, axis=-1)) if err.ndim else 0.5 * err


def _adamw(w, g, m, v):
    m = ADAM_B1 * m + (1.0 - ADAM_B1) * g
    v = ADAM_B2 * v + (1.0 - ADAM_B2) * _jnp.square(g)
    m_hat = m / (1.0 - ADAM_B1 ** ADAM_STEP)
    v_hat = v / (1.0 - ADAM_B2 ** ADAM_STEP)
    delta = -ADAM_LR * (m_hat / (_jnp.sqrt(v_hat) + ADAM_EPS) + ADAM_WD * w)
    return delta, m, v


def reference(x, mem, positions, norm_ffn1, w_ffn1_gu, w_ffn1_down, norm_mix, w_in_even, gm_ln_g, gm_ln_b, gm_ws, gm_bs, conv_w, conv_b, dt_bias, a_log, d_skip, ssd_norm, w_out_even, w_qkv, b_qkv, sinks, w_o_odd, norm_xq, norm_mem, w_xq, w_xkv, w_xo, norm_ffn2, w_ffn2_gu, w_ffn2_down, final_norm, loss_target, m_norm_ffn1, m_w_ffn1_gu, m_w_ffn1_down, m_norm_mix, m_w_in_even, m_gm_ln_g, m_gm_ln_b, m_gm_ws, m_gm_bs, m_conv_w, m_conv_b, m_dt_bias, m_a_log, m_d_skip, m_ssd_norm, m_w_out_even, m_w_qkv, m_b_qkv, m_sinks, m_w_o_odd, m_norm_xq, m_norm_mem, m_w_xq, m_w_xkv, m_w_xo, m_norm_ffn2, m_w_ffn2_gu, m_w_ffn2_down, m_final_norm, v_norm_ffn1, v_w_ffn1_gu, v_w_ffn1_down, v_norm_mix, v_w_in_even, v_gm_ln_g, v_gm_ln_b, v_gm_ws, v_gm_bs, v_conv_w, v_conv_b, v_dt_bias, v_a_log, v_d_skip, v_ssd_norm, v_w_out_even, v_w_qkv, v_b_qkv, v_sinks, v_w_o_odd, v_norm_xq, v_norm_mem, v_w_xq, v_w_xkv, v_w_xo, v_norm_ffn2, v_w_ffn2_gu, v_w_ffn2_down, v_final_norm):
    given = dict(x=x, mem=mem, positions=positions, norm_ffn1=norm_ffn1, w_ffn1_gu=w_ffn1_gu, w_ffn1_down=w_ffn1_down, norm_mix=norm_mix, w_in_even=w_in_even, gm_ln_g=gm_ln_g, gm_ln_b=gm_ln_b, gm_ws=gm_ws, gm_bs=gm_bs, conv_w=conv_w, conv_b=conv_b, dt_bias=dt_bias, a_log=a_log, d_skip=d_skip, ssd_norm=ssd_norm, w_out_even=w_out_even, w_qkv=w_qkv, b_qkv=b_qkv, sinks=sinks, w_o_odd=w_o_odd, norm_xq=norm_xq, norm_mem=norm_mem, w_xq=w_xq, w_xkv=w_xkv, w_xo=w_xo, norm_ffn2=norm_ffn2, w_ffn2_gu=w_ffn2_gu, w_ffn2_down=w_ffn2_down, final_norm=final_norm, loss_target=loss_target, m_norm_ffn1=m_norm_ffn1, m_w_ffn1_gu=m_w_ffn1_gu, m_w_ffn1_down=m_w_ffn1_down, m_norm_mix=m_norm_mix, m_w_in_even=m_w_in_even, m_gm_ln_g=m_gm_ln_g, m_gm_ln_b=m_gm_ln_b, m_gm_ws=m_gm_ws, m_gm_bs=m_gm_bs, m_conv_w=m_conv_w, m_conv_b=m_conv_b, m_dt_bias=m_dt_bias, m_a_log=m_a_log, m_d_skip=m_d_skip, m_ssd_norm=m_ssd_norm, m_w_out_even=m_w_out_even, m_w_qkv=m_w_qkv, m_b_qkv=m_b_qkv, m_sinks=m_sinks, m_w_o_odd=m_w_o_odd, m_norm_xq=m_norm_xq, m_norm_mem=m_norm_mem, m_w_xq=m_w_xq, m_w_xkv=m_w_xkv, m_w_xo=m_w_xo, m_norm_ffn2=m_norm_ffn2, m_w_ffn2_gu=m_w_ffn2_gu, m_w_ffn2_down=m_w_ffn2_down, m_final_norm=m_final_norm, v_norm_ffn1=v_norm_ffn1, v_w_ffn1_gu=v_w_ffn1_gu, v_w_ffn1_down=v_w_ffn1_down, v_norm_mix=v_norm_mix, v_w_in_even=v_w_in_even, v_gm_ln_g=v_gm_ln_g, v_gm_ln_b=v_gm_ln_b, v_gm_ws=v_gm_ws, v_gm_bs=v_gm_bs, v_conv_w=v_conv_w, v_conv_b=v_conv_b, v_dt_bias=v_dt_bias, v_a_log=v_a_log, v_d_skip=v_d_skip, v_ssd_norm=v_ssd_norm, v_w_out_even=v_w_out_even, v_w_qkv=v_w_qkv, v_b_qkv=v_b_qkv, v_sinks=v_sinks, v_w_o_odd=v_w_o_odd, v_norm_xq=v_norm_xq, v_norm_mem=v_norm_mem, v_w_xq=v_w_xq, v_w_xkv=v_w_xkv, v_w_xo=v_w_xo, v_norm_ffn2=v_norm_ffn2, v_w_ffn2_gu=v_w_ffn2_gu, v_w_ffn2_down=v_w_ffn2_down, v_final_norm=v_final_norm)
    weights = {n: given[n] for n in TWIN_WEIGHTS}
    shared = {n: given[n] for n in SHARED_INPUTS}
    per_example = {n: given[n] for n in ['x', 'mem', 'positions']}
    grad_fn = _jax.value_and_grad(_loss, argnums=(0, 1))

    def one_microbatch(ex, loss_target):
        ex = dict(ex)
        diff = ex.pop(TWIN_DIFF_INPUT)
        return grad_fn(weights, diff, {**shared, **ex}, loss_target)

    if N_MICROBATCH == 1:
        loss, (grad_w, grad_x) = one_microbatch(per_example, given["loss_target"])
    else:
        def body(carry, xs):
            loss_sum, grad_sum = carry
            l_k, (gw_k, gx_k) = one_microbatch(xs[0], xs[1])
            with _jax.named_scope("update"):
                return (loss_sum + l_k, _jax.tree.map(_jnp.add, grad_sum, gw_k)), gx_k

        init = (_jnp.zeros((), _jnp.float32), _jax.tree.map(_jnp.zeros_like, weights))
        (loss, grad_w), grad_x = _jax.lax.scan(body, init, (per_example, given["loss_target"]))
    with _jax.named_scope("update"):
        delta_w, new_m, new_v = {}, {}, {}
        for n in TWIN_WEIGHTS:
            delta_w[n], new_m[n], new_v[n] = _adamw(weights[n], grad_w[n], given["m_" + n], given["v_" + n])
    return (loss, grad_x, *[grad_w[n] for n in TWIN_WEIGHTS], *[delta_w[n] for n in TWIN_WEIGHTS],
            *[new_m[n] for n in TWIN_WEIGHTS], *[new_v[n] for n in TWIN_WEIGHTS])
```

```python
import functools
import math

import jax
import jax.numpy as jnp
from jax import lax
from jax.experimental import pallas as pl
from jax.experimental.pallas import tpu as pltpu

F32, BF16 = jnp.float32, jnp.bfloat16
S = jax.ShapeDtypeStruct
MESH = pl.DeviceIdType.MESH

D = 2048
DFF = 5632
EPS = 1e-5
CH = 128
GM_G, GM_GD = 4, 512
NH, HD, NG, HPG, NS = 32, 64, 4, 8, 128
CONV_C = 3072
EVEN_MAIN, EVEN_IN = 9216, 9248
AH, AKV, AREP, AHD = 32, 4, 8, 64
ODD_IN = 2560
XH, XHD, XW = 4, 128, 512
ATT_SCALE = AHD ** -0.5
X_SCALE = XHD ** -0.5
ROPE_THETA = 500000.0
ROT = 16
LR, B1, B2, AEPS, WD, STEP = 0.001, 0.9, 0.999, 1e-08, 0.01, 10
LANE = 128
VMEM_LIMIT_V7X = 56 * 1024 * 1024


def _params(*sem):
    return pltpu.CompilerParams(dimension_semantics=sem, vmem_limit_bytes=VMEM_LIMIT_V7X)


def _tile(dim, target):
    if dim <= target:
        return dim
    t = (target // LANE) * LANE
    while t > LANE and dim % t:
        t -= LANE
    assert dim % t == 0, (dim, target)
    return t


class Op:
    def __init__(self, arr, kind=None, layer=0):
        self.arr, self.kind, self.layer = arr, kind, layer
        if kind is None:
            self.R, self.C = arr.shape
        else:
            L = arr.shape[0]
            self.R = arr.shape[2] * (L if kind == "r" else 1)
            self.C = arr.shape[3] * (L if kind == "c" else 1)

    def unit(self, axis):
        if self.kind == "r" and axis == 0:
            return self.arr.shape[2]
        if self.kind == "c" and axis == 1:
            return self.arr.shape[3]
        return (self.R, self.C)[axis]

    def spec(self, tr, tc, pick):
        if self.kind is None:
            return pl.BlockSpec((tr, tc), lambda i, j, k: pick(i, j, k))
        l = self.layer
        if self.kind == "c":
            per = self.arr.shape[3] // tc
            return pl.BlockSpec((None, None, tr, tc),
                                lambda i, j, k: (pick(i, j, k)[1] // per, l, pick(i, j, k)[0], pick(i, j, k)[1] % per))
        per = self.arr.shape[2] // tr
        return pl.BlockSpec((None, None, tr, tc),
                            lambda i, j, k: (pick(i, j, k)[0] // per, l, pick(i, j, k)[0] % per, pick(i, j, k)[1]))


_DIMS = {"nn": (((1,), (0,)), ((), ())), "nt": (((1,), (1,)), ((), ())), "tn": (((0,), (0,)), ((), ()))}
_PICK_A = {"nn": lambda i, j, k: (i, k), "nt": lambda i, j, k: (i, k), "tn": lambda i, j, k: (k, i)}
_PICK_B = {"nn": lambda i, j, k: (k, j), "nt": lambda i, j, k: (j, k), "tn": lambda i, j, k: (k, j)}


def _mm(name, mode, a, b, out_dtype, *, out=None, res=None, bias=None, scale=1.0, into=None,
        tm_t=1024, tn_t=1024, tk_t=512):
    if not isinstance(a, Op):
        a = Op(a)
    if not isinstance(b, Op):
        b = Op(b)
    if mode == "nn":
        M, K, N = a.R, a.C, b.C
        assert b.R == K
        um, uk, un = a.unit(0), math.gcd(a.unit(1), b.unit(0)), b.unit(1)
    elif mode == "nt":
        M, K, N = a.R, a.C, b.R
        assert b.C == K
        um, uk, un = a.unit(0), math.gcd(a.unit(1), b.unit(1)), b.unit(0)
    else:
        K, M, N = a.R, a.C, b.C
        assert b.R == K
        um, uk, un = a.unit(1), math.gcd(a.unit(0), b.unit(0)), b.unit(1)
    if out is not None:
        okind, oL, olayers, olayer = out
        if okind == "c":
            un = math.gcd(un, N // oL)
        else:
            um = math.gcd(um, M // oL)
    tm, tn, tk = _tile(um, tm_t), _tile(un, tn_t), _tile(uk, tk_t)
    gi, gj, gk = M // tm, N // tn, K // tk
    a_blk = (tm, tk) if mode != "tn" else (tk, tm)
    b_blk = {"nn": (tk, tn), "nt": (tn, tk), "tn": (tk, tn)}[mode]
    in_specs = [a.spec(*a_blk, _PICK_A[mode]), b.spec(*b_blk, _PICK_B[mode])]
    args = [a.arr, b.arr]
    if res is not None:
        in_specs.append(pl.BlockSpec((tm, tn), lambda i, j, k: (i, j)))
        args.append(res)
    if bias is not None:
        in_specs.append(pl.BlockSpec((1, tn), lambda i, j, k: (0, j)))
        args.append(bias)
    aliases = {}
    if out is None:
        out_shape = S((M, N), out_dtype)
        out_spec = pl.BlockSpec((tm, tn), lambda i, j, k: (i, j))
    else:
        shp = (oL, olayers, M, N // oL) if okind == "c" else (oL, olayers, M // oL, N)
        out_shape = S(shp, out_dtype)
        out_spec = Op(out_shape, okind, olayer).spec(tm, tn, lambda i, j, k: (i, j))
        if into is not None:
            in_specs.append(pl.BlockSpec(memory_space=pl.ANY))
            args.append(into)
            aliases = {len(args) - 1: 0}
    has_res, has_bias, has_into = res is not None, bias is not None, into is not None
    dims = _DIMS[mode]

    def body(a_ref, b_ref, *rest):
        rest = list(rest)
        res_ref = rest.pop(0) if has_res else None
        bias_ref = rest.pop(0) if has_bias else None
        if has_into:
            rest.pop(0)
        o_ref, acc = rest
        k = pl.program_id(2)

        @pl.when(k == 0)
        def _():
            acc[...] = jnp.zeros_like(acc)

        acc[...] += lax.dot_general(a_ref[...].astype(BF16), b_ref[...].astype(BF16), dims,
                                    preferred_element_type=F32)

        @pl.when(k == gk - 1)
        def _():
            r = acc[...]
            if scale != 1.0:
                r = r * scale
            if has_bias:
                r = r + bias_ref[...]
            if has_res:
                r = r + res_ref[...]
            o_ref[...] = r.astype(o_ref.dtype)

    return pl.pallas_call(
        body, grid=(gi, gj, gk), in_specs=in_specs, out_specs=out_spec, out_shape=out_shape,
        scratch_shapes=[pltpu.VMEM((tm, tn), F32)], input_output_aliases=aliases,
        compiler_params=_params("parallel", "parallel", "arbitrary"), name=name)(*args)


def _rms_fwd(name, x, gain):
    T = x.shape[0]
    tt = _tile(T, 512)

    def body(x_ref, g_ref, o_ref):
        xv = x_ref[...]
        r = lax.rsqrt(jnp.mean(xv * xv, -1, keepdims=True) + EPS)
        o_ref[...] = (xv * r * g_ref[...]).astype(BF16)

    return pl.pallas_call(
        body, grid=(T // tt,),
        in_specs=[pl.BlockSpec((tt, D), lambda i: (i, 0)), pl.BlockSpec((1, D), lambda i: (0, 0))],
        out_specs=pl.BlockSpec((tt, D), lambda i: (i, 0)), out_shape=S((T, D), BF16),
        compiler_params=_params("parallel"), name=name)(x, gain)


def _rms_bwd(name, x, gain, dh, dx_in=None):
    T = x.shape[0]
    tt = _tile(T, 512)
    has_in = dx_in is not None

    def body(x_ref, g_ref, dh_ref, *rest):
        rest = list(rest)
        dxin_ref = rest.pop(0) if has_in else None
        dx_ref, dxb_ref, dg_ref = rest
        xv = x_ref[...]
        r = lax.rsqrt(jnp.mean(xv * xv, -1, keepdims=True) + EPS)
        xh = xv * r
        dy = dh_ref[...].astype(F32)
        dxh = dy * g_ref[...]
        dx = r * (dxh - xh * jnp.mean(dxh * xh, -1, keepdims=True))
        if has_in:
            dx = dx + dxin_ref[...]
        dx_ref[...] = dx
        dxb_ref[...] = dx.astype(BF16)
        part = jnp.sum(dy * xh, 0, keepdims=True)

        @pl.when(pl.program_id(0) == 0)
        def _():
            dg_ref[...] = part

        @pl.when(pl.program_id(0) > 0)
        def _():
            dg_ref[...] += part

    row = pl.BlockSpec((tt, D), lambda i: (i, 0))
    vec = pl.BlockSpec((1, D), lambda i: (0, 0))
    args = [x, gain, dh] + ([dx_in] if has_in else [])
    return pl.pallas_call(
        body, grid=(T // tt,), in_specs=[row, vec, row] + ([row] if has_in else []),
        out_specs=[row, row, vec], out_shape=[S((T, D), F32), S((T, D), BF16), S((1, D), F32)],
        compiler_params=_params("arbitrary"), name=name)(*args)


def _sigmoid(x):
    return 1.0 / (1.0 + jnp.exp(-x))


def _ffn_up(name, h, w4, layer):
    T = h.shape[0]
    n_sh = w4.shape[3]
    tm, tn, tk = _tile(T, 512), _tile(n_sh, 1408), _tile(D, 512)
    per = n_sh // tn
    gk = D // tk

    def body(h_ref, wg_ref, wu_ref, g_ref, u_ref, a_ref, accg, accu):
        k = pl.program_id(2)

        @pl.when(k == 0)
        def _():
            accg[...] = jnp.zeros_like(accg)
            accu[...] = jnp.zeros_like(accu)

        hv = h_ref[...]
        accg[...] += jnp.dot(hv, wg_ref[...], preferred_element_type=F32)
        accu[...] += jnp.dot(hv, wu_ref[...], preferred_element_type=F32)

        @pl.when(k == gk - 1)
        def _():
            g, u = accg[...], accu[...]
            g_ref[...] = g.astype(BF16)
            u_ref[...] = u.astype(BF16)
            a_ref[...] = (g * _sigmoid(g) * u).astype(BF16)

    o = pl.BlockSpec((tm, tn), lambda i, j, k: (i, j))
    return pl.pallas_call(
        body, grid=(T // tm, DFF // tn, gk),
        in_specs=[pl.BlockSpec((tm, tk), lambda i, j, k: (i, k)),
                  pl.BlockSpec((None, None, tk, tn), lambda i, j, k: (j // per, layer, k, j % per)),
                  pl.BlockSpec((None, None, tk, tn), lambda i, j, k: (2 + j // per, layer, k, j % per))],
        out_specs=[o, o, o], out_shape=[S((T, DFF), BF16)] * 3,
        scratch_shapes=[pltpu.VMEM((tm, tn), F32)] * 2,
        compiler_params=_params("parallel", "parallel", "arbitrary"), name=name)(h, w4, w4)


def _ffn_dact(name, dxb, wd4, layer, g, u):
    T = dxb.shape[0]
    r_sh = wd4.shape[2]
    tm, tn, tk = _tile(T, 512), _tile(r_sh, 1408), _tile(D, 512)
    per = r_sh // tn
    gk = D // tk

    def body(dx_ref, w_ref, g_ref, u_ref, o_ref, acc):
        k = pl.program_id(2)

        @pl.when(k == 0)
        def _():
            acc[...] = jnp.zeros_like(acc)

        acc[...] += lax.dot_general(dx_ref[...], w_ref[...], _DIMS["nt"], preferred_element_type=F32)

        @pl.when(k == gk - 1)
        def _():
            da = acc[...] * 0.5
            gv, uv = g_ref[...].astype(F32), u_ref[...].astype(F32)
            sg = _sigmoid(gv)
            o_ref[0, 0] = (da * uv * sg * (1.0 + gv * (1.0 - sg))).astype(BF16)
            o_ref[1, 0] = (da * gv * sg).astype(BF16)

    t = pl.BlockSpec((tm, tn), lambda i, j, k: (i, j))
    return pl.pallas_call(
        body, grid=(T // tm, DFF // tn, gk),
        in_specs=[pl.BlockSpec((tm, tk), lambda i, j, k: (i, k)),
                  pl.BlockSpec((None, None, tn, tk), lambda i, j, k: (j // per, layer, j % per, k)), t, t],
        out_specs=pl.BlockSpec((2, 1, tm, tn), lambda i, j, k: (0, 0, i, j)), out_shape=S((2, 1, T, DFF), BF16),
        scratch_shapes=[pltpu.VMEM((tm, tn), F32)],
        compiler_params=_params("parallel", "parallel", "arbitrary"), name=name)(dxb, wd4, g, u)


def _gelu(x):
    return 0.5 * x * (1.0 + lax.erf(x * 0.7071067811865476))


def _causal(n):
    return lax.broadcasted_iota(jnp.int32, (n, n), 0) >= lax.broadcasted_iota(jnp.int32, (n, n), 1)


def _gmlp_math(u_raw, v_raw, lng, lnb, ws, bs):
    causal = _causal(CH)
    outs = []
    for g in range(GM_G):
        u, v = _gelu(u_raw[g]), _gelu(v_raw[g])
        mu = jnp.mean(v, -1, keepdims=True)
        var = jnp.mean(jnp.square(v - mu), -1, keepdims=True)
        vn = (v - mu) * lax.rsqrt(var + EPS) * lng[g] + lnb[g]
        wm = jnp.where(causal, ws[g], 0.0)
        s = jnp.dot(wm.astype(BF16), vn.astype(BF16), preferred_element_type=F32) + bs[g]
        outs.append(u * s)
    return outs


def _gmlp_load(proj_ref, lng_ref, lnb_ref, ws_ref, bs_ref):
    sl = lambda g, off: slice(off + g * GM_GD, off + (g + 1) * GM_GD)
    u_raw = [proj_ref[:, sl(g, 0)].astype(F32) for g in range(GM_G)]
    v_raw = [proj_ref[:, sl(g, D)].astype(F32) for g in range(GM_G)]
    lng = [lng_ref[:, sl(g, 0)] for g in range(GM_G)]
    lnb = [lnb_ref[:, sl(g, 0)] for g in range(GM_G)]
    ws = [ws_ref[g] for g in range(GM_G)]
    bs = [bs_ref[g] for g in range(GM_G)]
    return u_raw, v_raw, lng, lnb, ws, bs


_GM_PAR = lambda: [pl.BlockSpec((1, D), lambda i: (0, 0)), pl.BlockSpec((1, D), lambda i: (0, 0)),
                   pl.BlockSpec((GM_G, CH, CH), lambda i: (0, 0, 0)), pl.BlockSpec((GM_G, CH, 1), lambda i: (0, 0, 0))]


def _gmlp_fwd(name, proj, lng, lnb, ws, bs):
    T = proj.shape[0]

    def body(proj_ref, lng_ref, lnb_ref, ws_ref, bs_ref, o_ref):
        outs = _gmlp_math(*_gmlp_load(proj_ref, lng_ref, lnb_ref, ws_ref, bs_ref))
        for g in range(GM_G):
            o_ref[:, g * GM_GD:(g + 1) * GM_GD] = outs[g].astype(BF16)

    return pl.pallas_call(
        body, grid=(T // CH,), in_specs=[pl.BlockSpec((CH, 2 * D), lambda i: (i, 0))] + _GM_PAR(),
        out_specs=pl.BlockSpec((CH, D), lambda i: (i, 0)), out_shape=S((T, 2 * D), BF16),
        compiler_params=_params("parallel"), name=name)(proj, lng, lnb, ws, bs)


def _acc_store(first, ref, idx, val):
    @pl.when(first)
    def _():
        ref[idx] = val

    @pl.when(jnp.logical_not(first))
    def _():
        ref[idx] += val


def _gmlp_bwd(name, proj, lng, lnb, ws, bs, dmix, dproj):
    T = proj.shape[0]

    def body(proj_ref, lng_ref, lnb_ref, ws_ref, bs_ref, dmix_ref, _, dproj_ref, dlng_ref, dlnb_ref, dws_ref, dbs_ref):
        first = pl.program_id(0) == 0
        prim = _gmlp_load(proj_ref, lng_ref, lnb_ref, ws_ref, bs_ref)
        _, vjp = jax.vjp(_gmlp_math, *prim)
        du, dv, dlng, dlnb, dws, dbs = vjp([dmix_ref[:, g * GM_GD:(g + 1) * GM_GD].astype(F32) for g in range(GM_G)])
        for g in range(GM_G):
            sl = slice(g * GM_GD, (g + 1) * GM_GD)
            dproj_ref[:, sl] = du[g].astype(BF16)
            dproj_ref[:, D + g * GM_GD:D + (g + 1) * GM_GD] = dv[g].astype(BF16)
            _acc_store(first, dlng_ref, (slice(None), sl), dlng[g])
            _acc_store(first, dlnb_ref, (slice(None), sl), dlnb[g])
            _acc_store(first, dws_ref, g, dws[g])
            _acc_store(first, dbs_ref, g, dbs[g])

    par = _GM_PAR()
    return pl.pallas_call(
        body, grid=(T // CH,),
        in_specs=[pl.BlockSpec((CH, 2 * D), lambda i: (i, 0))] + par +
                 [pl.BlockSpec((CH, D), lambda i: (i, 0)), pl.BlockSpec(memory_space=pl.ANY)],
        out_specs=[pl.BlockSpec((CH, 2 * D), lambda i: (i, 0))] + par,
        out_shape=[S(dproj.shape, BF16), S((1, D), F32), S((1, D), F32), S((GM_G, CH, CH), F32), S((GM_G, CH, 1), F32)],
        input_output_aliases={6: 0}, compiler_params=_params("arbitrary"), name=name)(proj, lng, lnb, ws, bs, dmix, dproj)


CONV_TT = 256
HALO = 8


def _shift_rows(cur, halo_after, s):
    if s == 0:
        return cur
    n = cur.shape[0]
    return pltpu.roll(jnp.concatenate([cur, halo_after], 0), s, 0)[:n]


def _conv_fwd(name, proj, w, b):
    T = proj.shape[0]
    tt = _tile(T, CONV_TT)
    hb = tt // HALO

    def body(x_ref, halo_ref, w_ref, b_ref, y_ref, xc_ref):
        i = pl.program_id(0)
        x = x_ref[...].astype(F32)
        halo = halo_ref[...].astype(F32) * (i > 0).astype(F32)
        y = b_ref[...] + w_ref[3:4, :] * x
        for s in (1, 2, 3):
            y = y + w_ref[3 - s:4 - s, :] * _shift_rows(x, halo, s)
        y_ref[...] = y.astype(BF16)
        xc_ref[...] = (y * _sigmoid(y)).astype(BF16)

    o = pl.BlockSpec((tt, CONV_C), lambda i: (i, 0))
    return pl.pallas_call(
        body, grid=(T // tt,),
        in_specs=[pl.BlockSpec((tt, CONV_C), lambda i: (i, 2)),
                  pl.BlockSpec((HALO, CONV_C), lambda i: (jnp.maximum(i * hb - 1, 0), 2)),
                  pl.BlockSpec((4, CONV_C), lambda i: (0, 0)), pl.BlockSpec((1, CONV_C), lambda i: (0, 0))],
        out_specs=[o, o], out_shape=[S((T, CONV_C), BF16)] * 2,
        compiler_params=_params("parallel"), name=name)(proj, proj, w, b)


def _conv_bwd(name, proj, ypre, dxc, w, dproj):
    T = proj.shape[0]
    tt = _tile(T, CONV_TT)
    hb = tt // HALO
    nt = T // tt

    def dsilu(y):
        sg = _sigmoid(y)
        return sg * (1.0 + y * (1.0 - sg))

    def body(x_ref, xh_ref, y_ref, yn_ref, d_ref, dn_ref, w_ref, _, dproj_ref, dw_ref, db_ref):
        i = pl.program_id(0)
        first = i == 0
        x = x_ref[...].astype(F32)
        halo = xh_ref[...].astype(F32) * (i > 0).astype(F32)
        dy = d_ref[...].astype(F32) * dsilu(y_ref[...].astype(F32))
        dyn = dn_ref[...].astype(F32) * dsilu(yn_ref[...].astype(F32)) * (i < nt - 1).astype(F32)
        ext = jnp.concatenate([dy, dyn], 0)
        dx = w_ref[3:4, :] * dy
        _acc_store(first, dw_ref, (slice(3, 4), slice(None)), jnp.sum(x * dy, 0, keepdims=True))
        for s in (1, 2, 3):
            dx = dx + w_ref[3 - s:4 - s, :] * pltpu.roll(ext, tt + HALO - s, 0)[:tt]
            _acc_store(first, dw_ref, (slice(3 - s, 4 - s), slice(None)),
                       jnp.sum(_shift_rows(x, halo, s) * dy, 0, keepdims=True))
        _acc_store(first, db_ref, (slice(None), slice(None)), jnp.sum(dy, 0, keepdims=True))
        dproj_ref[...] = dx.astype(BF16)

    cur = pl.BlockSpec((tt, CONV_C), lambda i: (i, 0))
    nxt = pl.BlockSpec((HALO, CONV_C), lambda i: (jnp.minimum((i + 1) * hb, T // HALO - 1), 0))
    return pl.pallas_call(
        body, grid=(nt,),
        in_specs=[pl.BlockSpec((tt, CONV_C), lambda i: (i, 2)),
                  pl.BlockSpec((HALO, CONV_C), lambda i: (jnp.maximum(i * hb - 1, 0), 2)),
                  cur, nxt, cur, nxt, pl.BlockSpec((4, CONV_C), lambda i: (0, 0)), pl.BlockSpec(memory_space=pl.ANY)],
        out_specs=[pl.BlockSpec((tt, CONV_C), lambda i: (i, 2)), pl.BlockSpec((4, CONV_C), lambda i: (0, 0)),
                   pl.BlockSpec((1, CONV_C), lambda i: (0, 0))],
        out_shape=[S(dproj.shape, BF16), S((4, CONV_C), F32), S((1, CONV_C), F32)],
        input_output_aliases={7: 0}, compiler_params=_params("arbitrary"), name=name)(proj, proj, ypre, ypre, dxc, dxc, w, dproj)


def _softplus(x):
    return jnp.maximum(x, 0.0) + jnp.log(1.0 + jnp.exp(-jnp.abs(x)))


def _ssd_math(x, Bm, Cm, dtr, z, prev, dtb, alog, dsk, nrm):
    hi = lax.Precision.HIGHEST
    causal = _causal(CH)
    tri = causal.astype(F32)
    lane = lax.broadcasted_iota(jnp.int32, (1, LANE), 1)
    sub = lax.broadcasted_iota(jnp.int32, (LANE, 1), 0)
    dt = _softplus(dtr + dtb)
    a = dt * (-jnp.exp(alog))
    a_cs = jnp.dot(tri, a, preferred_element_type=F32, precision=hi)
    a_csT = lax.dot_general(a, tri, (((0,), (1,)), ((), ())), preferred_element_type=F32, precision=hi)
    cb = [lax.dot_general(Cm[g].astype(BF16), Bm[g].astype(BF16), _DIMS["nt"], preferred_element_type=F32)
          for g in range(NG)]
    ys, new = [], []
    for h in range(NH):
        g = h // HPG
        ohl = (lane == h).astype(F32)
        col = jnp.sum(a_cs * ohl, 1, keepdims=True)
        row = jnp.sum(a_csT * (sub == h).astype(F32), 0, keepdims=True)
        dtc = jnp.sum(dt * ohl, 1, keepdims=True)
        a_last = jnp.sum(a * ohl, (0, 1), keepdims=True)
        lmat = jnp.where(causal, jnp.exp(jnp.where(causal, col - row, 0.0)), 0.0)
        xdt = x[h] * dtc
        y = jnp.dot((cb[g] * lmat).astype(BF16), xdt.astype(BF16), preferred_element_type=F32)
        y = y + jnp.exp(col) * lax.dot_general(Cm[g].astype(BF16), prev[h].astype(BF16), _DIMS["nt"],
                                               preferred_element_type=F32)
        st = lax.dot_general((xdt * jnp.exp(a_last - col)).astype(BF16), Bm[g].astype(BF16), _DIMS["tn"],
                             preferred_element_type=F32)
        new.append(prev[h] * jnp.exp(a_last) + st)
        ys.append(y + x[h] * dsk[h])
    outs = []
    for g in range(NG):
        yg = jnp.concatenate(ys[g * HPG:(g + 1) * HPG], 1) * (z[g] * _sigmoid(z[g]))
        yg = yg * lax.rsqrt(jnp.mean(yg * yg, -1, keepdims=True) + EPS)
        outs.append(yg * nrm[g])
    return outs, new


def _ssd_load(xc_ref, dtr_ref, z_ref, state_ref, dtb_ref, alog_ref, dsk_ref, nrm_ref):
    x = [xc_ref[:, h * HD:(h + 1) * HD].astype(F32) for h in range(NH)]
    Bm = [xc_ref[:, D + g * NS:D + (g + 1) * NS].astype(F32) for g in range(NG)]
    Cm = [xc_ref[:, D + NG * NS + g * NS:D + NG * NS + (g + 1) * NS].astype(F32) for g in range(NG)]
    z = [z_ref[:, g * 512:(g + 1) * 512].astype(F32) for g in range(NG)]
    prev = [state_ref[h * HD:(h + 1) * HD, :] for h in range(NH)]
    dsk = [dsk_ref[:, h:h + 1] for h in range(NH)]
    nrm = [nrm_ref[:, g * 512:(g + 1) * 512] for g in range(NG)]
    return x, Bm, Cm, dtr_ref[...], z, prev, dtb_ref[...], alog_ref[...], dsk, nrm


_SSD_PAR = lambda: [pl.BlockSpec((1, LANE), lambda c: (0, 0))] * 3 + [pl.BlockSpec((1, D), lambda c: (0, 0))]


def _ssd_fwd(name, xc, dtr, proj, dtb, alog, dsk, nrm, mix):
    T = xc.shape[0]
    nc = T // CH

    def body(xc_ref, dtr_ref, z_ref, dtb_ref, alog_ref, dsk_ref, nrm_ref, _, mix_ref, prev_ref, state):
        @pl.when(pl.program_id(0) == 0)
        def _():
            state[...] = jnp.zeros_like(state)

        prev_ref[...] = state[...]
        outs, new = _ssd_math(*_ssd_load(xc_ref, dtr_ref, z_ref, state, dtb_ref, alog_ref, dsk_ref, nrm_ref))
        for g in range(NG):
            mix_ref[:, g * 512:(g + 1) * 512] = outs[g].astype(BF16)
        for h in range(NH):
            state[h * HD:(h + 1) * HD, :] = new[h]

    return pl.pallas_call(
        body, grid=(nc,),
        in_specs=[pl.BlockSpec((CH, CONV_C), lambda c: (c, 0)), pl.BlockSpec((CH, LANE), lambda c: (c, 0)),
                  pl.BlockSpec((CH, D), lambda c: (c, 2))] + _SSD_PAR() + [pl.BlockSpec(memory_space=pl.ANY)],
        out_specs=[pl.BlockSpec((CH, D), lambda c: (c, 1)), pl.BlockSpec((None, NH * HD, NS), lambda c: (c, 0, 0))],
        out_shape=[S(mix.shape, BF16), S((nc, NH * HD, NS), F32)],
        scratch_shapes=[pltpu.VMEM((NH * HD, NS), F32)], input_output_aliases={7: 0},
        compiler_params=_params("arbitrary"), name=name)(xc, dtr, proj, dtb, alog, dsk, nrm, mix)


def _ssd_bwd(name, xc, dtr, proj, prevs, dtb, alog, dsk, nrm, dmix, dproj):
    T = xc.shape[0]
    nc = T // CH
    rev = lambda c: nc - 1 - c

    def body(xc_ref, dtr_ref, z_ref, prev_ref, dtb_ref, alog_ref, dsk_ref, nrm_ref, dmix_ref, _,
             dproj_ref, dxc_ref, ddtr_ref, ddtb_ref, dalog_ref, ddsk_ref, dnrm_ref, dstate):
        first = pl.program_id(0) == 0

        @pl.when(first)
        def _():
            dstate[...] = jnp.zeros_like(dstate)
            ddsk_ref[...] = jnp.zeros_like(ddsk_ref)

        prim = _ssd_load(xc_ref, dtr_ref, z_ref, prev_ref, dtb_ref, alog_ref, dsk_ref, nrm_ref)
        _, vjp = jax.vjp(_ssd_math, *prim)
        douts = [dmix_ref[:, g * 512:(g + 1) * 512].astype(F32) for g in range(NG)]
        dnew = [dstate[h * HD:(h + 1) * HD, :] for h in range(NH)]
        dx, dB, dC, ddtr, dz, dprev, ddtb, dalog, ddsk, dnrm = vjp((douts, dnew))
        for h in range(NH):
            dxc_ref[:, h * HD:(h + 1) * HD] = dx[h].astype(BF16)
            dstate[h * HD:(h + 1) * HD, :] = dprev[h]
            ddsk_ref[:, h:h + 1] += ddsk[h]
        for g in range(NG):
            dxc_ref[:, D + g * NS:D + (g + 1) * NS] = dB[g].astype(BF16)
            dxc_ref[:, D + NG * NS + g * NS:D + NG * NS + (g + 1) * NS] = dC[g].astype(BF16)
            dproj_ref[:, g * 512:(g + 1) * 512] = dz[g].astype(BF16)
            _acc_store(first, dnrm_ref, (slice(None), slice(g * 512, (g + 1) * 512)), dnrm[g])
        ddtr_ref[...] = ddtr
        _acc_store(first, ddtb_ref, (slice(None), slice(None)), ddtb)
        _acc_store(first, dalog_ref, (slice(None), slice(None)), dalog)

    vec = pl.BlockSpec((1, LANE), lambda c: (0, 0))
    return pl.pallas_call(
        body, grid=(nc,),
        in_specs=[pl.BlockSpec((CH, CONV_C), lambda c: (rev(c), 0)), pl.BlockSpec((CH, LANE), lambda c: (rev(c), 0)),
                  pl.BlockSpec((CH, D), lambda c: (rev(c), 2)),
                  pl.BlockSpec((None, NH * HD, NS), lambda c: (rev(c), 0, 0))] + _SSD_PAR() +
                 [pl.BlockSpec((CH, D), lambda c: (rev(c), 1)), pl.BlockSpec(memory_space=pl.ANY)],
        out_specs=[pl.BlockSpec((CH, D), lambda c: (rev(c), 2)), pl.BlockSpec((CH, CONV_C), lambda c: (rev(c), 0)),
                   pl.BlockSpec((CH, LANE), lambda c: (rev(c), 0)), vec, vec, vec, pl.BlockSpec((1, D), lambda c: (0, 0))],
        out_shape=[S(dproj.shape, BF16), S((T, CONV_C), BF16), S((T, LANE), F32), S((1, LANE), F32), S((1, LANE), F32),
                   S((1, LANE), F32), S((1, D), F32)],
        scratch_shapes=[pltpu.VMEM((NH * HD, NS), F32)], input_output_aliases={9: 0},
        compiler_params=_params("arbitrary"), name=name)(xc, dtr, proj, prevs, dtb, alog, dsk, nrm, dmix, dproj)


def _rope(x, c, s, sign):
    W = x.shape[1]
    reps = W // LANE
    C, Sg = jnp.tile(c, (1, reps)), jnp.tile(s, (1, reps))
    lane = lax.broadcasted_iota(jnp.int32, x.shape, 1) % AHD
    up, dn = pltpu.roll(x, W - ROT // 2, 1), pltpu.roll(x, ROT // 2, 1)
    sw = jnp.where(lane < ROT // 2, up, jnp.where(lane < ROT, dn, 0.0))
    return x * C + sign * sw * Sg


def _rope_fwd(name, qkv, cos, sin):
    T = qkv.shape[0]
    tt = _tile(T, 256)
    KV = AKV * AHD

    def body(x_ref, c_ref, s_ref, o_ref):
        c, s = c_ref[...], s_ref[...]
        o_ref[:, :D] = _rope(x_ref[:, :D], c, s, 1.0).astype(BF16)
        o_ref[:, D:D + KV] = _rope(x_ref[:, D:D + KV], c, s, 1.0).astype(BF16)
        o_ref[:, D + KV:] = x_ref[:, D + KV:].astype(BF16)

    tab = pl.BlockSpec((tt, LANE), lambda i: (i, 0))
    return pl.pallas_call(
        body, grid=(T // tt,), in_specs=[pl.BlockSpec((tt, ODD_IN), lambda i: (i, 0)), tab, tab],
        out_specs=pl.BlockSpec((tt, ODD_IN), lambda i: (i, 0)), out_shape=S((T, ODD_IN), BF16),
        compiler_params=_params("parallel"), name=name)(qkv, cos, sin)


def _rope_bwd(name, dq, dkv_cur, dkv_prev, cos, sin):
    T = dq.shape[0]
    nb = T // CH
    KV = AKV * AHD

    def body(dq_ref, cur_ref, nxt_ref, c_ref, s_ref, o_ref, db_ref):
        n = pl.program_id(0)
        c, s = c_ref[...], s_ref[...]
        dkv = cur_ref[...] + nxt_ref[...] * (n < nb - 1).astype(F32)
        o_ref[:, :D] = _rope(dq_ref[...].astype(F32), c, s, -1.0).astype(BF16)
        o_ref[:, D:D + KV] = _rope(dkv[:, :KV], c, s, -1.0).astype(BF16)
        o_ref[:, D + KV:] = dkv[:, KV:].astype(BF16)
        _acc_store(n == 0, db_ref, (slice(None), slice(None)), jnp.sum(o_ref[...].astype(F32), 0, keepdims=True))

    tab = pl.BlockSpec((CH, LANE), lambda n: (n, 0))
    return pl.pallas_call(
        body, grid=(nb,),
        in_specs=[pl.BlockSpec((CH, D), lambda n: (n, 0)), pl.BlockSpec((CH, 2 * KV), lambda n: (n, 0)),
                  pl.BlockSpec((CH, 2 * KV), lambda n: (jnp.minimum(n + 1, nb - 1), 0)), tab, tab],
        out_specs=[pl.BlockSpec((CH, ODD_IN), lambda n: (n, 0)), pl.BlockSpec((1, ODD_IN), lambda n: (0, 0))],
        out_shape=[S((T, ODD_IN), BF16), S((1, ODD_IN), F32)],
        compiler_params=_params("arbitrary"), name=name)(dq, dkv_cur, dkv_prev, cos, sin)


def _swa_math(q, kp, kc, vp, vc, snk, mask):
    outs = []
    for k in range(AKV):
        K = jnp.concatenate([kp[k], kc[k]], 0).astype(BF16)
        V = jnp.concatenate([vp[k], vc[k]], 0).astype(BF16)
        for r in range(AREP):
            h = k * AREP + r
            s = lax.dot_general(q[h].astype(BF16), K, _DIMS["nt"], preferred_element_type=F32) * ATT_SCALE
            s = jnp.where(mask, s, -jnp.inf)
            m = lax.stop_gradient(jnp.maximum(jnp.max(s, -1, keepdims=True), snk[h]))
            p = jnp.exp(s - m)
            pr = p / (jnp.sum(p, -1, keepdims=True) + jnp.exp(snk[h] - m))
            outs.append(jnp.dot(pr.astype(BF16), V, preferred_element_type=F32))
    return outs


def _swa_load(q_ref, cur_ref, prv_ref, snk_ref):
    KV = AKV * AHD
    q = [q_ref[:, h * AHD:(h + 1) * AHD].astype(F32) for h in range(AH)]
    kc = [cur_ref[:, k * AHD:(k + 1) * AHD].astype(F32) for k in range(AKV)]
    vc = [cur_ref[:, KV + k * AHD:KV + (k + 1) * AHD].astype(F32) for k in range(AKV)]
    kp = [prv_ref[:, k * AHD:(k + 1) * AHD].astype(F32) for k in range(AKV)]
    vp = [prv_ref[:, KV + k * AHD:KV + (k + 1) * AHD].astype(F32) for k in range(AKV)]
    snk = [snk_ref[:, h:h + 1] for h in range(AH)]
    return q, kp, kc, vp, vc, snk


def _swa_mask(n):
    iq = lax.broadcasted_iota(jnp.int32, (CH, 2 * CH), 0)
    js = lax.broadcasted_iota(jnp.int32, (CH, 2 * CH), 1)
    rel = iq + CH - js
    return (rel >= 0) & (rel < CH) & ((n > 0) | (js >= CH))


def _swa_specs(T):
    KV = AKV * AHD
    return [pl.BlockSpec((CH, D), lambda n: (n, 0)), pl.BlockSpec((CH, 2 * KV), lambda n: (n, D // (2 * KV))),
            pl.BlockSpec((CH, 2 * KV), lambda n: (jnp.maximum(n - 1, 0), D // (2 * KV))),
            pl.BlockSpec((1, LANE), lambda n: (0, 0))]


def _swa_fwd(name, qkvr, snk):
    T = qkvr.shape[0]

    def body(q_ref, cur_ref, prv_ref, snk_ref, o_ref):
        outs = _swa_math(*_swa_load(q_ref, cur_ref, prv_ref, snk_ref), _swa_mask(pl.program_id(0)))
        for h in range(AH):
            o_ref[:, h * AHD:(h + 1) * AHD] = outs[h].astype(BF16)

    return pl.pallas_call(
        body, grid=(T // CH,), in_specs=_swa_specs(T), out_specs=pl.BlockSpec((CH, D), lambda n: (n, 0)),
        out_shape=S((T, D), BF16), compiler_params=_params("parallel"), name=name)(qkvr, qkvr, qkvr, snk)


def _swa_bwd(name, qkvr, snk, do):
    T = qkvr.shape[0]
    KV = AKV * AHD

    def body(q_ref, cur_ref, prv_ref, snk_ref, do_ref, dq_ref, dcur_ref, dprv_ref, dsnk_ref):
        n = pl.program_id(0)

        @pl.when(n == 0)
        def _():
            dsnk_ref[...] = jnp.zeros_like(dsnk_ref)

        prim = _swa_load(q_ref, cur_ref, prv_ref, snk_ref)
        mask = _swa_mask(n)
        _, vjp = jax.vjp(lambda *p: _swa_math(*p, mask), *prim)
        dq, dkp, dkc, dvp, dvc, dsnk = vjp([do_ref[:, h * AHD:(h + 1) * AHD].astype(F32) for h in range(AH)])
        for h in range(AH):
            dq_ref[:, h * AHD:(h + 1) * AHD] = dq[h].astype(BF16)
            dsnk_ref[:, h:h + 1] += dsnk[h]
        for k in range(AKV):
            dcur_ref[:, k * AHD:(k + 1) * AHD] = dkc[k]
            dcur_ref[:, KV + k * AHD:KV + (k + 1) * AHD] = dvc[k]
            dprv_ref[:, k * AHD:(k + 1) * AHD] = dkp[k]
            dprv_ref[:, KV + k * AHD:KV + (k + 1) * AHD] = dvp[k]

    kv = pl.BlockSpec((CH, 2 * KV), lambda n: (n, 0))
    return pl.pallas_call(
        body, grid=(T // CH,), in_specs=_swa_specs(T) + [pl.BlockSpec((CH, D), lambda n: (n, 0))],
        out_specs=[pl.BlockSpec((CH, D), lambda n: (n, 0)), kv, kv, pl.BlockSpec((1, LANE), lambda n: (0, 0))],
        out_shape=[S((T, D), BF16), S((T, 2 * KV), F32), S((T, 2 * KV), F32), S((1, LANE), F32)],
        compiler_params=_params("arbitrary"), name=name)(qkvr, qkvr, qkvr, snk, do)


def _xat_math(q, k, v):
    outs = []
    for h in range(XH):
        s = lax.dot_general(q[h].astype(BF16), k[h].astype(BF16), _DIMS["nt"], preferred_element_type=F32) * X_SCALE
        m = lax.stop_gradient(jnp.max(s, -1, keepdims=True))
        p = jnp.exp(s - m)
        pr = p / jnp.sum(p, -1, keepdims=True)
        outs.append(jnp.dot(pr.astype(BF16), v[h].astype(BF16), preferred_element_type=F32))
    return outs


def _xat_load(q_ref, kv_ref):
    q = [q_ref[:, h * XHD:(h + 1) * XHD].astype(F32) for h in range(XH)]
    k = [kv_ref[:, h * XHD:(h + 1) * XHD].astype(F32) for h in range(XH)]
    v = [kv_ref[:, XW + h * XHD:XW + (h + 1) * XHD].astype(F32) for h in range(XH)]
    return q, k, v


def _xat_fwd(name, q, kv):
    T, M = q.shape[0], kv.shape[0]
    tt = _tile(T, 512)

    def body(q_ref, kv_ref, o_ref):
        outs = _xat_math(*_xat_load(q_ref, kv_ref))
        for h in range(XH):
            o_ref[:, h * XHD:(h + 1) * XHD] = outs[h].astype(BF16)

    return pl.pallas_call(
        body, grid=(T // tt,),
        in_specs=[pl.BlockSpec((tt, XW), lambda i: (i, 0)), pl.BlockSpec((M, 2 * XW), lambda i: (0, 0))],
        out_specs=pl.BlockSpec((tt, XW), lambda i: (i, 0)), out_shape=S((T, XW), BF16),
        compiler_params=_params("parallel"), name=name)(q, kv)


def _xat_bwd(name, q, kv, do):
    T, M = q.shape[0], kv.shape[0]
    tt = _tile(T, 512)

    def body(q_ref, kv_ref, do_ref, dq_ref, dkv_ref):
        first = pl.program_id(0) == 0
        _, vjp = jax.vjp(_xat_math, *_xat_load(q_ref, kv_ref))
        dq, dk, dv = vjp([do_ref[:, h * XHD:(h + 1) * XHD].astype(F32) for h in range(XH)])
        for h in range(XH):
            sl = slice(h * XHD, (h + 1) * XHD)
            dq_ref[:, sl] = dq[h].astype(BF16)
            _acc_store(first, dkv_ref, (slice(None), sl), dk[h])
            _acc_store(first, dkv_ref, (slice(None), slice(XW + h * XHD, XW + (h + 1) * XHD)), dv[h])

    qs = pl.BlockSpec((tt, XW), lambda i: (i, 0))
    kvs = pl.BlockSpec((M, 2 * XW), lambda i: (0, 0))
    return pl.pallas_call(
        body, grid=(T // tt,), in_specs=[qs, kvs, qs], out_specs=[qs, kvs],
        out_shape=[S((T, XW), BF16), S((M, 2 * XW), F32)],
        compiler_params=_params("arbitrary"), name=name)(q, kv, do)


def _loss_head(name, x, gain, target):
    T = x.shape[0]
    tt = _tile(T, 512)

    def body(x_ref, g_ref, t_ref, l_ref, dx_ref, dxb_ref, dg_ref):
        first = pl.program_id(0) == 0
        xv, g = x_ref[...], g_ref[...]
        r = lax.rsqrt(jnp.mean(xv * xv, -1, keepdims=True) + EPS)
        xh = xv * r
        e = xh * g - t_ref[...]
        part = 0.5 * jnp.sum(jnp.mean(e * e, -1, keepdims=True), (0, 1), keepdims=True)
        _acc_store(first, l_ref, (slice(None), slice(None)), jnp.broadcast_to(part, (1, LANE)))
        dy = e * (1.0 / D)
        dxh = dy * g
        dx = r * (dxh - xh * jnp.mean(dxh * xh, -1, keepdims=True))
        dx_ref[...] = dx
        dxb_ref[...] = dx.astype(BF16)
        _acc_store(first, dg_ref, (slice(None), slice(None)), jnp.sum(dy * xh, 0, keepdims=True))

    row = pl.BlockSpec((tt, D), lambda i: (i, 0))
    vec = pl.BlockSpec((1, D), lambda i: (0, 0))
    return pl.pallas_call(
        body, grid=(T // tt,), in_specs=[row, vec, row],
        out_specs=[pl.BlockSpec((1, LANE), lambda i: (0, 0)), row, row, vec],
        out_shape=[S((1, LANE), F32), S((T, D), F32), S((T, D), BF16), S((1, D), F32)],
        compiler_params=_params("arbitrary"), name=name)(x, gain, target)


def _ffn_fwd(tag, x, gain, wgu4, wd4, layer):
    h = _rms_fwd(f"{tag}_norm", x, gain)
    g, u, a = _ffn_up(f"{tag}_up", h, wgu4, layer)
    x_new = _mm(f"{tag}_down", "nn", a, Op(wd4, "r", layer), F32, res=x, scale=0.5, tk_t=1408)
    return x_new, (x, gain, h, g, u, a)


def _ffn_bwd(tag, saved, dx, dxb, wgu4, wd4, layer, nl, dwgu_into, dwd_into):
    x, gain, h, g, u, a = saved
    dgu = _ffn_dact(f"{tag}_dact", dxb, wd4, layer, g, u)
    dwd = _mm(f"{tag}_dwd", "tn", a, dxb, BF16, out=("r", 4, nl, layer), into=dwd_into, scale=0.5, tm_t=1408)
    dh = _mm(f"{tag}_dh", "nt", Op(dgu, "c"), Op(wgu4, "c", layer), BF16, tk_t=1408)
    dwgu = _mm(f"{tag}_dwgu", "tn", h, Op(dgu, "c"), BF16, out=("c", 4, nl, layer), into=dwgu_into, tn_t=1408)
    dx, dxb, dgain = _rms_bwd(f"{tag}_dnorm", x, gain, dh, dx)
    return dx, dxb, dgain, dwgu, dwd


def _xattn_fwd(tag, x, mem, gq, gm, wxq4, wxkv4, wxo4, layer):
    hq = _rms_fwd(f"{tag}_normq", x, gq)
    mn = _rms_fwd(f"{tag}_normm", mem, gm)
    q = _mm(f"{tag}_q", "nn", hq, Op(wxq4, "r", layer), BF16)
    kv = _mm(f"{tag}_kv", "nn", mn, Op(wxkv4, "r", layer), BF16)
    o = _xat_fwd(f"{tag}_att", q, kv)
    x_new = _mm(f"{tag}_o", "nn", o, Op(wxo4, "c", layer), F32, res=x)
    return x_new, (x, mem, gq, gm, hq, mn, q, kv, o)


def _xattn_bwd(tag, saved, dx, dxb, wxq4, wxkv4, wxo4, layer, nl, into):
    x, mem, gq, gm, hq, mn, q, kv, o = saved
    dwxo = _mm(f"{tag}_dwo", "tn", o, dxb, BF16, out=("c", 4, nl, layer), into=into[2])
    do = _mm(f"{tag}_do", "nt", dxb, Op(wxo4, "c", layer), BF16)
    dq, dkv = _xat_bwd(f"{tag}_datt", q, kv, do)
    dwxq = _mm(f"{tag}_dwq", "tn", hq, dq, BF16, out=("r", 4, nl, layer), into=into[0])
    dhq = _mm(f"{tag}_dhq", "nt", dq, Op(wxq4, "r", layer), BF16)
    dwxkv = _mm(f"{tag}_dwkv", "tn", mn, dkv, BF16, out=("r", 4, nl, layer), into=into[1])
    dmn = _mm(f"{tag}_dmn", "nt", dkv, Op(wxkv4, "r", layer), BF16)
    _, _, dgm = _rms_bwd(f"{tag}_dnormm", mem, gm, dmn)
    dx, dxb, dgq = _rms_bwd(f"{tag}_dnormq", x, gq, dhq, dx)
    return dx, dxb, dgq, dgm, dwxq, dwxkv, dwxo


def _even_fwd(tag, x, gain, w_main, w_dt, p, wout4):
    h = _rms_fwd(f"{tag}_norm", x, gain)
    proj = _mm(f"{tag}_in", "nn", h, w_main, BF16)
    dtr = _mm(f"{tag}_indt", "nn", h, w_dt, F32)
    mix = _gmlp_fwd(f"{tag}_gmlp", proj, p["lng"], p["lnb"], p["ws"], p["bs"])
    ypre, xc = _conv_fwd(f"{tag}_conv", proj, p["cw"], p["cb"])
    mix, prevs = _ssd_fwd(f"{tag}_ssd", xc, dtr, proj, p["dtb"], p["alog"], p["dsk"], p["nrm"], mix)
    x_new = _mm(f"{tag}_out", "nn", mix, Op(wout4, "r", 0), F32, res=x)
    return x_new, (x, gain, h, proj, dtr, mix, ypre, xc, prevs)


def _even_bwd(tag, saved, dx, dxb, w_main, w_dt, p, wout4):
    x, gain, h, proj, dtr, mix, ypre, xc, prevs = saved
    T = x.shape[0]
    dwout = _mm(f"{tag}_dwout", "tn", mix, dxb, BF16, out=("r", 4, 1, 0))
    dmix = _mm(f"{tag}_dmix", "nt", dxb, Op(wout4, "r", 0), BF16)
    dproj = jnp.zeros((T, EVEN_MAIN), BF16)
    dproj, dlng, dlnb, dws, dbs = _gmlp_bwd(f"{tag}_dgmlp", proj, p["lng"], p["lnb"], p["ws"], p["bs"], dmix, dproj)
    dproj, dxc, ddtr, ddtb, dalog, ddsk, dnrm = _ssd_bwd(
        f"{tag}_dssd", xc, dtr, proj, prevs, p["dtb"], p["alog"], p["dsk"], p["nrm"], dmix, dproj)
    dproj, dcw, dcb = _conv_bwd(f"{tag}_dconv", proj, ypre, dxc, p["cw"], dproj)
    dw_main = _mm(f"{tag}_dwin", "tn", h, dproj, BF16)
    dw_dt = _mm(f"{tag}_dwdt", "tn", h, ddtr, BF16)
    dh = _mm(f"{tag}_dh1", "nt", ddtr, w_dt, F32)
    dh = _mm(f"{tag}_dh2", "nt", dproj, w_main, BF16, res=dh)
    dx, dxb, dgain = _rms_bwd(f"{tag}_dnorm", x, gain, dh, dx)
    small = dict(lng=dlng, lnb=dlnb, ws=dws, bs=dbs, cw=dcw, cb=dcb, dtb=ddtb, alog=dalog, dsk=ddsk, nrm=dnrm)
    return dx, dxb, dgain, dw_main, dw_dt, dwout, small


def _odd_fwd(tag, x, gain, wqkv4, bqkv, snk, wo4, cos, sin):
    h = _rms_fwd(f"{tag}_norm", x, gain)
    qkv = _mm(f"{tag}_qkv", "nn", h, Op(wqkv4, "c", 0), F32, bias=bqkv, tn_t=640)
    qkvr = _rope_fwd(f"{tag}_rope", qkv, cos, sin)
    o = _swa_fwd(f"{tag}_swa", qkvr, snk)
    x_new = _mm(f"{tag}_o", "nn", o, Op(wo4, "r", 0), F32, res=x)
    return x_new, (x, gain, h, qkvr, o)


def _odd_bwd(tag, saved, dx, dxb, wqkv4, snk, wo4, cos, sin):
    x, gain, h, qkvr, o = saved
    dwo = _mm(f"{tag}_dwo", "tn", o, dxb, BF16, out=("r", 4, 1, 0))
    do = _mm(f"{tag}_do", "nt", dxb, Op(wo4, "r", 0), BF16)
    dq, dcur, dprv, dsnk = _swa_bwd(f"{tag}_dswa", qkvr, snk, do)
    dqkv, dbias = _rope_bwd(f"{tag}_drope", dq, dcur, dprv, cos, sin)
    dwqkv = _mm(f"{tag}_dwqkv", "tn", h, dqkv, BF16, out=("c", 4, 1, 0), tn_t=640)
    dh = _mm(f"{tag}_dh", "nt", dqkv, Op(wqkv4, "c", 0), BF16, tk_t=640)
    dx, dxb, dgain = _rms_bwd(f"{tag}_dnorm", x, gain, dh, dx)
    return dx, dxb, dgain, dwqkv, dbias, dsnk, dwo


def _row(v):
    return v.reshape(1, -1).astype(F32)


def _pad_lanes(v, n=LANE):
    v = v.reshape(1, -1).astype(F32)
    return jnp.pad(v, ((0, 0), (0, n - v.shape[1])))


def _local_step(x, mem, positions, target, W, P):
    inv_freq = ROPE_THETA ** (-jnp.arange(0, ROT, 2, dtype=F32) / ROT)
    ang = positions.astype(F32)[:, None] * inv_freq
    cos8, sin8 = jnp.cos(ang), jnp.sin(ang)
    ones, zeros = jnp.ones((x.shape[0], AHD - ROT), F32), jnp.zeros((x.shape[0], AHD - ROT), F32)
    cos = jnp.tile(jnp.concatenate([cos8, cos8, ones], 1), (1, 2))
    sin = jnp.tile(jnp.concatenate([-sin8, sin8, zeros], 1), (1, 2))

    ev = dict(lng=_row(P["gm_ln_g"]), lnb=_row(P["gm_ln_b"]), ws=P["gm_ws"].reshape(GM_G, CH, CH),
              bs=P["gm_bs"].reshape(GM_G, CH, 1), cw=P["conv_w"].reshape(4, CONV_C), cb=_row(P["conv_b"]),
              dtb=_pad_lanes(P["dt_bias"]), alog=_pad_lanes(P["a_log"]), dsk=_pad_lanes(P["d_skip"]),
              nrm=_row(P["ssd_norm"]))
    snk = _pad_lanes(P["sinks"])
    bqkv = _row(P["b_qkv"])
    w_main, w_dt = W["w_in_main"], W["w_in_dt"]

    saved = []
    for i in range(2):
        x, s1 = _ffn_fwd(f"l{i}_ffn1", x, _row(P["norm_ffn1"][i]), W["w_ffn1_gu"], W["w_ffn1_down"], i)
        if i == 0:
            x, s2 = _even_fwd("l0_mix", x, _row(P["norm_mix"][0]), w_main, w_dt, ev, W["w_out_even"])
        else:
            x, s2 = _odd_fwd("l1_mix", x, _row(P["norm_mix"][1]), W["w_qkv"], bqkv, snk, W["w_o_odd"], cos, sin)
        x, s3 = _xattn_fwd(f"l{i}_xat", x, mem, _row(P["norm_xq"][i]), _row(P["norm_mem"][i]),
                           W["w_xq"], W["w_xkv"], W["w_xo"], i)
        x, s4 = _ffn_fwd(f"l{i}_ffn2", x, _row(P["norm_ffn2"][i]), W["w_ffn2_gu"], W["w_ffn2_down"], i)
        saved.append((s1, s2, s3, s4))

    loss, dx, dxb, d_final = _loss_head("loss_head", x, _row(P["final_norm"]), target)

    G, sm = {}, {}
    into = {k: None for k in ("w_ffn1_gu", "w_ffn1_down", "w_ffn2_gu", "w_ffn2_down", "w_xq", "w_xkv", "w_xo")}
    dn = {k: [None, None] for k in ("norm_ffn1", "norm_mix", "norm_xq", "norm_mem", "norm_ffn2")}
    for i in (1, 0):
        s1, s2, s3, s4 = saved[i]
        dx, dxb, dn["norm_ffn2"][i], into["w_ffn2_gu"], into["w_ffn2_down"] = _ffn_bwd(
            f"l{i}_ffn2", s4, dx, dxb, W["w_ffn2_gu"], W["w_ffn2_down"], i, 2, into["w_ffn2_gu"], into["w_ffn2_down"])
        dx, dxb, dn["norm_xq"][i], dn["norm_mem"][i], into["w_xq"], into["w_xkv"], into["w_xo"] = _xattn_bwd(
            f"l{i}_xat", s3, dx, dxb, W["w_xq"], W["w_xkv"], W["w_xo"], i, 2, (into["w_xq"], into["w_xkv"], into["w_xo"]))
        if i == 0:
            dx, dxb, dn["norm_mix"][0], G["w_in_main"], G["w_in_dt"], G["w_out_even"], sm_even = _even_bwd(
                "l0_mix", s2, dx, dxb, w_main, w_dt, ev, W["w_out_even"])
        else:
            dx, dxb, dn["norm_mix"][1], G["w_qkv"], sm["b_qkv"], sm["sinks"], G["w_o_odd"] = _odd_bwd(
                "l1_mix", s2, dx, dxb, W["w_qkv"], snk, W["w_o_odd"], cos, sin)
        dx, dxb, dn["norm_ffn1"][i], into["w_ffn1_gu"], into["w_ffn1_down"] = _ffn_bwd(
            f"l{i}_ffn1", s1, dx, dxb, W["w_ffn1_gu"], W["w_ffn1_down"], i, 2, into["w_ffn1_gu"], into["w_ffn1_down"])
    G.update(into)
    for k, v in dn.items():
        sm[k] = jnp.concatenate(v, 0)
    sm.update(gm_ln_g=sm_even["lng"], gm_ln_b=sm_even["lnb"], gm_ws=sm_even["ws"], gm_bs=sm_even["bs"],
              conv_w=sm_even["cw"], conv_b=sm_even["cb"], dt_bias=sm_even["dtb"][:, :NH], a_log=sm_even["alog"][:, :NH],
              d_skip=sm_even["dsk"][:, :NH], ssd_norm=sm_even["nrm"], sinks=sm["sinks"][:, :AH], final_norm=d_final)
    return loss[0, 0], dx, G, sm


def _chip_peers():
    x, y, c = lax.axis_index("x"), lax.axis_index("y"), lax.axis_index("c")
    return 2 * x + y, [((1 - x, y, c), 2 * (1 - x) + y), ((x, 1 - y, c), 2 * x + (1 - y)),
                       ((1 - x, 1 - y, c), 2 * (1 - x) + (1 - y))]


def _any_specs(n):
    return [pl.BlockSpec(memory_space=pl.ANY)] * n


def _gather_chips(name, shards):
    n = len(shards)

    def body(*refs):
        ins, outs = refs[:n], refs[n:2 * n]
        ssem, rsem, lsem = refs[2 * n:]
        me, peers = _chip_peers()
        copies = []
        for i in range(n):
            loc = pltpu.make_async_copy(ins[i], outs[i].at[me], lsem.at[i])
            loc.start()
            copies.append(loc)
            for j, (dev, _) in enumerate(peers):
                cp = pltpu.make_async_remote_copy(src_ref=ins[i], dst_ref=outs[i].at[me], send_sem=ssem.at[3 * i + j],
                                                  recv_sem=rsem.at[3 * i + j], device_id=dev, device_id_type=MESH)
                cp.start()
                copies.append(cp)
        for cp in copies:
            cp.wait()

    return pl.pallas_call(
        body, in_specs=_any_specs(n), out_specs=_any_specs(n),
        out_shape=[S((4,) + s.shape, s.dtype) for s in shards],
        scratch_shapes=[pltpu.SemaphoreType.DMA((3 * n,)), pltpu.SemaphoreType.DMA((3 * n,)), pltpu.SemaphoreType.DMA((n,))],
        compiler_params=pltpu.CompilerParams(has_side_effects=True), name=name)(*shards)


def _scatter_chips(name, grads):
    n = len(grads)

    def body(*refs):
        ins, outs = refs[:n], refs[n:2 * n]
        ssem, rsem, lsem = refs[2 * n:]
        me, peers = _chip_peers()
        copies = []
        for i in range(n):
            loc = pltpu.make_async_copy(ins[i].at[me], outs[i].at[me], lsem.at[i])
            loc.start()
            copies.append(loc)
            for j, (dev, chip) in enumerate(peers):
                cp = pltpu.make_async_remote_copy(src_ref=ins[i].at[chip], dst_ref=outs[i].at[me],
                                                  send_sem=ssem.at[3 * i + j], recv_sem=rsem.at[3 * i + j],
                                                  device_id=dev, device_id_type=MESH)
                cp.start()
                copies.append(cp)
        for cp in copies:
            cp.wait()

    return pl.pallas_call(
        body, in_specs=_any_specs(n), out_specs=_any_specs(n), out_shape=[S(g.shape, g.dtype) for g in grads],
        scratch_shapes=[pltpu.SemaphoreType.DMA((3 * n,)), pltpu.SemaphoreType.DMA((3 * n,)), pltpu.SemaphoreType.DMA((n,))],
        compiler_params=pltpu.CompilerParams(has_side_effects=True), name=name)(*grads)


def _swap_sibling(name, arrs):
    n = len(arrs)

    def body(*refs):
        ins, outs = refs[:n], refs[n:2 * n]
        ssem, rsem = refs[2 * n:]
        dev = (lax.axis_index("x"), lax.axis_index("y"), 1 - lax.axis_index("c"))
        copies = [pltpu.make_async_remote_copy(src_ref=ins[i], dst_ref=outs[i], send_sem=ssem.at[i], recv_sem=rsem.at[i],
                                               device_id=dev, device_id_type=MESH) for i in range(n)]
        for cp in copies:
            cp.start()
        for cp in copies:
            cp.wait()

    return pl.pallas_call(
        body, in_specs=_any_specs(n), out_specs=_any_specs(n), out_shape=[S(a.shape, a.dtype) for a in arrs],
        scratch_shapes=[pltpu.SemaphoreType.DMA((n,)), pltpu.SemaphoreType.DMA((n,))],
        compiler_params=pltpu.CompilerParams(has_side_effects=True), name=name)(*arrs)


def _gather_all(name, v):
    def body(v_ref, o_ref, ssem, rsem, lsem):
        x, y, c = lax.axis_index("x"), lax.axis_index("y"), lax.axis_index("c")
        me = 4 * x + 2 * y + c
        loc = pltpu.make_async_copy(v_ref, o_ref.at[me], lsem)
        loc.start()
        copies = []
        for k in range(1, 8):
            fx, fy, fc = (k >> 2) & 1, (k >> 1) & 1, k & 1
            dev = (x ^ fx, y ^ fy, c ^ fc)
            cp = pltpu.make_async_remote_copy(src_ref=v_ref, dst_ref=o_ref.at[me], send_sem=ssem.at[k - 1],
                                              recv_sem=rsem.at[k - 1], device_id=dev, device_id_type=MESH)
            cp.start()
            copies.append(cp)
        for cp in copies:
            cp.wait()
        loc.wait()

    return pl.pallas_call(
        body, in_specs=_any_specs(1), out_specs=pl.BlockSpec(memory_space=pl.ANY), out_shape=S((8,) + v.shape, v.dtype),
        scratch_shapes=[pltpu.SemaphoreType.DMA((7,)), pltpu.SemaphoreType.DMA((7,)), pltpu.SemaphoreType.DMA(())],
        compiler_params=pltpu.CompilerParams(has_side_effects=True), name=name)(v)


def _row_tile(R, row_bytes, budget=4 << 20):
    if R * row_bytes <= budget or R % 16:
        return R
    t = max(16, budget // row_bytes // 16 * 16)
    while R % t:
        t -= 16
    return t


def _sum_slots(name, r, n):
    _, R, C = r.shape
    tr = _row_tile(R, C * (n * r.dtype.itemsize + 4))

    def body(r_ref, o_ref):
        acc = r_ref[0].astype(F32)
        for j in range(1, n):
            acc = acc + r_ref[j].astype(F32)
        o_ref[...] = acc

    return pl.pallas_call(
        body, grid=(R // tr,), in_specs=[pl.BlockSpec((n, tr, C), lambda i: (0, i, 0))],
        out_specs=pl.BlockSpec((tr, C), lambda i: (i, 0)), out_shape=S((R, C), F32),
        compiler_params=_params("parallel"), name=name)(r)


def _adamw(name, w, m, v, g1, g2=None):
    R, C = w.shape
    tr = _row_tile(R, C * 4 * 9)
    two = g2 is not None

    def body(w_ref, m_ref, v_ref, g1_ref, *rest):
        rest = list(rest)
        g = g1_ref[...]
        if two:
            g = g + rest.pop(0)[...]
        g_ref, d_ref, nm_ref, nv_ref = rest
        mn = B1 * m_ref[...] + (1.0 - B1) * g
        vn = B2 * v_ref[...] + (1.0 - B2) * jnp.square(g)
        m_hat = mn / (1.0 - B1 ** STEP)
        v_hat = vn / (1.0 - B2 ** STEP)
        g_ref[...] = g
        d_ref[...] = -LR * (m_hat / (jnp.sqrt(v_hat) + AEPS) + WD * w_ref[...])
        nm_ref[...] = mn
        nv_ref[...] = vn

    blk = pl.BlockSpec((tr, C), lambda i: (i, 0))
    args = [w, m, v, g1] + ([g2] if two else [])
    return pl.pallas_call(
        body, grid=(R // tr,), in_specs=[blk] * len(args), out_specs=[blk] * 4, out_shape=[S((R, C), F32)] * 4,
        compiler_params=_params("parallel"), name=name)(*args)


_BIG = ["w_ffn1_gu", "w_ffn1_down", "w_in_even", "w_out_even", "w_qkv", "w_o_odd", "w_xq", "w_xkv", "w_xo",
        "w_ffn2_gu", "w_ffn2_down"]
_SMALL = ["norm_ffn1", "norm_mix", "gm_ln_g", "gm_ln_b", "gm_ws", "gm_bs", "conv_w", "conv_b", "dt_bias", "a_log",
          "d_skip", "ssd_norm", "b_qkv", "sinks", "norm_xq", "norm_mem", "norm_ffn2", "final_norm"]
_WEIGHTS = ["norm_ffn1", "w_ffn1_gu", "w_ffn1_down", "norm_mix", "w_in_even", "gm_ln_g", "gm_ln_b", "gm_ws", "gm_bs",
            "conv_w", "conv_b", "dt_bias", "a_log", "d_skip", "ssd_norm", "w_out_even", "w_qkv", "b_qkv", "sinks",
            "w_o_odd", "norm_xq", "norm_mem", "w_xq", "w_xkv", "w_xo", "norm_ffn2", "w_ffn2_gu", "w_ffn2_down",
            "final_norm"]


def _pack(arrs):
    rows = []
    for a in arrs:
        f = a.reshape(-1).astype(F32)
        pad = (-f.shape[0]) % LANE
        rows.append(jnp.pad(f, (0, pad)).reshape(-1, LANE))
    out = jnp.concatenate(rows, 0)
    pad = (-out.shape[0]) % 8
    return jnp.pad(out, ((0, pad), (0, 0)))


def _unpack(packed, shapes):
    outs, r = [], 0
    for shp in shapes:
        n = math.prod(shp)
        nr = -(-n // LANE)
        outs.append(packed[r:r + nr].reshape(-1)[:n].reshape(shp))
        r += nr
    return outs


def kernel(x, mem, positions, norm_ffn1, w_ffn1_gu, w_ffn1_down, norm_mix, w_in_even, gm_ln_g, gm_ln_b, gm_ws, gm_bs, conv_w, conv_b, dt_bias, a_log, d_skip, ssd_norm, w_out_even, w_qkv, b_qkv, sinks, w_o_odd, norm_xq, norm_mem, w_xq, w_xkv, w_xo, norm_ffn2, w_ffn2_gu, w_ffn2_down, final_norm, loss_target, m_norm_ffn1, m_w_ffn1_gu, m_w_ffn1_down, m_norm_mix, m_w_in_even, m_gm_ln_g, m_gm_ln_b, m_gm_ws, m_gm_bs, m_conv_w, m_conv_b, m_dt_bias, m_a_log, m_d_skip, m_ssd_norm, m_w_out_even, m_w_qkv, m_b_qkv, m_sinks, m_w_o_odd, m_norm_xq, m_norm_mem, m_w_xq, m_w_xkv, m_w_xo, m_norm_ffn2, m_w_ffn2_gu, m_w_ffn2_down, m_final_norm, v_norm_ffn1, v_w_ffn1_gu, v_w_ffn1_down, v_norm_mix, v_w_in_even, v_gm_ln_g, v_gm_ln_b, v_gm_ws, v_gm_bs, v_conv_w, v_conv_b, v_dt_bias, v_a_log, v_d_skip, v_ssd_norm, v_w_out_even, v_w_qkv, v_b_qkv, v_sinks, v_w_o_odd, v_norm_xq, v_norm_mem, v_w_xq, v_w_xkv, v_w_xo, v_norm_ffn2, v_w_ffn2_gu, v_w_ffn2_down, v_final_norm):
    a = dict(locals())
    w = {k: a[k] for k in _WEIGHTS}
    mom = {k: a["m_" + k] for k in _WEIGHTS}
    var = {k: a["v_" + k] for k in _WEIGHTS}
    chip = 2 * lax.axis_index("x") + lax.axis_index("y")

    gathered = _gather_chips("gather_weights", [w[k].astype(BF16) for k in _BIG] + [w["conv_w"], w["b_qkv"]])
    W = dict(zip(_BIG, gathered[:len(_BIG)]))
    w_in = jnp.transpose(W.pop("w_in_even")[:, 0], (1, 0, 2)).reshape(D, EVEN_IN)
    W["w_in_main"] = w_in[:, :EVEN_MAIN]
    W["w_in_dt"] = jnp.pad(w_in[:, EVEN_MAIN:], ((0, 0), (0, LANE - (EVEN_IN - EVEN_MAIN))))
    P = {k: w[k] for k in _SMALL}
    P["conv_w"] = jnp.transpose(gathered[-2][:, 0], (1, 0, 2)).reshape(4, CONV_C)
    P["b_qkv"] = gathered[-1].reshape(ODD_IN)

    loss, grad_x, G, sm = _local_step(x[0], mem[0], positions[0], loss_target[0], W, P)
    loss = lax.psum(loss, ("x", "y", "c"))

    dw_in = jnp.concatenate([G.pop("w_in_main"), G.pop("w_in_dt")[:, :EVEN_IN - EVEN_MAIN]], 1)
    G["w_in_even"] = jnp.transpose(dw_in.reshape(D, 4, EVEN_IN // 4), (1, 0, 2)).reshape(4, 1, D, EVEN_IN // 4)
    pieces = _scatter_chips("scatter_grads", [G[k].reshape((4,) + w[k].shape) for k in _BIG])
    part = []
    for k, r in zip(_BIG, pieces):
        C = w[k].shape[-1]
        part.append(_sum_slots(f"sum_{k}", r.reshape(4, -1, C), 4))
    other = _swap_sibling("swap_partials", part)
    out = {}
    for k, p1, p2 in zip(_BIG, part, other):
        C = w[k].shape[-1]
        res = _adamw(f"adamw_{k}", w[k].reshape(-1, C), mom[k].reshape(-1, C), var[k].reshape(-1, C), p1, p2)
        out[k] = [r.reshape(w[k].shape) for r in res]

    full_shapes = {k: w[k].shape for k in _SMALL}
    full_shapes["conv_w"], full_shapes["b_qkv"] = (1, 4, CONV_C), (1, ODD_IN)
    packed = _pack([sm[k] for k in _SMALL])
    total = _sum_slots("sum_small", _gather_all("gather_small", packed), 8)
    gs = dict(zip(_SMALL, _unpack(total, [full_shapes[k] for k in _SMALL])))
    gs["conv_w"] = lax.dynamic_slice_in_dim(gs["conv_w"], chip * (CONV_C // 4), CONV_C // 4, 2)
    gs["b_qkv"] = lax.dynamic_slice_in_dim(gs["b_qkv"], chip * (ODD_IN // 4), ODD_IN // 4, 1)
    res = _adamw("adamw_small", _pack([w[k] for k in _SMALL]), _pack([mom[k] for k in _SMALL]),
                 _pack([var[k] for k in _SMALL]), _pack([gs[k] for k in _SMALL]))
    shapes = [w[k].shape for k in _SMALL]
    for k, g, d, nm, nv in zip(_SMALL, *[_unpack(r, shapes) for r in res]):
        out[k] = [g, d, nm, nv]

    return (loss, grad_x[None], *[out[k][0] for k in _WEIGHTS], *[out[k][1] for k in _WEIGHTS],
            *[out[k][2] for k in _WEIGHTS], *[out[k][3] for k in _WEIGHTS])
```

```python
import functools
import math

import jax
import jax.numpy as jnp
from jax import lax
from jax.experimental import pallas as pl
from jax.experimental.pallas import tpu as pltpu

F32, BF16 = jnp.float32, jnp.bfloat16
S = jax.ShapeDtypeStruct
MESH = pl.DeviceIdType.MESH

D = 2048
DFF = 5632
EPS = 1e-5
CH = 128
GM_G, GM_GD = 4, 512
NH, HD, NG, HPG, NS = 32, 64, 4, 8, 128
CONV_C = 3072
EVEN_MAIN, EVEN_IN = 9216, 9248
AH, AKV, AREP, AHD = 32, 4, 8, 64
ODD_IN = 2560
XH, XHD, XW = 4, 128, 512
ATT_SCALE = AHD ** -0.5
X_SCALE = XHD ** -0.5
ROPE_THETA = 500000.0
ROT = 16
LR, B1, B2, AEPS, WD, STEP = 0.001, 0.9, 0.999, 1e-08, 0.01, 10
LANE = 128
VMEM_LIMIT_V7X = 56 * 1024 * 1024


def _params(*sem):
    return pltpu.CompilerParams(dimension_semantics=sem, vmem_limit_bytes=VMEM_LIMIT_V7X)


def _tile(dim, target):
    if dim <= target:
        return dim
    t = (target // LANE) * LANE
    while t > LANE and dim % t:
        t -= LANE
    assert dim % t == 0, (dim, target)
    return t


class Op:
    def __init__(self, arr, kind=None, layer=0):
        self.arr, self.kind, self.layer = arr, kind, layer
        if kind is None:
            self.R, self.C = arr.shape
        else:
            L = arr.shape[0]
            self.R = arr.shape[2] * (L if kind == "r" else 1)
            self.C = arr.shape[3] * (L if kind == "c" else 1)

    def unit(self, axis):
        if self.kind == "r" and axis == 0:
            return self.arr.shape[2]
        if self.kind == "c" and axis == 1:
            return self.arr.shape[3]
        return (self.R, self.C)[axis]

    def spec(self, tr, tc, pick):
        if self.kind is None:
            return pl.BlockSpec((tr, tc), lambda i, j, k: pick(i, j, k))
        l = self.layer
        if self.kind == "c":
            per = self.arr.shape[3] // tc
            return pl.BlockSpec((None, None, tr, tc),
                                lambda i, j, k: (pick(i, j, k)[1] // per, l, pick(i, j, k)[0], pick(i, j, k)[1] % per))
        per = self.arr.shape[2] // tr
        return pl.BlockSpec((None, None, tr, tc),
                            lambda i, j, k: (pick(i, j, k)[0] // per, l, pick(i, j, k)[0] % per, pick(i, j, k)[1]))


_DIMS = {"nn": (((1,), (0,)), ((), ())), "nt": (((1,), (1,)), ((), ())), "tn": (((0,), (0,)), ((), ()))}
_PICK_A = {"nn": lambda i, j, k: (i, k), "nt": lambda i, j, k: (i, k), "tn": lambda i, j, k: (k, i)}
_PICK_B = {"nn": lambda i, j, k: (k, j), "nt": lambda i, j, k: (j, k), "tn": lambda i, j, k: (k, j)}


def _mm(name, mode, a, b, out_dtype, *, out=None, res=None, bias=None, scale=1.0,
        tm_t=1024, tn_t=1024, tk_t=2048):
    if not isinstance(a, Op):
        a = Op(a)
    if not isinstance(b, Op):
        b = Op(b)
    if mode == "nn":
        M, K, N = a.R, a.C, b.C
        assert b.R == K
        um, uk, un = a.unit(0), math.gcd(a.unit(1), b.unit(0)), b.unit(1)
    elif mode == "nt":
        M, K, N = a.R, a.C, b.R
        assert b.C == K
        um, uk, un = a.unit(0), math.gcd(a.unit(1), b.unit(1)), b.unit(0)
    else:
        K, M, N = a.R, a.C, b.C
        assert b.R == K
        um, uk, un = a.unit(1), math.gcd(a.unit(0), b.unit(0)), b.unit(1)
    if out is not None:
        okind, oL, olayers, olayer = out
        if okind == "c":
            un = math.gcd(un, N // oL)
        else:
            um = math.gcd(um, M // oL)
    tm, tn, tk = _tile(um, tm_t), _tile(un, tn_t), _tile(uk, tk_t)
    gi, gj, gk = M // tm, N // tn, K // tk
    a_blk = (tm, tk) if mode != "tn" else (tk, tm)
    b_blk = {"nn": (tk, tn), "nt": (tn, tk), "tn": (tk, tn)}[mode]
    in_specs = [a.spec(*a_blk, _PICK_A[mode]), b.spec(*b_blk, _PICK_B[mode])]
    args = [a.arr, b.arr]
    if res is not None:
        in_specs.append(pl.BlockSpec((tm, tn), lambda i, j, k: (i, j)))
        args.append(res)
    if bias is not None:
        in_specs.append(pl.BlockSpec((1, tn), lambda i, j, k: (0, j)))
        args.append(bias)
    if out is None:
        out_shape = S((M, N), out_dtype)
        out_spec = pl.BlockSpec((tm, tn), lambda i, j, k: (i, j))
    else:
        shp = (oL, olayers, M, N // oL) if okind == "c" else (oL, olayers, M // oL, N)
        out_shape = S(shp, out_dtype)
        out_spec = Op(out_shape, okind, olayer).spec(tm, tn, lambda i, j, k: (i, j))
    has_res, has_bias = res is not None, bias is not None
    dims = _DIMS[mode]

    def body(a_ref, b_ref, *rest):
        rest = list(rest)
        res_ref = rest.pop(0) if has_res else None
        bias_ref = rest.pop(0) if has_bias else None
        o_ref = rest.pop(0)
        part = lax.dot_general(a_ref[...].astype(BF16), b_ref[...].astype(BF16), dims, preferred_element_type=F32)

        def finish(r):
            if scale != 1.0:
                r = r * scale
            if has_bias:
                r = r + bias_ref[...]
            if has_res:
                r = r + res_ref[...]
            o_ref[...] = r.astype(o_ref.dtype)

        if gk == 1:
            finish(part)
            return
        acc, = rest
        k = pl.program_id(2)

        @pl.when(k == 0)
        def _():
            acc[...] = part

        @pl.when((k > 0) & (k < gk - 1))
        def _():
            acc[...] += part

        @pl.when(k == gk - 1)
        def _():
            finish(acc[...] + part)

    return pl.pallas_call(
        body, grid=(gi, gj, gk), in_specs=in_specs, out_specs=out_spec, out_shape=out_shape,
        scratch_shapes=[pltpu.VMEM((tm, tn), F32)] if gk > 1 else [],
        compiler_params=_params("parallel", "parallel", "arbitrary"), name=name)(*args)


def _rms_fwd(name, x, gain):
    T = x.shape[0]
    tt = _tile(T, 512)

    def body(x_ref, g_ref, o_ref):
        xv = x_ref[...]
        r = lax.rsqrt(jnp.mean(xv * xv, -1, keepdims=True) + EPS)
        o_ref[...] = (xv * r * g_ref[...]).astype(BF16)

    return pl.pallas_call(
        body, grid=(T // tt,),
        in_specs=[pl.BlockSpec((tt, D), lambda i: (i, 0)), pl.BlockSpec((1, D), lambda i: (0, 0))],
        out_specs=pl.BlockSpec((tt, D), lambda i: (i, 0)), out_shape=S((T, D), BF16),
        compiler_params=_params("parallel"), name=name)(x, gain)


def _rms_bwd(name, x, gain, dh, dx_in=None):
    T = x.shape[0]
    tt = _tile(T, 512)
    has_in = dx_in is not None

    def body(x_ref, g_ref, dh_ref, *rest):
        rest = list(rest)
        dxin_ref = rest.pop(0) if has_in else None
        dx_ref, dxb_ref, dg_ref = rest
        xv = x_ref[...]
        r = lax.rsqrt(jnp.mean(xv * xv, -1, keepdims=True) + EPS)
        xh = xv * r
        dy = dh_ref[...].astype(F32)
        dxh = dy * g_ref[...]
        dx = r * (dxh - xh * jnp.mean(dxh * xh, -1, keepdims=True))
        if has_in:
            dx = dx + dxin_ref[...]
        dx_ref[...] = dx
        dxb_ref[...] = dx.astype(BF16)
        part = jnp.sum(dy * xh, 0, keepdims=True)

        @pl.when(pl.program_id(0) == 0)
        def _():
            dg_ref[...] = part

        @pl.when(pl.program_id(0) > 0)
        def _():
            dg_ref[...] += part

    row = pl.BlockSpec((tt, D), lambda i: (i, 0))
    vec = pl.BlockSpec((1, D), lambda i: (0, 0))
    args = [x, gain, dh] + ([dx_in] if has_in else [])
    return pl.pallas_call(
        body, grid=(T // tt,), in_specs=[row, vec, row] + ([row] if has_in else []),
        out_specs=[row, row, vec], out_shape=[S((T, D), F32), S((T, D), BF16), S((1, D), F32)],
        compiler_params=_params("arbitrary"), name=name)(*args)


def _sigmoid(x):
    return 1.0 / (1.0 + jnp.exp(-x))


def _ffn_up(name, h, w4, layer):
    T = h.shape[0]
    n_sh = w4.shape[3]
    tm, tn = _tile(T, 512), _tile(n_sh, 1408)
    per = n_sh // tn

    def body(h_ref, wg_ref, wu_ref, g_ref, u_ref, a_ref):
        hv = h_ref[...]
        g = jnp.dot(hv, wg_ref[...], preferred_element_type=F32)
        u = jnp.dot(hv, wu_ref[...], preferred_element_type=F32)
        g_ref[...] = g.astype(BF16)
        u_ref[...] = u.astype(BF16)
        a_ref[...] = (g * _sigmoid(g) * u).astype(BF16)

    o = pl.BlockSpec((tm, tn), lambda j, i: (i, j))
    return pl.pallas_call(
        body, grid=(DFF // tn, T // tm),
        in_specs=[pl.BlockSpec((tm, D), lambda j, i: (i, 0)),
                  pl.BlockSpec((None, None, D, tn), lambda j, i: (j // per, layer, 0, j % per)),
                  pl.BlockSpec((None, None, D, tn), lambda j, i: (2 + j // per, layer, 0, j % per))],
        out_specs=[o, o, o], out_shape=[S((T, DFF), BF16)] * 3,
        compiler_params=_params("parallel", "parallel"), name=name)(h, w4, w4)


def _ffn_dact(name, dxb, wd4, layer, g, u):
    T = dxb.shape[0]
    r_sh = wd4.shape[2]
    tm, tn = _tile(T, 512), _tile(r_sh, 1408)
    per = r_sh // tn

    def body(dx_ref, w_ref, g_ref, u_ref, o_ref):
        da = 0.5 * lax.dot_general(dx_ref[...], w_ref[...], _DIMS["nt"], preferred_element_type=F32)
        gv, uv = g_ref[...].astype(F32), u_ref[...].astype(F32)
        sg = _sigmoid(gv)
        o_ref[0, 0] = (da * uv * sg * (1.0 + gv * (1.0 - sg))).astype(BF16)
        o_ref[1, 0] = (da * gv * sg).astype(BF16)

    t = pl.BlockSpec((tm, tn), lambda j, i: (i, j))
    return pl.pallas_call(
        body, grid=(DFF // tn, T // tm),
        in_specs=[pl.BlockSpec((tm, D), lambda j, i: (i, 0)),
                  pl.BlockSpec((None, None, tn, D), lambda j, i: (j // per, layer, j % per, 0)), t, t],
        out_specs=pl.BlockSpec((2, 1, tm, tn), lambda j, i: (0, 0, i, j)), out_shape=S((2, 1, T, DFF), BF16),
        compiler_params=_params("parallel", "parallel"), name=name)(dxb, wd4, g, u)


def _gelu(x):
    return 0.5 * x * (1.0 + lax.erf(x * 0.7071067811865476))


def _causal(n):
    return lax.broadcasted_iota(jnp.int32, (n, n), 0) >= lax.broadcasted_iota(jnp.int32, (n, n), 1)


def _gmlp_math(u_raw, v_raw, lng, lnb, ws, bs):
    causal = _causal(CH)
    outs = []
    for g in range(GM_G):
        u, v = _gelu(u_raw[g]), _gelu(v_raw[g])
        mu = jnp.mean(v, -1, keepdims=True)
        var = jnp.mean(jnp.square(v - mu), -1, keepdims=True)
        vn = (v - mu) * lax.rsqrt(var + EPS) * lng[g] + lnb[g]
        wm = jnp.where(causal, ws[g], 0.0)
        s = jnp.dot(wm.astype(BF16), vn.astype(BF16), preferred_element_type=F32) + bs[g]
        outs.append(u * s)
    return outs


def _gmlp_load(proj_ref, lng_ref, lnb_ref, ws_ref, bs_ref):
    sl = lambda g, off: slice(off + g * GM_GD, off + (g + 1) * GM_GD)
    u_raw = [proj_ref[:, sl(g, 0)].astype(F32) for g in range(GM_G)]
    v_raw = [proj_ref[:, sl(g, D)].astype(F32) for g in range(GM_G)]
    lng = [lng_ref[:, sl(g, 0)] for g in range(GM_G)]
    lnb = [lnb_ref[:, sl(g, 0)] for g in range(GM_G)]
    ws = [ws_ref[g] for g in range(GM_G)]
    bs = [bs_ref[g] for g in range(GM_G)]
    return u_raw, v_raw, lng, lnb, ws, bs


_GM_PAR = lambda: [pl.BlockSpec((1, D), lambda i: (0, 0)), pl.BlockSpec((1, D), lambda i: (0, 0)),
                   pl.BlockSpec((GM_G, CH, CH), lambda i: (0, 0, 0)), pl.BlockSpec((GM_G, CH, 1), lambda i: (0, 0, 0))]


def _gmlp_fwd(name, proj, lng, lnb, ws, bs):
    T = proj.shape[0]

    def body(proj_ref, lng_ref, lnb_ref, ws_ref, bs_ref, o_ref):
        outs = _gmlp_math(*_gmlp_load(proj_ref, lng_ref, lnb_ref, ws_ref, bs_ref))
        for g in range(GM_G):
            o_ref[:, g * GM_GD:(g + 1) * GM_GD] = outs[g].astype(BF16)

    return pl.pallas_call(
        body, grid=(T // CH,), in_specs=[pl.BlockSpec((CH, 2 * D), lambda i: (i, 0))] + _GM_PAR(),
        out_specs=pl.BlockSpec((CH, D), lambda i: (i, 0)), out_shape=S((T, 2 * D), BF16),
        compiler_params=_params("parallel"), name=name)(proj, lng, lnb, ws, bs)


def _acc_store(first, ref, idx, val):
    @pl.when(first)
    def _():
        ref[idx] = val

    @pl.when(jnp.logical_not(first))
    def _():
        ref[idx] += val


def _gmlp_bwd(name, proj, lng, lnb, ws, bs, dmix, dproj):
    T = proj.shape[0]

    def body(proj_ref, lng_ref, lnb_ref, ws_ref, bs_ref, dmix_ref, _, dproj_ref, dlng_ref, dlnb_ref, dws_ref, dbs_ref):
        first = pl.program_id(0) == 0
        prim = _gmlp_load(proj_ref, lng_ref, lnb_ref, ws_ref, bs_ref)
        _, vjp = jax.vjp(_gmlp_math, *prim)
        du, dv, dlng, dlnb, dws, dbs = vjp([dmix_ref[:, g * GM_GD:(g + 1) * GM_GD].astype(F32) for g in range(GM_G)])
        for g in range(GM_G):
            sl = slice(g * GM_GD, (g + 1) * GM_GD)
            dproj_ref[:, sl] = du[g].astype(BF16)
            dproj_ref[:, D + g * GM_GD:D + (g + 1) * GM_GD] = dv[g].astype(BF16)
            _acc_store(first, dlng_ref, (slice(None), sl), dlng[g])
            _acc_store(first, dlnb_ref, (slice(None), sl), dlnb[g])
            _acc_store(first, dws_ref, g, dws[g])
            _acc_store(first, dbs_ref, g, dbs[g])

    par = _GM_PAR()
    return pl.pallas_call(
        body, grid=(T // CH,),
        in_specs=[pl.BlockSpec((CH, 2 * D), lambda i: (i, 0))] + par +
                 [pl.BlockSpec((CH, D), lambda i: (i, 0)), pl.BlockSpec(memory_space=pl.ANY)],
        out_specs=[pl.BlockSpec((CH, 2 * D), lambda i: (i, 0))] + par,
        out_shape=[S(dproj.shape, BF16), S((1, D), F32), S((1, D), F32), S((GM_G, CH, CH), F32), S((GM_G, CH, 1), F32)],
        input_output_aliases={6: 0}, compiler_params=_params("arbitrary"), name=name)(proj, lng, lnb, ws, bs, dmix, dproj)


CONV_TT = 256
HALO = 8


def _shift_rows(cur, halo_after, s):
    if s == 0:
        return cur
    n = cur.shape[0]
    return pltpu.roll(jnp.concatenate([cur, halo_after], 0), s, 0)[:n]


def _conv_fwd(name, proj, w, b):
    T = proj.shape[0]
    tt = _tile(T, CONV_TT)
    hb = tt // HALO

    def body(x_ref, halo_ref, w_ref, b_ref, y_ref, xc_ref):
        i = pl.program_id(0)
        x = x_ref[...].astype(F32)
        halo = halo_ref[...].astype(F32) * (i > 0).astype(F32)
        y = b_ref[...] + w_ref[3:4, :] * x
        for s in (1, 2, 3):
            y = y + w_ref[3 - s:4 - s, :] * _shift_rows(x, halo, s)
        y_ref[...] = y.astype(BF16)
        xc_ref[...] = (y * _sigmoid(y)).astype(BF16)

    o = pl.BlockSpec((tt, CONV_C), lambda i: (i, 0))
    return pl.pallas_call(
        body, grid=(T // tt,),
        in_specs=[pl.BlockSpec((tt, CONV_C), lambda i: (i, 2)),
                  pl.BlockSpec((HALO, CONV_C), lambda i: (jnp.maximum(i * hb - 1, 0), 2)),
                  pl.BlockSpec((4, CONV_C), lambda i: (0, 0)), pl.BlockSpec((1, CONV_C), lambda i: (0, 0))],
        out_specs=[o, o], out_shape=[S((T, CONV_C), BF16)] * 2,
        compiler_params=_params("parallel"), name=name)(proj, proj, w, b)


def _conv_bwd(name, proj, ypre, dxc, w, dproj):
    T = proj.shape[0]
    tt = _tile(T, CONV_TT)
    hb = tt // HALO
    nt = T // tt

    def dsilu(y):
        sg = _sigmoid(y)
        return sg * (1.0 + y * (1.0 - sg))

    def body(x_ref, xh_ref, y_ref, yn_ref, d_ref, dn_ref, w_ref, _, dproj_ref, dw_ref, db_ref):
        i = pl.program_id(0)
        first = i == 0
        x = x_ref[...].astype(F32)
        halo = xh_ref[...].astype(F32) * (i > 0).astype(F32)
        dy = d_ref[...].astype(F32) * dsilu(y_ref[...].astype(F32))
        dyn = dn_ref[...].astype(F32) * dsilu(yn_ref[...].astype(F32)) * (i < nt - 1).astype(F32)
        ext = jnp.concatenate([dy, dyn], 0)
        dx = w_ref[3:4, :] * dy
        _acc_store(first, dw_ref, (slice(3, 4), slice(None)), jnp.sum(x * dy, 0, keepdims=True))
        for s in (1, 2, 3):
            dx = dx + w_ref[3 - s:4 - s, :] * pltpu.roll(ext, tt + HALO - s, 0)[:tt]
            _acc_store(first, dw_ref, (slice(3 - s, 4 - s), slice(None)),
                       jnp.sum(_shift_rows(x, halo, s) * dy, 0, keepdims=True))
        _acc_store(first, db_ref, (slice(None), slice(None)), jnp.sum(dy, 0, keepdims=True))
        dproj_ref[...] = dx.astype(BF16)

    cur = pl.BlockSpec((tt, CONV_C), lambda i: (i, 0))
    nxt = pl.BlockSpec((HALO, CONV_C), lambda i: (jnp.minimum((i + 1) * hb, T // HALO - 1), 0))
    return pl.pallas_call(
        body, grid=(nt,),
        in_specs=[pl.BlockSpec((tt, CONV_C), lambda i: (i, 2)),
                  pl.BlockSpec((HALO, CONV_C), lambda i: (jnp.maximum(i * hb - 1, 0), 2)),
                  cur, nxt, cur, nxt, pl.BlockSpec((4, CONV_C), lambda i: (0, 0)), pl.BlockSpec(memory_space=pl.ANY)],
        out_specs=[pl.BlockSpec((tt, CONV_C), lambda i: (i, 2)), pl.BlockSpec((4, CONV_C), lambda i: (0, 0)),
                   pl.BlockSpec((1, CONV_C), lambda i: (0, 0))],
        out_shape=[S(dproj.shape, BF16), S((4, CONV_C), F32), S((1, CONV_C), F32)],
        input_output_aliases={7: 0}, compiler_params=_params("arbitrary"), name=name)(proj, proj, ypre, ypre, dxc, dxc, w, dproj)


def _softplus(x):
    return jnp.maximum(x, 0.0) + jnp.log(1.0 + jnp.exp(-jnp.abs(x)))


def _ssd_math(x, Bm, Cm, dtr, z, prev, dtb, alog, dsk, nrm):
    hi = lax.Precision.HIGHEST
    causal = _causal(CH)
    tri = causal.astype(F32)
    lane = lax.broadcasted_iota(jnp.int32, (1, LANE), 1)
    sub = lax.broadcasted_iota(jnp.int32, (LANE, 1), 0)
    dt = _softplus(dtr + dtb)
    a = dt * (-jnp.exp(alog))
    a_cs = jnp.dot(tri, a, preferred_element_type=F32, precision=hi)
    a_csT = lax.dot_general(a, tri, (((0,), (1,)), ((), ())), preferred_element_type=F32, precision=hi)
    cb = [lax.dot_general(Cm[g].astype(BF16), Bm[g].astype(BF16), _DIMS["nt"], preferred_element_type=F32)
          for g in range(NG)]
    ys, new = [], []
    for h in range(NH):
        g = h // HPG
        ohl = (lane == h).astype(F32)
        col = jnp.sum(a_cs * ohl, 1, keepdims=True)
        row = jnp.sum(a_csT * (sub == h).astype(F32), 0, keepdims=True)
        dtc = jnp.sum(dt * ohl, 1, keepdims=True)
        a_last = jnp.sum(a * ohl, (0, 1), keepdims=True)
        lmat = jnp.where(causal, jnp.exp(jnp.where(causal, col - row, 0.0)), 0.0)
        xdt = x[h] * dtc
        y = jnp.dot((cb[g] * lmat).astype(BF16), xdt.astype(BF16), preferred_element_type=F32)
        y = y + jnp.exp(col) * lax.dot_general(Cm[g].astype(BF16), prev[h].astype(BF16), _DIMS["nt"],
                                               preferred_element_type=F32)
        st = lax.dot_general((xdt * jnp.exp(a_last - col)).astype(BF16), Bm[g].astype(BF16), _DIMS["tn"],
                             preferred_element_type=F32)
        new.append(prev[h] * jnp.exp(a_last) + st)
        ys.append(y + x[h] * dsk[h])
    outs = []
    for g in range(NG):
        yg = jnp.concatenate(ys[g * HPG:(g + 1) * HPG], 1) * (z[g] * _sigmoid(z[g]))
        yg = yg * lax.rsqrt(jnp.mean(yg * yg, -1, keepdims=True) + EPS)
        outs.append(yg * nrm[g])
    return outs, new


def _ssd_load(xc_ref, dtr_ref, z_ref, state_ref, dtb_ref, alog_ref, dsk_ref, nrm_ref):
    x = [xc_ref[:, h * HD:(h + 1) * HD].astype(F32) for h in range(NH)]
    Bm = [xc_ref[:, D + g * NS:D + (g + 1) * NS].astype(F32) for g in range(NG)]
    Cm = [xc_ref[:, D + NG * NS + g * NS:D + NG * NS + (g + 1) * NS].astype(F32) for g in range(NG)]
    z = [z_ref[:, g * 512:(g + 1) * 512].astype(F32) for g in range(NG)]
    prev = [state_ref[h * HD:(h + 1) * HD, :] for h in range(NH)]
    dsk = [dsk_ref[:, h:h + 1] for h in range(NH)]
    nrm = [nrm_ref[:, g * 512:(g + 1) * 512] for g in range(NG)]
    return x, Bm, Cm, dtr_ref[...], z, prev, dtb_ref[...], alog_ref[...], dsk, nrm


_SSD_PAR = lambda: [pl.BlockSpec((1, LANE), lambda c: (0, 0))] * 3 + [pl.BlockSpec((1, D), lambda c: (0, 0))]


def _ssd_fwd(name, xc, dtr, proj, dtb, alog, dsk, nrm, mix):
    T = xc.shape[0]
    nc = T // CH

    def body(xc_ref, dtr_ref, z_ref, dtb_ref, alog_ref, dsk_ref, nrm_ref, _, mix_ref, prev_ref, state):
        @pl.when(pl.program_id(0) == 0)
        def _():
            state[...] = jnp.zeros_like(state)

        prev_ref[...] = state[...]
        outs, new = _ssd_math(*_ssd_load(xc_ref, dtr_ref, z_ref, state, dtb_ref, alog_ref, dsk_ref, nrm_ref))
        for g in range(NG):
            mix_ref[:, g * 512:(g + 1) * 512] = outs[g].astype(BF16)
        for h in range(NH):
            state[h * HD:(h + 1) * HD, :] = new[h]

    return pl.pallas_call(
        body, grid=(nc,),
        in_specs=[pl.BlockSpec((CH, CONV_C), lambda c: (c, 0)), pl.BlockSpec((CH, LANE), lambda c: (c, 0)),
                  pl.BlockSpec((CH, D), lambda c: (c, 2))] + _SSD_PAR() + [pl.BlockSpec(memory_space=pl.ANY)],
        out_specs=[pl.BlockSpec((CH, D), lambda c: (c, 1)), pl.BlockSpec((None, NH * HD, NS), lambda c: (c, 0, 0))],
        out_shape=[S(mix.shape, BF16), S((nc, NH * HD, NS), F32)],
        scratch_shapes=[pltpu.VMEM((NH * HD, NS), F32)], input_output_aliases={7: 0},
        compiler_params=_params("arbitrary"), name=name)(xc, dtr, proj, dtb, alog, dsk, nrm, mix)


def _ssd_bwd(name, xc, dtr, proj, prevs, dtb, alog, dsk, nrm, dmix, dproj):
    T = xc.shape[0]
    nc = T // CH
    rev = lambda c: nc - 1 - c

    def body(xc_ref, dtr_ref, z_ref, prev_ref, dtb_ref, alog_ref, dsk_ref, nrm_ref, dmix_ref, _,
             dproj_ref, dxc_ref, ddtr_ref, ddtb_ref, dalog_ref, ddsk_ref, dnrm_ref, dstate):
        first = pl.program_id(0) == 0

        @pl.when(first)
        def _():
            dstate[...] = jnp.zeros_like(dstate)
            ddsk_ref[...] = jnp.zeros_like(ddsk_ref)

        prim = _ssd_load(xc_ref, dtr_ref, z_ref, prev_ref, dtb_ref, alog_ref, dsk_ref, nrm_ref)
        _, vjp = jax.vjp(_ssd_math, *prim)
        douts = [dmix_ref[:, g * 512:(g + 1) * 512].astype(F32) for g in range(NG)]
        dnew = [dstate[h * HD:(h + 1) * HD, :] for h in range(NH)]
        dx, dB, dC, ddtr, dz, dprev, ddtb, dalog, ddsk, dnrm = vjp((douts, dnew))
        for h in range(NH):
            dxc_ref[:, h * HD:(h + 1) * HD] = dx[h].astype(BF16)
            dstate[h * HD:(h + 1) * HD, :] = dprev[h]
            ddsk_ref[:, h:h + 1] += ddsk[h]
        for g in range(NG):
            dxc_ref[:, D + g * NS:D + (g + 1) * NS] = dB[g].astype(BF16)
            dxc_ref[:, D + NG * NS + g * NS:D + NG * NS + (g + 1) * NS] = dC[g].astype(BF16)
            dproj_ref[:, g * 512:(g + 1) * 512] = dz[g].astype(BF16)
            _acc_store(first, dnrm_ref, (slice(None), slice(g * 512, (g + 1) * 512)), dnrm[g])
        ddtr_ref[...] = ddtr
        _acc_store(first, ddtb_ref, (slice(None), slice(None)), ddtb)
        _acc_store(first, dalog_ref, (slice(None), slice(None)), dalog)

    vec = pl.BlockSpec((1, LANE), lambda c: (0, 0))
    return pl.pallas_call(
        body, grid=(nc,),
        in_specs=[pl.BlockSpec((CH, CONV_C), lambda c: (rev(c), 0)), pl.BlockSpec((CH, LANE), lambda c: (rev(c), 0)),
                  pl.BlockSpec((CH, D), lambda c: (rev(c), 2)),
                  pl.BlockSpec((None, NH * HD, NS), lambda c: (rev(c), 0, 0))] + _SSD_PAR() +
                 [pl.BlockSpec((CH, D), lambda c: (rev(c), 1)), pl.BlockSpec(memory_space=pl.ANY)],
        out_specs=[pl.BlockSpec((CH, D), lambda c: (rev(c), 2)), pl.BlockSpec((CH, CONV_C), lambda c: (rev(c), 0)),
                   pl.BlockSpec((CH, LANE), lambda c: (rev(c), 0)), vec, vec, vec, pl.BlockSpec((1, D), lambda c: (0, 0))],
        out_shape=[S(dproj.shape, BF16), S((T, CONV_C), BF16), S((T, LANE), F32), S((1, LANE), F32), S((1, LANE), F32),
                   S((1, LANE), F32), S((1, D), F32)],
        scratch_shapes=[pltpu.VMEM((NH * HD, NS), F32)], input_output_aliases={9: 0},
        compiler_params=_params("arbitrary"), name=name)(xc, dtr, proj, prevs, dtb, alog, dsk, nrm, dmix, dproj)


def _rope(x, c, s, sign):
    W = x.shape[1]
    reps = W // LANE
    C, Sg = jnp.tile(c, (1, reps)), jnp.tile(s, (1, reps))
    lane = lax.broadcasted_iota(jnp.int32, x.shape, 1) % AHD
    up, dn = pltpu.roll(x, W - ROT // 2, 1), pltpu.roll(x, ROT // 2, 1)
    sw = jnp.where(lane < ROT // 2, up, jnp.where(lane < ROT, dn, 0.0))
    return x * C + sign * sw * Sg


def _rope_fwd(name, qkv, cos, sin):
    T = qkv.shape[0]
    tt = _tile(T, 256)
    KV = AKV * AHD

    def body(x_ref, c_ref, s_ref, o_ref):
        c, s = c_ref[...], s_ref[...]
        o_ref[:, :D] = _rope(x_ref[:, :D], c, s, 1.0).astype(BF16)
        o_ref[:, D:D + KV] = _rope(x_ref[:, D:D + KV], c, s, 1.0).astype(BF16)
        o_ref[:, D + KV:] = x_ref[:, D + KV:].astype(BF16)

    tab = pl.BlockSpec((tt, LANE), lambda i: (i, 0))
    return pl.pallas_call(
        body, grid=(T // tt,), in_specs=[pl.BlockSpec((tt, ODD_IN), lambda i: (i, 0)), tab, tab],
        out_specs=pl.BlockSpec((tt, ODD_IN), lambda i: (i, 0)), out_shape=S((T, ODD_IN), BF16),
        compiler_params=_params("parallel"), name=name)(qkv, cos, sin)


def _rope_bwd(name, dq, dkv_cur, dkv_prev, cos, sin):
    T = dq.shape[0]
    nb = T // CH
    KV = AKV * AHD

    def body(dq_ref, cur_ref, nxt_ref, c_ref, s_ref, o_ref, db_ref):
        n = pl.program_id(0)
        c, s = c_ref[...], s_ref[...]
        dkv = cur_ref[...] + nxt_ref[...] * (n < nb - 1).astype(F32)
        o_ref[:, :D] = _rope(dq_ref[...].astype(F32), c, s, -1.0).astype(BF16)
        o_ref[:, D:D + KV] = _rope(dkv[:, :KV], c, s, -1.0).astype(BF16)
        o_ref[:, D + KV:] = dkv[:, KV:].astype(BF16)
        _acc_store(n == 0, db_ref, (slice(None), slice(None)), jnp.sum(o_ref[...].astype(F32), 0, keepdims=True))

    tab = pl.BlockSpec((CH, LANE), lambda n: (n, 0))
    return pl.pallas_call(
        body, grid=(nb,),
        in_specs=[pl.BlockSpec((CH, D), lambda n: (n, 0)), pl.BlockSpec((CH, 2 * KV), lambda n: (n, 0)),
                  pl.BlockSpec((CH, 2 * KV), lambda n: (jnp.minimum(n + 1, nb - 1), 0)), tab, tab],
        out_specs=[pl.BlockSpec((CH, ODD_IN), lambda n: (n, 0)), pl.BlockSpec((1, ODD_IN), lambda n: (0, 0))],
        out_shape=[S((T, ODD_IN), BF16), S((1, ODD_IN), F32)],
        compiler_params=_params("arbitrary"), name=name)(dq, dkv_cur, dkv_prev, cos, sin)


def _swa_math(q, kp, kc, vp, vc, snk, mask):
    outs = []
    for k in range(AKV):
        K = jnp.concatenate([kp[k], kc[k]], 0).astype(BF16)
        V = jnp.concatenate([vp[k], vc[k]], 0).astype(BF16)
        s = lax.dot_general(q[k].astype(BF16), K, _DIMS["nt"], preferred_element_type=F32) * ATT_SCALE
        s = jnp.where(mask, s, -jnp.inf)
        m = lax.stop_gradient(jnp.maximum(jnp.max(s, -1, keepdims=True), snk[k]))
        p = jnp.exp(s - m)
        pr = p / (jnp.sum(p, -1, keepdims=True) + jnp.exp(snk[k] - m))
        outs.append(jnp.dot(pr.astype(BF16), V, preferred_element_type=F32))
    return outs


def _stack_heads(ref, k):
    return jnp.concatenate([ref[:, (k * AREP + r) * AHD:(k * AREP + r + 1) * AHD].astype(F32) for r in range(AREP)], 0)


def _swa_load(q_ref, cur_ref, prv_ref, snk_ref):
    KV = AKV * AHD
    q = [_stack_heads(q_ref, k) for k in range(AKV)]
    kc = [cur_ref[:, k * AHD:(k + 1) * AHD].astype(F32) for k in range(AKV)]
    vc = [cur_ref[:, KV + k * AHD:KV + (k + 1) * AHD].astype(F32) for k in range(AKV)]
    kp = [prv_ref[:, k * AHD:(k + 1) * AHD].astype(F32) for k in range(AKV)]
    vp = [prv_ref[:, KV + k * AHD:KV + (k + 1) * AHD].astype(F32) for k in range(AKV)]
    snk = [jnp.concatenate([jnp.broadcast_to(snk_ref[:, k * AREP + r:k * AREP + r + 1], (CH, 1)) for r in range(AREP)], 0)
           for k in range(AKV)]
    return q, kp, kc, vp, vc, snk


def _swa_mask(n):
    iq = lax.broadcasted_iota(jnp.int32, (AREP * CH, 2 * CH), 0) % CH
    js = lax.broadcasted_iota(jnp.int32, (AREP * CH, 2 * CH), 1)
    rel = iq + CH - js
    return (rel >= 0) & (rel < CH) & ((n > 0) | (js >= CH))


def _swa_specs(T):
    KV = AKV * AHD
    return [pl.BlockSpec((CH, D), lambda n: (n, 0)), pl.BlockSpec((CH, 2 * KV), lambda n: (n, D // (2 * KV))),
            pl.BlockSpec((CH, 2 * KV), lambda n: (jnp.maximum(n - 1, 0), D // (2 * KV))),
            pl.BlockSpec((1, LANE), lambda n: (0, 0))]


def _swa_fwd(name, qkvr, snk):
    T = qkvr.shape[0]

    def body(q_ref, cur_ref, prv_ref, snk_ref, o_ref):
        outs = _swa_math(*_swa_load(q_ref, cur_ref, prv_ref, snk_ref), _swa_mask(pl.program_id(0)))
        for h in range(AH):
            k, r = divmod(h, AREP)
            o_ref[:, h * AHD:(h + 1) * AHD] = outs[k][r * CH:(r + 1) * CH].astype(BF16)

    return pl.pallas_call(
        body, grid=(T // CH,), in_specs=_swa_specs(T), out_specs=pl.BlockSpec((CH, D), lambda n: (n, 0)),
        out_shape=S((T, D), BF16), compiler_params=_params("parallel"), name=name)(qkvr, qkvr, qkvr, snk)


def _swa_bwd(name, qkvr, snk, do):
    T = qkvr.shape[0]
    KV = AKV * AHD

    def body(q_ref, cur_ref, prv_ref, snk_ref, do_ref, dq_ref, dcur_ref, dprv_ref, dsnk_ref):
        n = pl.program_id(0)

        @pl.when(n == 0)
        def _():
            dsnk_ref[...] = jnp.zeros_like(dsnk_ref)

        prim = _swa_load(q_ref, cur_ref, prv_ref, snk_ref)
        mask = _swa_mask(n)
        _, vjp = jax.vjp(lambda *p: _swa_math(*p, mask), *prim)
        dq, dkp, dkc, dvp, dvc, dsnk = vjp([_stack_heads(do_ref, k) for k in range(AKV)])
        for h in range(AH):
            k, r = divmod(h, AREP)
            dq_ref[:, h * AHD:(h + 1) * AHD] = dq[k][r * CH:(r + 1) * CH].astype(BF16)
            dsnk_ref[:, h:h + 1] += jnp.sum(dsnk[k][r * CH:(r + 1) * CH], 0, keepdims=True)
        for k in range(AKV):
            dcur_ref[:, k * AHD:(k + 1) * AHD] = dkc[k]
            dcur_ref[:, KV + k * AHD:KV + (k + 1) * AHD] = dvc[k]
            dprv_ref[:, k * AHD:(k + 1) * AHD] = dkp[k]
            dprv_ref[:, KV + k * AHD:KV + (k + 1) * AHD] = dvp[k]

    kv = pl.BlockSpec((CH, 2 * KV), lambda n: (n, 0))
    return pl.pallas_call(
        body, grid=(T // CH,), in_specs=_swa_specs(T) + [pl.BlockSpec((CH, D), lambda n: (n, 0))],
        out_specs=[pl.BlockSpec((CH, D), lambda n: (n, 0)), kv, kv, pl.BlockSpec((1, LANE), lambda n: (0, 0))],
        out_shape=[S((T, D), BF16), S((T, 2 * KV), F32), S((T, 2 * KV), F32), S((1, LANE), F32)],
        compiler_params=_params("arbitrary"), name=name)(qkvr, qkvr, qkvr, snk, do)


def _xat_math(q, k, v):
    outs = []
    for h in range(XH):
        s = lax.dot_general(q[h].astype(BF16), k[h].astype(BF16), _DIMS["nt"], preferred_element_type=F32) * X_SCALE
        m = lax.stop_gradient(jnp.max(s, -1, keepdims=True))
        p = jnp.exp(s - m)
        pr = p / jnp.sum(p, -1, keepdims=True)
        outs.append(jnp.dot(pr.astype(BF16), v[h].astype(BF16), preferred_element_type=F32))
    return outs


def _xat_load(q_ref, kv_ref):
    q = [q_ref[:, h * XHD:(h + 1) * XHD].astype(F32) for h in range(XH)]
    k = [kv_ref[:, h * XHD:(h + 1) * XHD].astype(F32) for h in range(XH)]
    v = [kv_ref[:, XW + h * XHD:XW + (h + 1) * XHD].astype(F32) for h in range(XH)]
    return q, k, v


def _xat_fwd(name, q, kv):
    T, M = q.shape[0], kv.shape[0]
    tt = _tile(T, 512)

    def body(q_ref, kv_ref, o_ref):
        outs = _xat_math(*_xat_load(q_ref, kv_ref))
        for h in range(XH):
            o_ref[:, h * XHD:(h + 1) * XHD] = outs[h].astype(BF16)

    return pl.pallas_call(
        body, grid=(T // tt,),
        in_specs=[pl.BlockSpec((tt, XW), lambda i: (i, 0)), pl.BlockSpec((M, 2 * XW), lambda i: (0, 0))],
        out_specs=pl.BlockSpec((tt, XW), lambda i: (i, 0)), out_shape=S((T, XW), BF16),
        compiler_params=_params("parallel"), name=name)(q, kv)


def _xat_bwd(name, q, kv, do):
    T, M = q.shape[0], kv.shape[0]
    tt = _tile(T, 512)

    def body(q_ref, kv_ref, do_ref, dq_ref, dkv_ref):
        first = pl.program_id(0) == 0
        _, vjp = jax.vjp(_xat_math, *_xat_load(q_ref, kv_ref))
        dq, dk, dv = vjp([do_ref[:, h * XHD:(h + 1) * XHD].astype(F32) for h in range(XH)])
        for h in range(XH):
            sl = slice(h * XHD, (h + 1) * XHD)
            dq_ref[:, sl] = dq[h].astype(BF16)
            _acc_store(first, dkv_ref, (slice(None), sl), dk[h])
            _acc_store(first, dkv_ref, (slice(None), slice(XW + h * XHD, XW + (h + 1) * XHD)), dv[h])

    qs = pl.BlockSpec((tt, XW), lambda i: (i, 0))
    kvs = pl.BlockSpec((M, 2 * XW), lambda i: (0, 0))
    return pl.pallas_call(
        body, grid=(T // tt,), in_specs=[qs, kvs, qs], out_specs=[qs, kvs],
        out_shape=[S((T, XW), BF16), S((M, 2 * XW), F32)],
        compiler_params=_params("arbitrary"), name=name)(q, kv, do)


def _loss_head(name, x, gain, target):
    T = x.shape[0]
    tt = _tile(T, 512)

    def body(x_ref, g_ref, t_ref, l_ref, dx_ref, dxb_ref, dg_ref):
        first = pl.program_id(0) == 0
        xv, g = x_ref[...], g_ref[...]
        r = lax.rsqrt(jnp.mean(xv * xv, -1, keepdims=True) + EPS)
        xh = xv * r
        e = xh * g - t_ref[...]
        part = 0.5 * jnp.sum(jnp.mean(e * e, -1, keepdims=True), (0, 1), keepdims=True)
        _acc_store(first, l_ref, (slice(None), slice(None)), jnp.broadcast_to(part, (1, LANE)))
        dy = e * (1.0 / D)
        dxh = dy * g
        dx = r * (dxh - xh * jnp.mean(dxh * xh, -1, keepdims=True))
        dx_ref[...] = dx
        dxb_ref[...] = dx.astype(BF16)
        _acc_store(first, dg_ref, (slice(None), slice(None)), jnp.sum(dy * xh, 0, keepdims=True))

    row = pl.BlockSpec((tt, D), lambda i: (i, 0))
    vec = pl.BlockSpec((1, D), lambda i: (0, 0))
    return pl.pallas_call(
        body, grid=(T // tt,), in_specs=[row, vec, row],
        out_specs=[pl.BlockSpec((1, LANE), lambda i: (0, 0)), row, row, vec],
        out_shape=[S((1, LANE), F32), S((T, D), F32), S((T, D), BF16), S((1, D), F32)],
        compiler_params=_params("arbitrary"), name=name)(x, gain, target)


def _ffn_fwd(tag, x, gain, wgu4, get_wd):
    h = _rms_fwd(f"{tag}_norm", x, gain)
    g, u, a = _ffn_up(f"{tag}_up", h, wgu4, 0)
    x_new = _mm(f"{tag}_down", "nn", a, Op(get_wd(a), "r"), F32, res=x, scale=0.5, tk_t=1408)
    return x_new, (x, gain, h, g, u, a)


def _ffn_bwd(tag, saved, dx, dxb, wgu4, wd4):
    x, gain, h, g, u, a = saved
    dgu = _ffn_dact(f"{tag}_dact", dxb, wd4, 0, g, u)
    dwd = _mm(f"{tag}_dwd", "tn", a, dxb, BF16, out=("r", 4, 1, 0), scale=0.5, tm_t=1408)
    dh = _mm(f"{tag}_dh", "nt", Op(dgu, "c"), Op(wgu4, "c"), BF16, tk_t=2816)
    dwgu = _mm(f"{tag}_dwgu", "tn", h, Op(dgu, "c"), BF16, out=("c", 4, 1, 0), tn_t=1408)
    dx, dxb, dgain = _rms_bwd(f"{tag}_dnorm", x, gain, dh, dx)
    return dx, dxb, dgain, dwgu, dwd


def _xattn_fwd(tag, x, mem, gq, gm, wxq4, wxkv4, wxo4):
    hq = _rms_fwd(f"{tag}_normq", x, gq)
    mn = _rms_fwd(f"{tag}_normm", mem, gm)
    q = _mm(f"{tag}_q", "nn", hq, Op(wxq4, "r"), BF16)
    kv = _mm(f"{tag}_kv", "nn", mn, Op(wxkv4, "r"), BF16)
    o = _xat_fwd(f"{tag}_att", q, kv)
    x_new = _mm(f"{tag}_o", "nn", o, Op(wxo4, "c"), F32, res=x)
    return x_new, (x, mem, gq, gm, hq, mn, q, kv, o)


def _xattn_bwd(tag, saved, dx, dxb, wxq4, wxkv4, wxo4):
    x, mem, gq, gm, hq, mn, q, kv, o = saved
    dwxo = _mm(f"{tag}_dwo", "tn", o, dxb, BF16, out=("c", 4, 1, 0))
    do = _mm(f"{tag}_do", "nt", dxb, Op(wxo4, "c"), BF16)
    dq, dkv = _xat_bwd(f"{tag}_datt", q, kv, do)
    dwxq = _mm(f"{tag}_dwq", "tn", hq, dq, BF16, out=("r", 4, 1, 0))
    dhq = _mm(f"{tag}_dhq", "nt", dq, Op(wxq4, "r"), BF16)
    dwxkv = _mm(f"{tag}_dwkv", "tn", mn, dkv, BF16, out=("r", 4, 1, 0))
    dmn = _mm(f"{tag}_dmn", "nt", dkv, Op(wxkv4, "r"), BF16)
    _, _, dgm = _rms_bwd(f"{tag}_dnormm", mem, gm, dmn)
    dx, dxb, dgq = _rms_bwd(f"{tag}_dnormq", x, gq, dhq, dx)
    return dx, dxb, dgq, dgm, dwxq, dwxkv, dwxo


def _even_fwd(tag, x, gain, w_main, w_dt, p, wout4):
    h = _rms_fwd(f"{tag}_norm", x, gain)
    proj = _mm(f"{tag}_in", "nn", h, w_main, BF16)
    dtr = _mm(f"{tag}_indt", "nn", h, w_dt, F32)
    mix = _gmlp_fwd(f"{tag}_gmlp", proj, p["lng"], p["lnb"], p["ws"], p["bs"])
    ypre, xc = _conv_fwd(f"{tag}_conv", proj, p["cw"], p["cb"])
    mix, prevs = _ssd_fwd(f"{tag}_ssd", xc, dtr, proj, p["dtb"], p["alog"], p["dsk"], p["nrm"], mix)
    x_new = _mm(f"{tag}_out", "nn", mix, Op(wout4, "r", 0), F32, res=x)
    return x_new, (x, gain, h, proj, dtr, mix, ypre, xc, prevs)


def _even_bwd(tag, saved, dx, dxb, w_main, w_dt, p, wout4):
    x, gain, h, proj, dtr, mix, ypre, xc, prevs = saved
    T = x.shape[0]
    dwout = _mm(f"{tag}_dwout", "tn", mix, dxb, BF16, out=("r", 4, 1, 0))
    dmix = _mm(f"{tag}_dmix", "nt", dxb, Op(wout4, "r", 0), BF16)
    dproj = jnp.zeros((T, EVEN_MAIN), BF16)
    dproj, dlng, dlnb, dws, dbs = _gmlp_bwd(f"{tag}_dgmlp", proj, p["lng"], p["lnb"], p["ws"], p["bs"], dmix, dproj)
    dproj, dxc, ddtr, ddtb, dalog, ddsk, dnrm = _ssd_bwd(
        f"{tag}_dssd", xc, dtr, proj, prevs, p["dtb"], p["alog"], p["dsk"], p["nrm"], dmix, dproj)
    dproj, dcw, dcb = _conv_bwd(f"{tag}_dconv", proj, ypre, dxc, p["cw"], dproj)
    dw_main = _mm(f"{tag}_dwin", "tn", h, dproj, BF16)
    dw_dt = _mm(f"{tag}_dwdt", "tn", h, ddtr, BF16)
    dh = _mm(f"{tag}_dh1", "nt", ddtr, w_dt, F32)
    dh = _mm(f"{tag}_dh2", "nt", dproj, w_main, BF16, res=dh)
    dx, dxb, dgain = _rms_bwd(f"{tag}_dnorm", x, gain, dh, dx)
    small = dict(lng=dlng, lnb=dlnb, ws=dws, bs=dbs, cw=dcw, cb=dcb, dtb=ddtb, alog=dalog, dsk=ddsk, nrm=dnrm)
    return dx, dxb, dgain, dw_main, dw_dt, dwout, small


def _odd_fwd(tag, x, gain, wqkv4, bqkv, snk, wo4, cos, sin):
    h = _rms_fwd(f"{tag}_norm", x, gain)
    qkv = _mm(f"{tag}_qkv", "nn", h, Op(wqkv4, "c", 0), F32, bias=bqkv, tn_t=640)
    qkvr = _rope_fwd(f"{tag}_rope", qkv, cos, sin)
    o = _swa_fwd(f"{tag}_swa", qkvr, snk)
    x_new = _mm(f"{tag}_o", "nn", o, Op(wo4, "r", 0), F32, res=x)
    return x_new, (x, gain, h, qkvr, o)


def _odd_bwd(tag, saved, dx, dxb, wqkv4, snk, wo4, cos, sin):
    x, gain, h, qkvr, o = saved
    dwo = _mm(f"{tag}_dwo", "tn", o, dxb, BF16, out=("r", 4, 1, 0))
    do = _mm(f"{tag}_do", "nt", dxb, Op(wo4, "r", 0), BF16)
    dq, dcur, dprv, dsnk = _swa_bwd(f"{tag}_dswa", qkvr, snk, do)
    dqkv, dbias = _rope_bwd(f"{tag}_drope", dq, dcur, dprv, cos, sin)
    dwqkv = _mm(f"{tag}_dwqkv", "tn", h, dqkv, BF16, out=("c", 4, 1, 0), tn_t=640)
    dh = _mm(f"{tag}_dh", "nt", dqkv, Op(wqkv4, "c", 0), BF16, tk_t=640)
    dx, dxb, dgain = _rms_bwd(f"{tag}_dnorm", x, gain, dh, dx)
    return dx, dxb, dgain, dwqkv, dbias, dsnk, dwo


def _row(v):
    return v.reshape(1, -1).astype(F32)


def _pad_lanes(v, n=LANE):
    v = v.reshape(1, -1).astype(F32)
    return jnp.pad(v, ((0, 0), (0, n - v.shape[1])))


def _local_step(x, mem, positions, target, getw, P, putg):
    inv_freq = ROPE_THETA ** (-jnp.arange(0, ROT, 2, dtype=F32) / ROT)
    ang = positions.astype(F32)[:, None] * inv_freq
    cos8, sin8 = jnp.cos(ang), jnp.sin(ang)
    ones, zeros = jnp.ones((x.shape[0], AHD - ROT), F32), jnp.zeros((x.shape[0], AHD - ROT), F32)
    cos = jnp.tile(jnp.concatenate([cos8, cos8, ones], 1), (1, 2))
    sin = jnp.tile(jnp.concatenate([-sin8, sin8, zeros], 1), (1, 2))

    snk = _pad_lanes(P["sinks"])
    W = {}

    def w(name, layer, after):
        if (name, layer) not in W:
            W[name, layer] = getw(name, layer, after)
        return W[name, layer]

    saved = []
    for i in range(2):
        x, s1 = _ffn_fwd(f"l{i}_ffn1", x, _row(P["norm_ffn1"][i]), w("w_ffn1_gu", i, x),
                         functools.partial(w, "w_ffn1_down", i))
        if i == 0:
            ev = dict(lng=_row(P["gm_ln_g"]), lnb=_row(P["gm_ln_b"]), ws=P["gm_ws"].reshape(GM_G, CH, CH),
                      bs=P["gm_bs"].reshape(GM_G, CH, 1), cw=w("conv_w", 0, x), cb=_row(P["conv_b"]),
                      dtb=_pad_lanes(P["dt_bias"]), alog=_pad_lanes(P["a_log"]), dsk=_pad_lanes(P["d_skip"]),
                      nrm=_row(P["ssd_norm"]))
            w_main, w_dt = w("w_in_even", 0, x)
            x, s2 = _even_fwd("l0_mix", x, _row(P["norm_mix"][0]), w_main, w_dt, ev, w("w_out_even", 0, x))
        else:
            x, s2 = _odd_fwd("l1_mix", x, _row(P["norm_mix"][1]), w("w_qkv", 0, x), w("b_qkv", 0, x), snk,
                             w("w_o_odd", 0, x), cos, sin)
        x, s3 = _xattn_fwd(f"l{i}_xat", x, mem, _row(P["norm_xq"][i]), _row(P["norm_mem"][i]),
                           w("w_xq", i, x), w("w_xkv", i, x), w("w_xo", i, x))
        x, s4 = _ffn_fwd(f"l{i}_ffn2", x, _row(P["norm_ffn2"][i]), w("w_ffn2_gu", i, x),
                         functools.partial(w, "w_ffn2_down", i))
        saved.append((s1, s2, s3, s4))

    loss, dx, dxb, d_final = _loss_head("loss_head", x, _row(P["final_norm"]), target)

    sm = {}
    dn = {k: [None, None] for k in ("norm_ffn1", "norm_mix", "norm_xq", "norm_mem", "norm_ffn2")}
    tok = 0.0
    for i in (1, 0):
        s1, s2, s3, s4 = saved[i]
        s4 = (s4[0], s4[1] + tok) + s4[2:]
        dx, dxb, dn["norm_ffn2"][i], dwgu, dwd = _ffn_bwd(f"l{i}_ffn2", s4, dx, dxb, W["w_ffn2_gu", i], W["w_ffn2_down", i])
        tok = putg({("w_ffn2_gu", i): dwgu, ("w_ffn2_down", i): dwd})
        s3 = s3[:2] + (s3[2] + tok,) + s3[3:]
        dx, dxb, dn["norm_xq"][i], dn["norm_mem"][i], dwxq, dwxkv, dwxo = _xattn_bwd(
            f"l{i}_xat", s3, dx, dxb, W["w_xq", i], W["w_xkv", i], W["w_xo", i])
        tok = putg({("w_xq", i): dwxq, ("w_xkv", i): dwxkv, ("w_xo", i): dwxo})
        s2 = (s2[0], s2[1] + tok) + s2[2:]
        if i == 0:
            dx, dxb, dn["norm_mix"][0], dw_main, dw_dt, dwout, sm_even = _even_bwd(
                "l0_mix", s2, dx, dxb, w_main, w_dt, ev, W["w_out_even", 0])
            tok = putg({("w_in_even", 0): (dw_main, dw_dt), ("w_out_even", 0): dwout})
        else:
            dx, dxb, dn["norm_mix"][1], dwqkv, sm["b_qkv"], sm["sinks"], dwo = _odd_bwd(
                "l1_mix", s2, dx, dxb, W["w_qkv", 0], snk, W["w_o_odd", 0], cos, sin)
            tok = putg({("w_qkv", 0): dwqkv, ("w_o_odd", 0): dwo})
        s1 = (s1[0], s1[1] + tok) + s1[2:]
        dx, dxb, dn["norm_ffn1"][i], dwgu, dwd = _ffn_bwd(f"l{i}_ffn1", s1, dx, dxb, W["w_ffn1_gu", i], W["w_ffn1_down", i])
        tok = putg({("w_ffn1_gu", i): dwgu, ("w_ffn1_down", i): dwd})
    for k, v in dn.items():
        sm[k] = jnp.concatenate(v, 0)
    sm.update(gm_ln_g=sm_even["lng"], gm_ln_b=sm_even["lnb"], gm_ws=sm_even["ws"], gm_bs=sm_even["bs"],
              conv_w=sm_even["cw"], conv_b=sm_even["cb"], dt_bias=sm_even["dtb"][:, :NH], a_log=sm_even["alog"][:, :NH],
              d_skip=sm_even["dsk"][:, :NH], ssd_norm=sm_even["nrm"], sinks=sm["sinks"][:, :AH], final_norm=d_final)
    return loss[0, 0], dx, sm


def _chip_peers():
    x, y, c = lax.axis_index("x"), lax.axis_index("y"), lax.axis_index("c")
    return 2 * x + y, [((1 - x, y, c), 2 * (1 - x) + y), ((x, 1 - y, c), 2 * x + (1 - y)),
                       ((1 - x, 1 - y, c), 2 * (1 - x) + (1 - y))]


def _any_specs(n):
    return [pl.BlockSpec(memory_space=pl.ANY)] * n


_HBM = pl.BlockSpec(memory_space=pltpu.HBM)
_SEM = pl.BlockSpec(memory_space=pltpu.SEMAPHORE)
_EFFECT = pltpu.SideEffectType.DATAFLOW_SIDE_EFFECTING


def _chip_copies(gather, srcs, lands, ssems, rsems):
    me, peers = _chip_peers()
    return [pltpu.make_async_remote_copy(src_ref=srcs[i] if gather else srcs[i].at[chip], dst_ref=lands[i].at[me],
                                         send_sem=ssems[i].at[j], recv_sem=rsems[i].at[j], device_id=dev, device_id_type=MESH)
            for i in range(len(srcs)) for j, (dev, chip) in enumerate(peers)]


def _exchange_start(name, gather, arrs):
    n = len(arrs)
    lands = [lax.empty(((4,) + a.shape) if gather else a.shape, a.dtype) for a in arrs]

    def body(*refs):
        srcs, land_refs = refs[:n], refs[n:2 * n]
        ssems, rsems = refs[2 * n:3 * n], refs[3 * n:4 * n]
        token, lsem = refs[6 * n], refs[6 * n + 1]
        for cp in _chip_copies(gather, srcs, land_refs, ssems, rsems):
            cp.start()
        me, _ = _chip_peers()
        own = [pltpu.make_async_copy(srcs[i] if gather else srcs[i].at[me], land_refs[i].at[me], lsem.at[i])
               for i in range(n)]
        for cp in own:
            cp.start()
        for cp in own:
            cp.wait()
        token[...] = jnp.zeros_like(token)

    sems = [pltpu.SemaphoreType.DMA((3,))] * (2 * n)
    res = pl.pallas_call(
        body, name=name,
        out_shape=sems + [pltpu.HBM(a.shape, a.dtype) for a in arrs] + [pltpu.HBM(l.shape, l.dtype) for l in lands]
        + [S((8, LANE), F32)],
        in_specs=[_HBM] * (2 * n), out_specs=[_SEM] * (2 * n) + [_HBM] * (2 * n) + [pl.BlockSpec(memory_space=pltpu.VMEM)],
        input_output_aliases={i: 2 * n + i for i in range(2 * n)},
        scratch_shapes=[pltpu.SemaphoreType.DMA((n,))],
        compiler_params=pltpu.CompilerParams(has_side_effects=_EFFECT),
    )(*[pltpu.with_memory_space_constraint(a, pltpu.HBM) for a in arrs],
      *[pltpu.with_memory_space_constraint(l, pltpu.HBM) for l in lands])
    items = [(res[i], res[n + i], res[2 * n + i], res[3 * n + i]) for i in range(n)]
    return items, res[4 * n][0, 0]


def _exchange_wait(name, gather, item, after):
    ssem, rsem, arr, land = item

    def body(src, land_ref, ssem_ref, rsem_ref, _, src_out, land_out):
        for cp in _chip_copies(gather, [src], [land_ref], [ssem_ref], [rsem_ref]):
            cp.wait_send()
            cp.wait_recv()

    return pl.pallas_call(
        body, name=name, out_shape=[pltpu.HBM(arr.shape, arr.dtype), pltpu.HBM(land.shape, land.dtype)],
        in_specs=[_HBM, _HBM, _SEM, _SEM, pl.BlockSpec(memory_space=pl.ANY)], out_specs=[_HBM, _HBM],
        input_output_aliases={0: 0, 1: 1}, compiler_params=pltpu.CompilerParams(has_side_effects=_EFFECT),
    )(arr, land, ssem, rsem, after)[1]


def _swap_sibling(name, arrs):
    n = len(arrs)

    def body(*refs):
        ins, outs = refs[:n], refs[n:2 * n]
        ssem, rsem = refs[2 * n:]
        dev = (lax.axis_index("x"), lax.axis_index("y"), 1 - lax.axis_index("c"))
        copies = [pltpu.make_async_remote_copy(src_ref=ins[i], dst_ref=outs[i], send_sem=ssem.at[i], recv_sem=rsem.at[i],
                                               device_id=dev, device_id_type=MESH) for i in range(n)]
        for cp in copies:
            cp.start()
        for cp in copies:
            cp.wait()

    return pl.pallas_call(
        body, in_specs=_any_specs(n), out_specs=_any_specs(n), out_shape=[S(a.shape, a.dtype) for a in arrs],
        scratch_shapes=[pltpu.SemaphoreType.DMA((n,)), pltpu.SemaphoreType.DMA((n,))],
        compiler_params=pltpu.CompilerParams(has_side_effects=True), name=name)(*arrs)


def _gather_all(name, v):
    def body(v_ref, o_ref, ssem, rsem, lsem):
        x, y, c = lax.axis_index("x"), lax.axis_index("y"), lax.axis_index("c")
        me = 4 * x + 2 * y + c
        loc = pltpu.make_async_copy(v_ref, o_ref.at[me], lsem)
        loc.start()
        copies = []
        for k in range(1, 8):
            fx, fy, fc = (k >> 2) & 1, (k >> 1) & 1, k & 1
            dev = (x ^ fx, y ^ fy, c ^ fc)
            cp = pltpu.make_async_remote_copy(src_ref=v_ref, dst_ref=o_ref.at[me], send_sem=ssem.at[k - 1],
                                              recv_sem=rsem.at[k - 1], device_id=dev, device_id_type=MESH)
            cp.start()
            copies.append(cp)
        for cp in copies:
            cp.wait()
        loc.wait()

    return pl.pallas_call(
        body, in_specs=_any_specs(1), out_specs=pl.BlockSpec(memory_space=pl.ANY), out_shape=S((8,) + v.shape, v.dtype),
        scratch_shapes=[pltpu.SemaphoreType.DMA((7,)), pltpu.SemaphoreType.DMA((7,)), pltpu.SemaphoreType.DMA(())],
        compiler_params=pltpu.CompilerParams(has_side_effects=True), name=name)(v)


def _row_tile(R, row_bytes, budget=4 << 20):
    if R * row_bytes <= budget or R % 16:
        return R
    t = max(16, budget // row_bytes // 16 * 16)
    while R % t:
        t -= 16
    return t


def _sum_slots(name, r, n):
    _, R, C = r.shape
    tr = _row_tile(R, C * (n * r.dtype.itemsize + 4))

    def body(r_ref, o_ref):
        acc = r_ref[0].astype(F32)
        for j in range(1, n):
            acc = acc + r_ref[j].astype(F32)
        o_ref[...] = acc

    return pl.pallas_call(
        body, grid=(R // tr,), in_specs=[pl.BlockSpec((n, tr, C), lambda i: (0, i, 0))],
        out_specs=pl.BlockSpec((tr, C), lambda i: (i, 0)), out_shape=S((R, C), F32),
        compiler_params=_params("parallel"), name=name)(r)


def _adamw(name, w, m, v, layer, g1, g2=None, into=None):
    nl, R, C = w.shape
    tr = _row_tile(R, C * 4 * 9)
    two, has_into = g2 is not None, into is not None

    def body(w_ref, m_ref, v_ref, g1_ref, *rest):
        rest = list(rest)
        g = g1_ref[...]
        if two:
            g = g + rest.pop(0)[...]
        g_ref, d_ref, nm_ref, nv_ref = rest[-4:]
        mn = B1 * m_ref[...] + (1.0 - B1) * g
        vn = B2 * v_ref[...] + (1.0 - B2) * jnp.square(g)
        m_hat = mn / (1.0 - B1 ** STEP)
        v_hat = vn / (1.0 - B2 ** STEP)
        g_ref[...] = g
        d_ref[...] = -LR * (m_hat / (jnp.sqrt(v_hat) + AEPS) + WD * w_ref[...])
        nm_ref[...] = mn
        nv_ref[...] = vn

    blk = pl.BlockSpec((tr, C), lambda i: (i, 0))
    lay = pl.BlockSpec((None, tr, C), lambda i: (layer, i, 0))
    args = [w, m, v, g1] + ([g2] if two else []) + (list(into) if has_into else [])
    in_specs = [lay] * 3 + [blk] * (2 if two else 1) + (_any_specs(4) if has_into else [])
    aliases = {len(args) - 4 + t: t for t in range(4)} if has_into else {}
    return pl.pallas_call(
        body, grid=(R // tr,), in_specs=in_specs, out_specs=[lay] * 4, out_shape=[S((nl, R, C), F32)] * 4,
        input_output_aliases=aliases, compiler_params=_params("parallel"), name=name)(*args)


_USE_ORDER = [("w_ffn1_gu", 0), ("w_ffn1_down", 0), ("conv_w", 0), ("w_in_even", 0), ("w_out_even", 0), ("w_xq", 0),
              ("w_xkv", 0), ("w_xo", 0), ("w_ffn2_gu", 0), ("w_ffn2_down", 0), ("w_ffn1_gu", 1), ("w_ffn1_down", 1),
              ("w_qkv", 0), ("b_qkv", 0), ("w_o_odd", 0), ("w_xq", 1), ("w_xkv", 1), ("w_xo", 1), ("w_ffn2_gu", 1),
              ("w_ffn2_down", 1)]
_SMALL = ["norm_ffn1", "norm_mix", "gm_ln_g", "gm_ln_b", "gm_ws", "gm_bs", "conv_w", "conv_b", "dt_bias", "a_log",
          "d_skip", "ssd_norm", "b_qkv", "sinks", "norm_xq", "norm_mem", "norm_ffn2", "final_norm"]
_WEIGHTS = ["norm_ffn1", "w_ffn1_gu", "w_ffn1_down", "norm_mix", "w_in_even", "gm_ln_g", "gm_ln_b", "gm_ws", "gm_bs",
            "conv_w", "conv_b", "dt_bias", "a_log", "d_skip", "ssd_norm", "w_out_even", "w_qkv", "b_qkv", "sinks",
            "w_o_odd", "norm_xq", "norm_mem", "w_xq", "w_xkv", "w_xo", "norm_ffn2", "w_ffn2_gu", "w_ffn2_down",
            "final_norm"]


def _pack(arrs):
    rows = []
    for a in arrs:
        f = a.reshape(-1).astype(F32)
        pad = (-f.shape[0]) % LANE
        rows.append(jnp.pad(f, (0, pad)).reshape(-1, LANE))
    out = jnp.concatenate(rows, 0)
    pad = (-out.shape[0]) % 8
    return jnp.pad(out, ((0, pad), (0, 0)))


def _unpack(packed, shapes):
    outs, r = [], 0
    for shp in shapes:
        n = math.prod(shp)
        nr = -(-n // LANE)
        outs.append(packed[r:r + nr].reshape(-1)[:n].reshape(shp))
        r += nr
    return outs


def kernel(x, mem, positions, norm_ffn1, w_ffn1_gu, w_ffn1_down, norm_mix, w_in_even, gm_ln_g, gm_ln_b, gm_ws, gm_bs, conv_w, conv_b, dt_bias, a_log, d_skip, ssd_norm, w_out_even, w_qkv, b_qkv, sinks, w_o_odd, norm_xq, norm_mem, w_xq, w_xkv, w_xo, norm_ffn2, w_ffn2_gu, w_ffn2_down, final_norm, loss_target, m_norm_ffn1, m_w_ffn1_gu, m_w_ffn1_down, m_norm_mix, m_w_in_even, m_gm_ln_g, m_gm_ln_b, m_gm_ws, m_gm_bs, m_conv_w, m_conv_b, m_dt_bias, m_a_log, m_d_skip, m_ssd_norm, m_w_out_even, m_w_qkv, m_b_qkv, m_sinks, m_w_o_odd, m_norm_xq, m_norm_mem, m_w_xq, m_w_xkv, m_w_xo, m_norm_ffn2, m_w_ffn2_gu, m_w_ffn2_down, m_final_norm, v_norm_ffn1, v_w_ffn1_gu, v_w_ffn1_down, v_norm_mix, v_w_in_even, v_gm_ln_g, v_gm_ln_b, v_gm_ws, v_gm_bs, v_conv_w, v_conv_b, v_dt_bias, v_a_log, v_d_skip, v_ssd_norm, v_w_out_even, v_w_qkv, v_b_qkv, v_sinks, v_w_o_odd, v_norm_xq, v_norm_mem, v_w_xq, v_w_xkv, v_w_xo, v_norm_ffn2, v_w_ffn2_gu, v_w_ffn2_down, v_final_norm):
    a = dict(locals())
    w = {k: a[k] for k in _WEIGHTS}
    mom = {k: a["m_" + k] for k in _WEIGHTS}
    var = {k: a["v_" + k] for k in _WEIGHTS}
    chip = 2 * lax.axis_index("x") + lax.axis_index("y")

    shards = [w[k][i:i + 1] if k in ("conv_w", "b_qkv") else w[k][i:i + 1].astype(BF16) for k, i in _USE_ORDER]
    items, _ = _exchange_start("gather_start", True, shards)
    pending = dict(zip(_USE_ORDER, items))

    def getw(name, layer, after):
        got = _exchange_wait(f"gather_wait_{name}_{layer}", True, pending.pop((name, layer)), after)
        if name == "w_in_even":
            w_in = jnp.transpose(got[:, 0], (1, 0, 2)).reshape(D, EVEN_IN)
            return w_in[:, :EVEN_MAIN], jnp.pad(w_in[:, EVEN_MAIN:], ((0, 0), (0, LANE - (EVEN_IN - EVEN_MAIN))))
        if name == "conv_w":
            return jnp.transpose(got[:, 0], (1, 0, 2)).reshape(4, CONV_C)
        if name == "b_qkv":
            return got.reshape(1, ODD_IN)
        return got

    sent = []

    def putg(grads):
        names, arrs = [], []
        for (name, layer), g in grads.items():
            if name == "w_in_even":
                dw_in = jnp.concatenate([g[0], g[1][:, :EVEN_IN - EVEN_MAIN]], 1)
                g = jnp.transpose(dw_in.reshape(D, 4, EVEN_IN // 4), (1, 0, 2)).reshape(4, 1, D, EVEN_IN // 4)
            names.append((name, layer))
            arrs.append(g)
        its, tok = _exchange_start(f"scatter_start_{names[0][0]}_{names[0][1]}", False, arrs)
        sent.append(list(zip(names, its)))
        return tok

    P = {k: w[k] for k in _SMALL}
    loss, grad_x, sm = _local_step(x[0], mem[0], positions[0], loss_target[0], getw, P, putg)
    loss = lax.psum(loss, ("x", "y", "c"))

    out = {}

    def update(groups, tag):
        names = [nm for grp in groups for nm, _ in grp]
        part = []
        for grp in groups:
            for (name, layer), it in grp:
                r = _exchange_wait(f"scatter_wait_{name}_{layer}", False, it, grad_x)
                part.append(_sum_slots(f"sum_{name}_{layer}", r.reshape(4, -1, r.shape[-1]), 4))
        other = _swap_sibling(f"swap_partials_{tag}", part)
        for (name, layer), p1, p2 in zip(names, part, other):
            out[name] = _adamw(f"adamw_{name}_{layer}", w[name], mom[name], var[name], layer, p1, p2, out.get(name))

    update(sent[:-1], "a")
    update(sent[-1:], "b")

    full_shapes = {k: w[k].shape for k in _SMALL}
    full_shapes["conv_w"], full_shapes["b_qkv"] = (1, 4, CONV_C), (1, ODD_IN)
    packed = _pack([sm[k] for k in _SMALL])
    total = _sum_slots("sum_small", _gather_all("gather_small", packed), 8)
    gs = dict(zip(_SMALL, _unpack(total, [full_shapes[k] for k in _SMALL])))
    gs["conv_w"] = lax.dynamic_slice_in_dim(gs["conv_w"], chip * (CONV_C // 4), CONV_C // 4, 2)
    gs["b_qkv"] = lax.dynamic_slice_in_dim(gs["b_qkv"], chip * (ODD_IN // 4), ODD_IN // 4, 1)
    res = _adamw("adamw_small", _pack([w[k] for k in _SMALL])[None], _pack([mom[k] for k in _SMALL])[None],
                 _pack([var[k] for k in _SMALL])[None], 0, _pack([gs[k] for k in _SMALL]))
    shapes = [w[k].shape for k in _SMALL]
    for k, g, d, nm, nv in zip(_SMALL, *[_unpack(r[0], shapes) for r in res]):
        out[k] = [g, d, nm, nv]

    return (loss, grad_x[None], *[out[k][0] for k in _WEIGHTS], *[out[k][1] for k in _WEIGHTS],
            *[out[k][2] for k in _WEIGHTS], *[out[k][3] for k in _WEIGHTS])
```

```python
import functools
import math

import jax
import jax.numpy as jnp
from jax import lax
from jax.experimental import pallas as pl
from jax.experimental.pallas import tpu as pltpu

F32, BF16 = jnp.float32, jnp.bfloat16
S = jax.ShapeDtypeStruct
MESH = pl.DeviceIdType.MESH

D = 2048
DFF = 5632
EPS = 1e-5
CH = 128
GM_G, GM_GD = 4, 512
NH, HD, NG, HPG, NS = 32, 64, 4, 8, 128
CONV_C = 3072
EVEN_MAIN, EVEN_IN = 9216, 9248
AH, AKV, AREP, AHD = 32, 4, 8, 64
ODD_IN = 2560
XH, XHD, XW = 4, 128, 512
ATT_SCALE = AHD ** -0.5
X_SCALE = XHD ** -0.5
ROPE_THETA = 500000.0
ROT = 16
LR, B1, B2, AEPS, WD, STEP = 0.001, 0.9, 0.999, 1e-08, 0.01, 10
LANE = 128
VMEM_LIMIT_V7X = 56 * 1024 * 1024


def _params(*sem):
    return pltpu.CompilerParams(dimension_semantics=sem, vmem_limit_bytes=VMEM_LIMIT_V7X)


def _tile(dim, target):
    if dim <= target:
        return dim
    t = (target // LANE) * LANE
    while t > LANE and dim % t:
        t -= LANE
    assert dim % t == 0, (dim, target)
    return t


class Op:
    def __init__(self, arr, kind=None, layer=0):
        self.arr, self.kind, self.layer = arr, kind, layer
        if kind is None:
            self.R, self.C = arr.shape
        else:
            L = arr.shape[0]
            self.R = arr.shape[2] * (L if kind == "r" else 1)
            self.C = arr.shape[3] * (L if kind == "c" else 1)

    def unit(self, axis):
        if self.kind == "r" and axis == 0:
            return self.arr.shape[2]
        if self.kind == "c" and axis == 1:
            return self.arr.shape[3]
        return (self.R, self.C)[axis]

    def spec(self, tr, tc, pick):
        if self.kind is None:
            return pl.BlockSpec((tr, tc), lambda i, j, k: pick(i, j, k))
        l = self.layer
        if self.kind == "c":
            per = self.arr.shape[3] // tc
            return pl.BlockSpec((None, None, tr, tc),
                                lambda i, j, k: (pick(i, j, k)[1] // per, l, pick(i, j, k)[0], pick(i, j, k)[1] % per))
        per = self.arr.shape[2] // tr
        return pl.BlockSpec((None, None, tr, tc),
                            lambda i, j, k: (pick(i, j, k)[0] // per, l, pick(i, j, k)[0] % per, pick(i, j, k)[1]))


_DIMS = {"nn": (((1,), (0,)), ((), ())), "nt": (((1,), (1,)), ((), ())), "tn": (((0,), (0,)), ((), ()))}
_PICK_A = {"nn": lambda i, j, k: (i, k), "nt": lambda i, j, k: (i, k), "tn": lambda i, j, k: (k, i)}
_PICK_B = {"nn": lambda i, j, k: (k, j), "nt": lambda i, j, k: (j, k), "tn": lambda i, j, k: (k, j)}


def _mm(name, mode, a, b, out_dtype, *, out=None, res=None, bias=None, scale=1.0,
        tm_t=1024, tn_t=1024, tk_t=2048):
    if not isinstance(a, Op):
        a = Op(a)
    if not isinstance(b, Op):
        b = Op(b)
    if mode == "nn":
        M, K, N = a.R, a.C, b.C
        assert b.R == K
        um, uk, un = a.unit(0), math.gcd(a.unit(1), b.unit(0)), b.unit(1)
    elif mode == "nt":
        M, K, N = a.R, a.C, b.R
        assert b.C == K
        um, uk, un = a.unit(0), math.gcd(a.unit(1), b.unit(1)), b.unit(0)
    else:
        K, M, N = a.R, a.C, b.C
        assert b.R == K
        um, uk, un = a.unit(1), math.gcd(a.unit(0), b.unit(0)), b.unit(1)
    if out is not None:
        okind, oL, olayers, olayer = out
        if okind == "c":
            un = math.gcd(un, N // oL)
        else:
            um = math.gcd(um, M // oL)
    tm, tn, tk = _tile(um, tm_t), _tile(un, tn_t), _tile(uk, tk_t)
    gi, gj, gk = M // tm, N // tn, K // tk
    a_blk = (tm, tk) if mode != "tn" else (tk, tm)
    b_blk = {"nn": (tk, tn), "nt": (tn, tk), "tn": (tk, tn)}[mode]
    in_specs = [a.spec(*a_blk, _PICK_A[mode]), b.spec(*b_blk, _PICK_B[mode])]
    args = [a.arr, b.arr]
    if res is not None:
        in_specs.append(pl.BlockSpec((tm, tn), lambda i, j, k: (i, j)))
        args.append(res)
    if bias is not None:
        in_specs.append(pl.BlockSpec((1, tn), lambda i, j, k: (0, j)))
        args.append(bias)
    if out is None:
        out_shape = S((M, N), out_dtype)
        out_spec = pl.BlockSpec((tm, tn), lambda i, j, k: (i, j))
    else:
        shp = (oL, olayers, M, N // oL) if okind == "c" else (oL, olayers, M // oL, N)
        out_shape = S(shp, out_dtype)
        out_spec = Op(out_shape, okind, olayer).spec(tm, tn, lambda i, j, k: (i, j))
    has_res, has_bias = res is not None, bias is not None
    dims = _DIMS[mode]

    def body(a_ref, b_ref, *rest):
        rest = list(rest)
        res_ref = rest.pop(0) if has_res else None
        bias_ref = rest.pop(0) if has_bias else None
        o_ref = rest.pop(0)
        part = lax.dot_general(a_ref[...].astype(BF16), b_ref[...].astype(BF16), dims, preferred_element_type=F32)

        def finish(r):
            if scale != 1.0:
                r = r * scale
            if has_bias:
                r = r + bias_ref[...]
            if has_res:
                r = r + res_ref[...]
            o_ref[...] = r.astype(o_ref.dtype)

        if gk == 1:
            finish(part)
            return
        acc, = rest
        k = pl.program_id(2)

        @pl.when(k == 0)
        def _():
            acc[...] = part

        @pl.when((k > 0) & (k < gk - 1))
        def _():
            acc[...] += part

        @pl.when(k == gk - 1)
        def _():
            finish(acc[...] + part)

    return pl.pallas_call(
        body, grid=(gi, gj, gk), in_specs=in_specs, out_specs=out_spec, out_shape=out_shape,
        scratch_shapes=[pltpu.VMEM((tm, tn), F32)] if gk > 1 else [],
        compiler_params=_params("parallel", "parallel", "arbitrary"), name=name)(*args)


def _rms_fwd(name, x, gain):
    T = x.shape[0]
    tt = _tile(T, 512)

    def body(x_ref, g_ref, o_ref):
        xv = x_ref[...]
        r = lax.rsqrt(jnp.mean(xv * xv, -1, keepdims=True) + EPS)
        o_ref[...] = (xv * r * g_ref[...]).astype(BF16)

    return pl.pallas_call(
        body, grid=(T // tt,),
        in_specs=[pl.BlockSpec((tt, D), lambda i: (i, 0)), pl.BlockSpec((1, D), lambda i: (0, 0))],
        out_specs=pl.BlockSpec((tt, D), lambda i: (i, 0)), out_shape=S((T, D), BF16),
        compiler_params=_params("parallel"), name=name)(x, gain)


def _rms_bwd(name, x, gain, dh, dx_in=None):
    T = x.shape[0]
    tt = _tile(T, 512)
    has_in = dx_in is not None

    def body(x_ref, g_ref, dh_ref, *rest):
        rest = list(rest)
        dxin_ref = rest.pop(0) if has_in else None
        dx_ref, dxb_ref, dg_ref = rest
        xv = x_ref[...]
        r = lax.rsqrt(jnp.mean(xv * xv, -1, keepdims=True) + EPS)
        xh = xv * r
        dy = dh_ref[...].astype(F32)
        dxh = dy * g_ref[...]
        dx = r * (dxh - xh * jnp.mean(dxh * xh, -1, keepdims=True))
        if has_in:
            dx = dx + dxin_ref[...]
        dx_ref[...] = dx
        dxb_ref[...] = dx.astype(BF16)
        part = jnp.sum(dy * xh, 0, keepdims=True)

        @pl.when(pl.program_id(0) == 0)
        def _():
            dg_ref[...] = part

        @pl.when(pl.program_id(0) > 0)
        def _():
            dg_ref[...] += part

    row = pl.BlockSpec((tt, D), lambda i: (i, 0))
    vec = pl.BlockSpec((1, D), lambda i: (0, 0))
    args = [x, gain, dh] + ([dx_in] if has_in else [])
    return pl.pallas_call(
        body, grid=(T // tt,), in_specs=[row, vec, row] + ([row] if has_in else []),
        out_specs=[row, row, vec], out_shape=[S((T, D), F32), S((T, D), BF16), S((1, D), F32)],
        compiler_params=_params("arbitrary"), name=name)(*args)


def _sigmoid(x):
    return 1.0 / (1.0 + jnp.exp(-x))


def _ffn_up(name, h, w4, layer):
    T = h.shape[0]
    n_sh = w4.shape[3]
    tm, tn = _tile(T, 512), _tile(n_sh, 1408)
    per = n_sh // tn

    def body(h_ref, wg_ref, wu_ref, g_ref, u_ref, a_ref):
        hv = h_ref[...]
        g = jnp.dot(hv, wg_ref[...], preferred_element_type=F32)
        u = jnp.dot(hv, wu_ref[...], preferred_element_type=F32)
        g_ref[...] = g.astype(BF16)
        u_ref[...] = u.astype(BF16)
        a_ref[...] = (g * _sigmoid(g) * u).astype(BF16)

    o = pl.BlockSpec((tm, tn), lambda j, i: (i, j))
    return pl.pallas_call(
        body, grid=(DFF // tn, T // tm),
        in_specs=[pl.BlockSpec((tm, D), lambda j, i: (i, 0)),
                  pl.BlockSpec((None, None, D, tn), lambda j, i: (j // per, layer, 0, j % per)),
                  pl.BlockSpec((None, None, D, tn), lambda j, i: (2 + j // per, layer, 0, j % per))],
        out_specs=[o, o, o], out_shape=[S((T, DFF), BF16)] * 3,
        compiler_params=_params("parallel", "parallel"), name=name)(h, w4, w4)


def _ffn_dact(name, dxb, wd4, layer, g, u):
    T = dxb.shape[0]
    r_sh = wd4.shape[2]
    tm, tn = _tile(T, 512), _tile(r_sh, 1408)
    per = r_sh // tn

    def body(dx_ref, w_ref, g_ref, u_ref, o_ref):
        da = 0.5 * lax.dot_general(dx_ref[...], w_ref[...], _DIMS["nt"], preferred_element_type=F32)
        gv, uv = g_ref[...].astype(F32), u_ref[...].astype(F32)
        sg = _sigmoid(gv)
        o_ref[0, 0] = (da * uv * sg * (1.0 + gv * (1.0 - sg))).astype(BF16)
        o_ref[1, 0] = (da * gv * sg).astype(BF16)

    t = pl.BlockSpec((tm, tn), lambda j, i: (i, j))
    return pl.pallas_call(
        body, grid=(DFF // tn, T // tm),
        in_specs=[pl.BlockSpec((tm, D), lambda j, i: (i, 0)),
                  pl.BlockSpec((None, None, tn, D), lambda j, i: (j // per, layer, j % per, 0)), t, t],
        out_specs=pl.BlockSpec((2, 1, tm, tn), lambda j, i: (0, 0, i, j)), out_shape=S((2, 1, T, DFF), BF16),
        compiler_params=_params("parallel", "parallel"), name=name)(dxb, wd4, g, u)


def _gelu(x):
    return 0.5 * x * (1.0 + lax.erf(x * 0.7071067811865476))


def _causal(n):
    return lax.broadcasted_iota(jnp.int32, (n, n), 0) >= lax.broadcasted_iota(jnp.int32, (n, n), 1)


def _gmlp_math(u_raw, v_raw, lng, lnb, ws, bs):
    causal = _causal(CH)
    outs = []
    for g in range(GM_G):
        u, v = _gelu(u_raw[g]), _gelu(v_raw[g])
        mu = jnp.mean(v, -1, keepdims=True)
        var = jnp.mean(jnp.square(v - mu), -1, keepdims=True)
        vn = (v - mu) * lax.rsqrt(var + EPS) * lng[g] + lnb[g]
        wm = jnp.where(causal, ws[g], 0.0)
        s = jnp.dot(wm.astype(BF16), vn.astype(BF16), preferred_element_type=F32) + bs[g]
        outs.append(u * s)
    return outs


def _gmlp_load(proj_ref, lng_ref, lnb_ref, ws_ref, bs_ref):
    sl = lambda g, off: slice(off + g * GM_GD, off + (g + 1) * GM_GD)
    u_raw = [proj_ref[:, sl(g, 0)].astype(F32) for g in range(GM_G)]
    v_raw = [proj_ref[:, sl(g, D)].astype(F32) for g in range(GM_G)]
    lng = [lng_ref[:, sl(g, 0)] for g in range(GM_G)]
    lnb = [lnb_ref[:, sl(g, 0)] for g in range(GM_G)]
    ws = [ws_ref[g] for g in range(GM_G)]
    bs = [bs_ref[g] for g in range(GM_G)]
    return u_raw, v_raw, lng, lnb, ws, bs


_GM_PAR = lambda: [pl.BlockSpec((1, D), lambda i: (0, 0)), pl.BlockSpec((1, D), lambda i: (0, 0)),
                   pl.BlockSpec((GM_G, CH, CH), lambda i: (0, 0, 0)), pl.BlockSpec((GM_G, CH, 1), lambda i: (0, 0, 0))]


def _gmlp_fwd(name, proj, lng, lnb, ws, bs):
    T = proj.shape[0]

    def body(proj_ref, lng_ref, lnb_ref, ws_ref, bs_ref, o_ref):
        outs = _gmlp_math(*_gmlp_load(proj_ref, lng_ref, lnb_ref, ws_ref, bs_ref))
        for g in range(GM_G):
            o_ref[:, g * GM_GD:(g + 1) * GM_GD] = outs[g].astype(BF16)

    return pl.pallas_call(
        body, grid=(T // CH,), in_specs=[pl.BlockSpec((CH, 2 * D), lambda i: (i, 0))] + _GM_PAR(),
        out_specs=pl.BlockSpec((CH, D), lambda i: (i, 0)), out_shape=S((T, 2 * D), BF16),
        compiler_params=_params("parallel"), name=name)(proj, lng, lnb, ws, bs)


def _acc_store(first, ref, idx, val):
    @pl.when(first)
    def _():
        ref[idx] = val

    @pl.when(jnp.logical_not(first))
    def _():
        ref[idx] += val


def _gmlp_bwd(name, proj, lng, lnb, ws, bs, dmix, dproj):
    T = proj.shape[0]

    def body(proj_ref, lng_ref, lnb_ref, ws_ref, bs_ref, dmix_ref, _, dproj_ref, dlng_ref, dlnb_ref, dws_ref, dbs_ref):
        first = pl.program_id(0) == 0
        prim = _gmlp_load(proj_ref, lng_ref, lnb_ref, ws_ref, bs_ref)
        _, vjp = jax.vjp(_gmlp_math, *prim)
        du, dv, dlng, dlnb, dws, dbs = vjp([dmix_ref[:, g * GM_GD:(g + 1) * GM_GD].astype(F32) for g in range(GM_G)])
        for g in range(GM_G):
            sl = slice(g * GM_GD, (g + 1) * GM_GD)
            dproj_ref[:, sl] = du[g].astype(BF16)
            dproj_ref[:, D + g * GM_GD:D + (g + 1) * GM_GD] = dv[g].astype(BF16)
            _acc_store(first, dlng_ref, (slice(None), sl), dlng[g])
            _acc_store(first, dlnb_ref, (slice(None), sl), dlnb[g])
            _acc_store(first, dws_ref, g, dws[g])
            _acc_store(first, dbs_ref, g, dbs[g])

    par = _GM_PAR()
    return pl.pallas_call(
        body, grid=(T // CH,),
        in_specs=[pl.BlockSpec((CH, 2 * D), lambda i: (i, 0))] + par +
                 [pl.BlockSpec((CH, D), lambda i: (i, 0)), pl.BlockSpec(memory_space=pl.ANY)],
        out_specs=[pl.BlockSpec((CH, 2 * D), lambda i: (i, 0))] + par,
        out_shape=[S(dproj.shape, BF16), S((1, D), F32), S((1, D), F32), S((GM_G, CH, CH), F32), S((GM_G, CH, 1), F32)],
        input_output_aliases={6: 0}, compiler_params=_params("arbitrary"), name=name)(proj, lng, lnb, ws, bs, dmix, dproj)


CONV_TT = 256
HALO = 8


def _shift_rows(cur, halo_after, s):
    if s == 0:
        return cur
    n = cur.shape[0]
    return pltpu.roll(jnp.concatenate([cur, halo_after], 0), s, 0)[:n]


def _conv_fwd(name, proj, w, b):
    T = proj.shape[0]
    tt = _tile(T, CONV_TT)
    hb = tt // HALO

    def body(x_ref, halo_ref, w_ref, b_ref, y_ref, xc_ref):
        i = pl.program_id(0)
        x = x_ref[...].astype(F32)
        halo = halo_ref[...].astype(F32) * (i > 0).astype(F32)
        y = b_ref[...] + w_ref[3:4, :] * x
        for s in (1, 2, 3):
            y = y + w_ref[3 - s:4 - s, :] * _shift_rows(x, halo, s)
        y_ref[...] = y.astype(BF16)
        xc_ref[...] = (y * _sigmoid(y)).astype(BF16)

    o = pl.BlockSpec((tt, CONV_C), lambda i: (i, 0))
    return pl.pallas_call(
        body, grid=(T // tt,),
        in_specs=[pl.BlockSpec((tt, CONV_C), lambda i: (i, 2)),
                  pl.BlockSpec((HALO, CONV_C), lambda i: (jnp.maximum(i * hb - 1, 0), 2)),
                  pl.BlockSpec((4, CONV_C), lambda i: (0, 0)), pl.BlockSpec((1, CONV_C), lambda i: (0, 0))],
        out_specs=[o, o], out_shape=[S((T, CONV_C), BF16)] * 2,
        compiler_params=_params("parallel"), name=name)(proj, proj, w, b)


def _conv_bwd(name, proj, ypre, dxc, w, dproj):
    T = proj.shape[0]
    tt = _tile(T, CONV_TT)
    hb = tt // HALO
    nt = T // tt

    def dsilu(y):
        sg = _sigmoid(y)
        return sg * (1.0 + y * (1.0 - sg))

    def body(x_ref, xh_ref, y_ref, yn_ref, d_ref, dn_ref, w_ref, _, dproj_ref, dw_ref, db_ref):
        i = pl.program_id(0)
        first = i == 0
        x = x_ref[...].astype(F32)
        halo = xh_ref[...].astype(F32) * (i > 0).astype(F32)
        dy = d_ref[...].astype(F32) * dsilu(y_ref[...].astype(F32))
        dyn = dn_ref[...].astype(F32) * dsilu(yn_ref[...].astype(F32)) * (i < nt - 1).astype(F32)
        ext = jnp.concatenate([dy, dyn], 0)
        dx = w_ref[3:4, :] * dy
        _acc_store(first, dw_ref, (slice(3, 4), slice(None)), jnp.sum(x * dy, 0, keepdims=True))
        for s in (1, 2, 3):
            dx = dx + w_ref[3 - s:4 - s, :] * pltpu.roll(ext, tt + HALO - s, 0)[:tt]
            _acc_store(first, dw_ref, (slice(3 - s, 4 - s), slice(None)),
                       jnp.sum(_shift_rows(x, halo, s) * dy, 0, keepdims=True))
        _acc_store(first, db_ref, (slice(None), slice(None)), jnp.sum(dy, 0, keepdims=True))
        dproj_ref[...] = dx.astype(BF16)

    cur = pl.BlockSpec((tt, CONV_C), lambda i: (i, 0))
    nxt = pl.BlockSpec((HALO, CONV_C), lambda i: (jnp.minimum((i + 1) * hb, T // HALO - 1), 0))
    return pl.pallas_call(
        body, grid=(nt,),
        in_specs=[pl.BlockSpec((tt, CONV_C), lambda i: (i, 2)),
                  pl.BlockSpec((HALO, CONV_C), lambda i: (jnp.maximum(i * hb - 1, 0), 2)),
                  cur, nxt, cur, nxt, pl.BlockSpec((4, CONV_C), lambda i: (0, 0)), pl.BlockSpec(memory_space=pl.ANY)],
        out_specs=[pl.BlockSpec((tt, CONV_C), lambda i: (i, 2)), pl.BlockSpec((4, CONV_C), lambda i: (0, 0)),
                   pl.BlockSpec((1, CONV_C), lambda i: (0, 0))],
        out_shape=[S(dproj.shape, BF16), S((4, CONV_C), F32), S((1, CONV_C), F32)],
        input_output_aliases={7: 0}, compiler_params=_params("arbitrary"), name=name)(proj, proj, ypre, ypre, dxc, dxc, w, dproj)


def _softplus(x):
    return jnp.maximum(x, 0.0) + jnp.log(1.0 + jnp.exp(-jnp.abs(x)))


def _ssd_math(x, Bm, Cm, dtr, z, prev, dtb, alog, dsk, nrm):
    hi = lax.Precision.HIGHEST
    causal = _causal(CH)
    tri = causal.astype(F32)
    lane = lax.broadcasted_iota(jnp.int32, (1, LANE), 1)
    sub = lax.broadcasted_iota(jnp.int32, (LANE, 1), 0)
    dt = _softplus(dtr + dtb)
    a = dt * (-jnp.exp(alog))
    a_cs = jnp.dot(tri, a, preferred_element_type=F32, precision=hi)
    a_csT = lax.dot_general(a, tri, (((0,), (1,)), ((), ())), preferred_element_type=F32, precision=hi)
    a_last = jnp.sum(a, 0, keepdims=True)
    gw = HPG * HD
    outs, new = [], []
    for g in range(NG):
        spread = (lax.broadcasted_iota(jnp.int32, (LANE, gw), 0)
                  == g * HPG + lax.broadcasted_iota(jnp.int32, (LANE, gw), 1) // HD).astype(F32)
        to_lanes = lambda v: jnp.dot(v, spread, preferred_element_type=F32, precision=hi)
        col_e, dt_e, last_e, dsk_e = to_lanes(a_cs), to_lanes(dt), to_lanes(a_last), to_lanes(dsk)
        last_r = lax.dot_general(spread, a_last, (((0,), (1,)), ((), ())), preferred_element_type=F32, precision=hi)
        cb = lax.dot_general(Cm[g].astype(BF16), Bm[g].astype(BF16), _DIMS["nt"], preferred_element_type=F32)
        xg = jnp.concatenate(x[g * HPG:(g + 1) * HPG], 1)
        yd = []
        for h in range(g * HPG, (g + 1) * HPG):
            ohl = (lane == h).astype(F32)
            col = jnp.sum(a_cs * ohl, 1, keepdims=True)
            row = jnp.sum(a_csT * (sub == h).astype(F32), 0, keepdims=True)
            dtc = jnp.sum(dt * ohl, 1, keepdims=True)
            lmat = jnp.where(causal, jnp.exp(jnp.where(causal, col - row, 0.0)), 0.0)
            yd.append(jnp.dot((cb * lmat).astype(BF16), (x[h] * dtc).astype(BF16), preferred_element_type=F32))
        y = jnp.concatenate(yd, 1)
        y = y + jnp.exp(col_e) * lax.dot_general(Cm[g].astype(BF16), prev[g].astype(BF16), _DIMS["nt"],
                                                 preferred_element_type=F32)
        st = lax.dot_general((xg * dt_e * jnp.exp(last_e - col_e)).astype(BF16), Bm[g].astype(BF16), _DIMS["tn"],
                             preferred_element_type=F32)
        new.append(prev[g] * jnp.exp(last_r) + st)
        yg = (y + xg * dsk_e) * (z[g] * _sigmoid(z[g]))
        yg = yg * lax.rsqrt(jnp.mean(yg * yg, -1, keepdims=True) + EPS)
        outs.append(yg * nrm[g])
    return outs, new


def _ssd_load(xc_ref, dtr_ref, z_ref, state_ref, dtb_ref, alog_ref, dsk_ref, nrm_ref):
    gw = HPG * HD
    x = [xc_ref[:, h * HD:(h + 1) * HD].astype(F32) for h in range(NH)]
    Bm = [xc_ref[:, D + g * NS:D + (g + 1) * NS].astype(F32) for g in range(NG)]
    Cm = [xc_ref[:, D + NG * NS + g * NS:D + NG * NS + (g + 1) * NS].astype(F32) for g in range(NG)]
    z = [z_ref[:, g * gw:(g + 1) * gw].astype(F32) for g in range(NG)]
    prev = [state_ref[g * gw:(g + 1) * gw, :] for g in range(NG)]
    nrm = [nrm_ref[:, g * gw:(g + 1) * gw] for g in range(NG)]
    return x, Bm, Cm, dtr_ref[...], z, prev, dtb_ref[...], alog_ref[...], dsk_ref[...], nrm


_SSD_PAR = lambda: [pl.BlockSpec((1, LANE), lambda c: (0, 0))] * 3 + [pl.BlockSpec((1, D), lambda c: (0, 0))]


def _ssd_fwd(name, xc, dtr, proj, dtb, alog, dsk, nrm, mix):
    T = xc.shape[0]
    nc = T // CH

    def body(xc_ref, dtr_ref, z_ref, dtb_ref, alog_ref, dsk_ref, nrm_ref, _, mix_ref, prev_ref, state):
        @pl.when(pl.program_id(0) == 0)
        def _():
            state[...] = jnp.zeros_like(state)

        prev_ref[...] = state[...]
        outs, new = _ssd_math(*_ssd_load(xc_ref, dtr_ref, z_ref, state, dtb_ref, alog_ref, dsk_ref, nrm_ref))
        for g in range(NG):
            mix_ref[:, g * 512:(g + 1) * 512] = outs[g].astype(BF16)
            state[g * 512:(g + 1) * 512, :] = new[g]

    return pl.pallas_call(
        body, grid=(nc,),
        in_specs=[pl.BlockSpec((CH, CONV_C), lambda c: (c, 0)), pl.BlockSpec((CH, LANE), lambda c: (c, 0)),
                  pl.BlockSpec((CH, D), lambda c: (c, 2))] + _SSD_PAR() + [pl.BlockSpec(memory_space=pl.ANY)],
        out_specs=[pl.BlockSpec((CH, D), lambda c: (c, 1)), pl.BlockSpec((None, NH * HD, NS), lambda c: (c, 0, 0))],
        out_shape=[S(mix.shape, BF16), S((nc, NH * HD, NS), F32)],
        scratch_shapes=[pltpu.VMEM((NH * HD, NS), F32)], input_output_aliases={7: 0},
        compiler_params=_params("arbitrary"), name=name)(xc, dtr, proj, dtb, alog, dsk, nrm, mix)


def _ssd_bwd(name, xc, dtr, proj, prevs, dtb, alog, dsk, nrm, dmix, dproj):
    T = xc.shape[0]
    nc = T // CH
    rev = lambda c: nc - 1 - c

    def body(xc_ref, dtr_ref, z_ref, prev_ref, dtb_ref, alog_ref, dsk_ref, nrm_ref, dmix_ref, _,
             dproj_ref, dxc_ref, ddtr_ref, ddtb_ref, dalog_ref, ddsk_ref, dnrm_ref, dstate):
        first = pl.program_id(0) == 0

        @pl.when(first)
        def _():
            dstate[...] = jnp.zeros_like(dstate)

        prim = _ssd_load(xc_ref, dtr_ref, z_ref, prev_ref, dtb_ref, alog_ref, dsk_ref, nrm_ref)
        _, vjp = jax.vjp(_ssd_math, *prim)
        douts = [dmix_ref[:, g * 512:(g + 1) * 512].astype(F32) for g in range(NG)]
        dnew = [dstate[g * 512:(g + 1) * 512, :] for g in range(NG)]
        dx, dB, dC, ddtr, dz, dprev, ddtb, dalog, ddsk, dnrm = vjp((douts, dnew))
        for h in range(NH):
            dxc_ref[:, h * HD:(h + 1) * HD] = dx[h].astype(BF16)
        for g in range(NG):
            dstate[g * 512:(g + 1) * 512, :] = dprev[g]
            dxc_ref[:, D + g * NS:D + (g + 1) * NS] = dB[g].astype(BF16)
            dxc_ref[:, D + NG * NS + g * NS:D + NG * NS + (g + 1) * NS] = dC[g].astype(BF16)
            dproj_ref[:, g * 512:(g + 1) * 512] = dz[g].astype(BF16)
            _acc_store(first, dnrm_ref, (slice(None), slice(g * 512, (g + 1) * 512)), dnrm[g])
        ddtr_ref[...] = ddtr
        _acc_store(first, ddtb_ref, (slice(None), slice(None)), ddtb)
        _acc_store(first, dalog_ref, (slice(None), slice(None)), dalog)
        _acc_store(first, ddsk_ref, (slice(None), slice(None)), ddsk)

    vec = pl.BlockSpec((1, LANE), lambda c: (0, 0))
    return pl.pallas_call(
        body, grid=(nc,),
        in_specs=[pl.BlockSpec((CH, CONV_C), lambda c: (rev(c), 0)), pl.BlockSpec((CH, LANE), lambda c: (rev(c), 0)),
                  pl.BlockSpec((CH, D), lambda c: (rev(c), 2)),
                  pl.BlockSpec((None, NH * HD, NS), lambda c: (rev(c), 0, 0))] + _SSD_PAR() +
                 [pl.BlockSpec((CH, D), lambda c: (rev(c), 1)), pl.BlockSpec(memory_space=pl.ANY)],
        out_specs=[pl.BlockSpec((CH, D), lambda c: (rev(c), 2)), pl.BlockSpec((CH, CONV_C), lambda c: (rev(c), 0)),
                   pl.BlockSpec((CH, LANE), lambda c: (rev(c), 0)), vec, vec, vec, pl.BlockSpec((1, D), lambda c: (0, 0))],
        out_shape=[S(dproj.shape, BF16), S((T, CONV_C), BF16), S((T, LANE), F32), S((1, LANE), F32), S((1, LANE), F32),
                   S((1, LANE), F32), S((1, D), F32)],
        scratch_shapes=[pltpu.VMEM((NH * HD, NS), F32)], input_output_aliases={9: 0},
        compiler_params=_params("arbitrary"), name=name)(xc, dtr, proj, prevs, dtb, alog, dsk, nrm, dmix, dproj)


def _rope(x, c, s, sign):
    W = x.shape[1]
    reps = W // LANE
    C, Sg = jnp.tile(c, (1, reps)), jnp.tile(s, (1, reps))
    lane = lax.broadcasted_iota(jnp.int32, x.shape, 1) % AHD
    up, dn = pltpu.roll(x, W - ROT // 2, 1), pltpu.roll(x, ROT // 2, 1)
    sw = jnp.where(lane < ROT // 2, up, jnp.where(lane < ROT, dn, 0.0))
    return x * C + sign * sw * Sg


def _rope_fwd(name, qkv, cos, sin):
    T = qkv.shape[0]
    tt = _tile(T, 256)
    KV = AKV * AHD

    def body(x_ref, c_ref, s_ref, o_ref):
        c, s = c_ref[...], s_ref[...]
        o_ref[:, :D] = _rope(x_ref[:, :D], c, s, 1.0).astype(BF16)
        o_ref[:, D:D + KV] = _rope(x_ref[:, D:D + KV], c, s, 1.0).astype(BF16)
        o_ref[:, D + KV:] = x_ref[:, D + KV:].astype(BF16)

    tab = pl.BlockSpec((tt, LANE), lambda i: (i, 0))
    return pl.pallas_call(
        body, grid=(T // tt,), in_specs=[pl.BlockSpec((tt, ODD_IN), lambda i: (i, 0)), tab, tab],
        out_specs=pl.BlockSpec((tt, ODD_IN), lambda i: (i, 0)), out_shape=S((T, ODD_IN), BF16),
        compiler_params=_params("parallel"), name=name)(qkv, cos, sin)


def _rope_bwd(name, dq, dkv_cur, dkv_prev, cos, sin):
    T = dq.shape[0]
    nb = T // CH
    KV = AKV * AHD

    def body(dq_ref, cur_ref, nxt_ref, c_ref, s_ref, o_ref, db_ref):
        n = pl.program_id(0)
        c, s = c_ref[...], s_ref[...]
        dkv = cur_ref[...] + nxt_ref[...] * (n < nb - 1).astype(F32)
        o_ref[:, :D] = _rope(dq_ref[...].astype(F32), c, s, -1.0).astype(BF16)
        o_ref[:, D:D + KV] = _rope(dkv[:, :KV], c, s, -1.0).astype(BF16)
        o_ref[:, D + KV:] = dkv[:, KV:].astype(BF16)
        _acc_store(n == 0, db_ref, (slice(None), slice(None)), jnp.sum(o_ref[...].astype(F32), 0, keepdims=True))

    tab = pl.BlockSpec((CH, LANE), lambda n: (n, 0))
    return pl.pallas_call(
        body, grid=(nb,),
        in_specs=[pl.BlockSpec((CH, D), lambda n: (n, 0)), pl.BlockSpec((CH, 2 * KV), lambda n: (n, 0)),
                  pl.BlockSpec((CH, 2 * KV), lambda n: (jnp.minimum(n + 1, nb - 1), 0)), tab, tab],
        out_specs=[pl.BlockSpec((CH, ODD_IN), lambda n: (n, 0)), pl.BlockSpec((1, ODD_IN), lambda n: (0, 0))],
        out_shape=[S((T, ODD_IN), BF16), S((1, ODD_IN), F32)],
        compiler_params=_params("arbitrary"), name=name)(dq, dkv_cur, dkv_prev, cos, sin)


def _swa_math(q, kp, kc, vp, vc, snk, mask):
    outs = []
    for k in range(AKV):
        K = jnp.concatenate([kp[k], kc[k]], 0).astype(BF16)
        V = jnp.concatenate([vp[k], vc[k]], 0).astype(BF16)
        s = lax.dot_general(q[k].astype(BF16), K, _DIMS["nt"], preferred_element_type=F32) * ATT_SCALE
        s = jnp.where(mask, s, -jnp.inf)
        m = lax.stop_gradient(jnp.maximum(jnp.max(s, -1, keepdims=True), snk[k]))
        p = jnp.exp(s - m)
        pr = p / (jnp.sum(p, -1, keepdims=True) + jnp.exp(snk[k] - m))
        outs.append(jnp.dot(pr.astype(BF16), V, preferred_element_type=F32))
    return outs


def _stack_heads(ref, k):
    return jnp.concatenate([ref[:, (k * AREP + r) * AHD:(k * AREP + r + 1) * AHD].astype(F32) for r in range(AREP)], 0)


def _swa_load(q_ref, cur_ref, prv_ref, snk_ref):
    KV = AKV * AHD
    q = [_stack_heads(q_ref, k) for k in range(AKV)]
    kc = [cur_ref[:, k * AHD:(k + 1) * AHD].astype(F32) for k in range(AKV)]
    vc = [cur_ref[:, KV + k * AHD:KV + (k + 1) * AHD].astype(F32) for k in range(AKV)]
    kp = [prv_ref[:, k * AHD:(k + 1) * AHD].astype(F32) for k in range(AKV)]
    vp = [prv_ref[:, KV + k * AHD:KV + (k + 1) * AHD].astype(F32) for k in range(AKV)]
    snk = [jnp.concatenate([jnp.broadcast_to(snk_ref[:, k * AREP + r:k * AREP + r + 1], (CH, 1)) for r in range(AREP)], 0)
           for k in range(AKV)]
    return q, kp, kc, vp, vc, snk


def _swa_mask(n):
    iq = lax.broadcasted_iota(jnp.int32, (AREP * CH, 2 * CH), 0) % CH
    js = lax.broadcasted_iota(jnp.int32, (AREP * CH, 2 * CH), 1)
    rel = iq + CH - js
    return (rel >= 0) & (rel < CH) & ((n > 0) | (js >= CH))


def _swa_specs(T):
    KV = AKV * AHD
    return [pl.BlockSpec((CH, D), lambda n: (n, 0)), pl.BlockSpec((CH, 2 * KV), lambda n: (n, D // (2 * KV))),
            pl.BlockSpec((CH, 2 * KV), lambda n: (jnp.maximum(n - 1, 0), D // (2 * KV))),
            pl.BlockSpec((1, LANE), lambda n: (0, 0))]


def _swa_fwd(name, qkvr, snk):
    T = qkvr.shape[0]

    def body(q_ref, cur_ref, prv_ref, snk_ref, o_ref):
        outs = _swa_math(*_swa_load(q_ref, cur_ref, prv_ref, snk_ref), _swa_mask(pl.program_id(0)))
        for h in range(AH):
            k, r = divmod(h, AREP)
            o_ref[:, h * AHD:(h + 1) * AHD] = outs[k][r * CH:(r + 1) * CH].astype(BF16)

    return pl.pallas_call(
        body, grid=(T // CH,), in_specs=_swa_specs(T), out_specs=pl.BlockSpec((CH, D), lambda n: (n, 0)),
        out_shape=S((T, D), BF16), compiler_params=_params("parallel"), name=name)(qkvr, qkvr, qkvr, snk)


def _swa_bwd(name, qkvr, snk, do):
    T = qkvr.shape[0]
    KV = AKV * AHD

    def body(q_ref, cur_ref, prv_ref, snk_ref, do_ref, dq_ref, dcur_ref, dprv_ref, dsnk_ref):
        n = pl.program_id(0)

        @pl.when(n == 0)
        def _():
            dsnk_ref[...] = jnp.zeros_like(dsnk_ref)

        prim = _swa_load(q_ref, cur_ref, prv_ref, snk_ref)
        mask = _swa_mask(n)
        _, vjp = jax.vjp(lambda *p: _swa_math(*p, mask), *prim)
        dq, dkp, dkc, dvp, dvc, dsnk = vjp([_stack_heads(do_ref, k) for k in range(AKV)])
        for h in range(AH):
            k, r = divmod(h, AREP)
            dq_ref[:, h * AHD:(h + 1) * AHD] = dq[k][r * CH:(r + 1) * CH].astype(BF16)
            dsnk_ref[:, h:h + 1] += jnp.sum(dsnk[k][r * CH:(r + 1) * CH], 0, keepdims=True)
        for k in range(AKV):
            dcur_ref[:, k * AHD:(k + 1) * AHD] = dkc[k]
            dcur_ref[:, KV + k * AHD:KV + (k + 1) * AHD] = dvc[k]
            dprv_ref[:, k * AHD:(k + 1) * AHD] = dkp[k]
            dprv_ref[:, KV + k * AHD:KV + (k + 1) * AHD] = dvp[k]

    kv = pl.BlockSpec((CH, 2 * KV), lambda n: (n, 0))
    return pl.pallas_call(
        body, grid=(T // CH,), in_specs=_swa_specs(T) + [pl.BlockSpec((CH, D), lambda n: (n, 0))],
        out_specs=[pl.BlockSpec((CH, D), lambda n: (n, 0)), kv, kv, pl.BlockSpec((1, LANE), lambda n: (0, 0))],
        out_shape=[S((T, D), BF16), S((T, 2 * KV), F32), S((T, 2 * KV), F32), S((1, LANE), F32)],
        compiler_params=_params("arbitrary"), name=name)(qkvr, qkvr, qkvr, snk, do)


def _xat_math(q, k, v):
    outs = []
    for h in range(XH):
        s = lax.dot_general(q[h].astype(BF16), k[h].astype(BF16), _DIMS["nt"], preferred_element_type=F32) * X_SCALE
        m = lax.stop_gradient(jnp.max(s, -1, keepdims=True))
        p = jnp.exp(s - m)
        pr = p / jnp.sum(p, -1, keepdims=True)
        outs.append(jnp.dot(pr.astype(BF16), v[h].astype(BF16), preferred_element_type=F32))
    return outs


def _xat_load(q_ref, kv_ref):
    q = [q_ref[:, h * XHD:(h + 1) * XHD].astype(F32) for h in range(XH)]
    k = [kv_ref[:, h * XHD:(h + 1) * XHD].astype(F32) for h in range(XH)]
    v = [kv_ref[:, XW + h * XHD:XW + (h + 1) * XHD].astype(F32) for h in range(XH)]
    return q, k, v


def _xat_fwd(name, q, kv):
    T, M = q.shape[0], kv.shape[0]
    tt = _tile(T, 512)

    def body(q_ref, kv_ref, o_ref):
        outs = _xat_math(*_xat_load(q_ref, kv_ref))
        for h in range(XH):
            o_ref[:, h * XHD:(h + 1) * XHD] = outs[h].astype(BF16)

    return pl.pallas_call(
        body, grid=(T // tt,),
        in_specs=[pl.BlockSpec((tt, XW), lambda i: (i, 0)), pl.BlockSpec((M, 2 * XW), lambda i: (0, 0))],
        out_specs=pl.BlockSpec((tt, XW), lambda i: (i, 0)), out_shape=S((T, XW), BF16),
        compiler_params=_params("parallel"), name=name)(q, kv)


def _xat_bwd(name, q, kv, do):
    T, M = q.shape[0], kv.shape[0]
    tt = _tile(T, 512)

    def body(q_ref, kv_ref, do_ref, dq_ref, dkv_ref):
        first = pl.program_id(0) == 0
        _, vjp = jax.vjp(_xat_math, *_xat_load(q_ref, kv_ref))
        dq, dk, dv = vjp([do_ref[:, h * XHD:(h + 1) * XHD].astype(F32) for h in range(XH)])
        for h in range(XH):
            sl = slice(h * XHD, (h + 1) * XHD)
            dq_ref[:, sl] = dq[h].astype(BF16)
            _acc_store(first, dkv_ref, (slice(None), sl), dk[h])
            _acc_store(first, dkv_ref, (slice(None), slice(XW + h * XHD, XW + (h + 1) * XHD)), dv[h])

    qs = pl.BlockSpec((tt, XW), lambda i: (i, 0))
    kvs = pl.BlockSpec((M, 2 * XW), lambda i: (0, 0))
    return pl.pallas_call(
        body, grid=(T // tt,), in_specs=[qs, kvs, qs], out_specs=[qs, kvs],
        out_shape=[S((T, XW), BF16), S((M, 2 * XW), F32)],
        compiler_params=_params("arbitrary"), name=name)(q, kv, do)


def _loss_head(name, x, gain, target):
    T = x.shape[0]
    tt = _tile(T, 512)

    def body(x_ref, g_ref, t_ref, l_ref, dx_ref, dxb_ref, dg_ref):
        first = pl.program_id(0) == 0
        xv, g = x_ref[...], g_ref[...]
        r = lax.rsqrt(jnp.mean(xv * xv, -1, keepdims=True) + EPS)
        xh = xv * r
        e = xh * g - t_ref[...]
        part = 0.5 * jnp.sum(jnp.mean(e * e, -1, keepdims=True), (0, 1), keepdims=True)
        _acc_store(first, l_ref, (slice(None), slice(None)), jnp.broadcast_to(part, (1, LANE)))
        dy = e * (1.0 / D)
        dxh = dy * g
        dx = r * (dxh - xh * jnp.mean(dxh * xh, -1, keepdims=True))
        dx_ref[...] = dx
        dxb_ref[...] = dx.astype(BF16)
        _acc_store(first, dg_ref, (slice(None), slice(None)), jnp.sum(dy * xh, 0, keepdims=True))

    row = pl.BlockSpec((tt, D), lambda i: (i, 0))
    vec = pl.BlockSpec((1, D), lambda i: (0, 0))
    return pl.pallas_call(
        body, grid=(T // tt,), in_specs=[row, vec, row],
        out_specs=[pl.BlockSpec((1, LANE), lambda i: (0, 0)), row, row, vec],
        out_shape=[S((1, LANE), F32), S((T, D), F32), S((T, D), BF16), S((1, D), F32)],
        compiler_params=_params("arbitrary"), name=name)(x, gain, target)


def _ffn_fwd(tag, x, gain, wgu4, get_wd):
    h = _rms_fwd(f"{tag}_norm", x, gain)
    g, u, a = _ffn_up(f"{tag}_up", h, wgu4, 0)
    x_new = _mm(f"{tag}_down", "nn", a, Op(get_wd(a), "r"), F32, res=x, scale=0.5, tk_t=1408)
    return x_new, (x, gain, h, g, u, a)


def _ffn_bwd(tag, saved, dx, dxb, wgu4, wd4, put):
    x, gain, h, g, u, a = saved
    dgu = _ffn_dact(f"{tag}_dact", dxb, wd4, 0, g, u)
    dwd = _mm(f"{tag}_dwd", "tn", a, dxb, BF16, out=("r", 4, 1, 0), scale=0.5, tm_t=1408)
    dwgu = _mm(f"{tag}_dwgu", "tn", h, Op(dgu, "c"), BF16, out=("c", 4, 1, 0), tn_t=1408)
    tok = put(dwgu, dwd)
    dh = _mm(f"{tag}_dh", "nt", Op(dgu, "c"), Op(wgu4, "c"), BF16, tk_t=2816)
    dx, dxb, dgain = _rms_bwd(f"{tag}_dnorm", x, gain + tok, dh, dx)
    return dx, dxb, dgain


def _xattn_fwd(tag, x, mem, gq, gm, wxq4, wxkv4, wxo4):
    hq = _rms_fwd(f"{tag}_normq", x, gq)
    mn = _rms_fwd(f"{tag}_normm", mem, gm)
    q = _mm(f"{tag}_q", "nn", hq, Op(wxq4, "r"), BF16)
    kv = _mm(f"{tag}_kv", "nn", mn, Op(wxkv4, "r"), BF16)
    o = _xat_fwd(f"{tag}_att", q, kv)
    x_new = _mm(f"{tag}_o", "nn", o, Op(wxo4, "c"), F32, res=x)
    return x_new, (x, mem, gq, gm, hq, mn, q, kv, o)


def _xattn_bwd(tag, saved, dx, dxb, wxq4, wxkv4, wxo4, put):
    x, mem, gq, gm, hq, mn, q, kv, o = saved
    dwxo = _mm(f"{tag}_dwo", "tn", o, dxb, BF16, out=("c", 4, 1, 0))
    do = _mm(f"{tag}_do", "nt", dxb, Op(wxo4, "c"), BF16)
    dq, dkv = _xat_bwd(f"{tag}_datt", q, kv, do)
    dwxq = _mm(f"{tag}_dwq", "tn", hq, dq, BF16, out=("r", 4, 1, 0))
    dwxkv = _mm(f"{tag}_dwkv", "tn", mn, dkv, BF16, out=("r", 4, 1, 0))
    tok = put(dwxq, dwxkv, dwxo)
    dhq = _mm(f"{tag}_dhq", "nt", dq, Op(wxq4, "r"), BF16)
    dmn = _mm(f"{tag}_dmn", "nt", dkv, Op(wxkv4, "r"), BF16)
    _, _, dgm = _rms_bwd(f"{tag}_dnormm", mem, gm + tok, dmn)
    dx, dxb, dgq = _rms_bwd(f"{tag}_dnormq", x, gq, dhq, dx)
    return dx, dxb, dgq, dgm


def _even_fwd(tag, x, gain, w_main, w_dt, p, wout4):
    h = _rms_fwd(f"{tag}_norm", x, gain)
    proj = _mm(f"{tag}_in", "nn", h, w_main, BF16)
    dtr = _mm(f"{tag}_indt", "nn", h, w_dt, F32)
    mix = _gmlp_fwd(f"{tag}_gmlp", proj, p["lng"], p["lnb"], p["ws"], p["bs"])
    ypre, xc = _conv_fwd(f"{tag}_conv", proj, p["cw"], p["cb"])
    mix, prevs = _ssd_fwd(f"{tag}_ssd", xc, dtr, proj, p["dtb"], p["alog"], p["dsk"], p["nrm"], mix)
    x_new = _mm(f"{tag}_out", "nn", mix, Op(wout4, "r", 0), F32, res=x)
    return x_new, (x, gain, h, proj, dtr, mix, ypre, xc, prevs)


def _even_bwd(tag, saved, dx, dxb, w_main, w_dt, p, wout4, put):
    x, gain, h, proj, dtr, mix, ypre, xc, prevs = saved
    T = x.shape[0]
    dwout = _mm(f"{tag}_dwout", "tn", mix, dxb, BF16, out=("r", 4, 1, 0))
    dmix = _mm(f"{tag}_dmix", "nt", dxb, Op(wout4, "r", 0), BF16)
    dproj = jnp.zeros((T, EVEN_MAIN), BF16)
    dproj, dlng, dlnb, dws, dbs = _gmlp_bwd(f"{tag}_dgmlp", proj, p["lng"], p["lnb"], p["ws"], p["bs"], dmix, dproj)
    dproj, dxc, ddtr, ddtb, dalog, ddsk, dnrm = _ssd_bwd(
        f"{tag}_dssd", xc, dtr, proj, prevs, p["dtb"], p["alog"], p["dsk"], p["nrm"], dmix, dproj)
    dproj, dcw, dcb = _conv_bwd(f"{tag}_dconv", proj, ypre, dxc, p["cw"], dproj)
    dw_main = _mm(f"{tag}_dwin", "tn", h, dproj, BF16)
    dw_dt = _mm(f"{tag}_dwdt", "tn", h, ddtr, BF16)
    tok = put(dw_main, dw_dt, dwout)
    dh = _mm(f"{tag}_dh1", "nt", ddtr, w_dt + tok.astype(BF16), F32)
    dh = _mm(f"{tag}_dh2", "nt", dproj, w_main, BF16, res=dh)
    dx, dxb, dgain = _rms_bwd(f"{tag}_dnorm", x, gain, dh, dx)
    small = dict(lng=dlng, lnb=dlnb, ws=dws, bs=dbs, cw=dcw, cb=dcb, dtb=ddtb, alog=dalog, dsk=ddsk, nrm=dnrm)
    return dx, dxb, dgain, small


def _odd_fwd(tag, x, gain, wqkv4, bqkv, snk, wo4, cos, sin):
    h = _rms_fwd(f"{tag}_norm", x, gain)
    qkv = _mm(f"{tag}_qkv", "nn", h, Op(wqkv4, "c", 0), F32, bias=bqkv, tn_t=640)
    qkvr = _rope_fwd(f"{tag}_rope", qkv, cos, sin)
    o = _swa_fwd(f"{tag}_swa", qkvr, snk)
    x_new = _mm(f"{tag}_o", "nn", o, Op(wo4, "r", 0), F32, res=x)
    return x_new, (x, gain, h, qkvr, o)


def _odd_bwd(tag, saved, dx, dxb, wqkv4, snk, wo4, cos, sin, put):
    x, gain, h, qkvr, o = saved
    dwo = _mm(f"{tag}_dwo", "tn", o, dxb, BF16, out=("r", 4, 1, 0))
    do = _mm(f"{tag}_do", "nt", dxb, Op(wo4, "r", 0), BF16)
    dq, dcur, dprv, dsnk = _swa_bwd(f"{tag}_dswa", qkvr, snk, do)
    dqkv, dbias = _rope_bwd(f"{tag}_drope", dq, dcur, dprv, cos, sin)
    dwqkv = _mm(f"{tag}_dwqkv", "tn", h, dqkv, BF16, out=("c", 4, 1, 0), tn_t=640)
    tok = put(dwqkv, dwo)
    dh = _mm(f"{tag}_dh", "nt", dqkv, Op(wqkv4, "c", 0), BF16, tk_t=640)
    dx, dxb, dgain = _rms_bwd(f"{tag}_dnorm", x, gain + tok, dh, dx)
    return dx, dxb, dgain, dbias, dsnk


def _row(v):
    return v.reshape(1, -1).astype(F32)


def _pad_lanes(v, n=LANE):
    v = v.reshape(1, -1).astype(F32)
    return jnp.pad(v, ((0, 0), (0, n - v.shape[1])))


def _local_step(x, mem, positions, target, getw, P, putg):
    inv_freq = ROPE_THETA ** (-jnp.arange(0, ROT, 2, dtype=F32) / ROT)
    ang = positions.astype(F32)[:, None] * inv_freq
    cos8, sin8 = jnp.cos(ang), jnp.sin(ang)
    ones, zeros = jnp.ones((x.shape[0], AHD - ROT), F32), jnp.zeros((x.shape[0], AHD - ROT), F32)
    cos = jnp.tile(jnp.concatenate([cos8, cos8, ones], 1), (1, 2))
    sin = jnp.tile(jnp.concatenate([-sin8, sin8, zeros], 1), (1, 2))

    snk = _pad_lanes(P["sinks"])
    W = {}

    def w(name, layer, after):
        if (name, layer) not in W:
            W[name, layer] = getw(name, layer, after)
        return W[name, layer]

    saved = []
    for i in range(2):
        x, s1 = _ffn_fwd(f"l{i}_ffn1", x, _row(P["norm_ffn1"][i]), w("w_ffn1_gu", i, x),
                         functools.partial(w, "w_ffn1_down", i))
        if i == 0:
            ev = dict(lng=_row(P["gm_ln_g"]), lnb=_row(P["gm_ln_b"]), ws=P["gm_ws"].reshape(GM_G, CH, CH),
                      bs=P["gm_bs"].reshape(GM_G, CH, 1), cw=w("conv_w", 0, x), cb=_row(P["conv_b"]),
                      dtb=_pad_lanes(P["dt_bias"]), alog=_pad_lanes(P["a_log"]), dsk=_pad_lanes(P["d_skip"]),
                      nrm=_row(P["ssd_norm"]))
            w_main, w_dt = w("w_in_even", 0, x)
            x, s2 = _even_fwd("l0_mix", x, _row(P["norm_mix"][0]), w_main, w_dt, ev, w("w_out_even", 0, x))
        else:
            x, s2 = _odd_fwd("l1_mix", x, _row(P["norm_mix"][1]), w("w_qkv", 0, x), w("b_qkv", 0, x), snk,
                             w("w_o_odd", 0, x), cos, sin)
        x, s3 = _xattn_fwd(f"l{i}_xat", x, mem, _row(P["norm_xq"][i]), _row(P["norm_mem"][i]),
                           w("w_xq", i, x), w("w_xkv", i, x), w("w_xo", i, x))
        x, s4 = _ffn_fwd(f"l{i}_ffn2", x, _row(P["norm_ffn2"][i]), w("w_ffn2_gu", i, x),
                         functools.partial(w, "w_ffn2_down", i))
        saved.append((s1, s2, s3, s4))

    loss, dx, dxb, d_final = _loss_head("loss_head", x, _row(P["final_norm"]), target)

    sm = {}
    dn = {k: [None, None] for k in ("norm_ffn1", "norm_mix", "norm_xq", "norm_mem", "norm_ffn2")}
    for i in (1, 0):
        s1, s2, s3, s4 = saved[i]
        dx, dxb, dn["norm_ffn2"][i] = _ffn_bwd(
            f"l{i}_ffn2", s4, dx, dxb, W["w_ffn2_gu", i], W["w_ffn2_down", i],
            lambda dwgu, dwd, i=i: putg({("w_ffn2_gu", i): dwgu, ("w_ffn2_down", i): dwd}))
        dx, dxb, dn["norm_xq"][i], dn["norm_mem"][i] = _xattn_bwd(
            f"l{i}_xat", s3, dx, dxb, W["w_xq", i], W["w_xkv", i], W["w_xo", i],
            lambda dwxq, dwxkv, dwxo, i=i: putg({("w_xq", i): dwxq, ("w_xkv", i): dwxkv, ("w_xo", i): dwxo}))
        if i == 0:
            dx, dxb, dn["norm_mix"][0], sm_even = _even_bwd(
                "l0_mix", s2, dx, dxb, w_main, w_dt, ev, W["w_out_even", 0],
                lambda dw_main, dw_dt, dwout: putg({("w_in_even", 0): (dw_main, dw_dt), ("w_out_even", 0): dwout}))
        else:
            dx, dxb, dn["norm_mix"][1], sm["b_qkv"], sm["sinks"] = _odd_bwd(
                "l1_mix", s2, dx, dxb, W["w_qkv", 0], snk, W["w_o_odd", 0], cos, sin,
                lambda dwqkv, dwo: putg({("w_qkv", 0): dwqkv, ("w_o_odd", 0): dwo}))
        dx, dxb, dn["norm_ffn1"][i] = _ffn_bwd(
            f"l{i}_ffn1", s1, dx, dxb, W["w_ffn1_gu", i], W["w_ffn1_down", i],
            lambda dwgu, dwd, i=i: putg({("w_ffn1_gu", i): dwgu, ("w_ffn1_down", i): dwd}))
    for k, v in dn.items():
        sm[k] = jnp.concatenate(v, 0)
    sm.update(gm_ln_g=sm_even["lng"], gm_ln_b=sm_even["lnb"], gm_ws=sm_even["ws"], gm_bs=sm_even["bs"],
              conv_w=sm_even["cw"], conv_b=sm_even["cb"], dt_bias=sm_even["dtb"][:, :NH], a_log=sm_even["alog"][:, :NH],
              d_skip=sm_even["dsk"][:, :NH], ssd_norm=sm_even["nrm"], sinks=sm["sinks"][:, :AH], final_norm=d_final)
    return loss[0, 0], dx, sm


def _chip_peers():
    x, y, c = lax.axis_index("x"), lax.axis_index("y"), lax.axis_index("c")
    return 2 * x + y, [((1 - x, y, c), 2 * (1 - x) + y), ((x, 1 - y, c), 2 * x + (1 - y)),
                       ((1 - x, 1 - y, c), 2 * (1 - x) + (1 - y))]


def _any_specs(n):
    return [pl.BlockSpec(memory_space=pl.ANY)] * n


_HBM = pl.BlockSpec(memory_space=pltpu.HBM)
_SEM = pl.BlockSpec(memory_space=pltpu.SEMAPHORE)
_EFFECT = pltpu.SideEffectType.DATAFLOW_SIDE_EFFECTING


def _own_slot(piece, chip):
    zone = lax.empty((4,) + piece.shape, piece.dtype)
    return lax.dynamic_update_slice(zone, piece[None], (chip,) + (0,) * piece.ndim)


def _chip_copies(srcs, lands, ssems, rsems):
    me, peers = _chip_peers()
    return [pltpu.make_async_remote_copy(src_ref=lands[i].at[me] if srcs[i] is None else srcs[i].at[chip],
                                         dst_ref=lands[i].at[me], send_sem=ssems[i].at[j], recv_sem=rsems[i].at[j],
                                         device_id=dev, device_id_type=MESH)
            for i in range(len(lands)) for j, (dev, chip) in enumerate(peers)]


def _exchange_start(name, srcs, lands):
    n = len(lands)
    ns = 0 if srcs is None else n

    def body(*refs):
        src_refs = [None] * n if srcs is None else refs[:n]
        land_refs = refs[ns:ns + n]
        ssems, rsems = refs[ns + n:ns + 2 * n], refs[ns + 2 * n:ns + 3 * n]
        token = refs[2 * ns + 4 * n]
        for cp in _chip_copies(src_refs, land_refs, ssems, rsems):
            cp.start()
        token[...] = jnp.zeros_like(token)

    ins = ([] if srcs is None else list(srcs)) + list(lands)
    res = pl.pallas_call(
        body, name=name,
        out_shape=[pltpu.SemaphoreType.DMA((3,))] * (2 * n) + [pltpu.HBM(a.shape, a.dtype) for a in ins] + [S((8, LANE), F32)],
        in_specs=[_HBM] * (ns + n),
        out_specs=[_SEM] * (2 * n) + [_HBM] * (ns + n) + [pl.BlockSpec(memory_space=pltpu.VMEM)],
        input_output_aliases={i: 2 * n + i for i in range(ns + n)},
        compiler_params=pltpu.CompilerParams(has_side_effects=_EFFECT),
    )(*[pltpu.with_memory_space_constraint(a, pltpu.HBM) for a in ins])
    items = [(res[i], res[n + i], None if srcs is None else res[2 * n + i], res[2 * n + ns + i]) for i in range(n)]
    return items, res[2 * n + ns + n][0, 0]


def _exchange_wait(name, item, after):
    ssem, rsem, src, land = item
    ns = 0 if src is None else 1

    def body(*refs):
        src_ref = refs[0] if ns else None
        land_ref, ssem_ref, rsem_ref = refs[ns], refs[ns + 1], refs[ns + 2]
        for cp in _chip_copies([src_ref], [land_ref], [ssem_ref], [rsem_ref]):
            cp.wait_send()
            cp.wait_recv()

    ins = ([src] if ns else []) + [land]
    return pl.pallas_call(
        body, name=name, out_shape=[pltpu.HBM(a.shape, a.dtype) for a in ins],
        in_specs=[_HBM] * (ns + 1) + [_SEM, _SEM, pl.BlockSpec(memory_space=pl.ANY)], out_specs=[_HBM] * (ns + 1),
        input_output_aliases={i: i for i in range(ns + 1)}, compiler_params=pltpu.CompilerParams(has_side_effects=_EFFECT),
    )(*ins, ssem, rsem, after)[ns]


def _swap_sibling(name, arrs):
    n = len(arrs)

    def body(*refs):
        ins, outs = refs[:n], refs[n:2 * n]
        ssem, rsem = refs[2 * n:]
        dev = (lax.axis_index("x"), lax.axis_index("y"), 1 - lax.axis_index("c"))
        copies = [pltpu.make_async_remote_copy(src_ref=ins[i], dst_ref=outs[i], send_sem=ssem.at[i], recv_sem=rsem.at[i],
                                               device_id=dev, device_id_type=MESH) for i in range(n)]
        for cp in copies:
            cp.start()
        for cp in copies:
            cp.wait()

    return pl.pallas_call(
        body, in_specs=_any_specs(n), out_specs=_any_specs(n), out_shape=[S(a.shape, a.dtype) for a in arrs],
        scratch_shapes=[pltpu.SemaphoreType.DMA((n,)), pltpu.SemaphoreType.DMA((n,))],
        compiler_params=pltpu.CompilerParams(has_side_effects=True), name=name)(*arrs)


def _gather_all(name, v):
    def body(v_ref, o_ref, ssem, rsem, lsem):
        x, y, c = lax.axis_index("x"), lax.axis_index("y"), lax.axis_index("c")
        me = 4 * x + 2 * y + c
        loc = pltpu.make_async_copy(v_ref, o_ref.at[me], lsem)
        loc.start()
        copies = []
        for k in range(1, 8):
            fx, fy, fc = (k >> 2) & 1, (k >> 1) & 1, k & 1
            dev = (x ^ fx, y ^ fy, c ^ fc)
            cp = pltpu.make_async_remote_copy(src_ref=v_ref, dst_ref=o_ref.at[me], send_sem=ssem.at[k - 1],
                                              recv_sem=rsem.at[k - 1], device_id=dev, device_id_type=MESH)
            cp.start()
            copies.append(cp)
        for cp in copies:
            cp.wait()
        loc.wait()

    return pl.pallas_call(
        body, in_specs=_any_specs(1), out_specs=pl.BlockSpec(memory_space=pl.ANY), out_shape=S((8,) + v.shape, v.dtype),
        scratch_shapes=[pltpu.SemaphoreType.DMA((7,)), pltpu.SemaphoreType.DMA((7,)), pltpu.SemaphoreType.DMA(())],
        compiler_params=pltpu.CompilerParams(has_side_effects=True), name=name)(v)


def _row_tile(R, row_bytes, budget=4 << 20):
    if R * row_bytes <= budget or R % 16:
        return R
    t = max(16, budget // row_bytes // 16 * 16)
    while R % t:
        t -= 16
    return t


def _sum_slots(name, r, n):
    _, R, C = r.shape
    tr = _row_tile(R, C * (n * r.dtype.itemsize + 4))

    def body(r_ref, o_ref):
        acc = r_ref[0].astype(F32)
        for j in range(1, n):
            acc = acc + r_ref[j].astype(F32)
        o_ref[...] = acc

    return pl.pallas_call(
        body, grid=(R // tr,), in_specs=[pl.BlockSpec((n, tr, C), lambda i: (0, i, 0))],
        out_specs=pl.BlockSpec((tr, C), lambda i: (i, 0)), out_shape=S((R, C), F32),
        compiler_params=_params("parallel"), name=name)(r)


def _adamw(name, w, m, v, layer, g1, g2=None, into=None):
    nl, R, C = w.shape
    tr = _row_tile(R, C * 4 * 9)
    two, has_into = g2 is not None, into is not None

    def body(w_ref, m_ref, v_ref, g1_ref, *rest):
        rest = list(rest)
        g = g1_ref[...]
        if two:
            g = g + rest.pop(0)[...]
        g_ref, d_ref, nm_ref, nv_ref = rest[-4:]
        mn = B1 * m_ref[...] + (1.0 - B1) * g
        vn = B2 * v_ref[...] + (1.0 - B2) * jnp.square(g)
        m_hat = mn / (1.0 - B1 ** STEP)
        v_hat = vn / (1.0 - B2 ** STEP)
        g_ref[...] = g
        d_ref[...] = -LR * (m_hat / (jnp.sqrt(v_hat) + AEPS) + WD * w_ref[...])
        nm_ref[...] = mn
        nv_ref[...] = vn

    blk = pl.BlockSpec((tr, C), lambda i: (i, 0))
    lay = pl.BlockSpec((None, tr, C), lambda i: (layer, i, 0))
    args = [w, m, v, g1] + ([g2] if two else []) + (list(into) if has_into else [])
    in_specs = [lay] * 3 + [blk] * (2 if two else 1) + (_any_specs(4) if has_into else [])
    aliases = {len(args) - 4 + t: t for t in range(4)} if has_into else {}
    return pl.pallas_call(
        body, grid=(R // tr,), in_specs=in_specs, out_specs=[lay] * 4, out_shape=[S((nl, R, C), F32)] * 4,
        input_output_aliases=aliases, compiler_params=_params("parallel"), name=name)(*args)


_USE_ORDER = [("w_ffn1_gu", 0), ("w_ffn1_down", 0), ("conv_w", 0), ("w_in_even", 0), ("w_out_even", 0), ("w_xq", 0),
              ("w_xkv", 0), ("w_xo", 0), ("w_ffn2_gu", 0), ("w_ffn2_down", 0), ("w_ffn1_gu", 1), ("w_ffn1_down", 1),
              ("w_qkv", 0), ("b_qkv", 0), ("w_o_odd", 0), ("w_xq", 1), ("w_xkv", 1), ("w_xo", 1), ("w_ffn2_gu", 1),
              ("w_ffn2_down", 1)]
_SMALL = ["norm_ffn1", "norm_mix", "gm_ln_g", "gm_ln_b", "gm_ws", "gm_bs", "conv_w", "conv_b", "dt_bias", "a_log",
          "d_skip", "ssd_norm", "b_qkv", "sinks", "norm_xq", "norm_mem", "norm_ffn2", "final_norm"]
_WEIGHTS = ["norm_ffn1", "w_ffn1_gu", "w_ffn1_down", "norm_mix", "w_in_even", "gm_ln_g", "gm_ln_b", "gm_ws", "gm_bs",
            "conv_w", "conv_b", "dt_bias", "a_log", "d_skip", "ssd_norm", "w_out_even", "w_qkv", "b_qkv", "sinks",
            "w_o_odd", "norm_xq", "norm_mem", "w_xq", "w_xkv", "w_xo", "norm_ffn2", "w_ffn2_gu", "w_ffn2_down",
            "final_norm"]


def _pack(arrs):
    rows = []
    for a in arrs:
        f = a.reshape(-1).astype(F32)
        pad = (-f.shape[0]) % LANE
        rows.append(jnp.pad(f, (0, pad)).reshape(-1, LANE))
    out = jnp.concatenate(rows, 0)
    pad = (-out.shape[0]) % 8
    return jnp.pad(out, ((0, pad), (0, 0)))


def _unpack(packed, shapes):
    outs, r = [], 0
    for shp in shapes:
        n = math.prod(shp)
        nr = -(-n // LANE)
        outs.append(packed[r:r + nr].reshape(-1)[:n].reshape(shp))
        r += nr
    return outs


def kernel(x, mem, positions, norm_ffn1, w_ffn1_gu, w_ffn1_down, norm_mix, w_in_even, gm_ln_g, gm_ln_b, gm_ws, gm_bs, conv_w, conv_b, dt_bias, a_log, d_skip, ssd_norm, w_out_even, w_qkv, b_qkv, sinks, w_o_odd, norm_xq, norm_mem, w_xq, w_xkv, w_xo, norm_ffn2, w_ffn2_gu, w_ffn2_down, final_norm, loss_target, m_norm_ffn1, m_w_ffn1_gu, m_w_ffn1_down, m_norm_mix, m_w_in_even, m_gm_ln_g, m_gm_ln_b, m_gm_ws, m_gm_bs, m_conv_w, m_conv_b, m_dt_bias, m_a_log, m_d_skip, m_ssd_norm, m_w_out_even, m_w_qkv, m_b_qkv, m_sinks, m_w_o_odd, m_norm_xq, m_norm_mem, m_w_xq, m_w_xkv, m_w_xo, m_norm_ffn2, m_w_ffn2_gu, m_w_ffn2_down, m_final_norm, v_norm_ffn1, v_w_ffn1_gu, v_w_ffn1_down, v_norm_mix, v_w_in_even, v_gm_ln_g, v_gm_ln_b, v_gm_ws, v_gm_bs, v_conv_w, v_conv_b, v_dt_bias, v_a_log, v_d_skip, v_ssd_norm, v_w_out_even, v_w_qkv, v_b_qkv, v_sinks, v_w_o_odd, v_norm_xq, v_norm_mem, v_w_xq, v_w_xkv, v_w_xo, v_norm_ffn2, v_w_ffn2_gu, v_w_ffn2_down, v_final_norm):
    a = dict(locals())
    w = {k: a[k] for k in _WEIGHTS}
    mom = {k: a["m_" + k] for k in _WEIGHTS}
    var = {k: a["v_" + k] for k in _WEIGHTS}
    chip = 2 * lax.axis_index("x") + lax.axis_index("y")

    shards = [w[k][i:i + 1] if k in ("conv_w", "b_qkv") else w[k][i:i + 1].astype(BF16) for k, i in _USE_ORDER]
    items, _ = _exchange_start("gather_start", None, [_own_slot(s, chip) for s in shards])
    pending = dict(zip(_USE_ORDER, items))

    def getw(name, layer, after):
        got = _exchange_wait(f"gather_wait_{name}_{layer}", pending.pop((name, layer)), after)
        if name == "w_in_even":
            w_in = jnp.transpose(got[:, 0], (1, 0, 2)).reshape(D, EVEN_IN)
            return w_in[:, :EVEN_MAIN], jnp.pad(w_in[:, EVEN_MAIN:], ((0, 0), (0, LANE - (EVEN_IN - EVEN_MAIN))))
        if name == "conv_w":
            return jnp.transpose(got[:, 0], (1, 0, 2)).reshape(4, CONV_C)
        if name == "b_qkv":
            return got.reshape(1, ODD_IN)
        return got

    sent = []

    def putg(grads):
        names, arrs = [], []
        for (name, layer), g in grads.items():
            if name == "w_in_even":
                dw_in = jnp.concatenate([g[0], g[1][:, :EVEN_IN - EVEN_MAIN]], 1)
                g = jnp.transpose(dw_in.reshape(D, 4, EVEN_IN // 4), (1, 0, 2)).reshape(4, 1, D, EVEN_IN // 4)
            names.append((name, layer))
            arrs.append(g)
        own = [_own_slot(lax.dynamic_index_in_dim(g, chip, 0, keepdims=False), chip) for g in arrs]
        its, tok = _exchange_start(f"scatter_start_{names[0][0]}_{names[0][1]}", arrs, own)
        sent.append(list(zip(names, its)))
        return tok

    P = {k: w[k] for k in _SMALL}
    loss, grad_x, sm = _local_step(x[0], mem[0], positions[0], loss_target[0], getw, P, putg)
    loss = lax.psum(loss, ("x", "y", "c"))

    out = {}

    def update(groups, tag):
        names = [nm for grp in groups for nm, _ in grp]
        part = []
        for grp in groups:
            for (name, layer), it in grp:
                r = _exchange_wait(f"scatter_wait_{name}_{layer}", it, grad_x)
                part.append(_sum_slots(f"sum_{name}_{layer}", r.reshape(4, -1, r.shape[-1]), 4))
        other = _swap_sibling(f"swap_partials_{tag}", part)
        for (name, layer), p1, p2 in zip(names, part, other):
            out[name] = _adamw(f"adamw_{name}_{layer}", w[name], mom[name], var[name], layer, p1, p2, out.get(name))

    update(sent[:-1], "a")
    update(sent[-1:], "b")

    full_shapes = {k: w[k].shape for k in _SMALL}
    full_shapes["conv_w"], full_shapes["b_qkv"] = (1, 4, CONV_C), (1, ODD_IN)
    packed = _pack([sm[k] for k in _SMALL])
    total = _sum_slots("sum_small", _gather_all("gather_small", packed), 8)
    gs = dict(zip(_SMALL, _unpack(total, [full_shapes[k] for k in _SMALL])))
    gs["conv_w"] = lax.dynamic_slice_in_dim(gs["conv_w"], chip * (CONV_C // 4), CONV_C // 4, 2)
    gs["b_qkv"] = lax.dynamic_slice_in_dim(gs["b_qkv"], chip * (ODD_IN // 4), ODD_IN // 4, 1)
    res = _adamw("adamw_small", _pack([w[k] for k in _SMALL])[None], _pack([mom[k] for k in _SMALL])[None],
                 _pack([var[k] for k in _SMALL])[None], 0, _pack([gs[k] for k in _SMALL]))
    shapes = [w[k].shape for k in _SMALL]
    for k, g, d, nm, nv in zip(_SMALL, *[_unpack(r[0], shapes) for r in res]):
        out[k] = [g, d, nm, nv]

    return (loss, grad_x[None], *[out[k][0] for k in _WEIGHTS], *[out[k][1] for k in _WEIGHTS],
            *[out[k][2] for k in _WEIGHTS], *[out[k][3] for k in _WEIGHTS])
```

```python
import functools
import math

import jax
import jax.numpy as jnp
from jax import lax
from jax.experimental import pallas as pl
from jax.experimental.pallas import tpu as pltpu

F32, BF16 = jnp.float32, jnp.bfloat16
S = jax.ShapeDtypeStruct
MESH = pl.DeviceIdType.MESH

D = 2048
DFF = 5632
EPS = 1e-5
CH = 128
GM_G, GM_GD = 4, 512
NH, HD, NG, HPG, NS = 32, 64, 4, 8, 128
CONV_C = 3072
EVEN_MAIN, EVEN_IN = 9216, 9248
AH, AKV, AREP, AHD = 32, 4, 8, 64
ODD_IN = 2560
XH, XHD, XW = 4, 128, 512
ATT_SCALE = AHD ** -0.5
X_SCALE = XHD ** -0.5
ROPE_THETA = 500000.0
ROT = 16
LR, B1, B2, AEPS, WD, STEP = 0.001, 0.9, 0.999, 1e-08, 0.01, 10
LANE = 128
VMEM_LIMIT_V7X = 56 * 1024 * 1024


def _params(*sem):
    return pltpu.CompilerParams(dimension_semantics=sem, vmem_limit_bytes=VMEM_LIMIT_V7X)


def _tile(dim, target):
    if dim <= target:
        return dim
    t = (target // LANE) * LANE
    while t > LANE and dim % t:
        t -= LANE
    assert dim % t == 0, (dim, target)
    return t


class Op:
    def __init__(self, arr, kind=None, layer=0):
        self.arr, self.kind, self.layer = arr, kind, layer
        if kind is None:
            self.R, self.C = arr.shape
        else:
            L = arr.shape[0]
            self.R = arr.shape[2] * (L if kind == "r" else 1)
            self.C = arr.shape[3] * (L if kind == "c" else 1)

    def unit(self, axis):
        if self.kind == "r" and axis == 0:
            return self.arr.shape[2]
        if self.kind == "c" and axis == 1:
            return self.arr.shape[3]
        return (self.R, self.C)[axis]

    def spec(self, tr, tc, pick):
        if self.kind is None:
            return pl.BlockSpec((tr, tc), lambda i, j, k: pick(i, j, k))
        l = self.layer
        if self.kind == "c":
            per = self.arr.shape[3] // tc
            return pl.BlockSpec((None, None, tr, tc),
                                lambda i, j, k: (pick(i, j, k)[1] // per, l, pick(i, j, k)[0], pick(i, j, k)[1] % per))
        per = self.arr.shape[2] // tr
        return pl.BlockSpec((None, None, tr, tc),
                            lambda i, j, k: (pick(i, j, k)[0] // per, l, pick(i, j, k)[0] % per, pick(i, j, k)[1]))


_DIMS = {"nn": (((1,), (0,)), ((), ())), "nt": (((1,), (1,)), ((), ())), "tn": (((0,), (0,)), ((), ()))}
_PICK_A = {"nn": lambda i, j, k: (i, k), "nt": lambda i, j, k: (i, k), "tn": lambda i, j, k: (k, i)}
_PICK_B = {"nn": lambda i, j, k: (k, j), "nt": lambda i, j, k: (j, k), "tn": lambda i, j, k: (k, j)}


def _mm(name, mode, a, b, out_dtype, *, out=None, res=None, bias=None, scale=1.0,
        tm_t=1024, tn_t=1024, tk_t=2048):
    if not isinstance(a, Op):
        a = Op(a)
    if not isinstance(b, Op):
        b = Op(b)
    if mode == "nn":
        M, K, N = a.R, a.C, b.C
        assert b.R == K
        um, uk, un = a.unit(0), math.gcd(a.unit(1), b.unit(0)), b.unit(1)
    elif mode == "nt":
        M, K, N = a.R, a.C, b.R
        assert b.C == K
        um, uk, un = a.unit(0), math.gcd(a.unit(1), b.unit(1)), b.unit(0)
    else:
        K, M, N = a.R, a.C, b.C
        assert b.R == K
        um, uk, un = a.unit(1), math.gcd(a.unit(0), b.unit(0)), b.unit(1)
    if out is not None:
        okind, oL, olayers, olayer = out
        if okind == "c":
            un = math.gcd(un, N // oL)
        else:
            um = math.gcd(um, M // oL)
    tm, tn, tk = _tile(um, tm_t), _tile(un, tn_t), _tile(uk, tk_t)
    gi, gj, gk = M // tm, N // tn, K // tk
    a_blk = (tm, tk) if mode != "tn" else (tk, tm)
    b_blk = {"nn": (tk, tn), "nt": (tn, tk), "tn": (tk, tn)}[mode]
    in_specs = [a.spec(*a_blk, _PICK_A[mode]), b.spec(*b_blk, _PICK_B[mode])]
    args = [a.arr, b.arr]
    if res is not None:
        in_specs.append(pl.BlockSpec((tm, tn), lambda i, j, k: (i, j)))
        args.append(res)
    if bias is not None:
        in_specs.append(pl.BlockSpec((1, tn), lambda i, j, k: (0, j)))
        args.append(bias)
    if out is None:
        out_shape = S((M, N), out_dtype)
        out_spec = pl.BlockSpec((tm, tn), lambda i, j, k: (i, j))
    else:
        shp = (oL, olayers, M, N // oL) if okind == "c" else (oL, olayers, M // oL, N)
        out_shape = S(shp, out_dtype)
        out_spec = Op(out_shape, okind, olayer).spec(tm, tn, lambda i, j, k: (i, j))
    has_res, has_bias = res is not None, bias is not None
    dims = _DIMS[mode]

    def body(a_ref, b_ref, *rest):
        rest = list(rest)
        res_ref = rest.pop(0) if has_res else None
        bias_ref = rest.pop(0) if has_bias else None
        o_ref = rest.pop(0)
        part = lax.dot_general(a_ref[...].astype(BF16), b_ref[...].astype(BF16), dims, preferred_element_type=F32)

        def finish(r):
            if scale != 1.0:
                r = r * scale
            if has_bias:
                r = r + bias_ref[...]
            if has_res:
                r = r + res_ref[...]
            o_ref[...] = r.astype(o_ref.dtype)

        if gk == 1:
            finish(part)
            return
        acc, = rest
        k = pl.program_id(2)

        @pl.when(k == 0)
        def _():
            acc[...] = part

        @pl.when((k > 0) & (k < gk - 1))
        def _():
            acc[...] += part

        @pl.when(k == gk - 1)
        def _():
            finish(acc[...] + part)

    return pl.pallas_call(
        body, grid=(gi, gj, gk), in_specs=in_specs, out_specs=out_spec, out_shape=out_shape,
        scratch_shapes=[pltpu.VMEM((tm, tn), F32)] if gk > 1 else [],
        compiler_params=_params("parallel", "parallel", "arbitrary"), name=name)(*args)


def _rms_fwd(name, x, gain):
    T = x.shape[0]
    tt = _tile(T, 512)

    def body(x_ref, g_ref, o_ref):
        xv = x_ref[...]
        r = lax.rsqrt(jnp.mean(xv * xv, -1, keepdims=True) + EPS)
        o_ref[...] = (xv * r * g_ref[...]).astype(BF16)

    return pl.pallas_call(
        body, grid=(T // tt,),
        in_specs=[pl.BlockSpec((tt, D), lambda i: (i, 0)), pl.BlockSpec((1, D), lambda i: (0, 0))],
        out_specs=pl.BlockSpec((tt, D), lambda i: (i, 0)), out_shape=S((T, D), BF16),
        compiler_params=_params("parallel"), name=name)(x, gain)


def _rms_bwd(name, x, gain, dh, dx_in=None):
    T = x.shape[0]
    tt = _tile(T, 512)
    has_in = dx_in is not None

    def body(x_ref, g_ref, dh_ref, *rest):
        rest = list(rest)
        dxin_ref = rest.pop(0) if has_in else None
        dx_ref, dxb_ref, dg_ref = rest
        xv = x_ref[...]
        r = lax.rsqrt(jnp.mean(xv * xv, -1, keepdims=True) + EPS)
        xh = xv * r
        dy = dh_ref[...].astype(F32)
        dxh = dy * g_ref[...]
        dx = r * (dxh - xh * jnp.mean(dxh * xh, -1, keepdims=True))
        if has_in:
            dx = dx + dxin_ref[...]
        dx_ref[...] = dx
        dxb_ref[...] = dx.astype(BF16)
        part = jnp.sum(dy * xh, 0, keepdims=True)

        @pl.when(pl.program_id(0) == 0)
        def _():
            dg_ref[...] = part

        @pl.when(pl.program_id(0) > 0)
        def _():
            dg_ref[...] += part

    row = pl.BlockSpec((tt, D), lambda i: (i, 0))
    vec = pl.BlockSpec((1, D), lambda i: (0, 0))
    args = [x, gain, dh] + ([dx_in] if has_in else [])
    return pl.pallas_call(
        body, grid=(T // tt,), in_specs=[row, vec, row] + ([row] if has_in else []),
        out_specs=[row, row, vec], out_shape=[S((T, D), F32), S((T, D), BF16), S((1, D), F32)],
        compiler_params=_params("arbitrary"), name=name)(*args)


def _sigmoid(x):
    return 0.5 * jnp.tanh(0.5 * x) + 0.5


def _ffn_up(name, h, w4, layer, tm_t=512, tn_t=1408):
    T = h.shape[0]
    n_sh = w4.shape[3]
    tm, tn = _tile(T, tm_t), _tile(n_sh, tn_t)
    per = n_sh // tn

    def body(h_ref, wg_ref, wu_ref, g_ref, u_ref, a_ref):
        hv = h_ref[...]
        g = jnp.dot(hv, wg_ref[...], preferred_element_type=F32)
        u = jnp.dot(hv, wu_ref[...], preferred_element_type=F32)
        g_ref[...] = g.astype(BF16)
        u_ref[...] = u.astype(BF16)
        a_ref[...] = (g * _sigmoid(g) * u).astype(BF16)

    o = pl.BlockSpec((tm, tn), lambda j, i: (i, j))
    return pl.pallas_call(
        body, grid=(DFF // tn, T // tm),
        in_specs=[pl.BlockSpec((tm, D), lambda j, i: (i, 0)),
                  pl.BlockSpec((None, None, D, tn), lambda j, i: (j // per, layer, 0, j % per)),
                  pl.BlockSpec((None, None, D, tn), lambda j, i: (2 + j // per, layer, 0, j % per))],
        out_specs=[o, o, o], out_shape=[S((T, DFF), BF16)] * 3,
        compiler_params=_params("parallel", "parallel"), name=name)(h, w4, w4)


def _ffn_dact(name, dxb, wd4, layer, g, u, tm_t=512):
    T = dxb.shape[0]
    r_sh = wd4.shape[2]
    tm, tn = _tile(T, tm_t), _tile(r_sh, 1408)
    per = r_sh // tn

    def body(dx_ref, w_ref, g_ref, u_ref, o_ref):
        da = 0.5 * lax.dot_general(dx_ref[...], w_ref[...], _DIMS["nt"], preferred_element_type=F32)
        gv, uv = g_ref[...].astype(F32), u_ref[...].astype(F32)
        sg = _sigmoid(gv)
        o_ref[0, 0] = (da * uv * sg * (1.0 + gv * (1.0 - sg))).astype(BF16)
        o_ref[1, 0] = (da * gv * sg).astype(BF16)

    t = pl.BlockSpec((tm, tn), lambda j, i: (i, j))
    return pl.pallas_call(
        body, grid=(DFF // tn, T // tm),
        in_specs=[pl.BlockSpec((tm, D), lambda j, i: (i, 0)),
                  pl.BlockSpec((None, None, tn, D), lambda j, i: (j // per, layer, j % per, 0)), t, t],
        out_specs=pl.BlockSpec((2, 1, tm, tn), lambda j, i: (0, 0, i, j)), out_shape=S((2, 1, T, DFF), BF16),
        compiler_params=_params("parallel", "parallel"), name=name)(dxb, wd4, g, u)


def _gelu(x):
    return 0.5 * x * (1.0 + lax.erf(x * 0.7071067811865476))


def _causal(n):
    return lax.broadcasted_iota(jnp.int32, (n, n), 0) >= lax.broadcasted_iota(jnp.int32, (n, n), 1)


def _gmlp_math(u_raw, v_raw, lng, lnb, ws, bs):
    causal = _causal(CH)
    outs = []
    for g in range(GM_G):
        u, v = _gelu(u_raw[g]), _gelu(v_raw[g])
        mu = jnp.mean(v, -1, keepdims=True)
        var = jnp.mean(jnp.square(v - mu), -1, keepdims=True)
        vn = (v - mu) * lax.rsqrt(var + EPS) * lng[g] + lnb[g]
        wm = jnp.where(causal, ws[g], 0.0)
        s = jnp.dot(wm.astype(BF16), vn.astype(BF16), preferred_element_type=F32) + bs[g]
        outs.append(u * s)
    return outs


def _gmlp_load(proj_ref, lng_ref, lnb_ref, ws_ref, bs_ref):
    sl = lambda g, off: slice(off + g * GM_GD, off + (g + 1) * GM_GD)
    u_raw = [proj_ref[:, sl(g, 0)].astype(F32) for g in range(GM_G)]
    v_raw = [proj_ref[:, sl(g, D)].astype(F32) for g in range(GM_G)]
    lng = [lng_ref[:, sl(g, 0)] for g in range(GM_G)]
    lnb = [lnb_ref[:, sl(g, 0)] for g in range(GM_G)]
    ws = [ws_ref[g] for g in range(GM_G)]
    bs = [bs_ref[g] for g in range(GM_G)]
    return u_raw, v_raw, lng, lnb, ws, bs


_GM_PAR = lambda: [pl.BlockSpec((1, D), lambda i: (0, 0)), pl.BlockSpec((1, D), lambda i: (0, 0)),
                   pl.BlockSpec((GM_G, CH, CH), lambda i: (0, 0, 0)), pl.BlockSpec((GM_G, CH, 1), lambda i: (0, 0, 0))]


def _gmlp_fwd(name, proj, lng, lnb, ws, bs):
    T = proj.shape[0]

    def body(proj_ref, lng_ref, lnb_ref, ws_ref, bs_ref, o_ref):
        outs = _gmlp_math(*_gmlp_load(proj_ref, lng_ref, lnb_ref, ws_ref, bs_ref))
        for g in range(GM_G):
            o_ref[:, g * GM_GD:(g + 1) * GM_GD] = outs[g].astype(BF16)

    return pl.pallas_call(
        body, grid=(T // CH,), in_specs=[pl.BlockSpec((CH, 2 * D), lambda i: (i, 0))] + _GM_PAR(),
        out_specs=pl.BlockSpec((CH, D), lambda i: (i, 0)), out_shape=S((T, 2 * D), BF16),
        compiler_params=_params("parallel"), name=name)(proj, lng, lnb, ws, bs)


def _acc_store(first, ref, idx, val):
    @pl.when(first)
    def _():
        ref[idx] = val

    @pl.when(jnp.logical_not(first))
    def _():
        ref[idx] += val


def _gmlp_bwd(name, proj, lng, lnb, ws, bs, dmix, dproj):
    T = proj.shape[0]

    def body(proj_ref, lng_ref, lnb_ref, ws_ref, bs_ref, dmix_ref, _, dproj_ref, dlng_ref, dlnb_ref, dws_ref, dbs_ref):
        first = pl.program_id(0) == 0
        prim = _gmlp_load(proj_ref, lng_ref, lnb_ref, ws_ref, bs_ref)
        _, vjp = jax.vjp(_gmlp_math, *prim)
        du, dv, dlng, dlnb, dws, dbs = vjp([dmix_ref[:, g * GM_GD:(g + 1) * GM_GD].astype(F32) for g in range(GM_G)])
        for g in range(GM_G):
            sl = slice(g * GM_GD, (g + 1) * GM_GD)
            dproj_ref[:, sl] = du[g].astype(BF16)
            dproj_ref[:, D + g * GM_GD:D + (g + 1) * GM_GD] = dv[g].astype(BF16)
            _acc_store(first, dlng_ref, (slice(None), sl), dlng[g])
            _acc_store(first, dlnb_ref, (slice(None), sl), dlnb[g])
            _acc_store(first, dws_ref, g, dws[g])
            _acc_store(first, dbs_ref, g, dbs[g])

    par = _GM_PAR()
    return pl.pallas_call(
        body, grid=(T // CH,),
        in_specs=[pl.BlockSpec((CH, 2 * D), lambda i: (i, 0))] + par +
                 [pl.BlockSpec((CH, D), lambda i: (i, 0)), pl.BlockSpec(memory_space=pl.ANY)],
        out_specs=[pl.BlockSpec((CH, 2 * D), lambda i: (i, 0))] + par,
        out_shape=[S(dproj.shape, BF16), S((1, D), F32), S((1, D), F32), S((GM_G, CH, CH), F32), S((GM_G, CH, 1), F32)],
        input_output_aliases={6: 0}, compiler_params=_params("arbitrary"), name=name)(proj, lng, lnb, ws, bs, dmix, dproj)


CONV_TT = 256
HALO = 8


def _shift_rows(cur, halo_after, s):
    if s == 0:
        return cur
    n = cur.shape[0]
    return pltpu.roll(jnp.concatenate([cur, halo_after], 0), s, 0)[:n]


def _conv_fwd(name, proj, w, b):
    T = proj.shape[0]
    tt = _tile(T, CONV_TT)
    hb = tt // HALO

    def body(x_ref, halo_ref, w_ref, b_ref, y_ref, xc_ref):
        i = pl.program_id(0)
        x = x_ref[...].astype(F32)
        halo = halo_ref[...].astype(F32) * (i > 0).astype(F32)
        y = b_ref[...] + w_ref[3:4, :] * x
        for s in (1, 2, 3):
            y = y + w_ref[3 - s:4 - s, :] * _shift_rows(x, halo, s)
        y_ref[...] = y.astype(BF16)
        xc_ref[...] = (y * _sigmoid(y)).astype(BF16)

    o = pl.BlockSpec((tt, CONV_C), lambda i: (i, 0))
    return pl.pallas_call(
        body, grid=(T // tt,),
        in_specs=[pl.BlockSpec((tt, CONV_C), lambda i: (i, 2)),
                  pl.BlockSpec((HALO, CONV_C), lambda i: (jnp.maximum(i * hb - 1, 0), 2)),
                  pl.BlockSpec((4, CONV_C), lambda i: (0, 0)), pl.BlockSpec((1, CONV_C), lambda i: (0, 0))],
        out_specs=[o, o], out_shape=[S((T, CONV_C), BF16)] * 2,
        compiler_params=_params("parallel"), name=name)(proj, proj, w, b)


def _conv_bwd(name, proj, ypre, dxc, w, dproj):
    T = proj.shape[0]
    tt = _tile(T, CONV_TT)
    hb = tt // HALO
    nt = T // tt

    def dsilu(y):
        sg = _sigmoid(y)
        return sg * (1.0 + y * (1.0 - sg))

    def body(x_ref, xh_ref, y_ref, yn_ref, d_ref, dn_ref, w_ref, _, dproj_ref, dw_ref, db_ref):
        i = pl.program_id(0)
        first = i == 0
        x = x_ref[...].astype(F32)
        halo = xh_ref[...].astype(F32) * (i > 0).astype(F32)
        dy = d_ref[...].astype(F32) * dsilu(y_ref[...].astype(F32))
        dyn = dn_ref[...].astype(F32) * dsilu(yn_ref[...].astype(F32)) * (i < nt - 1).astype(F32)
        ext = jnp.concatenate([dy, dyn], 0)
        dx = w_ref[3:4, :] * dy
        _acc_store(first, dw_ref, (slice(3, 4), slice(None)), jnp.sum(x * dy, 0, keepdims=True))
        for s in (1, 2, 3):
            dx = dx + w_ref[3 - s:4 - s, :] * pltpu.roll(ext, tt + HALO - s, 0)[:tt]
            _acc_store(first, dw_ref, (slice(3 - s, 4 - s), slice(None)),
                       jnp.sum(_shift_rows(x, halo, s) * dy, 0, keepdims=True))
        _acc_store(first, db_ref, (slice(None), slice(None)), jnp.sum(dy, 0, keepdims=True))
        dproj_ref[...] = dx.astype(BF16)

    cur = pl.BlockSpec((tt, CONV_C), lambda i: (i, 0))
    nxt = pl.BlockSpec((HALO, CONV_C), lambda i: (jnp.minimum((i + 1) * hb, T // HALO - 1), 0))
    return pl.pallas_call(
        body, grid=(nt,),
        in_specs=[pl.BlockSpec((tt, CONV_C), lambda i: (i, 2)),
                  pl.BlockSpec((HALO, CONV_C), lambda i: (jnp.maximum(i * hb - 1, 0), 2)),
                  cur, nxt, cur, nxt, pl.BlockSpec((4, CONV_C), lambda i: (0, 0)), pl.BlockSpec(memory_space=pl.ANY)],
        out_specs=[pl.BlockSpec((tt, CONV_C), lambda i: (i, 2)), pl.BlockSpec((4, CONV_C), lambda i: (0, 0)),
                   pl.BlockSpec((1, CONV_C), lambda i: (0, 0))],
        out_shape=[S(dproj.shape, BF16), S((4, CONV_C), F32), S((1, CONV_C), F32)],
        input_output_aliases={7: 0}, compiler_params=_params("arbitrary"), name=name)(proj, proj, ypre, ypre, dxc, dxc, w, dproj)


def _softplus(x):
    return jnp.maximum(x, 0.0) + jnp.log(1.0 + jnp.exp(-jnp.abs(x)))


def _ssd_math(x, Bm, Cm, dtr, z, prev, dtb, alog, dsk, nrm):
    hi = lax.Precision.HIGHEST
    causal = _causal(CH)
    tri = causal.astype(F32)
    lane = lax.broadcasted_iota(jnp.int32, (1, LANE), 1)
    sub = lax.broadcasted_iota(jnp.int32, (LANE, 1), 0)
    dt = _softplus(dtr + dtb)
    a = dt * (-jnp.exp(alog))
    a_cs = jnp.dot(tri, a, preferred_element_type=F32, precision=hi)
    a_csT = lax.dot_general(a, tri, (((0,), (1,)), ((), ())), preferred_element_type=F32, precision=hi)
    a_last = jnp.sum(a, 0, keepdims=True)
    gw = HPG * HD
    outs, new = [], []
    for g in range(NG):
        spread = (lax.broadcasted_iota(jnp.int32, (LANE, gw), 0)
                  == g * HPG + lax.broadcasted_iota(jnp.int32, (LANE, gw), 1) // HD).astype(F32)
        to_lanes = lambda v: jnp.dot(v, spread, preferred_element_type=F32, precision=hi)
        col_e, dt_e, last_e, dsk_e = to_lanes(a_cs), to_lanes(dt), to_lanes(a_last), to_lanes(dsk)
        last_r = lax.dot_general(spread, a_last, (((0,), (1,)), ((), ())), preferred_element_type=F32, precision=hi)
        cb = lax.dot_general(Cm[g].astype(BF16), Bm[g].astype(BF16), _DIMS["nt"], preferred_element_type=F32)
        xg = jnp.concatenate(x[g * HPG:(g + 1) * HPG], 1)
        yd = []
        for h in range(g * HPG, (g + 1) * HPG):
            ohl = (lane == h).astype(F32)
            col = jnp.sum(a_cs * ohl, 1, keepdims=True)
            row = jnp.sum(a_csT * (sub == h).astype(F32), 0, keepdims=True)
            dtc = jnp.sum(dt * ohl, 1, keepdims=True)
            lmat = jnp.where(causal, jnp.exp(jnp.where(causal, col - row, 0.0)), 0.0)
            yd.append(jnp.dot((cb * lmat).astype(BF16), (x[h] * dtc).astype(BF16), preferred_element_type=F32))
        y = jnp.concatenate(yd, 1)
        y = y + jnp.exp(col_e) * lax.dot_general(Cm[g].astype(BF16), prev[g].astype(BF16), _DIMS["nt"],
                                                 preferred_element_type=F32)
        st = lax.dot_general((xg * dt_e * jnp.exp(last_e - col_e)).astype(BF16), Bm[g].astype(BF16), _DIMS["tn"],
                             preferred_element_type=F32)
        new.append(prev[g] * jnp.exp(last_r) + st)
        yg = (y + xg * dsk_e) * (z[g] * _sigmoid(z[g]))
        yg = yg * lax.rsqrt(jnp.mean(yg * yg, -1, keepdims=True) + EPS)
        outs.append(yg * nrm[g])
    return outs, new


def _ssd_load(xc_ref, dtr_ref, z_ref, state_ref, dtb_ref, alog_ref, dsk_ref, nrm_ref):
    gw = HPG * HD
    x = [xc_ref[:, h * HD:(h + 1) * HD].astype(F32) for h in range(NH)]
    Bm = [xc_ref[:, D + g * NS:D + (g + 1) * NS].astype(F32) for g in range(NG)]
    Cm = [xc_ref[:, D + NG * NS + g * NS:D + NG * NS + (g + 1) * NS].astype(F32) for g in range(NG)]
    z = [z_ref[:, g * gw:(g + 1) * gw].astype(F32) for g in range(NG)]
    prev = [state_ref[g * gw:(g + 1) * gw, :] for g in range(NG)]
    nrm = [nrm_ref[:, g * gw:(g + 1) * gw] for g in range(NG)]
    return x, Bm, Cm, dtr_ref[...], z, prev, dtb_ref[...], alog_ref[...], dsk_ref[...], nrm


_SSD_PAR = lambda: [pl.BlockSpec((1, LANE), lambda c: (0, 0))] * 3 + [pl.BlockSpec((1, D), lambda c: (0, 0))]


def _ssd_fwd(name, xc, dtr, proj, dtb, alog, dsk, nrm, mix):
    T = xc.shape[0]
    nc = T // CH

    def body(xc_ref, dtr_ref, z_ref, dtb_ref, alog_ref, dsk_ref, nrm_ref, _, mix_ref, prev_ref, state):
        @pl.when(pl.program_id(0) == 0)
        def _():
            state[...] = jnp.zeros_like(state)

        prev_ref[...] = state[...]
        outs, new = _ssd_math(*_ssd_load(xc_ref, dtr_ref, z_ref, state, dtb_ref, alog_ref, dsk_ref, nrm_ref))
        for g in range(NG):
            mix_ref[:, g * 512:(g + 1) * 512] = outs[g].astype(BF16)
            state[g * 512:(g + 1) * 512, :] = new[g]

    return pl.pallas_call(
        body, grid=(nc,),
        in_specs=[pl.BlockSpec((CH, CONV_C), lambda c: (c, 0)), pl.BlockSpec((CH, LANE), lambda c: (c, 0)),
                  pl.BlockSpec((CH, D), lambda c: (c, 2))] + _SSD_PAR() + [pl.BlockSpec(memory_space=pl.ANY)],
        out_specs=[pl.BlockSpec((CH, D), lambda c: (c, 1)), pl.BlockSpec((None, NH * HD, NS), lambda c: (c, 0, 0))],
        out_shape=[S(mix.shape, BF16), S((nc, NH * HD, NS), F32)],
        scratch_shapes=[pltpu.VMEM((NH * HD, NS), F32)], input_output_aliases={7: 0},
        compiler_params=_params("arbitrary"), name=name)(xc, dtr, proj, dtb, alog, dsk, nrm, mix)


def _ssd_bwd(name, xc, dtr, proj, prevs, dtb, alog, dsk, nrm, dmix, dproj):
    T = xc.shape[0]
    nc = T // CH
    rev = lambda c: nc - 1 - c

    def body(xc_ref, dtr_ref, z_ref, prev_ref, dtb_ref, alog_ref, dsk_ref, nrm_ref, dmix_ref, _,
             dproj_ref, dxc_ref, ddtr_ref, ddtb_ref, dalog_ref, ddsk_ref, dnrm_ref, dstate):
        first = pl.program_id(0) == 0

        @pl.when(first)
        def _():
            dstate[...] = jnp.zeros_like(dstate)

        prim = _ssd_load(xc_ref, dtr_ref, z_ref, prev_ref, dtb_ref, alog_ref, dsk_ref, nrm_ref)
        _, vjp = jax.vjp(_ssd_math, *prim)
        douts = [dmix_ref[:, g * 512:(g + 1) * 512].astype(F32) for g in range(NG)]
        dnew = [dstate[g * 512:(g + 1) * 512, :] for g in range(NG)]
        dx, dB, dC, ddtr, dz, dprev, ddtb, dalog, ddsk, dnrm = vjp((douts, dnew))
        for h in range(NH):
            dxc_ref[:, h * HD:(h + 1) * HD] = dx[h].astype(BF16)
        for g in range(NG):
            dstate[g * 512:(g + 1) * 512, :] = dprev[g]
            dxc_ref[:, D + g * NS:D + (g + 1) * NS] = dB[g].astype(BF16)
            dxc_ref[:, D + NG * NS + g * NS:D + NG * NS + (g + 1) * NS] = dC[g].astype(BF16)
            dproj_ref[:, g * 512:(g + 1) * 512] = dz[g].astype(BF16)
            _acc_store(first, dnrm_ref, (slice(None), slice(g * 512, (g + 1) * 512)), dnrm[g])
        ddtr_ref[...] = ddtr
        _acc_store(first, ddtb_ref, (slice(None), slice(None)), ddtb)
        _acc_store(first, dalog_ref, (slice(None), slice(None)), dalog)
        _acc_store(first, ddsk_ref, (slice(None), slice(None)), ddsk)

    vec = pl.BlockSpec((1, LANE), lambda c: (0, 0))
    return pl.pallas_call(
        body, grid=(nc,),
        in_specs=[pl.BlockSpec((CH, CONV_C), lambda c: (rev(c), 0)), pl.BlockSpec((CH, LANE), lambda c: (rev(c), 0)),
                  pl.BlockSpec((CH, D), lambda c: (rev(c), 2)),
                  pl.BlockSpec((None, NH * HD, NS), lambda c: (rev(c), 0, 0))] + _SSD_PAR() +
                 [pl.BlockSpec((CH, D), lambda c: (rev(c), 1)), pl.BlockSpec(memory_space=pl.ANY)],
        out_specs=[pl.BlockSpec((CH, D), lambda c: (rev(c), 2)), pl.BlockSpec((CH, CONV_C), lambda c: (rev(c), 0)),
                   pl.BlockSpec((CH, LANE), lambda c: (rev(c), 0)), vec, vec, vec, pl.BlockSpec((1, D), lambda c: (0, 0))],
        out_shape=[S(dproj.shape, BF16), S((T, CONV_C), BF16), S((T, LANE), F32), S((1, LANE), F32), S((1, LANE), F32),
                   S((1, LANE), F32), S((1, D), F32)],
        scratch_shapes=[pltpu.VMEM((NH * HD, NS), F32)], input_output_aliases={9: 0},
        compiler_params=_params("arbitrary"), name=name)(xc, dtr, proj, prevs, dtb, alog, dsk, nrm, dmix, dproj)


def _rope(x, c, s, sign):
    W = x.shape[1]
    reps = W // LANE
    C, Sg = jnp.tile(c, (1, reps)), jnp.tile(s, (1, reps))
    lane = lax.broadcasted_iota(jnp.int32, x.shape, 1) % AHD
    up, dn = pltpu.roll(x, W - ROT // 2, 1), pltpu.roll(x, ROT // 2, 1)
    sw = jnp.where(lane < ROT // 2, up, jnp.where(lane < ROT, dn, 0.0))
    return x * C + sign * sw * Sg


def _rope_fwd(name, qkv, cos, sin):
    T = qkv.shape[0]
    tt = _tile(T, 256)
    KV = AKV * AHD

    def body(x_ref, c_ref, s_ref, o_ref):
        c, s = c_ref[...], s_ref[...]
        o_ref[:, :D] = _rope(x_ref[:, :D], c, s, 1.0).astype(BF16)
        o_ref[:, D:D + KV] = _rope(x_ref[:, D:D + KV], c, s, 1.0).astype(BF16)
        o_ref[:, D + KV:] = x_ref[:, D + KV:].astype(BF16)

    tab = pl.BlockSpec((tt, LANE), lambda i: (i, 0))
    return pl.pallas_call(
        body, grid=(T // tt,), in_specs=[pl.BlockSpec((tt, ODD_IN), lambda i: (i, 0)), tab, tab],
        out_specs=pl.BlockSpec((tt, ODD_IN), lambda i: (i, 0)), out_shape=S((T, ODD_IN), BF16),
        compiler_params=_params("parallel"), name=name)(qkv, cos, sin)


def _rope_bwd(name, dq, dkv_cur, dkv_prev, cos, sin):
    T = dq.shape[0]
    nb = T // CH
    KV = AKV * AHD

    def body(dq_ref, cur_ref, nxt_ref, c_ref, s_ref, o_ref, db_ref):
        n = pl.program_id(0)
        c, s = c_ref[...], s_ref[...]
        dkv = cur_ref[...] + nxt_ref[...] * (n < nb - 1).astype(F32)
        o_ref[:, :D] = _rope(dq_ref[...].astype(F32), c, s, -1.0).astype(BF16)
        o_ref[:, D:D + KV] = _rope(dkv[:, :KV], c, s, -1.0).astype(BF16)
        o_ref[:, D + KV:] = dkv[:, KV:].astype(BF16)
        _acc_store(n == 0, db_ref, (slice(None), slice(None)), jnp.sum(o_ref[...].astype(F32), 0, keepdims=True))

    tab = pl.BlockSpec((CH, LANE), lambda n: (n, 0))
    return pl.pallas_call(
        body, grid=(nb,),
        in_specs=[pl.BlockSpec((CH, D), lambda n: (n, 0)), pl.BlockSpec((CH, 2 * KV), lambda n: (n, 0)),
                  pl.BlockSpec((CH, 2 * KV), lambda n: (jnp.minimum(n + 1, nb - 1), 0)), tab, tab],
        out_specs=[pl.BlockSpec((CH, ODD_IN), lambda n: (n, 0)), pl.BlockSpec((1, ODD_IN), lambda n: (0, 0))],
        out_shape=[S((T, ODD_IN), BF16), S((1, ODD_IN), F32)],
        compiler_params=_params("arbitrary"), name=name)(dq, dkv_cur, dkv_prev, cos, sin)


def _swa_math(q, kp, kc, vp, vc, snk, mask):
    outs = []
    for k in range(AKV):
        K = jnp.concatenate([kp[k], kc[k]], 0).astype(BF16)
        V = jnp.concatenate([vp[k], vc[k]], 0).astype(BF16)
        s = lax.dot_general(q[k].astype(BF16), K, _DIMS["nt"], preferred_element_type=F32) * ATT_SCALE
        s = jnp.where(mask, s, -jnp.inf)
        m = lax.stop_gradient(jnp.maximum(jnp.max(s, -1, keepdims=True), snk[k]))
        p = jnp.exp(s - m)
        pr = p / (jnp.sum(p, -1, keepdims=True) + jnp.exp(snk[k] - m))
        outs.append(jnp.dot(pr.astype(BF16), V, preferred_element_type=F32))
    return outs


def _stack_heads(ref, k):
    return jnp.concatenate([ref[:, (k * AREP + r) * AHD:(k * AREP + r + 1) * AHD].astype(F32) for r in range(AREP)], 0)


def _swa_load(q_ref, cur_ref, prv_ref, snk_ref):
    KV = AKV * AHD
    q = [_stack_heads(q_ref, k) for k in range(AKV)]
    kc = [cur_ref[:, k * AHD:(k + 1) * AHD].astype(F32) for k in range(AKV)]
    vc = [cur_ref[:, KV + k * AHD:KV + (k + 1) * AHD].astype(F32) for k in range(AKV)]
    kp = [prv_ref[:, k * AHD:(k + 1) * AHD].astype(F32) for k in range(AKV)]
    vp = [prv_ref[:, KV + k * AHD:KV + (k + 1) * AHD].astype(F32) for k in range(AKV)]
    snk = [jnp.concatenate([jnp.broadcast_to(snk_ref[:, k * AREP + r:k * AREP + r + 1], (CH, 1)) for r in range(AREP)], 0)
           for k in range(AKV)]
    return q, kp, kc, vp, vc, snk


def _swa_mask(n):
    iq = lax.broadcasted_iota(jnp.int32, (AREP * CH, 2 * CH), 0) % CH
    js = lax.broadcasted_iota(jnp.int32, (AREP * CH, 2 * CH), 1)
    rel = iq + CH - js
    return (rel >= 0) & (rel < CH) & ((n > 0) | (js >= CH))


def _swa_specs(T):
    KV = AKV * AHD
    return [pl.BlockSpec((CH, D), lambda n: (n, 0)), pl.BlockSpec((CH, 2 * KV), lambda n: (n, D // (2 * KV))),
            pl.BlockSpec((CH, 2 * KV), lambda n: (jnp.maximum(n - 1, 0), D // (2 * KV))),
            pl.BlockSpec((1, LANE), lambda n: (0, 0))]


def _swa_fwd(name, qkvr, snk):
    T = qkvr.shape[0]

    def body(q_ref, cur_ref, prv_ref, snk_ref, o_ref):
        outs = _swa_math(*_swa_load(q_ref, cur_ref, prv_ref, snk_ref), _swa_mask(pl.program_id(0)))
        for h in range(AH):
            k, r = divmod(h, AREP)
            o_ref[:, h * AHD:(h + 1) * AHD] = outs[k][r * CH:(r + 1) * CH].astype(BF16)

    return pl.pallas_call(
        body, grid=(T // CH,), in_specs=_swa_specs(T), out_specs=pl.BlockSpec((CH, D), lambda n: (n, 0)),
        out_shape=S((T, D), BF16), compiler_params=_params("parallel"), name=name)(qkvr, qkvr, qkvr, snk)


def _swa_bwd(name, qkvr, snk, do):
    T = qkvr.shape[0]
    KV = AKV * AHD

    def body(q_ref, cur_ref, prv_ref, snk_ref, do_ref, dq_ref, dcur_ref, dprv_ref, dsnk_ref):
        n = pl.program_id(0)

        @pl.when(n == 0)
        def _():
            dsnk_ref[...] = jnp.zeros_like(dsnk_ref)

        prim = _swa_load(q_ref, cur_ref, prv_ref, snk_ref)
        mask = _swa_mask(n)
        _, vjp = jax.vjp(lambda *p: _swa_math(*p, mask), *prim)
        dq, dkp, dkc, dvp, dvc, dsnk = vjp([_stack_heads(do_ref, k) for k in range(AKV)])
        for h in range(AH):
            k, r = divmod(h, AREP)
            dq_ref[:, h * AHD:(h + 1) * AHD] = dq[k][r * CH:(r + 1) * CH].astype(BF16)
            dsnk_ref[:, h:h + 1] += jnp.sum(dsnk[k][r * CH:(r + 1) * CH], 0, keepdims=True)
        for k in range(AKV):
            dcur_ref[:, k * AHD:(k + 1) * AHD] = dkc[k]
            dcur_ref[:, KV + k * AHD:KV + (k + 1) * AHD] = dvc[k]
            dprv_ref[:, k * AHD:(k + 1) * AHD] = dkp[k]
            dprv_ref[:, KV + k * AHD:KV + (k + 1) * AHD] = dvp[k]

    kv = pl.BlockSpec((CH, 2 * KV), lambda n: (n, 0))
    return pl.pallas_call(
        body, grid=(T // CH,), in_specs=_swa_specs(T) + [pl.BlockSpec((CH, D), lambda n: (n, 0))],
        out_specs=[pl.BlockSpec((CH, D), lambda n: (n, 0)), kv, kv, pl.BlockSpec((1, LANE), lambda n: (0, 0))],
        out_shape=[S((T, D), BF16), S((T, 2 * KV), F32), S((T, 2 * KV), F32), S((1, LANE), F32)],
        compiler_params=_params("arbitrary"), name=name)(qkvr, qkvr, qkvr, snk, do)


def _xat_math(q, k, v):
    outs = []
    for h in range(XH):
        s = lax.dot_general(q[h].astype(BF16), k[h].astype(BF16), _DIMS["nt"], preferred_element_type=F32) * X_SCALE
        m = lax.stop_gradient(jnp.max(s, -1, keepdims=True))
        p = jnp.exp(s - m)
        pr = p / jnp.sum(p, -1, keepdims=True)
        outs.append(jnp.dot(pr.astype(BF16), v[h].astype(BF16), preferred_element_type=F32))
    return outs


def _xat_load(q_ref, kv_ref):
    q = [q_ref[:, h * XHD:(h + 1) * XHD].astype(F32) for h in range(XH)]
    k = [kv_ref[:, h * XHD:(h + 1) * XHD].astype(F32) for h in range(XH)]
    v = [kv_ref[:, XW + h * XHD:XW + (h + 1) * XHD].astype(F32) for h in range(XH)]
    return q, k, v


def _xat_fwd(name, q, kv):
    T, M = q.shape[0], kv.shape[0]
    tt = _tile(T, 512)

    def body(q_ref, kv_ref, o_ref):
        outs = _xat_math(*_xat_load(q_ref, kv_ref))
        for h in range(XH):
            o_ref[:, h * XHD:(h + 1) * XHD] = outs[h].astype(BF16)

    return pl.pallas_call(
        body, grid=(T // tt,),
        in_specs=[pl.BlockSpec((tt, XW), lambda i: (i, 0)), pl.BlockSpec((M, 2 * XW), lambda i: (0, 0))],
        out_specs=pl.BlockSpec((tt, XW), lambda i: (i, 0)), out_shape=S((T, XW), BF16),
        compiler_params=_params("parallel"), name=name)(q, kv)


def _xat_bwd(name, q, kv, do):
    T, M = q.shape[0], kv.shape[0]
    tt = _tile(T, 512)

    def body(q_ref, kv_ref, do_ref, dq_ref, dkv_ref):
        first = pl.program_id(0) == 0
        _, vjp = jax.vjp(_xat_math, *_xat_load(q_ref, kv_ref))
        dq, dk, dv = vjp([do_ref[:, h * XHD:(h + 1) * XHD].astype(F32) for h in range(XH)])
        for h in range(XH):
            sl = slice(h * XHD, (h + 1) * XHD)
            dq_ref[:, sl] = dq[h].astype(BF16)
            _acc_store(first, dkv_ref, (slice(None), sl), dk[h])
            _acc_store(first, dkv_ref, (slice(None), slice(XW + h * XHD, XW + (h + 1) * XHD)), dv[h])

    qs = pl.BlockSpec((tt, XW), lambda i: (i, 0))
    kvs = pl.BlockSpec((M, 2 * XW), lambda i: (0, 0))
    return pl.pallas_call(
        body, grid=(T // tt,), in_specs=[qs, kvs, qs], out_specs=[qs, kvs],
        out_shape=[S((T, XW), BF16), S((M, 2 * XW), F32)],
        compiler_params=_params("arbitrary"), name=name)(q, kv, do)


def _loss_head(name, x, gain, target):
    T = x.shape[0]
    tt = _tile(T, 512)

    def body(x_ref, g_ref, t_ref, l_ref, dx_ref, dxb_ref, dg_ref):
        first = pl.program_id(0) == 0
        xv, g = x_ref[...], g_ref[...]
        r = lax.rsqrt(jnp.mean(xv * xv, -1, keepdims=True) + EPS)
        xh = xv * r
        e = xh * g - t_ref[...]
        part = 0.5 * jnp.sum(jnp.mean(e * e, -1, keepdims=True), (0, 1), keepdims=True)
        _acc_store(first, l_ref, (slice(None), slice(None)), jnp.broadcast_to(part, (1, LANE)))
        dy = e * (1.0 / D)
        dxh = dy * g
        dx = r * (dxh - xh * jnp.mean(dxh * xh, -1, keepdims=True))
        dx_ref[...] = dx
        dxb_ref[...] = dx.astype(BF16)
        _acc_store(first, dg_ref, (slice(None), slice(None)), jnp.sum(dy * xh, 0, keepdims=True))

    row = pl.BlockSpec((tt, D), lambda i: (i, 0))
    vec = pl.BlockSpec((1, D), lambda i: (0, 0))
    return pl.pallas_call(
        body, grid=(T // tt,), in_specs=[row, vec, row],
        out_specs=[pl.BlockSpec((1, LANE), lambda i: (0, 0)), row, row, vec],
        out_shape=[S((1, LANE), F32), S((T, D), F32), S((T, D), BF16), S((1, D), F32)],
        compiler_params=_params("arbitrary"), name=name)(x, gain, target)


_FFN_TILES = {
    "l0_ffn1": dict(up=(512, 1408), down=(1024, 1024, 1408), dact=512, dwd=(1408, 1024, 2048), dh=(1024, 1024, 2816),
                    dwgu=(1024, 1408, 2048)),
    "l0_ffn2": dict(up=(1024, 256), down=(512, 1024, 5632), dact=1024, dwd=(1408, 1024, 1024), dh=(512, 2048, 2816),
                    dwgu=(1024, 1408, 1024)),
    "l1_ffn1": dict(up=(256, 1408), down=(512, 2048, 2816), dact=256, dwd=(1408, 2048, 1024), dh=(1024, 2048, 1408),
                    dwgu=(1024, 256, 8192)),
    "l1_ffn2": dict(up=(512, 1408), down=(1024, 1024, 2816), dact=512, dwd=(1408, 512, 2048), dh=(1024, 1024, 1408),
                    dwgu=(2048, 1408, 1024)),
}


def _tiles(t):
    return dict(tm_t=t[0], tn_t=t[1], tk_t=t[2])


def _ffn_fwd(tag, x, gain, wgu4, get_wd):
    cfg = _FFN_TILES[tag]
    h = _rms_fwd(f"{tag}_norm", x, gain)
    g, u, a = _ffn_up(f"{tag}_up", h, wgu4, 0, *cfg["up"])
    wd = get_wd(a).reshape(1, 1, DFF, D)
    x_new = _mm(f"{tag}_down", "nn", a, Op(wd, "r"), F32, res=x, scale=0.5, **_tiles(cfg["down"]))
    return x_new, (x, gain, h, g, u, a)


def _ffn_bwd(tag, saved, dx, dxb, wgu4, wd4, put):
    cfg = _FFN_TILES[tag]
    x, gain, h, g, u, a = saved
    dgu = _ffn_dact(f"{tag}_dact", dxb, wd4, 0, g, u, cfg["dact"])
    dwd = _mm(f"{tag}_dwd", "tn", a, dxb, BF16, out=("r", 4, 1, 0), scale=0.5, **_tiles(cfg["dwd"]))
    dwgu = _mm(f"{tag}_dwgu", "tn", h, Op(dgu, "c"), BF16, out=("c", 4, 1, 0), **_tiles(cfg["dwgu"]))
    tok = put(dwgu, dwd)
    dh = _mm(f"{tag}_dh", "nt", Op(dgu, "c"), Op(wgu4, "c"), BF16, bias=jnp.zeros((1, D), F32) + tok, **_tiles(cfg["dh"]))
    dx, dxb, dgain = _rms_bwd(f"{tag}_dnorm", x, gain, dh, dx)
    return dx, dxb, dgain


def _xattn_fwd(tag, x, mem, gq, gm, wxq4, wxkv4, wxo4):
    hq = _rms_fwd(f"{tag}_normq", x, gq)
    mn = _rms_fwd(f"{tag}_normm", mem, gm)
    q = _mm(f"{tag}_q", "nn", hq, Op(wxq4, "r"), BF16)
    kv = _mm(f"{tag}_kv", "nn", mn, Op(wxkv4, "r"), BF16)
    o = _xat_fwd(f"{tag}_att", q, kv)
    x_new = _mm(f"{tag}_o", "nn", o, Op(wxo4, "c"), F32, res=x)
    return x_new, (x, mem, gq, gm, hq, mn, q, kv, o)


def _xattn_bwd(tag, saved, dx, dxb, wxq4, wxkv4, wxo4, put):
    x, mem, gq, gm, hq, mn, q, kv, o = saved
    dwxo = _mm(f"{tag}_dwo", "tn", o, dxb, BF16, out=("c", 4, 1, 0))
    do = _mm(f"{tag}_do", "nt", dxb, Op(wxo4, "c"), BF16)
    dq, dkv = _xat_bwd(f"{tag}_datt", q, kv, do)
    dwxq = _mm(f"{tag}_dwq", "tn", hq, dq, BF16, out=("r", 4, 1, 0))
    dwxkv = _mm(f"{tag}_dwkv", "tn", mn, dkv, BF16, out=("r", 4, 1, 0))
    tok = put(dwxq, dwxkv, dwxo)
    dhq = _mm(f"{tag}_dhq", "nt", dq, Op(wxq4, "r"), BF16, bias=jnp.zeros((1, D), F32) + tok)
    dmn = _mm(f"{tag}_dmn", "nt", dkv, Op(wxkv4, "r"), BF16)
    _, _, dgm = _rms_bwd(f"{tag}_dnormm", mem, gm, dmn)
    dx, dxb, dgq = _rms_bwd(f"{tag}_dnormq", x, gq, dhq, dx)
    return dx, dxb, dgq, dgm


def _even_fwd(tag, x, gain, w_main, w_dt, p, wout4):
    h = _rms_fwd(f"{tag}_norm", x, gain)
    proj = _mm(f"{tag}_in", "nn", h, w_main, BF16)
    dtr = _mm(f"{tag}_indt", "nn", h, w_dt, F32)
    mix = _gmlp_fwd(f"{tag}_gmlp", proj, p["lng"], p["lnb"], p["ws"], p["bs"])
    ypre, xc = _conv_fwd(f"{tag}_conv", proj, p["cw"], p["cb"])
    mix, prevs = _ssd_fwd(f"{tag}_ssd", xc, dtr, proj, p["dtb"], p["alog"], p["dsk"], p["nrm"], mix)
    x_new = _mm(f"{tag}_out", "nn", mix, Op(wout4, "r", 0), F32, res=x)
    return x_new, (x, gain, h, proj, dtr, mix, ypre, xc, prevs)


def _even_bwd(tag, saved, dx, dxb, w_main, w_dt, p, wout4, put):
    x, gain, h, proj, dtr, mix, ypre, xc, prevs = saved
    T = x.shape[0]
    dwout = _mm(f"{tag}_dwout", "tn", mix, dxb, BF16, out=("r", 4, 1, 0))
    dmix = _mm(f"{tag}_dmix", "nt", dxb, Op(wout4, "r", 0), BF16)
    dproj = lax.empty((T, EVEN_MAIN), BF16)
    dproj, dlng, dlnb, dws, dbs = _gmlp_bwd(f"{tag}_dgmlp", proj, p["lng"], p["lnb"], p["ws"], p["bs"], dmix, dproj)
    dproj, dxc, ddtr, ddtb, dalog, ddsk, dnrm = _ssd_bwd(
        f"{tag}_dssd", xc, dtr, proj, prevs, p["dtb"], p["alog"], p["dsk"], p["nrm"], dmix, dproj)
    dproj, dcw, dcb = _conv_bwd(f"{tag}_dconv", proj, ypre, dxc, p["cw"], dproj)
    dw_main = _mm(f"{tag}_dwin", "tn", h, dproj, BF16)
    dw_dt = _mm(f"{tag}_dwdt", "tn", h, ddtr, BF16)
    tok = put(dw_main, dw_dt, dwout)
    dh = _mm(f"{tag}_dh1", "nt", ddtr, w_dt + tok.astype(BF16), F32)
    dh = _mm(f"{tag}_dh2", "nt", dproj, w_main, BF16, res=dh)
    dx, dxb, dgain = _rms_bwd(f"{tag}_dnorm", x, gain, dh, dx)
    small = dict(lng=dlng, lnb=dlnb, ws=dws, bs=dbs, cw=dcw, cb=dcb, dtb=ddtb, alog=dalog, dsk=ddsk, nrm=dnrm)
    return dx, dxb, dgain, small


def _odd_fwd(tag, x, gain, wqkv4, bqkv, snk, wo4, cos, sin):
    h = _rms_fwd(f"{tag}_norm", x, gain)
    qkv = _mm(f"{tag}_qkv", "nn", h, Op(wqkv4, "c", 0), F32, bias=bqkv, tn_t=640)
    qkvr = _rope_fwd(f"{tag}_rope", qkv, cos, sin)
    o = _swa_fwd(f"{tag}_swa", qkvr, snk)
    x_new = _mm(f"{tag}_o", "nn", o, Op(wo4, "r", 0), F32, res=x)
    return x_new, (x, gain, h, qkvr, o)


def _odd_bwd(tag, saved, dx, dxb, wqkv4, snk, wo4, cos, sin, put):
    x, gain, h, qkvr, o = saved
    dwo = _mm(f"{tag}_dwo", "tn", o, dxb, BF16, out=("r", 4, 1, 0))
    do = _mm(f"{tag}_do", "nt", dxb, Op(wo4, "r", 0), BF16)
    dq, dcur, dprv, dsnk = _swa_bwd(f"{tag}_dswa", qkvr, snk, do)
    dqkv, dbias = _rope_bwd(f"{tag}_drope", dq, dcur, dprv, cos, sin)
    dwqkv = _mm(f"{tag}_dwqkv", "tn", h, dqkv, BF16, out=("c", 4, 1, 0), tn_t=640)
    tok = put(dwqkv, dwo)
    dh = _mm(f"{tag}_dh", "nt", dqkv, Op(wqkv4, "c", 0), BF16, bias=jnp.zeros((1, D), F32) + tok, tk_t=640)
    dx, dxb, dgain = _rms_bwd(f"{tag}_dnorm", x, gain, dh, dx)
    return dx, dxb, dgain, dbias, dsnk


def _row(v):
    return v.reshape(1, -1).astype(F32)


def _pad_lanes(v, n=LANE):
    v = v.reshape(1, -1).astype(F32)
    return jnp.pad(v, ((0, 0), (0, n - v.shape[1])))


def _local_step(x, mem, positions, target, getw, P, putg):
    inv_freq = ROPE_THETA ** (-jnp.arange(0, ROT, 2, dtype=F32) / ROT)
    ang = positions.astype(F32)[:, None] * inv_freq
    cos8, sin8 = jnp.cos(ang), jnp.sin(ang)
    ones, zeros = jnp.ones((x.shape[0], AHD - ROT), F32), jnp.zeros((x.shape[0], AHD - ROT), F32)
    cos = jnp.tile(jnp.concatenate([cos8, cos8, ones], 1), (1, 2))
    sin = jnp.tile(jnp.concatenate([-sin8, sin8, zeros], 1), (1, 2))

    snk = _pad_lanes(P["sinks"])
    W = {}

    def w(name, layer, after):
        if (name, layer) not in W:
            W[name, layer] = getw(name, layer, after)
        return W[name, layer]

    saved = []
    for i in range(2):
        x, s1 = _ffn_fwd(f"l{i}_ffn1", x, _row(P["norm_ffn1"][i]), w("w_ffn1_gu", i, x),
                         functools.partial(w, "w_ffn1_down", i))
        if i == 0:
            ev = dict(lng=_row(P["gm_ln_g"]), lnb=_row(P["gm_ln_b"]), ws=P["gm_ws"].reshape(GM_G, CH, CH),
                      bs=P["gm_bs"].reshape(GM_G, CH, 1), cw=w("conv_w", 0, x), cb=_row(P["conv_b"]),
                      dtb=_pad_lanes(P["dt_bias"]), alog=_pad_lanes(P["a_log"]), dsk=_pad_lanes(P["d_skip"]),
                      nrm=_row(P["ssd_norm"]))
            w_main, w_dt = w("w_in_even", 0, x)
            x, s2 = _even_fwd("l0_mix", x, _row(P["norm_mix"][0]), w_main, w_dt, ev, w("w_out_even", 0, x))
        else:
            x, s2 = _odd_fwd("l1_mix", x, _row(P["norm_mix"][1]), w("w_qkv", 0, x), w("b_qkv", 0, x), snk,
                             w("w_o_odd", 0, x), cos, sin)
        x, s3 = _xattn_fwd(f"l{i}_xat", x, mem, _row(P["norm_xq"][i]), _row(P["norm_mem"][i]),
                           w("w_xq", i, x), w("w_xkv", i, x), w("w_xo", i, x))
        x, s4 = _ffn_fwd(f"l{i}_ffn2", x, _row(P["norm_ffn2"][i]), w("w_ffn2_gu", i, x),
                         functools.partial(w, "w_ffn2_down", i))
        saved.append((s1, s2, s3, s4))

    loss, dx, dxb, d_final = _loss_head("loss_head", x, _row(P["final_norm"]), target)

    sm = {}
    dn = {k: [None, None] for k in ("norm_ffn1", "norm_mix", "norm_xq", "norm_mem", "norm_ffn2")}
    for i in (1, 0):
        s1, s2, s3, s4 = saved[i]
        dx, dxb, dn["norm_ffn2"][i] = _ffn_bwd(
            f"l{i}_ffn2", s4, dx, dxb, W["w_ffn2_gu", i], W["w_ffn2_down", i],
            lambda dwgu, dwd, i=i: putg({("w_ffn2_gu", i): dwgu, ("w_ffn2_down", i): dwd}))
        dx, dxb, dn["norm_xq"][i], dn["norm_mem"][i] = _xattn_bwd(
            f"l{i}_xat", s3, dx, dxb, W["w_xq", i], W["w_xkv", i], W["w_xo", i],
            lambda dwxq, dwxkv, dwxo, i=i: putg({("w_xq", i): dwxq, ("w_xkv", i): dwxkv, ("w_xo", i): dwxo}))
        if i == 0:
            dx, dxb, dn["norm_mix"][0], sm_even = _even_bwd(
                "l0_mix", s2, dx, dxb, w_main, w_dt, ev, W["w_out_even", 0],
                lambda dw_main, dw_dt, dwout: putg({("w_in_even", 0): (dw_main, dw_dt), ("w_out_even", 0): dwout}))
        else:
            dx, dxb, dn["norm_mix"][1], sm["b_qkv"], sm["sinks"] = _odd_bwd(
                "l1_mix", s2, dx, dxb, W["w_qkv", 0], snk, W["w_o_odd", 0], cos, sin,
                lambda dwqkv, dwo: putg({("w_qkv", 0): dwqkv, ("w_o_odd", 0): dwo}))
        dx, dxb, dn["norm_ffn1"][i] = _ffn_bwd(
            f"l{i}_ffn1", s1, dx, dxb, W["w_ffn1_gu", i], W["w_ffn1_down", i],
            lambda dwgu, dwd, i=i: putg({("w_ffn1_gu", i): dwgu, ("w_ffn1_down", i): dwd}))
    for k, v in dn.items():
        sm[k] = jnp.concatenate(v, 0)
    sm.update(gm_ln_g=sm_even["lng"], gm_ln_b=sm_even["lnb"], gm_ws=sm_even["ws"], gm_bs=sm_even["bs"],
              conv_w=sm_even["cw"], conv_b=sm_even["cb"], dt_bias=sm_even["dtb"][:, :NH], a_log=sm_even["alog"][:, :NH],
              d_skip=sm_even["dsk"][:, :NH], ssd_norm=sm_even["nrm"], sinks=sm["sinks"][:, :AH], final_norm=d_final)
    return loss[0, 0], dx, sm


def _chip_peers():
    x, y, c = lax.axis_index("x"), lax.axis_index("y"), lax.axis_index("c")
    return 2 * x + y, [((1 - x, y, c), 2 * (1 - x) + y), ((x, 1 - y, c), 2 * x + (1 - y)),
                       ((1 - x, 1 - y, c), 2 * (1 - x) + (1 - y))]


def _any_specs(n):
    return [pl.BlockSpec(memory_space=pl.ANY)] * n


_HBM = pl.BlockSpec(memory_space=pltpu.HBM)
_SEM = pl.BlockSpec(memory_space=pltpu.SEMAPHORE)
_EFFECT = pltpu.SideEffectType.DATAFLOW_SIDE_EFFECTING


def _own_slot(piece, chip):
    zone = lax.empty((4,) + piece.shape, piece.dtype)
    return lax.dynamic_update_slice(zone, piece[None], (chip,) + (0,) * piece.ndim)


def _chip_copies(srcs, lands, ssems, rsems):
    me, peers = _chip_peers()
    return [pltpu.make_async_remote_copy(src_ref=lands[i].at[me] if srcs[i] is None else srcs[i].at[chip],
                                         dst_ref=lands[i].at[me], send_sem=ssems[i].at[j], recv_sem=rsems[i].at[j],
                                         device_id=dev, device_id_type=MESH)
            for i in range(len(lands)) for j, (dev, chip) in enumerate(peers)]


def _exchange_start(name, srcs, lands):
    n = len(lands)
    ns = 0 if srcs is None else n

    def body(*refs):
        src_refs = [None] * n if srcs is None else refs[:n]
        land_refs = refs[ns:ns + n]
        ssems, rsems = refs[ns + n:ns + 2 * n], refs[ns + 2 * n:ns + 3 * n]
        token = refs[2 * ns + 4 * n]
        for cp in _chip_copies(src_refs, land_refs, ssems, rsems):
            cp.start()
        token[...] = jnp.zeros_like(token)

    ins = ([] if srcs is None else list(srcs)) + list(lands)
    res = pl.pallas_call(
        body, name=name,
        out_shape=[pltpu.SemaphoreType.DMA((3,))] * (2 * n) + [pltpu.HBM(a.shape, a.dtype) for a in ins] + [S((8, LANE), F32)],
        in_specs=[_HBM] * (ns + n),
        out_specs=[_SEM] * (2 * n) + [_HBM] * (ns + n) + [pl.BlockSpec(memory_space=pltpu.VMEM)],
        input_output_aliases={i: 2 * n + i for i in range(ns + n)},
        compiler_params=pltpu.CompilerParams(has_side_effects=_EFFECT),
    )(*[pltpu.with_memory_space_constraint(a, pltpu.HBM) for a in ins])
    items = [(res[i], res[n + i], None if srcs is None else res[2 * n + i], res[2 * n + ns + i]) for i in range(n)]
    return items, res[2 * n + ns + n][0, 0]


def _exchange_wait(name, item, after):
    ssem, rsem, src, land = item
    ns = 0 if src is None else 1

    def body(*refs):
        src_ref = refs[0] if ns else None
        land_ref, ssem_ref, rsem_ref = refs[ns], refs[ns + 1], refs[ns + 2]
        for cp in _chip_copies([src_ref], [land_ref], [ssem_ref], [rsem_ref]):
            cp.wait_send()
            cp.wait_recv()

    ins = ([src] if ns else []) + [land]
    return pl.pallas_call(
        body, name=name, out_shape=[pltpu.HBM(a.shape, a.dtype) for a in ins],
        in_specs=[_HBM] * (ns + 1) + [_SEM, _SEM, pl.BlockSpec(memory_space=pl.ANY)], out_specs=[_HBM] * (ns + 1),
        input_output_aliases={i: i for i in range(ns + 1)}, compiler_params=pltpu.CompilerParams(has_side_effects=_EFFECT),
    )(*ins, ssem, rsem, after)[ns]


def _swap_sibling(name, arrs):
    n = len(arrs)

    def body(*refs):
        ins, outs = refs[:n], refs[n:2 * n]
        ssem, rsem = refs[2 * n:]
        dev = (lax.axis_index("x"), lax.axis_index("y"), 1 - lax.axis_index("c"))
        copies = [pltpu.make_async_remote_copy(src_ref=ins[i], dst_ref=outs[i], send_sem=ssem.at[i], recv_sem=rsem.at[i],
                                               device_id=dev, device_id_type=MESH) for i in range(n)]
        for cp in copies:
            cp.start()
        for cp in copies:
            cp.wait()

    return pl.pallas_call(
        body, in_specs=_any_specs(n), out_specs=_any_specs(n), out_shape=[S(a.shape, a.dtype) for a in arrs],
        scratch_shapes=[pltpu.SemaphoreType.DMA((n,)), pltpu.SemaphoreType.DMA((n,))],
        compiler_params=pltpu.CompilerParams(has_side_effects=True), name=name)(*arrs)


def _gather_all(name, v, after):
    def body(v_ref, _, o_ref, ssem, rsem, lsem):
        x, y, c = lax.axis_index("x"), lax.axis_index("y"), lax.axis_index("c")
        me = 4 * x + 2 * y + c
        loc = pltpu.make_async_copy(v_ref, o_ref.at[me], lsem)
        loc.start()
        copies = []
        for k in range(1, 8):
            fx, fy, fc = (k >> 2) & 1, (k >> 1) & 1, k & 1
            dev = (x ^ fx, y ^ fy, c ^ fc)
            cp = pltpu.make_async_remote_copy(src_ref=v_ref, dst_ref=o_ref.at[me], send_sem=ssem.at[k - 1],
                                              recv_sem=rsem.at[k - 1], device_id=dev, device_id_type=MESH)
            cp.start()
            copies.append(cp)
        for cp in copies:
            cp.wait()
        loc.wait()

    return pl.pallas_call(
        body, in_specs=_any_specs(2), out_specs=pl.BlockSpec(memory_space=pl.ANY), out_shape=S((8,) + v.shape, v.dtype),
        scratch_shapes=[pltpu.SemaphoreType.DMA((7,)), pltpu.SemaphoreType.DMA((7,)), pltpu.SemaphoreType.DMA(())],
        compiler_params=pltpu.CompilerParams(has_side_effects=True), name=name)(v, after)


def _row_tile(R, row_bytes, budget=4 << 20):
    if R * row_bytes <= budget or R % 16:
        return R
    t = max(16, budget // row_bytes // 16 * 16)
    while R % t:
        t -= 16
    return t


def _sum_slots(name, r, n):
    _, R, C = r.shape
    tr = _row_tile(R, C * (n * r.dtype.itemsize + 4))

    def body(r_ref, o_ref):
        acc = r_ref[0].astype(F32)
        for j in range(1, n):
            acc = acc + r_ref[j].astype(F32)
        o_ref[...] = acc

    return pl.pallas_call(
        body, grid=(R // tr,), in_specs=[pl.BlockSpec((n, tr, C), lambda i: (0, i, 0))],
        out_specs=pl.BlockSpec((tr, C), lambda i: (i, 0)), out_shape=S((R, C), F32),
        compiler_params=_params("parallel"), name=name)(r)


def _adamw(name, w, m, v, layer, g1, g2=None, into=None):
    nl, R, C = w.shape
    tr = _row_tile(R, C * 4 * 9)
    two, has_into = g2 is not None, into is not None

    def body(w_ref, m_ref, v_ref, g1_ref, *rest):
        rest = list(rest)
        g = g1_ref[...]
        if two:
            g = g + rest.pop(0)[...]
        g_ref, d_ref, nm_ref, nv_ref = rest[-4:]
        mn = B1 * m_ref[...] + (1.0 - B1) * g
        vn = B2 * v_ref[...] + (1.0 - B2) * jnp.square(g)
        m_hat = mn / (1.0 - B1 ** STEP)
        v_hat = vn / (1.0 - B2 ** STEP)
        g_ref[...] = g
        d_ref[...] = -LR * (m_hat / (jnp.sqrt(v_hat) + AEPS) + WD * w_ref[...])
        nm_ref[...] = mn
        nv_ref[...] = vn

    blk = pl.BlockSpec((tr, C), lambda i: (i, 0))
    lay = pl.BlockSpec((None, tr, C), lambda i: (layer, i, 0))
    args = [w, m, v, g1] + ([g2] if two else []) + (list(into) if has_into else [])
    in_specs = [lay] * 3 + [blk] * (2 if two else 1) + (_any_specs(4) if has_into else [])
    aliases = {len(args) - 4 + t: t for t in range(4)} if has_into else {}
    return pl.pallas_call(
        body, grid=(R // tr,), in_specs=in_specs, out_specs=[lay] * 4, out_shape=[S((nl, R, C), F32)] * 4,
        input_output_aliases=aliases, compiler_params=_params("parallel"), name=name)(*args)


_USE_ORDER = [("w_ffn1_gu", 0), ("w_ffn1_down", 0), ("conv_w", 0), ("w_in_even", 0), ("w_out_even", 0), ("w_xq", 0),
              ("w_xkv", 0), ("w_xo", 0), ("w_ffn2_gu", 0), ("w_ffn2_down", 0), ("w_ffn1_gu", 1), ("w_ffn1_down", 1),
              ("w_qkv", 0), ("b_qkv", 0), ("w_o_odd", 0), ("w_xq", 1), ("w_xkv", 1), ("w_xo", 1), ("w_ffn2_gu", 1),
              ("w_ffn2_down", 1)]
_SMALL = ["norm_ffn1", "norm_mix", "gm_ln_g", "gm_ln_b", "gm_ws", "gm_bs", "conv_w", "conv_b", "dt_bias", "a_log",
          "d_skip", "ssd_norm", "b_qkv", "sinks", "norm_xq", "norm_mem", "norm_ffn2", "final_norm"]
_WEIGHTS = ["norm_ffn1", "w_ffn1_gu", "w_ffn1_down", "norm_mix", "w_in_even", "gm_ln_g", "gm_ln_b", "gm_ws", "gm_bs",
            "conv_w", "conv_b", "dt_bias", "a_log", "d_skip", "ssd_norm", "w_out_even", "w_qkv", "b_qkv", "sinks",
            "w_o_odd", "norm_xq", "norm_mem", "w_xq", "w_xkv", "w_xo", "norm_ffn2", "w_ffn2_gu", "w_ffn2_down",
            "final_norm"]


def _pack(arrs):
    rows = []
    for a in arrs:
        f = a.reshape(-1).astype(F32)
        pad = (-f.shape[0]) % LANE
        rows.append(jnp.pad(f, (0, pad)).reshape(-1, LANE))
    out = jnp.concatenate(rows, 0)
    pad = (-out.shape[0]) % 8
    return jnp.pad(out, ((0, pad), (0, 0)))


def _unpack(packed, shapes):
    outs, r = [], 0
    for shp in shapes:
        n = math.prod(shp)
        nr = -(-n // LANE)
        outs.append(packed[r:r + nr].reshape(-1)[:n].reshape(shp))
        r += nr
    return outs


def kernel(x, mem, positions, norm_ffn1, w_ffn1_gu, w_ffn1_down, norm_mix, w_in_even, gm_ln_g, gm_ln_b, gm_ws, gm_bs, conv_w, conv_b, dt_bias, a_log, d_skip, ssd_norm, w_out_even, w_qkv, b_qkv, sinks, w_o_odd, norm_xq, norm_mem, w_xq, w_xkv, w_xo, norm_ffn2, w_ffn2_gu, w_ffn2_down, final_norm, loss_target, m_norm_ffn1, m_w_ffn1_gu, m_w_ffn1_down, m_norm_mix, m_w_in_even, m_gm_ln_g, m_gm_ln_b, m_gm_ws, m_gm_bs, m_conv_w, m_conv_b, m_dt_bias, m_a_log, m_d_skip, m_ssd_norm, m_w_out_even, m_w_qkv, m_b_qkv, m_sinks, m_w_o_odd, m_norm_xq, m_norm_mem, m_w_xq, m_w_xkv, m_w_xo, m_norm_ffn2, m_w_ffn2_gu, m_w_ffn2_down, m_final_norm, v_norm_ffn1, v_w_ffn1_gu, v_w_ffn1_down, v_norm_mix, v_w_in_even, v_gm_ln_g, v_gm_ln_b, v_gm_ws, v_gm_bs, v_conv_w, v_conv_b, v_dt_bias, v_a_log, v_d_skip, v_ssd_norm, v_w_out_even, v_w_qkv, v_b_qkv, v_sinks, v_w_o_odd, v_norm_xq, v_norm_mem, v_w_xq, v_w_xkv, v_w_xo, v_norm_ffn2, v_w_ffn2_gu, v_w_ffn2_down, v_final_norm):
    a = dict(locals())
    w = {k: a[k] for k in _WEIGHTS}
    mom = {k: a["m_" + k] for k in _WEIGHTS}
    var = {k: a["v_" + k] for k in _WEIGHTS}
    chip = 2 * lax.axis_index("x") + lax.axis_index("y")

    shards = [w[k][i:i + 1] if k in ("conv_w", "b_qkv") else w[k][i:i + 1].astype(BF16) for k, i in _USE_ORDER]
    items, _ = _exchange_start("gather_start", None, [_own_slot(s, chip) for s in shards])
    pending = dict(zip(_USE_ORDER, items))

    def getw(name, layer, after):
        got = _exchange_wait(f"gather_wait_{name}_{layer}", pending.pop((name, layer)), after)
        if name == "w_in_even":
            w_in = jnp.transpose(got[:, 0], (1, 0, 2)).reshape(D, EVEN_IN)
            return w_in[:, :EVEN_MAIN], jnp.pad(w_in[:, EVEN_MAIN:], ((0, 0), (0, LANE - (EVEN_IN - EVEN_MAIN))))
        if name == "conv_w":
            return jnp.transpose(got[:, 0], (1, 0, 2)).reshape(4, CONV_C)
        if name == "b_qkv":
            return got.reshape(1, ODD_IN)
        return got

    sent = []

    def putg(grads):
        names, arrs = [], []
        for (name, layer), g in grads.items():
            if name == "w_in_even":
                dw_in = jnp.concatenate([g[0], g[1][:, :EVEN_IN - EVEN_MAIN]], 1)
                g = jnp.transpose(dw_in.reshape(D, 4, EVEN_IN // 4), (1, 0, 2)).reshape(4, 1, D, EVEN_IN // 4)
            names.append((name, layer))
            arrs.append(g)
        own = [_own_slot(lax.dynamic_index_in_dim(g, chip, 0, keepdims=False), chip) for g in arrs]
        its, tok = _exchange_start(f"scatter_start_{names[0][0]}_{names[0][1]}", arrs, own)
        sent.append(list(zip(names, its)))
        return tok

    P = {k: w[k] for k in _SMALL}
    loss, grad_x, sm = _local_step(x[0], mem[0], positions[0], loss_target[0], getw, P, putg)
    loss = lax.psum(loss, ("x", "y", "c"))

    out = {}

    def update(groups, tag, after):
        names = [nm for grp in groups for nm, _ in grp]
        part = []
        for grp in groups:
            for (name, layer), it in grp:
                r = _exchange_wait(f"scatter_wait_{name}_{layer}", it, after)
                part.append(_sum_slots(f"sum_{name}_{layer}", r.reshape(4, -1, r.shape[-1]), 4))
        other = _swap_sibling(f"swap_partials_{tag}", part)
        for (name, layer), p1, p2 in zip(names, part, other):
            out[name] = _adamw(f"adamw_{name}_{layer}", w[name], mom[name], var[name], layer, p1, p2, out.get(name))

    update(sent[:-1], "a", grad_x)
    done_a = out["w_out_even"][0]
    update(sent[-1:], "b", done_a)

    full_shapes = {k: w[k].shape for k in _SMALL}
    full_shapes["conv_w"], full_shapes["b_qkv"] = (1, 4, CONV_C), (1, ODD_IN)
    packed = _pack([sm[k] for k in _SMALL])
    total = _sum_slots("sum_small", _gather_all("gather_small", packed, done_a), 8)
    gs = dict(zip(_SMALL, _unpack(total, [full_shapes[k] for k in _SMALL])))
    gs["conv_w"] = lax.dynamic_slice_in_dim(gs["conv_w"], chip * (CONV_C // 4), CONV_C // 4, 2)
    gs["b_qkv"] = lax.dynamic_slice_in_dim(gs["b_qkv"], chip * (ODD_IN // 4), ODD_IN // 4, 1)
    res = _adamw("adamw_small", _pack([w[k] for k in _SMALL])[None], _pack([mom[k] for k in _SMALL])[None],
                 _pack([var[k] for k in _SMALL])[None], 0, _pack([gs[k] for k in _SMALL]))
    shapes = [w[k].shape for k in _SMALL]
    for k, g, d, nm, nv in zip(_SMALL, *[_unpack(r[0], shapes) for r in res]):
        out[k] = [g, d, nm, nv]

    return (loss, grad_x[None], *[out[k][0] for k in _WEIGHTS], *[out[k][1] for k in _WEIGHTS],
            *[out[k][2] for k in _WEIGHTS], *[out[k][3] for k in _WEIGHTS])
```

```python
import functools
import math

import jax
import jax.numpy as jnp
from jax import lax
from jax.experimental import pallas as pl
from jax.experimental.pallas import tpu as pltpu

F32, BF16 = jnp.float32, jnp.bfloat16
S = jax.ShapeDtypeStruct
MESH = pl.DeviceIdType.MESH

D = 2048
DFF = 5632
EPS = 1e-5
CH = 128
GM_G, GM_GD = 4, 512
NH, HD, NG, HPG, NS = 32, 64, 4, 8, 128
CONV_C = 3072
EVEN_MAIN, EVEN_IN = 9216, 9248
AH, AKV, AREP, AHD = 32, 4, 8, 64
ODD_IN = 2560
XH, XHD, XW = 4, 128, 512
ATT_SCALE = AHD ** -0.5
X_SCALE = XHD ** -0.5
ROPE_THETA = 500000.0
ROT = 16
LR, B1, B2, AEPS, WD, STEP = 0.001, 0.9, 0.999, 1e-08, 0.01, 10
LANE = 128
VMEM_LIMIT_V7X = 56 * 1024 * 1024


def _params(*sem):
    return pltpu.CompilerParams(dimension_semantics=sem, vmem_limit_bytes=VMEM_LIMIT_V7X)


def _tile(dim, target):
    if dim <= target:
        return dim
    t = (target // LANE) * LANE
    while t > LANE and dim % t:
        t -= LANE
    assert dim % t == 0, (dim, target)
    return t


class Op:
    def __init__(self, arr, kind=None, layer=0):
        self.arr, self.kind, self.layer = arr, kind, layer
        if kind is None:
            self.R, self.C = arr.shape
        else:
            L = arr.shape[0]
            self.R = arr.shape[2] * (L if kind == "r" else 1)
            self.C = arr.shape[3] * (L if kind == "c" else 1)

    def unit(self, axis):
        if self.kind == "r" and axis == 0:
            return self.arr.shape[2]
        if self.kind == "c" and axis == 1:
            return self.arr.shape[3]
        return (self.R, self.C)[axis]

    def spec(self, tr, tc, pick):
        if self.kind is None:
            return pl.BlockSpec((tr, tc), lambda i, j, k: pick(i, j, k))
        l = self.layer
        if self.kind == "c":
            per = self.arr.shape[3] // tc
            return pl.BlockSpec((None, None, tr, tc),
                                lambda i, j, k: (pick(i, j, k)[1] // per, l, pick(i, j, k)[0], pick(i, j, k)[1] % per))
        per = self.arr.shape[2] // tr
        return pl.BlockSpec((None, None, tr, tc),
                            lambda i, j, k: (pick(i, j, k)[0] // per, l, pick(i, j, k)[0] % per, pick(i, j, k)[1]))


_DIMS = {"nn": (((1,), (0,)), ((), ())), "nt": (((1,), (1,)), ((), ())), "tn": (((0,), (0,)), ((), ()))}
_PICK_A = {"nn": lambda i, j, k: (i, k), "nt": lambda i, j, k: (i, k), "tn": lambda i, j, k: (k, i)}
_PICK_B = {"nn": lambda i, j, k: (k, j), "nt": lambda i, j, k: (j, k), "tn": lambda i, j, k: (k, j)}


def _mm(name, mode, a, b, out_dtype, *, out=None, res=None, bias=None, scale=1.0, norm_gain=None,
        tm_t=1024, tn_t=1024, tk_t=2048):
    if not isinstance(a, Op):
        a = Op(a)
    if not isinstance(b, Op):
        b = Op(b)
    if mode == "nn":
        M, K, N = a.R, a.C, b.C
        assert b.R == K
        um, uk, un = a.unit(0), math.gcd(a.unit(1), b.unit(0)), b.unit(1)
    elif mode == "nt":
        M, K, N = a.R, a.C, b.R
        assert b.C == K
        um, uk, un = a.unit(0), math.gcd(a.unit(1), b.unit(1)), b.unit(0)
    else:
        K, M, N = a.R, a.C, b.C
        assert b.R == K
        um, uk, un = a.unit(1), math.gcd(a.unit(0), b.unit(0)), b.unit(1)
    if out is not None:
        okind, oL, olayers, olayer = out
        if okind == "c":
            un = math.gcd(un, N // oL)
        else:
            um = math.gcd(um, M // oL)
    tm, tn, tk = _tile(um, tm_t), _tile(un, tn_t), _tile(uk, tk_t)
    gi, gj, gk = M // tm, N // tn, K // tk
    a_blk = (tm, tk) if mode != "tn" else (tk, tm)
    b_blk = {"nn": (tk, tn), "nt": (tn, tk), "tn": (tk, tn)}[mode]
    in_specs = [a.spec(*a_blk, _PICK_A[mode]), b.spec(*b_blk, _PICK_B[mode])]
    args = [a.arr, b.arr]
    if res is not None:
        in_specs.append(pl.BlockSpec((tm, tn), lambda i, j, k: (i, j)))
        args.append(res)
    if bias is not None:
        in_specs.append(pl.BlockSpec((1, tn), lambda i, j, k: (0, j)))
        args.append(bias)
    if out is None:
        out_shape = S((M, N), out_dtype)
        out_spec = pl.BlockSpec((tm, tn), lambda i, j, k: (i, j))
    else:
        shp = (oL, olayers, M, N // oL) if okind == "c" else (oL, olayers, M // oL, N)
        out_shape = S(shp, out_dtype)
        out_spec = Op(out_shape, okind, olayer).spec(tm, tn, lambda i, j, k: (i, j))
    has_res, has_bias, has_norm = res is not None, bias is not None, norm_gain is not None
    if has_norm:
        assert out is None and tn == N
        in_specs.append(pl.BlockSpec((1, N), lambda i, j, k: (0, 0)))
        args.append(norm_gain)
        out_shape = [out_shape, S((M, N), BF16)]
        out_spec = [out_spec, pl.BlockSpec((tm, tn), lambda i, j, k: (i, j))]
    dims = _DIMS[mode]

    def body(a_ref, b_ref, *rest):
        rest = list(rest)
        res_ref = rest.pop(0) if has_res else None
        bias_ref = rest.pop(0) if has_bias else None
        gain_ref = rest.pop(0) if has_norm else None
        o_ref = rest.pop(0)
        h_ref = rest.pop(0) if has_norm else None
        part = lax.dot_general(a_ref[...].astype(BF16), b_ref[...].astype(BF16), dims, preferred_element_type=F32)

        def finish(r):
            if scale != 1.0:
                r = r * scale
            if has_bias:
                r = r + bias_ref[...]
            if has_res:
                r = r + res_ref[...]
            o_ref[...] = r.astype(o_ref.dtype)
            if has_norm:
                h_ref[...] = (r * lax.rsqrt(jnp.mean(r * r, -1, keepdims=True) + EPS) * gain_ref[...]).astype(BF16)

        if gk == 1:
            finish(part)
            return
        acc, = rest
        k = pl.program_id(2)

        @pl.when(k == 0)
        def _():
            acc[...] = part

        @pl.when((k > 0) & (k < gk - 1))
        def _():
            acc[...] += part

        @pl.when(k == gk - 1)
        def _():
            finish(acc[...] + part)

    return pl.pallas_call(
        body, grid=(gi, gj, gk), in_specs=in_specs, out_specs=out_spec, out_shape=out_shape,
        scratch_shapes=[pltpu.VMEM((tm, tn), F32)] if gk > 1 else [],
        compiler_params=_params("parallel", "parallel", "arbitrary"), name=name)(*args)


def _rms_fwd(name, x, gain):
    T = x.shape[0]
    tt = _tile(T, 512)

    def body(x_ref, g_ref, o_ref):
        xv = x_ref[...]
        r = lax.rsqrt(jnp.mean(xv * xv, -1, keepdims=True) + EPS)
        o_ref[...] = (xv * r * g_ref[...]).astype(BF16)

    return pl.pallas_call(
        body, grid=(T // tt,),
        in_specs=[pl.BlockSpec((tt, D), lambda i: (i, 0)), pl.BlockSpec((1, D), lambda i: (0, 0))],
        out_specs=pl.BlockSpec((tt, D), lambda i: (i, 0)), out_shape=S((T, D), BF16),
        compiler_params=_params("parallel"), name=name)(x, gain)


def _rms_bwd(name, x, gain, dh, dx_in=None):
    T = x.shape[0]
    tt = _tile(T, 512)
    has_in = dx_in is not None

    def body(x_ref, g_ref, dh_ref, *rest):
        rest = list(rest)
        dxin_ref = rest.pop(0) if has_in else None
        dx_ref, dxb_ref, dg_ref = rest
        xv = x_ref[...]
        r = lax.rsqrt(jnp.mean(xv * xv, -1, keepdims=True) + EPS)
        xh = xv * r
        dy = dh_ref[...].astype(F32)
        dxh = dy * g_ref[...]
        dx = r * (dxh - xh * jnp.mean(dxh * xh, -1, keepdims=True))
        if has_in:
            dx = dx + dxin_ref[...]
        dx_ref[...] = dx
        dxb_ref[...] = dx.astype(BF16)
        part = jnp.sum(dy * xh, 0, keepdims=True)

        @pl.when(pl.program_id(0) == 0)
        def _():
            dg_ref[...] = part

        @pl.when(pl.program_id(0) > 0)
        def _():
            dg_ref[...] += part

    row = pl.BlockSpec((tt, D), lambda i: (i, 0))
    vec = pl.BlockSpec((1, D), lambda i: (0, 0))
    args = [x, gain, dh] + ([dx_in] if has_in else [])
    return pl.pallas_call(
        body, grid=(T // tt,), in_specs=[row, vec, row] + ([row] if has_in else []),
        out_specs=[row, row, vec], out_shape=[S((T, D), F32), S((T, D), BF16), S((1, D), F32)],
        compiler_params=_params("arbitrary"), name=name)(*args)


def _sigmoid(x):
    return 0.5 * jnp.tanh(0.5 * x) + 0.5


def _ffn_up(name, h, w4, layer, tm_t=512, tn_t=1408):
    T = h.shape[0]
    n_sh = w4.shape[3]
    tm, tn = _tile(T, tm_t), _tile(n_sh, tn_t)
    per = n_sh // tn

    def body(h_ref, wg_ref, wu_ref, g_ref, u_ref, a_ref):
        hv = h_ref[...]
        g = jnp.dot(hv, wg_ref[...], preferred_element_type=F32)
        u = jnp.dot(hv, wu_ref[...], preferred_element_type=F32)
        g_ref[...] = g.astype(BF16)
        u_ref[...] = u.astype(BF16)
        a_ref[...] = (g * _sigmoid(g) * u).astype(BF16)

    o = pl.BlockSpec((tm, tn), lambda j, i: (i, j))
    return pl.pallas_call(
        body, grid=(DFF // tn, T // tm),
        in_specs=[pl.BlockSpec((tm, D), lambda j, i: (i, 0)),
                  pl.BlockSpec((None, None, D, tn), lambda j, i: (j // per, layer, 0, j % per)),
                  pl.BlockSpec((None, None, D, tn), lambda j, i: (2 + j // per, layer, 0, j % per))],
        out_specs=[o, o, o], out_shape=[S((T, DFF), BF16)] * 3,
        compiler_params=_params("parallel", "parallel"), name=name)(h, w4, w4)


def _ffn_dact(name, dxb, wd4, layer, g, u, tm_t=512):
    T = dxb.shape[0]
    r_sh = wd4.shape[2]
    tm, tn = _tile(T, tm_t), _tile(r_sh, 1408)
    per = r_sh // tn

    def body(dx_ref, w_ref, g_ref, u_ref, o_ref):
        da = 0.5 * lax.dot_general(dx_ref[...], w_ref[...], _DIMS["nt"], preferred_element_type=F32)
        gv, uv = g_ref[...].astype(F32), u_ref[...].astype(F32)
        sg = _sigmoid(gv)
        o_ref[0, 0] = (da * uv * sg * (1.0 + gv * (1.0 - sg))).astype(BF16)
        o_ref[1, 0] = (da * gv * sg).astype(BF16)

    t = pl.BlockSpec((tm, tn), lambda j, i: (i, j))
    return pl.pallas_call(
        body, grid=(DFF // tn, T // tm),
        in_specs=[pl.BlockSpec((tm, D), lambda j, i: (i, 0)),
                  pl.BlockSpec((None, None, tn, D), lambda j, i: (j // per, layer, j % per, 0)), t, t],
        out_specs=pl.BlockSpec((2, 1, tm, tn), lambda j, i: (0, 0, i, j)), out_shape=S((2, 1, T, DFF), BF16),
        compiler_params=_params("parallel", "parallel"), name=name)(dxb, wd4, g, u)


def _gelu(x):
    return 0.5 * x * (1.0 + lax.erf(x * 0.7071067811865476))


def _causal(n):
    return lax.broadcasted_iota(jnp.int32, (n, n), 0) >= lax.broadcasted_iota(jnp.int32, (n, n), 1)


def _gmlp_math(u_raw, v_raw, lng, lnb, ws, bs):
    causal = _causal(CH)
    outs = []
    for g in range(GM_G):
        u, v = _gelu(u_raw[g]), _gelu(v_raw[g])
        mu = jnp.mean(v, -1, keepdims=True)
        var = jnp.mean(jnp.square(v - mu), -1, keepdims=True)
        vn = (v - mu) * lax.rsqrt(var + EPS) * lng[g] + lnb[g]
        wm = jnp.where(causal, ws[g], 0.0)
        s = jnp.dot(wm.astype(BF16), vn.astype(BF16), preferred_element_type=F32) + bs[g]
        outs.append(u * s)
    return outs


def _gmlp_load(proj_ref, lng_ref, lnb_ref, ws_ref, bs_ref):
    sl = lambda g, off: slice(off + g * GM_GD, off + (g + 1) * GM_GD)
    u_raw = [proj_ref[:, sl(g, 0)].astype(F32) for g in range(GM_G)]
    v_raw = [proj_ref[:, sl(g, D)].astype(F32) for g in range(GM_G)]
    lng = [lng_ref[:, sl(g, 0)] for g in range(GM_G)]
    lnb = [lnb_ref[:, sl(g, 0)] for g in range(GM_G)]
    ws = [ws_ref[g] for g in range(GM_G)]
    bs = [bs_ref[g] for g in range(GM_G)]
    return u_raw, v_raw, lng, lnb, ws, bs


_GM_PAR = lambda: [pl.BlockSpec((1, D), lambda i: (0, 0)), pl.BlockSpec((1, D), lambda i: (0, 0)),
                   pl.BlockSpec((GM_G, CH, CH), lambda i: (0, 0, 0)), pl.BlockSpec((GM_G, CH, 1), lambda i: (0, 0, 0))]


def _gmlp_fwd(name, proj, lng, lnb, ws, bs):
    T = proj.shape[0]

    def body(proj_ref, lng_ref, lnb_ref, ws_ref, bs_ref, o_ref):
        outs = _gmlp_math(*_gmlp_load(proj_ref, lng_ref, lnb_ref, ws_ref, bs_ref))
        for g in range(GM_G):
            o_ref[:, g * GM_GD:(g + 1) * GM_GD] = outs[g].astype(BF16)

    return pl.pallas_call(
        body, grid=(T // CH,), in_specs=[pl.BlockSpec((CH, 2 * D), lambda i: (i, 0))] + _GM_PAR(),
        out_specs=pl.BlockSpec((CH, D), lambda i: (i, 0)), out_shape=S((T, 2 * D), BF16),
        compiler_params=_params("parallel"), name=name)(proj, lng, lnb, ws, bs)


def _acc_store(first, ref, idx, val):
    @pl.when(first)
    def _():
        ref[idx] = val

    @pl.when(jnp.logical_not(first))
    def _():
        ref[idx] += val


def _gmlp_bwd(name, proj, lng, lnb, ws, bs, dmix, dproj):
    T = proj.shape[0]

    def body(proj_ref, lng_ref, lnb_ref, ws_ref, bs_ref, dmix_ref, _, dproj_ref, dlng_ref, dlnb_ref, dws_ref, dbs_ref):
        first = pl.program_id(0) == 0
        prim = _gmlp_load(proj_ref, lng_ref, lnb_ref, ws_ref, bs_ref)
        _, vjp = jax.vjp(_gmlp_math, *prim)
        du, dv, dlng, dlnb, dws, dbs = vjp([dmix_ref[:, g * GM_GD:(g + 1) * GM_GD].astype(F32) for g in range(GM_G)])
        for g in range(GM_G):
            sl = slice(g * GM_GD, (g + 1) * GM_GD)
            dproj_ref[:, sl] = du[g].astype(BF16)
            dproj_ref[:, D + g * GM_GD:D + (g + 1) * GM_GD] = dv[g].astype(BF16)
            _acc_store(first, dlng_ref, (slice(None), sl), dlng[g])
            _acc_store(first, dlnb_ref, (slice(None), sl), dlnb[g])
            _acc_store(first, dws_ref, g, dws[g])
            _acc_store(first, dbs_ref, g, dbs[g])

    par = _GM_PAR()
    return pl.pallas_call(
        body, grid=(T // CH,),
        in_specs=[pl.BlockSpec((CH, 2 * D), lambda i: (i, 0))] + par +
                 [pl.BlockSpec((CH, D), lambda i: (i, 0)), pl.BlockSpec(memory_space=pl.ANY)],
        out_specs=[pl.BlockSpec((CH, 2 * D), lambda i: (i, 0))] + par,
        out_shape=[S(dproj.shape, BF16), S((1, D), F32), S((1, D), F32), S((GM_G, CH, CH), F32), S((GM_G, CH, 1), F32)],
        input_output_aliases={6: 0}, compiler_params=_params("arbitrary"), name=name)(proj, lng, lnb, ws, bs, dmix, dproj)


CONV_TT = 256
HALO = 8


def _shift_rows(cur, halo_after, s):
    if s == 0:
        return cur
    n = cur.shape[0]
    return pltpu.roll(jnp.concatenate([cur, halo_after], 0), s, 0)[:n]


def _conv_fwd(name, proj, w, b):
    T = proj.shape[0]
    tt = _tile(T, CONV_TT)
    hb = tt // HALO

    def body(x_ref, halo_ref, w_ref, b_ref, y_ref, xc_ref):
        i = pl.program_id(0)
        x = x_ref[...].astype(F32)
        halo = halo_ref[...].astype(F32) * (i > 0).astype(F32)
        y = b_ref[...] + w_ref[3:4, :] * x
        for s in (1, 2, 3):
            y = y + w_ref[3 - s:4 - s, :] * _shift_rows(x, halo, s)
        y_ref[...] = y.astype(BF16)
        xc_ref[...] = (y * _sigmoid(y)).astype(BF16)

    o = pl.BlockSpec((tt, CONV_C), lambda i: (i, 0))
    return pl.pallas_call(
        body, grid=(T // tt,),
        in_specs=[pl.BlockSpec((tt, CONV_C), lambda i: (i, 2)),
                  pl.BlockSpec((HALO, CONV_C), lambda i: (jnp.maximum(i * hb - 1, 0), 2)),
                  pl.BlockSpec((4, CONV_C), lambda i: (0, 0)), pl.BlockSpec((1, CONV_C), lambda i: (0, 0))],
        out_specs=[o, o], out_shape=[S((T, CONV_C), BF16)] * 2,
        compiler_params=_params("parallel"), name=name)(proj, proj, w, b)


def _conv_bwd(name, proj, ypre, dxc, w, dproj):
    T = proj.shape[0]
    tt = _tile(T, CONV_TT)
    hb = tt // HALO
    nt = T // tt

    def dsilu(y):
        sg = _sigmoid(y)
        return sg * (1.0 + y * (1.0 - sg))

    def body(x_ref, xh_ref, y_ref, yn_ref, d_ref, dn_ref, w_ref, _, dproj_ref, dw_ref, db_ref):
        i = pl.program_id(0)
        first = i == 0
        x = x_ref[...].astype(F32)
        halo = xh_ref[...].astype(F32) * (i > 0).astype(F32)
        dy = d_ref[...].astype(F32) * dsilu(y_ref[...].astype(F32))
        dyn = dn_ref[...].astype(F32) * dsilu(yn_ref[...].astype(F32)) * (i < nt - 1).astype(F32)
        ext = jnp.concatenate([dy, dyn], 0)
        dx = w_ref[3:4, :] * dy
        _acc_store(first, dw_ref, (slice(3, 4), slice(None)), jnp.sum(x * dy, 0, keepdims=True))
        for s in (1, 2, 3):
            dx = dx + w_ref[3 - s:4 - s, :] * pltpu.roll(ext, tt + HALO - s, 0)[:tt]
            _acc_store(first, dw_ref, (slice(3 - s, 4 - s), slice(None)),
                       jnp.sum(_shift_rows(x, halo, s) * dy, 0, keepdims=True))
        _acc_store(first, db_ref, (slice(None), slice(None)), jnp.sum(dy, 0, keepdims=True))
        dproj_ref[...] = dx.astype(BF16)

    cur = pl.BlockSpec((tt, CONV_C), lambda i: (i, 0))
    nxt = pl.BlockSpec((HALO, CONV_C), lambda i: (jnp.minimum((i + 1) * hb, T // HALO - 1), 0))
    return pl.pallas_call(
        body, grid=(nt,),
        in_specs=[pl.BlockSpec((tt, CONV_C), lambda i: (i, 2)),
                  pl.BlockSpec((HALO, CONV_C), lambda i: (jnp.maximum(i * hb - 1, 0), 2)),
                  cur, nxt, cur, nxt, pl.BlockSpec((4, CONV_C), lambda i: (0, 0)), pl.BlockSpec(memory_space=pl.ANY)],
        out_specs=[pl.BlockSpec((tt, CONV_C), lambda i: (i, 2)), pl.BlockSpec((4, CONV_C), lambda i: (0, 0)),
                   pl.BlockSpec((1, CONV_C), lambda i: (0, 0))],
        out_shape=[S(dproj.shape, BF16), S((4, CONV_C), F32), S((1, CONV_C), F32)],
        input_output_aliases={7: 0}, compiler_params=_params("arbitrary"), name=name)(proj, proj, ypre, ypre, dxc, dxc, w, dproj)


def _softplus(x):
    return jnp.maximum(x, 0.0) + jnp.log(1.0 + jnp.exp(-jnp.abs(x)))


def _ssd_math(x, Bm, Cm, dtr, z, prev, dtb, alog, dsk, nrm):
    hi = lax.Precision.HIGHEST
    causal = _causal(CH)
    tri = causal.astype(F32)
    lane = lax.broadcasted_iota(jnp.int32, (1, LANE), 1)
    sub = lax.broadcasted_iota(jnp.int32, (LANE, 1), 0)
    dt = _softplus(dtr + dtb)
    a = dt * (-jnp.exp(alog))
    a_cs = jnp.dot(tri, a, preferred_element_type=F32, precision=hi)
    a_csT = lax.dot_general(a, tri, (((0,), (1,)), ((), ())), preferred_element_type=F32, precision=hi)
    a_last = jnp.sum(a, 0, keepdims=True)
    gw = HPG * HD
    outs, new = [], []
    for g in range(NG):
        spread = (lax.broadcasted_iota(jnp.int32, (LANE, gw), 0)
                  == g * HPG + lax.broadcasted_iota(jnp.int32, (LANE, gw), 1) // HD).astype(F32)
        to_lanes = lambda v: jnp.dot(v, spread, preferred_element_type=F32, precision=hi)
        col_e, dt_e, last_e, dsk_e = to_lanes(a_cs), to_lanes(dt), to_lanes(a_last), to_lanes(dsk)
        last_r = lax.dot_general(spread, a_last, (((0,), (1,)), ((), ())), preferred_element_type=F32, precision=hi)
        cb = lax.dot_general(Cm[g].astype(BF16), Bm[g].astype(BF16), _DIMS["nt"], preferred_element_type=F32)
        xg = jnp.concatenate(x[g * HPG:(g + 1) * HPG], 1)
        yd = []
        for h in range(g * HPG, (g + 1) * HPG):
            ohl = (lane == h).astype(F32)
            col = jnp.sum(a_cs * ohl, 1, keepdims=True)
            row = jnp.sum(a_csT * (sub == h).astype(F32), 0, keepdims=True)
            dtc = jnp.sum(dt * ohl, 1, keepdims=True)
            lmat = jnp.where(causal, jnp.exp(jnp.where(causal, col - row, 0.0)), 0.0)
            yd.append(jnp.dot((cb * lmat).astype(BF16), (x[h] * dtc).astype(BF16), preferred_element_type=F32))
        y = jnp.concatenate(yd, 1)
        y = y + jnp.exp(col_e) * lax.dot_general(Cm[g].astype(BF16), prev[g].astype(BF16), _DIMS["nt"],
                                                 preferred_element_type=F32)
        st = lax.dot_general((xg * dt_e * jnp.exp(last_e - col_e)).astype(BF16), Bm[g].astype(BF16), _DIMS["tn"],
                             preferred_element_type=F32)
        new.append(prev[g] * jnp.exp(last_r) + st)
        yg = (y + xg * dsk_e) * (z[g] * _sigmoid(z[g]))
        yg = yg * lax.rsqrt(jnp.mean(yg * yg, -1, keepdims=True) + EPS)
        outs.append(yg * nrm[g])
    return outs, new


def _ssd_load(xc_ref, dtr_ref, z_ref, state_ref, dtb_ref, alog_ref, dsk_ref, nrm_ref):
    gw = HPG * HD
    x = [xc_ref[:, h * HD:(h + 1) * HD].astype(F32) for h in range(NH)]
    Bm = [xc_ref[:, D + g * NS:D + (g + 1) * NS].astype(F32) for g in range(NG)]
    Cm = [xc_ref[:, D + NG * NS + g * NS:D + NG * NS + (g + 1) * NS].astype(F32) for g in range(NG)]
    z = [z_ref[:, g * gw:(g + 1) * gw].astype(F32) for g in range(NG)]
    prev = [state_ref[g * gw:(g + 1) * gw, :] for g in range(NG)]
    nrm = [nrm_ref[:, g * gw:(g + 1) * gw] for g in range(NG)]
    return x, Bm, Cm, dtr_ref[...], z, prev, dtb_ref[...], alog_ref[...], dsk_ref[...], nrm


_SSD_PAR = lambda: [pl.BlockSpec((1, LANE), lambda c: (0, 0))] * 3 + [pl.BlockSpec((1, D), lambda c: (0, 0))]


def _ssd_fwd(name, xc, dtr, proj, dtb, alog, dsk, nrm, mix):
    T = xc.shape[0]
    nc = T // CH

    def body(xc_ref, dtr_ref, z_ref, dtb_ref, alog_ref, dsk_ref, nrm_ref, _, mix_ref, prev_ref, state):
        @pl.when(pl.program_id(0) == 0)
        def _():
            state[...] = jnp.zeros_like(state)

        prev_ref[...] = state[...]
        outs, new = _ssd_math(*_ssd_load(xc_ref, dtr_ref, z_ref, state, dtb_ref, alog_ref, dsk_ref, nrm_ref))
        for g in range(NG):
            mix_ref[:, g * 512:(g + 1) * 512] = outs[g].astype(BF16)
            state[g * 512:(g + 1) * 512, :] = new[g]

    return pl.pallas_call(
        body, grid=(nc,),
        in_specs=[pl.BlockSpec((CH, CONV_C), lambda c: (c, 0)), pl.BlockSpec((CH, LANE), lambda c: (c, 0)),
                  pl.BlockSpec((CH, D), lambda c: (c, 2))] + _SSD_PAR() + [pl.BlockSpec(memory_space=pl.ANY)],
        out_specs=[pl.BlockSpec((CH, D), lambda c: (c, 1)), pl.BlockSpec((None, NH * HD, NS), lambda c: (c, 0, 0))],
        out_shape=[S(mix.shape, BF16), S((nc, NH * HD, NS), F32)],
        scratch_shapes=[pltpu.VMEM((NH * HD, NS), F32)], input_output_aliases={7: 0},
        compiler_params=_params("arbitrary"), name=name)(xc, dtr, proj, dtb, alog, dsk, nrm, mix)


def _ssd_bwd(name, xc, dtr, proj, prevs, dtb, alog, dsk, nrm, dmix, dproj):
    T = xc.shape[0]
    nc = T // CH
    rev = lambda c: nc - 1 - c

    def body(xc_ref, dtr_ref, z_ref, prev_ref, dtb_ref, alog_ref, dsk_ref, nrm_ref, dmix_ref, _,
             dproj_ref, dxc_ref, ddtr_ref, ddtb_ref, dalog_ref, ddsk_ref, dnrm_ref, dstate):
        first = pl.program_id(0) == 0

        @pl.when(first)
        def _():
            dstate[...] = jnp.zeros_like(dstate)

        prim = _ssd_load(xc_ref, dtr_ref, z_ref, prev_ref, dtb_ref, alog_ref, dsk_ref, nrm_ref)
        _, vjp = jax.vjp(_ssd_math, *prim)
        douts = [dmix_ref[:, g * 512:(g + 1) * 512].astype(F32) for g in range(NG)]
        dnew = [dstate[g * 512:(g + 1) * 512, :] for g in range(NG)]
        dx, dB, dC, ddtr, dz, dprev, ddtb, dalog, ddsk, dnrm = vjp((douts, dnew))
        for h in range(NH):
            dxc_ref[:, h * HD:(h + 1) * HD] = dx[h].astype(BF16)
        for g in range(NG):
            dstate[g * 512:(g + 1) * 512, :] = dprev[g]
            dxc_ref[:, D + g * NS:D + (g + 1) * NS] = dB[g].astype(BF16)
            dxc_ref[:, D + NG * NS + g * NS:D + NG * NS + (g + 1) * NS] = dC[g].astype(BF16)
            dproj_ref[:, g * 512:(g + 1) * 512] = dz[g].astype(BF16)
            _acc_store(first, dnrm_ref, (slice(None), slice(g * 512, (g + 1) * 512)), dnrm[g])
        ddtr_ref[...] = ddtr
        _acc_store(first, ddtb_ref, (slice(None), slice(None)), ddtb)
        _acc_store(first, dalog_ref, (slice(None), slice(None)), dalog)
        _acc_store(first, ddsk_ref, (slice(None), slice(None)), ddsk)

    vec = pl.BlockSpec((1, LANE), lambda c: (0, 0))
    return pl.pallas_call(
        body, grid=(nc,),
        in_specs=[pl.BlockSpec((CH, CONV_C), lambda c: (rev(c), 0)), pl.BlockSpec((CH, LANE), lambda c: (rev(c), 0)),
                  pl.BlockSpec((CH, D), lambda c: (rev(c), 2)),
                  pl.BlockSpec((None, NH * HD, NS), lambda c: (rev(c), 0, 0))] + _SSD_PAR() +
                 [pl.BlockSpec((CH, D), lambda c: (rev(c), 1)), pl.BlockSpec(memory_space=pl.ANY)],
        out_specs=[pl.BlockSpec((CH, D), lambda c: (rev(c), 2)), pl.BlockSpec((CH, CONV_C), lambda c: (rev(c), 0)),
                   pl.BlockSpec((CH, LANE), lambda c: (rev(c), 0)), vec, vec, vec, pl.BlockSpec((1, D), lambda c: (0, 0))],
        out_shape=[S(dproj.shape, BF16), S((T, CONV_C), BF16), S((T, LANE), F32), S((1, LANE), F32), S((1, LANE), F32),
                   S((1, LANE), F32), S((1, D), F32)],
        scratch_shapes=[pltpu.VMEM((NH * HD, NS), F32)], input_output_aliases={9: 0},
        compiler_params=_params("arbitrary"), name=name)(xc, dtr, proj, prevs, dtb, alog, dsk, nrm, dmix, dproj)


def _rope(x, c, s, sign):
    W = x.shape[1]
    reps = W // LANE
    C, Sg = jnp.tile(c, (1, reps)), jnp.tile(s, (1, reps))
    lane = lax.broadcasted_iota(jnp.int32, x.shape, 1) % AHD
    up, dn = pltpu.roll(x, W - ROT // 2, 1), pltpu.roll(x, ROT // 2, 1)
    sw = jnp.where(lane < ROT // 2, up, jnp.where(lane < ROT, dn, 0.0))
    return x * C + sign * sw * Sg


def _rope_fwd(name, qkv, cos, sin):
    T = qkv.shape[0]
    tt = _tile(T, 256)
    KV = AKV * AHD

    def body(x_ref, c_ref, s_ref, o_ref):
        c, s = c_ref[...], s_ref[...]
        o_ref[:, :D] = _rope(x_ref[:, :D], c, s, 1.0).astype(BF16)
        o_ref[:, D:D + KV] = _rope(x_ref[:, D:D + KV], c, s, 1.0).astype(BF16)
        o_ref[:, D + KV:] = x_ref[:, D + KV:].astype(BF16)

    tab = pl.BlockSpec((tt, LANE), lambda i: (i, 0))
    return pl.pallas_call(
        body, grid=(T // tt,), in_specs=[pl.BlockSpec((tt, ODD_IN), lambda i: (i, 0)), tab, tab],
        out_specs=pl.BlockSpec((tt, ODD_IN), lambda i: (i, 0)), out_shape=S((T, ODD_IN), BF16),
        compiler_params=_params("parallel"), name=name)(qkv, cos, sin)


def _rope_bwd(name, dq, dkv_cur, dkv_prev, cos, sin):
    T = dq.shape[0]
    nb = T // CH
    KV = AKV * AHD

    def body(dq_ref, cur_ref, nxt_ref, c_ref, s_ref, o_ref, db_ref):
        n = pl.program_id(0)
        c, s = c_ref[...], s_ref[...]
        dkv = cur_ref[...] + nxt_ref[...] * (n < nb - 1).astype(F32)
        o_ref[:, :D] = _rope(dq_ref[...].astype(F32), c, s, -1.0).astype(BF16)
        o_ref[:, D:D + KV] = _rope(dkv[:, :KV], c, s, -1.0).astype(BF16)
        o_ref[:, D + KV:] = dkv[:, KV:].astype(BF16)
        _acc_store(n == 0, db_ref, (slice(None), slice(None)), jnp.sum(o_ref[...].astype(F32), 0, keepdims=True))

    tab = pl.BlockSpec((CH, LANE), lambda n: (n, 0))
    return pl.pallas_call(
        body, grid=(nb,),
        in_specs=[pl.BlockSpec((CH, D), lambda n: (n, 0)), pl.BlockSpec((CH, 2 * KV), lambda n: (n, 0)),
                  pl.BlockSpec((CH, 2 * KV), lambda n: (jnp.minimum(n + 1, nb - 1), 0)), tab, tab],
        out_specs=[pl.BlockSpec((CH, ODD_IN), lambda n: (n, 0)), pl.BlockSpec((1, ODD_IN), lambda n: (0, 0))],
        out_shape=[S((T, ODD_IN), BF16), S((1, ODD_IN), F32)],
        compiler_params=_params("arbitrary"), name=name)(dq, dkv_cur, dkv_prev, cos, sin)


def _swa_math(q, kp, kc, vp, vc, snk, mask):
    outs = []
    for k in range(AKV):
        K = jnp.concatenate([kp[k], kc[k]], 0).astype(BF16)
        V = jnp.concatenate([vp[k], vc[k]], 0).astype(BF16)
        s = lax.dot_general(q[k].astype(BF16), K, _DIMS["nt"], preferred_element_type=F32) * ATT_SCALE
        s = jnp.where(mask, s, -jnp.inf)
        m = lax.stop_gradient(jnp.maximum(jnp.max(s, -1, keepdims=True), snk[k]))
        p = jnp.exp(s - m)
        pr = p / (jnp.sum(p, -1, keepdims=True) + jnp.exp(snk[k] - m))
        outs.append(jnp.dot(pr.astype(BF16), V, preferred_element_type=F32))
    return outs


def _stack_heads(ref, k):
    return jnp.concatenate([ref[:, (k * AREP + r) * AHD:(k * AREP + r + 1) * AHD].astype(F32) for r in range(AREP)], 0)


def _swa_load(q_ref, cur_ref, prv_ref, snk_ref):
    KV = AKV * AHD
    q = [_stack_heads(q_ref, k) for k in range(AKV)]
    kc = [cur_ref[:, k * AHD:(k + 1) * AHD].astype(F32) for k in range(AKV)]
    vc = [cur_ref[:, KV + k * AHD:KV + (k + 1) * AHD].astype(F32) for k in range(AKV)]
    kp = [prv_ref[:, k * AHD:(k + 1) * AHD].astype(F32) for k in range(AKV)]
    vp = [prv_ref[:, KV + k * AHD:KV + (k + 1) * AHD].astype(F32) for k in range(AKV)]
    snk = [jnp.concatenate([jnp.broadcast_to(snk_ref[:, k * AREP + r:k * AREP + r + 1], (CH, 1)) for r in range(AREP)], 0)
           for k in range(AKV)]
    return q, kp, kc, vp, vc, snk


def _swa_mask(n):
    iq = lax.broadcasted_iota(jnp.int32, (AREP * CH, 2 * CH), 0) % CH
    js = lax.broadcasted_iota(jnp.int32, (AREP * CH, 2 * CH), 1)
    rel = iq + CH - js
    return (rel >= 0) & (rel < CH) & ((n > 0) | (js >= CH))


def _swa_specs(T):
    KV = AKV * AHD
    return [pl.BlockSpec((CH, D), lambda n: (n, 0)), pl.BlockSpec((CH, 2 * KV), lambda n: (n, D // (2 * KV))),
            pl.BlockSpec((CH, 2 * KV), lambda n: (jnp.maximum(n - 1, 0), D // (2 * KV))),
            pl.BlockSpec((1, LANE), lambda n: (0, 0))]


def _swa_fwd(name, qkvr, snk):
    T = qkvr.shape[0]

    def body(q_ref, cur_ref, prv_ref, snk_ref, o_ref):
        outs = _swa_math(*_swa_load(q_ref, cur_ref, prv_ref, snk_ref), _swa_mask(pl.program_id(0)))
        for h in range(AH):
            k, r = divmod(h, AREP)
            o_ref[:, h * AHD:(h + 1) * AHD] = outs[k][r * CH:(r + 1) * CH].astype(BF16)

    return pl.pallas_call(
        body, grid=(T // CH,), in_specs=_swa_specs(T), out_specs=pl.BlockSpec((CH, D), lambda n: (n, 0)),
        out_shape=S((T, D), BF16), compiler_params=_params("parallel"), name=name)(qkvr, qkvr, qkvr, snk)


def _swa_bwd(name, qkvr, snk, do):
    T = qkvr.shape[0]
    KV = AKV * AHD

    def body(q_ref, cur_ref, prv_ref, snk_ref, do_ref, dq_ref, dcur_ref, dprv_ref, dsnk_ref):
        n = pl.program_id(0)

        @pl.when(n == 0)
        def _():
            dsnk_ref[...] = jnp.zeros_like(dsnk_ref)

        prim = _swa_load(q_ref, cur_ref, prv_ref, snk_ref)
        mask = _swa_mask(n)
        _, vjp = jax.vjp(lambda *p: _swa_math(*p, mask), *prim)
        dq, dkp, dkc, dvp, dvc, dsnk = vjp([_stack_heads(do_ref, k) for k in range(AKV)])
        for h in range(AH):
            k, r = divmod(h, AREP)
            dq_ref[:, h * AHD:(h + 1) * AHD] = dq[k][r * CH:(r + 1) * CH].astype(BF16)
            dsnk_ref[:, h:h + 1] += jnp.sum(dsnk[k][r * CH:(r + 1) * CH], 0, keepdims=True)
        for k in range(AKV):
            dcur_ref[:, k * AHD:(k + 1) * AHD] = dkc[k]
            dcur_ref[:, KV + k * AHD:KV + (k + 1) * AHD] = dvc[k]
            dprv_ref[:, k * AHD:(k + 1) * AHD] = dkp[k]
            dprv_ref[:, KV + k * AHD:KV + (k + 1) * AHD] = dvp[k]

    kv = pl.BlockSpec((CH, 2 * KV), lambda n: (n, 0))
    return pl.pallas_call(
        body, grid=(T // CH,), in_specs=_swa_specs(T) + [pl.BlockSpec((CH, D), lambda n: (n, 0))],
        out_specs=[pl.BlockSpec((CH, D), lambda n: (n, 0)), kv, kv, pl.BlockSpec((1, LANE), lambda n: (0, 0))],
        out_shape=[S((T, D), BF16), S((T, 2 * KV), F32), S((T, 2 * KV), F32), S((1, LANE), F32)],
        compiler_params=_params("arbitrary"), name=name)(qkvr, qkvr, qkvr, snk, do)


def _xat_math(q, k, v):
    outs = []
    for h in range(XH):
        s = lax.dot_general(q[h].astype(BF16), k[h].astype(BF16), _DIMS["nt"], preferred_element_type=F32) * X_SCALE
        m = lax.stop_gradient(jnp.max(s, -1, keepdims=True))
        p = jnp.exp(s - m)
        pr = p / jnp.sum(p, -1, keepdims=True)
        outs.append(jnp.dot(pr.astype(BF16), v[h].astype(BF16), preferred_element_type=F32))
    return outs


def _xat_load(q_ref, kv_ref):
    q = [q_ref[:, h * XHD:(h + 1) * XHD].astype(F32) for h in range(XH)]
    k = [kv_ref[:, h * XHD:(h + 1) * XHD].astype(F32) for h in range(XH)]
    v = [kv_ref[:, XW + h * XHD:XW + (h + 1) * XHD].astype(F32) for h in range(XH)]
    return q, k, v


def _xat_fwd(name, q, kv):
    T, M = q.shape[0], kv.shape[0]
    tt = _tile(T, 512)

    def body(q_ref, kv_ref, o_ref):
        outs = _xat_math(*_xat_load(q_ref, kv_ref))
        for h in range(XH):
            o_ref[:, h * XHD:(h + 1) * XHD] = outs[h].astype(BF16)

    return pl.pallas_call(
        body, grid=(T // tt,),
        in_specs=[pl.BlockSpec((tt, XW), lambda i: (i, 0)), pl.BlockSpec((M, 2 * XW), lambda i: (0, 0))],
        out_specs=pl.BlockSpec((tt, XW), lambda i: (i, 0)), out_shape=S((T, XW), BF16),
        compiler_params=_params("parallel"), name=name)(q, kv)


def _xat_bwd(name, q, kv, do):
    T, M = q.shape[0], kv.shape[0]
    tt = _tile(T, 512)

    def body(q_ref, kv_ref, do_ref, dq_ref, dkv_ref):
        first = pl.program_id(0) == 0
        _, vjp = jax.vjp(_xat_math, *_xat_load(q_ref, kv_ref))
        dq, dk, dv = vjp([do_ref[:, h * XHD:(h + 1) * XHD].astype(F32) for h in range(XH)])
        for h in range(XH):
            sl = slice(h * XHD, (h + 1) * XHD)
            dq_ref[:, sl] = dq[h].astype(BF16)
            _acc_store(first, dkv_ref, (slice(None), sl), dk[h])
            _acc_store(first, dkv_ref, (slice(None), slice(XW + h * XHD, XW + (h + 1) * XHD)), dv[h])

    qs = pl.BlockSpec((tt, XW), lambda i: (i, 0))
    kvs = pl.BlockSpec((M, 2 * XW), lambda i: (0, 0))
    return pl.pallas_call(
        body, grid=(T // tt,), in_specs=[qs, kvs, qs], out_specs=[qs, kvs],
        out_shape=[S((T, XW), BF16), S((M, 2 * XW), F32)],
        compiler_params=_params("arbitrary"), name=name)(q, kv, do)


def _loss_head(name, x, gain, target):
    T = x.shape[0]
    tt = _tile(T, 512)

    def body(x_ref, g_ref, t_ref, l_ref, dx_ref, dxb_ref, dg_ref):
        first = pl.program_id(0) == 0
        xv, g = x_ref[...], g_ref[...]
        r = lax.rsqrt(jnp.mean(xv * xv, -1, keepdims=True) + EPS)
        xh = xv * r
        e = xh * g - t_ref[...]
        part = 0.5 * jnp.sum(jnp.mean(e * e, -1, keepdims=True), (0, 1), keepdims=True)
        _acc_store(first, l_ref, (slice(None), slice(None)), jnp.broadcast_to(part, (1, LANE)))
        dy = e * (1.0 / D)
        dxh = dy * g
        dx = r * (dxh - xh * jnp.mean(dxh * xh, -1, keepdims=True))
        dx_ref[...] = dx
        dxb_ref[...] = dx.astype(BF16)
        _acc_store(first, dg_ref, (slice(None), slice(None)), jnp.sum(dy * xh, 0, keepdims=True))

    row = pl.BlockSpec((tt, D), lambda i: (i, 0))
    vec = pl.BlockSpec((1, D), lambda i: (0, 0))
    return pl.pallas_call(
        body, grid=(T // tt,), in_specs=[row, vec, row],
        out_specs=[pl.BlockSpec((1, LANE), lambda i: (0, 0)), row, row, vec],
        out_shape=[S((1, LANE), F32), S((T, D), F32), S((T, D), BF16), S((1, D), F32)],
        compiler_params=_params("arbitrary"), name=name)(x, gain, target)


def _out_proj(name, a, b, x, next_gain, scale=1.0, tk_t=2048, plain=(1024, 1024)):
    if next_gain is None:
        return _mm(name, "nn", a, b, F32, res=x, scale=scale, tm_t=plain[0], tn_t=plain[1], tk_t=tk_t), None
    return _mm(name, "nn", a, b, F32, res=x, scale=scale, norm_gain=next_gain, tm_t=512, tn_t=D, tk_t=min(tk_t, 2048))


def _ffn_fwd(tag, x, h, gain, wgu4, get_wd, next_gain):
    g, u, a = _ffn_up(f"{tag}_up", h, wgu4, 0)
    wd = get_wd(a).reshape(1, 1, DFF, D)
    x_new, h_next = _out_proj(f"{tag}_down", a, Op(wd, "r"), x, next_gain, 0.5, 2816)
    return x_new, h_next, (x, gain, h, g, u, a)


def _ffn_bwd(tag, saved, dx, dxb, wgu4, wd4, put):
    x, gain, h, g, u, a = saved
    dgu = _ffn_dact(f"{tag}_dact", dxb, wd4, 0, g, u, 1024)
    dwd = _mm(f"{tag}_dwd", "tn", a, dxb, BF16, out=("r", 4, 1, 0), scale=0.5, tm_t=1408, tn_t=1024, tk_t=2048)
    dwgu = _mm(f"{tag}_dwgu", "tn", h, Op(dgu, "c"), BF16, out=("c", 4, 1, 0), tm_t=1024, tn_t=256, tk_t=8192)
    tok = put(dwgu, dwd)
    dh = _mm(f"{tag}_dh", "nt", Op(dgu, "c"), Op(wgu4, "c"), BF16, bias=jnp.zeros((1, D), F32) + tok, tk_t=2816)
    dx, dxb, dgain = _rms_bwd(f"{tag}_dnorm", x, gain, dh, dx)
    return dx, dxb, dgain


def _xattn_fwd(tag, x, hq, mem, gq, gm, wxq4, wxkv4, wxo4, next_gain):
    mn = _rms_fwd(f"{tag}_normm", mem, gm)
    q = _mm(f"{tag}_q", "nn", hq, Op(wxq4, "r"), BF16)
    kv = _mm(f"{tag}_kv", "nn", mn, Op(wxkv4, "r"), BF16)
    o = _xat_fwd(f"{tag}_att", q, kv)
    wxo = jnp.transpose(wxo4[:, 0], (1, 0, 2)).reshape(XW, D)
    x_new, h_next = _out_proj(f"{tag}_o", o, wxo, x, next_gain)
    return x_new, h_next, (x, mem, gq, gm, hq, mn, q, kv, o)


def _xattn_bwd(tag, saved, dx, dxb, wxq4, wxkv4, wxo4, put):
    x, mem, gq, gm, hq, mn, q, kv, o = saved
    dwxo = _mm(f"{tag}_dwo", "tn", o, dxb, BF16, out=("c", 4, 1, 0))
    do = _mm(f"{tag}_do", "nt", dxb, Op(wxo4, "c"), BF16)
    dq, dkv = _xat_bwd(f"{tag}_datt", q, kv, do)
    dwxq = _mm(f"{tag}_dwq", "tn", hq, dq, BF16, out=("r", 4, 1, 0))
    dwxkv = _mm(f"{tag}_dwkv", "tn", mn, dkv, BF16, out=("r", 4, 1, 0))
    tok = put(dwxq, dwxkv, dwxo)
    dhq = _mm(f"{tag}_dhq", "nt", dq, Op(wxq4, "r"), BF16, bias=jnp.zeros((1, D), F32) + tok)
    dmn = _mm(f"{tag}_dmn", "nt", dkv, Op(wxkv4, "r"), BF16)
    _, _, dgm = _rms_bwd(f"{tag}_dnormm", mem, gm, dmn)
    dx, dxb, dgq = _rms_bwd(f"{tag}_dnormq", x, gq, dhq, dx)
    return dx, dxb, dgq, dgm


def _even_fwd(tag, x, h, gain, w_main, w_dt, p, wout4, next_gain):
    proj = _mm(f"{tag}_in", "nn", h, w_main, BF16)
    dtr = _mm(f"{tag}_indt", "nn", h, w_dt, F32)
    mix = _gmlp_fwd(f"{tag}_gmlp", proj, p["lng"], p["lnb"], p["ws"], p["bs"])
    ypre, xc = _conv_fwd(f"{tag}_conv", proj, p["cw"], p["cb"])
    mix, prevs = _ssd_fwd(f"{tag}_ssd", xc, dtr, proj, p["dtb"], p["alog"], p["dsk"], p["nrm"], mix)
    x_new, h_next = _out_proj(f"{tag}_out", mix, Op(wout4.reshape(1, 1, 2 * D, D), "r"), x, next_gain)
    return x_new, h_next, (x, gain, h, proj, dtr, mix, ypre, xc, prevs)


def _even_bwd(tag, saved, dx, dxb, w_main, w_dt, p, wout4, put):
    x, gain, h, proj, dtr, mix, ypre, xc, prevs = saved
    T = x.shape[0]
    dwout = _mm(f"{tag}_dwout", "tn", mix, dxb, BF16, out=("r", 4, 1, 0), tm_t=1024, tn_t=256, tk_t=8192)
    dmix = _mm(f"{tag}_dmix", "nt", dxb, Op(wout4, "r", 0), BF16)
    dproj = lax.empty((T, EVEN_MAIN), BF16)
    dproj, dlng, dlnb, dws, dbs = _gmlp_bwd(f"{tag}_dgmlp", proj, p["lng"], p["lnb"], p["ws"], p["bs"], dmix, dproj)
    dproj, dxc, ddtr, ddtb, dalog, ddsk, dnrm = _ssd_bwd(
        f"{tag}_dssd", xc, dtr, proj, prevs, p["dtb"], p["alog"], p["dsk"], p["nrm"], dmix, dproj)
    dproj, dcw, dcb = _conv_bwd(f"{tag}_dconv", proj, ypre, dxc, p["cw"], dproj)
    dw_main = _mm(f"{tag}_dwin", "tn", h, dproj, BF16, tm_t=1024, tn_t=256, tk_t=8192)
    dw_dt = _mm(f"{tag}_dwdt", "tn", h, ddtr, BF16)
    tok = put(dw_main, dw_dt, dwout)
    dh = _mm(f"{tag}_dh1", "nt", ddtr, w_dt + tok.astype(BF16), F32)
    dh = _mm(f"{tag}_dh2", "nt", dproj, w_main, BF16, res=dh)
    dx, dxb, dgain = _rms_bwd(f"{tag}_dnorm", x, gain, dh, dx)
    small = dict(lng=dlng, lnb=dlnb, ws=dws, bs=dbs, cw=dcw, cb=dcb, dtb=ddtb, alog=dalog, dsk=ddsk, nrm=dnrm)
    return dx, dxb, dgain, small


def _odd_fwd(tag, x, h, gain, wqkv4, bqkv, snk, wo4, cos, sin, next_gain):
    qkv = _mm(f"{tag}_qkv", "nn", h, Op(wqkv4, "c", 0), F32, bias=bqkv, tn_t=640)
    qkvr = _rope_fwd(f"{tag}_rope", qkv, cos, sin)
    o = _swa_fwd(f"{tag}_swa", qkvr, snk)
    x_new, h_next = _out_proj(f"{tag}_o", o, Op(wo4.reshape(1, 1, D, D), "r"), x, next_gain)
    return x_new, h_next, (x, gain, h, qkvr, o)


def _odd_bwd(tag, saved, dx, dxb, wqkv4, snk, wo4, cos, sin, put):
    x, gain, h, qkvr, o = saved
    dwo = _mm(f"{tag}_dwo", "tn", o, dxb, BF16, out=("r", 4, 1, 0))
    do = _mm(f"{tag}_do", "nt", dxb, Op(wo4, "r", 0), BF16)
    dq, dcur, dprv, dsnk = _swa_bwd(f"{tag}_dswa", qkvr, snk, do)
    dqkv, dbias = _rope_bwd(f"{tag}_drope", dq, dcur, dprv, cos, sin)
    dwqkv = _mm(f"{tag}_dwqkv", "tn", h, dqkv, BF16, out=("c", 4, 1, 0), tn_t=640)
    tok = put(dwqkv, dwo)
    dh = _mm(f"{tag}_dh", "nt", dqkv, Op(wqkv4, "c", 0), BF16, bias=jnp.zeros((1, D), F32) + tok, tk_t=640)
    dx, dxb, dgain = _rms_bwd(f"{tag}_dnorm", x, gain, dh, dx)
    return dx, dxb, dgain, dbias, dsnk


def _row(v):
    return v.reshape(1, -1).astype(F32)


def _pad_lanes(v, n=LANE):
    v = v.reshape(1, -1).astype(F32)
    return jnp.pad(v, ((0, 0), (0, n - v.shape[1])))


def _local_step(x, mem, positions, target, getw, P, putg):
    inv_freq = ROPE_THETA ** (-jnp.arange(0, ROT, 2, dtype=F32) / ROT)
    ang = positions.astype(F32)[:, None] * inv_freq
    cos8, sin8 = jnp.cos(ang), jnp.sin(ang)
    ones, zeros = jnp.ones((x.shape[0], AHD - ROT), F32), jnp.zeros((x.shape[0], AHD - ROT), F32)
    cos = jnp.tile(jnp.concatenate([cos8, cos8, ones], 1), (1, 2))
    sin = jnp.tile(jnp.concatenate([-sin8, sin8, zeros], 1), (1, 2))

    snk = _pad_lanes(P["sinks"])
    W = {}

    def w(name, layer, after):
        if (name, layer) not in W:
            W[name, layer] = getw(name, layer, after)
        return W[name, layer]

    saved = []
    h = _rms_fwd("l0_ffn1_norm", x, _row(P["norm_ffn1"][0]))
    for i in range(2):
        x, h, s1 = _ffn_fwd(f"l{i}_ffn1", x, h, _row(P["norm_ffn1"][i]), w("w_ffn1_gu", i, x),
                            functools.partial(w, "w_ffn1_down", i), _row(P["norm_mix"][i]))
        if i == 0:
            ev = dict(lng=_row(P["gm_ln_g"]), lnb=_row(P["gm_ln_b"]), ws=P["gm_ws"].reshape(GM_G, CH, CH),
                      bs=P["gm_bs"].reshape(GM_G, CH, 1), cw=w("conv_w", 0, x), cb=_row(P["conv_b"]),
                      dtb=_pad_lanes(P["dt_bias"]), alog=_pad_lanes(P["a_log"]), dsk=_pad_lanes(P["d_skip"]),
                      nrm=_row(P["ssd_norm"]))
            w_main, w_dt = w("w_in_even", 0, x)
            x, h, s2 = _even_fwd("l0_mix", x, h, _row(P["norm_mix"][0]), w_main, w_dt, ev, w("w_out_even", 0, x),
                                 _row(P["norm_xq"][0]))
        else:
            x, h, s2 = _odd_fwd("l1_mix", x, h, _row(P["norm_mix"][1]), w("w_qkv", 0, x), w("b_qkv", 0, x), snk,
                                w("w_o_odd", 0, x), cos, sin, _row(P["norm_xq"][1]))
        x, h, s3 = _xattn_fwd(f"l{i}_xat", x, h, mem, _row(P["norm_xq"][i]), _row(P["norm_mem"][i]),
                              w("w_xq", i, x), w("w_xkv", i, x), w("w_xo", i, x), _row(P["norm_ffn2"][i]))
        x, h, s4 = _ffn_fwd(f"l{i}_ffn2", x, h, _row(P["norm_ffn2"][i]), w("w_ffn2_gu", i, x),
                            functools.partial(w, "w_ffn2_down", i), _row(P["norm_ffn1"][1]) if i == 0 else None)
        saved.append((s1, s2, s3, s4))

    loss, dx, dxb, d_final = _loss_head("loss_head", x, _row(P["final_norm"]), target)

    sm = {}
    dn = {k: [None, None] for k in ("norm_ffn1", "norm_mix", "norm_xq", "norm_mem", "norm_ffn2")}
    for i in (1, 0):
        s1, s2, s3, s4 = saved[i]
        dx, dxb, dn["norm_ffn2"][i] = _ffn_bwd(
            f"l{i}_ffn2", s4, dx, dxb, W["w_ffn2_gu", i], W["w_ffn2_down", i],
            lambda dwgu, dwd, i=i: putg({("w_ffn2_gu", i): dwgu, ("w_ffn2_down", i): dwd}))
        dx, dxb, dn["norm_xq"][i], dn["norm_mem"][i] = _xattn_bwd(
            f"l{i}_xat", s3, dx, dxb, W["w_xq", i], W["w_xkv", i], W["w_xo", i],
            lambda dwxq, dwxkv, dwxo, i=i: putg({("w_xq", i): dwxq, ("w_xkv", i): dwxkv, ("w_xo", i): dwxo}))
        if i == 0:
            dx, dxb, dn["norm_mix"][0], sm_even = _even_bwd(
                "l0_mix", s2, dx, dxb, w_main, w_dt, ev, W["w_out_even", 0],
                lambda dw_main, dw_dt, dwout: putg({("w_in_even", 0): (dw_main, dw_dt), ("w_out_even", 0): dwout}))
        else:
            dx, dxb, dn["norm_mix"][1], sm["b_qkv"], sm["sinks"] = _odd_bwd(
                "l1_mix", s2, dx, dxb, W["w_qkv", 0], snk, W["w_o_odd", 0], cos, sin,
                lambda dwqkv, dwo: putg({("w_qkv", 0): dwqkv, ("w_o_odd", 0): dwo}))
        dx, dxb, dn["norm_ffn1"][i] = _ffn_bwd(
            f"l{i}_ffn1", s1, dx, dxb, W["w_ffn1_gu", i], W["w_ffn1_down", i],
            lambda dwgu, dwd, i=i: putg({("w_ffn1_gu", i): dwgu, ("w_ffn1_down", i): dwd}))
    for k, v in dn.items():
        sm[k] = jnp.concatenate(v, 0)
    sm.update(gm_ln_g=sm_even["lng"], gm_ln_b=sm_even["lnb"], gm_ws=sm_even["ws"], gm_bs=sm_even["bs"],
              conv_w=sm_even["cw"], conv_b=sm_even["cb"], dt_bias=sm_even["dtb"][:, :NH], a_log=sm_even["alog"][:, :NH],
              d_skip=sm_even["dsk"][:, :NH], ssd_norm=sm_even["nrm"], sinks=sm["sinks"][:, :AH], final_norm=d_final)
    return loss[0, 0], dx, sm


def _chip_peers():
    x, y, c = lax.axis_index("x"), lax.axis_index("y"), lax.axis_index("c")
    return 2 * x + y, [((1 - x, y, c), 2 * (1 - x) + y), ((x, 1 - y, c), 2 * x + (1 - y)),
                       ((1 - x, 1 - y, c), 2 * (1 - x) + (1 - y))]


def _any_specs(n):
    return [pl.BlockSpec(memory_space=pl.ANY)] * n


_HBM = pl.BlockSpec(memory_space=pltpu.HBM)
_SEM = pl.BlockSpec(memory_space=pltpu.SEMAPHORE)
_EFFECT = pltpu.SideEffectType.DATAFLOW_SIDE_EFFECTING


def _own_slot(piece, chip):
    zone = lax.empty((4,) + piece.shape, piece.dtype)
    return lax.dynamic_update_slice(zone, piece[None], (chip,) + (0,) * piece.ndim)


def _chip_copies(srcs, lands, ssems, rsems, sibling=False):
    if sibling:
        dev = (lax.axis_index("x"), lax.axis_index("y"), 1 - lax.axis_index("c"))
        return [pltpu.make_async_remote_copy(src_ref=srcs[i], dst_ref=lands[i], send_sem=ssems[i].at[0],
                                             recv_sem=rsems[i].at[0], device_id=dev, device_id_type=MESH)
                for i in range(len(lands))]
    me, peers = _chip_peers()
    return [pltpu.make_async_remote_copy(src_ref=lands[i].at[me] if srcs[i] is None else srcs[i].at[chip],
                                         dst_ref=lands[i].at[me], send_sem=ssems[i].at[j], recv_sem=rsems[i].at[j],
                                         device_id=dev, device_id_type=MESH)
            for i in range(len(lands)) for j, (dev, chip) in enumerate(peers)]


def _exchange_start(name, srcs, lands, sibling=False):
    n = len(lands)
    ns = 0 if srcs is None else n

    def body(*refs):
        src_refs = [None] * n if srcs is None else refs[:n]
        land_refs = refs[ns:ns + n]
        ssems, rsems = refs[ns + n:ns + 2 * n], refs[ns + 2 * n:ns + 3 * n]
        token = refs[2 * ns + 4 * n]
        for cp in _chip_copies(src_refs, land_refs, ssems, rsems, sibling):
            cp.start()
        token[...] = jnp.zeros_like(token)

    ins = ([] if srcs is None else list(srcs)) + list(lands)
    res = pl.pallas_call(
        body, name=name,
        out_shape=[pltpu.SemaphoreType.DMA((1 if sibling else 3,))] * (2 * n) + [pltpu.HBM(a.shape, a.dtype) for a in ins]
        + [S((8, LANE), F32)],
        in_specs=[_HBM] * (ns + n),
        out_specs=[_SEM] * (2 * n) + [_HBM] * (ns + n) + [pl.BlockSpec(memory_space=pltpu.VMEM)],
        input_output_aliases={i: 2 * n + i for i in range(ns + n)},
        compiler_params=pltpu.CompilerParams(has_side_effects=_EFFECT),
    )(*[pltpu.with_memory_space_constraint(a, pltpu.HBM) for a in ins])
    items = [(res[i], res[n + i], None if srcs is None else res[2 * n + i], res[2 * n + ns + i]) for i in range(n)]
    return items, res[2 * n + ns + n][0, 0]


def _exchange_wait(name, item, after, sibling=False):
    ssem, rsem, src, land = item
    ns = 0 if src is None else 1

    def body(*refs):
        src_ref = refs[0] if ns else None
        land_ref, ssem_ref, rsem_ref = refs[ns], refs[ns + 1], refs[ns + 2]
        for cp in _chip_copies([src_ref], [land_ref], [ssem_ref], [rsem_ref], sibling):
            cp.wait_send()
            cp.wait_recv()

    ins = ([src] if ns else []) + [land]
    return pl.pallas_call(
        body, name=name, out_shape=[pltpu.HBM(a.shape, a.dtype) for a in ins],
        in_specs=[_HBM] * (ns + 1) + [_SEM, _SEM, pl.BlockSpec(memory_space=pl.ANY)], out_specs=[_HBM] * (ns + 1),
        input_output_aliases={i: i for i in range(ns + 1)}, compiler_params=pltpu.CompilerParams(has_side_effects=_EFFECT),
    )(*ins, ssem, rsem, after)[ns]


def _gather_all(name, v, after):
    def body(v_ref, _, o_ref, ssem, rsem, lsem):
        x, y, c = lax.axis_index("x"), lax.axis_index("y"), lax.axis_index("c")
        me = 4 * x + 2 * y + c
        loc = pltpu.make_async_copy(v_ref, o_ref.at[me], lsem)
        loc.start()
        copies = []
        for k in range(1, 8):
            fx, fy, fc = (k >> 2) & 1, (k >> 1) & 1, k & 1
            dev = (x ^ fx, y ^ fy, c ^ fc)
            cp = pltpu.make_async_remote_copy(src_ref=v_ref, dst_ref=o_ref.at[me], send_sem=ssem.at[k - 1],
                                              recv_sem=rsem.at[k - 1], device_id=dev, device_id_type=MESH)
            cp.start()
            copies.append(cp)
        for cp in copies:
            cp.wait()
        loc.wait()

    return pl.pallas_call(
        body, in_specs=_any_specs(2), out_specs=pl.BlockSpec(memory_space=pl.ANY), out_shape=S((8,) + v.shape, v.dtype),
        scratch_shapes=[pltpu.SemaphoreType.DMA((7,)), pltpu.SemaphoreType.DMA((7,)), pltpu.SemaphoreType.DMA(())],
        compiler_params=pltpu.CompilerParams(has_side_effects=True), name=name)(v, after)


def _row_tile(R, row_bytes, budget=4 << 20):
    if R * row_bytes <= budget or R % 16:
        return R
    t = max(16, budget // row_bytes // 16 * 16)
    while R % t:
        t -= 16
    return t


def _sum_slots(name, r, n):
    _, R, C = r.shape
    tr = _row_tile(R, C * (n * r.dtype.itemsize + 4))

    def body(r_ref, o_ref):
        acc = r_ref[0].astype(F32)
        for j in range(1, n):
            acc = acc + r_ref[j].astype(F32)
        o_ref[...] = acc

    return pl.pallas_call(
        body, grid=(R // tr,), in_specs=[pl.BlockSpec((n, tr, C), lambda i: (0, i, 0))],
        out_specs=pl.BlockSpec((tr, C), lambda i: (i, 0)), out_shape=S((R, C), F32),
        compiler_params=_params("parallel"), name=name)(r)


def _adamw(name, w, m, v, layer, g1, g2=None, into=None):
    nl, R, C = w.shape
    tr = _row_tile(R, C * 4 * 9)
    two, has_into = g2 is not None, into is not None

    def body(w_ref, m_ref, v_ref, g1_ref, *rest):
        rest = list(rest)
        g = g1_ref[...]
        if two:
            g = g + rest.pop(0)[...]
        g_ref, d_ref, nm_ref, nv_ref = rest[-4:]
        mn = B1 * m_ref[...] + (1.0 - B1) * g
        vn = B2 * v_ref[...] + (1.0 - B2) * jnp.square(g)
        m_hat = mn / (1.0 - B1 ** STEP)
        v_hat = vn / (1.0 - B2 ** STEP)
        g_ref[...] = g
        d_ref[...] = -LR * (m_hat / (jnp.sqrt(v_hat) + AEPS) + WD * w_ref[...])
        nm_ref[...] = mn
        nv_ref[...] = vn

    blk = pl.BlockSpec((tr, C), lambda i: (i, 0))
    lay = pl.BlockSpec((None, tr, C), lambda i: (layer, i, 0))
    args = [w, m, v, g1] + ([g2] if two else []) + (list(into) if has_into else [])
    in_specs = [lay] * 3 + [blk] * (2 if two else 1) + (_any_specs(4) if has_into else [])
    aliases = {len(args) - 4 + t: t for t in range(4)} if has_into else {}
    return pl.pallas_call(
        body, grid=(R // tr,), in_specs=in_specs, out_specs=[lay] * 4, out_shape=[S((nl, R, C), F32)] * 4,
        input_output_aliases=aliases, compiler_params=_params("parallel"), name=name)(*args)


_USE_ORDER = [("w_ffn1_gu", 0), ("w_ffn1_down", 0), ("conv_w", 0), ("w_in_even", 0), ("w_out_even", 0), ("w_xq", 0),
              ("w_xkv", 0), ("w_xo", 0), ("w_ffn2_gu", 0), ("w_ffn2_down", 0), ("w_ffn1_gu", 1), ("w_ffn1_down", 1),
              ("w_qkv", 0), ("b_qkv", 0), ("w_o_odd", 0), ("w_xq", 1), ("w_xkv", 1), ("w_xo", 1), ("w_ffn2_gu", 1),
              ("w_ffn2_down", 1)]
_SMALL = ["norm_ffn1", "norm_mix", "gm_ln_g", "gm_ln_b", "gm_ws", "gm_bs", "conv_w", "conv_b", "dt_bias", "a_log",
          "d_skip", "ssd_norm", "b_qkv", "sinks", "norm_xq", "norm_mem", "norm_ffn2", "final_norm"]
_WEIGHTS = ["norm_ffn1", "w_ffn1_gu", "w_ffn1_down", "norm_mix", "w_in_even", "gm_ln_g", "gm_ln_b", "gm_ws", "gm_bs",
            "conv_w", "conv_b", "dt_bias", "a_log", "d_skip", "ssd_norm", "w_out_even", "w_qkv", "b_qkv", "sinks",
            "w_o_odd", "norm_xq", "norm_mem", "w_xq", "w_xkv", "w_xo", "norm_ffn2", "w_ffn2_gu", "w_ffn2_down",
            "final_norm"]


def _pack(arrs):
    rows = []
    for a in arrs:
        f = a.reshape(-1).astype(F32)
        pad = (-f.shape[0]) % LANE
        rows.append(jnp.pad(f, (0, pad)).reshape(-1, LANE))
    out = jnp.concatenate(rows, 0)
    pad = (-out.shape[0]) % 8
    return jnp.pad(out, ((0, pad), (0, 0)))


def _unpack(packed, shapes):
    outs, r = [], 0
    for shp in shapes:
        n = math.prod(shp)
        nr = -(-n // LANE)
        outs.append(packed[r:r + nr].reshape(-1)[:n].reshape(shp))
        r += nr
    return outs


def kernel(x, mem, positions, norm_ffn1, w_ffn1_gu, w_ffn1_down, norm_mix, w_in_even, gm_ln_g, gm_ln_b, gm_ws, gm_bs, conv_w, conv_b, dt_bias, a_log, d_skip, ssd_norm, w_out_even, w_qkv, b_qkv, sinks, w_o_odd, norm_xq, norm_mem, w_xq, w_xkv, w_xo, norm_ffn2, w_ffn2_gu, w_ffn2_down, final_norm, loss_target, m_norm_ffn1, m_w_ffn1_gu, m_w_ffn1_down, m_norm_mix, m_w_in_even, m_gm_ln_g, m_gm_ln_b, m_gm_ws, m_gm_bs, m_conv_w, m_conv_b, m_dt_bias, m_a_log, m_d_skip, m_ssd_norm, m_w_out_even, m_w_qkv, m_b_qkv, m_sinks, m_w_o_odd, m_norm_xq, m_norm_mem, m_w_xq, m_w_xkv, m_w_xo, m_norm_ffn2, m_w_ffn2_gu, m_w_ffn2_down, m_final_norm, v_norm_ffn1, v_w_ffn1_gu, v_w_ffn1_down, v_norm_mix, v_w_in_even, v_gm_ln_g, v_gm_ln_b, v_gm_ws, v_gm_bs, v_conv_w, v_conv_b, v_dt_bias, v_a_log, v_d_skip, v_ssd_norm, v_w_out_even, v_w_qkv, v_b_qkv, v_sinks, v_w_o_odd, v_norm_xq, v_norm_mem, v_w_xq, v_w_xkv, v_w_xo, v_norm_ffn2, v_w_ffn2_gu, v_w_ffn2_down, v_final_norm):
    a = dict(locals())
    w = {k: a[k] for k in _WEIGHTS}
    mom = {k: a["m_" + k] for k in _WEIGHTS}
    var = {k: a["v_" + k] for k in _WEIGHTS}
    chip = 2 * lax.axis_index("x") + lax.axis_index("y")

    shards = [w[k][i:i + 1] if k in ("conv_w", "b_qkv") else w[k][i:i + 1].astype(BF16) for k, i in _USE_ORDER]
    items, _ = _exchange_start("gather_start", None, [_own_slot(s, chip) for s in shards])
    pending = dict(zip(_USE_ORDER, items))

    def getw(name, layer, after):
        got = _exchange_wait(f"gather_wait_{name}_{layer}", pending.pop((name, layer)), after)
        if name == "w_in_even":
            w_in = jnp.transpose(got[:, 0], (1, 0, 2)).reshape(D, EVEN_IN)
            return w_in[:, :EVEN_MAIN], jnp.pad(w_in[:, EVEN_MAIN:], ((0, 0), (0, LANE - (EVEN_IN - EVEN_MAIN))))
        if name == "conv_w":
            return jnp.transpose(got[:, 0], (1, 0, 2)).reshape(4, CONV_C)
        if name == "b_qkv":
            return got.reshape(1, ODD_IN)
        return got

    sent = []

    def putg(grads):
        names, arrs = [], []
        for (name, layer), g in grads.items():
            if name == "w_in_even":
                dw_in = jnp.concatenate([g[0], g[1][:, :EVEN_IN - EVEN_MAIN]], 1)
                g = jnp.transpose(dw_in.reshape(D, 4, EVEN_IN // 4), (1, 0, 2)).reshape(4, 1, D, EVEN_IN // 4)
            names.append((name, layer))
            arrs.append(g)
        own = [_own_slot(lax.dynamic_index_in_dim(g, chip, 0, keepdims=False), chip) for g in arrs]
        its, tok = _exchange_start(f"scatter_start_{names[0][0]}_{names[0][1]}", arrs, own)
        sent.append(list(zip(names, its)))
        return tok

    P = {k: w[k] for k in _SMALL}
    loss, grad_x, sm = _local_step(x[0], mem[0], positions[0], loss_target[0], getw, P, putg)
    loss = lax.psum(loss, ("x", "y", "c"))

    out = {}

    def update(groups, after):
        flying = []
        for grp in groups:
            part = []
            for (name, layer), it in grp:
                r = _exchange_wait(f"scatter_wait_{name}_{layer}", it, after)
                part.append(_sum_slots(f"sum_{name}_{layer}", r.reshape(4, -1, r.shape[-1]), 4))
            name0, layer0 = grp[0][0]
            its, _ = _exchange_start(f"swap_start_{name0}_{layer0}", part, [lax.empty(p.shape, p.dtype) for p in part], True)
            flying += [(nm, p, it) for (nm, _), p, it in zip(grp, part, its)]
        for (name, layer), p1, it in flying:
            p2 = _exchange_wait(f"swap_wait_{name}_{layer}", it, after, True)
            out[name] = _adamw(f"adamw_{name}_{layer}", w[name], mom[name], var[name], layer, p1, p2, out.get(name))
            after = out[name][0]

    update(sent[:-1], grad_x)
    done_a = out["w_out_even"][0]
    update(sent[-1:], done_a)

    full_shapes = {k: w[k].shape for k in _SMALL}
    full_shapes["conv_w"], full_shapes["b_qkv"] = (1, 4, CONV_C), (1, ODD_IN)
    packed = _pack([sm[k] for k in _SMALL])
    total = _sum_slots("sum_small", _gather_all("gather_small", packed, done_a), 8)
    gs = dict(zip(_SMALL, _unpack(total, [full_shapes[k] for k in _SMALL])))
    gs["conv_w"] = lax.dynamic_slice_in_dim(gs["conv_w"], chip * (CONV_C // 4), CONV_C // 4, 2)
    gs["b_qkv"] = lax.dynamic_slice_in_dim(gs["b_qkv"], chip * (ODD_IN // 4), ODD_IN // 4, 1)
    res = _adamw("adamw_small", _pack([w[k] for k in _SMALL])[None], _pack([mom[k] for k in _SMALL])[None],
                 _pack([var[k] for k in _SMALL])[None], 0, _pack([gs[k] for k in _SMALL]))
    shapes = [w[k].shape for k in _SMALL]
    for k, g, d, nm, nv in zip(_SMALL, *[_unpack(r[0], shapes) for r in res]):
        out[k] = [g, d, nm, nv]

    return (loss, grad_x[None], *[out[k][0] for k in _WEIGHTS], *[out[k][1] for k in _WEIGHTS],
            *[out[k][2] for k in _WEIGHTS], *[out[k][3] for k in _WEIGHTS])
```

```python
import functools
import math

import jax
import jax.numpy as jnp
from jax import lax
from jax.experimental import pallas as pl
from jax.experimental.pallas import tpu as pltpu

F32, BF16 = jnp.float32, jnp.bfloat16
S = jax.ShapeDtypeStruct
MESH = pl.DeviceIdType.MESH

D = 2048
DFF = 5632
EPS = 1e-5
CH = 128
GM_G, GM_GD = 4, 512
NH, HD, NG, HPG, NS = 32, 64, 4, 8, 128
CONV_C = 3072
EVEN_MAIN, EVEN_IN = 9216, 9248
AH, AKV, AREP, AHD = 32, 4, 8, 64
ODD_IN = 2560
XH, XHD, XW = 4, 128, 512
ATT_SCALE = AHD ** -0.5
X_SCALE = XHD ** -0.5
ROPE_THETA = 500000.0
ROT = 16
LR, B1, B2, AEPS, WD, STEP = 0.001, 0.9, 0.999, 1e-08, 0.01, 10
LANE = 128
VMEM_LIMIT_V7X = 56 * 1024 * 1024
SUB_COLS = 256


def _params(*sem):
    return pltpu.CompilerParams(dimension_semantics=sem, vmem_limit_bytes=VMEM_LIMIT_V7X)


def _tile(dim, target):
    if dim <= target:
        return dim
    t = (target // LANE) * LANE
    while t > LANE and dim % t:
        t -= LANE
    assert dim % t == 0, (dim, target)
    return t


class Op:
    def __init__(self, arr, kind=None, layer=0):
        self.arr, self.kind, self.layer = arr, kind, layer
        if kind is None:
            self.R, self.C = arr.shape
        else:
            L = arr.shape[0]
            self.R = arr.shape[2] * (L if kind == "r" else 1)
            self.C = arr.shape[3] * (L if kind == "c" else 1)

    def unit(self, axis):
        if self.kind == "r" and axis == 0:
            return self.arr.shape[2]
        if self.kind == "c" and axis == 1:
            return self.arr.shape[3]
        return (self.R, self.C)[axis]

    def spec(self, tr, tc, pick):
        if self.kind is None:
            return pl.BlockSpec((tr, tc), lambda i, j, k: pick(i, j, k))
        l = self.layer
        if self.kind == "c":
            per = self.arr.shape[3] // tc
            return pl.BlockSpec((None, None, tr, tc),
                                lambda i, j, k: (pick(i, j, k)[1] // per, l, pick(i, j, k)[0], pick(i, j, k)[1] % per))
        per = self.arr.shape[2] // tr
        return pl.BlockSpec((None, None, tr, tc),
                            lambda i, j, k: (pick(i, j, k)[0] // per, l, pick(i, j, k)[0] % per, pick(i, j, k)[1]))


_DIMS = {"nn": (((1,), (0,)), ((), ())), "nt": (((1,), (1,)), ((), ())), "tn": (((0,), (0,)), ((), ()))}
_PICK_A = {"nn": lambda i, j, k: (i, k), "nt": lambda i, j, k: (i, k), "tn": lambda i, j, k: (k, i)}
_PICK_B = {"nn": lambda i, j, k: (k, j), "nt": lambda i, j, k: (j, k), "tn": lambda i, j, k: (k, j)}


def _mm(name, mode, a, b, out_dtype, *, out=None, res=None, bias=None, scale=1.0, norm_gain=None,
        tm_t=1024, tn_t=1024, tk_t=2048):
    if not isinstance(a, Op):
        a = Op(a)
    if not isinstance(b, Op):
        b = Op(b)
    if mode == "nn":
        M, K, N = a.R, a.C, b.C
        assert b.R == K
        um, uk, un = a.unit(0), math.gcd(a.unit(1), b.unit(0)), b.unit(1)
    elif mode == "nt":
        M, K, N = a.R, a.C, b.R
        assert b.C == K
        um, uk, un = a.unit(0), math.gcd(a.unit(1), b.unit(1)), b.unit(0)
    else:
        K, M, N = a.R, a.C, b.C
        assert b.R == K
        um, uk, un = a.unit(1), math.gcd(a.unit(0), b.unit(0)), b.unit(1)
    if out is not None:
        okind, oL, olayers, olayer = out
        if okind == "c":
            un = math.gcd(un, N // oL)
        else:
            um = math.gcd(um, M // oL)
    tm, tn, tk = _tile(um, tm_t), _tile(un, tn_t), _tile(uk, tk_t)
    gi, gj, gk = M // tm, N // tn, K // tk
    a_blk = (tm, tk) if mode != "tn" else (tk, tm)
    b_blk = {"nn": (tk, tn), "nt": (tn, tk), "tn": (tk, tn)}[mode]
    in_specs = [a.spec(*a_blk, _PICK_A[mode]), b.spec(*b_blk, _PICK_B[mode])]
    args = [a.arr, b.arr]
    if res is not None:
        in_specs.append(pl.BlockSpec((tm, tn), lambda i, j, k: (i, j)))
        args.append(res)
    if bias is not None:
        in_specs.append(pl.BlockSpec((1, tn), lambda i, j, k: (0, j)))
        args.append(bias)
    if out is None:
        out_shape = S((M, N), out_dtype)
        out_spec = pl.BlockSpec((tm, tn), lambda i, j, k: (i, j))
    else:
        shp = (oL, olayers, M, N // oL) if okind == "c" else (oL, olayers, M // oL, N)
        out_shape = S(shp, out_dtype)
        out_spec = Op(out_shape, okind, olayer).spec(tm, tn, lambda i, j, k: (i, j))
    has_res, has_bias, has_norm = res is not None, bias is not None, norm_gain is not None
    if has_norm:
        assert out is None and tn == N
        in_specs.append(pl.BlockSpec((1, N), lambda i, j, k: (0, 0)))
        args.append(norm_gain)
        out_shape = [out_shape, S((M, N), BF16)]
        out_spec = [out_spec, pl.BlockSpec((tm, tn), lambda i, j, k: (i, j))]
    dims = _DIMS[mode]

    def body(a_ref, b_ref, *rest):
        rest = list(rest)
        res_ref = rest.pop(0) if has_res else None
        bias_ref = rest.pop(0) if has_bias else None
        gain_ref = rest.pop(0) if has_norm else None
        o_ref = rest.pop(0)
        h_ref = rest.pop(0) if has_norm else None
        part = lax.dot_general(a_ref[...].astype(BF16), b_ref[...].astype(BF16), dims, preferred_element_type=F32)

        def finish(r):
            if scale != 1.0:
                r = r * scale
            if has_bias:
                r = r + bias_ref[...]
            if has_res:
                r = r + res_ref[...]
            o_ref[...] = r.astype(o_ref.dtype)
            if has_norm:
                h_ref[...] = (r * lax.rsqrt(jnp.mean(r * r, -1, keepdims=True) + EPS) * gain_ref[...]).astype(BF16)

        if gk == 1:
            finish(part)
            return
        acc, = rest
        k = pl.program_id(2)

        @pl.when(k == 0)
        def _():
            acc[...] = part

        @pl.when((k > 0) & (k < gk - 1))
        def _():
            acc[...] += part

        @pl.when(k == gk - 1)
        def _():
            finish(acc[...] + part)

    return pl.pallas_call(
        body, grid=(gi, gj, gk), in_specs=in_specs, out_specs=out_spec, out_shape=out_shape,
        scratch_shapes=[pltpu.VMEM((tm, tn), F32)] if gk > 1 else [],
        compiler_params=_params("parallel", "parallel", "arbitrary"), name=name)(*args)


def _rms_fwd(name, x, gain):
    T = x.shape[0]
    tt = _tile(T, 512)

    def body(x_ref, g_ref, o_ref):
        xv = x_ref[...]
        r = lax.rsqrt(jnp.mean(xv * xv, -1, keepdims=True) + EPS)
        o_ref[...] = (xv * r * g_ref[...]).astype(BF16)

    return pl.pallas_call(
        body, grid=(T // tt,),
        in_specs=[pl.BlockSpec((tt, D), lambda i: (i, 0)), pl.BlockSpec((1, D), lambda i: (0, 0))],
        out_specs=pl.BlockSpec((tt, D), lambda i: (i, 0)), out_shape=S((T, D), BF16),
        compiler_params=_params("parallel"), name=name)(x, gain)


def _rms_bwd(name, x, gain, dh, dx_in=None):
    T = x.shape[0]
    tt = _tile(T, 512)
    has_in = dx_in is not None

    def body(x_ref, g_ref, dh_ref, *rest):
        rest = list(rest)
        dxin_ref = rest.pop(0) if has_in else None
        dx_ref, dxb_ref, dg_ref = rest
        xv = x_ref[...]
        r = lax.rsqrt(jnp.mean(xv * xv, -1, keepdims=True) + EPS)
        xh = xv * r
        dy = dh_ref[...].astype(F32)
        dxh = dy * g_ref[...]
        dx = r * (dxh - xh * jnp.mean(dxh * xh, -1, keepdims=True))
        if has_in:
            dx = dx + dxin_ref[...]
        dx_ref[...] = dx
        dxb_ref[...] = dx.astype(BF16)
        part = jnp.sum(dy * xh, 0, keepdims=True)

        @pl.when(pl.program_id(0) == 0)
        def _():
            dg_ref[...] = part

        @pl.when(pl.program_id(0) > 0)
        def _():
            dg_ref[...] += part

    row = pl.BlockSpec((tt, D), lambda i: (i, 0))
    vec = pl.BlockSpec((1, D), lambda i: (0, 0))
    args = [x, gain, dh] + ([dx_in] if has_in else [])
    return pl.pallas_call(
        body, grid=(T // tt,), in_specs=[row, vec, row] + ([row] if has_in else []),
        out_specs=[row, row, vec], out_shape=[S((T, D), F32), S((T, D), BF16), S((1, D), F32)],
        compiler_params=_params("arbitrary"), name=name)(*args)


def _sigmoid(x):
    return 0.5 * jnp.tanh(0.5 * x) + 0.5


def _ffn_up(name, h, w4, layer, tm_t=512, tn_t=1408, chunk=SUB_COLS):
    T = h.shape[0]
    n_sh = w4.shape[3]
    tm, tn = _tile(T, tm_t), _tile(n_sh, tn_t)
    per = n_sh // tn

    def body(h_ref, wg_ref, wu_ref, g_ref, u_ref, a_ref):
        hv = h_ref[...]
        for c0 in range(0, tn, chunk):
            cols = slice(c0, min(c0 + chunk, tn))
            g = jnp.dot(hv, wg_ref[:, cols], preferred_element_type=F32)
            u = jnp.dot(hv, wu_ref[:, cols], preferred_element_type=F32)
            g_ref[:, cols] = g.astype(BF16)
            u_ref[:, cols] = u.astype(BF16)
            a_ref[:, cols] = (g * _sigmoid(g) * u).astype(BF16)

    o = pl.BlockSpec((tm, tn), lambda j, i: (i, j))
    return pl.pallas_call(
        body, grid=(DFF // tn, T // tm),
        in_specs=[pl.BlockSpec((tm, D), lambda j, i: (i, 0)),
                  pl.BlockSpec((None, None, D, tn), lambda j, i: (j // per, layer, 0, j % per)),
                  pl.BlockSpec((None, None, D, tn), lambda j, i: (2 + j // per, layer, 0, j % per))],
        out_specs=[o, o, o], out_shape=[S((T, DFF), BF16)] * 3,
        compiler_params=_params("parallel", "parallel"), name=name)(h, w4, w4)


def _ffn_dact(name, dxb, wd4, layer, g, u, tm_t=512, chunk=SUB_COLS):
    T = dxb.shape[0]
    r_sh = wd4.shape[2]
    tm, tn = _tile(T, tm_t), _tile(r_sh, 1408)
    per = r_sh // tn

    def body(dx_ref, w_ref, g_ref, u_ref, o_ref):
        dxv = dx_ref[...]
        for c0 in range(0, tn, chunk):
            cols = slice(c0, min(c0 + chunk, tn))
            da = 0.5 * lax.dot_general(dxv, w_ref[cols, :], _DIMS["nt"], preferred_element_type=F32)
            gv, uv = g_ref[:, cols].astype(F32), u_ref[:, cols].astype(F32)
            sg = _sigmoid(gv)
            o_ref[0, 0, :, cols] = (da * uv * sg * (1.0 + gv * (1.0 - sg))).astype(BF16)
            o_ref[1, 0, :, cols] = (da * gv * sg).astype(BF16)

    t = pl.BlockSpec((tm, tn), lambda j, i: (i, j))
    return pl.pallas_call(
        body, grid=(DFF // tn, T // tm),
        in_specs=[pl.BlockSpec((tm, D), lambda j, i: (i, 0)),
                  pl.BlockSpec((None, None, tn, D), lambda j, i: (j // per, layer, j % per, 0)), t, t],
        out_specs=pl.BlockSpec((2, 1, tm, tn), lambda j, i: (0, 0, i, j)), out_shape=S((2, 1, T, DFF), BF16),
        compiler_params=_params("parallel", "parallel"), name=name)(dxb, wd4, g, u)


def _gelu(x):
    return 0.5 * x * (1.0 + lax.erf(x * 0.7071067811865476))


def _causal(n):
    return lax.broadcasted_iota(jnp.int32, (n, n), 0) >= lax.broadcasted_iota(jnp.int32, (n, n), 1)


def _gmlp_math(u_raw, v_raw, lng, lnb, ws, bs):
    causal = _causal(CH)
    outs = []
    for g in range(GM_G):
        u, v = _gelu(u_raw[g]), _gelu(v_raw[g])
        mu = jnp.mean(v, -1, keepdims=True)
        var = jnp.mean(jnp.square(v - mu), -1, keepdims=True)
        vn = (v - mu) * lax.rsqrt(var + EPS) * lng[g] + lnb[g]
        wm = jnp.where(causal, ws[g], 0.0)
        s = jnp.dot(wm.astype(BF16), vn.astype(BF16), preferred_element_type=F32) + bs[g]
        outs.append(u * s)
    return outs


def _gmlp_load(proj_ref, lng_ref, lnb_ref, ws_ref, bs_ref):
    sl = lambda g, off: slice(off + g * GM_GD, off + (g + 1) * GM_GD)
    u_raw = [proj_ref[:, sl(g, 0)].astype(F32) for g in range(GM_G)]
    v_raw = [proj_ref[:, sl(g, D)].astype(F32) for g in range(GM_G)]
    lng = [lng_ref[:, sl(g, 0)] for g in range(GM_G)]
    lnb = [lnb_ref[:, sl(g, 0)] for g in range(GM_G)]
    ws = [ws_ref[g] for g in range(GM_G)]
    bs = [bs_ref[g] for g in range(GM_G)]
    return u_raw, v_raw, lng, lnb, ws, bs


_GM_PAR = lambda: [pl.BlockSpec((1, D), lambda i: (0, 0)), pl.BlockSpec((1, D), lambda i: (0, 0)),
                   pl.BlockSpec((GM_G, CH, CH), lambda i: (0, 0, 0)), pl.BlockSpec((GM_G, CH, 1), lambda i: (0, 0, 0))]


def _gmlp_fwd(name, proj, lng, lnb, ws, bs):
    T = proj.shape[0]

    def body(proj_ref, lng_ref, lnb_ref, ws_ref, bs_ref, o_ref):
        outs = _gmlp_math(*_gmlp_load(proj_ref, lng_ref, lnb_ref, ws_ref, bs_ref))
        for g in range(GM_G):
            o_ref[:, g * GM_GD:(g + 1) * GM_GD] = outs[g].astype(BF16)

    return pl.pallas_call(
        body, grid=(T // CH,), in_specs=[pl.BlockSpec((CH, 2 * D), lambda i: (i, 0))] + _GM_PAR(),
        out_specs=pl.BlockSpec((CH, D), lambda i: (i, 0)), out_shape=S((T, 2 * D), BF16),
        compiler_params=_params("parallel"), name=name)(proj, lng, lnb, ws, bs)


def _acc_store(first, ref, idx, val):
    @pl.when(first)
    def _():
        ref[idx] = val

    @pl.when(jnp.logical_not(first))
    def _():
        ref[idx] += val


def _gmlp_bwd(name, proj, lng, lnb, ws, bs, dmix, dproj):
    T = proj.shape[0]

    def body(proj_ref, lng_ref, lnb_ref, ws_ref, bs_ref, dmix_ref, _, dproj_ref, dlng_ref, dlnb_ref, dws_ref, dbs_ref):
        first = pl.program_id(0) == 0
        prim = _gmlp_load(proj_ref, lng_ref, lnb_ref, ws_ref, bs_ref)
        _, vjp = jax.vjp(_gmlp_math, *prim)
        du, dv, dlng, dlnb, dws, dbs = vjp([dmix_ref[:, g * GM_GD:(g + 1) * GM_GD].astype(F32) for g in range(GM_G)])
        for g in range(GM_G):
            sl = slice(g * GM_GD, (g + 1) * GM_GD)
            dproj_ref[:, sl] = du[g].astype(BF16)
            dproj_ref[:, D + g * GM_GD:D + (g + 1) * GM_GD] = dv[g].astype(BF16)
            _acc_store(first, dlng_ref, (slice(None), sl), dlng[g])
            _acc_store(first, dlnb_ref, (slice(None), sl), dlnb[g])
            _acc_store(first, dws_ref, g, dws[g])
            _acc_store(first, dbs_ref, g, dbs[g])

    par = _GM_PAR()
    return pl.pallas_call(
        body, grid=(T // CH,),
        in_specs=[pl.BlockSpec((CH, 2 * D), lambda i: (i, 0))] + par +
                 [pl.BlockSpec((CH, D), lambda i: (i, 0)), pl.BlockSpec(memory_space=pl.ANY)],
        out_specs=[pl.BlockSpec((CH, 2 * D), lambda i: (i, 0))] + par,
        out_shape=[S(dproj.shape, BF16), S((1, D), F32), S((1, D), F32), S((GM_G, CH, CH), F32), S((GM_G, CH, 1), F32)],
        input_output_aliases={6: 0}, compiler_params=_params("arbitrary"), name=name)(proj, lng, lnb, ws, bs, dmix, dproj)


CONV_TT = 256
HALO = 8


def _shift_rows(cur, halo_after, s):
    if s == 0:
        return cur
    n = cur.shape[0]
    return pltpu.roll(jnp.concatenate([cur, halo_after], 0), s, 0)[:n]


def _conv_fwd(name, proj, w, b):
    T = proj.shape[0]
    tt = _tile(T, CONV_TT)
    hb = tt // HALO

    def body(x_ref, halo_ref, w_ref, b_ref, y_ref, xc_ref):
        i = pl.program_id(0)
        x = x_ref[...].astype(F32)
        halo = halo_ref[...].astype(F32) * (i > 0).astype(F32)
        y = b_ref[...] + w_ref[3:4, :] * x
        for s in (1, 2, 3):
            y = y + w_ref[3 - s:4 - s, :] * _shift_rows(x, halo, s)
        y_ref[...] = y.astype(BF16)
        xc_ref[...] = (y * _sigmoid(y)).astype(BF16)

    o = pl.BlockSpec((tt, CONV_C), lambda i: (i, 0))
    return pl.pallas_call(
        body, grid=(T // tt,),
        in_specs=[pl.BlockSpec((tt, CONV_C), lambda i: (i, 2)),
                  pl.BlockSpec((HALO, CONV_C), lambda i: (jnp.maximum(i * hb - 1, 0), 2)),
                  pl.BlockSpec((4, CONV_C), lambda i: (0, 0)), pl.BlockSpec((1, CONV_C), lambda i: (0, 0))],
        out_specs=[o, o], out_shape=[S((T, CONV_C), BF16)] * 2,
        compiler_params=_params("parallel"), name=name)(proj, proj, w, b)


def _conv_bwd(name, proj, ypre, dxc, w, dproj):
    T = proj.shape[0]
    tt = _tile(T, CONV_TT)
    hb = tt // HALO
    nt = T // tt

    def dsilu(y):
        sg = _sigmoid(y)
        return sg * (1.0 + y * (1.0 - sg))

    def body(x_ref, xh_ref, y_ref, yn_ref, d_ref, dn_ref, w_ref, _, dproj_ref, dw_ref, db_ref):
        i = pl.program_id(0)
        first = i == 0
        x = x_ref[...].astype(F32)
        halo = xh_ref[...].astype(F32) * (i > 0).astype(F32)
        dy = d_ref[...].astype(F32) * dsilu(y_ref[...].astype(F32))
        dyn = dn_ref[...].astype(F32) * dsilu(yn_ref[...].astype(F32)) * (i < nt - 1).astype(F32)
        ext = jnp.concatenate([dy, dyn], 0)
        dx = w_ref[3:4, :] * dy
        _acc_store(first, dw_ref, (slice(3, 4), slice(None)), jnp.sum(x * dy, 0, keepdims=True))
        for s in (1, 2, 3):
            dx = dx + w_ref[3 - s:4 - s, :] * pltpu.roll(ext, tt + HALO - s, 0)[:tt]
            _acc_store(first, dw_ref, (slice(3 - s, 4 - s), slice(None)),
                       jnp.sum(_shift_rows(x, halo, s) * dy, 0, keepdims=True))
        _acc_store(first, db_ref, (slice(None), slice(None)), jnp.sum(dy, 0, keepdims=True))
        dproj_ref[...] = dx.astype(BF16)

    cur = pl.BlockSpec((tt, CONV_C), lambda i: (i, 0))
    nxt = pl.BlockSpec((HALO, CONV_C), lambda i: (jnp.minimum((i + 1) * hb, T // HALO - 1), 0))
    return pl.pallas_call(
        body, grid=(nt,),
        in_specs=[pl.BlockSpec((tt, CONV_C), lambda i: (i, 2)),
                  pl.BlockSpec((HALO, CONV_C), lambda i: (jnp.maximum(i * hb - 1, 0), 2)),
                  cur, nxt, cur, nxt, pl.BlockSpec((4, CONV_C), lambda i: (0, 0)), pl.BlockSpec(memory_space=pl.ANY)],
        out_specs=[pl.BlockSpec((tt, CONV_C), lambda i: (i, 2)), pl.BlockSpec((4, CONV_C), lambda i: (0, 0)),
                   pl.BlockSpec((1, CONV_C), lambda i: (0, 0))],
        out_shape=[S(dproj.shape, BF16), S((4, CONV_C), F32), S((1, CONV_C), F32)],
        input_output_aliases={7: 0}, compiler_params=_params("arbitrary"), name=name)(proj, proj, ypre, ypre, dxc, dxc, w, dproj)


def _softplus(x):
    return jnp.maximum(x, 0.0) + jnp.log(1.0 + jnp.exp(-jnp.abs(x)))


def _ssd_math(x, Bm, Cm, dtr, z, prev, dtb, alog, dsk, nrm):
    hi = lax.Precision.HIGHEST
    causal = _causal(CH)
    tri = causal.astype(F32)
    lane = lax.broadcasted_iota(jnp.int32, (1, LANE), 1)
    sub = lax.broadcasted_iota(jnp.int32, (LANE, 1), 0)
    dt = _softplus(dtr + dtb)
    a = dt * (-jnp.exp(alog))
    a_cs = jnp.dot(tri, a, preferred_element_type=F32, precision=hi)
    a_csT = lax.dot_general(a, tri, (((0,), (1,)), ((), ())), preferred_element_type=F32, precision=hi)
    a_last = jnp.sum(a, 0, keepdims=True)
    gw = HPG * HD
    outs, new = [], []
    for g in range(NG):
        cb = lax.dot_general(Cm[g].astype(BF16), Bm[g].astype(BF16), _DIMS["nt"], preferred_element_type=F32)
        xg = jnp.concatenate(x[g * HPG:(g + 1) * HPG], 1)
        yd, cols, dts, lasts, dsks, lastr = [], [], [], [], [], []
        for h in range(g * HPG, (g + 1) * HPG):
            ohl = (lane == h).astype(F32)
            col = jnp.sum(a_cs * ohl, 1, keepdims=True)
            row = jnp.sum(a_csT * (sub == h).astype(F32), 0, keepdims=True)
            dtc = jnp.sum(dt * ohl, 1, keepdims=True)
            last = jnp.sum(a_last * ohl, 1, keepdims=True)
            lmat = jnp.where(causal, jnp.exp(jnp.where(causal, col - row, 0.0)), 0.0)
            yd.append(jnp.dot((cb * lmat).astype(BF16), (x[h] * dtc).astype(BF16), preferred_element_type=F32))
            cols.append(jnp.broadcast_to(col, (CH, HD)))
            dts.append(jnp.broadcast_to(dtc, (CH, HD)))
            lasts.append(jnp.broadcast_to(last, (1, HD)))
            dsks.append(jnp.broadcast_to(jnp.sum(dsk * ohl, 1, keepdims=True), (1, HD)))
            lastr.append(jnp.broadcast_to(last, (HD, 1)))
        col_e, dt_e, last_e, dsk_e = (jnp.concatenate(v, 1) for v in (cols, dts, lasts, dsks))
        last_r = jnp.concatenate(lastr, 0)
        y = jnp.concatenate(yd, 1)
        y = y + jnp.exp(col_e) * lax.dot_general(Cm[g].astype(BF16), prev[g].astype(BF16), _DIMS["nt"],
                                                 preferred_element_type=F32)
        st = lax.dot_general((xg * dt_e * jnp.exp(last_e - col_e)).astype(BF16), Bm[g].astype(BF16), _DIMS["tn"],
                             preferred_element_type=F32)
        new.append(prev[g] * jnp.exp(last_r) + st)
        yg = (y + xg * dsk_e) * (z[g] * _sigmoid(z[g]))
        yg = yg * lax.rsqrt(jnp.mean(yg * yg, -1, keepdims=True) + EPS)
        outs.append(yg * nrm[g])
    return outs, new


def _ssd_load(xc_ref, dtr_ref, z_ref, state_ref, dtb_ref, alog_ref, dsk_ref, nrm_ref):
    gw = HPG * HD
    x = [xc_ref[:, h * HD:(h + 1) * HD].astype(F32) for h in range(NH)]
    Bm = [xc_ref[:, D + g * NS:D + (g + 1) * NS].astype(F32) for g in range(NG)]
    Cm = [xc_ref[:, D + NG * NS + g * NS:D + NG * NS + (g + 1) * NS].astype(F32) for g in range(NG)]
    z = [z_ref[:, g * gw:(g + 1) * gw].astype(F32) for g in range(NG)]
    prev = [state_ref[g * gw:(g + 1) * gw, :] for g in range(NG)]
    nrm = [nrm_ref[:, g * gw:(g + 1) * gw] for g in range(NG)]
    return x, Bm, Cm, dtr_ref[...], z, prev, dtb_ref[...], alog_ref[...], dsk_ref[...], nrm


_SSD_PAR = lambda: [pl.BlockSpec((1, LANE), lambda c: (0, 0))] * 3 + [pl.BlockSpec((1, D), lambda c: (0, 0))]


def _ssd_fwd(name, xc, dtr, proj, dtb, alog, dsk, nrm, mix):
    T = xc.shape[0]
    nc = T // CH

    def body(xc_ref, dtr_ref, z_ref, dtb_ref, alog_ref, dsk_ref, nrm_ref, _, mix_ref, prev_ref, state):
        @pl.when(pl.program_id(0) == 0)
        def _():
            state[...] = jnp.zeros_like(state)

        prev_ref[...] = state[...]
        outs, new = _ssd_math(*_ssd_load(xc_ref, dtr_ref, z_ref, state, dtb_ref, alog_ref, dsk_ref, nrm_ref))
        for g in range(NG):
            mix_ref[:, g * 512:(g + 1) * 512] = outs[g].astype(BF16)
            state[g * 512:(g + 1) * 512, :] = new[g]

    return pl.pallas_call(
        body, grid=(nc,),
        in_specs=[pl.BlockSpec((CH, CONV_C), lambda c: (c, 0)), pl.BlockSpec((CH, LANE), lambda c: (c, 0)),
                  pl.BlockSpec((CH, D), lambda c: (c, 2))] + _SSD_PAR() + [pl.BlockSpec(memory_space=pl.ANY)],
        out_specs=[pl.BlockSpec((CH, D), lambda c: (c, 1)), pl.BlockSpec((None, NH * HD, NS), lambda c: (c, 0, 0))],
        out_shape=[S(mix.shape, BF16), S((nc, NH * HD, NS), F32)],
        scratch_shapes=[pltpu.VMEM((NH * HD, NS), F32)], input_output_aliases={7: 0},
        compiler_params=_params("arbitrary"), name=name)(xc, dtr, proj, dtb, alog, dsk, nrm, mix)


def _ssd_bwd(name, xc, dtr, proj, prevs, dtb, alog, dsk, nrm, dmix, dproj):
    T = xc.shape[0]
    nc = T // CH
    rev = lambda c: nc - 1 - c

    def body(xc_ref, dtr_ref, z_ref, prev_ref, dtb_ref, alog_ref, dsk_ref, nrm_ref, dmix_ref, _,
             dproj_ref, dxc_ref, ddtr_ref, ddtb_ref, dalog_ref, ddsk_ref, dnrm_ref, dstate):
        first = pl.program_id(0) == 0

        @pl.when(first)
        def _():
            dstate[...] = jnp.zeros_like(dstate)

        prim = _ssd_load(xc_ref, dtr_ref, z_ref, prev_ref, dtb_ref, alog_ref, dsk_ref, nrm_ref)
        _, vjp = jax.vjp(_ssd_math, *prim)
        douts = [dmix_ref[:, g * 512:(g + 1) * 512].astype(F32) for g in range(NG)]
        dnew = [dstate[g * 512:(g + 1) * 512, :] for g in range(NG)]
        dx, dB, dC, ddtr, dz, dprev, ddtb, dalog, ddsk, dnrm = vjp((douts, dnew))
        for h in range(NH):
            dxc_ref[:, h * HD:(h + 1) * HD] = dx[h].astype(BF16)
        for g in range(NG):
            dstate[g * 512:(g + 1) * 512, :] = dprev[g]
            dxc_ref[:, D + g * NS:D + (g + 1) * NS] = dB[g].astype(BF16)
            dxc_ref[:, D + NG * NS + g * NS:D + NG * NS + (g + 1) * NS] = dC[g].astype(BF16)
            dproj_ref[:, g * 512:(g + 1) * 512] = dz[g].astype(BF16)
            _acc_store(first, dnrm_ref, (slice(None), slice(g * 512, (g + 1) * 512)), dnrm[g])
        ddtr_ref[...] = ddtr
        _acc_store(first, ddtb_ref, (slice(None), slice(None)), ddtb)
        _acc_store(first, dalog_ref, (slice(None), slice(None)), dalog)
        _acc_store(first, ddsk_ref, (slice(None), slice(None)), ddsk)

    vec = pl.BlockSpec((1, LANE), lambda c: (0, 0))
    return pl.pallas_call(
        body, grid=(nc,),
        in_specs=[pl.BlockSpec((CH, CONV_C), lambda c: (rev(c), 0)), pl.BlockSpec((CH, LANE), lambda c: (rev(c), 0)),
                  pl.BlockSpec((CH, D), lambda c: (rev(c), 2)),
                  pl.BlockSpec((None, NH * HD, NS), lambda c: (rev(c), 0, 0))] + _SSD_PAR() +
                 [pl.BlockSpec((CH, D), lambda c: (rev(c), 1)), pl.BlockSpec(memory_space=pl.ANY)],
        out_specs=[pl.BlockSpec((CH, D), lambda c: (rev(c), 2)), pl.BlockSpec((CH, CONV_C), lambda c: (rev(c), 0)),
                   pl.BlockSpec((CH, LANE), lambda c: (rev(c), 0)), vec, vec, vec, pl.BlockSpec((1, D), lambda c: (0, 0))],
        out_shape=[S(dproj.shape, BF16), S((T, CONV_C), BF16), S((T, LANE), F32), S((1, LANE), F32), S((1, LANE), F32),
                   S((1, LANE), F32), S((1, D), F32)],
        scratch_shapes=[pltpu.VMEM((NH * HD, NS), F32)], input_output_aliases={9: 0},
        compiler_params=_params("arbitrary"), name=name)(xc, dtr, proj, prevs, dtb, alog, dsk, nrm, dmix, dproj)


def _rope(x, c, s, sign):
    W = x.shape[1]
    reps = W // LANE
    C, Sg = jnp.tile(c, (1, reps)), jnp.tile(s, (1, reps))
    lane = lax.broadcasted_iota(jnp.int32, x.shape, 1) % AHD
    up, dn = pltpu.roll(x, W - ROT // 2, 1), pltpu.roll(x, ROT // 2, 1)
    sw = jnp.where(lane < ROT // 2, up, jnp.where(lane < ROT, dn, 0.0))
    return x * C + sign * sw * Sg


def _rope_fwd(name, qkv, cos, sin):
    T = qkv.shape[0]
    tt = _tile(T, 256)
    KV = AKV * AHD

    def body(x_ref, c_ref, s_ref, o_ref):
        c, s = c_ref[...], s_ref[...]
        o_ref[:, :D] = _rope(x_ref[:, :D], c, s, 1.0).astype(BF16)
        o_ref[:, D:D + KV] = _rope(x_ref[:, D:D + KV], c, s, 1.0).astype(BF16)
        o_ref[:, D + KV:] = x_ref[:, D + KV:].astype(BF16)

    tab = pl.BlockSpec((tt, LANE), lambda i: (i, 0))
    return pl.pallas_call(
        body, grid=(T // tt,), in_specs=[pl.BlockSpec((tt, ODD_IN), lambda i: (i, 0)), tab, tab],
        out_specs=pl.BlockSpec((tt, ODD_IN), lambda i: (i, 0)), out_shape=S((T, ODD_IN), BF16),
        compiler_params=_params("parallel"), name=name)(qkv, cos, sin)


def _rope_bwd(name, dq, dkv_cur, dkv_prev, cos, sin):
    T = dq.shape[0]
    nb = T // CH
    KV = AKV * AHD

    def body(dq_ref, cur_ref, nxt_ref, c_ref, s_ref, o_ref, db_ref):
        n = pl.program_id(0)
        c, s = c_ref[...], s_ref[...]
        dkv = cur_ref[...] + nxt_ref[...] * (n < nb - 1).astype(F32)
        o_ref[:, :D] = _rope(dq_ref[...].astype(F32), c, s, -1.0).astype(BF16)
        o_ref[:, D:D + KV] = _rope(dkv[:, :KV], c, s, -1.0).astype(BF16)
        o_ref[:, D + KV:] = dkv[:, KV:].astype(BF16)
        _acc_store(n == 0, db_ref, (slice(None), slice(None)), jnp.sum(o_ref[...].astype(F32), 0, keepdims=True))

    tab = pl.BlockSpec((CH, LANE), lambda n: (n, 0))
    return pl.pallas_call(
        body, grid=(nb,),
        in_specs=[pl.BlockSpec((CH, D), lambda n: (n, 0)), pl.BlockSpec((CH, 2 * KV), lambda n: (n, 0)),
                  pl.BlockSpec((CH, 2 * KV), lambda n: (jnp.minimum(n + 1, nb - 1), 0)), tab, tab],
        out_specs=[pl.BlockSpec((CH, ODD_IN), lambda n: (n, 0)), pl.BlockSpec((1, ODD_IN), lambda n: (0, 0))],
        out_shape=[S((T, ODD_IN), BF16), S((1, ODD_IN), F32)],
        compiler_params=_params("arbitrary"), name=name)(dq, dkv_cur, dkv_prev, cos, sin)


def _swa_math(q, kp, kc, vp, vc, snk, mask):
    outs = []
    for k in range(AKV):
        K = jnp.concatenate([kp[k], kc[k]], 0).astype(BF16)
        V = jnp.concatenate([vp[k], vc[k]], 0).astype(BF16)
        s = lax.dot_general(q[k].astype(BF16), K, _DIMS["nt"], preferred_element_type=F32) * ATT_SCALE
        s = jnp.where(mask, s, -jnp.inf)
        m = lax.stop_gradient(jnp.maximum(jnp.max(s, -1, keepdims=True), snk[k]))
        p = jnp.exp(s - m)
        pr = p / (jnp.sum(p, -1, keepdims=True) + jnp.exp(snk[k] - m))
        outs.append(jnp.dot(pr.astype(BF16), V, preferred_element_type=F32))
    return outs


def _stack_heads(ref, k):
    return jnp.concatenate([ref[:, (k * AREP + r) * AHD:(k * AREP + r + 1) * AHD].astype(F32) for r in range(AREP)], 0)


def _swa_load(q_ref, cur_ref, prv_ref, snk_ref):
    KV = AKV * AHD
    q = [_stack_heads(q_ref, k) for k in range(AKV)]
    kc = [cur_ref[:, k * AHD:(k + 1) * AHD].astype(F32) for k in range(AKV)]
    vc = [cur_ref[:, KV + k * AHD:KV + (k + 1) * AHD].astype(F32) for k in range(AKV)]
    kp = [prv_ref[:, k * AHD:(k + 1) * AHD].astype(F32) for k in range(AKV)]
    vp = [prv_ref[:, KV + k * AHD:KV + (k + 1) * AHD].astype(F32) for k in range(AKV)]
    snk = [jnp.concatenate([jnp.broadcast_to(snk_ref[:, k * AREP + r:k * AREP + r + 1], (CH, 1)) for r in range(AREP)], 0)
           for k in range(AKV)]
    return q, kp, kc, vp, vc, snk


def _swa_mask(n):
    iq = lax.broadcasted_iota(jnp.int32, (AREP * CH, 2 * CH), 0) % CH
    js = lax.broadcasted_iota(jnp.int32, (AREP * CH, 2 * CH), 1)
    rel = iq + CH - js
    return (rel >= 0) & (rel < CH) & ((n > 0) | (js >= CH))


def _swa_specs(T):
    KV = AKV * AHD
    return [pl.BlockSpec((CH, D), lambda n: (n, 0)), pl.BlockSpec((CH, 2 * KV), lambda n: (n, D // (2 * KV))),
            pl.BlockSpec((CH, 2 * KV), lambda n: (jnp.maximum(n - 1, 0), D // (2 * KV))),
            pl.BlockSpec((1, LANE), lambda n: (0, 0))]


def _swa_fwd(name, qkvr, snk):
    T = qkvr.shape[0]

    def body(q_ref, cur_ref, prv_ref, snk_ref, o_ref):
        outs = _swa_math(*_swa_load(q_ref, cur_ref, prv_ref, snk_ref), _swa_mask(pl.program_id(0)))
        for h in range(AH):
            k, r = divmod(h, AREP)
            o_ref[:, h * AHD:(h + 1) * AHD] = outs[k][r * CH:(r + 1) * CH].astype(BF16)

    return pl.pallas_call(
        body, grid=(T // CH,), in_specs=_swa_specs(T), out_specs=pl.BlockSpec((CH, D), lambda n: (n, 0)),
        out_shape=S((T, D), BF16), compiler_params=_params("parallel"), name=name)(qkvr, qkvr, qkvr, snk)


def _swa_bwd(name, qkvr, snk, do):
    T = qkvr.shape[0]
    KV = AKV * AHD

    def body(q_ref, cur_ref, prv_ref, snk_ref, do_ref, dq_ref, dcur_ref, dprv_ref, dsnk_ref):
        n = pl.program_id(0)

        @pl.when(n == 0)
        def _():
            dsnk_ref[...] = jnp.zeros_like(dsnk_ref)

        prim = _swa_load(q_ref, cur_ref, prv_ref, snk_ref)
        mask = _swa_mask(n)
        _, vjp = jax.vjp(lambda *p: _swa_math(*p, mask), *prim)
        dq, dkp, dkc, dvp, dvc, dsnk = vjp([_stack_heads(do_ref, k) for k in range(AKV)])
        for h in range(AH):
            k, r = divmod(h, AREP)
            dq_ref[:, h * AHD:(h + 1) * AHD] = dq[k][r * CH:(r + 1) * CH].astype(BF16)
            dsnk_ref[:, h:h + 1] += jnp.sum(dsnk[k][r * CH:(r + 1) * CH], 0, keepdims=True)
        for k in range(AKV):
            dcur_ref[:, k * AHD:(k + 1) * AHD] = dkc[k]
            dcur_ref[:, KV + k * AHD:KV + (k + 1) * AHD] = dvc[k]
            dprv_ref[:, k * AHD:(k + 1) * AHD] = dkp[k]
            dprv_ref[:, KV + k * AHD:KV + (k + 1) * AHD] = dvp[k]

    kv = pl.BlockSpec((CH, 2 * KV), lambda n: (n, 0))
    return pl.pallas_call(
        body, grid=(T // CH,), in_specs=_swa_specs(T) + [pl.BlockSpec((CH, D), lambda n: (n, 0))],
        out_specs=[pl.BlockSpec((CH, D), lambda n: (n, 0)), kv, kv, pl.BlockSpec((1, LANE), lambda n: (0, 0))],
        out_shape=[S((T, D), BF16), S((T, 2 * KV), F32), S((T, 2 * KV), F32), S((1, LANE), F32)],
        compiler_params=_params("arbitrary"), name=name)(qkvr, qkvr, qkvr, snk, do)


def _xat_math(q, k, v):
    outs = []
    for h in range(XH):
        s = lax.dot_general(q[h].astype(BF16), k[h].astype(BF16), _DIMS["nt"], preferred_element_type=F32) * X_SCALE
        m = lax.stop_gradient(jnp.max(s, -1, keepdims=True))
        p = jnp.exp(s - m)
        pr = p / jnp.sum(p, -1, keepdims=True)
        outs.append(jnp.dot(pr.astype(BF16), v[h].astype(BF16), preferred_element_type=F32))
    return outs


def _xat_load(q_ref, kv_ref):
    q = [q_ref[:, h * XHD:(h + 1) * XHD].astype(F32) for h in range(XH)]
    k = [kv_ref[:, h * XHD:(h + 1) * XHD].astype(F32) for h in range(XH)]
    v = [kv_ref[:, XW + h * XHD:XW + (h + 1) * XHD].astype(F32) for h in range(XH)]
    return q, k, v


def _xat_fwd(name, q, kv):
    T, M = q.shape[0], kv.shape[0]
    tt = _tile(T, 512)

    def body(q_ref, kv_ref, o_ref):
        outs = _xat_math(*_xat_load(q_ref, kv_ref))
        for h in range(XH):
            o_ref[:, h * XHD:(h + 1) * XHD] = outs[h].astype(BF16)

    return pl.pallas_call(
        body, grid=(T // tt,),
        in_specs=[pl.BlockSpec((tt, XW), lambda i: (i, 0)), pl.BlockSpec((M, 2 * XW), lambda i: (0, 0))],
        out_specs=pl.BlockSpec((tt, XW), lambda i: (i, 0)), out_shape=S((T, XW), BF16),
        compiler_params=_params("parallel"), name=name)(q, kv)


def _xat_bwd(name, q, kv, do):
    T, M = q.shape[0], kv.shape[0]
    tt = _tile(T, 512)

    def body(q_ref, kv_ref, do_ref, dq_ref, dkv_ref):
        first = pl.program_id(0) == 0
        _, vjp = jax.vjp(_xat_math, *_xat_load(q_ref, kv_ref))
        dq, dk, dv = vjp([do_ref[:, h * XHD:(h + 1) * XHD].astype(F32) for h in range(XH)])
        for h in range(XH):
            sl = slice(h * XHD, (h + 1) * XHD)
            dq_ref[:, sl] = dq[h].astype(BF16)
            _acc_store(first, dkv_ref, (slice(None), sl), dk[h])
            _acc_store(first, dkv_ref, (slice(None), slice(XW + h * XHD, XW + (h + 1) * XHD)), dv[h])

    qs = pl.BlockSpec((tt, XW), lambda i: (i, 0))
    kvs = pl.BlockSpec((M, 2 * XW), lambda i: (0, 0))
    return pl.pallas_call(
        body, grid=(T // tt,), in_specs=[qs, kvs, qs], out_specs=[qs, kvs],
        out_shape=[S((T, XW), BF16), S((M, 2 * XW), F32)],
        compiler_params=_params("arbitrary"), name=name)(q, kv, do)


def _loss_head(name, x, gain, target):
    T = x.shape[0]
    tt = _tile(T, 512)

    def body(x_ref, g_ref, t_ref, l_ref, dx_ref, dxb_ref, dg_ref):
        first = pl.program_id(0) == 0
        xv, g = x_ref[...], g_ref[...]
        r = lax.rsqrt(jnp.mean(xv * xv, -1, keepdims=True) + EPS)
        xh = xv * r
        e = xh * g - t_ref[...]
        part = 0.5 * jnp.sum(jnp.mean(e * e, -1, keepdims=True), (0, 1), keepdims=True)
        _acc_store(first, l_ref, (slice(None), slice(None)), jnp.broadcast_to(part, (1, LANE)))
        dy = e * (1.0 / D)
        dxh = dy * g
        dx = r * (dxh - xh * jnp.mean(dxh * xh, -1, keepdims=True))
        dx_ref[...] = dx
        dxb_ref[...] = dx.astype(BF16)
        _acc_store(first, dg_ref, (slice(None), slice(None)), jnp.sum(dy * xh, 0, keepdims=True))

    row = pl.BlockSpec((tt, D), lambda i: (i, 0))
    vec = pl.BlockSpec((1, D), lambda i: (0, 0))
    return pl.pallas_call(
        body, grid=(T // tt,), in_specs=[row, vec, row],
        out_specs=[pl.BlockSpec((1, LANE), lambda i: (0, 0)), row, row, vec],
        out_shape=[S((1, LANE), F32), S((T, D), F32), S((T, D), BF16), S((1, D), F32)],
        compiler_params=_params("arbitrary"), name=name)(x, gain, target)


def _out_proj(name, a, b, x, next_gain, scale=1.0, tk_t=2048, plain=(1024, 1024)):
    if next_gain is None:
        return _mm(name, "nn", a, b, F32, res=x, scale=scale, tm_t=plain[0], tn_t=plain[1], tk_t=tk_t), None
    return _mm(name, "nn", a, b, F32, res=x, scale=scale, norm_gain=next_gain, tm_t=512, tn_t=D, tk_t=min(tk_t, 2048))


_FFN_CHUNK = {"l0_ffn1": 256, "l0_ffn2": 512, "l1_ffn1": 1408, "l1_ffn2": 768}


def _ffn_fwd(tag, x, h, gain, wgu4, get_wd, next_gain):
    g, u, a = _ffn_up(f"{tag}_up", h, wgu4, 0, chunk=_FFN_CHUNK[tag])
    wd = get_wd(a).reshape(1, 1, DFF, D)
    x_new, _ = _out_proj(f"{tag}_down", a, Op(wd, "r"), x, None, 0.5, 2816)
    h_next = None if next_gain is None else _rms_fwd(f"{tag}_nextnorm", x_new, next_gain)
    return x_new, h_next, (x, gain, h, g, u, a)


def _ffn_bwd(tag, saved, dx, dxb, wgu4, wd4, put):
    x, gain, h, g, u, a = saved
    dgu = _ffn_dact(f"{tag}_dact", dxb, wd4, 0, g, u, 1024, chunk=_FFN_CHUNK[tag])
    dwd = _mm(f"{tag}_dwd", "tn", a, dxb, BF16, out=("r", 4, 1, 0), scale=0.5, tm_t=1408, tn_t=1024, tk_t=2048)
    dwgu = _mm(f"{tag}_dwgu", "tn", h, Op(dgu, "c"), BF16, out=("c", 4, 1, 0), tm_t=1024, tn_t=256, tk_t=8192)
    tok = put(dwgu, dwd)
    dh = _mm(f"{tag}_dh", "nt", Op(dgu, "c"), Op(wgu4, "c"), BF16, bias=jnp.zeros((1, D), F32) + tok, tk_t=2816)
    dx, dxb, dgain = _rms_bwd(f"{tag}_dnorm", x, gain, dh, dx)
    return dx, dxb, dgain


def _xattn_fwd(tag, x, hq, mem, gq, gm, wxq4, wxkv4, wxo4, next_gain):
    mn = _rms_fwd(f"{tag}_normm", mem, gm)
    q = _mm(f"{tag}_q", "nn", hq, Op(wxq4, "r"), BF16)
    kv = _mm(f"{tag}_kv", "nn", mn, Op(wxkv4, "r"), BF16)
    o = _xat_fwd(f"{tag}_att", q, kv)
    wxo = jnp.transpose(wxo4[:, 0], (1, 0, 2)).reshape(XW, D)
    x_new, h_next = _out_proj(f"{tag}_o", o, wxo, x, next_gain)
    return x_new, h_next, (x, mem, gq, gm, hq, mn, q, kv, o)


def _xattn_bwd(tag, saved, dx, dxb, wxq4, wxkv4, wxo4, put):
    x, mem, gq, gm, hq, mn, q, kv, o = saved
    dwxo = _mm(f"{tag}_dwo", "tn", o, dxb, BF16, out=("c", 4, 1, 0))
    do = _mm(f"{tag}_do", "nt", dxb, Op(wxo4, "c"), BF16)
    dq, dkv = _xat_bwd(f"{tag}_datt", q, kv, do)
    dwxq = _mm(f"{tag}_dwq", "tn", hq, dq, BF16, out=("r", 4, 1, 0))
    dwxkv = _mm(f"{tag}_dwkv", "tn", mn, dkv, BF16, out=("r", 4, 1, 0))
    tok = put(dwxq, dwxkv, dwxo)
    dhq = _mm(f"{tag}_dhq", "nt", dq, Op(wxq4, "r"), BF16, bias=jnp.zeros((1, D), F32) + tok)
    dmn = _mm(f"{tag}_dmn", "nt", dkv, Op(wxkv4, "r"), BF16)
    _, _, dgm = _rms_bwd(f"{tag}_dnormm", mem, gm, dmn)
    dx, dxb, dgq = _rms_bwd(f"{tag}_dnormq", x, gq, dhq, dx)
    return dx, dxb, dgq, dgm


def _even_fwd(tag, x, h, gain, w_main, w_dt, p, wout4, next_gain):
    proj = _mm(f"{tag}_in", "nn", h, w_main, BF16)
    dtr = _mm(f"{tag}_indt", "nn", h, w_dt, F32)
    mix = _gmlp_fwd(f"{tag}_gmlp", proj, p["lng"], p["lnb"], p["ws"], p["bs"])
    ypre, xc = _conv_fwd(f"{tag}_conv", proj, p["cw"], p["cb"])
    mix, prevs = _ssd_fwd(f"{tag}_ssd", xc, dtr, proj, p["dtb"], p["alog"], p["dsk"], p["nrm"], mix)
    x_new, h_next = _out_proj(f"{tag}_out", mix, Op(wout4.reshape(1, 1, 2 * D, D), "r"), x, next_gain)
    return x_new, h_next, (x, gain, h, proj, dtr, mix, ypre, xc, prevs)


def _even_bwd(tag, saved, dx, dxb, w_main, w_dt, p, wout4, put):
    x, gain, h, proj, dtr, mix, ypre, xc, prevs = saved
    T = x.shape[0]
    dwout = _mm(f"{tag}_dwout", "tn", mix, dxb, BF16, out=("r", 4, 1, 0))
    dmix = _mm(f"{tag}_dmix", "nt", dxb, Op(wout4, "r", 0), BF16)
    dproj = lax.empty((T, EVEN_MAIN), BF16)
    dproj, dlng, dlnb, dws, dbs = _gmlp_bwd(f"{tag}_dgmlp", proj, p["lng"], p["lnb"], p["ws"], p["bs"], dmix, dproj)
    dproj, dxc, ddtr, ddtb, dalog, ddsk, dnrm = _ssd_bwd(
        f"{tag}_dssd", xc, dtr, proj, prevs, p["dtb"], p["alog"], p["dsk"], p["nrm"], dmix, dproj)
    dproj, dcw, dcb = _conv_bwd(f"{tag}_dconv", proj, ypre, dxc, p["cw"], dproj)
    dw_main = _mm(f"{tag}_dwin", "tn", h, dproj, BF16, tm_t=1024, tn_t=256, tk_t=8192)
    dw_dt = _mm(f"{tag}_dwdt", "tn", h, ddtr, BF16)
    tok = put(dw_main, dw_dt, dwout)
    dh = _mm(f"{tag}_dh1", "nt", ddtr, w_dt + tok.astype(BF16), F32)
    dh = _mm(f"{tag}_dh2", "nt", dproj, w_main, BF16, res=dh)
    dx, dxb, dgain = _rms_bwd(f"{tag}_dnorm", x, gain, dh, dx)
    small = dict(lng=dlng, lnb=dlnb, ws=dws, bs=dbs, cw=dcw, cb=dcb, dtb=ddtb, alog=dalog, dsk=ddsk, nrm=dnrm)
    return dx, dxb, dgain, small


def _odd_fwd(tag, x, h, gain, wqkv4, bqkv, snk, wo4, cos, sin, next_gain):
    qkv = _mm(f"{tag}_qkv", "nn", h, Op(wqkv4, "c", 0), F32, bias=bqkv, tn_t=640)
    qkvr = _rope_fwd(f"{tag}_rope", qkv, cos, sin)
    o = _swa_fwd(f"{tag}_swa", qkvr, snk)
    x_new, h_next = _out_proj(f"{tag}_o", o, Op(wo4.reshape(1, 1, D, D), "r"), x, next_gain)
    return x_new, h_next, (x, gain, h, qkvr, o)


def _odd_bwd(tag, saved, dx, dxb, wqkv4, snk, wo4, cos, sin, put):
    x, gain, h, qkvr, o = saved
    dwo = _mm(f"{tag}_dwo", "tn", o, dxb, BF16, out=("r", 4, 1, 0))
    do = _mm(f"{tag}_do", "nt", dxb, Op(wo4, "r", 0), BF16)
    dq, dcur, dprv, dsnk = _swa_bwd(f"{tag}_dswa", qkvr, snk, do)
    dqkv, dbias = _rope_bwd(f"{tag}_drope", dq, dcur, dprv, cos, sin)
    dwqkv = _mm(f"{tag}_dwqkv", "tn", h, dqkv, BF16, out=("c", 4, 1, 0), tn_t=640)
    tok = put(dwqkv, dwo)
    dh = _mm(f"{tag}_dh", "nt", dqkv, Op(wqkv4, "c", 0), BF16, bias=jnp.zeros((1, D), F32) + tok, tk_t=640)
    dx, dxb, dgain = _rms_bwd(f"{tag}_dnorm", x, gain, dh, dx)
    return dx, dxb, dgain, dbias, dsnk


def _row(v):
    return v.reshape(1, -1).astype(F32)


def _pad_lanes(v, n=LANE):
    v = v.reshape(1, -1).astype(F32)
    return jnp.pad(v, ((0, 0), (0, n - v.shape[1])))


def _local_step(x, mem, positions, target, getw, P, putg):
    inv_freq = ROPE_THETA ** (-jnp.arange(0, ROT, 2, dtype=F32) / ROT)
    ang = positions.astype(F32)[:, None] * inv_freq
    cos8, sin8 = jnp.cos(ang), jnp.sin(ang)
    ones, zeros = jnp.ones((x.shape[0], AHD - ROT), F32), jnp.zeros((x.shape[0], AHD - ROT), F32)
    cos = jnp.tile(jnp.concatenate([cos8, cos8, ones], 1), (1, 2))
    sin = jnp.tile(jnp.concatenate([-sin8, sin8, zeros], 1), (1, 2))

    snk = _pad_lanes(P["sinks"])
    W = {}

    def w(name, layer, after):
        if (name, layer) not in W:
            W[name, layer] = getw(name, layer, after)
        return W[name, layer]

    saved = []
    h = _rms_fwd("l0_ffn1_norm", x, _row(P["norm_ffn1"][0]))
    for i in range(2):
        x, h, s1 = _ffn_fwd(f"l{i}_ffn1", x, h, _row(P["norm_ffn1"][i]), w("w_ffn1_gu", i, x),
                            functools.partial(w, "w_ffn1_down", i), _row(P["norm_mix"][i]))
        if i == 0:
            ev = dict(lng=_row(P["gm_ln_g"]), lnb=_row(P["gm_ln_b"]), ws=P["gm_ws"].reshape(GM_G, CH, CH),
                      bs=P["gm_bs"].reshape(GM_G, CH, 1), cw=w("conv_w", 0, x), cb=_row(P["conv_b"]),
                      dtb=_pad_lanes(P["dt_bias"]), alog=_pad_lanes(P["a_log"]), dsk=_pad_lanes(P["d_skip"]),
                      nrm=_row(P["ssd_norm"]))
            w_main, w_dt = w("w_in_even", 0, x)
            x, h, s2 = _even_fwd("l0_mix", x, h, _row(P["norm_mix"][0]), w_main, w_dt, ev, w("w_out_even", 0, x),
                                 _row(P["norm_xq"][0]))
        else:
            x, h, s2 = _odd_fwd("l1_mix", x, h, _row(P["norm_mix"][1]), w("w_qkv", 0, x), w("b_qkv", 0, x), snk,
                                w("w_o_odd", 0, x), cos, sin, _row(P["norm_xq"][1]))
        x, h, s3 = _xattn_fwd(f"l{i}_xat", x, h, mem, _row(P["norm_xq"][i]), _row(P["norm_mem"][i]),
                              w("w_xq", i, x), w("w_xkv", i, x), w("w_xo", i, x), _row(P["norm_ffn2"][i]))
        x, h, s4 = _ffn_fwd(f"l{i}_ffn2", x, h, _row(P["norm_ffn2"][i]), w("w_ffn2_gu", i, x),
                            functools.partial(w, "w_ffn2_down", i), _row(P["norm_ffn1"][1]) if i == 0 else None)
        saved.append((s1, s2, s3, s4))

    loss, dx, dxb, d_final = _loss_head("loss_head", x, _row(P["final_norm"]), target)

    sm = {}
    dn = {k: [None, None] for k in ("norm_ffn1", "norm_mix", "norm_xq", "norm_mem", "norm_ffn2")}
    for i in (1, 0):
        s1, s2, s3, s4 = saved[i]
        dx, dxb, dn["norm_ffn2"][i] = _ffn_bwd(
            f"l{i}_ffn2", s4, dx, dxb, W["w_ffn2_gu", i], W["w_ffn2_down", i],
            lambda dwgu, dwd, i=i: putg({("w_ffn2_gu", i): dwgu, ("w_ffn2_down", i): dwd}))
        dx, dxb, dn["norm_xq"][i], dn["norm_mem"][i] = _xattn_bwd(
            f"l{i}_xat", s3, dx, dxb, W["w_xq", i], W["w_xkv", i], W["w_xo", i],
            lambda dwxq, dwxkv, dwxo, i=i: putg({("w_xq", i): dwxq, ("w_xkv", i): dwxkv, ("w_xo", i): dwxo}))
        if i == 0:
            dx, dxb, dn["norm_mix"][0], sm_even = _even_bwd(
                "l0_mix", s2, dx, dxb, w_main, w_dt, ev, W["w_out_even", 0],
                lambda dw_main, dw_dt, dwout: putg({("w_in_even", 0): (dw_main, dw_dt), ("w_out_even", 0): dwout}))
        else:
            dx, dxb, dn["norm_mix"][1], sm["b_qkv"], sm["sinks"] = _odd_bwd(
                "l1_mix", s2, dx, dxb, W["w_qkv", 0], snk, W["w_o_odd", 0], cos, sin,
                lambda dwqkv, dwo: putg({("w_qkv", 0): dwqkv, ("w_o_odd", 0): dwo}))
        dx, dxb, dn["norm_ffn1"][i] = _ffn_bwd(
            f"l{i}_ffn1", s1, dx, dxb, W["w_ffn1_gu", i], W["w_ffn1_down", i],
            lambda dwgu, dwd, i=i: putg({("w_ffn1_gu", i): dwgu, ("w_ffn1_down", i): dwd}))
    for k, v in dn.items():
        sm[k] = jnp.concatenate(v, 0)
    sm.update(gm_ln_g=sm_even["lng"], gm_ln_b=sm_even["lnb"], gm_ws=sm_even["ws"], gm_bs=sm_even["bs"],
              conv_w=sm_even["cw"], conv_b=sm_even["cb"], dt_bias=sm_even["dtb"][:, :NH], a_log=sm_even["alog"][:, :NH],
              d_skip=sm_even["dsk"][:, :NH], ssd_norm=sm_even["nrm"], sinks=sm["sinks"][:, :AH], final_norm=d_final)
    return loss[0, 0], dx, sm


def _chip_peers():
    x, y, c = lax.axis_index("x"), lax.axis_index("y"), lax.axis_index("c")
    return 2 * x + y, [((1 - x, y, c), 2 * (1 - x) + y), ((x, 1 - y, c), 2 * x + (1 - y)),
                       ((1 - x, 1 - y, c), 2 * (1 - x) + (1 - y))]


def _any_specs(n):
    return [pl.BlockSpec(memory_space=pl.ANY)] * n


_HBM = pl.BlockSpec(memory_space=pltpu.HBM)
_SEM = pl.BlockSpec(memory_space=pltpu.SEMAPHORE)
_EFFECT = pltpu.SideEffectType.DATAFLOW_SIDE_EFFECTING


def _own_slot(piece, chip):
    zone = lax.empty((4,) + piece.shape, piece.dtype)
    return lax.dynamic_update_slice(zone, piece[None], (chip,) + (0,) * piece.ndim)


def _chip_copies(srcs, lands, ssems, rsems, sibling=False):
    if sibling:
        dev = (lax.axis_index("x"), lax.axis_index("y"), 1 - lax.axis_index("c"))
        return [pltpu.make_async_remote_copy(src_ref=srcs[i], dst_ref=lands[i], send_sem=ssems[i].at[0],
                                             recv_sem=rsems[i].at[0], device_id=dev, device_id_type=MESH)
                for i in range(len(lands))]
    me, peers = _chip_peers()
    return [pltpu.make_async_remote_copy(src_ref=lands[i].at[me] if srcs[i] is None else srcs[i].at[chip],
                                         dst_ref=lands[i].at[me], send_sem=ssems[i].at[j], recv_sem=rsems[i].at[j],
                                         device_id=dev, device_id_type=MESH)
            for i in range(len(lands)) for j, (dev, chip) in enumerate(peers)]


def _exchange_start(name, srcs, lands, sibling=False):
    n = len(lands)
    ns = 0 if srcs is None else n

    def body(*refs):
        src_refs = [None] * n if srcs is None else refs[:n]
        land_refs = refs[ns:ns + n]
        ssems, rsems = refs[ns + n:ns + 2 * n], refs[ns + 2 * n:ns + 3 * n]
        token = refs[2 * ns + 4 * n]
        for cp in _chip_copies(src_refs, land_refs, ssems, rsems, sibling):
            cp.start()
        token[...] = jnp.zeros_like(token)

    ins = ([] if srcs is None else list(srcs)) + list(lands)
    res = pl.pallas_call(
        body, name=name,
        out_shape=[pltpu.SemaphoreType.DMA((1 if sibling else 3,))] * (2 * n) + [pltpu.HBM(a.shape, a.dtype) for a in ins]
        + [S((8, LANE), F32)],
        in_specs=[_HBM] * (ns + n),
        out_specs=[_SEM] * (2 * n) + [_HBM] * (ns + n) + [pl.BlockSpec(memory_space=pltpu.VMEM)],
        input_output_aliases={i: 2 * n + i for i in range(ns + n)},
        compiler_params=pltpu.CompilerParams(has_side_effects=_EFFECT),
    )(*[pltpu.with_memory_space_constraint(a, pltpu.HBM) for a in ins])
    items = [(res[i], res[n + i], None if srcs is None else res[2 * n + i], res[2 * n + ns + i]) for i in range(n)]
    return items, res[2 * n + ns + n][0, 0]


def _exchange_wait(name, item, after, sibling=False):
    ssem, rsem, src, land = item
    ns = 0 if src is None else 1

    def body(*refs):
        src_ref = refs[0] if ns else None
        land_ref, ssem_ref, rsem_ref = refs[ns], refs[ns + 1], refs[ns + 2]
        for cp in _chip_copies([src_ref], [land_ref], [ssem_ref], [rsem_ref], sibling):
            cp.wait_send()
            cp.wait_recv()

    ins = ([src] if ns else []) + [land]
    return pl.pallas_call(
        body, name=name, out_shape=[pltpu.HBM(a.shape, a.dtype) for a in ins],
        in_specs=[_HBM] * (ns + 1) + [_SEM, _SEM, pl.BlockSpec(memory_space=pl.ANY)], out_specs=[_HBM] * (ns + 1),
        input_output_aliases={i: i for i in range(ns + 1)}, compiler_params=pltpu.CompilerParams(has_side_effects=_EFFECT),
    )(*ins, ssem, rsem, after)[ns]


def _gather_all(name, v, after):
    def body(v_ref, _, o_ref, ssem, rsem, lsem):
        x, y, c = lax.axis_index("x"), lax.axis_index("y"), lax.axis_index("c")
        me = 4 * x + 2 * y + c
        loc = pltpu.make_async_copy(v_ref, o_ref.at[me], lsem)
        loc.start()
        copies = []
        for k in range(1, 8):
            fx, fy, fc = (k >> 2) & 1, (k >> 1) & 1, k & 1
            dev = (x ^ fx, y ^ fy, c ^ fc)
            cp = pltpu.make_async_remote_copy(src_ref=v_ref, dst_ref=o_ref.at[me], send_sem=ssem.at[k - 1],
                                              recv_sem=rsem.at[k - 1], device_id=dev, device_id_type=MESH)
            cp.start()
            copies.append(cp)
        for cp in copies:
            cp.wait()
        loc.wait()

    return pl.pallas_call(
        body, in_specs=_any_specs(2), out_specs=pl.BlockSpec(memory_space=pl.ANY), out_shape=S((8,) + v.shape, v.dtype),
        scratch_shapes=[pltpu.SemaphoreType.DMA((7,)), pltpu.SemaphoreType.DMA((7,)), pltpu.SemaphoreType.DMA(())],
        compiler_params=pltpu.CompilerParams(has_side_effects=True), name=name)(v, after)


def _row_tile(R, row_bytes, budget=4 << 20):
    if R * row_bytes <= budget or R % 16:
        return R
    t = max(16, budget // row_bytes // 16 * 16)
    while R % t:
        t -= 16
    return t


def _sum_slots(name, r, n):
    _, R, C = r.shape
    tr = _row_tile(R, C * (n * r.dtype.itemsize + 4))

    def body(r_ref, o_ref):
        acc = r_ref[0].astype(F32)
        for j in range(1, n):
            acc = acc + r_ref[j].astype(F32)
        o_ref[...] = acc

    return pl.pallas_call(
        body, grid=(R // tr,), in_specs=[pl.BlockSpec((n, tr, C), lambda i: (0, i, 0))],
        out_specs=pl.BlockSpec((tr, C), lambda i: (i, 0)), out_shape=S((R, C), F32),
        compiler_params=_params("parallel"), name=name)(r)


def _adamw(name, w, m, v, layer, g1, g2=None, into=None):
    nl, R, C = w.shape
    tr = _row_tile(R, C * 4 * 9)
    two, has_into = g2 is not None, into is not None

    def body(w_ref, m_ref, v_ref, g1_ref, *rest):
        rest = list(rest)
        g = g1_ref[...]
        if two:
            g = g + rest.pop(0)[...]
        g_ref, d_ref, nm_ref, nv_ref = rest[-4:]
        mn = B1 * m_ref[...] + (1.0 - B1) * g
        vn = B2 * v_ref[...] + (1.0 - B2) * jnp.square(g)
        m_hat = mn / (1.0 - B1 ** STEP)
        v_hat = vn / (1.0 - B2 ** STEP)
        g_ref[...] = g
        d_ref[...] = -LR * (m_hat / (jnp.sqrt(v_hat) + AEPS) + WD * w_ref[...])
        nm_ref[...] = mn
        nv_ref[...] = vn

    blk = pl.BlockSpec((tr, C), lambda i: (i, 0))
    lay = pl.BlockSpec((None, tr, C), lambda i: (layer, i, 0))
    args = [w, m, v, g1] + ([g2] if two else []) + (list(into) if has_into else [])
    in_specs = [lay] * 3 + [blk] * (2 if two else 1) + (_any_specs(4) if has_into else [])
    aliases = {len(args) - 4 + t: t for t in range(4)} if has_into else {}
    return pl.pallas_call(
        body, grid=(R // tr,), in_specs=in_specs, out_specs=[lay] * 4, out_shape=[S((nl, R, C), F32)] * 4,
        input_output_aliases=aliases, compiler_params=_params("parallel"), name=name)(*args)


_USE_ORDER = [("w_ffn1_gu", 0), ("w_ffn1_down", 0), ("conv_w", 0), ("w_in_even", 0), ("w_out_even", 0), ("w_xq", 0),
              ("w_xkv", 0), ("w_xo", 0), ("w_ffn2_gu", 0), ("w_ffn2_down", 0), ("w_ffn1_gu", 1), ("w_ffn1_down", 1),
              ("w_qkv", 0), ("b_qkv", 0), ("w_o_odd", 0), ("w_xq", 1), ("w_xkv", 1), ("w_xo", 1), ("w_ffn2_gu", 1),
              ("w_ffn2_down", 1)]
_SMALL = ["norm_ffn1", "norm_mix", "gm_ln_g", "gm_ln_b", "gm_ws", "gm_bs", "conv_w", "conv_b", "dt_bias", "a_log",
          "d_skip", "ssd_norm", "b_qkv", "sinks", "norm_xq", "norm_mem", "norm_ffn2", "final_norm"]
_WEIGHTS = ["norm_ffn1", "w_ffn1_gu", "w_ffn1_down", "norm_mix", "w_in_even", "gm_ln_g", "gm_ln_b", "gm_ws", "gm_bs",
            "conv_w", "conv_b", "dt_bias", "a_log", "d_skip", "ssd_norm", "w_out_even", "w_qkv", "b_qkv", "sinks",
            "w_o_odd", "norm_xq", "norm_mem", "w_xq", "w_xkv", "w_xo", "norm_ffn2", "w_ffn2_gu", "w_ffn2_down",
            "final_norm"]


def _pack(arrs):
    rows = []
    for a in arrs:
        f = a.reshape(-1).astype(F32)
        pad = (-f.shape[0]) % LANE
        rows.append(jnp.pad(f, (0, pad)).reshape(-1, LANE))
    out = jnp.concatenate(rows, 0)
    pad = (-out.shape[0]) % 8
    return jnp.pad(out, ((0, pad), (0, 0)))


def _unpack(packed, shapes):
    outs, r = [], 0
    for shp in shapes:
        n = math.prod(shp)
        nr = -(-n // LANE)
        outs.append(packed[r:r + nr].reshape(-1)[:n].reshape(shp))
        r += nr
    return outs


def kernel(x, mem, positions, norm_ffn1, w_ffn1_gu, w_ffn1_down, norm_mix, w_in_even, gm_ln_g, gm_ln_b, gm_ws, gm_bs, conv_w, conv_b, dt_bias, a_log, d_skip, ssd_norm, w_out_even, w_qkv, b_qkv, sinks, w_o_odd, norm_xq, norm_mem, w_xq, w_xkv, w_xo, norm_ffn2, w_ffn2_gu, w_ffn2_down, final_norm, loss_target, m_norm_ffn1, m_w_ffn1_gu, m_w_ffn1_down, m_norm_mix, m_w_in_even, m_gm_ln_g, m_gm_ln_b, m_gm_ws, m_gm_bs, m_conv_w, m_conv_b, m_dt_bias, m_a_log, m_d_skip, m_ssd_norm, m_w_out_even, m_w_qkv, m_b_qkv, m_sinks, m_w_o_odd, m_norm_xq, m_norm_mem, m_w_xq, m_w_xkv, m_w_xo, m_norm_ffn2, m_w_ffn2_gu, m_w_ffn2_down, m_final_norm, v_norm_ffn1, v_w_ffn1_gu, v_w_ffn1_down, v_norm_mix, v_w_in_even, v_gm_ln_g, v_gm_ln_b, v_gm_ws, v_gm_bs, v_conv_w, v_conv_b, v_dt_bias, v_a_log, v_d_skip, v_ssd_norm, v_w_out_even, v_w_qkv, v_b_qkv, v_sinks, v_w_o_odd, v_norm_xq, v_norm_mem, v_w_xq, v_w_xkv, v_w_xo, v_norm_ffn2, v_w_ffn2_gu, v_w_ffn2_down, v_final_norm):
    a = dict(locals())
    w = {k: a[k] for k in _WEIGHTS}
    mom = {k: a["m_" + k] for k in _WEIGHTS}
    var = {k: a["v_" + k] for k in _WEIGHTS}
    chip = 2 * lax.axis_index("x") + lax.axis_index("y")

    shards = [w[k][i:i + 1] if k in ("conv_w", "b_qkv") else w[k][i:i + 1].astype(BF16) for k, i in _USE_ORDER]
    first, _ = _exchange_start("gather_start_first", None, [_own_slot(s, chip) for s in shards[:2]])
    rest, _ = _exchange_start("gather_start_rest", None, [_own_slot(s, chip) for s in shards[2:]])
    pending = dict(zip(_USE_ORDER, first + rest))

    def getw(name, layer, after):
        got = _exchange_wait(f"gather_wait_{name}_{layer}", pending.pop((name, layer)), after)
        if name == "w_in_even":
            w_in = jnp.transpose(got[:, 0], (1, 0, 2)).reshape(D, EVEN_IN)
            return w_in[:, :EVEN_MAIN], jnp.pad(w_in[:, EVEN_MAIN:], ((0, 0), (0, LANE - (EVEN_IN - EVEN_MAIN))))
        if name == "conv_w":
            return jnp.transpose(got[:, 0], (1, 0, 2)).reshape(4, CONV_C)
        if name == "b_qkv":
            return got.reshape(1, ODD_IN)
        return got

    sent = []

    def putg(grads):
        names, arrs = [], []
        for (name, layer), g in grads.items():
            if name == "w_in_even":
                dw_in = jnp.concatenate([g[0], g[1][:, :EVEN_IN - EVEN_MAIN]], 1)
                g = jnp.transpose(dw_in.reshape(D, 4, EVEN_IN // 4), (1, 0, 2)).reshape(4, 1, D, EVEN_IN // 4)
            names.append((name, layer))
            arrs.append(g)
        own = [_own_slot(lax.dynamic_index_in_dim(g, chip, 0, keepdims=False), chip) for g in arrs]
        its, tok = _exchange_start(f"scatter_start_{names[0][0]}_{names[0][1]}", arrs, own)
        sent.append(list(zip(names, its)))
        return tok

    P = {k: w[k] for k in _SMALL}
    loss, grad_x, sm = _local_step(x[0], mem[0], positions[0], loss_target[0], getw, P, putg)
    loss = lax.psum(loss, ("x", "y", "c"))

    out = {}

    def update(groups, after):
        flying = []
        for grp in groups:
            part = []
            for (name, layer), it in grp:
                r = _exchange_wait(f"scatter_wait_{name}_{layer}", it, after)
                part.append(_sum_slots(f"sum_{name}_{layer}", r.reshape(4, -1, r.shape[-1]), 4))
            name0, layer0 = grp[0][0]
            its, _ = _exchange_start(f"swap_start_{name0}_{layer0}", part, [lax.empty(p.shape, p.dtype) for p in part], True)
            flying += [(nm, p, it) for (nm, _), p, it in zip(grp, part, its)]
        for (name, layer), p1, it in flying:
            p2 = _exchange_wait(f"swap_wait_{name}_{layer}", it, after, True)
            out[name] = _adamw(f"adamw_{name}_{layer}", w[name], mom[name], var[name], layer, p1, p2, out.get(name))
            after = out[name][0]

    update(sent[:-1], grad_x)
    done_a = out["w_out_even"][0]
    update(sent[-1:], done_a)

    full_shapes = {k: w[k].shape for k in _SMALL}
    full_shapes["conv_w"], full_shapes["b_qkv"] = (1, 4, CONV_C), (1, ODD_IN)
    packed = _pack([sm[k] for k in _SMALL])
    total = _sum_slots("sum_small", _gather_all("gather_small", packed, done_a), 8)
    gs = dict(zip(_SMALL, _unpack(total, [full_shapes[k] for k in _SMALL])))
    gs["conv_w"] = lax.dynamic_slice_in_dim(gs["conv_w"], chip * (CONV_C // 4), CONV_C // 4, 2)
    gs["b_qkv"] = lax.dynamic_slice_in_dim(gs["b_qkv"], chip * (ODD_IN // 4), ODD_IN // 4, 1)
    res = _adamw("adamw_small", _pack([w[k] for k in _SMALL])[None], _pack([mom[k] for k in _SMALL])[None],
                 _pack([var[k] for k in _SMALL])[None], 0, _pack([gs[k] for k in _SMALL]))
    shapes = [w[k].shape for k in _SMALL]
    for k, g, d, nm, nv in zip(_SMALL, *[_unpack(r[0], shapes) for r in res]):
        out[k] = [g, d, nm, nv]

    return (loss, grad_x[None], *[out[k][0] for k in _WEIGHTS], *[out[k][1] for k in _WEIGHTS],
            *[out[k][2] for k in _WEIGHTS], *[out[k][3] for k in _WEIGHTS])
```

```python
import functools
import math

import jax
import jax.numpy as jnp
from jax import lax
from jax.experimental import pallas as pl
from jax.experimental.pallas import tpu as pltpu

F32, BF16 = jnp.float32, jnp.bfloat16
S = jax.ShapeDtypeStruct
MESH = pl.DeviceIdType.MESH

D = 2048
DFF = 5632
EPS = 1e-5
CH = 128
GM_G, GM_GD = 4, 512
NH, HD, NG, HPG, NS = 32, 64, 4, 8, 128
CONV_C = 3072
EVEN_MAIN, EVEN_IN = 9216, 9248
AH, AKV, AREP, AHD = 32, 4, 8, 64
ODD_IN = 2560
XH, XHD, XW = 4, 128, 512
ATT_SCALE = AHD ** -0.5
X_SCALE = XHD ** -0.5
ROPE_THETA = 500000.0
ROT = 16
LR, B1, B2, AEPS, WD, STEP = 0.001, 0.9, 0.999, 1e-08, 0.01, 10
LANE = 128
VMEM_LIMIT_V7X = 56 * 1024 * 1024


def _params(*sem):
    return pltpu.CompilerParams(dimension_semantics=sem, vmem_limit_bytes=VMEM_LIMIT_V7X)


def _tile(dim, target):
    if dim <= target:
        return dim
    t = (target // LANE) * LANE
    while t > LANE and dim % t:
        t -= LANE
    assert dim % t == 0, (dim, target)
    return t


class Op:
    def __init__(self, arr, kind=None, layer=0):
        self.arr, self.kind, self.layer = arr, kind, layer
        if kind is None:
            self.R, self.C = arr.shape
        else:
            L = arr.shape[0]
            self.R = arr.shape[2] * (L if kind == "r" else 1)
            self.C = arr.shape[3] * (L if kind == "c" else 1)

    def unit(self, axis):
        if self.kind == "r" and axis == 0:
            return self.arr.shape[2]
        if self.kind == "c" and axis == 1:
            return self.arr.shape[3]
        return (self.R, self.C)[axis]

    def spec(self, tr, tc, pick):
        if self.kind is None:
            return pl.BlockSpec((tr, tc), lambda i, j, k: pick(i, j, k))
        l = self.layer
        if self.kind == "c":
            per = self.arr.shape[3] // tc
            return pl.BlockSpec((None, None, tr, tc),
                                lambda i, j, k: (pick(i, j, k)[1] // per, l, pick(i, j, k)[0], pick(i, j, k)[1] % per))
        per = self.arr.shape[2] // tr
        return pl.BlockSpec((None, None, tr, tc),
                            lambda i, j, k: (pick(i, j, k)[0] // per, l, pick(i, j, k)[0] % per, pick(i, j, k)[1]))


_DIMS = {"nn": (((1,), (0,)), ((), ())), "nt": (((1,), (1,)), ((), ())), "tn": (((0,), (0,)), ((), ()))}
_PICK_A = {"nn": lambda i, j, k: (i, k), "nt": lambda i, j, k: (i, k), "tn": lambda i, j, k: (k, i)}
_PICK_B = {"nn": lambda i, j, k: (k, j), "nt": lambda i, j, k: (j, k), "tn": lambda i, j, k: (k, j)}


def _mm(name, mode, a, b, out_dtype, *, out=None, res=None, bias=None, scale=1.0, norm_gain=None,
        tm_t=1024, tn_t=1024, tk_t=2048):
    if not isinstance(a, Op):
        a = Op(a)
    if not isinstance(b, Op):
        b = Op(b)
    if mode == "nn":
        M, K, N = a.R, a.C, b.C
        assert b.R == K
        um, uk, un = a.unit(0), math.gcd(a.unit(1), b.unit(0)), b.unit(1)
    elif mode == "nt":
        M, K, N = a.R, a.C, b.R
        assert b.C == K
        um, uk, un = a.unit(0), math.gcd(a.unit(1), b.unit(1)), b.unit(0)
    else:
        K, M, N = a.R, a.C, b.C
        assert b.R == K
        um, uk, un = a.unit(1), math.gcd(a.unit(0), b.unit(0)), b.unit(1)
    if out is not None:
        okind, oL, olayers, olayer = out
        if okind == "c":
            un = math.gcd(un, N // oL)
        else:
            um = math.gcd(um, M // oL)
    tm, tn, tk = _tile(um, tm_t), _tile(un, tn_t), _tile(uk, tk_t)
    gi, gj, gk = M // tm, N // tn, K // tk
    a_blk = (tm, tk) if mode != "tn" else (tk, tm)
    b_blk = {"nn": (tk, tn), "nt": (tn, tk), "tn": (tk, tn)}[mode]
    in_specs = [a.spec(*a_blk, _PICK_A[mode]), b.spec(*b_blk, _PICK_B[mode])]
    args = [a.arr, b.arr]
    if res is not None:
        in_specs.append(pl.BlockSpec((tm, tn), lambda i, j, k: (i, j)))
        args.append(res)
    if bias is not None:
        in_specs.append(pl.BlockSpec((1, tn), lambda i, j, k: (0, j)))
        args.append(bias)
    if out is None:
        out_shape = S((M, N), out_dtype)
        out_spec = pl.BlockSpec((tm, tn), lambda i, j, k: (i, j))
    else:
        shp = (oL, olayers, M, N // oL) if okind == "c" else (oL, olayers, M // oL, N)
        out_shape = S(shp, out_dtype)
        out_spec = Op(out_shape, okind, olayer).spec(tm, tn, lambda i, j, k: (i, j))
    has_res, has_bias, has_norm = res is not None, bias is not None, norm_gain is not None
    if has_norm:
        assert out is None and tn == N
        in_specs.append(pl.BlockSpec((1, N), lambda i, j, k: (0, 0)))
        args.append(norm_gain)
        out_shape = [out_shape, S((M, N), BF16)]
        out_spec = [out_spec, pl.BlockSpec((tm, tn), lambda i, j, k: (i, j))]
    dims = _DIMS[mode]

    def body(a_ref, b_ref, *rest):
        rest = list(rest)
        res_ref = rest.pop(0) if has_res else None
        bias_ref = rest.pop(0) if has_bias else None
        gain_ref = rest.pop(0) if has_norm else None
        o_ref = rest.pop(0)
        h_ref = rest.pop(0) if has_norm else None
        part = lax.dot_general(a_ref[...].astype(BF16), b_ref[...].astype(BF16), dims, preferred_element_type=F32)

        def finish(r):
            if scale != 1.0:
                r = r * scale
            if has_bias:
                r = r + bias_ref[...]
            if has_res:
                r = r + res_ref[...]
            o_ref[...] = r.astype(o_ref.dtype)
            if has_norm:
                h_ref[...] = (r * lax.rsqrt(jnp.mean(r * r, -1, keepdims=True) + EPS) * gain_ref[...]).astype(BF16)

        if gk == 1:
            finish(part)
            return
        acc, = rest
        k = pl.program_id(2)

        @pl.when(k == 0)
        def _():
            acc[...] = part

        @pl.when((k > 0) & (k < gk - 1))
        def _():
            acc[...] += part

        @pl.when(k == gk - 1)
        def _():
            finish(acc[...] + part)

    return pl.pallas_call(
        body, grid=(gi, gj, gk), in_specs=in_specs, out_specs=out_spec, out_shape=out_shape,
        scratch_shapes=[pltpu.VMEM((tm, tn), F32)] if gk > 1 else [],
        compiler_params=_params("parallel", "parallel", "arbitrary"), name=name)(*args)


def _rms_fwd(name, x, gain):
    T = x.shape[0]
    tt = _tile(T, 512)

    def body(x_ref, g_ref, o_ref):
        xv = x_ref[...]
        r = lax.rsqrt(jnp.mean(xv * xv, -1, keepdims=True) + EPS)
        o_ref[...] = (xv * r * g_ref[...]).astype(BF16)

    return pl.pallas_call(
        body, grid=(T // tt,),
        in_specs=[pl.BlockSpec((tt, D), lambda i: (i, 0)), pl.BlockSpec((1, D), lambda i: (0, 0))],
        out_specs=pl.BlockSpec((tt, D), lambda i: (i, 0)), out_shape=S((T, D), BF16),
        compiler_params=_params("parallel"), name=name)(x, gain)


def _rms_bwd(name, x, gain, dh, dx_in=None):
    T = x.shape[0]
    tt = _tile(T, 512)
    has_in = dx_in is not None

    def body(x_ref, g_ref, dh_ref, *rest):
        rest = list(rest)
        dxin_ref = rest.pop(0) if has_in else None
        dx_ref, dxb_ref, dg_ref = rest
        xv = x_ref[...]
        r = lax.rsqrt(jnp.mean(xv * xv, -1, keepdims=True) + EPS)
        xh = xv * r
        dy = dh_ref[...].astype(F32)
        dxh = dy * g_ref[...]
        dx = r * (dxh - xh * jnp.mean(dxh * xh, -1, keepdims=True))
        if has_in:
            dx = dx + dxin_ref[...]
        dx_ref[...] = dx
        dxb_ref[...] = dx.astype(BF16)
        part = jnp.sum(dy * xh, 0, keepdims=True)

        @pl.when(pl.program_id(0) == 0)
        def _():
            dg_ref[...] = part

        @pl.when(pl.program_id(0) > 0)
        def _():
            dg_ref[...] += part

    row = pl.BlockSpec((tt, D), lambda i: (i, 0))
    vec = pl.BlockSpec((1, D), lambda i: (0, 0))
    args = [x, gain, dh] + ([dx_in] if has_in else [])
    return pl.pallas_call(
        body, grid=(T // tt,), in_specs=[row, vec, row] + ([row] if has_in else []),
        out_specs=[row, row, vec], out_shape=[S((T, D), F32), S((T, D), BF16), S((1, D), F32)],
        compiler_params=_params("arbitrary"), name=name)(*args)


def _sigmoid(x):
    return 0.5 * jnp.tanh(0.5 * x) + 0.5


def _ffn_up(name, h, w4, layer, tm_t=512, tn_t=1408):
    T = h.shape[0]
    n_sh = w4.shape[3]
    tm, tn = _tile(T, tm_t), _tile(n_sh, tn_t)
    per = n_sh // tn

    def body(h_ref, wg_ref, wu_ref, g_ref, u_ref, a_ref):
        hv = h_ref[...]
        g = jnp.dot(hv, wg_ref[...], preferred_element_type=F32)
        u = jnp.dot(hv, wu_ref[...], preferred_element_type=F32)
        g_ref[...] = g.astype(BF16)
        u_ref[...] = u.astype(BF16)
        a_ref[...] = (g * _sigmoid(g) * u).astype(BF16)

    o = pl.BlockSpec((tm, tn), lambda j, i: (i, j))
    return pl.pallas_call(
        body, grid=(DFF // tn, T // tm),
        in_specs=[pl.BlockSpec((tm, D), lambda j, i: (i, 0)),
                  pl.BlockSpec((None, None, D, tn), lambda j, i: (j // per, layer, 0, j % per)),
                  pl.BlockSpec((None, None, D, tn), lambda j, i: (2 + j // per, layer, 0, j % per))],
        out_specs=[o, o, o], out_shape=[S((T, DFF), BF16)] * 3,
        compiler_params=_params("parallel", "parallel"), name=name)(h, w4, w4)


def _ffn_dact(name, dxb, wd4, layer, g, u, tm_t=512):
    T = dxb.shape[0]
    r_sh = wd4.shape[2]
    tm, tn = _tile(T, tm_t), _tile(r_sh, 1408)
    per = r_sh // tn

    def body(dx_ref, w_ref, g_ref, u_ref, o_ref):
        da = 0.5 * lax.dot_general(dx_ref[...], w_ref[...], _DIMS["nt"], preferred_element_type=F32)
        gv, uv = g_ref[...].astype(F32), u_ref[...].astype(F32)
        sg = _sigmoid(gv)
        o_ref[0, 0] = (da * uv * sg * (1.0 + gv * (1.0 - sg))).astype(BF16)
        o_ref[1, 0] = (da * gv * sg).astype(BF16)

    t = pl.BlockSpec((tm, tn), lambda j, i: (i, j))
    return pl.pallas_call(
        body, grid=(DFF // tn, T // tm),
        in_specs=[pl.BlockSpec((tm, D), lambda j, i: (i, 0)),
                  pl.BlockSpec((None, None, tn, D), lambda j, i: (j // per, layer, j % per, 0)), t, t],
        out_specs=pl.BlockSpec((2, 1, tm, tn), lambda j, i: (0, 0, i, j)), out_shape=S((2, 1, T, DFF), BF16),
        compiler_params=_params("parallel", "parallel"), name=name)(dxb, wd4, g, u)


def _gelu(x):
    return 0.5 * x * (1.0 + lax.erf(x * 0.7071067811865476))


def _causal(n):
    return lax.broadcasted_iota(jnp.int32, (n, n), 0) >= lax.broadcasted_iota(jnp.int32, (n, n), 1)


def _gmlp_math(u_raw, v_raw, lng, lnb, ws, bs):
    causal = _causal(CH)
    outs = []
    for g in range(GM_G):
        u, v = _gelu(u_raw[g]), _gelu(v_raw[g])
        mu = jnp.mean(v, -1, keepdims=True)
        var = jnp.mean(jnp.square(v - mu), -1, keepdims=True)
        vn = (v - mu) * lax.rsqrt(var + EPS) * lng[g] + lnb[g]
        wm = jnp.where(causal, ws[g], 0.0)
        s = jnp.dot(wm.astype(BF16), vn.astype(BF16), preferred_element_type=F32) + bs[g]
        outs.append(u * s)
    return outs


def _gmlp_load(proj_ref, lng_ref, lnb_ref, ws_ref, bs_ref):
    sl = lambda g, off: slice(off + g * GM_GD, off + (g + 1) * GM_GD)
    u_raw = [proj_ref[:, sl(g, 0)].astype(F32) for g in range(GM_G)]
    v_raw = [proj_ref[:, sl(g, D)].astype(F32) for g in range(GM_G)]
    lng = [lng_ref[:, sl(g, 0)] for g in range(GM_G)]
    lnb = [lnb_ref[:, sl(g, 0)] for g in range(GM_G)]
    ws = [ws_ref[g] for g in range(GM_G)]
    bs = [bs_ref[g] for g in range(GM_G)]
    return u_raw, v_raw, lng, lnb, ws, bs


_GM_PAR = lambda: [pl.BlockSpec((1, D), lambda i: (0, 0)), pl.BlockSpec((1, D), lambda i: (0, 0)),
                   pl.BlockSpec((GM_G, CH, CH), lambda i: (0, 0, 0)), pl.BlockSpec((GM_G, CH, 1), lambda i: (0, 0, 0))]


def _gmlp_fwd(name, proj, lng, lnb, ws, bs):
    T = proj.shape[0]

    def body(proj_ref, lng_ref, lnb_ref, ws_ref, bs_ref, o_ref):
        outs = _gmlp_math(*_gmlp_load(proj_ref, lng_ref, lnb_ref, ws_ref, bs_ref))
        for g in range(GM_G):
            o_ref[:, g * GM_GD:(g + 1) * GM_GD] = outs[g].astype(BF16)

    return pl.pallas_call(
        body, grid=(T // CH,), in_specs=[pl.BlockSpec((CH, 2 * D), lambda i: (i, 0))] + _GM_PAR(),
        out_specs=pl.BlockSpec((CH, D), lambda i: (i, 0)), out_shape=S((T, 2 * D), BF16),
        compiler_params=_params("parallel"), name=name)(proj, lng, lnb, ws, bs)


def _acc_store(first, ref, idx, val):
    @pl.when(first)
    def _():
        ref[idx] = val

    @pl.when(jnp.logical_not(first))
    def _():
        ref[idx] += val


def _gmlp_bwd(name, proj, lng, lnb, ws, bs, dmix, dproj):
    T = proj.shape[0]

    def body(proj_ref, lng_ref, lnb_ref, ws_ref, bs_ref, dmix_ref, _, dproj_ref, dlng_ref, dlnb_ref, dws_ref, dbs_ref):
        first = pl.program_id(0) == 0
        prim = _gmlp_load(proj_ref, lng_ref, lnb_ref, ws_ref, bs_ref)
        _, vjp = jax.vjp(_gmlp_math, *prim)
        du, dv, dlng, dlnb, dws, dbs = vjp([dmix_ref[:, g * GM_GD:(g + 1) * GM_GD].astype(F32) for g in range(GM_G)])
        for g in range(GM_G):
            sl = slice(g * GM_GD, (g + 1) * GM_GD)
            dproj_ref[:, sl] = du[g].astype(BF16)
            dproj_ref[:, D + g * GM_GD:D + (g + 1) * GM_GD] = dv[g].astype(BF16)
            _acc_store(first, dlng_ref, (slice(None), sl), dlng[g])
            _acc_store(first, dlnb_ref, (slice(None), sl), dlnb[g])
            _acc_store(first, dws_ref, g, dws[g])
            _acc_store(first, dbs_ref, g, dbs[g])

    par = _GM_PAR()
    return pl.pallas_call(
        body, grid=(T // CH,),
        in_specs=[pl.BlockSpec((CH, 2 * D), lambda i: (i, 0))] + par +
                 [pl.BlockSpec((CH, D), lambda i: (i, 0)), pl.BlockSpec(memory_space=pl.ANY)],
        out_specs=[pl.BlockSpec((CH, 2 * D), lambda i: (i, 0))] + par,
        out_shape=[S(dproj.shape, BF16), S((1, D), F32), S((1, D), F32), S((GM_G, CH, CH), F32), S((GM_G, CH, 1), F32)],
        input_output_aliases={6: 0}, compiler_params=_params("arbitrary"), name=name)(proj, lng, lnb, ws, bs, dmix, dproj)


CONV_TT = 256
HALO = 8


def _shift_rows(cur, halo_after, s):
    if s == 0:
        return cur
    n = cur.shape[0]
    return pltpu.roll(jnp.concatenate([cur, halo_after], 0), s, 0)[:n]


def _conv_fwd(name, proj, w, b):
    T = proj.shape[0]
    tt = _tile(T, CONV_TT)
    hb = tt // HALO

    def body(x_ref, halo_ref, w_ref, b_ref, y_ref, xc_ref):
        i = pl.program_id(0)
        x = x_ref[...].astype(F32)
        halo = halo_ref[...].astype(F32) * (i > 0).astype(F32)
        y = b_ref[...] + w_ref[3:4, :] * x
        for s in (1, 2, 3):
            y = y + w_ref[3 - s:4 - s, :] * _shift_rows(x, halo, s)
        y_ref[...] = y.astype(BF16)
        xc_ref[...] = (y * _sigmoid(y)).astype(BF16)

    o = pl.BlockSpec((tt, CONV_C), lambda i: (i, 0))
    return pl.pallas_call(
        body, grid=(T // tt,),
        in_specs=[pl.BlockSpec((tt, CONV_C), lambda i: (i, 2)),
                  pl.BlockSpec((HALO, CONV_C), lambda i: (jnp.maximum(i * hb - 1, 0), 2)),
                  pl.BlockSpec((4, CONV_C), lambda i: (0, 0)), pl.BlockSpec((1, CONV_C), lambda i: (0, 0))],
        out_specs=[o, o], out_shape=[S((T, CONV_C), BF16)] * 2,
        compiler_params=_params("parallel"), name=name)(proj, proj, w, b)


def _conv_bwd(name, proj, ypre, dxc, w, dproj):
    T = proj.shape[0]
    tt = _tile(T, CONV_TT)
    hb = tt // HALO
    nt = T // tt

    def dsilu(y):
        sg = _sigmoid(y)
        return sg * (1.0 + y * (1.0 - sg))

    def body(x_ref, xh_ref, y_ref, yn_ref, d_ref, dn_ref, w_ref, _, dproj_ref, dw_ref, db_ref):
        i = pl.program_id(0)
        first = i == 0
        x = x_ref[...].astype(F32)
        halo = xh_ref[...].astype(F32) * (i > 0).astype(F32)
        dy = d_ref[...].astype(F32) * dsilu(y_ref[...].astype(F32))
        dyn = dn_ref[...].astype(F32) * dsilu(yn_ref[...].astype(F32)) * (i < nt - 1).astype(F32)
        ext = jnp.concatenate([dy, dyn], 0)
        dx = w_ref[3:4, :] * dy
        _acc_store(first, dw_ref, (slice(3, 4), slice(None)), jnp.sum(x * dy, 0, keepdims=True))
        for s in (1, 2, 3):
            dx = dx + w_ref[3 - s:4 - s, :] * pltpu.roll(ext, tt + HALO - s, 0)[:tt]
            _acc_store(first, dw_ref, (slice(3 - s, 4 - s), slice(None)),
                       jnp.sum(_shift_rows(x, halo, s) * dy, 0, keepdims=True))
        _acc_store(first, db_ref, (slice(None), slice(None)), jnp.sum(dy, 0, keepdims=True))
        dproj_ref[...] = dx.astype(BF16)

    cur = pl.BlockSpec((tt, CONV_C), lambda i: (i, 0))
    nxt = pl.BlockSpec((HALO, CONV_C), lambda i: (jnp.minimum((i + 1) * hb, T // HALO - 1), 0))
    return pl.pallas_call(
        body, grid=(nt,),
        in_specs=[pl.BlockSpec((tt, CONV_C), lambda i: (i, 2)),
                  pl.BlockSpec((HALO, CONV_C), lambda i: (jnp.maximum(i * hb - 1, 0), 2)),
                  cur, nxt, cur, nxt, pl.BlockSpec((4, CONV_C), lambda i: (0, 0)), pl.BlockSpec(memory_space=pl.ANY)],
        out_specs=[pl.BlockSpec((tt, CONV_C), lambda i: (i, 2)), pl.BlockSpec((4, CONV_C), lambda i: (0, 0)),
                   pl.BlockSpec((1, CONV_C), lambda i: (0, 0))],
        out_shape=[S(dproj.shape, BF16), S((4, CONV_C), F32), S((1, CONV_C), F32)],
        input_output_aliases={7: 0}, compiler_params=_params("arbitrary"), name=name)(proj, proj, ypre, ypre, dxc, dxc, w, dproj)


def _softplus(x):
    return jnp.maximum(x, 0.0) + jnp.log(1.0 + jnp.exp(-jnp.abs(x)))


def _ssd_math(x, Bm, Cm, dtr, z, prev, dtb, alog, dsk, nrm):
    hi = lax.Precision.HIGHEST
    causal = _causal(CH)
    tri = causal.astype(F32)
    lane = lax.broadcasted_iota(jnp.int32, (1, LANE), 1)
    sub = lax.broadcasted_iota(jnp.int32, (LANE, 1), 0)
    dt = _softplus(dtr + dtb)
    a = dt * (-jnp.exp(alog))
    a_cs = jnp.dot(tri, a, preferred_element_type=F32, precision=hi)
    a_csT = lax.dot_general(a, tri, (((0,), (1,)), ((), ())), preferred_element_type=F32, precision=hi)
    a_last = jnp.sum(a, 0, keepdims=True)
    gw = HPG * HD
    from_prev = jnp.exp(a_cs)
    to_last = dt * jnp.exp(a_last - a_cs)
    whole = jnp.exp(a_last)
    outs, new = [], []
    for g in range(NG):
        spread = (lax.broadcasted_iota(jnp.int32, (LANE, gw), 0)
                  == g * HPG + lax.broadcasted_iota(jnp.int32, (LANE, gw), 1) // HD).astype(F32)
        to_lanes = lambda v: jnp.dot(v, spread, preferred_element_type=F32, precision=hi)
        whole_r = lax.dot_general(spread, whole, (((0,), (1,)), ((), ())), preferred_element_type=F32, precision=hi)
        cb = lax.dot_general(Cm[g].astype(BF16), Bm[g].astype(BF16), _DIMS["nt"], preferred_element_type=F32)
        xg = jnp.concatenate(x[g * HPG:(g + 1) * HPG], 1)
        yd = []
        for h in range(g * HPG, (g + 1) * HPG):
            ohl = (lane == h).astype(F32)
            col = jnp.sum(a_cs * ohl, 1, keepdims=True)
            row = jnp.sum(a_csT * (sub == h).astype(F32), 0, keepdims=True)
            dtc = jnp.sum(dt * ohl, 1, keepdims=True)
            lmat = jnp.where(causal, jnp.exp(jnp.where(causal, col - row, 0.0)), 0.0)
            yd.append(jnp.dot((cb * lmat).astype(BF16), (x[h] * dtc).astype(BF16), preferred_element_type=F32))
        y = jnp.concatenate(yd, 1)
        y = y + to_lanes(from_prev) * lax.dot_general(Cm[g].astype(BF16), prev[g].astype(BF16), _DIMS["nt"],
                                                      preferred_element_type=F32)
        st = lax.dot_general((xg * to_lanes(to_last)).astype(BF16), Bm[g].astype(BF16), _DIMS["tn"],
                             preferred_element_type=F32)
        new.append(prev[g] * whole_r + st)
        yg = (y + xg * to_lanes(dsk)) * (z[g] * _sigmoid(z[g]))
        yg = yg * lax.rsqrt(jnp.mean(yg * yg, -1, keepdims=True) + EPS)
        outs.append(yg * nrm[g])
    return outs, new


def _ssd_load(xc_ref, dtr_ref, z_ref, state_ref, dtb_ref, alog_ref, dsk_ref, nrm_ref):
    gw = HPG * HD
    x = [xc_ref[:, h * HD:(h + 1) * HD].astype(F32) for h in range(NH)]
    Bm = [xc_ref[:, D + g * NS:D + (g + 1) * NS].astype(F32) for g in range(NG)]
    Cm = [xc_ref[:, D + NG * NS + g * NS:D + NG * NS + (g + 1) * NS].astype(F32) for g in range(NG)]
    z = [z_ref[:, g * gw:(g + 1) * gw].astype(F32) for g in range(NG)]
    prev = [state_ref[g * gw:(g + 1) * gw, :] for g in range(NG)]
    nrm = [nrm_ref[:, g * gw:(g + 1) * gw] for g in range(NG)]
    return x, Bm, Cm, dtr_ref[...], z, prev, dtb_ref[...], alog_ref[...], dsk_ref[...], nrm


_SSD_PAR = lambda: [pl.BlockSpec((1, LANE), lambda c: (0, 0))] * 3 + [pl.BlockSpec((1, D), lambda c: (0, 0))]


def _ssd_fwd(name, xc, dtr, proj, dtb, alog, dsk, nrm, mix):
    T = xc.shape[0]
    nc = T // CH

    def body(xc_ref, dtr_ref, z_ref, dtb_ref, alog_ref, dsk_ref, nrm_ref, _, mix_ref, prev_ref, state):
        @pl.when(pl.program_id(0) == 0)
        def _():
            state[...] = jnp.zeros_like(state)

        prev_ref[...] = state[...]
        outs, new = _ssd_math(*_ssd_load(xc_ref, dtr_ref, z_ref, state, dtb_ref, alog_ref, dsk_ref, nrm_ref))
        for g in range(NG):
            mix_ref[:, g * 512:(g + 1) * 512] = outs[g].astype(BF16)
            state[g * 512:(g + 1) * 512, :] = new[g]

    return pl.pallas_call(
        body, grid=(nc,),
        in_specs=[pl.BlockSpec((CH, CONV_C), lambda c: (c, 0)), pl.BlockSpec((CH, LANE), lambda c: (c, 0)),
                  pl.BlockSpec((CH, D), lambda c: (c, 2))] + _SSD_PAR() + [pl.BlockSpec(memory_space=pl.ANY)],
        out_specs=[pl.BlockSpec((CH, D), lambda c: (c, 1)), pl.BlockSpec((None, NH * HD, NS), lambda c: (c, 0, 0))],
        out_shape=[S(mix.shape, BF16), S((nc, NH * HD, NS), F32)],
        scratch_shapes=[pltpu.VMEM((NH * HD, NS), F32)], input_output_aliases={7: 0},
        compiler_params=_params("arbitrary"), name=name)(xc, dtr, proj, dtb, alog, dsk, nrm, mix)


def _ssd_bwd(name, xc, dtr, proj, prevs, dtb, alog, dsk, nrm, dmix, dproj):
    T = xc.shape[0]
    nc = T // CH
    rev = lambda c: nc - 1 - c

    def body(xc_ref, dtr_ref, z_ref, prev_ref, dtb_ref, alog_ref, dsk_ref, nrm_ref, dmix_ref, _,
             dproj_ref, dxc_ref, ddtr_ref, ddtb_ref, dalog_ref, ddsk_ref, dnrm_ref, dstate):
        first = pl.program_id(0) == 0

        @pl.when(first)
        def _():
            dstate[...] = jnp.zeros_like(dstate)

        prim = _ssd_load(xc_ref, dtr_ref, z_ref, prev_ref, dtb_ref, alog_ref, dsk_ref, nrm_ref)
        _, vjp = jax.vjp(_ssd_math, *prim)
        douts = [dmix_ref[:, g * 512:(g + 1) * 512].astype(F32) for g in range(NG)]
        dnew = [dstate[g * 512:(g + 1) * 512, :] for g in range(NG)]
        dx, dB, dC, ddtr, dz, dprev, ddtb, dalog, ddsk, dnrm = vjp((douts, dnew))
        for h in range(NH):
            dxc_ref[:, h * HD:(h + 1) * HD] = dx[h].astype(BF16)
        for g in range(NG):
            dstate[g * 512:(g + 1) * 512, :] = dprev[g]
            dxc_ref[:, D + g * NS:D + (g + 1) * NS] = dB[g].astype(BF16)
            dxc_ref[:, D + NG * NS + g * NS:D + NG * NS + (g + 1) * NS] = dC[g].astype(BF16)
            dproj_ref[:, g * 512:(g + 1) * 512] = dz[g].astype(BF16)
            _acc_store(first, dnrm_ref, (slice(None), slice(g * 512, (g + 1) * 512)), dnrm[g])
        ddtr_ref[...] = ddtr
        _acc_store(first, ddtb_ref, (slice(None), slice(None)), ddtb)
        _acc_store(first, dalog_ref, (slice(None), slice(None)), dalog)
        _acc_store(first, ddsk_ref, (slice(None), slice(None)), ddsk)

    vec = pl.BlockSpec((1, LANE), lambda c: (0, 0))
    return pl.pallas_call(
        body, grid=(nc,),
        in_specs=[pl.BlockSpec((CH, CONV_C), lambda c: (rev(c), 0)), pl.BlockSpec((CH, LANE), lambda c: (rev(c), 0)),
                  pl.BlockSpec((CH, D), lambda c: (rev(c), 2)),
                  pl.BlockSpec((None, NH * HD, NS), lambda c: (rev(c), 0, 0))] + _SSD_PAR() +
                 [pl.BlockSpec((CH, D), lambda c: (rev(c), 1)), pl.BlockSpec(memory_space=pl.ANY)],
        out_specs=[pl.BlockSpec((CH, D), lambda c: (rev(c), 2)), pl.BlockSpec((CH, CONV_C), lambda c: (rev(c), 0)),
                   pl.BlockSpec((CH, LANE), lambda c: (rev(c), 0)), vec, vec, vec, pl.BlockSpec((1, D), lambda c: (0, 0))],
        out_shape=[S(dproj.shape, BF16), S((T, CONV_C), BF16), S((T, LANE), F32), S((1, LANE), F32), S((1, LANE), F32),
                   S((1, LANE), F32), S((1, D), F32)],
        scratch_shapes=[pltpu.VMEM((NH * HD, NS), F32)], input_output_aliases={9: 0},
        compiler_params=_params("arbitrary"), name=name)(xc, dtr, proj, prevs, dtb, alog, dsk, nrm, dmix, dproj)


def _rope(x, c, s, sign):
    W = x.shape[1]
    reps = W // LANE
    C, Sg = jnp.tile(c, (1, reps)), jnp.tile(s, (1, reps))
    lane = lax.broadcasted_iota(jnp.int32, x.shape, 1) % AHD
    up, dn = pltpu.roll(x, W - ROT // 2, 1), pltpu.roll(x, ROT // 2, 1)
    sw = jnp.where(lane < ROT // 2, up, jnp.where(lane < ROT, dn, 0.0))
    return x * C + sign * sw * Sg


def _rope_fwd(name, qkv, cos, sin):
    T = qkv.shape[0]
    tt = _tile(T, 256)
    KV = AKV * AHD

    def body(x_ref, c_ref, s_ref, o_ref):
        c, s = c_ref[...], s_ref[...]
        o_ref[:, :D] = _rope(x_ref[:, :D], c, s, 1.0).astype(BF16)
        o_ref[:, D:D + KV] = _rope(x_ref[:, D:D + KV], c, s, 1.0).astype(BF16)
        o_ref[:, D + KV:] = x_ref[:, D + KV:].astype(BF16)

    tab = pl.BlockSpec((tt, LANE), lambda i: (i, 0))
    return pl.pallas_call(
        body, grid=(T // tt,), in_specs=[pl.BlockSpec((tt, ODD_IN), lambda i: (i, 0)), tab, tab],
        out_specs=pl.BlockSpec((tt, ODD_IN), lambda i: (i, 0)), out_shape=S((T, ODD_IN), BF16),
        compiler_params=_params("parallel"), name=name)(qkv, cos, sin)


def _rope_bwd(name, dq, dkv_cur, dkv_prev, cos, sin):
    T = dq.shape[0]
    nb = T // CH
    KV = AKV * AHD

    def body(dq_ref, cur_ref, nxt_ref, c_ref, s_ref, o_ref, db_ref):
        n = pl.program_id(0)
        c, s = c_ref[...], s_ref[...]
        dkv = cur_ref[...] + nxt_ref[...] * (n < nb - 1).astype(F32)
        o_ref[:, :D] = _rope(dq_ref[...].astype(F32), c, s, -1.0).astype(BF16)
        o_ref[:, D:D + KV] = _rope(dkv[:, :KV], c, s, -1.0).astype(BF16)
        o_ref[:, D + KV:] = dkv[:, KV:].astype(BF16)
        _acc_store(n == 0, db_ref, (slice(None), slice(None)), jnp.sum(o_ref[...].astype(F32), 0, keepdims=True))

    tab = pl.BlockSpec((CH, LANE), lambda n: (n, 0))
    return pl.pallas_call(
        body, grid=(nb,),
        in_specs=[pl.BlockSpec((CH, D), lambda n: (n, 0)), pl.BlockSpec((CH, 2 * KV), lambda n: (n, 0)),
                  pl.BlockSpec((CH, 2 * KV), lambda n: (jnp.minimum(n + 1, nb - 1), 0)), tab, tab],
        out_specs=[pl.BlockSpec((CH, ODD_IN), lambda n: (n, 0)), pl.BlockSpec((1, ODD_IN), lambda n: (0, 0))],
        out_shape=[S((T, ODD_IN), BF16), S((1, ODD_IN), F32)],
        compiler_params=_params("arbitrary"), name=name)(dq, dkv_cur, dkv_prev, cos, sin)


def _swa_math(q, kp, kc, vp, vc, snk, mask):
    outs = []
    for k in range(AKV):
        K = jnp.concatenate([kp[k], kc[k]], 0).astype(BF16)
        V = jnp.concatenate([vp[k], vc[k]], 0).astype(BF16)
        s = lax.dot_general(q[k].astype(BF16), K, _DIMS["nt"], preferred_element_type=F32) * ATT_SCALE
        s = jnp.where(mask, s, -jnp.inf)
        m = lax.stop_gradient(jnp.maximum(jnp.max(s, -1, keepdims=True), snk[k]))
        p = jnp.exp(s - m)
        pr = p / (jnp.sum(p, -1, keepdims=True) + jnp.exp(snk[k] - m))
        outs.append(jnp.dot(pr.astype(BF16), V, preferred_element_type=F32))
    return outs


def _stack_heads(ref, k):
    return jnp.concatenate([ref[:, (k * AREP + r) * AHD:(k * AREP + r + 1) * AHD].astype(F32) for r in range(AREP)], 0)


def _swa_load(q_ref, cur_ref, prv_ref, snk_ref):
    KV = AKV * AHD
    q = [_stack_heads(q_ref, k) for k in range(AKV)]
    kc = [cur_ref[:, k * AHD:(k + 1) * AHD].astype(F32) for k in range(AKV)]
    vc = [cur_ref[:, KV + k * AHD:KV + (k + 1) * AHD].astype(F32) for k in range(AKV)]
    kp = [prv_ref[:, k * AHD:(k + 1) * AHD].astype(F32) for k in range(AKV)]
    vp = [prv_ref[:, KV + k * AHD:KV + (k + 1) * AHD].astype(F32) for k in range(AKV)]
    snk = [jnp.concatenate([jnp.broadcast_to(snk_ref[:, k * AREP + r:k * AREP + r + 1], (CH, 1)) for r in range(AREP)], 0)
           for k in range(AKV)]
    return q, kp, kc, vp, vc, snk


def _swa_mask(n):
    iq = lax.broadcasted_iota(jnp.int32, (AREP * CH, 2 * CH), 0) % CH
    js = lax.broadcasted_iota(jnp.int32, (AREP * CH, 2 * CH), 1)
    rel = iq + CH - js
    return (rel >= 0) & (rel < CH) & ((n > 0) | (js >= CH))


def _swa_specs(T):
    KV = AKV * AHD
    return [pl.BlockSpec((CH, D), lambda n: (n, 0)), pl.BlockSpec((CH, 2 * KV), lambda n: (n, D // (2 * KV))),
            pl.BlockSpec((CH, 2 * KV), lambda n: (jnp.maximum(n - 1, 0), D // (2 * KV))),
            pl.BlockSpec((1, LANE), lambda n: (0, 0))]


def _swa_fwd(name, qkvr, snk):
    T = qkvr.shape[0]

    def body(q_ref, cur_ref, prv_ref, snk_ref, o_ref):
        outs = _swa_math(*_swa_load(q_ref, cur_ref, prv_ref, snk_ref), _swa_mask(pl.program_id(0)))
        for h in range(AH):
            k, r = divmod(h, AREP)
            o_ref[:, h * AHD:(h + 1) * AHD] = outs[k][r * CH:(r + 1) * CH].astype(BF16)

    return pl.pallas_call(
        body, grid=(T // CH,), in_specs=_swa_specs(T), out_specs=pl.BlockSpec((CH, D), lambda n: (n, 0)),
        out_shape=S((T, D), BF16), compiler_params=_params("parallel"), name=name)(qkvr, qkvr, qkvr, snk)


def _swa_bwd(name, qkvr, snk, do):
    T = qkvr.shape[0]
    KV = AKV * AHD

    def body(q_ref, cur_ref, prv_ref, snk_ref, do_ref, dq_ref, dcur_ref, dprv_ref, dsnk_ref):
        n = pl.program_id(0)

        @pl.when(n == 0)
        def _():
            dsnk_ref[...] = jnp.zeros_like(dsnk_ref)

        prim = _swa_load(q_ref, cur_ref, prv_ref, snk_ref)
        mask = _swa_mask(n)
        _, vjp = jax.vjp(lambda *p: _swa_math(*p, mask), *prim)
        dq, dkp, dkc, dvp, dvc, dsnk = vjp([_stack_heads(do_ref, k) for k in range(AKV)])
        for h in range(AH):
            k, r = divmod(h, AREP)
            dq_ref[:, h * AHD:(h + 1) * AHD] = dq[k][r * CH:(r + 1) * CH].astype(BF16)
            dsnk_ref[:, h:h + 1] += jnp.sum(dsnk[k][r * CH:(r + 1) * CH], 0, keepdims=True)
        for k in range(AKV):
            dcur_ref[:, k * AHD:(k + 1) * AHD] = dkc[k]
            dcur_ref[:, KV + k * AHD:KV + (k + 1) * AHD] = dvc[k]
            dprv_ref[:, k * AHD:(k + 1) * AHD] = dkp[k]
            dprv_ref[:, KV + k * AHD:KV + (k + 1) * AHD] = dvp[k]

    kv = pl.BlockSpec((CH, 2 * KV), lambda n: (n, 0))
    return pl.pallas_call(
        body, grid=(T // CH,), in_specs=_swa_specs(T) + [pl.BlockSpec((CH, D), lambda n: (n, 0))],
        out_specs=[pl.BlockSpec((CH, D), lambda n: (n, 0)), kv, kv, pl.BlockSpec((1, LANE), lambda n: (0, 0))],
        out_shape=[S((T, D), BF16), S((T, 2 * KV), F32), S((T, 2 * KV), F32), S((1, LANE), F32)],
        compiler_params=_params("arbitrary"), name=name)(qkvr, qkvr, qkvr, snk, do)


def _xat_math(q, k, v):
    outs = []
    for h in range(XH):
        s = lax.dot_general(q[h].astype(BF16), k[h].astype(BF16), _DIMS["nt"], preferred_element_type=F32) * X_SCALE
        m = lax.stop_gradient(jnp.max(s, -1, keepdims=True))
        p = jnp.exp(s - m)
        pr = p / jnp.sum(p, -1, keepdims=True)
        outs.append(jnp.dot(pr.astype(BF16), v[h].astype(BF16), preferred_element_type=F32))
    return outs


def _xat_load(q_ref, kv_ref):
    q = [q_ref[:, h * XHD:(h + 1) * XHD].astype(F32) for h in range(XH)]
    k = [kv_ref[:, h * XHD:(h + 1) * XHD].astype(F32) for h in range(XH)]
    v = [kv_ref[:, XW + h * XHD:XW + (h + 1) * XHD].astype(F32) for h in range(XH)]
    return q, k, v


def _xat_fwd(name, q, kv):
    T, M = q.shape[0], kv.shape[0]
    tt = _tile(T, 512)

    def body(q_ref, kv_ref, o_ref):
        outs = _xat_math(*_xat_load(q_ref, kv_ref))
        for h in range(XH):
            o_ref[:, h * XHD:(h + 1) * XHD] = outs[h].astype(BF16)

    return pl.pallas_call(
        body, grid=(T // tt,),
        in_specs=[pl.BlockSpec((tt, XW), lambda i: (i, 0)), pl.BlockSpec((M, 2 * XW), lambda i: (0, 0))],
        out_specs=pl.BlockSpec((tt, XW), lambda i: (i, 0)), out_shape=S((T, XW), BF16),
        compiler_params=_params("parallel"), name=name)(q, kv)


def _xat_bwd(name, q, kv, do):
    T, M = q.shape[0], kv.shape[0]
    tt = _tile(T, 512)

    def body(q_ref, kv_ref, do_ref, dq_ref, dkv_ref):
        first = pl.program_id(0) == 0
        _, vjp = jax.vjp(_xat_math, *_xat_load(q_ref, kv_ref))
        dq, dk, dv = vjp([do_ref[:, h * XHD:(h + 1) * XHD].astype(F32) for h in range(XH)])
        for h in range(XH):
            sl = slice(h * XHD, (h + 1) * XHD)
            dq_ref[:, sl] = dq[h].astype(BF16)
            _acc_store(first, dkv_ref, (slice(None), sl), dk[h])
            _acc_store(first, dkv_ref, (slice(None), slice(XW + h * XHD, XW + (h + 1) * XHD)), dv[h])

    qs = pl.BlockSpec((tt, XW), lambda i: (i, 0))
    kvs = pl.BlockSpec((M, 2 * XW), lambda i: (0, 0))
    return pl.pallas_call(
        body, grid=(T // tt,), in_specs=[qs, kvs, qs], out_specs=[qs, kvs],
        out_shape=[S((T, XW), BF16), S((M, 2 * XW), F32)],
        compiler_params=_params("arbitrary"), name=name)(q, kv, do)


def _loss_head(name, x, gain, target):
    T = x.shape[0]
    tt = _tile(T, 512)

    def body(x_ref, g_ref, t_ref, l_ref, dx_ref, dxb_ref, dg_ref):
        first = pl.program_id(0) == 0
        xv, g = x_ref[...], g_ref[...]
        r = lax.rsqrt(jnp.mean(xv * xv, -1, keepdims=True) + EPS)
        xh = xv * r
        e = xh * g - t_ref[...]
        part = 0.5 * jnp.sum(jnp.mean(e * e, -1, keepdims=True), (0, 1), keepdims=True)
        _acc_store(first, l_ref, (slice(None), slice(None)), jnp.broadcast_to(part, (1, LANE)))
        dy = e * (1.0 / D)
        dxh = dy * g
        dx = r * (dxh - xh * jnp.mean(dxh * xh, -1, keepdims=True))
        dx_ref[...] = dx
        dxb_ref[...] = dx.astype(BF16)
        _acc_store(first, dg_ref, (slice(None), slice(None)), jnp.sum(dy * xh, 0, keepdims=True))

    row = pl.BlockSpec((tt, D), lambda i: (i, 0))
    vec = pl.BlockSpec((1, D), lambda i: (0, 0))
    return pl.pallas_call(
        body, grid=(T // tt,), in_specs=[row, vec, row],
        out_specs=[pl.BlockSpec((1, LANE), lambda i: (0, 0)), row, row, vec],
        out_shape=[S((1, LANE), F32), S((T, D), F32), S((T, D), BF16), S((1, D), F32)],
        compiler_params=_params("arbitrary"), name=name)(x, gain, target)


def _out_proj(name, a, b, x, next_gain, scale=1.0, tk_t=2048, plain=(1024, 1024)):
    if next_gain is None:
        return _mm(name, "nn", a, b, F32, res=x, scale=scale, tm_t=plain[0], tn_t=plain[1], tk_t=tk_t), None
    return _mm(name, "nn", a, b, F32, res=x, scale=scale, norm_gain=next_gain, tm_t=512, tn_t=D, tk_t=min(tk_t, 2048))


def _ffn_fwd(tag, x, h, gain, wgu4, get_wd, next_gain):
    g, u, a = _ffn_up(f"{tag}_up", h, wgu4, 0)
    wd = get_wd(a).reshape(1, 1, DFF, D)
    x_new, _ = _out_proj(f"{tag}_down", a, Op(wd, "r"), x, None, 0.5, 2816)
    h_next = None if next_gain is None else _rms_fwd(f"{tag}_nextnorm", x_new, next_gain)
    return x_new, h_next, (x, gain, h, g, u, a)


def _ffn_bwd(tag, saved, dx, dxb, wgu4, wd4, put):
    x, gain, h, g, u, a = saved
    dgu = _ffn_dact(f"{tag}_dact", dxb, wd4, 0, g, u, 1024)
    dwd = _mm(f"{tag}_dwd", "tn", a, dxb, BF16, out=("r", 4, 1, 0), scale=0.5, tm_t=1408, tn_t=1024, tk_t=2048)
    dwgu = _mm(f"{tag}_dwgu", "tn", h, Op(dgu, "c"), BF16, out=("c", 4, 1, 0), tm_t=1024, tn_t=256, tk_t=8192)
    tok = put(dwgu, dwd)
    dh = _mm(f"{tag}_dh", "nt", Op(dgu, "c"), Op(wgu4, "c"), BF16, bias=jnp.zeros((1, D), F32) + tok, tk_t=2816)
    dx, dxb, dgain = _rms_bwd(f"{tag}_dnorm", x, gain, dh, dx)
    return dx, dxb, dgain


def _xattn_fwd(tag, x, hq, mem, gq, gm, wxq4, wxkv4, wxo4, next_gain):
    mn = _rms_fwd(f"{tag}_normm", mem, gm)
    q = _mm(f"{tag}_q", "nn", hq, Op(wxq4, "r"), BF16)
    kv = _mm(f"{tag}_kv", "nn", mn, Op(wxkv4, "r"), BF16)
    o = _xat_fwd(f"{tag}_att", q, kv)
    wxo = jnp.transpose(wxo4[:, 0], (1, 0, 2)).reshape(XW, D)
    x_new, h_next = _out_proj(f"{tag}_o", o, wxo, x, next_gain)
    return x_new, h_next, (x, mem, gq, gm, hq, mn, q, kv, o)


def _xattn_bwd(tag, saved, dx, dxb, wxq4, wxkv4, wxo4, put):
    x, mem, gq, gm, hq, mn, q, kv, o = saved
    dwxo = _mm(f"{tag}_dwo", "tn", o, dxb, BF16, out=("c", 4, 1, 0))
    do = _mm(f"{tag}_do", "nt", dxb, Op(wxo4, "c"), BF16)
    dq, dkv = _xat_bwd(f"{tag}_datt", q, kv, do)
    dwxq = _mm(f"{tag}_dwq", "tn", hq, dq, BF16, out=("r", 4, 1, 0))
    dwxkv = _mm(f"{tag}_dwkv", "tn", mn, dkv, BF16, out=("r", 4, 1, 0))
    tok = put(dwxq, dwxkv, dwxo)
    dhq = _mm(f"{tag}_dhq", "nt", dq, Op(wxq4, "r"), BF16, bias=jnp.zeros((1, D), F32) + tok)
    dmn = _mm(f"{tag}_dmn", "nt", dkv, Op(wxkv4, "r"), BF16)
    _, _, dgm = _rms_bwd(f"{tag}_dnormm", mem, gm, dmn)
    dx, dxb, dgq = _rms_bwd(f"{tag}_dnormq", x, gq, dhq, dx)
    return dx, dxb, dgq, dgm


def _even_fwd(tag, x, h, gain, w_main, w_dt, p, wout4, next_gain):
    proj = _mm(f"{tag}_in", "nn", h, w_main, BF16)
    dtr = _mm(f"{tag}_indt", "nn", h, w_dt, F32)
    mix = _gmlp_fwd(f"{tag}_gmlp", proj, p["lng"], p["lnb"], p["ws"], p["bs"])
    ypre, xc = _conv_fwd(f"{tag}_conv", proj, p["cw"], p["cb"])
    mix, prevs = _ssd_fwd(f"{tag}_ssd", xc, dtr, proj, p["dtb"], p["alog"], p["dsk"], p["nrm"], mix)
    x_new, h_next = _out_proj(f"{tag}_out", mix, Op(wout4.reshape(1, 1, 2 * D, D), "r"), x, next_gain)
    return x_new, h_next, (x, gain, h, proj, dtr, mix, ypre, xc, prevs)


def _even_bwd(tag, saved, dx, dxb, w_main, w_dt, p, wout4, put):
    x, gain, h, proj, dtr, mix, ypre, xc, prevs = saved
    T = x.shape[0]
    dwout = _mm(f"{tag}_dwout", "tn", mix, dxb, BF16, out=("r", 4, 1, 0))
    dmix = _mm(f"{tag}_dmix", "nt", dxb, Op(wout4, "r", 0), BF16)
    dproj = lax.empty((T, EVEN_MAIN), BF16)
    dproj, dlng, dlnb, dws, dbs = _gmlp_bwd(f"{tag}_dgmlp", proj, p["lng"], p["lnb"], p["ws"], p["bs"], dmix, dproj)
    dproj, dxc, ddtr, ddtb, dalog, ddsk, dnrm = _ssd_bwd(
        f"{tag}_dssd", xc, dtr, proj, prevs, p["dtb"], p["alog"], p["dsk"], p["nrm"], dmix, dproj)
    dproj, dcw, dcb = _conv_bwd(f"{tag}_dconv", proj, ypre, dxc, p["cw"], dproj)
    dw_main = _mm(f"{tag}_dwin", "tn", h, dproj, BF16, tm_t=1024, tn_t=256, tk_t=8192)
    dw_dt = _mm(f"{tag}_dwdt", "tn", h, ddtr, BF16)
    tok = put(dw_main, dw_dt, dwout)
    dh = _mm(f"{tag}_dh1", "nt", ddtr, w_dt + tok.astype(BF16), F32)
    dh = _mm(f"{tag}_dh2", "nt", dproj, w_main, BF16, res=dh)
    dx, dxb, dgain = _rms_bwd(f"{tag}_dnorm", x, gain, dh, dx)
    small = dict(lng=dlng, lnb=dlnb, ws=dws, bs=dbs, cw=dcw, cb=dcb, dtb=ddtb, alog=dalog, dsk=ddsk, nrm=dnrm)
    return dx, dxb, dgain, small


def _odd_fwd(tag, x, h, gain, wqkv4, bqkv, snk, wo4, cos, sin, next_gain):
    qkv = _mm(f"{tag}_qkv", "nn", h, Op(wqkv4, "c", 0), F32, bias=bqkv, tn_t=640)
    qkvr = _rope_fwd(f"{tag}_rope", qkv, cos, sin)
    o = _swa_fwd(f"{tag}_swa", qkvr, snk)
    x_new, h_next = _out_proj(f"{tag}_o", o, Op(wo4.reshape(1, 1, D, D), "r"), x, next_gain)
    return x_new, h_next, (x, gain, h, qkvr, o)


def _odd_bwd(tag, saved, dx, dxb, wqkv4, snk, wo4, cos, sin, put):
    x, gain, h, qkvr, o = saved
    dwo = _mm(f"{tag}_dwo", "tn", o, dxb, BF16, out=("r", 4, 1, 0))
    do = _mm(f"{tag}_do", "nt", dxb, Op(wo4, "r", 0), BF16)
    dq, dcur, dprv, dsnk = _swa_bwd(f"{tag}_dswa", qkvr, snk, do)
    dqkv, dbias = _rope_bwd(f"{tag}_drope", dq, dcur, dprv, cos, sin)
    dwqkv = _mm(f"{tag}_dwqkv", "tn", h, dqkv, BF16, out=("c", 4, 1, 0), tn_t=640)
    tok = put(dwqkv, dwo)
    dh = _mm(f"{tag}_dh", "nt", dqkv, Op(wqkv4, "c", 0), BF16, bias=jnp.zeros((1, D), F32) + tok, tk_t=640)
    dx, dxb, dgain = _rms_bwd(f"{tag}_dnorm", x, gain, dh, dx)
    return dx, dxb, dgain, dbias, dsnk


def _row(v):
    return v.reshape(1, -1).astype(F32)


def _pad_lanes(v, n=LANE):
    v = v.reshape(1, -1).astype(F32)
    return jnp.pad(v, ((0, 0), (0, n - v.shape[1])))


def _local_step(x, mem, positions, target, getw, P, putg):
    inv_freq = ROPE_THETA ** (-jnp.arange(0, ROT, 2, dtype=F32) / ROT)
    ang = positions.astype(F32)[:, None] * inv_freq
    cos8, sin8 = jnp.cos(ang), jnp.sin(ang)
    ones, zeros = jnp.ones((x.shape[0], AHD - ROT), F32), jnp.zeros((x.shape[0], AHD - ROT), F32)
    cos = jnp.tile(jnp.concatenate([cos8, cos8, ones], 1), (1, 2))
    sin = jnp.tile(jnp.concatenate([-sin8, sin8, zeros], 1), (1, 2))

    snk = _pad_lanes(P["sinks"])
    W = {}

    def w(name, layer, after):
        if (name, layer) not in W:
            W[name, layer] = getw(name, layer, after)
        return W[name, layer]

    saved = []
    h = _rms_fwd("l0_ffn1_norm", x, _row(P["norm_ffn1"][0]))
    for i in range(2):
        x, h, s1 = _ffn_fwd(f"l{i}_ffn1", x, h, _row(P["norm_ffn1"][i]), w("w_ffn1_gu", i, x),
                            functools.partial(w, "w_ffn1_down", i), _row(P["norm_mix"][i]))
        if i == 0:
            ev = dict(lng=_row(P["gm_ln_g"]), lnb=_row(P["gm_ln_b"]), ws=P["gm_ws"].reshape(GM_G, CH, CH),
                      bs=P["gm_bs"].reshape(GM_G, CH, 1), cw=w("conv_w", 0, x), cb=_row(P["conv_b"]),
                      dtb=_pad_lanes(P["dt_bias"]), alog=_pad_lanes(P["a_log"]), dsk=_pad_lanes(P["d_skip"]),
                      nrm=_row(P["ssd_norm"]))
            w_main, w_dt = w("w_in_even", 0, x)
            x, h, s2 = _even_fwd("l0_mix", x, h, _row(P["norm_mix"][0]), w_main, w_dt, ev, w("w_out_even", 0, x),
                                 _row(P["norm_xq"][0]))
        else:
            x, h, s2 = _odd_fwd("l1_mix", x, h, _row(P["norm_mix"][1]), w("w_qkv", 0, x), w("b_qkv", 0, x), snk,
                                w("w_o_odd", 0, x), cos, sin, _row(P["norm_xq"][1]))
        x, h, s3 = _xattn_fwd(f"l{i}_xat", x, h, mem, _row(P["norm_xq"][i]), _row(P["norm_mem"][i]),
                              w("w_xq", i, x), w("w_xkv", i, x), w("w_xo", i, x), _row(P["norm_ffn2"][i]))
        x, h, s4 = _ffn_fwd(f"l{i}_ffn2", x, h, _row(P["norm_ffn2"][i]), w("w_ffn2_gu", i, x),
                            functools.partial(w, "w_ffn2_down", i), _row(P["norm_ffn1"][1]) if i == 0 else None)
        saved.append((s1, s2, s3, s4))

    loss, dx, dxb, d_final = _loss_head("loss_head", x, _row(P["final_norm"]), target)

    sm = {}
    dn = {k: [None, None] for k in ("norm_ffn1", "norm_mix", "norm_xq", "norm_mem", "norm_ffn2")}
    for i in (1, 0):
        s1, s2, s3, s4 = saved[i]
        dx, dxb, dn["norm_ffn2"][i] = _ffn_bwd(
            f"l{i}_ffn2", s4, dx, dxb, W["w_ffn2_gu", i], W["w_ffn2_down", i],
            lambda dwgu, dwd, i=i: putg({("w_ffn2_gu", i): dwgu, ("w_ffn2_down", i): dwd}))
        dx, dxb, dn["norm_xq"][i], dn["norm_mem"][i] = _xattn_bwd(
            f"l{i}_xat", s3, dx, dxb, W["w_xq", i], W["w_xkv", i], W["w_xo", i],
            lambda dwxq, dwxkv, dwxo, i=i: putg({("w_xq", i): dwxq, ("w_xkv", i): dwxkv, ("w_xo", i): dwxo}))
        if i == 0:
            dx, dxb, dn["norm_mix"][0], sm_even = _even_bwd(
                "l0_mix", s2, dx, dxb, w_main, w_dt, ev, W["w_out_even", 0],
                lambda dw_main, dw_dt, dwout: putg({("w_in_even", 0): (dw_main, dw_dt), ("w_out_even", 0): dwout}))
        else:
            dx, dxb, dn["norm_mix"][1], sm["b_qkv"], sm["sinks"] = _odd_bwd(
                "l1_mix", s2, dx, dxb, W["w_qkv", 0], snk, W["w_o_odd", 0], cos, sin,
                lambda dwqkv, dwo: putg({("w_qkv", 0): dwqkv, ("w_o_odd", 0): dwo}))
        dx, dxb, dn["norm_ffn1"][i] = _ffn_bwd(
            f"l{i}_ffn1", s1, dx, dxb, W["w_ffn1_gu", i], W["w_ffn1_down", i],
            lambda dwgu, dwd, i=i: putg({("w_ffn1_gu", i): dwgu, ("w_ffn1_down", i): dwd}))
    for k, v in dn.items():
        sm[k] = jnp.concatenate(v, 0)
    sm.update(gm_ln_g=sm_even["lng"], gm_ln_b=sm_even["lnb"], gm_ws=sm_even["ws"], gm_bs=sm_even["bs"],
              conv_w=sm_even["cw"], conv_b=sm_even["cb"], dt_bias=sm_even["dtb"][:, :NH], a_log=sm_even["alog"][:, :NH],
              d_skip=sm_even["dsk"][:, :NH], ssd_norm=sm_even["nrm"], sinks=sm["sinks"][:, :AH], final_norm=d_final)
    return loss[0, 0], dx, sm


def _chip_peers():
    x, y, c = lax.axis_index("x"), lax.axis_index("y"), lax.axis_index("c")
    return 2 * x + y, [((1 - x, y, c), 2 * (1 - x) + y), ((x, 1 - y, c), 2 * x + (1 - y)),
                       ((1 - x, 1 - y, c), 2 * (1 - x) + (1 - y))]


def _any_specs(n):
    return [pl.BlockSpec(memory_space=pl.ANY)] * n


_HBM = pl.BlockSpec(memory_space=pltpu.HBM)
_SEM = pl.BlockSpec(memory_space=pltpu.SEMAPHORE)
_EFFECT = pltpu.SideEffectType.DATAFLOW_SIDE_EFFECTING


def _own_slot(piece, chip):
    zone = lax.empty((4,) + piece.shape, piece.dtype)
    return lax.dynamic_update_slice(zone, piece[None], (chip,) + (0,) * piece.ndim)


def _chip_copies(srcs, lands, ssems, rsems, mode="chips"):
    c = lax.axis_index("c")
    sib = (lax.axis_index("x"), lax.axis_index("y"), 1 - c)
    if mode == "sibling":
        return [pltpu.make_async_remote_copy(src_ref=srcs[i], dst_ref=lands[i], send_sem=ssems[i].at[0],
                                             recv_sem=rsems[i].at[0], device_id=sib, device_id_type=MESH)
                for i in range(len(lands))]
    me, peers = _chip_peers()
    if mode == "half":
        return [pltpu.make_async_remote_copy(src_ref=lands[i].at[me, c], dst_ref=lands[i].at[me, c], send_sem=ssems[i].at[j],
                                             recv_sem=rsems[i].at[j], device_id=dev, device_id_type=MESH)
                for i in range(len(lands)) for j, (dev, _) in enumerate(peers)]
    if mode == "pass_on":
        return [pltpu.make_async_remote_copy(src_ref=lands[i].at[chip, c], dst_ref=lands[i].at[chip, c],
                                             send_sem=ssems[i].at[j], recv_sem=rsems[i].at[j], device_id=sib, device_id_type=MESH)
                for i in range(len(lands)) for j, (_, chip) in enumerate(peers)]
    return [pltpu.make_async_remote_copy(src_ref=lands[i].at[me] if srcs[i] is None else srcs[i].at[chip],
                                         dst_ref=lands[i].at[me], send_sem=ssems[i].at[j], recv_sem=rsems[i].at[j],
                                         device_id=dev, device_id_type=MESH)
            for i in range(len(lands)) for j, (dev, chip) in enumerate(peers)]


def _exchange_start(name, srcs, lands, mode="chips"):
    n = len(lands)
    ns = 0 if srcs is None else n

    def body(*refs):
        src_refs = [None] * n if srcs is None else refs[:n]
        land_refs = refs[ns:ns + n]
        ssems, rsems = refs[ns + n:ns + 2 * n], refs[ns + 2 * n:ns + 3 * n]
        token = refs[2 * ns + 4 * n]
        for cp in _chip_copies(src_refs, land_refs, ssems, rsems, mode):
            cp.start()
        token[...] = jnp.zeros_like(token)

    ins = ([] if srcs is None else list(srcs)) + list(lands)
    res = pl.pallas_call(
        body, name=name,
        out_shape=[pltpu.SemaphoreType.DMA((1 if mode == "sibling" else 3,))] * (2 * n) + [pltpu.HBM(a.shape, a.dtype) for a in ins]
        + [S((8, LANE), F32)],
        in_specs=[_HBM] * (ns + n),
        out_specs=[_SEM] * (2 * n) + [_HBM] * (ns + n) + [pl.BlockSpec(memory_space=pltpu.VMEM)],
        input_output_aliases={i: 2 * n + i for i in range(ns + n)},
        compiler_params=pltpu.CompilerParams(has_side_effects=_EFFECT),
    )(*[pltpu.with_memory_space_constraint(a, pltpu.HBM) for a in ins])
    items = [(res[i], res[n + i], None if srcs is None else res[2 * n + i], res[2 * n + ns + i]) for i in range(n)]
    return items, res[2 * n + ns + n][0, 0]


def _exchange_wait(name, item, after, mode="chips"):
    ssem, rsem, src, land = item
    ns = 0 if src is None else 1

    def body(*refs):
        src_ref = refs[0] if ns else None
        land_ref, ssem_ref, rsem_ref = refs[ns], refs[ns + 1], refs[ns + 2]
        for cp in _chip_copies([src_ref], [land_ref], [ssem_ref], [rsem_ref], mode):
            cp.wait_send()
            cp.wait_recv()

    ins = ([src] if ns else []) + [land]
    return pl.pallas_call(
        body, name=name, out_shape=[pltpu.HBM(a.shape, a.dtype) for a in ins],
        in_specs=[_HBM] * (ns + 1) + [_SEM, _SEM, pl.BlockSpec(memory_space=pl.ANY)], out_specs=[_HBM] * (ns + 1),
        input_output_aliases={i: i for i in range(ns + 1)}, compiler_params=pltpu.CompilerParams(has_side_effects=_EFFECT),
    )(*ins, ssem, rsem, after)[ns]


def _gather_all(name, v, after):
    def body(v_ref, _, o_ref, ssem, rsem, lsem):
        x, y, c = lax.axis_index("x"), lax.axis_index("y"), lax.axis_index("c")
        me = 4 * x + 2 * y + c
        loc = pltpu.make_async_copy(v_ref, o_ref.at[me], lsem)
        loc.start()
        copies = []
        for k in range(1, 8):
            fx, fy, fc = (k >> 2) & 1, (k >> 1) & 1, k & 1
            dev = (x ^ fx, y ^ fy, c ^ fc)
            cp = pltpu.make_async_remote_copy(src_ref=v_ref, dst_ref=o_ref.at[me], send_sem=ssem.at[k - 1],
                                              recv_sem=rsem.at[k - 1], device_id=dev, device_id_type=MESH)
            cp.start()
            copies.append(cp)
        for cp in copies:
            cp.wait()
        loc.wait()

    return pl.pallas_call(
        body, in_specs=_any_specs(2), out_specs=pl.BlockSpec(memory_space=pl.ANY), out_shape=S((8,) + v.shape, v.dtype),
        scratch_shapes=[pltpu.SemaphoreType.DMA((7,)), pltpu.SemaphoreType.DMA((7,)), pltpu.SemaphoreType.DMA(())],
        compiler_params=pltpu.CompilerParams(has_side_effects=True), name=name)(v, after)


def _row_tile(R, row_bytes, budget=4 << 20):
    if R * row_bytes <= budget or R % 16:
        return R
    t = max(16, budget // row_bytes // 16 * 16)
    while R % t:
        t -= 16
    return t


def _sum_slots(name, r, n):
    _, R, C = r.shape
    tr = _row_tile(R, C * (n * r.dtype.itemsize + 4))

    def body(r_ref, o_ref):
        acc = r_ref[0].astype(F32)
        for j in range(1, n):
            acc = acc + r_ref[j].astype(F32)
        o_ref[...] = acc

    return pl.pallas_call(
        body, grid=(R // tr,), in_specs=[pl.BlockSpec((n, tr, C), lambda i: (0, i, 0))],
        out_specs=pl.BlockSpec((tr, C), lambda i: (i, 0)), out_shape=S((R, C), F32),
        compiler_params=_params("parallel"), name=name)(r)


def _adamw(name, w, m, v, layer, g1, g2=None, into=None):
    nl, R, C = w.shape
    tr = _row_tile(R, C * 4 * 9)
    two, has_into = g2 is not None, into is not None

    def body(w_ref, m_ref, v_ref, g1_ref, *rest):
        rest = list(rest)
        g = g1_ref[...]
        if two:
            g = g + rest.pop(0)[...]
        g_ref, d_ref, nm_ref, nv_ref = rest[-4:]
        mn = B1 * m_ref[...] + (1.0 - B1) * g
        vn = B2 * v_ref[...] + (1.0 - B2) * jnp.square(g)
        m_hat = mn / (1.0 - B1 ** STEP)
        v_hat = vn / (1.0 - B2 ** STEP)
        g_ref[...] = g
        d_ref[...] = -LR * (m_hat / (jnp.sqrt(v_hat) + AEPS) + WD * w_ref[...])
        nm_ref[...] = mn
        nv_ref[...] = vn

    blk = pl.BlockSpec((tr, C), lambda i: (i, 0))
    lay = pl.BlockSpec((None, tr, C), lambda i: (layer, i, 0))
    args = [w, m, v, g1] + ([g2] if two else []) + (list(into) if has_into else [])
    in_specs = [lay] * 3 + [blk] * (2 if two else 1) + (_any_specs(4) if has_into else [])
    aliases = {len(args) - 4 + t: t for t in range(4)} if has_into else {}
    return pl.pallas_call(
        body, grid=(R // tr,), in_specs=in_specs, out_specs=[lay] * 4, out_shape=[S((nl, R, C), F32)] * 4,
        input_output_aliases=aliases, compiler_params=_params("parallel"), name=name)(*args)


_USE_ORDER = [("w_ffn1_gu", 0), ("w_ffn1_down", 0), ("conv_w", 0), ("w_in_even", 0), ("w_out_even", 0), ("w_xq", 0),
              ("w_xkv", 0), ("w_xo", 0), ("w_ffn2_gu", 0), ("w_ffn2_down", 0), ("w_ffn1_gu", 1), ("w_ffn1_down", 1),
              ("w_qkv", 0), ("b_qkv", 0), ("w_o_odd", 0), ("w_xq", 1), ("w_xkv", 1), ("w_xo", 1), ("w_ffn2_gu", 1),
              ("w_ffn2_down", 1)]
_SMALL = ["norm_ffn1", "norm_mix", "gm_ln_g", "gm_ln_b", "gm_ws", "gm_bs", "conv_w", "conv_b", "dt_bias", "a_log",
          "d_skip", "ssd_norm", "b_qkv", "sinks", "norm_xq", "norm_mem", "norm_ffn2", "final_norm"]
_WEIGHTS = ["norm_ffn1", "w_ffn1_gu", "w_ffn1_down", "norm_mix", "w_in_even", "gm_ln_g", "gm_ln_b", "gm_ws", "gm_bs",
            "conv_w", "conv_b", "dt_bias", "a_log", "d_skip", "ssd_norm", "w_out_even", "w_qkv", "b_qkv", "sinks",
            "w_o_odd", "norm_xq", "norm_mem", "w_xq", "w_xkv", "w_xo", "norm_ffn2", "w_ffn2_gu", "w_ffn2_down",
            "final_norm"]


def _pack(arrs):
    rows = []
    for a in arrs:
        f = a.reshape(-1).astype(F32)
        pad = (-f.shape[0]) % LANE
        rows.append(jnp.pad(f, (0, pad)).reshape(-1, LANE))
    out = jnp.concatenate(rows, 0)
    pad = (-out.shape[0]) % 8
    return jnp.pad(out, ((0, pad), (0, 0)))


def _unpack(packed, shapes):
    outs, r = [], 0
    for shp in shapes:
        n = math.prod(shp)
        nr = -(-n // LANE)
        outs.append(packed[r:r + nr].reshape(-1)[:n].reshape(shp))
        r += nr
    return outs


def kernel(x, mem, positions, norm_ffn1, w_ffn1_gu, w_ffn1_down, norm_mix, w_in_even, gm_ln_g, gm_ln_b, gm_ws, gm_bs, conv_w, conv_b, dt_bias, a_log, d_skip, ssd_norm, w_out_even, w_qkv, b_qkv, sinks, w_o_odd, norm_xq, norm_mem, w_xq, w_xkv, w_xo, norm_ffn2, w_ffn2_gu, w_ffn2_down, final_norm, loss_target, m_norm_ffn1, m_w_ffn1_gu, m_w_ffn1_down, m_norm_mix, m_w_in_even, m_gm_ln_g, m_gm_ln_b, m_gm_ws, m_gm_bs, m_conv_w, m_conv_b, m_dt_bias, m_a_log, m_d_skip, m_ssd_norm, m_w_out_even, m_w_qkv, m_b_qkv, m_sinks, m_w_o_odd, m_norm_xq, m_norm_mem, m_w_xq, m_w_xkv, m_w_xo, m_norm_ffn2, m_w_ffn2_gu, m_w_ffn2_down, m_final_norm, v_norm_ffn1, v_w_ffn1_gu, v_w_ffn1_down, v_norm_mix, v_w_in_even, v_gm_ln_g, v_gm_ln_b, v_gm_ws, v_gm_bs, v_conv_w, v_conv_b, v_dt_bias, v_a_log, v_d_skip, v_ssd_norm, v_w_out_even, v_w_qkv, v_b_qkv, v_sinks, v_w_o_odd, v_norm_xq, v_norm_mem, v_w_xq, v_w_xkv, v_w_xo, v_norm_ffn2, v_w_ffn2_gu, v_w_ffn2_down, v_final_norm):
    a = dict(locals())
    w = {k: a[k] for k in _WEIGHTS}
    mom = {k: a["m_" + k] for k in _WEIGHTS}
    var = {k: a["v_" + k] for k in _WEIGHTS}
    chip = 2 * lax.axis_index("x") + lax.axis_index("y")

    shards = [w[k][i:i + 1] if k in ("conv_w", "b_qkv") else w[k][i:i + 1].astype(BF16) for k, i in _USE_ORDER]
    half = shards[0].reshape((2, shards[0].shape[1] // 2) + shards[0].shape[2:])
    (first,), _ = _exchange_start("gather_start_first", None, [_own_slot(half, chip)], "half")
    rest, _ = _exchange_start("gather_start_rest", None, [_own_slot(s, chip) for s in shards[1:]])
    pending = dict(zip(_USE_ORDER[1:], rest))

    def getw(name, layer, after):
        if (name, layer) == _USE_ORDER[0]:
            zone = _exchange_wait("gather_wait_first_half", first, after, "half")
            (passed,), _ = _exchange_start("gather_pass_on_first", None, [zone], "pass_on")
            return _exchange_wait("gather_wait_first_passed", passed, after, "pass_on").reshape((4,) + shards[0].shape)
        got = _exchange_wait(f"gather_wait_{name}_{layer}", pending.pop((name, layer)), after)
        if name == "w_in_even":
            w_in = jnp.transpose(got[:, 0], (1, 0, 2)).reshape(D, EVEN_IN)
            return w_in[:, :EVEN_MAIN], jnp.pad(w_in[:, EVEN_MAIN:], ((0, 0), (0, LANE - (EVEN_IN - EVEN_MAIN))))
        if name == "conv_w":
            return jnp.transpose(got[:, 0], (1, 0, 2)).reshape(4, CONV_C)
        if name == "b_qkv":
            return got.reshape(1, ODD_IN)
        return got

    sent = []

    def putg(grads):
        names, arrs = [], []
        for (name, layer), g in grads.items():
            if name == "w_in_even":
                dw_in = jnp.concatenate([g[0], g[1][:, :EVEN_IN - EVEN_MAIN]], 1)
                g = jnp.transpose(dw_in.reshape(D, 4, EVEN_IN // 4), (1, 0, 2)).reshape(4, 1, D, EVEN_IN // 4)
            names.append((name, layer))
            arrs.append(g)
        own = [_own_slot(lax.dynamic_index_in_dim(g, chip, 0, keepdims=False), chip) for g in arrs]
        its, tok = _exchange_start(f"scatter_start_{names[0][0]}_{names[0][1]}", arrs, own)
        sent.append(list(zip(names, its)))
        return tok

    P = {k: w[k] for k in _SMALL}
    loss, grad_x, sm = _local_step(x[0], mem[0], positions[0], loss_target[0], getw, P, putg)
    loss = lax.psum(loss, ("x", "y", "c"))

    out = {}

    def update(groups, after):
        flying = []
        for grp in groups:
            part = []
            for (name, layer), it in grp:
                r = _exchange_wait(f"scatter_wait_{name}_{layer}", it, after)
                part.append(_sum_slots(f"sum_{name}_{layer}", r.reshape(4, -1, r.shape[-1]), 4))
            name0, layer0 = grp[0][0]
            its, _ = _exchange_start(f"swap_start_{name0}_{layer0}", part, [lax.empty(p.shape, p.dtype) for p in part], "sibling")
            flying += [(nm, p, it) for (nm, _), p, it in zip(grp, part, its)]
        for (name, layer), p1, it in flying:
            p2 = _exchange_wait(f"swap_wait_{name}_{layer}", it, after, "sibling")
            out[name] = _adamw(f"adamw_{name}_{layer}", w[name], mom[name], var[name], layer, p1, p2, out.get(name))
            after = out[name][0]

    update(sent[:-1], grad_x)
    done_a = out["w_out_even"][0]
    update(sent[-1:], done_a)

    full_shapes = {k: w[k].shape for k in _SMALL}
    full_shapes["conv_w"], full_shapes["b_qkv"] = (1, 4, CONV_C), (1, ODD_IN)
    packed = _pack([sm[k] for k in _SMALL])
    total = _sum_slots("sum_small", _gather_all("gather_small", packed, done_a), 8)
    gs = dict(zip(_SMALL, _unpack(total, [full_shapes[k] for k in _SMALL])))
    gs["conv_w"] = lax.dynamic_slice_in_dim(gs["conv_w"], chip * (CONV_C // 4), CONV_C // 4, 2)
    gs["b_qkv"] = lax.dynamic_slice_in_dim(gs["b_qkv"], chip * (ODD_IN // 4), ODD_IN // 4, 1)
    res = _adamw("adamw_small", _pack([w[k] for k in _SMALL])[None], _pack([mom[k] for k in _SMALL])[None],
                 _pack([var[k] for k in _SMALL])[None], 0, _pack([gs[k] for k in _SMALL]))
    shapes = [w[k].shape for k in _SMALL]
    for k, g, d, nm, nv in zip(_SMALL, *[_unpack(r[0], shapes) for r in res]):
        out[k] = [g, d, nm, nv]

    return (loss, grad_x[None], *[out[k][0] for k in _WEIGHTS], *[out[k][1] for k in _WEIGHTS],
            *[out[k][2] for k in _WEIGHTS], *[out[k][3] for k in _WEIGHTS])
```

```python
import functools
import math

import jax
import jax.numpy as jnp
from jax import lax
from jax.experimental import pallas as pl
from jax.experimental.pallas import tpu as pltpu

F32, BF16 = jnp.float32, jnp.bfloat16
S = jax.ShapeDtypeStruct
MESH = pl.DeviceIdType.MESH

D = 2048
DFF = 5632
EPS = 1e-5
CH = 128
GM_G, GM_GD = 4, 512
NH, HD, NG, HPG, NS = 32, 64, 4, 8, 128
CONV_C = 3072
EVEN_MAIN, EVEN_IN = 9216, 9248
AH, AKV, AREP, AHD = 32, 4, 8, 64
ODD_IN = 2560
XH, XHD, XW = 4, 128, 512
ATT_SCALE = AHD ** -0.5
X_SCALE = XHD ** -0.5
ROPE_THETA = 500000.0
ROT = 16
LR, B1, B2, AEPS, WD, STEP = 0.001, 0.9, 0.999, 1e-08, 0.01, 10
LANE = 128
VMEM_LIMIT_V7X = 56 * 1024 * 1024


def _params(*sem):
    return pltpu.CompilerParams(dimension_semantics=sem, vmem_limit_bytes=VMEM_LIMIT_V7X)


def _tile(dim, target):
    if dim <= target:
        return dim
    t = (target // LANE) * LANE
    while t > LANE and dim % t:
        t -= LANE
    assert dim % t == 0, (dim, target)
    return t


class Op:
    def __init__(self, arr, kind=None, layer=0):
        self.arr, self.kind, self.layer = arr, kind, layer
        if kind is None:
            self.R, self.C = arr.shape
        else:
            L = arr.shape[0]
            self.R = arr.shape[2] * (L if kind == "r" else 1)
            self.C = arr.shape[3] * (L if kind == "c" else 1)

    def unit(self, axis):
        if self.kind == "r" and axis == 0:
            return self.arr.shape[2]
        if self.kind == "c" and axis == 1:
            return self.arr.shape[3]
        return (self.R, self.C)[axis]

    def spec(self, tr, tc, pick):
        if self.kind is None:
            return pl.BlockSpec((tr, tc), lambda i, j, k: pick(i, j, k))
        l = self.layer
        if self.kind == "c":
            per = self.arr.shape[3] // tc
            return pl.BlockSpec((None, None, tr, tc),
                                lambda i, j, k: (pick(i, j, k)[1] // per, l, pick(i, j, k)[0], pick(i, j, k)[1] % per))
        per = self.arr.shape[2] // tr
        return pl.BlockSpec((None, None, tr, tc),
                            lambda i, j, k: (pick(i, j, k)[0] // per, l, pick(i, j, k)[0] % per, pick(i, j, k)[1]))


_DIMS = {"nn": (((1,), (0,)), ((), ())), "nt": (((1,), (1,)), ((), ())), "tn": (((0,), (0,)), ((), ()))}
_PICK_A = {"nn": lambda i, j, k: (i, k), "nt": lambda i, j, k: (i, k), "tn": lambda i, j, k: (k, i)}
_PICK_B = {"nn": lambda i, j, k: (k, j), "nt": lambda i, j, k: (j, k), "tn": lambda i, j, k: (k, j)}


def _mm(name, mode, a, b, out_dtype, *, out=None, res=None, bias=None, scale=1.0, norm_gain=None,
        tm_t=1024, tn_t=1024, tk_t=2048):
    if not isinstance(a, Op):
        a = Op(a)
    if not isinstance(b, Op):
        b = Op(b)
    if mode == "nn":
        M, K, N = a.R, a.C, b.C
        assert b.R == K
        um, uk, un = a.unit(0), math.gcd(a.unit(1), b.unit(0)), b.unit(1)
    elif mode == "nt":
        M, K, N = a.R, a.C, b.R
        assert b.C == K
        um, uk, un = a.unit(0), math.gcd(a.unit(1), b.unit(1)), b.unit(0)
    else:
        K, M, N = a.R, a.C, b.C
        assert b.R == K
        um, uk, un = a.unit(1), math.gcd(a.unit(0), b.unit(0)), b.unit(1)
    if out is not None:
        okind, oL, olayers, olayer = out
        if okind == "c":
            un = math.gcd(un, N // oL)
        else:
            um = math.gcd(um, M // oL)
    tm, tn, tk = _tile(um, tm_t), _tile(un, tn_t), _tile(uk, tk_t)
    gi, gj, gk = M // tm, N // tn, K // tk
    a_blk = (tm, tk) if mode != "tn" else (tk, tm)
    b_blk = {"nn": (tk, tn), "nt": (tn, tk), "tn": (tk, tn)}[mode]
    in_specs = [a.spec(*a_blk, _PICK_A[mode]), b.spec(*b_blk, _PICK_B[mode])]
    args = [a.arr, b.arr]
    if res is not None:
        in_specs.append(pl.BlockSpec((tm, tn), lambda i, j, k: (i, j)))
        args.append(res)
    if bias is not None:
        in_specs.append(pl.BlockSpec((1, tn), lambda i, j, k: (0, j)))
        args.append(bias)
    if out is None:
        out_shape = S((M, N), out_dtype)
        out_spec = pl.BlockSpec((tm, tn), lambda i, j, k: (i, j))
    else:
        shp = (oL, olayers, M, N // oL) if okind == "c" else (oL, olayers, M // oL, N)
        out_shape = S(shp, out_dtype)
        out_spec = Op(out_shape, okind, olayer).spec(tm, tn, lambda i, j, k: (i, j))
    has_res, has_bias, has_norm = res is not None, bias is not None, norm_gain is not None
    if has_norm:
        assert out is None and tn == N
        in_specs.append(pl.BlockSpec((1, N), lambda i, j, k: (0, 0)))
        args.append(norm_gain)
        out_shape = [out_shape, S((M, N), BF16)]
        out_spec = [out_spec, pl.BlockSpec((tm, tn), lambda i, j, k: (i, j))]
    dims = _DIMS[mode]

    def body(a_ref, b_ref, *rest):
        rest = list(rest)
        res_ref = rest.pop(0) if has_res else None
        bias_ref = rest.pop(0) if has_bias else None
        gain_ref = rest.pop(0) if has_norm else None
        o_ref = rest.pop(0)
        h_ref = rest.pop(0) if has_norm else None
        part = lax.dot_general(a_ref[...].astype(BF16), b_ref[...].astype(BF16), dims, preferred_element_type=F32)

        def finish(r):
            if scale != 1.0:
                r = r * scale
            if has_bias:
                r = r + bias_ref[...]
            if has_res:
                r = r + res_ref[...]
            o_ref[...] = r.astype(o_ref.dtype)
            if has_norm:
                h_ref[...] = (r * lax.rsqrt(jnp.mean(r * r, -1, keepdims=True) + EPS) * gain_ref[...]).astype(BF16)

        if gk == 1:
            finish(part)
            return
        acc, = rest
        k = pl.program_id(2)

        @pl.when(k == 0)
        def _():
            acc[...] = part

        @pl.when((k > 0) & (k < gk - 1))
        def _():
            acc[...] += part

        @pl.when(k == gk - 1)
        def _():
            finish(acc[...] + part)

    return pl.pallas_call(
        body, grid=(gi, gj, gk), in_specs=in_specs, out_specs=out_spec, out_shape=out_shape,
        scratch_shapes=[pltpu.VMEM((tm, tn), F32)] if gk > 1 else [],
        compiler_params=_params("parallel", "parallel", "arbitrary"), name=name)(*args)


def _rms_fwd(name, x, gain):
    T = x.shape[0]
    tt = _tile(T, 512)

    def body(x_ref, g_ref, o_ref):
        xv = x_ref[...]
        r = lax.rsqrt(jnp.mean(xv * xv, -1, keepdims=True) + EPS)
        o_ref[...] = (xv * r * g_ref[...]).astype(BF16)

    return pl.pallas_call(
        body, grid=(T // tt,),
        in_specs=[pl.BlockSpec((tt, D), lambda i: (i, 0)), pl.BlockSpec((1, D), lambda i: (0, 0))],
        out_specs=pl.BlockSpec((tt, D), lambda i: (i, 0)), out_shape=S((T, D), BF16),
        compiler_params=_params("parallel"), name=name)(x, gain)


def _rms_bwd(name, x, gain, dh, dx_in=None):
    T = x.shape[0]
    tt = _tile(T, 512)
    has_in = dx_in is not None

    def body(x_ref, g_ref, dh_ref, *rest):
        rest = list(rest)
        dxin_ref = rest.pop(0) if has_in else None
        dx_ref, dxb_ref, dg_ref = rest
        xv = x_ref[...]
        r = lax.rsqrt(jnp.mean(xv * xv, -1, keepdims=True) + EPS)
        xh = xv * r
        dy = dh_ref[...].astype(F32)
        dxh = dy * g_ref[...]
        dx = r * (dxh - xh * jnp.mean(dxh * xh, -1, keepdims=True))
        if has_in:
            dx = dx + dxin_ref[...]
        dx_ref[...] = dx
        dxb_ref[...] = dx.astype(BF16)
        part = jnp.sum(dy * xh, 0, keepdims=True)

        @pl.when(pl.program_id(0) == 0)
        def _():
            dg_ref[...] = part

        @pl.when(pl.program_id(0) > 0)
        def _():
            dg_ref[...] += part

    row = pl.BlockSpec((tt, D), lambda i: (i, 0))
    vec = pl.BlockSpec((1, D), lambda i: (0, 0))
    args = [x, gain, dh] + ([dx_in] if has_in else [])
    return pl.pallas_call(
        body, grid=(T // tt,), in_specs=[row, vec, row] + ([row] if has_in else []),
        out_specs=[row, row, vec], out_shape=[S((T, D), F32), S((T, D), BF16), S((1, D), F32)],
        compiler_params=_params("arbitrary"), name=name)(*args)


def _sigmoid(x):
    return 0.5 * jnp.tanh(0.5 * x) + 0.5


def _ffn_up(name, h, w4, layer, tm_t=512, tn_t=1408):
    T = h.shape[0]
    n_sh = w4.shape[3]
    tm, tn = _tile(T, tm_t), _tile(n_sh, tn_t)
    per = n_sh // tn

    def body(h_ref, wg_ref, wu_ref, g_ref, u_ref, a_ref):
        hv = h_ref[...]
        g = jnp.dot(hv, wg_ref[...], preferred_element_type=F32)
        u = jnp.dot(hv, wu_ref[...], preferred_element_type=F32)
        g_ref[...] = g.astype(BF16)
        u_ref[...] = u.astype(BF16)
        a_ref[...] = (g * _sigmoid(g) * u).astype(BF16)

    o = pl.BlockSpec((tm, tn), lambda j, i: (i, j))
    return pl.pallas_call(
        body, grid=(DFF // tn, T // tm),
        in_specs=[pl.BlockSpec((tm, D), lambda j, i: (i, 0)),
                  pl.BlockSpec((None, None, D, tn), lambda j, i: (j // per, layer, 0, j % per)),
                  pl.BlockSpec((None, None, D, tn), lambda j, i: (2 + j // per, layer, 0, j % per))],
        out_specs=[o, o, o], out_shape=[S((T, DFF), BF16)] * 3,
        compiler_params=_params("parallel", "parallel"), name=name)(h, w4, w4)


def _ffn_dact(name, dxb, wd4, layer, g, u, tm_t=512):
    T = dxb.shape[0]
    r_sh = wd4.shape[2]
    tm, tn = _tile(T, tm_t), _tile(r_sh, 1408)
    per = r_sh // tn

    def body(dx_ref, w_ref, g_ref, u_ref, o_ref):
        da = 0.5 * lax.dot_general(dx_ref[...], w_ref[...], _DIMS["nt"], preferred_element_type=F32)
        gv, uv = g_ref[...].astype(F32), u_ref[...].astype(F32)
        sg = _sigmoid(gv)
        o_ref[0, 0] = (da * uv * sg * (1.0 + gv * (1.0 - sg))).astype(BF16)
        o_ref[1, 0] = (da * gv * sg).astype(BF16)

    t = pl.BlockSpec((tm, tn), lambda j, i: (i, j))
    return pl.pallas_call(
        body, grid=(DFF // tn, T // tm),
        in_specs=[pl.BlockSpec((tm, D), lambda j, i: (i, 0)),
                  pl.BlockSpec((None, None, tn, D), lambda j, i: (j // per, layer, j % per, 0)), t, t],
        out_specs=pl.BlockSpec((2, 1, tm, tn), lambda j, i: (0, 0, i, j)), out_shape=S((2, 1, T, DFF), BF16),
        compiler_params=_params("parallel", "parallel"), name=name)(dxb, wd4, g, u)


def _gelu(x):
    return 0.5 * x * (1.0 + lax.erf(x * 0.7071067811865476))


def _causal(n):
    return lax.broadcasted_iota(jnp.int32, (n, n), 0) >= lax.broadcasted_iota(jnp.int32, (n, n), 1)


def _gmlp_math(u_raw, v_raw, lng, lnb, ws, bs):
    causal = _causal(CH)
    outs = []
    for g in range(GM_G):
        u, v = _gelu(u_raw[g]), _gelu(v_raw[g])
        mu = jnp.mean(v, -1, keepdims=True)
        var = jnp.mean(jnp.square(v - mu), -1, keepdims=True)
        vn = (v - mu) * lax.rsqrt(var + EPS) * lng[g] + lnb[g]
        wm = jnp.where(causal, ws[g], 0.0)
        s = jnp.dot(wm.astype(BF16), vn.astype(BF16), preferred_element_type=F32) + bs[g]
        outs.append(u * s)
    return outs


def _gmlp_load(proj_ref, lng_ref, lnb_ref, ws_ref, bs_ref):
    sl = lambda g, off: slice(off + g * GM_GD, off + (g + 1) * GM_GD)
    u_raw = [proj_ref[:, sl(g, 0)].astype(F32) for g in range(GM_G)]
    v_raw = [proj_ref[:, sl(g, D)].astype(F32) for g in range(GM_G)]
    lng = [lng_ref[:, sl(g, 0)] for g in range(GM_G)]
    lnb = [lnb_ref[:, sl(g, 0)] for g in range(GM_G)]
    ws = [ws_ref[g] for g in range(GM_G)]
    bs = [bs_ref[g] for g in range(GM_G)]
    return u_raw, v_raw, lng, lnb, ws, bs


_GM_PAR = lambda: [pl.BlockSpec((1, D), lambda i: (0, 0)), pl.BlockSpec((1, D), lambda i: (0, 0)),
                   pl.BlockSpec((GM_G, CH, CH), lambda i: (0, 0, 0)), pl.BlockSpec((GM_G, CH, 1), lambda i: (0, 0, 0))]


def _gmlp_fwd(name, proj, lng, lnb, ws, bs):
    T = proj.shape[0]

    def body(proj_ref, lng_ref, lnb_ref, ws_ref, bs_ref, o_ref):
        outs = _gmlp_math(*_gmlp_load(proj_ref, lng_ref, lnb_ref, ws_ref, bs_ref))
        for g in range(GM_G):
            o_ref[:, g * GM_GD:(g + 1) * GM_GD] = outs[g].astype(BF16)

    return pl.pallas_call(
        body, grid=(T // CH,), in_specs=[pl.BlockSpec((CH, 2 * D), lambda i: (i, 0))] + _GM_PAR(),
        out_specs=pl.BlockSpec((CH, D), lambda i: (i, 0)), out_shape=S((T, 2 * D), BF16),
        compiler_params=_params("parallel"), name=name)(proj, lng, lnb, ws, bs)


def _acc_store(first, ref, idx, val):
    @pl.when(first)
    def _():
        ref[idx] = val

    @pl.when(jnp.logical_not(first))
    def _():
        ref[idx] += val


def _gmlp_bwd(name, proj, lng, lnb, ws, bs, dmix, dproj):
    T = proj.shape[0]

    def body(proj_ref, lng_ref, lnb_ref, ws_ref, bs_ref, dmix_ref, _, dproj_ref, dlng_ref, dlnb_ref, dws_ref, dbs_ref):
        first = pl.program_id(0) == 0
        prim = _gmlp_load(proj_ref, lng_ref, lnb_ref, ws_ref, bs_ref)
        _, vjp = jax.vjp(_gmlp_math, *prim)
        du, dv, dlng, dlnb, dws, dbs = vjp([dmix_ref[:, g * GM_GD:(g + 1) * GM_GD].astype(F32) for g in range(GM_G)])
        for g in range(GM_G):
            sl = slice(g * GM_GD, (g + 1) * GM_GD)
            dproj_ref[:, sl] = du[g].astype(BF16)
            dproj_ref[:, D + g * GM_GD:D + (g + 1) * GM_GD] = dv[g].astype(BF16)
            _acc_store(first, dlng_ref, (slice(None), sl), dlng[g])
            _acc_store(first, dlnb_ref, (slice(None), sl), dlnb[g])
            _acc_store(first, dws_ref, g, dws[g])
            _acc_store(first, dbs_ref, g, dbs[g])

    par = _GM_PAR()
    return pl.pallas_call(
        body, grid=(T // CH,),
        in_specs=[pl.BlockSpec((CH, 2 * D), lambda i: (i, 0))] + par +
                 [pl.BlockSpec((CH, D), lambda i: (i, 0)), pl.BlockSpec(memory_space=pl.ANY)],
        out_specs=[pl.BlockSpec((CH, 2 * D), lambda i: (i, 0))] + par,
        out_shape=[S(dproj.shape, BF16), S((1, D), F32), S((1, D), F32), S((GM_G, CH, CH), F32), S((GM_G, CH, 1), F32)],
        input_output_aliases={6: 0}, compiler_params=_params("arbitrary"), name=name)(proj, lng, lnb, ws, bs, dmix, dproj)


CONV_TT = 256
HALO = 8


def _shift_rows(cur, halo_after, s):
    if s == 0:
        return cur
    n = cur.shape[0]
    return pltpu.roll(jnp.concatenate([cur, halo_after], 0), s, 0)[:n]


def _conv_fwd(name, proj, w, b):
    T = proj.shape[0]
    tt = _tile(T, CONV_TT)
    hb = tt // HALO

    def body(x_ref, halo_ref, w_ref, b_ref, y_ref, xc_ref):
        i = pl.program_id(0)
        x = x_ref[...].astype(F32)
        halo = halo_ref[...].astype(F32) * (i > 0).astype(F32)
        y = b_ref[...] + w_ref[3:4, :] * x
        for s in (1, 2, 3):
            y = y + w_ref[3 - s:4 - s, :] * _shift_rows(x, halo, s)
        y_ref[...] = y.astype(BF16)
        xc_ref[...] = (y * _sigmoid(y)).astype(BF16)

    o = pl.BlockSpec((tt, CONV_C), lambda i: (i, 0))
    return pl.pallas_call(
        body, grid=(T // tt,),
        in_specs=[pl.BlockSpec((tt, CONV_C), lambda i: (i, 2)),
                  pl.BlockSpec((HALO, CONV_C), lambda i: (jnp.maximum(i * hb - 1, 0), 2)),
                  pl.BlockSpec((4, CONV_C), lambda i: (0, 0)), pl.BlockSpec((1, CONV_C), lambda i: (0, 0))],
        out_specs=[o, o], out_shape=[S((T, CONV_C), BF16)] * 2,
        compiler_params=_params("parallel"), name=name)(proj, proj, w, b)


def _conv_bwd(name, proj, ypre, dxc, w, dproj):
    T = proj.shape[0]
    tt = _tile(T, CONV_TT)
    hb = tt // HALO
    nt = T // tt

    def dsilu(y):
        sg = _sigmoid(y)
        return sg * (1.0 + y * (1.0 - sg))

    def body(x_ref, xh_ref, y_ref, yn_ref, d_ref, dn_ref, w_ref, _, dproj_ref, dw_ref, db_ref):
        i = pl.program_id(0)
        first = i == 0
        x = x_ref[...].astype(F32)
        halo = xh_ref[...].astype(F32) * (i > 0).astype(F32)
        dy = d_ref[...].astype(F32) * dsilu(y_ref[...].astype(F32))
        dyn = dn_ref[...].astype(F32) * dsilu(yn_ref[...].astype(F32)) * (i < nt - 1).astype(F32)
        ext = jnp.concatenate([dy, dyn], 0)
        dx = w_ref[3:4, :] * dy
        _acc_store(first, dw_ref, (slice(3, 4), slice(None)), jnp.sum(x * dy, 0, keepdims=True))
        for s in (1, 2, 3):
            dx = dx + w_ref[3 - s:4 - s, :] * pltpu.roll(ext, tt + HALO - s, 0)[:tt]
            _acc_store(first, dw_ref, (slice(3 - s, 4 - s), slice(None)),
                       jnp.sum(_shift_rows(x, halo, s) * dy, 0, keepdims=True))
        _acc_store(first, db_ref, (slice(None), slice(None)), jnp.sum(dy, 0, keepdims=True))
        dproj_ref[...] = dx.astype(BF16)

    cur = pl.BlockSpec((tt, CONV_C), lambda i: (i, 0))
    nxt = pl.BlockSpec((HALO, CONV_C), lambda i: (jnp.minimum((i + 1) * hb, T // HALO - 1), 0))
    return pl.pallas_call(
        body, grid=(nt,),
        in_specs=[pl.BlockSpec((tt, CONV_C), lambda i: (i, 2)),
                  pl.BlockSpec((HALO, CONV_C), lambda i: (jnp.maximum(i * hb - 1, 0), 2)),
                  cur, nxt, cur, nxt, pl.BlockSpec((4, CONV_C), lambda i: (0, 0)), pl.BlockSpec(memory_space=pl.ANY)],
        out_specs=[pl.BlockSpec((tt, CONV_C), lambda i: (i, 2)), pl.BlockSpec((4, CONV_C), lambda i: (0, 0)),
                   pl.BlockSpec((1, CONV_C), lambda i: (0, 0))],
        out_shape=[S(dproj.shape, BF16), S((4, CONV_C), F32), S((1, CONV_C), F32)],
        input_output_aliases={7: 0}, compiler_params=_params("arbitrary"), name=name)(proj, proj, ypre, ypre, dxc, dxc, w, dproj)


def _softplus(x):
    return jnp.maximum(x, 0.0) + jnp.log(1.0 + jnp.exp(-jnp.abs(x)))


def _ssd_math(x, Bm, Cm, dtr, z, prev, dtb, alog, dsk, nrm):
    hi = lax.Precision.HIGHEST
    causal = _causal(CH)
    tri = causal.astype(F32)
    lane = lax.broadcasted_iota(jnp.int32, (1, LANE), 1)
    sub = lax.broadcasted_iota(jnp.int32, (LANE, 1), 0)
    dt = _softplus(dtr + dtb)
    a = dt * (-jnp.exp(alog))
    a_cs = jnp.dot(tri, a, preferred_element_type=F32, precision=hi)
    a_csT = lax.dot_general(a, tri, (((0,), (1,)), ((), ())), preferred_element_type=F32, precision=hi)
    a_last = jnp.sum(a, 0, keepdims=True)
    gw = HPG * HD
    outs, new = [], []
    for g in range(NG):
        spread = (lax.broadcasted_iota(jnp.int32, (LANE, gw), 0)
                  == g * HPG + lax.broadcasted_iota(jnp.int32, (LANE, gw), 1) // HD).astype(F32)
        to_lanes = lambda v: jnp.dot(v, spread, preferred_element_type=F32, precision=hi)
        col_e, dt_e, last_e, dsk_e = to_lanes(a_cs), to_lanes(dt), to_lanes(a_last), to_lanes(dsk)
        last_r = lax.dot_general(spread, a_last, (((0,), (1,)), ((), ())), preferred_element_type=F32, precision=hi)
        cb = lax.dot_general(Cm[g].astype(BF16), Bm[g].astype(BF16), _DIMS["nt"], preferred_element_type=F32)
        xg = jnp.concatenate(x[g * HPG:(g + 1) * HPG], 1)
        yd = []
        for h in range(g * HPG, (g + 1) * HPG):
            ohl = (lane == h).astype(F32)
            col = jnp.sum(a_cs * ohl, 1, keepdims=True)
            row = jnp.sum(a_csT * (sub == h).astype(F32), 0, keepdims=True)
            dtc = jnp.sum(dt * ohl, 1, keepdims=True)
            lmat = jnp.where(causal, jnp.exp(jnp.where(causal, col - row, 0.0)), 0.0)
            yd.append(jnp.dot((cb * lmat).astype(BF16), (x[h] * dtc).astype(BF16), preferred_element_type=F32))
        y = jnp.concatenate(yd, 1)
        y = y + jnp.exp(col_e) * lax.dot_general(Cm[g].astype(BF16), prev[g].astype(BF16), _DIMS["nt"],
                                                 preferred_element_type=F32)
        st = lax.dot_general((xg * dt_e * jnp.exp(last_e - col_e)).astype(BF16), Bm[g].astype(BF16), _DIMS["tn"],
                             preferred_element_type=F32)
        new.append(prev[g] * jnp.exp(last_r) + st)
        yg = (y + xg * dsk_e) * (z[g] * _sigmoid(z[g]))
        yg = yg * lax.rsqrt(jnp.mean(yg * yg, -1, keepdims=True) + EPS)
        outs.append(yg * nrm[g])
    return outs, new


def _ssd_load(xc_ref, dtr_ref, z_ref, state_ref, dtb_ref, alog_ref, dsk_ref, nrm_ref):
    gw = HPG * HD
    x = [xc_ref[:, h * HD:(h + 1) * HD].astype(F32) for h in range(NH)]
    Bm = [xc_ref[:, D + g * NS:D + (g + 1) * NS].astype(F32) for g in range(NG)]
    Cm = [xc_ref[:, D + NG * NS + g * NS:D + NG * NS + (g + 1) * NS].astype(F32) for g in range(NG)]
    z = [z_ref[:, g * gw:(g + 1) * gw].astype(F32) for g in range(NG)]
    prev = [state_ref[g * gw:(g + 1) * gw, :] for g in range(NG)]
    nrm = [nrm_ref[:, g * gw:(g + 1) * gw] for g in range(NG)]
    return x, Bm, Cm, dtr_ref[...], z, prev, dtb_ref[...], alog_ref[...], dsk_ref[...], nrm


_SSD_PAR = lambda: [pl.BlockSpec((1, LANE), lambda c: (0, 0))] * 3 + [pl.BlockSpec((1, D), lambda c: (0, 0))]


def _ssd_fwd(name, xc, dtr, proj, dtb, alog, dsk, nrm, mix):
    T = xc.shape[0]
    nc = T // CH

    def body(xc_ref, dtr_ref, z_ref, dtb_ref, alog_ref, dsk_ref, nrm_ref, _, mix_ref, prev_ref, state):
        @pl.when(pl.program_id(0) == 0)
        def _():
            state[...] = jnp.zeros_like(state)

        prev_ref[...] = state[...]
        outs, new = _ssd_math(*_ssd_load(xc_ref, dtr_ref, z_ref, state, dtb_ref, alog_ref, dsk_ref, nrm_ref))
        for g in range(NG):
            mix_ref[:, g * 512:(g + 1) * 512] = outs[g].astype(BF16)
            state[g * 512:(g + 1) * 512, :] = new[g]

    return pl.pallas_call(
        body, grid=(nc,),
        in_specs=[pl.BlockSpec((CH, CONV_C), lambda c: (c, 0)), pl.BlockSpec((CH, LANE), lambda c: (c, 0)),
                  pl.BlockSpec((CH, D), lambda c: (c, 2))] + _SSD_PAR() + [pl.BlockSpec(memory_space=pl.ANY)],
        out_specs=[pl.BlockSpec((CH, D), lambda c: (c, 1)), pl.BlockSpec((None, NH * HD, NS), lambda c: (c, 0, 0))],
        out_shape=[S(mix.shape, BF16), S((nc, NH * HD, NS), F32)],
        scratch_shapes=[pltpu.VMEM((NH * HD, NS), F32)], input_output_aliases={7: 0},
        compiler_params=_params("arbitrary"), name=name)(xc, dtr, proj, dtb, alog, dsk, nrm, mix)


def _ssd_bwd(name, xc, dtr, proj, prevs, dtb, alog, dsk, nrm, dmix, dproj):
    T = xc.shape[0]
    nc = T // CH
    rev = lambda c: nc - 1 - c

    def body(xc_ref, dtr_ref, z_ref, prev_ref, dtb_ref, alog_ref, dsk_ref, nrm_ref, dmix_ref, _,
             dproj_ref, dxc_ref, ddtr_ref, ddtb_ref, dalog_ref, ddsk_ref, dnrm_ref, dstate):
        first = pl.program_id(0) == 0

        @pl.when(first)
        def _():
            dstate[...] = jnp.zeros_like(dstate)

        prim = _ssd_load(xc_ref, dtr_ref, z_ref, prev_ref, dtb_ref, alog_ref, dsk_ref, nrm_ref)
        _, vjp = jax.vjp(_ssd_math, *prim)
        douts = [dmix_ref[:, g * 512:(g + 1) * 512].astype(F32) for g in range(NG)]
        dnew = [dstate[g * 512:(g + 1) * 512, :] for g in range(NG)]
        dx, dB, dC, ddtr, dz, dprev, ddtb, dalog, ddsk, dnrm = vjp((douts, dnew))
        for h in range(NH):
            dxc_ref[:, h * HD:(h + 1) * HD] = dx[h].astype(BF16)
        for g in range(NG):
            dstate[g * 512:(g + 1) * 512, :] = dprev[g]
            dxc_ref[:, D + g * NS:D + (g + 1) * NS] = dB[g].astype(BF16)
            dxc_ref[:, D + NG * NS + g * NS:D + NG * NS + (g + 1) * NS] = dC[g].astype(BF16)
            dproj_ref[:, g * 512:(g + 1) * 512] = dz[g].astype(BF16)
            _acc_store(first, dnrm_ref, (slice(None), slice(g * 512, (g + 1) * 512)), dnrm[g])
        ddtr_ref[...] = ddtr
        _acc_store(first, ddtb_ref, (slice(None), slice(None)), ddtb)
        _acc_store(first, dalog_ref, (slice(None), slice(None)), dalog)
        _acc_store(first, ddsk_ref, (slice(None), slice(None)), ddsk)

    vec = pl.BlockSpec((1, LANE), lambda c: (0, 0))
    return pl.pallas_call(
        body, grid=(nc,),
        in_specs=[pl.BlockSpec((CH, CONV_C), lambda c: (rev(c), 0)), pl.BlockSpec((CH, LANE), lambda c: (rev(c), 0)),
                  pl.BlockSpec((CH, D), lambda c: (rev(c), 2)),
                  pl.BlockSpec((None, NH * HD, NS), lambda c: (rev(c), 0, 0))] + _SSD_PAR() +
                 [pl.BlockSpec((CH, D), lambda c: (rev(c), 1)), pl.BlockSpec(memory_space=pl.ANY)],
        out_specs=[pl.BlockSpec((CH, D), lambda c: (rev(c), 2)), pl.BlockSpec((CH, CONV_C), lambda c: (rev(c), 0)),
                   pl.BlockSpec((CH, LANE), lambda c: (rev(c), 0)), vec, vec, vec, pl.BlockSpec((1, D), lambda c: (0, 0))],
        out_shape=[S(dproj.shape, BF16), S((T, CONV_C), BF16), S((T, LANE), F32), S((1, LANE), F32), S((1, LANE), F32),
                   S((1, LANE), F32), S((1, D), F32)],
        scratch_shapes=[pltpu.VMEM((NH * HD, NS), F32)], input_output_aliases={9: 0},
        compiler_params=_params("arbitrary"), name=name)(xc, dtr, proj, prevs, dtb, alog, dsk, nrm, dmix, dproj)


def _rope(x, c, s, sign):
    W = x.shape[1]
    reps = W // LANE
    C, Sg = jnp.tile(c, (1, reps)), jnp.tile(s, (1, reps))
    lane = lax.broadcasted_iota(jnp.int32, x.shape, 1) % AHD
    up, dn = pltpu.roll(x, W - ROT // 2, 1), pltpu.roll(x, ROT // 2, 1)
    sw = jnp.where(lane < ROT // 2, up, jnp.where(lane < ROT, dn, 0.0))
    return x * C + sign * sw * Sg


def _rope_fwd(name, qkv, cos, sin):
    T = qkv.shape[0]
    tt = _tile(T, 256)
    KV = AKV * AHD

    def body(x_ref, c_ref, s_ref, o_ref):
        c, s = c_ref[...], s_ref[...]
        o_ref[:, :D] = _rope(x_ref[:, :D], c, s, 1.0).astype(BF16)
        o_ref[:, D:D + KV] = _rope(x_ref[:, D:D + KV], c, s, 1.0).astype(BF16)
        o_ref[:, D + KV:] = x_ref[:, D + KV:].astype(BF16)

    tab = pl.BlockSpec((tt, LANE), lambda i: (i, 0))
    return pl.pallas_call(
        body, grid=(T // tt,), in_specs=[pl.BlockSpec((tt, ODD_IN), lambda i: (i, 0)), tab, tab],
        out_specs=pl.BlockSpec((tt, ODD_IN), lambda i: (i, 0)), out_shape=S((T, ODD_IN), BF16),
        compiler_params=_params("parallel"), name=name)(qkv, cos, sin)


def _rope_bwd(name, dq, dkv_cur, dkv_prev, cos, sin):
    T = dq.shape[0]
    nb = T // CH
    KV = AKV * AHD

    def body(dq_ref, cur_ref, nxt_ref, c_ref, s_ref, o_ref, db_ref):
        n = pl.program_id(0)
        c, s = c_ref[...], s_ref[...]
        dkv = cur_ref[...] + nxt_ref[...] * (n < nb - 1).astype(F32)
        o_ref[:, :D] = _rope(dq_ref[...].astype(F32), c, s, -1.0).astype(BF16)
        o_ref[:, D:D + KV] = _rope(dkv[:, :KV], c, s, -1.0).astype(BF16)
        o_ref[:, D + KV:] = dkv[:, KV:].astype(BF16)
        _acc_store(n == 0, db_ref, (slice(None), slice(None)), jnp.sum(o_ref[...].astype(F32), 0, keepdims=True))

    tab = pl.BlockSpec((CH, LANE), lambda n: (n, 0))
    return pl.pallas_call(
        body, grid=(nb,),
        in_specs=[pl.BlockSpec((CH, D), lambda n: (n, 0)), pl.BlockSpec((CH, 2 * KV), lambda n: (n, 0)),
                  pl.BlockSpec((CH, 2 * KV), lambda n: (jnp.minimum(n + 1, nb - 1), 0)), tab, tab],
        out_specs=[pl.BlockSpec((CH, ODD_IN), lambda n: (n, 0)), pl.BlockSpec((1, ODD_IN), lambda n: (0, 0))],
        out_shape=[S((T, ODD_IN), BF16), S((1, ODD_IN), F32)],
        compiler_params=_params("arbitrary"), name=name)(dq, dkv_cur, dkv_prev, cos, sin)


def _swa_math(q, kp, kc, vp, vc, snk, mask):
    outs = []
    for k in range(AKV):
        K = jnp.concatenate([kp[k], kc[k]], 0).astype(BF16)
        V = jnp.concatenate([vp[k], vc[k]], 0).astype(BF16)
        s = lax.dot_general(q[k].astype(BF16), K, _DIMS["nt"], preferred_element_type=F32) * ATT_SCALE
        s = jnp.where(mask, s, -jnp.inf)
        m = lax.stop_gradient(jnp.maximum(jnp.max(s, -1, keepdims=True), snk[k]))
        p = jnp.exp(s - m)
        pr = p / (jnp.sum(p, -1, keepdims=True) + jnp.exp(snk[k] - m))
        outs.append(jnp.dot(pr.astype(BF16), V, preferred_element_type=F32))
    return outs


def _stack_heads(ref, k):
    return jnp.concatenate([ref[:, (k * AREP + r) * AHD:(k * AREP + r + 1) * AHD].astype(F32) for r in range(AREP)], 0)


def _swa_load(q_ref, cur_ref, prv_ref, snk_ref):
    KV = AKV * AHD
    q = [_stack_heads(q_ref, k) for k in range(AKV)]
    kc = [cur_ref[:, k * AHD:(k + 1) * AHD].astype(F32) for k in range(AKV)]
    vc = [cur_ref[:, KV + k * AHD:KV + (k + 1) * AHD].astype(F32) for k in range(AKV)]
    kp = [prv_ref[:, k * AHD:(k + 1) * AHD].astype(F32) for k in range(AKV)]
    vp = [prv_ref[:, KV + k * AHD:KV + (k + 1) * AHD].astype(F32) for k in range(AKV)]
    snk = [jnp.concatenate([jnp.broadcast_to(snk_ref[:, k * AREP + r:k * AREP + r + 1], (CH, 1)) for r in range(AREP)], 0)
           for k in range(AKV)]
    return q, kp, kc, vp, vc, snk


def _swa_mask(n):
    iq = lax.broadcasted_iota(jnp.int32, (AREP * CH, 2 * CH), 0) % CH
    js = lax.broadcasted_iota(jnp.int32, (AREP * CH, 2 * CH), 1)
    rel = iq + CH - js
    return (rel >= 0) & (rel < CH) & ((n > 0) | (js >= CH))


def _swa_specs(T):
    KV = AKV * AHD
    return [pl.BlockSpec((CH, D), lambda n: (n, 0)), pl.BlockSpec((CH, 2 * KV), lambda n: (n, D // (2 * KV))),
            pl.BlockSpec((CH, 2 * KV), lambda n: (jnp.maximum(n - 1, 0), D // (2 * KV))),
            pl.BlockSpec((1, LANE), lambda n: (0, 0))]


def _swa_fwd(name, qkvr, snk):
    T = qkvr.shape[0]

    def body(q_ref, cur_ref, prv_ref, snk_ref, o_ref):
        outs = _swa_math(*_swa_load(q_ref, cur_ref, prv_ref, snk_ref), _swa_mask(pl.program_id(0)))
        for h in range(AH):
            k, r = divmod(h, AREP)
            o_ref[:, h * AHD:(h + 1) * AHD] = outs[k][r * CH:(r + 1) * CH].astype(BF16)

    return pl.pallas_call(
        body, grid=(T // CH,), in_specs=_swa_specs(T), out_specs=pl.BlockSpec((CH, D), lambda n: (n, 0)),
        out_shape=S((T, D), BF16), compiler_params=_params("parallel"), name=name)(qkvr, qkvr, qkvr, snk)


def _swa_bwd(name, qkvr, snk, do):
    T = qkvr.shape[0]
    KV = AKV * AHD

    def body(q_ref, cur_ref, prv_ref, snk_ref, do_ref, dq_ref, dcur_ref, dprv_ref, dsnk_ref):
        n = pl.program_id(0)

        @pl.when(n == 0)
        def _():
            dsnk_ref[...] = jnp.zeros_like(dsnk_ref)

        prim = _swa_load(q_ref, cur_ref, prv_ref, snk_ref)
        mask = _swa_mask(n)
        _, vjp = jax.vjp(lambda *p: _swa_math(*p, mask), *prim)
        dq, dkp, dkc, dvp, dvc, dsnk = vjp([_stack_heads(do_ref, k) for k in range(AKV)])
        for h in range(AH):
            k, r = divmod(h, AREP)
            dq_ref[:, h * AHD:(h + 1) * AHD] = dq[k][r * CH:(r + 1) * CH].astype(BF16)
            dsnk_ref[:, h:h + 1] += jnp.sum(dsnk[k][r * CH:(r + 1) * CH], 0, keepdims=True)
        for k in range(AKV):
            dcur_ref[:, k * AHD:(k + 1) * AHD] = dkc[k]
            dcur_ref[:, KV + k * AHD:KV + (k + 1) * AHD] = dvc[k]
            dprv_ref[:, k * AHD:(k + 1) * AHD] = dkp[k]
            dprv_ref[:, KV + k * AHD:KV + (k + 1) * AHD] = dvp[k]

    kv = pl.BlockSpec((CH, 2 * KV), lambda n: (n, 0))
    return pl.pallas_call(
        body, grid=(T // CH,), in_specs=_swa_specs(T) + [pl.BlockSpec((CH, D), lambda n: (n, 0))],
        out_specs=[pl.BlockSpec((CH, D), lambda n: (n, 0)), kv, kv, pl.BlockSpec((1, LANE), lambda n: (0, 0))],
        out_shape=[S((T, D), BF16), S((T, 2 * KV), F32), S((T, 2 * KV), F32), S((1, LANE), F32)],
        compiler_params=_params("arbitrary"), name=name)(qkvr, qkvr, qkvr, snk, do)


def _xat_math(q, k, v):
    outs = []
    for h in range(XH):
        s = lax.dot_general(q[h].astype(BF16), k[h].astype(BF16), _DIMS["nt"], preferred_element_type=F32) * X_SCALE
        m = lax.stop_gradient(jnp.max(s, -1, keepdims=True))
        p = jnp.exp(s - m)
        pr = p / jnp.sum(p, -1, keepdims=True)
        outs.append(jnp.dot(pr.astype(BF16), v[h].astype(BF16), preferred_element_type=F32))
    return outs


def _xat_load(q_ref, kv_ref):
    q = [q_ref[:, h * XHD:(h + 1) * XHD].astype(F32) for h in range(XH)]
    k = [kv_ref[:, h * XHD:(h + 1) * XHD].astype(F32) for h in range(XH)]
    v = [kv_ref[:, XW + h * XHD:XW + (h + 1) * XHD].astype(F32) for h in range(XH)]
    return q, k, v


def _xat_fwd(name, q, kv):
    T, M = q.shape[0], kv.shape[0]
    tt = _tile(T, 512)

    def body(q_ref, kv_ref, o_ref):
        outs = _xat_math(*_xat_load(q_ref, kv_ref))
        for h in range(XH):
            o_ref[:, h * XHD:(h + 1) * XHD] = outs[h].astype(BF16)

    return pl.pallas_call(
        body, grid=(T // tt,),
        in_specs=[pl.BlockSpec((tt, XW), lambda i: (i, 0)), pl.BlockSpec((M, 2 * XW), lambda i: (0, 0))],
        out_specs=pl.BlockSpec((tt, XW), lambda i: (i, 0)), out_shape=S((T, XW), BF16),
        compiler_params=_params("parallel"), name=name)(q, kv)


def _xat_bwd(name, q, kv, do):
    T, M = q.shape[0], kv.shape[0]
    tt = _tile(T, 512)

    def body(q_ref, kv_ref, do_ref, dq_ref, dkv_ref):
        first = pl.program_id(0) == 0
        _, vjp = jax.vjp(_xat_math, *_xat_load(q_ref, kv_ref))
        dq, dk, dv = vjp([do_ref[:, h * XHD:(h + 1) * XHD].astype(F32) for h in range(XH)])
        for h in range(XH):
            sl = slice(h * XHD, (h + 1) * XHD)
            dq_ref[:, sl] = dq[h].astype(BF16)
            _acc_store(first, dkv_ref, (slice(None), sl), dk[h])
            _acc_store(first, dkv_ref, (slice(None), slice(XW + h * XHD, XW + (h + 1) * XHD)), dv[h])

    qs = pl.BlockSpec((tt, XW), lambda i: (i, 0))
    kvs = pl.BlockSpec((M, 2 * XW), lambda i: (0, 0))
    return pl.pallas_call(
        body, grid=(T // tt,), in_specs=[qs, kvs, qs], out_specs=[qs, kvs],
        out_shape=[S((T, XW), BF16), S((M, 2 * XW), F32)],
        compiler_params=_params("arbitrary"), name=name)(q, kv, do)


def _loss_head(name, x, gain, target):
    T = x.shape[0]
    tt = _tile(T, 512)

    def body(x_ref, g_ref, t_ref, l_ref, dx_ref, dxb_ref, dg_ref):
        first = pl.program_id(0) == 0
        xv, g = x_ref[...], g_ref[...]
        r = lax.rsqrt(jnp.mean(xv * xv, -1, keepdims=True) + EPS)
        xh = xv * r
        e = xh * g - t_ref[...]
        part = 0.5 * jnp.sum(jnp.mean(e * e, -1, keepdims=True), (0, 1), keepdims=True)
        _acc_store(first, l_ref, (slice(None), slice(None)), jnp.broadcast_to(part, (1, LANE)))
        dy = e * (1.0 / D)
        dxh = dy * g
        dx = r * (dxh - xh * jnp.mean(dxh * xh, -1, keepdims=True))
        dx_ref[...] = dx
        dxb_ref[...] = dx.astype(BF16)
        _acc_store(first, dg_ref, (slice(None), slice(None)), jnp.sum(dy * xh, 0, keepdims=True))

    row = pl.BlockSpec((tt, D), lambda i: (i, 0))
    vec = pl.BlockSpec((1, D), lambda i: (0, 0))
    return pl.pallas_call(
        body, grid=(T // tt,), in_specs=[row, vec, row],
        out_specs=[pl.BlockSpec((1, LANE), lambda i: (0, 0)), row, row, vec],
        out_shape=[S((1, LANE), F32), S((T, D), F32), S((T, D), BF16), S((1, D), F32)],
        compiler_params=_params("arbitrary"), name=name)(x, gain, target)


def _out_proj(name, a, b, x, next_gain, scale=1.0, tk_t=2048, plain=(1024, 1024)):
    if next_gain is None:
        return _mm(name, "nn", a, b, F32, res=x, scale=scale, tm_t=plain[0], tn_t=plain[1], tk_t=tk_t), None
    return _mm(name, "nn", a, b, F32, res=x, scale=scale, norm_gain=next_gain, tm_t=512, tn_t=D, tk_t=min(tk_t, 2048))


def _ffn_fwd(tag, x, h, gain, wgu4, get_wd, next_gain):
    g, u, a = _ffn_up(f"{tag}_up", h, wgu4, 0)
    wd = get_wd(a).reshape(1, 1, DFF, D)
    x_new, _ = _out_proj(f"{tag}_down", a, Op(wd, "r"), x, None, 0.5, 2816)
    h_next = None if next_gain is None else _rms_fwd(f"{tag}_nextnorm", x_new, next_gain)
    return x_new, h_next, (x, gain, h, g, u, a)


def _ffn_bwd(tag, saved, dx, dxb, wgu4, wd4, put):
    x, gain, h, g, u, a = saved
    dgu = _ffn_dact(f"{tag}_dact", dxb, wd4, 0, g, u, 1024)
    dwd = _mm(f"{tag}_dwd", "tn", a, dxb, BF16, out=("r", 4, 1, 0), scale=0.5, tm_t=1408, tn_t=1024, tk_t=2048)
    dwgu = _mm(f"{tag}_dwgu", "tn", h, Op(dgu, "c"), BF16, out=("c", 4, 1, 0), tm_t=1024, tn_t=256, tk_t=8192)
    tok = put(dwgu, dwd)
    dh = _mm(f"{tag}_dh", "nt", Op(dgu, "c"), Op(wgu4, "c"), BF16, bias=jnp.zeros((1, D), F32) + tok, tk_t=2816)
    dx, dxb, dgain = _rms_bwd(f"{tag}_dnorm", x, gain, dh, dx)
    return dx, dxb, dgain


def _xattn_fwd(tag, x, hq, mem, gq, gm, wxq4, wxkv4, wxo4, next_gain):
    mn = _rms_fwd(f"{tag}_normm", mem, gm)
    q = _mm(f"{tag}_q", "nn", hq, Op(wxq4, "r"), BF16)
    kv = _mm(f"{tag}_kv", "nn", mn, Op(wxkv4, "r"), BF16)
    o = _xat_fwd(f"{tag}_att", q, kv)
    wxo = jnp.transpose(wxo4[:, 0], (1, 0, 2)).reshape(XW, D)
    x_new, h_next = _out_proj(f"{tag}_o", o, wxo, x, next_gain)
    return x_new, h_next, (x, mem, gq, gm, hq, mn, q, kv, o)


def _xattn_bwd(tag, saved, dx, dxb, wxq4, wxkv4, wxo4, put):
    x, mem, gq, gm, hq, mn, q, kv, o = saved
    dwxo = _mm(f"{tag}_dwo", "tn", o, dxb, BF16, out=("c", 4, 1, 0))
    do = _mm(f"{tag}_do", "nt", dxb, Op(wxo4, "c"), BF16)
    dq, dkv = _xat_bwd(f"{tag}_datt", q, kv, do)
    dwxq = _mm(f"{tag}_dwq", "tn", hq, dq, BF16, out=("r", 4, 1, 0))
    dwxkv = _mm(f"{tag}_dwkv", "tn", mn, dkv, BF16, out=("r", 4, 1, 0))
    tok = put(dwxq, dwxkv, dwxo)
    dhq = _mm(f"{tag}_dhq", "nt", dq, Op(wxq4, "r"), BF16, bias=jnp.zeros((1, D), F32) + tok)
    dmn = _mm(f"{tag}_dmn", "nt", dkv, Op(wxkv4, "r"), BF16)
    _, _, dgm = _rms_bwd(f"{tag}_dnormm", mem, gm, dmn)
    dx, dxb, dgq = _rms_bwd(f"{tag}_dnormq", x, gq, dhq, dx)
    return dx, dxb, dgq, dgm


def _even_fwd(tag, x, h, gain, w_main, w_dt, p, wout4, next_gain):
    proj = _mm(f"{tag}_in", "nn", h, w_main, BF16)
    dtr = _mm(f"{tag}_indt", "nn", h, w_dt, F32)
    mix = _gmlp_fwd(f"{tag}_gmlp", proj, p["lng"], p["lnb"], p["ws"], p["bs"])
    ypre, xc = _conv_fwd(f"{tag}_conv", proj, p["cw"], p["cb"])
    mix, prevs = _ssd_fwd(f"{tag}_ssd", xc, dtr, proj, p["dtb"], p["alog"], p["dsk"], p["nrm"], mix)
    x_new, h_next = _out_proj(f"{tag}_out", mix, Op(wout4.reshape(1, 1, 2 * D, D), "r"), x, next_gain)
    return x_new, h_next, (x, gain, h, proj, dtr, mix, ypre, xc, prevs)


def _even_bwd(tag, saved, dx, dxb, w_main, w_dt, p, wout4, put):
    x, gain, h, proj, dtr, mix, ypre, xc, prevs = saved
    T = x.shape[0]
    dwout = _mm(f"{tag}_dwout", "tn", mix, dxb, BF16, out=("r", 4, 1, 0))
    dmix = _mm(f"{tag}_dmix", "nt", dxb, Op(wout4, "r", 0), BF16)
    dproj = lax.empty((T, EVEN_MAIN), BF16)
    dproj, dlng, dlnb, dws, dbs = _gmlp_bwd(f"{tag}_dgmlp", proj, p["lng"], p["lnb"], p["ws"], p["bs"], dmix, dproj)
    dproj, dxc, ddtr, ddtb, dalog, ddsk, dnrm = _ssd_bwd(
        f"{tag}_dssd", xc, dtr, proj, prevs, p["dtb"], p["alog"], p["dsk"], p["nrm"], dmix, dproj)
    dproj, dcw, dcb = _conv_bwd(f"{tag}_dconv", proj, ypre, dxc, p["cw"], dproj)
    dw_main = _mm(f"{tag}_dwin", "tn", h, dproj, BF16, tm_t=1024, tn_t=256, tk_t=8192)
    dw_dt = _mm(f"{tag}_dwdt", "tn", h, ddtr, BF16)
    tok = put(dw_main, dw_dt, dwout)
    dh = _mm(f"{tag}_dh1", "nt", ddtr, w_dt + tok.astype(BF16), F32)
    dh = _mm(f"{tag}_dh2", "nt", dproj, w_main, BF16, res=dh)
    dx, dxb, dgain = _rms_bwd(f"{tag}_dnorm", x, gain, dh, dx)
    small = dict(lng=dlng, lnb=dlnb, ws=dws, bs=dbs, cw=dcw, cb=dcb, dtb=ddtb, alog=dalog, dsk=ddsk, nrm=dnrm)
    return dx, dxb, dgain, small


def _odd_fwd(tag, x, h, gain, wqkv4, bqkv, snk, wo4, cos, sin, next_gain):
    qkv = _mm(f"{tag}_qkv", "nn", h, Op(wqkv4, "c", 0), F32, bias=bqkv, tn_t=640)
    qkvr = _rope_fwd(f"{tag}_rope", qkv, cos, sin)
    o = _swa_fwd(f"{tag}_swa", qkvr, snk)
    x_new, h_next = _out_proj(f"{tag}_o", o, Op(wo4.reshape(1, 1, D, D), "r"), x, next_gain)
    return x_new, h_next, (x, gain, h, qkvr, o)


def _odd_bwd(tag, saved, dx, dxb, wqkv4, snk, wo4, cos, sin, put):
    x, gain, h, qkvr, o = saved
    dwo = _mm(f"{tag}_dwo", "tn", o, dxb, BF16, out=("r", 4, 1, 0))
    do = _mm(f"{tag}_do", "nt", dxb, Op(wo4, "r", 0), BF16)
    dq, dcur, dprv, dsnk = _swa_bwd(f"{tag}_dswa", qkvr, snk, do)
    dqkv, dbias = _rope_bwd(f"{tag}_drope", dq, dcur, dprv, cos, sin)
    dwqkv = _mm(f"{tag}_dwqkv", "tn", h, dqkv, BF16, out=("c", 4, 1, 0), tn_t=640)
    tok = put(dwqkv, dwo)
    dh = _mm(f"{tag}_dh", "nt", dqkv, Op(wqkv4, "c", 0), BF16, bias=jnp.zeros((1, D), F32) + tok, tk_t=640)
    dx, dxb, dgain = _rms_bwd(f"{tag}_dnorm", x, gain, dh, dx)
    return dx, dxb, dgain, dbias, dsnk


def _row(v):
    return v.reshape(1, -1).astype(F32)


def _pad_lanes(v, n=LANE):
    v = v.reshape(1, -1).astype(F32)
    return jnp.pad(v, ((0, 0), (0, n - v.shape[1])))


def _local_step(x, mem, positions, target, getw, P, putg):
    inv_freq = ROPE_THETA ** (-jnp.arange(0, ROT, 2, dtype=F32) / ROT)
    ang = positions.astype(F32)[:, None] * inv_freq
    cos8, sin8 = jnp.cos(ang), jnp.sin(ang)
    ones, zeros = jnp.ones((x.shape[0], AHD - ROT), F32), jnp.zeros((x.shape[0], AHD - ROT), F32)
    cos = jnp.tile(jnp.concatenate([cos8, cos8, ones], 1), (1, 2))
    sin = jnp.tile(jnp.concatenate([-sin8, sin8, zeros], 1), (1, 2))

    snk = _pad_lanes(P["sinks"])
    W = {}

    def w(name, layer, after):
        if (name, layer) not in W:
            W[name, layer] = getw(name, layer, after)
        return W[name, layer]

    saved = []
    h = _rms_fwd("l0_ffn1_norm", x, _row(P["norm_ffn1"][0]))
    for i in range(2):
        x, h, s1 = _ffn_fwd(f"l{i}_ffn1", x, h, _row(P["norm_ffn1"][i]), w("w_ffn1_gu", i, x),
                            functools.partial(w, "w_ffn1_down", i), _row(P["norm_mix"][i]))
        if i == 0:
            ev = dict(lng=_row(P["gm_ln_g"]), lnb=_row(P["gm_ln_b"]), ws=P["gm_ws"].reshape(GM_G, CH, CH),
                      bs=P["gm_bs"].reshape(GM_G, CH, 1), cw=w("conv_w", 0, x), cb=_row(P["conv_b"]),
                      dtb=_pad_lanes(P["dt_bias"]), alog=_pad_lanes(P["a_log"]), dsk=_pad_lanes(P["d_skip"]),
                      nrm=_row(P["ssd_norm"]))
            w_main, w_dt = w("w_in_even", 0, x)
            x, h, s2 = _even_fwd("l0_mix", x, h, _row(P["norm_mix"][0]), w_main, w_dt, ev, w("w_out_even", 0, x),
                                 _row(P["norm_xq"][0]))
        else:
            x, h, s2 = _odd_fwd("l1_mix", x, h, _row(P["norm_mix"][1]), w("w_qkv", 0, x), w("b_qkv", 0, x), snk,
                                w("w_o_odd", 0, x), cos, sin, _row(P["norm_xq"][1]))
        x, h, s3 = _xattn_fwd(f"l{i}_xat", x, h, mem, _row(P["norm_xq"][i]), _row(P["norm_mem"][i]),
                              w("w_xq", i, x), w("w_xkv", i, x), w("w_xo", i, x), _row(P["norm_ffn2"][i]))
        x, h, s4 = _ffn_fwd(f"l{i}_ffn2", x, h, _row(P["norm_ffn2"][i]), w("w_ffn2_gu", i, x),
                            functools.partial(w, "w_ffn2_down", i), _row(P["norm_ffn1"][1]) if i == 0 else None)
        saved.append((s1, s2, s3, s4))

    loss, dx, dxb, d_final = _loss_head("loss_head", x, _row(P["final_norm"]), target)

    sm = {}
    dn = {k: [None, None] for k in ("norm_ffn1", "norm_mix", "norm_xq", "norm_mem", "norm_ffn2")}
    for i in (1, 0):
        s1, s2, s3, s4 = saved[i]
        dx, dxb, dn["norm_ffn2"][i] = _ffn_bwd(
            f"l{i}_ffn2", s4, dx, dxb, W["w_ffn2_gu", i], W["w_ffn2_down", i],
            lambda dwgu, dwd, i=i: putg({("w_ffn2_gu", i): dwgu, ("w_ffn2_down", i): dwd}))
        dx, dxb, dn["norm_xq"][i], dn["norm_mem"][i] = _xattn_bwd(
            f"l{i}_xat", s3, dx, dxb, W["w_xq", i], W["w_xkv", i], W["w_xo", i],
            lambda dwxq, dwxkv, dwxo, i=i: putg({("w_xq", i): dwxq, ("w_xkv", i): dwxkv, ("w_xo", i): dwxo}))
        if i == 0:
            dx, dxb, dn["norm_mix"][0], sm_even = _even_bwd(
                "l0_mix", s2, dx, dxb, w_main, w_dt, ev, W["w_out_even", 0],
                lambda dw_main, dw_dt, dwout: putg({("w_in_even", 0): (dw_main, dw_dt), ("w_out_even", 0): dwout}))
        else:
            dx, dxb, dn["norm_mix"][1], sm["b_qkv"], sm["sinks"] = _odd_bwd(
                "l1_mix", s2, dx, dxb, W["w_qkv", 0], snk, W["w_o_odd", 0], cos, sin,
                lambda dwqkv, dwo: putg({("w_qkv", 0): dwqkv, ("w_o_odd", 0): dwo}))
        dx, dxb, dn["norm_ffn1"][i] = _ffn_bwd(
            f"l{i}_ffn1", s1, dx, dxb, W["w_ffn1_gu", i], W["w_ffn1_down", i],
            lambda dwgu, dwd, i=i: putg({("w_ffn1_gu", i): dwgu, ("w_ffn1_down", i): dwd}))
    for k, v in dn.items():
        sm[k] = jnp.concatenate(v, 0)
    sm.update(gm_ln_g=sm_even["lng"], gm_ln_b=sm_even["lnb"], gm_ws=sm_even["ws"], gm_bs=sm_even["bs"],
              conv_w=sm_even["cw"], conv_b=sm_even["cb"], dt_bias=sm_even["dtb"][:, :NH], a_log=sm_even["alog"][:, :NH],
              d_skip=sm_even["dsk"][:, :NH], ssd_norm=sm_even["nrm"], sinks=sm["sinks"][:, :AH], final_norm=d_final)
    return loss[0, 0], dx, sm


def _chip_peers():
    x, y, c = lax.axis_index("x"), lax.axis_index("y"), lax.axis_index("c")
    return 2 * x + y, [((1 - x, y, c), 2 * (1 - x) + y), ((x, 1 - y, c), 2 * x + (1 - y)),
                       ((1 - x, 1 - y, c), 2 * (1 - x) + (1 - y))]


def _any_specs(n):
    return [pl.BlockSpec(memory_space=pl.ANY)] * n


_HBM = pl.BlockSpec(memory_space=pltpu.HBM)
_SEM = pl.BlockSpec(memory_space=pltpu.SEMAPHORE)
_EFFECT = pltpu.SideEffectType.DATAFLOW_SIDE_EFFECTING


def _own_slot(piece, chip):
    zone = lax.empty((4,) + piece.shape, piece.dtype)
    return lax.dynamic_update_slice(zone, piece[None], (chip,) + (0,) * piece.ndim)


def _chip_copies(srcs, lands, ssems, rsems, mode="chips"):
    c = lax.axis_index("c")
    sib = (lax.axis_index("x"), lax.axis_index("y"), 1 - c)
    if mode == "sibling":
        return [pltpu.make_async_remote_copy(src_ref=srcs[i], dst_ref=lands[i], send_sem=ssems[i].at[0],
                                             recv_sem=rsems[i].at[0], device_id=sib, device_id_type=MESH)
                for i in range(len(lands))]
    me, peers = _chip_peers()
    if mode == "half":
        return [pltpu.make_async_remote_copy(src_ref=lands[i].at[me, c], dst_ref=lands[i].at[me, c], send_sem=ssems[i].at[j],
                                             recv_sem=rsems[i].at[j], device_id=dev, device_id_type=MESH)
                for i in range(len(lands)) for j, (dev, _) in enumerate(peers)]
    if mode == "pass_on":
        return [pltpu.make_async_remote_copy(src_ref=lands[i].at[chip, c], dst_ref=lands[i].at[chip, c],
                                             send_sem=ssems[i].at[j], recv_sem=rsems[i].at[j], device_id=sib, device_id_type=MESH)
                for i in range(len(lands)) for j, (_, chip) in enumerate(peers)]
    return [pltpu.make_async_remote_copy(src_ref=lands[i].at[me] if srcs[i] is None else srcs[i].at[chip],
                                         dst_ref=lands[i].at[me], send_sem=ssems[i].at[j], recv_sem=rsems[i].at[j],
                                         device_id=dev, device_id_type=MESH)
            for i in range(len(lands)) for j, (dev, chip) in enumerate(peers)]


def _exchange_start(name, srcs, lands, mode="chips"):
    n = len(lands)
    ns = 0 if srcs is None else n

    def body(*refs):
        src_refs = [None] * n if srcs is None else refs[:n]
        land_refs = refs[ns:ns + n]
        ssems, rsems = refs[ns + n:ns + 2 * n], refs[ns + 2 * n:ns + 3 * n]
        token = refs[2 * ns + 4 * n]
        for cp in _chip_copies(src_refs, land_refs, ssems, rsems, mode):
            cp.start()
        token[...] = jnp.zeros_like(token)

    ins = ([] if srcs is None else list(srcs)) + list(lands)
    res = pl.pallas_call(
        body, name=name,
        out_shape=[pltpu.SemaphoreType.DMA((1 if mode == "sibling" else 3,))] * (2 * n) + [pltpu.HBM(a.shape, a.dtype) for a in ins]
        + [S((8, LANE), F32)],
        in_specs=[_HBM] * (ns + n),
        out_specs=[_SEM] * (2 * n) + [_HBM] * (ns + n) + [pl.BlockSpec(memory_space=pltpu.VMEM)],
        input_output_aliases={i: 2 * n + i for i in range(ns + n)},
        compiler_params=pltpu.CompilerParams(has_side_effects=_EFFECT),
    )(*[pltpu.with_memory_space_constraint(a, pltpu.HBM) for a in ins])
    items = [(res[i], res[n + i], None if srcs is None else res[2 * n + i], res[2 * n + ns + i]) for i in range(n)]
    return items, res[2 * n + ns + n][0, 0]


def _exchange_wait(name, item, after, mode="chips"):
    ssem, rsem, src, land = item
    ns = 0 if src is None else 1

    def body(*refs):
        src_ref = refs[0] if ns else None
        land_ref, ssem_ref, rsem_ref = refs[ns], refs[ns + 1], refs[ns + 2]
        for cp in _chip_copies([src_ref], [land_ref], [ssem_ref], [rsem_ref], mode):
            cp.wait_send()
            cp.wait_recv()

    ins = ([src] if ns else []) + [land]
    return pl.pallas_call(
        body, name=name, out_shape=[pltpu.HBM(a.shape, a.dtype) for a in ins],
        in_specs=[_HBM] * (ns + 1) + [_SEM, _SEM, pl.BlockSpec(memory_space=pl.ANY)], out_specs=[_HBM] * (ns + 1),
        input_output_aliases={i: i for i in range(ns + 1)}, compiler_params=pltpu.CompilerParams(has_side_effects=_EFFECT),
    )(*ins, ssem, rsem, after)[ns]


def _gather_all(name, v, after):
    def body(v_ref, _, o_ref, ssem, rsem, lsem):
        x, y, c = lax.axis_index("x"), lax.axis_index("y"), lax.axis_index("c")
        me = 4 * x + 2 * y + c
        loc = pltpu.make_async_copy(v_ref, o_ref.at[me], lsem)
        loc.start()
        copies = []
        for k in range(1, 8):
            fx, fy, fc = (k >> 2) & 1, (k >> 1) & 1, k & 1
            dev = (x ^ fx, y ^ fy, c ^ fc)
            cp = pltpu.make_async_remote_copy(src_ref=v_ref, dst_ref=o_ref.at[me], send_sem=ssem.at[k - 1],
                                              recv_sem=rsem.at[k - 1], device_id=dev, device_id_type=MESH)
            cp.start()
            copies.append(cp)
        for cp in copies:
            cp.wait()
        loc.wait()

    return pl.pallas_call(
        body, in_specs=_any_specs(2), out_specs=pl.BlockSpec(memory_space=pl.ANY), out_shape=S((8,) + v.shape, v.dtype),
        scratch_shapes=[pltpu.SemaphoreType.DMA((7,)), pltpu.SemaphoreType.DMA((7,)), pltpu.SemaphoreType.DMA(())],
        compiler_params=pltpu.CompilerParams(has_side_effects=True), name=name)(v, after)


def _row_tile(R, row_bytes, budget=4 << 20):
    if R * row_bytes <= budget or R % 16:
        return R
    t = max(16, budget // row_bytes // 16 * 16)
    while R % t:
        t -= 16
    return t


def _sum_slots(name, r, n):
    _, R, C = r.shape
    tr = _row_tile(R, C * (n * r.dtype.itemsize + 4))

    def body(r_ref, o_ref):
        acc = r_ref[0].astype(F32)
        for j in range(1, n):
            acc = acc + r_ref[j].astype(F32)
        o_ref[...] = acc

    return pl.pallas_call(
        body, grid=(R // tr,), in_specs=[pl.BlockSpec((n, tr, C), lambda i: (0, i, 0))],
        out_specs=pl.BlockSpec((tr, C), lambda i: (i, 0)), out_shape=S((R, C), F32),
        compiler_params=_params("parallel"), name=name)(r)


def _adamw(name, w, m, v, layer, g1, g2=None, into=None):
    nl, R, C = w.shape
    tr = _row_tile(R, C * 4 * 9)
    two, has_into = g2 is not None, into is not None

    def body(w_ref, m_ref, v_ref, g1_ref, *rest):
        rest = list(rest)
        g = g1_ref[...]
        if two:
            g = g + rest.pop(0)[...]
        g_ref, d_ref, nm_ref, nv_ref = rest[-4:]
        mn = B1 * m_ref[...] + (1.0 - B1) * g
        vn = B2 * v_ref[...] + (1.0 - B2) * jnp.square(g)
        m_hat = mn / (1.0 - B1 ** STEP)
        v_hat = vn / (1.0 - B2 ** STEP)
        g_ref[...] = g
        d_ref[...] = -LR * (m_hat / (jnp.sqrt(v_hat) + AEPS) + WD * w_ref[...])
        nm_ref[...] = mn
        nv_ref[...] = vn

    blk = pl.BlockSpec((tr, C), lambda i: (i, 0))
    lay = pl.BlockSpec((None, tr, C), lambda i: (layer, i, 0))
    args = [w, m, v, g1] + ([g2] if two else []) + (list(into) if has_into else [])
    in_specs = [lay] * 3 + [blk] * (2 if two else 1) + (_any_specs(4) if has_into else [])
    aliases = {len(args) - 4 + t: t for t in range(4)} if has_into else {}
    return pl.pallas_call(
        body, grid=(R // tr,), in_specs=in_specs, out_specs=[lay] * 4, out_shape=[S((nl, R, C), F32)] * 4,
        input_output_aliases=aliases, compiler_params=_params("parallel"), name=name)(*args)


_USE_ORDER = [("w_ffn1_gu", 0), ("w_ffn1_down", 0), ("conv_w", 0), ("w_in_even", 0), ("w_out_even", 0), ("w_xq", 0),
              ("w_xkv", 0), ("w_xo", 0), ("w_ffn2_gu", 0), ("w_ffn2_down", 0), ("w_ffn1_gu", 1), ("w_ffn1_down", 1),
              ("w_qkv", 0), ("b_qkv", 0), ("w_o_odd", 0), ("w_xq", 1), ("w_xkv", 1), ("w_xo", 1), ("w_ffn2_gu", 1),
              ("w_ffn2_down", 1)]
_SMALL = ["norm_ffn1", "norm_mix", "gm_ln_g", "gm_ln_b", "gm_ws", "gm_bs", "conv_w", "conv_b", "dt_bias", "a_log",
          "d_skip", "ssd_norm", "b_qkv", "sinks", "norm_xq", "norm_mem", "norm_ffn2", "final_norm"]
_WEIGHTS = ["norm_ffn1", "w_ffn1_gu", "w_ffn1_down", "norm_mix", "w_in_even", "gm_ln_g", "gm_ln_b", "gm_ws", "gm_bs",
            "conv_w", "conv_b", "dt_bias", "a_log", "d_skip", "ssd_norm", "w_out_even", "w_qkv", "b_qkv", "sinks",
            "w_o_odd", "norm_xq", "norm_mem", "w_xq", "w_xkv", "w_xo", "norm_ffn2", "w_ffn2_gu", "w_ffn2_down",
            "final_norm"]


def _pack(arrs):
    rows = []
    for a in arrs:
        f = a.reshape(-1).astype(F32)
        pad = (-f.shape[0]) % LANE
        rows.append(jnp.pad(f, (0, pad)).reshape(-1, LANE))
    out = jnp.concatenate(rows, 0)
    pad = (-out.shape[0]) % 8
    return jnp.pad(out, ((0, pad), (0, 0)))


def _unpack(packed, shapes):
    outs, r = [], 0
    for shp in shapes:
        n = math.prod(shp)
        nr = -(-n // LANE)
        outs.append(packed[r:r + nr].reshape(-1)[:n].reshape(shp))
        r += nr
    return outs


def kernel(x, mem, positions, norm_ffn1, w_ffn1_gu, w_ffn1_down, norm_mix, w_in_even, gm_ln_g, gm_ln_b, gm_ws, gm_bs, conv_w, conv_b, dt_bias, a_log, d_skip, ssd_norm, w_out_even, w_qkv, b_qkv, sinks, w_o_odd, norm_xq, norm_mem, w_xq, w_xkv, w_xo, norm_ffn2, w_ffn2_gu, w_ffn2_down, final_norm, loss_target, m_norm_ffn1, m_w_ffn1_gu, m_w_ffn1_down, m_norm_mix, m_w_in_even, m_gm_ln_g, m_gm_ln_b, m_gm_ws, m_gm_bs, m_conv_w, m_conv_b, m_dt_bias, m_a_log, m_d_skip, m_ssd_norm, m_w_out_even, m_w_qkv, m_b_qkv, m_sinks, m_w_o_odd, m_norm_xq, m_norm_mem, m_w_xq, m_w_xkv, m_w_xo, m_norm_ffn2, m_w_ffn2_gu, m_w_ffn2_down, m_final_norm, v_norm_ffn1, v_w_ffn1_gu, v_w_ffn1_down, v_norm_mix, v_w_in_even, v_gm_ln_g, v_gm_ln_b, v_gm_ws, v_gm_bs, v_conv_w, v_conv_b, v_dt_bias, v_a_log, v_d_skip, v_ssd_norm, v_w_out_even, v_w_qkv, v_b_qkv, v_sinks, v_w_o_odd, v_norm_xq, v_norm_mem, v_w_xq, v_w_xkv, v_w_xo, v_norm_ffn2, v_w_ffn2_gu, v_w_ffn2_down, v_final_norm):
    a = dict(locals())
    w = {k: a[k] for k in _WEIGHTS}
    mom = {k: a["m_" + k] for k in _WEIGHTS}
    var = {k: a["v_" + k] for k in _WEIGHTS}
    chip = 2 * lax.axis_index("x") + lax.axis_index("y")

    shards = [w[k][i:i + 1] if k in ("conv_w", "b_qkv") else w[k][i:i + 1].astype(BF16) for k, i in _USE_ORDER]
    half = shards[0].reshape((2, shards[0].shape[1] // 2) + shards[0].shape[2:])
    (first,), _ = _exchange_start("gather_start_first", None, [_own_slot(half, chip)], "half")
    rest, rest_started = _exchange_start("gather_start_rest", None, [_own_slot(s, chip) for s in shards[1:]])
    pending = dict(zip(_USE_ORDER[1:], rest))

    def getw(name, layer, after):
        if (name, layer) == _USE_ORDER[0]:
            zone = _exchange_wait("gather_wait_first_half", first, after, "half")
            (passed,), _ = _exchange_start("gather_pass_on_first", None, [zone], "pass_on")
            return _exchange_wait("gather_wait_first_passed", passed, after, "pass_on").reshape((4,) + shards[0].shape)
        got = _exchange_wait(f"gather_wait_{name}_{layer}", pending.pop((name, layer)), after)
        if name == "w_in_even":
            w_in = jnp.transpose(got[:, 0], (1, 0, 2)).reshape(D, EVEN_IN)
            return w_in[:, :EVEN_MAIN], jnp.pad(w_in[:, EVEN_MAIN:], ((0, 0), (0, LANE - (EVEN_IN - EVEN_MAIN))))
        if name == "conv_w":
            return jnp.transpose(got[:, 0], (1, 0, 2)).reshape(4, CONV_C)
        if name == "b_qkv":
            return got.reshape(1, ODD_IN)
        return got

    sent = []

    def putg(grads):
        names, arrs = [], []
        for (name, layer), g in grads.items():
            if name == "w_in_even":
                dw_in = jnp.concatenate([g[0], g[1][:, :EVEN_IN - EVEN_MAIN]], 1)
                g = jnp.transpose(dw_in.reshape(D, 4, EVEN_IN // 4), (1, 0, 2)).reshape(4, 1, D, EVEN_IN // 4)
            names.append((name, layer))
            arrs.append(g)
        own = [_own_slot(lax.dynamic_index_in_dim(g, chip, 0, keepdims=False), chip) for g in arrs]
        its, tok = _exchange_start(f"scatter_start_{names[0][0]}_{names[0][1]}", arrs, own)
        sent.append(list(zip(names, its)))
        return tok

    P = {k: w[k] for k in _SMALL}
    P["norm_ffn1"] = P["norm_ffn1"] + rest_started
    loss, grad_x, sm = _local_step(x[0], mem[0], positions[0], loss_target[0], getw, P, putg)
    loss = lax.psum(loss, ("x", "y", "c"))

    out = {}

    def update(groups, after):
        flying = []
        for grp in groups:
            part = []
            for (name, layer), it in grp:
                r = _exchange_wait(f"scatter_wait_{name}_{layer}", it, after)
                part.append(_sum_slots(f"sum_{name}_{layer}", r.reshape(4, -1, r.shape[-1]), 4))
            name0, layer0 = grp[0][0]
            its, _ = _exchange_start(f"swap_start_{name0}_{layer0}", part, [lax.empty(p.shape, p.dtype) for p in part], "sibling")
            flying += [(nm, p, it) for (nm, _), p, it in zip(grp, part, its)]
        for (name, layer), p1, it in flying:
            p2 = _exchange_wait(f"swap_wait_{name}_{layer}", it, after, "sibling")
            out[name] = _adamw(f"adamw_{name}_{layer}", w[name], mom[name], var[name], layer, p1, p2, out.get(name))
            after = out[name][0]

    update(sent[:-1], grad_x)
    done_a = out["w_out_even"][0]
    update(sent[-1:], done_a)

    full_shapes = {k: w[k].shape for k in _SMALL}
    full_shapes["conv_w"], full_shapes["b_qkv"] = (1, 4, CONV_C), (1, ODD_IN)
    packed = _pack([sm[k] for k in _SMALL])
    total = _sum_slots("sum_small", _gather_all("gather_small", packed, done_a), 8)
    gs = dict(zip(_SMALL, _unpack(total, [full_shapes[k] for k in _SMALL])))
    gs["conv_w"] = lax.dynamic_slice_in_dim(gs["conv_w"], chip * (CONV_C // 4), CONV_C // 4, 2)
    gs["b_qkv"] = lax.dynamic_slice_in_dim(gs["b_qkv"], chip * (ODD_IN // 4), ODD_IN // 4, 1)
    res = _adamw("adamw_small", _pack([w[k] for k in _SMALL])[None], _pack([mom[k] for k in _SMALL])[None],
                 _pack([var[k] for k in _SMALL])[None], 0, _pack([gs[k] for k in _SMALL]))
    shapes = [w[k].shape for k in _SMALL]
    for k, g, d, nm, nv in zip(_SMALL, *[_unpack(r[0], shapes) for r in res]):
        out[k] = [g, d, nm, nv]

    return (loss, grad_x[None], *[out[k][0] for k in _WEIGHTS], *[out[k][1] for k in _WEIGHTS],
            *[out[k][2] for k in _WEIGHTS], *[out[k][3] for k in _WEIGHTS])
```

```python
import functools
import math

import jax
import jax.numpy as jnp
from jax import lax
from jax.experimental import pallas as pl
from jax.experimental.pallas import tpu as pltpu

F32, BF16 = jnp.float32, jnp.bfloat16
S = jax.ShapeDtypeStruct
MESH = pl.DeviceIdType.MESH

D = 2048
DFF = 5632
EPS = 1e-5
CH = 128
GM_G, GM_GD = 4, 512
NH, HD, NG, HPG, NS = 32, 64, 4, 8, 128
CONV_C = 3072
EVEN_MAIN, EVEN_IN = 9216, 9248
AH, AKV, AREP, AHD = 32, 4, 8, 64
ODD_IN = 2560
XH, XHD, XW = 4, 128, 512
ATT_SCALE = AHD ** -0.5
X_SCALE = XHD ** -0.5
ROPE_THETA = 500000.0
ROT = 16
LR, B1, B2, AEPS, WD, STEP = 0.001, 0.9, 0.999, 1e-08, 0.01, 10
LANE = 128
VMEM_LIMIT_V7X = 56 * 1024 * 1024


def _params(*sem):
    return pltpu.CompilerParams(dimension_semantics=sem, vmem_limit_bytes=VMEM_LIMIT_V7X)


def _tile(dim, target):
    if dim <= target:
        return dim
    t = (target // LANE) * LANE
    while t > LANE and dim % t:
        t -= LANE
    assert dim % t == 0, (dim, target)
    return t


class Op:
    def __init__(self, arr, kind=None, layer=0):
        self.arr, self.kind, self.layer = arr, kind, layer
        if kind is None:
            self.R, self.C = arr.shape
        else:
            L = arr.shape[0]
            self.R = arr.shape[2] * (L if kind == "r" else 1)
            self.C = arr.shape[3] * (L if kind == "c" else 1)

    def unit(self, axis):
        if self.kind == "r" and axis == 0:
            return self.arr.shape[2]
        if self.kind == "c" and axis == 1:
            return self.arr.shape[3]
        return (self.R, self.C)[axis]

    def spec(self, tr, tc, pick):
        if self.kind is None:
            return pl.BlockSpec((tr, tc), lambda i, j, k: pick(i, j, k))
        l = self.layer
        if self.kind == "c":
            per = self.arr.shape[3] // tc
            return pl.BlockSpec((None, None, tr, tc),
                                lambda i, j, k: (pick(i, j, k)[1] // per, l, pick(i, j, k)[0], pick(i, j, k)[1] % per))
        per = self.arr.shape[2] // tr
        return pl.BlockSpec((None, None, tr, tc),
                            lambda i, j, k: (pick(i, j, k)[0] // per, l, pick(i, j, k)[0] % per, pick(i, j, k)[1]))


_DIMS = {"nn": (((1,), (0,)), ((), ())), "nt": (((1,), (1,)), ((), ())), "tn": (((0,), (0,)), ((), ()))}
_PICK_A = {"nn": lambda i, j, k: (i, k), "nt": lambda i, j, k: (i, k), "tn": lambda i, j, k: (k, i)}
_PICK_B = {"nn": lambda i, j, k: (k, j), "nt": lambda i, j, k: (j, k), "tn": lambda i, j, k: (k, j)}


def _mm(name, mode, a, b, out_dtype, *, out=None, res=None, bias=None, scale=1.0, norm_gain=None,
        tm_t=1024, tn_t=1024, tk_t=2048):
    if not isinstance(a, Op):
        a = Op(a)
    if not isinstance(b, Op):
        b = Op(b)
    if mode == "nn":
        M, K, N = a.R, a.C, b.C
        assert b.R == K
        um, uk, un = a.unit(0), math.gcd(a.unit(1), b.unit(0)), b.unit(1)
    elif mode == "nt":
        M, K, N = a.R, a.C, b.R
        assert b.C == K
        um, uk, un = a.unit(0), math.gcd(a.unit(1), b.unit(1)), b.unit(0)
    else:
        K, M, N = a.R, a.C, b.C
        assert b.R == K
        um, uk, un = a.unit(1), math.gcd(a.unit(0), b.unit(0)), b.unit(1)
    if out is not None:
        okind, oL, olayers, olayer = out
        if okind == "c":
            un = math.gcd(un, N // oL)
        else:
            um = math.gcd(um, M // oL)
    tm, tn, tk = _tile(um, tm_t), _tile(un, tn_t), _tile(uk, tk_t)
    gi, gj, gk = M // tm, N // tn, K // tk
    a_blk = (tm, tk) if mode != "tn" else (tk, tm)
    b_blk = {"nn": (tk, tn), "nt": (tn, tk), "tn": (tk, tn)}[mode]
    in_specs = [a.spec(*a_blk, _PICK_A[mode]), b.spec(*b_blk, _PICK_B[mode])]
    args = [a.arr, b.arr]
    if res is not None:
        in_specs.append(pl.BlockSpec((tm, tn), lambda i, j, k: (i, j)))
        args.append(res)
    if bias is not None:
        in_specs.append(pl.BlockSpec((1, tn), lambda i, j, k: (0, j)))
        args.append(bias)
    if out is None:
        out_shape = S((M, N), out_dtype)
        out_spec = pl.BlockSpec((tm, tn), lambda i, j, k: (i, j))
    else:
        shp = (oL, olayers, M, N // oL) if okind == "c" else (oL, olayers, M // oL, N)
        out_shape = S(shp, out_dtype)
        out_spec = Op(out_shape, okind, olayer).spec(tm, tn, lambda i, j, k: (i, j))
    has_res, has_bias, has_norm = res is not None, bias is not None, norm_gain is not None
    if has_norm:
        assert out is None and tn == N
        in_specs.append(pl.BlockSpec((1, N), lambda i, j, k: (0, 0)))
        args.append(norm_gain)
        out_shape = [out_shape, S((M, N), BF16)]
        out_spec = [out_spec, pl.BlockSpec((tm, tn), lambda i, j, k: (i, j))]
    dims = _DIMS[mode]

    def body(a_ref, b_ref, *rest):
        rest = list(rest)
        res_ref = rest.pop(0) if has_res else None
        bias_ref = rest.pop(0) if has_bias else None
        gain_ref = rest.pop(0) if has_norm else None
        o_ref = rest.pop(0)
        h_ref = rest.pop(0) if has_norm else None
        part = lax.dot_general(a_ref[...].astype(BF16), b_ref[...].astype(BF16), dims, preferred_element_type=F32)

        def finish(r):
            if scale != 1.0:
                r = r * scale
            if has_bias:
                r = r + bias_ref[...]
            if has_res:
                r = r + res_ref[...]
            o_ref[...] = r.astype(o_ref.dtype)
            if has_norm:
                h_ref[...] = (r * lax.rsqrt(jnp.mean(r * r, -1, keepdims=True) + EPS) * gain_ref[...]).astype(BF16)

        if gk == 1:
            finish(part)
            return
        acc, = rest
        k = pl.program_id(2)

        @pl.when(k == 0)
        def _():
            acc[...] = part

        @pl.when((k > 0) & (k < gk - 1))
        def _():
            acc[...] += part

        @pl.when(k == gk - 1)
        def _():
            finish(acc[...] + part)

    return pl.pallas_call(
        body, grid=(gi, gj, gk), in_specs=in_specs, out_specs=out_spec, out_shape=out_shape,
        scratch_shapes=[pltpu.VMEM((tm, tn), F32)] if gk > 1 else [],
        compiler_params=_params("parallel", "parallel", "arbitrary"), name=name)(*args)


def _rms_fwd(name, x, gain):
    T = x.shape[0]
    tt = _tile(T, 512)

    def body(x_ref, g_ref, o_ref):
        xv = x_ref[...]
        r = lax.rsqrt(jnp.mean(xv * xv, -1, keepdims=True) + EPS)
        o_ref[...] = (xv * r * g_ref[...]).astype(BF16)

    return pl.pallas_call(
        body, grid=(T // tt,),
        in_specs=[pl.BlockSpec((tt, D), lambda i: (i, 0)), pl.BlockSpec((1, D), lambda i: (0, 0))],
        out_specs=pl.BlockSpec((tt, D), lambda i: (i, 0)), out_shape=S((T, D), BF16),
        compiler_params=_params("parallel"), name=name)(x, gain)


def _rms_bwd(name, x, gain, dh, dx_in=None):
    T = x.shape[0]
    tt = _tile(T, 512)
    has_in = dx_in is not None

    def body(x_ref, g_ref, dh_ref, *rest):
        rest = list(rest)
        dxin_ref = rest.pop(0) if has_in else None
        dx_ref, dxb_ref, dg_ref = rest
        xv = x_ref[...]
        r = lax.rsqrt(jnp.mean(xv * xv, -1, keepdims=True) + EPS)
        xh = xv * r
        dy = dh_ref[...].astype(F32)
        dxh = dy * g_ref[...]
        dx = r * (dxh - xh * jnp.mean(dxh * xh, -1, keepdims=True))
        if has_in:
            dx = dx + dxin_ref[...]
        dx_ref[...] = dx
        dxb_ref[...] = dx.astype(BF16)
        part = jnp.sum(dy * xh, 0, keepdims=True)

        @pl.when(pl.program_id(0) == 0)
        def _():
            dg_ref[...] = part

        @pl.when(pl.program_id(0) > 0)
        def _():
            dg_ref[...] += part

    row = pl.BlockSpec((tt, D), lambda i: (i, 0))
    vec = pl.BlockSpec((1, D), lambda i: (0, 0))
    args = [x, gain, dh] + ([dx_in] if has_in else [])
    return pl.pallas_call(
        body, grid=(T // tt,), in_specs=[row, vec, row] + ([row] if has_in else []),
        out_specs=[row, row, vec], out_shape=[S((T, D), F32), S((T, D), BF16), S((1, D), F32)],
        compiler_params=_params("arbitrary"), name=name)(*args)


def _sigmoid(x):
    return 0.5 * jnp.tanh(0.5 * x) + 0.5


def _ffn_up(name, h, w4, layer, tm_t=512, tn_t=1408):
    T = h.shape[0]
    n_sh = w4.shape[3]
    tm, tn = _tile(T, tm_t), _tile(n_sh, tn_t)
    per = n_sh // tn

    def body(h_ref, wg_ref, wu_ref, g_ref, u_ref, a_ref):
        hv = h_ref[...]
        g = jnp.dot(hv, wg_ref[...], preferred_element_type=F32)
        u = jnp.dot(hv, wu_ref[...], preferred_element_type=F32)
        g_ref[...] = g.astype(BF16)
        u_ref[...] = u.astype(BF16)
        a_ref[...] = (g * _sigmoid(g) * u).astype(BF16)

    o = pl.BlockSpec((tm, tn), lambda j, i: (i, j))
    return pl.pallas_call(
        body, grid=(DFF // tn, T // tm),
        in_specs=[pl.BlockSpec((tm, D), lambda j, i: (i, 0)),
                  pl.BlockSpec((None, None, D, tn), lambda j, i: (j // per, layer, 0, j % per)),
                  pl.BlockSpec((None, None, D, tn), lambda j, i: (2 + j // per, layer, 0, j % per))],
        out_specs=[o, o, o], out_shape=[S((T, DFF), BF16)] * 3,
        compiler_params=_params("parallel", "parallel"), name=name)(h, w4, w4)


def _ffn_dact(name, dxb, wd4, layer, g, u, tm_t=512):
    T = dxb.shape[0]
    r_sh = wd4.shape[2]
    tm, tn = _tile(T, tm_t), _tile(r_sh, 1408)
    per = r_sh // tn

    def body(dx_ref, w_ref, g_ref, u_ref, o_ref):
        da = 0.5 * lax.dot_general(dx_ref[...], w_ref[...], _DIMS["nt"], preferred_element_type=F32)
        gv, uv = g_ref[...].astype(F32), u_ref[...].astype(F32)
        sg = _sigmoid(gv)
        o_ref[0, 0] = (da * uv * sg * (1.0 + gv * (1.0 - sg))).astype(BF16)
        o_ref[1, 0] = (da * gv * sg).astype(BF16)

    t = pl.BlockSpec((tm, tn), lambda j, i: (i, j))
    return pl.pallas_call(
        body, grid=(DFF // tn, T // tm),
        in_specs=[pl.BlockSpec((tm, D), lambda j, i: (i, 0)),
                  pl.BlockSpec((None, None, tn, D), lambda j, i: (j // per, layer, j % per, 0)), t, t],
        out_specs=pl.BlockSpec((2, 1, tm, tn), lambda j, i: (0, 0, i, j)), out_shape=S((2, 1, T, DFF), BF16),
        compiler_params=_params("parallel", "parallel"), name=name)(dxb, wd4, g, u)


def _gelu(x):
    return 0.5 * x * (1.0 + lax.erf(x * 0.7071067811865476))


def _causal(n):
    return lax.broadcasted_iota(jnp.int32, (n, n), 0) >= lax.broadcasted_iota(jnp.int32, (n, n), 1)


def _gmlp_math(u_raw, v_raw, lng, lnb, ws, bs):
    causal = _causal(CH)
    outs = []
    for g in range(GM_G):
        u, v = _gelu(u_raw[g]), _gelu(v_raw[g])
        mu = jnp.mean(v, -1, keepdims=True)
        var = jnp.mean(jnp.square(v - mu), -1, keepdims=True)
        vn = (v - mu) * lax.rsqrt(var + EPS) * lng[g] + lnb[g]
        wm = jnp.where(causal, ws[g], 0.0)
        s = jnp.dot(wm.astype(BF16), vn.astype(BF16), preferred_element_type=F32) + bs[g]
        outs.append(u * s)
    return outs


def _gmlp_load(proj_ref, lng_ref, lnb_ref, ws_ref, bs_ref):
    sl = lambda g, off: slice(off + g * GM_GD, off + (g + 1) * GM_GD)
    u_raw = [proj_ref[:, sl(g, 0)].astype(F32) for g in range(GM_G)]
    v_raw = [proj_ref[:, sl(g, D)].astype(F32) for g in range(GM_G)]
    lng = [lng_ref[:, sl(g, 0)] for g in range(GM_G)]
    lnb = [lnb_ref[:, sl(g, 0)] for g in range(GM_G)]
    ws = [ws_ref[g] for g in range(GM_G)]
    bs = [bs_ref[g] for g in range(GM_G)]
    return u_raw, v_raw, lng, lnb, ws, bs


_GM_PAR = lambda: [pl.BlockSpec((1, D), lambda i: (0, 0)), pl.BlockSpec((1, D), lambda i: (0, 0)),
                   pl.BlockSpec((GM_G, CH, CH), lambda i: (0, 0, 0)), pl.BlockSpec((GM_G, CH, 1), lambda i: (0, 0, 0))]


def _gmlp_fwd(name, proj, lng, lnb, ws, bs):
    T = proj.shape[0]

    def body(proj_ref, lng_ref, lnb_ref, ws_ref, bs_ref, o_ref):
        outs = _gmlp_math(*_gmlp_load(proj_ref, lng_ref, lnb_ref, ws_ref, bs_ref))
        for g in range(GM_G):
            o_ref[:, g * GM_GD:(g + 1) * GM_GD] = outs[g].astype(BF16)

    return pl.pallas_call(
        body, grid=(T // CH,), in_specs=[pl.BlockSpec((CH, 2 * D), lambda i: (i, 0))] + _GM_PAR(),
        out_specs=pl.BlockSpec((CH, D), lambda i: (i, 0)), out_shape=S((T, 2 * D), BF16),
        compiler_params=_params("parallel"), name=name)(proj, lng, lnb, ws, bs)


def _acc_store(first, ref, idx, val):
    @pl.when(first)
    def _():
        ref[idx] = val

    @pl.when(jnp.logical_not(first))
    def _():
        ref[idx] += val


def _gmlp_bwd(name, proj, lng, lnb, ws, bs, dmix, dproj):
    T = proj.shape[0]

    def body(proj_ref, lng_ref, lnb_ref, ws_ref, bs_ref, dmix_ref, _, dproj_ref, dlng_ref, dlnb_ref, dws_ref, dbs_ref):
        first = pl.program_id(0) == 0
        prim = _gmlp_load(proj_ref, lng_ref, lnb_ref, ws_ref, bs_ref)
        _, vjp = jax.vjp(_gmlp_math, *prim)
        du, dv, dlng, dlnb, dws, dbs = vjp([dmix_ref[:, g * GM_GD:(g + 1) * GM_GD].astype(F32) for g in range(GM_G)])
        for g in range(GM_G):
            sl = slice(g * GM_GD, (g + 1) * GM_GD)
            dproj_ref[:, sl] = du[g].astype(BF16)
            dproj_ref[:, D + g * GM_GD:D + (g + 1) * GM_GD] = dv[g].astype(BF16)
            _acc_store(first, dlng_ref, (slice(None), sl), dlng[g])
            _acc_store(first, dlnb_ref, (slice(None), sl), dlnb[g])
            _acc_store(first, dws_ref, g, dws[g])
            _acc_store(first, dbs_ref, g, dbs[g])

    par = _GM_PAR()
    return pl.pallas_call(
        body, grid=(T // CH,),
        in_specs=[pl.BlockSpec((CH, 2 * D), lambda i: (i, 0))] + par +
                 [pl.BlockSpec((CH, D), lambda i: (i, 0)), pl.BlockSpec(memory_space=pl.ANY)],
        out_specs=[pl.BlockSpec((CH, 2 * D), lambda i: (i, 0))] + par,
        out_shape=[S(dproj.shape, BF16), S((1, D), F32), S((1, D), F32), S((GM_G, CH, CH), F32), S((GM_G, CH, 1), F32)],
        input_output_aliases={6: 0}, compiler_params=_params("arbitrary"), name=name)(proj, lng, lnb, ws, bs, dmix, dproj)


CONV_TT = 256
HALO = 8


def _shift_rows(cur, halo_after, s):
    if s == 0:
        return cur
    n = cur.shape[0]
    return pltpu.roll(jnp.concatenate([cur, halo_after], 0), s, 0)[:n]


def _conv_fwd(name, proj, w, b):
    T = proj.shape[0]
    tt = _tile(T, CONV_TT)
    hb = tt // HALO

    def body(x_ref, halo_ref, w_ref, b_ref, y_ref, xc_ref):
        i = pl.program_id(0)
        x = x_ref[...].astype(F32)
        halo = halo_ref[...].astype(F32) * (i > 0).astype(F32)
        y = b_ref[...] + w_ref[3:4, :] * x
        for s in (1, 2, 3):
            y = y + w_ref[3 - s:4 - s, :] * _shift_rows(x, halo, s)
        y_ref[...] = y.astype(BF16)
        xc_ref[...] = (y * _sigmoid(y)).astype(BF16)

    o = pl.BlockSpec((tt, CONV_C), lambda i: (i, 0))
    return pl.pallas_call(
        body, grid=(T // tt,),
        in_specs=[pl.BlockSpec((tt, CONV_C), lambda i: (i, 2)),
                  pl.BlockSpec((HALO, CONV_C), lambda i: (jnp.maximum(i * hb - 1, 0), 2)),
                  pl.BlockSpec((4, CONV_C), lambda i: (0, 0)), pl.BlockSpec((1, CONV_C), lambda i: (0, 0))],
        out_specs=[o, o], out_shape=[S((T, CONV_C), BF16)] * 2,
        compiler_params=_params("parallel"), name=name)(proj, proj, w, b)


def _conv_bwd(name, proj, ypre, dxc, w, dproj):
    T = proj.shape[0]
    tt = _tile(T, CONV_TT)
    hb = tt // HALO
    nt = T // tt

    def dsilu(y):
        sg = _sigmoid(y)
        return sg * (1.0 + y * (1.0 - sg))

    def body(x_ref, xh_ref, y_ref, yn_ref, d_ref, dn_ref, w_ref, _, dproj_ref, dw_ref, db_ref):
        i = pl.program_id(0)
        first = i == 0
        x = x_ref[...].astype(F32)
        halo = xh_ref[...].astype(F32) * (i > 0).astype(F32)
        dy = d_ref[...].astype(F32) * dsilu(y_ref[...].astype(F32))
        dyn = dn_ref[...].astype(F32) * dsilu(yn_ref[...].astype(F32)) * (i < nt - 1).astype(F32)
        ext = jnp.concatenate([dy, dyn], 0)
        dx = w_ref[3:4, :] * dy
        _acc_store(first, dw_ref, (slice(3, 4), slice(None)), jnp.sum(x * dy, 0, keepdims=True))
        for s in (1, 2, 3):
            dx = dx + w_ref[3 - s:4 - s, :] * pltpu.roll(ext, tt + HALO - s, 0)[:tt]
            _acc_store(first, dw_ref, (slice(3 - s, 4 - s), slice(None)),
                       jnp.sum(_shift_rows(x, halo, s) * dy, 0, keepdims=True))
        _acc_store(first, db_ref, (slice(None), slice(None)), jnp.sum(dy, 0, keepdims=True))
        dproj_ref[...] = dx.astype(BF16)

    cur = pl.BlockSpec((tt, CONV_C), lambda i: (i, 0))
    nxt = pl.BlockSpec((HALO, CONV_C), lambda i: (jnp.minimum((i + 1) * hb, T // HALO - 1), 0))
    return pl.pallas_call(
        body, grid=(nt,),
        in_specs=[pl.BlockSpec((tt, CONV_C), lambda i: (i, 2)),
                  pl.BlockSpec((HALO, CONV_C), lambda i: (jnp.maximum(i * hb - 1, 0), 2)),
                  cur, nxt, cur, nxt, pl.BlockSpec((4, CONV_C), lambda i: (0, 0)), pl.BlockSpec(memory_space=pl.ANY)],
        out_specs=[pl.BlockSpec((tt, CONV_C), lambda i: (i, 2)), pl.BlockSpec((4, CONV_C), lambda i: (0, 0)),
                   pl.BlockSpec((1, CONV_C), lambda i: (0, 0))],
        out_shape=[S(dproj.shape, BF16), S((4, CONV_C), F32), S((1, CONV_C), F32)],
        input_output_aliases={7: 0}, compiler_params=_params("arbitrary"), name=name)(proj, proj, ypre, ypre, dxc, dxc, w, dproj)


def _softplus(x):
    return jnp.maximum(x, 0.0) + jnp.log(1.0 + jnp.exp(-jnp.abs(x)))


def _ssd_math(x, Bm, Cm, dtr, z, prev, dtb, alog, dsk, nrm):
    hi = lax.Precision.HIGHEST
    causal = _causal(CH)
    tri = causal.astype(F32)
    lane = lax.broadcasted_iota(jnp.int32, (1, LANE), 1)
    sub = lax.broadcasted_iota(jnp.int32, (LANE, 1), 0)
    dt = _softplus(dtr + dtb)
    a = dt * (-jnp.exp(alog))
    a_cs = jnp.dot(tri, a, preferred_element_type=F32, precision=hi)
    a_csT = lax.dot_general(a, tri, (((0,), (1,)), ((), ())), preferred_element_type=F32, precision=hi)
    a_last = jnp.sum(a, 0, keepdims=True)
    gw = HPG * HD
    outs, new = [], []
    for g in range(NG):
        spread = (lax.broadcasted_iota(jnp.int32, (LANE, gw), 0)
                  == g * HPG + lax.broadcasted_iota(jnp.int32, (LANE, gw), 1) // HD).astype(F32)
        to_lanes = lambda v: jnp.dot(v, spread, preferred_element_type=F32, precision=hi)
        col_e, dt_e, last_e, dsk_e = to_lanes(a_cs), to_lanes(dt), to_lanes(a_last), to_lanes(dsk)
        last_r = lax.dot_general(spread, a_last, (((0,), (1,)), ((), ())), preferred_element_type=F32, precision=hi)
        cb = lax.dot_general(Cm[g].astype(BF16), Bm[g].astype(BF16), _DIMS["nt"], preferred_element_type=F32)
        xg = jnp.concatenate(x[g * HPG:(g + 1) * HPG], 1)
        yd = []
        for h in range(g * HPG, (g + 1) * HPG):
            ohl = (lane == h).astype(F32)
            col = jnp.sum(a_cs * ohl, 1, keepdims=True)
            row = jnp.sum(a_csT * (sub == h).astype(F32), 0, keepdims=True)
            dtc = jnp.sum(dt * ohl, 1, keepdims=True)
            lmat = jnp.where(causal, jnp.exp(jnp.where(causal, col - row, 0.0)), 0.0)
            yd.append(jnp.dot((cb * lmat).astype(BF16), (x[h] * dtc).astype(BF16), preferred_element_type=F32))
        y = jnp.concatenate(yd, 1)
        y = y + jnp.exp(col_e) * lax.dot_general(Cm[g].astype(BF16), prev[g].astype(BF16), _DIMS["nt"],
                                                 preferred_element_type=F32)
        st = lax.dot_general((xg * dt_e * jnp.exp(last_e - col_e)).astype(BF16), Bm[g].astype(BF16), _DIMS["tn"],
                             preferred_element_type=F32)
        new.append(prev[g] * jnp.exp(last_r) + st)
        yg = (y + xg * dsk_e) * (z[g] * _sigmoid(z[g]))
        yg = yg * lax.rsqrt(jnp.mean(yg * yg, -1, keepdims=True) + EPS)
        outs.append(yg * nrm[g])
    return outs, new


def _ssd_load(xc_ref, dtr_ref, z_ref, state_ref, dtb_ref, alog_ref, dsk_ref, nrm_ref):
    gw = HPG * HD
    x = [xc_ref[:, h * HD:(h + 1) * HD].astype(F32) for h in range(NH)]
    Bm = [xc_ref[:, D + g * NS:D + (g + 1) * NS].astype(F32) for g in range(NG)]
    Cm = [xc_ref[:, D + NG * NS + g * NS:D + NG * NS + (g + 1) * NS].astype(F32) for g in range(NG)]
    z = [z_ref[:, g * gw:(g + 1) * gw].astype(F32) for g in range(NG)]
    prev = [state_ref[g * gw:(g + 1) * gw, :] for g in range(NG)]
    nrm = [nrm_ref[:, g * gw:(g + 1) * gw] for g in range(NG)]
    return x, Bm, Cm, dtr_ref[...], z, prev, dtb_ref[...], alog_ref[...], dsk_ref[...], nrm


_SSD_PAR = lambda: [pl.BlockSpec((1, LANE), lambda c: (0, 0))] * 3 + [pl.BlockSpec((1, D), lambda c: (0, 0))]


def _ssd_fwd(name, xc, dtr, proj, dtb, alog, dsk, nrm, mix):
    T = xc.shape[0]
    nc = T // CH

    def body(xc_ref, dtr_ref, z_ref, dtb_ref, alog_ref, dsk_ref, nrm_ref, _, mix_ref, prev_ref, state):
        @pl.when(pl.program_id(0) == 0)
        def _():
            state[...] = jnp.zeros_like(state)

        prev_ref[...] = state[...]
        outs, new = _ssd_math(*_ssd_load(xc_ref, dtr_ref, z_ref, state, dtb_ref, alog_ref, dsk_ref, nrm_ref))
        for g in range(NG):
            mix_ref[:, g * 512:(g + 1) * 512] = outs[g].astype(BF16)
            state[g * 512:(g + 1) * 512, :] = new[g]

    return pl.pallas_call(
        body, grid=(nc,),
        in_specs=[pl.BlockSpec((CH, CONV_C), lambda c: (c, 0)), pl.BlockSpec((CH, LANE), lambda c: (c, 0)),
                  pl.BlockSpec((CH, D), lambda c: (c, 2))] + _SSD_PAR() + [pl.BlockSpec(memory_space=pl.ANY)],
        out_specs=[pl.BlockSpec((CH, D), lambda c: (c, 1)), pl.BlockSpec((None, NH * HD, NS), lambda c: (c, 0, 0))],
        out_shape=[S(mix.shape, BF16), S((nc, NH * HD, NS), F32)],
        scratch_shapes=[pltpu.VMEM((NH * HD, NS), F32)], input_output_aliases={7: 0},
        compiler_params=_params("arbitrary"), name=name)(xc, dtr, proj, dtb, alog, dsk, nrm, mix)


def _ssd_bwd(name, xc, dtr, proj, prevs, dtb, alog, dsk, nrm, dmix, dproj):
    T = xc.shape[0]
    nc = T // CH
    rev = lambda c: nc - 1 - c

    def body(xc_ref, dtr_ref, z_ref, prev_ref, dtb_ref, alog_ref, dsk_ref, nrm_ref, dmix_ref, _,
             dproj_ref, dxc_ref, ddtr_ref, ddtb_ref, dalog_ref, ddsk_ref, dnrm_ref, dstate):
        first = pl.program_id(0) == 0

        @pl.when(first)
        def _():
            dstate[...] = jnp.zeros_like(dstate)

        prim = _ssd_load(xc_ref, dtr_ref, z_ref, prev_ref, dtb_ref, alog_ref, dsk_ref, nrm_ref)
        _, vjp = jax.vjp(_ssd_math, *prim)
        douts = [dmix_ref[:, g * 512:(g + 1) * 512].astype(F32) for g in range(NG)]
        dnew = [dstate[g * 512:(g + 1) * 512, :] for g in range(NG)]
        dx, dB, dC, ddtr, dz, dprev, ddtb, dalog, ddsk, dnrm = vjp((douts, dnew))
        for h in range(NH):
            dxc_ref[:, h * HD:(h + 1) * HD] = dx[h].astype(BF16)
        for g in range(NG):
            dstate[g * 512:(g + 1) * 512, :] = dprev[g]
            dxc_ref[:, D + g * NS:D + (g + 1) * NS] = dB[g].astype(BF16)
            dxc_ref[:, D + NG * NS + g * NS:D + NG * NS + (g + 1) * NS] = dC[g].astype(BF16)
            dproj_ref[:, g * 512:(g + 1) * 512] = dz[g].astype(BF16)
            _acc_store(first, dnrm_ref, (slice(None), slice(g * 512, (g + 1) * 512)), dnrm[g])
        ddtr_ref[...] = ddtr
        _acc_store(first, ddtb_ref, (slice(None), slice(None)), ddtb)
        _acc_store(first, dalog_ref, (slice(None), slice(None)), dalog)
        _acc_store(first, ddsk_ref, (slice(None), slice(None)), ddsk)

    vec = pl.BlockSpec((1, LANE), lambda c: (0, 0))
    return pl.pallas_call(
        body, grid=(nc,),
        in_specs=[pl.BlockSpec((CH, CONV_C), lambda c: (rev(c), 0)), pl.BlockSpec((CH, LANE), lambda c: (rev(c), 0)),
                  pl.BlockSpec((CH, D), lambda c: (rev(c), 2)),
                  pl.BlockSpec((None, NH * HD, NS), lambda c: (rev(c), 0, 0))] + _SSD_PAR() +
                 [pl.BlockSpec((CH, D), lambda c: (rev(c), 1)), pl.BlockSpec(memory_space=pl.ANY)],
        out_specs=[pl.BlockSpec((CH, D), lambda c: (rev(c), 2)), pl.BlockSpec((CH, CONV_C), lambda c: (rev(c), 0)),
                   pl.BlockSpec((CH, LANE), lambda c: (rev(c), 0)), vec, vec, vec, pl.BlockSpec((1, D), lambda c: (0, 0))],
        out_shape=[S(dproj.shape, BF16), S((T, CONV_C), BF16), S((T, LANE), F32), S((1, LANE), F32), S((1, LANE), F32),
                   S((1, LANE), F32), S((1, D), F32)],
        scratch_shapes=[pltpu.VMEM((NH * HD, NS), F32)], input_output_aliases={9: 0},
        compiler_params=_params("arbitrary"), name=name)(xc, dtr, proj, prevs, dtb, alog, dsk, nrm, dmix, dproj)


def _rope(x, c, s, sign):
    W = x.shape[1]
    reps = W // LANE
    C, Sg = jnp.tile(c, (1, reps)), jnp.tile(s, (1, reps))
    lane = lax.broadcasted_iota(jnp.int32, x.shape, 1) % AHD
    up, dn = pltpu.roll(x, W - ROT // 2, 1), pltpu.roll(x, ROT // 2, 1)
    sw = jnp.where(lane < ROT // 2, up, jnp.where(lane < ROT, dn, 0.0))
    return x * C + sign * sw * Sg


def _rope_fwd(name, qkv, cos, sin):
    T = qkv.shape[0]
    tt = _tile(T, 256)
    KV = AKV * AHD

    def body(x_ref, c_ref, s_ref, o_ref):
        c, s = c_ref[...], s_ref[...]
        o_ref[:, :D] = _rope(x_ref[:, :D], c, s, 1.0).astype(BF16)
        o_ref[:, D:D + KV] = _rope(x_ref[:, D:D + KV], c, s, 1.0).astype(BF16)
        o_ref[:, D + KV:] = x_ref[:, D + KV:].astype(BF16)

    tab = pl.BlockSpec((tt, LANE), lambda i: (i, 0))
    return pl.pallas_call(
        body, grid=(T // tt,), in_specs=[pl.BlockSpec((tt, ODD_IN), lambda i: (i, 0)), tab, tab],
        out_specs=pl.BlockSpec((tt, ODD_IN), lambda i: (i, 0)), out_shape=S((T, ODD_IN), BF16),
        compiler_params=_params("parallel"), name=name)(qkv, cos, sin)


def _rope_bwd(name, dq, dkv_cur, dkv_prev, cos, sin):
    T = dq.shape[0]
    nb = T // CH
    KV = AKV * AHD

    def body(dq_ref, cur_ref, nxt_ref, c_ref, s_ref, o_ref, db_ref):
        n = pl.program_id(0)
        c, s = c_ref[...], s_ref[...]
        dkv = cur_ref[...] + nxt_ref[...] * (n < nb - 1).astype(F32)
        o_ref[:, :D] = _rope(dq_ref[...].astype(F32), c, s, -1.0).astype(BF16)
        o_ref[:, D:D + KV] = _rope(dkv[:, :KV], c, s, -1.0).astype(BF16)
        o_ref[:, D + KV:] = dkv[:, KV:].astype(BF16)
        _acc_store(n == 0, db_ref, (slice(None), slice(None)), jnp.sum(o_ref[...].astype(F32), 0, keepdims=True))

    tab = pl.BlockSpec((CH, LANE), lambda n: (n, 0))
    return pl.pallas_call(
        body, grid=(nb,),
        in_specs=[pl.BlockSpec((CH, D), lambda n: (n, 0)), pl.BlockSpec((CH, 2 * KV), lambda n: (n, 0)),
                  pl.BlockSpec((CH, 2 * KV), lambda n: (jnp.minimum(n + 1, nb - 1), 0)), tab, tab],
        out_specs=[pl.BlockSpec((CH, ODD_IN), lambda n: (n, 0)), pl.BlockSpec((1, ODD_IN), lambda n: (0, 0))],
        out_shape=[S((T, ODD_IN), BF16), S((1, ODD_IN), F32)],
        compiler_params=_params("arbitrary"), name=name)(dq, dkv_cur, dkv_prev, cos, sin)


def _swa_math(q, kp, kc, vp, vc, snk, mask):
    outs = []
    for k in range(AKV):
        K = jnp.concatenate([kp[k], kc[k]], 0).astype(BF16)
        V = jnp.concatenate([vp[k], vc[k]], 0).astype(BF16)
        s = lax.dot_general(q[k].astype(BF16), K, _DIMS["nt"], preferred_element_type=F32) * ATT_SCALE
        s = jnp.where(mask, s, -jnp.inf)
        m = lax.stop_gradient(jnp.maximum(jnp.max(s, -1, keepdims=True), snk[k]))
        p = jnp.exp(s - m)
        pr = p / (jnp.sum(p, -1, keepdims=True) + jnp.exp(snk[k] - m))
        outs.append(jnp.dot(pr.astype(BF16), V, preferred_element_type=F32))
    return outs


def _stack_heads(ref, k):
    return jnp.concatenate([ref[:, (k * AREP + r) * AHD:(k * AREP + r + 1) * AHD].astype(F32) for r in range(AREP)], 0)


def _swa_load(q_ref, cur_ref, prv_ref, snk_ref):
    KV = AKV * AHD
    q = [_stack_heads(q_ref, k) for k in range(AKV)]
    kc = [cur_ref[:, k * AHD:(k + 1) * AHD].astype(F32) for k in range(AKV)]
    vc = [cur_ref[:, KV + k * AHD:KV + (k + 1) * AHD].astype(F32) for k in range(AKV)]
    kp = [prv_ref[:, k * AHD:(k + 1) * AHD].astype(F32) for k in range(AKV)]
    vp = [prv_ref[:, KV + k * AHD:KV + (k + 1) * AHD].astype(F32) for k in range(AKV)]
    snk = [jnp.concatenate([jnp.broadcast_to(snk_ref[:, k * AREP + r:k * AREP + r + 1], (CH, 1)) for r in range(AREP)], 0)
           for k in range(AKV)]
    return q, kp, kc, vp, vc, snk


def _swa_mask(n):
    iq = lax.broadcasted_iota(jnp.int32, (AREP * CH, 2 * CH), 0) % CH
    js = lax.broadcasted_iota(jnp.int32, (AREP * CH, 2 * CH), 1)
    rel = iq + CH - js
    return (rel >= 0) & (rel < CH) & ((n > 0) | (js >= CH))


def _swa_specs(T):
    KV = AKV * AHD
    return [pl.BlockSpec((CH, D), lambda n: (n, 0)), pl.BlockSpec((CH, 2 * KV), lambda n: (n, D // (2 * KV))),
            pl.BlockSpec((CH, 2 * KV), lambda n: (jnp.maximum(n - 1, 0), D // (2 * KV))),
            pl.BlockSpec((1, LANE), lambda n: (0, 0))]


def _swa_fwd(name, qkvr, snk):
    T = qkvr.shape[0]

    def body(q_ref, cur_ref, prv_ref, snk_ref, o_ref):
        outs = _swa_math(*_swa_load(q_ref, cur_ref, prv_ref, snk_ref), _swa_mask(pl.program_id(0)))
        for h in range(AH):
            k, r = divmod(h, AREP)
            o_ref[:, h * AHD:(h + 1) * AHD] = outs[k][r * CH:(r + 1) * CH].astype(BF16)

    return pl.pallas_call(
        body, grid=(T // CH,), in_specs=_swa_specs(T), out_specs=pl.BlockSpec((CH, D), lambda n: (n, 0)),
        out_shape=S((T, D), BF16), compiler_params=_params("parallel"), name=name)(qkvr, qkvr, qkvr, snk)


def _swa_bwd(name, qkvr, snk, do):
    T = qkvr.shape[0]
    KV = AKV * AHD

    def body(q_ref, cur_ref, prv_ref, snk_ref, do_ref, dq_ref, dcur_ref, dprv_ref, dsnk_ref):
        n = pl.program_id(0)

        @pl.when(n == 0)
        def _():
            dsnk_ref[...] = jnp.zeros_like(dsnk_ref)

        prim = _swa_load(q_ref, cur_ref, prv_ref, snk_ref)
        mask = _swa_mask(n)
        _, vjp = jax.vjp(lambda *p: _swa_math(*p, mask), *prim)
        dq, dkp, dkc, dvp, dvc, dsnk = vjp([_stack_heads(do_ref, k) for k in range(AKV)])
        for h in range(AH):
            k, r = divmod(h, AREP)
            dq_ref[:, h * AHD:(h + 1) * AHD] = dq[k][r * CH:(r + 1) * CH].astype(BF16)
            dsnk_ref[:, h:h + 1] += jnp.sum(dsnk[k][r * CH:(r + 1) * CH], 0, keepdims=True)
        for k in range(AKV):
            dcur_ref[:, k * AHD:(k + 1) * AHD] = dkc[k]
            dcur_ref[:, KV + k * AHD:KV + (k + 1) * AHD] = dvc[k]
            dprv_ref[:, k * AHD:(k + 1) * AHD] = dkp[k]
            dprv_ref[:, KV + k * AHD:KV + (k + 1) * AHD] = dvp[k]

    kv = pl.BlockSpec((CH, 2 * KV), lambda n: (n, 0))
    return pl.pallas_call(
        body, grid=(T // CH,), in_specs=_swa_specs(T) + [pl.BlockSpec((CH, D), lambda n: (n, 0))],
        out_specs=[pl.BlockSpec((CH, D), lambda n: (n, 0)), kv, kv, pl.BlockSpec((1, LANE), lambda n: (0, 0))],
        out_shape=[S((T, D), BF16), S((T, 2 * KV), F32), S((T, 2 * KV), F32), S((1, LANE), F32)],
        compiler_params=_params("arbitrary"), name=name)(qkvr, qkvr, qkvr, snk, do)


def _xat_math(q, k, v):
    outs = []
    for h in range(XH):
        s = lax.dot_general(q[h].astype(BF16), k[h].astype(BF16), _DIMS["nt"], preferred_element_type=F32) * X_SCALE
        m = lax.stop_gradient(jnp.max(s, -1, keepdims=True))
        p = jnp.exp(s - m)
        pr = p / jnp.sum(p, -1, keepdims=True)
        outs.append(jnp.dot(pr.astype(BF16), v[h].astype(BF16), preferred_element_type=F32))
    return outs


def _xat_load(q_ref, kv_ref):
    q = [q_ref[:, h * XHD:(h + 1) * XHD].astype(F32) for h in range(XH)]
    k = [kv_ref[:, h * XHD:(h + 1) * XHD].astype(F32) for h in range(XH)]
    v = [kv_ref[:, XW + h * XHD:XW + (h + 1) * XHD].astype(F32) for h in range(XH)]
    return q, k, v


def _xat_fwd(name, q, kv):
    T, M = q.shape[0], kv.shape[0]
    tt = _tile(T, 512)

    def body(q_ref, kv_ref, o_ref):
        outs = _xat_math(*_xat_load(q_ref, kv_ref))
        for h in range(XH):
            o_ref[:, h * XHD:(h + 1) * XHD] = outs[h].astype(BF16)

    return pl.pallas_call(
        body, grid=(T // tt,),
        in_specs=[pl.BlockSpec((tt, XW), lambda i: (i, 0)), pl.BlockSpec((M, 2 * XW), lambda i: (0, 0))],
        out_specs=pl.BlockSpec((tt, XW), lambda i: (i, 0)), out_shape=S((T, XW), BF16),
        compiler_params=_params("parallel"), name=name)(q, kv)


def _xat_bwd(name, q, kv, do):
    T, M = q.shape[0], kv.shape[0]
    tt = _tile(T, 512)

    def body(q_ref, kv_ref, do_ref, dq_ref, dkv_ref):
        first = pl.program_id(0) == 0
        _, vjp = jax.vjp(_xat_math, *_xat_load(q_ref, kv_ref))
        dq, dk, dv = vjp([do_ref[:, h * XHD:(h + 1) * XHD].astype(F32) for h in range(XH)])
        for h in range(XH):
            sl = slice(h * XHD, (h + 1) * XHD)
            dq_ref[:, sl] = dq[h].astype(BF16)
            _acc_store(first, dkv_ref, (slice(None), sl), dk[h])
            _acc_store(first, dkv_ref, (slice(None), slice(XW + h * XHD, XW + (h + 1) * XHD)), dv[h])

    qs = pl.BlockSpec((tt, XW), lambda i: (i, 0))
    kvs = pl.BlockSpec((M, 2 * XW), lambda i: (0, 0))
    return pl.pallas_call(
        body, grid=(T // tt,), in_specs=[qs, kvs, qs], out_specs=[qs, kvs],
        out_shape=[S((T, XW), BF16), S((M, 2 * XW), F32)],
        compiler_params=_params("arbitrary"), name=name)(q, kv, do)


def _loss_head(name, x, gain, target):
    T = x.shape[0]
    tt = _tile(T, 512)

    def body(x_ref, g_ref, t_ref, l_ref, dx_ref, dxb_ref, dg_ref):
        first = pl.program_id(0) == 0
        xv, g = x_ref[...], g_ref[...]
        r = lax.rsqrt(jnp.mean(xv * xv, -1, keepdims=True) + EPS)
        xh = xv * r
        e = xh * g - t_ref[...]
        part = 0.5 * jnp.sum(jnp.mean(e * e, -1, keepdims=True), (0, 1), keepdims=True)
        _acc_store(first, l_ref, (slice(None), slice(None)), jnp.broadcast_to(part, (1, LANE)))
        dy = e * (1.0 / D)
        dxh = dy * g
        dx = r * (dxh - xh * jnp.mean(dxh * xh, -1, keepdims=True))
        dx_ref[...] = dx
        dxb_ref[...] = dx.astype(BF16)
        _acc_store(first, dg_ref, (slice(None), slice(None)), jnp.sum(dy * xh, 0, keepdims=True))

    row = pl.BlockSpec((tt, D), lambda i: (i, 0))
    vec = pl.BlockSpec((1, D), lambda i: (0, 0))
    return pl.pallas_call(
        body, grid=(T // tt,), in_specs=[row, vec, row],
        out_specs=[pl.BlockSpec((1, LANE), lambda i: (0, 0)), row, row, vec],
        out_shape=[S((1, LANE), F32), S((T, D), F32), S((T, D), BF16), S((1, D), F32)],
        compiler_params=_params("arbitrary"), name=name)(x, gain, target)


def _out_proj(name, a, b, x, next_gain, scale=1.0, tk_t=2048, plain=(1024, 1024)):
    if next_gain is None:
        return _mm(name, "nn", a, b, F32, res=x, scale=scale, tm_t=plain[0], tn_t=plain[1], tk_t=tk_t), None
    return _mm(name, "nn", a, b, F32, res=x, scale=scale, norm_gain=next_gain, tm_t=512, tn_t=D, tk_t=min(tk_t, 2048))


def _ffn_fwd(tag, x, h, gain, wgu4, get_wd, next_gain):
    g, u, a = _ffn_up(f"{tag}_up", h, wgu4, 0)
    wd = get_wd(a).reshape(1, 1, DFF, D)
    x_new, _ = _out_proj(f"{tag}_down", a, Op(wd, "r"), x, None, 0.5, 2816)
    h_next = None if next_gain is None else _rms_fwd(f"{tag}_nextnorm", x_new, next_gain)
    return x_new, h_next, (x, gain, h, g, u, a)


def _ffn_bwd(tag, saved, dx, dxb, wgu4, wd4, put):
    x, gain, h, g, u, a = saved
    dgu = _ffn_dact(f"{tag}_dact", dxb, wd4, 0, g, u, 1024)
    dwd = _mm(f"{tag}_dwd", "tn", a, dxb, BF16, out=("r", 4, 1, 0), scale=0.5, tm_t=1408, tn_t=1024, tk_t=2048)
    dwgu = _mm(f"{tag}_dwgu", "tn", h, Op(dgu, "c"), BF16, out=("c", 4, 1, 0), tm_t=1024, tn_t=256, tk_t=8192)
    tok = put(dwgu, dwd)
    dh = _mm(f"{tag}_dh", "nt", Op(dgu, "c"), Op(wgu4, "c"), BF16, bias=jnp.zeros((1, D), F32) + tok, tk_t=2816)
    dx, dxb, dgain = _rms_bwd(f"{tag}_dnorm", x, gain, dh, dx)
    return dx, dxb, dgain


def _xattn_fwd(tag, x, hq, mem, gq, gm, wxq4, wxkv4, wxo4, next_gain):
    mn = _rms_fwd(f"{tag}_normm", mem, gm)
    q = _mm(f"{tag}_q", "nn", hq, Op(wxq4, "r"), BF16)
    kv = _mm(f"{tag}_kv", "nn", mn, Op(wxkv4, "r"), BF16)
    o = _xat_fwd(f"{tag}_att", q, kv)
    wxo = jnp.transpose(wxo4[:, 0], (1, 0, 2)).reshape(XW, D)
    x_new, h_next = _out_proj(f"{tag}_o", o, wxo, x, next_gain)
    return x_new, h_next, (x, mem, gq, gm, hq, mn, q, kv, o)


def _xattn_bwd(tag, saved, dx, dxb, wxq4, wxkv4, wxo4, put):
    x, mem, gq, gm, hq, mn, q, kv, o = saved
    dwxo = _mm(f"{tag}_dwo", "tn", o, dxb, BF16, out=("c", 4, 1, 0))
    do = _mm(f"{tag}_do", "nt", dxb, Op(wxo4, "c"), BF16)
    dq, dkv = _xat_bwd(f"{tag}_datt", q, kv, do)
    dwxq = _mm(f"{tag}_dwq", "tn", hq, dq, BF16, out=("r", 4, 1, 0))
    dwxkv = _mm(f"{tag}_dwkv", "tn", mn, dkv, BF16, out=("r", 4, 1, 0))
    tok = put(dwxq, dwxkv, dwxo)
    dhq = _mm(f"{tag}_dhq", "nt", dq, Op(wxq4, "r"), BF16, bias=jnp.zeros((1, D), F32) + tok)
    dmn = _mm(f"{tag}_dmn", "nt", dkv, Op(wxkv4, "r"), BF16)
    _, _, dgm = _rms_bwd(f"{tag}_dnormm", mem, gm, dmn)
    dx, dxb, dgq = _rms_bwd(f"{tag}_dnormq", x, gq, dhq, dx)
    return dx, dxb, dgq, dgm


def _even_fwd(tag, x, h, gain, w_main, w_dt, p, wout4, next_gain):
    proj = _mm(f"{tag}_in", "nn", h, w_main, BF16)
    dtr = _mm(f"{tag}_indt", "nn", h, w_dt, F32)
    mix = _gmlp_fwd(f"{tag}_gmlp", proj, p["lng"], p["lnb"], p["ws"], p["bs"])
    ypre, xc = _conv_fwd(f"{tag}_conv", proj, p["cw"], p["cb"])
    mix, prevs = _ssd_fwd(f"{tag}_ssd", xc, dtr, proj, p["dtb"], p["alog"], p["dsk"], p["nrm"], mix)
    x_new, h_next = _out_proj(f"{tag}_out", mix, Op(wout4.reshape(1, 1, 2 * D, D), "r"), x, next_gain)
    return x_new, h_next, (x, gain, h, proj, dtr, mix, ypre, xc, prevs)


def _even_bwd(tag, saved, dx, dxb, w_main, w_dt, p, wout4, put):
    x, gain, h, proj, dtr, mix, ypre, xc, prevs = saved
    T = x.shape[0]
    dwout = _mm(f"{tag}_dwout", "tn", mix, dxb, BF16, out=("r", 4, 1, 0))
    dmix = _mm(f"{tag}_dmix", "nt", dxb, Op(wout4, "r", 0), BF16)
    dproj = lax.empty((T, EVEN_MAIN), BF16)
    dproj, dlng, dlnb, dws, dbs = _gmlp_bwd(f"{tag}_dgmlp", proj, p["lng"], p["lnb"], p["ws"], p["bs"], dmix, dproj)
    dproj, dxc, ddtr, ddtb, dalog, ddsk, dnrm = _ssd_bwd(
        f"{tag}_dssd", xc, dtr, proj, prevs, p["dtb"], p["alog"], p["dsk"], p["nrm"], dmix, dproj)
    dproj, dcw, dcb = _conv_bwd(f"{tag}_dconv", proj, ypre, dxc, p["cw"], dproj)
    dw_main = _mm(f"{tag}_dwin", "tn", h, dproj, BF16, tm_t=1024, tn_t=256, tk_t=8192)
    dw_dt = _mm(f"{tag}_dwdt", "tn", h, ddtr, BF16)
    tok = put(dw_main, dw_dt, dwout)
    dh = _mm(f"{tag}_dh1", "nt", ddtr, w_dt + tok.astype(BF16), F32)
    dh = _mm(f"{tag}_dh2", "nt", dproj, w_main, BF16, res=dh)
    dx, dxb, dgain = _rms_bwd(f"{tag}_dnorm", x, gain, dh, dx)
    small = dict(lng=dlng, lnb=dlnb, ws=dws, bs=dbs, cw=dcw, cb=dcb, dtb=ddtb, alog=dalog, dsk=ddsk, nrm=dnrm)
    return dx, dxb, dgain, small


def _odd_fwd(tag, x, h, gain, wqkv4, bqkv, snk, wo4, cos, sin, next_gain):
    qkv = _mm(f"{tag}_qkv", "nn", h, Op(wqkv4, "c", 0), F32, bias=bqkv, tn_t=640)
    qkvr = _rope_fwd(f"{tag}_rope", qkv, cos, sin)
    o = _swa_fwd(f"{tag}_swa", qkvr, snk)
    x_new, h_next = _out_proj(f"{tag}_o", o, Op(wo4.reshape(1, 1, D, D), "r"), x, next_gain)
    return x_new, h_next, (x, gain, h, qkvr, o)


def _odd_bwd(tag, saved, dx, dxb, wqkv4, snk, wo4, cos, sin, put):
    x, gain, h, qkvr, o = saved
    dwo = _mm(f"{tag}_dwo", "tn", o, dxb, BF16, out=("r", 4, 1, 0))
    do = _mm(f"{tag}_do", "nt", dxb, Op(wo4, "r", 0), BF16)
    dq, dcur, dprv, dsnk = _swa_bwd(f"{tag}_dswa", qkvr, snk, do)
    dqkv, dbias = _rope_bwd(f"{tag}_drope", dq, dcur, dprv, cos, sin)
    dwqkv = _mm(f"{tag}_dwqkv", "tn", h, dqkv, BF16, out=("c", 4, 1, 0), tn_t=640)
    tok = put(dwqkv, dwo)
    dh = _mm(f"{tag}_dh", "nt", dqkv, Op(wqkv4, "c", 0), BF16, bias=jnp.zeros((1, D), F32) + tok, tk_t=640)
    dx, dxb, dgain = _rms_bwd(f"{tag}_dnorm", x, gain, dh, dx)
    return dx, dxb, dgain, dbias, dsnk


def _row(v):
    return v.reshape(1, -1).astype(F32)


def _pad_lanes(v, n=LANE):
    v = v.reshape(1, -1).astype(F32)
    return jnp.pad(v, ((0, 0), (0, n - v.shape[1])))


def _local_step(x, mem, positions, target, getw, P, putg):
    inv_freq = ROPE_THETA ** (-jnp.arange(0, ROT, 2, dtype=F32) / ROT)
    ang = positions.astype(F32)[:, None] * inv_freq
    cos8, sin8 = jnp.cos(ang), jnp.sin(ang)
    ones, zeros = jnp.ones((x.shape[0], AHD - ROT), F32), jnp.zeros((x.shape[0], AHD - ROT), F32)
    cos = jnp.tile(jnp.concatenate([cos8, cos8, ones], 1), (1, 2))
    sin = jnp.tile(jnp.concatenate([-sin8, sin8, zeros], 1), (1, 2))

    snk = _pad_lanes(P["sinks"])
    W = {}

    def w(name, layer, after):
        if (name, layer) not in W:
            W[name, layer] = getw(name, layer, after)
        return W[name, layer]

    saved = []
    h = _rms_fwd("l0_ffn1_norm", x, _row(P["norm_ffn1"][0]))
    for i in range(2):
        x, h, s1 = _ffn_fwd(f"l{i}_ffn1", x, h, _row(P["norm_ffn1"][i]), w("w_ffn1_gu", i, x),
                            functools.partial(w, "w_ffn1_down", i), _row(P["norm_mix"][i]))
        if i == 0:
            ev = dict(lng=_row(P["gm_ln_g"]), lnb=_row(P["gm_ln_b"]), ws=P["gm_ws"].reshape(GM_G, CH, CH),
                      bs=P["gm_bs"].reshape(GM_G, CH, 1), cw=w("conv_w", 0, x), cb=_row(P["conv_b"]),
                      dtb=_pad_lanes(P["dt_bias"]), alog=_pad_lanes(P["a_log"]), dsk=_pad_lanes(P["d_skip"]),
                      nrm=_row(P["ssd_norm"]))
            w_main, w_dt = w("w_in_even", 0, x)
            x, h, s2 = _even_fwd("l0_mix", x, h, _row(P["norm_mix"][0]), w_main, w_dt, ev, w("w_out_even", 0, x),
                                 _row(P["norm_xq"][0]))
        else:
            x, h, s2 = _odd_fwd("l1_mix", x, h, _row(P["norm_mix"][1]), w("w_qkv", 0, x), w("b_qkv", 0, x), snk,
                                w("w_o_odd", 0, x), cos, sin, _row(P["norm_xq"][1]))
        x, h, s3 = _xattn_fwd(f"l{i}_xat", x, h, mem, _row(P["norm_xq"][i]), _row(P["norm_mem"][i]),
                              w("w_xq", i, x), w("w_xkv", i, x), w("w_xo", i, x), _row(P["norm_ffn2"][i]))
        x, h, s4 = _ffn_fwd(f"l{i}_ffn2", x, h, _row(P["norm_ffn2"][i]), w("w_ffn2_gu", i, x),
                            functools.partial(w, "w_ffn2_down", i), _row(P["norm_ffn1"][1]) if i == 0 else None)
        saved.append((s1, s2, s3, s4))

    loss, dx, dxb, d_final = _loss_head("loss_head", x, _row(P["final_norm"]), target)

    sm = {}
    dn = {k: [None, None] for k in ("norm_ffn1", "norm_mix", "norm_xq", "norm_mem", "norm_ffn2")}
    for i in (1, 0):
        s1, s2, s3, s4 = saved[i]
        dx, dxb, dn["norm_ffn2"][i] = _ffn_bwd(
            f"l{i}_ffn2", s4, dx, dxb, W["w_ffn2_gu", i], W["w_ffn2_down", i],
            lambda dwgu, dwd, i=i: putg({("w_ffn2_gu", i): dwgu, ("w_ffn2_down", i): dwd}))
        dx, dxb, dn["norm_xq"][i], dn["norm_mem"][i] = _xattn_bwd(
            f"l{i}_xat", s3, dx, dxb, W["w_xq", i], W["w_xkv", i], W["w_xo", i],
            lambda dwxq, dwxkv, dwxo, i=i: putg({("w_xq", i): dwxq, ("w_xkv", i): dwxkv, ("w_xo", i): dwxo}))
        if i == 0:
            dx, dxb, dn["norm_mix"][0], sm_even = _even_bwd(
                "l0_mix", s2, dx, dxb, w_main, w_dt, ev, W["w_out_even", 0],
                lambda dw_main, dw_dt, dwout: putg({("w_in_even", 0): (dw_main, dw_dt), ("w_out_even", 0): dwout}))
        else:
            dx, dxb, dn["norm_mix"][1], sm["b_qkv"], sm["sinks"] = _odd_bwd(
                "l1_mix", s2, dx, dxb, W["w_qkv", 0], snk, W["w_o_odd", 0], cos, sin,
                lambda dwqkv, dwo: putg({("w_qkv", 0): dwqkv, ("w_o_odd", 0): dwo}))
        dx, dxb, dn["norm_ffn1"][i] = _ffn_bwd(
            f"l{i}_ffn1", s1, dx, dxb, W["w_ffn1_gu", i], W["w_ffn1_down", i],
            lambda dwgu, dwd, i=i: putg({("w_ffn1_gu", i): dwgu, ("w_ffn1_down", i): dwd}))
    for k, v in dn.items():
        sm[k] = jnp.concatenate(v, 0)
    sm.update(gm_ln_g=sm_even["lng"], gm_ln_b=sm_even["lnb"], gm_ws=sm_even["ws"], gm_bs=sm_even["bs"],
              conv_w=sm_even["cw"], conv_b=sm_even["cb"], dt_bias=sm_even["dtb"][:, :NH], a_log=sm_even["alog"][:, :NH],
              d_skip=sm_even["dsk"][:, :NH], ssd_norm=sm_even["nrm"], sinks=sm["sinks"][:, :AH], final_norm=d_final)
    return loss[0, 0], dx, sm


def _chip_peers():
    x, y, c = lax.axis_index("x"), lax.axis_index("y"), lax.axis_index("c")
    return 2 * x + y, [((1 - x, y, c), 2 * (1 - x) + y), ((x, 1 - y, c), 2 * x + (1 - y)),
                       ((1 - x, 1 - y, c), 2 * (1 - x) + (1 - y))]


def _any_specs(n):
    return [pl.BlockSpec(memory_space=pl.ANY)] * n


_HBM = pl.BlockSpec(memory_space=pltpu.HBM)
_SEM = pl.BlockSpec(memory_space=pltpu.SEMAPHORE)
_EFFECT = pltpu.SideEffectType.DATAFLOW_SIDE_EFFECTING


def _own_slot(piece, chip):
    zone = lax.empty((4,) + piece.shape, piece.dtype)
    return lax.dynamic_update_slice(zone, piece[None], (chip,) + (0,) * piece.ndim)


def _chip_copies(srcs, lands, ssems, rsems, mode="chips"):
    c = lax.axis_index("c")
    sib = (lax.axis_index("x"), lax.axis_index("y"), 1 - c)
    if mode == "sibling":
        return [pltpu.make_async_remote_copy(src_ref=srcs[i], dst_ref=lands[i], send_sem=ssems[i].at[0],
                                             recv_sem=rsems[i].at[0], device_id=sib, device_id_type=MESH)
                for i in range(len(lands))]
    me, peers = _chip_peers()
    if mode == "half":
        return [pltpu.make_async_remote_copy(src_ref=lands[i].at[me, c], dst_ref=lands[i].at[me, c], send_sem=ssems[i].at[j],
                                             recv_sem=rsems[i].at[j], device_id=dev, device_id_type=MESH)
                for i in range(len(lands)) for j, (dev, _) in enumerate(peers)]
    if mode == "pass_on":
        return [pltpu.make_async_remote_copy(src_ref=lands[i].at[chip, c], dst_ref=lands[i].at[chip, c],
                                             send_sem=ssems[i].at[j], recv_sem=rsems[i].at[j], device_id=sib, device_id_type=MESH)
                for i in range(len(lands)) for j, (_, chip) in enumerate(peers)]
    return [pltpu.make_async_remote_copy(src_ref=lands[i].at[me] if srcs[i] is None else srcs[i].at[chip],
                                         dst_ref=lands[i].at[me], send_sem=ssems[i].at[j], recv_sem=rsems[i].at[j],
                                         device_id=dev, device_id_type=MESH)
            for i in range(len(lands)) for j, (dev, chip) in enumerate(peers)]


def _exchange_start(name, srcs, lands, mode="chips"):
    n = len(lands)
    ns = 0 if srcs is None else n

    def body(*refs):
        src_refs = [None] * n if srcs is None else refs[:n]
        land_refs = refs[ns:ns + n]
        ssems, rsems = refs[ns + n:ns + 2 * n], refs[ns + 2 * n:ns + 3 * n]
        token = refs[2 * ns + 4 * n]
        for cp in _chip_copies(src_refs, land_refs, ssems, rsems, mode):
            cp.start()
        token[...] = jnp.zeros_like(token)

    ins = ([] if srcs is None else list(srcs)) + list(lands)
    res = pl.pallas_call(
        body, name=name,
        out_shape=[pltpu.SemaphoreType.DMA((1 if mode == "sibling" else 3,))] * (2 * n) + [pltpu.HBM(a.shape, a.dtype) for a in ins]
        + [S((8, LANE), F32)],
        in_specs=[_HBM] * (ns + n),
        out_specs=[_SEM] * (2 * n) + [_HBM] * (ns + n) + [pl.BlockSpec(memory_space=pltpu.VMEM)],
        input_output_aliases={i: 2 * n + i for i in range(ns + n)},
        compiler_params=pltpu.CompilerParams(has_side_effects=_EFFECT),
    )(*[pltpu.with_memory_space_constraint(a, pltpu.HBM) for a in ins])
    items = [(res[i], res[n + i], None if srcs is None else res[2 * n + i], res[2 * n + ns + i]) for i in range(n)]
    return items, res[2 * n + ns + n][0, 0]


def _exchange_wait(name, item, after, mode="chips"):
    ssem, rsem, src, land = item
    ns = 0 if src is None else 1

    def body(*refs):
        src_ref = refs[0] if ns else None
        land_ref, ssem_ref, rsem_ref = refs[ns], refs[ns + 1], refs[ns + 2]
        for cp in _chip_copies([src_ref], [land_ref], [ssem_ref], [rsem_ref], mode):
            cp.wait_send()
            cp.wait_recv()

    ins = ([src] if ns else []) + [land]
    return pl.pallas_call(
        body, name=name, out_shape=[pltpu.HBM(a.shape, a.dtype) for a in ins],
        in_specs=[_HBM] * (ns + 1) + [_SEM, _SEM, pl.BlockSpec(memory_space=pl.ANY)], out_specs=[_HBM] * (ns + 1),
        input_output_aliases={i: i for i in range(ns + 1)}, compiler_params=pltpu.CompilerParams(has_side_effects=_EFFECT),
    )(*ins, ssem, rsem, after)[ns]


def _gather_all(name, v, after):
    def body(v_ref, _, o_ref, ssem, rsem, lsem):
        x, y, c = lax.axis_index("x"), lax.axis_index("y"), lax.axis_index("c")
        me = 4 * x + 2 * y + c
        loc = pltpu.make_async_copy(v_ref, o_ref.at[me], lsem)
        loc.start()
        copies = []
        for k in range(1, 8):
            fx, fy, fc = (k >> 2) & 1, (k >> 1) & 1, k & 1
            dev = (x ^ fx, y ^ fy, c ^ fc)
            cp = pltpu.make_async_remote_copy(src_ref=v_ref, dst_ref=o_ref.at[me], send_sem=ssem.at[k - 1],
                                              recv_sem=rsem.at[k - 1], device_id=dev, device_id_type=MESH)
            cp.start()
            copies.append(cp)
        for cp in copies:
            cp.wait()
        loc.wait()

    return pl.pallas_call(
        body, in_specs=_any_specs(2), out_specs=pl.BlockSpec(memory_space=pl.ANY), out_shape=S((8,) + v.shape, v.dtype),
        scratch_shapes=[pltpu.SemaphoreType.DMA((7,)), pltpu.SemaphoreType.DMA((7,)), pltpu.SemaphoreType.DMA(())],
        compiler_params=pltpu.CompilerParams(has_side_effects=True), name=name)(v, after)


def _row_tile(R, row_bytes, budget=4 << 20):
    if R * row_bytes <= budget or R % 16:
        return R
    t = max(16, budget // row_bytes // 16 * 16)
    while R % t:
        t -= 16
    return t


def _sum_slots(name, r, n):
    _, R, C = r.shape
    tr = _row_tile(R, C * (n * r.dtype.itemsize + 4))

    def body(r_ref, o_ref):
        acc = r_ref[0].astype(F32)
        for j in range(1, n):
            acc = acc + r_ref[j].astype(F32)
        o_ref[...] = acc

    return pl.pallas_call(
        body, grid=(R // tr,), in_specs=[pl.BlockSpec((n, tr, C), lambda i: (0, i, 0))],
        out_specs=pl.BlockSpec((tr, C), lambda i: (i, 0)), out_shape=S((R, C), F32),
        compiler_params=_params("parallel"), name=name)(r)


def _adamw(name, w, m, v, layer, g1, g2=None, into=None):
    nl, R, C = w.shape
    tr = _row_tile(R, C * 4 * 9)
    two, has_into = g2 is not None, into is not None

    def body(w_ref, m_ref, v_ref, g1_ref, *rest):
        rest = list(rest)
        g = g1_ref[...]
        if two:
            g = g + rest.pop(0)[...]
        g_ref, d_ref, nm_ref, nv_ref = rest[-4:]
        mn = B1 * m_ref[...] + (1.0 - B1) * g
        vn = B2 * v_ref[...] + (1.0 - B2) * jnp.square(g)
        m_hat = mn / (1.0 - B1 ** STEP)
        v_hat = vn / (1.0 - B2 ** STEP)
        g_ref[...] = g
        d_ref[...] = -LR * (m_hat / (jnp.sqrt(v_hat) + AEPS) + WD * w_ref[...])
        nm_ref[...] = mn
        nv_ref[...] = vn

    blk = pl.BlockSpec((tr, C), lambda i: (i, 0))
    lay = pl.BlockSpec((None, tr, C), lambda i: (layer, i, 0))
    args = [w, m, v, g1] + ([g2] if two else []) + (list(into) if has_into else [])
    in_specs = [lay] * 3 + [blk] * (2 if two else 1) + (_any_specs(4) if has_into else [])
    aliases = {len(args) - 4 + t: t for t in range(4)} if has_into else {}
    return pl.pallas_call(
        body, grid=(R // tr,), in_specs=in_specs, out_specs=[lay] * 4, out_shape=[S((nl, R, C), F32)] * 4,
        input_output_aliases=aliases, compiler_params=_params("parallel"), name=name)(*args)


_USE_ORDER = [("w_ffn1_gu", 0), ("w_ffn1_down", 0), ("conv_w", 0), ("w_in_even", 0), ("w_out_even", 0), ("w_xq", 0),
              ("w_xkv", 0), ("w_xo", 0), ("w_ffn2_gu", 0), ("w_ffn2_down", 0), ("w_ffn1_gu", 1), ("w_ffn1_down", 1),
              ("w_qkv", 0), ("b_qkv", 0), ("w_o_odd", 0), ("w_xq", 1), ("w_xkv", 1), ("w_xo", 1), ("w_ffn2_gu", 1),
              ("w_ffn2_down", 1)]
_SMALL = ["norm_ffn1", "norm_mix", "gm_ln_g", "gm_ln_b", "gm_ws", "gm_bs", "conv_w", "conv_b", "dt_bias", "a_log",
          "d_skip", "ssd_norm", "b_qkv", "sinks", "norm_xq", "norm_mem", "norm_ffn2", "final_norm"]
_WEIGHTS = ["norm_ffn1", "w_ffn1_gu", "w_ffn1_down", "norm_mix", "w_in_even", "gm_ln_g", "gm_ln_b", "gm_ws", "gm_bs",
            "conv_w", "conv_b", "dt_bias", "a_log", "d_skip", "ssd_norm", "w_out_even", "w_qkv", "b_qkv", "sinks",
            "w_o_odd", "norm_xq", "norm_mem", "w_xq", "w_xkv", "w_xo", "norm_ffn2", "w_ffn2_gu", "w_ffn2_down",
            "final_norm"]


def _pack(arrs):
    rows = []
    for a in arrs:
        f = a.reshape(-1).astype(F32)
        pad = (-f.shape[0]) % LANE
        rows.append(jnp.pad(f, (0, pad)).reshape(-1, LANE))
    out = jnp.concatenate(rows, 0)
    pad = (-out.shape[0]) % 8
    return jnp.pad(out, ((0, pad), (0, 0)))


def _unpack(packed, shapes):
    outs, r = [], 0
    for shp in shapes:
        n = math.prod(shp)
        nr = -(-n // LANE)
        outs.append(packed[r:r + nr].reshape(-1)[:n].reshape(shp))
        r += nr
    return outs


def kernel(x, mem, positions, norm_ffn1, w_ffn1_gu, w_ffn1_down, norm_mix, w_in_even, gm_ln_g, gm_ln_b, gm_ws, gm_bs, conv_w, conv_b, dt_bias, a_log, d_skip, ssd_norm, w_out_even, w_qkv, b_qkv, sinks, w_o_odd, norm_xq, norm_mem, w_xq, w_xkv, w_xo, norm_ffn2, w_ffn2_gu, w_ffn2_down, final_norm, loss_target, m_norm_ffn1, m_w_ffn1_gu, m_w_ffn1_down, m_norm_mix, m_w_in_even, m_gm_ln_g, m_gm_ln_b, m_gm_ws, m_gm_bs, m_conv_w, m_conv_b, m_dt_bias, m_a_log, m_d_skip, m_ssd_norm, m_w_out_even, m_w_qkv, m_b_qkv, m_sinks, m_w_o_odd, m_norm_xq, m_norm_mem, m_w_xq, m_w_xkv, m_w_xo, m_norm_ffn2, m_w_ffn2_gu, m_w_ffn2_down, m_final_norm, v_norm_ffn1, v_w_ffn1_gu, v_w_ffn1_down, v_norm_mix, v_w_in_even, v_gm_ln_g, v_gm_ln_b, v_gm_ws, v_gm_bs, v_conv_w, v_conv_b, v_dt_bias, v_a_log, v_d_skip, v_ssd_norm, v_w_out_even, v_w_qkv, v_b_qkv, v_sinks, v_w_o_odd, v_norm_xq, v_norm_mem, v_w_xq, v_w_xkv, v_w_xo, v_norm_ffn2, v_w_ffn2_gu, v_w_ffn2_down, v_final_norm):
    a = dict(locals())
    w = {k: a[k] for k in _WEIGHTS}
    mom = {k: a["m_" + k] for k in _WEIGHTS}
    var = {k: a["v_" + k] for k in _WEIGHTS}
    chip = 2 * lax.axis_index("x") + lax.axis_index("y")

    shards = [w[k][i:i + 1] if k in ("conv_w", "b_qkv") else w[k][i:i + 1].astype(BF16) for k, i in _USE_ORDER]
    half = shards[0].reshape((2, shards[0].shape[1] // 2) + shards[0].shape[2:])
    (first,), first_started = _exchange_start("gather_start_first", None, [_own_slot(half, chip)], "half")
    small = _USE_ORDER.index(("conv_w", 0))
    shards[small] = shards[small] + first_started
    rest, rest_started = _exchange_start("gather_start_rest", None, [_own_slot(s, chip) for s in shards[1:]])
    pending = dict(zip(_USE_ORDER[1:], rest))

    def getw(name, layer, after):
        if (name, layer) == _USE_ORDER[0]:
            zone = _exchange_wait("gather_wait_first_half", first, after, "half")
            (passed,), _ = _exchange_start("gather_pass_on_first", None, [zone], "pass_on")
            return _exchange_wait("gather_wait_first_passed", passed, after, "pass_on").reshape((4,) + shards[0].shape)
        got = _exchange_wait(f"gather_wait_{name}_{layer}", pending.pop((name, layer)), after)
        if name == "w_in_even":
            w_in = jnp.transpose(got[:, 0], (1, 0, 2)).reshape(D, EVEN_IN)
            return w_in[:, :EVEN_MAIN], jnp.pad(w_in[:, EVEN_MAIN:], ((0, 0), (0, LANE - (EVEN_IN - EVEN_MAIN))))
        if name == "conv_w":
            return jnp.transpose(got[:, 0], (1, 0, 2)).reshape(4, CONV_C)
        if name == "b_qkv":
            return got.reshape(1, ODD_IN)
        return got

    sent = []

    def putg(grads):
        names, arrs = [], []
        for (name, layer), g in grads.items():
            if name == "w_in_even":
                dw_in = jnp.concatenate([g[0], g[1][:, :EVEN_IN - EVEN_MAIN]], 1)
                g = jnp.transpose(dw_in.reshape(D, 4, EVEN_IN // 4), (1, 0, 2)).reshape(4, 1, D, EVEN_IN // 4)
            names.append((name, layer))
            arrs.append(g)
        own = [_own_slot(lax.dynamic_index_in_dim(g, chip, 0, keepdims=False), chip) for g in arrs]
        its, tok = _exchange_start(f"scatter_start_{names[0][0]}_{names[0][1]}", arrs, own)
        sent.append(list(zip(names, its)))
        return tok

    P = {k: w[k] for k in _SMALL}
    P["norm_ffn1"] = P["norm_ffn1"] + rest_started
    loss, grad_x, sm = _local_step(x[0], mem[0], positions[0], loss_target[0], getw, P, putg)
    loss = lax.psum(loss, ("x", "y", "c"))

    out = {}

    def update(groups, after):
        flying = []
        for grp in groups:
            part = []
            for (name, layer), it in grp:
                r = _exchange_wait(f"scatter_wait_{name}_{layer}", it, after)
                part.append(_sum_slots(f"sum_{name}_{layer}", r.reshape(4, -1, r.shape[-1]), 4))
            name0, layer0 = grp[0][0]
            its, _ = _exchange_start(f"swap_start_{name0}_{layer0}", part, [lax.empty(p.shape, p.dtype) for p in part], "sibling")
            flying += [(nm, p, it) for (nm, _), p, it in zip(grp, part, its)]
        for (name, layer), p1, it in flying:
            p2 = _exchange_wait(f"swap_wait_{name}_{layer}", it, after, "sibling")
            out[name] = _adamw(f"adamw_{name}_{layer}", w[name], mom[name], var[name], layer, p1, p2, out.get(name))
            after = out[name][0]

    update(sent[:-1], grad_x)
    done_a = out["w_out_even"][0]
    update(sent[-1:], done_a)

    full_shapes = {k: w[k].shape for k in _SMALL}
    full_shapes["conv_w"], full_shapes["b_qkv"] = (1, 4, CONV_C), (1, ODD_IN)
    packed = _pack([sm[k] for k in _SMALL])
    total = _sum_slots("sum_small", _gather_all("gather_small", packed, done_a), 8)
    gs = dict(zip(_SMALL, _unpack(total, [full_shapes[k] for k in _SMALL])))
    gs["conv_w"] = lax.dynamic_slice_in_dim(gs["conv_w"], chip * (CONV_C // 4), CONV_C // 4, 2)
    gs["b_qkv"] = lax.dynamic_slice_in_dim(gs["b_qkv"], chip * (ODD_IN // 4), ODD_IN // 4, 1)
    res = _adamw("adamw_small", _pack([w[k] for k in _SMALL])[None], _pack([mom[k] for k in _SMALL])[None],
                 _pack([var[k] for k in _SMALL])[None], 0, _pack([gs[k] for k in _SMALL]))
    shapes = [w[k].shape for k in _SMALL]
    for k, g, d, nm, nv in zip(_SMALL, *[_unpack(r[0], shapes) for r in res]):
        out[k] = [g, d, nm, nv]

    return (loss, grad_x[None], *[out[k][0] for k in _WEIGHTS], *[out[k][1] for k in _WEIGHTS],
            *[out[k][2] for k in _WEIGHTS], *[out[k][3] for k in _WEIGHTS])
```

```python
import functools
import math

import jax
import jax.numpy as jnp
from jax import lax
from jax.experimental import pallas as pl
from jax.experimental.pallas import tpu as pltpu

F32, BF16 = jnp.float32, jnp.bfloat16
S = jax.ShapeDtypeStruct
MESH = pl.DeviceIdType.MESH

D = 2048
DFF = 5632
EPS = 1e-5
CH = 128
GM_G, GM_GD = 4, 512
NH, HD, NG, HPG, NS = 32, 64, 4, 8, 128
CONV_C = 3072
EVEN_MAIN, EVEN_IN = 9216, 9248
AH, AKV, AREP, AHD = 32, 4, 8, 64
ODD_IN = 2560
XH, XHD, XW = 4, 128, 512
ATT_SCALE = AHD ** -0.5
X_SCALE = XHD ** -0.5
ROPE_THETA = 500000.0
ROT = 16
LR, B1, B2, AEPS, WD, STEP = 0.001, 0.9, 0.999, 1e-08, 0.01, 10
LANE = 128
VMEM_LIMIT_V7X = 56 * 1024 * 1024


def _params(*sem):
    return pltpu.CompilerParams(dimension_semantics=sem, vmem_limit_bytes=VMEM_LIMIT_V7X)


def _tile(dim, target):
    if dim <= target:
        return dim
    t = (target // LANE) * LANE
    while t > LANE and dim % t:
        t -= LANE
    assert dim % t == 0, (dim, target)
    return t


class Op:
    def __init__(self, arr, kind=None, layer=0):
        self.arr, self.kind, self.layer = arr, kind, layer
        if kind is None:
            self.R, self.C = arr.shape
        else:
            L = arr.shape[0]
            self.R = arr.shape[2] * (L if kind == "r" else 1)
            self.C = arr.shape[3] * (L if kind == "c" else 1)

    def unit(self, axis):
        if self.kind == "r" and axis == 0:
            return self.arr.shape[2]
        if self.kind == "c" and axis == 1:
            return self.arr.shape[3]
        return (self.R, self.C)[axis]

    def spec(self, tr, tc, pick):
        if self.kind is None:
            return pl.BlockSpec((tr, tc), lambda i, j, k: pick(i, j, k))
        l = self.layer
        if self.kind == "c":
            per = self.arr.shape[3] // tc
            return pl.BlockSpec((None, None, tr, tc),
                                lambda i, j, k: (pick(i, j, k)[1] // per, l, pick(i, j, k)[0], pick(i, j, k)[1] % per))
        per = self.arr.shape[2] // tr
        return pl.BlockSpec((None, None, tr, tc),
                            lambda i, j, k: (pick(i, j, k)[0] // per, l, pick(i, j, k)[0] % per, pick(i, j, k)[1]))


_DIMS = {"nn": (((1,), (0,)), ((), ())), "nt": (((1,), (1,)), ((), ())), "tn": (((0,), (0,)), ((), ()))}
_PICK_A = {"nn": lambda i, j, k: (i, k), "nt": lambda i, j, k: (i, k), "tn": lambda i, j, k: (k, i)}
_PICK_B = {"nn": lambda i, j, k: (k, j), "nt": lambda i, j, k: (j, k), "tn": lambda i, j, k: (k, j)}


def _mm(name, mode, a, b, out_dtype, *, out=None, res=None, bias=None, scale=1.0, norm_gain=None,
        tm_t=1024, tn_t=1024, tk_t=2048):
    if not isinstance(a, Op):
        a = Op(a)
    if not isinstance(b, Op):
        b = Op(b)
    if mode == "nn":
        M, K, N = a.R, a.C, b.C
        assert b.R == K
        um, uk, un = a.unit(0), math.gcd(a.unit(1), b.unit(0)), b.unit(1)
    elif mode == "nt":
        M, K, N = a.R, a.C, b.R
        assert b.C == K
        um, uk, un = a.unit(0), math.gcd(a.unit(1), b.unit(1)), b.unit(0)
    else:
        K, M, N = a.R, a.C, b.C
        assert b.R == K
        um, uk, un = a.unit(1), math.gcd(a.unit(0), b.unit(0)), b.unit(1)
    if out is not None:
        okind, oL, olayers, olayer = out
        if okind == "c":
            un = math.gcd(un, N // oL)
        else:
            um = math.gcd(um, M // oL)
    tm, tn, tk = _tile(um, tm_t), _tile(un, tn_t), _tile(uk, tk_t)
    gi, gj, gk = M // tm, N // tn, K // tk
    a_blk = (tm, tk) if mode != "tn" else (tk, tm)
    b_blk = {"nn": (tk, tn), "nt": (tn, tk), "tn": (tk, tn)}[mode]
    in_specs = [a.spec(*a_blk, _PICK_A[mode]), b.spec(*b_blk, _PICK_B[mode])]
    args = [a.arr, b.arr]
    if res is not None:
        in_specs.append(pl.BlockSpec((tm, tn), lambda i, j, k: (i, j)))
        args.append(res)
    if bias is not None:
        in_specs.append(pl.BlockSpec((1, tn), lambda i, j, k: (0, j)))
        args.append(bias)
    if out is None:
        out_shape = S((M, N), out_dtype)
        out_spec = pl.BlockSpec((tm, tn), lambda i, j, k: (i, j))
    else:
        shp = (oL, olayers, M, N // oL) if okind == "c" else (oL, olayers, M // oL, N)
        out_shape = S(shp, out_dtype)
        out_spec = Op(out_shape, okind, olayer).spec(tm, tn, lambda i, j, k: (i, j))
    has_res, has_bias, has_norm = res is not None, bias is not None, norm_gain is not None
    if has_norm:
        assert out is None and tn == N
        in_specs.append(pl.BlockSpec((1, N), lambda i, j, k: (0, 0)))
        args.append(norm_gain)
        out_shape = [out_shape, S((M, N), BF16)]
        out_spec = [out_spec, pl.BlockSpec((tm, tn), lambda i, j, k: (i, j))]
    dims = _DIMS[mode]

    def body(a_ref, b_ref, *rest):
        rest = list(rest)
        res_ref = rest.pop(0) if has_res else None
        bias_ref = rest.pop(0) if has_bias else None
        gain_ref = rest.pop(0) if has_norm else None
        o_ref = rest.pop(0)
        h_ref = rest.pop(0) if has_norm else None
        part = lax.dot_general(a_ref[...].astype(BF16), b_ref[...].astype(BF16), dims, preferred_element_type=F32)

        def finish(r):
            if scale != 1.0:
                r = r * scale
            if has_bias:
                r = r + bias_ref[...]
            if has_res:
                r = r + res_ref[...]
            o_ref[...] = r.astype(o_ref.dtype)
            if has_norm:
                h_ref[...] = (r * lax.rsqrt(jnp.mean(r * r, -1, keepdims=True) + EPS) * gain_ref[...]).astype(BF16)

        if gk == 1:
            finish(part)
            return
        acc, = rest
        k = pl.program_id(2)

        @pl.when(k == 0)
        def _():
            acc[...] = part

        @pl.when((k > 0) & (k < gk - 1))
        def _():
            acc[...] += part

        @pl.when(k == gk - 1)
        def _():
            finish(acc[...] + part)

    return pl.pallas_call(
        body, grid=(gi, gj, gk), in_specs=in_specs, out_specs=out_spec, out_shape=out_shape,
        scratch_shapes=[pltpu.VMEM((tm, tn), F32)] if gk > 1 else [],
        compiler_params=_params("parallel", "parallel", "arbitrary"), name=name)(*args)


def _rms_fwd(name, x, gain):
    T = x.shape[0]
    tt = _tile(T, 512)

    def body(x_ref, g_ref, o_ref):
        xv = x_ref[...]
        r = lax.rsqrt(jnp.mean(xv * xv, -1, keepdims=True) + EPS)
        o_ref[...] = (xv * r * g_ref[...]).astype(BF16)

    return pl.pallas_call(
        body, grid=(T // tt,),
        in_specs=[pl.BlockSpec((tt, D), lambda i: (i, 0)), pl.BlockSpec((1, D), lambda i: (0, 0))],
        out_specs=pl.BlockSpec((tt, D), lambda i: (i, 0)), out_shape=S((T, D), BF16),
        compiler_params=_params("parallel"), name=name)(x, gain)


def _rms_bwd(name, x, gain, dh, dx_in=None):
    T = x.shape[0]
    tt = _tile(T, 512)
    has_in = dx_in is not None

    def body(x_ref, g_ref, dh_ref, *rest):
        rest = list(rest)
        dxin_ref = rest.pop(0) if has_in else None
        dx_ref, dxb_ref, dg_ref = rest
        xv = x_ref[...]
        r = lax.rsqrt(jnp.mean(xv * xv, -1, keepdims=True) + EPS)
        xh = xv * r
        dy = dh_ref[...].astype(F32)
        dxh = dy * g_ref[...]
        dx = r * (dxh - xh * jnp.mean(dxh * xh, -1, keepdims=True))
        if has_in:
            dx = dx + dxin_ref[...]
        dx_ref[...] = dx
        dxb_ref[...] = dx.astype(BF16)
        part = jnp.sum(dy * xh, 0, keepdims=True)

        @pl.when(pl.program_id(0) == 0)
        def _():
            dg_ref[...] = part

        @pl.when(pl.program_id(0) > 0)
        def _():
            dg_ref[...] += part

    row = pl.BlockSpec((tt, D), lambda i: (i, 0))
    vec = pl.BlockSpec((1, D), lambda i: (0, 0))
    args = [x, gain, dh] + ([dx_in] if has_in else [])
    return pl.pallas_call(
        body, grid=(T // tt,), in_specs=[row, vec, row] + ([row] if has_in else []),
        out_specs=[row, row, vec], out_shape=[S((T, D), F32), S((T, D), BF16), S((1, D), F32)],
        compiler_params=_params("arbitrary"), name=name)(*args)


def _sigmoid(x):
    return 0.5 * jnp.tanh(0.5 * x) + 0.5


def _ffn_up(name, h, w4, layer, tm_t=512, tn_t=1408):
    T = h.shape[0]
    n_sh = w4.shape[3]
    tm, tn = _tile(T, tm_t), _tile(n_sh, tn_t)
    per = n_sh // tn

    def body(h_ref, wg_ref, wu_ref, g_ref, u_ref, a_ref):
        hv = h_ref[...]
        g = jnp.dot(hv, wg_ref[...], preferred_element_type=F32)
        u = jnp.dot(hv, wu_ref[...], preferred_element_type=F32)
        g_ref[...] = g.astype(BF16)
        u_ref[...] = u.astype(BF16)
        a_ref[...] = (g * _sigmoid(g) * u).astype(BF16)

    o = pl.BlockSpec((tm, tn), lambda j, i: (i, j))
    return pl.pallas_call(
        body, grid=(DFF // tn, T // tm),
        in_specs=[pl.BlockSpec((tm, D), lambda j, i: (i, 0)),
                  pl.BlockSpec((None, None, D, tn), lambda j, i: (j // per, layer, 0, j % per)),
                  pl.BlockSpec((None, None, D, tn), lambda j, i: (2 + j // per, layer, 0, j % per))],
        out_specs=[o, o, o], out_shape=[S((T, DFF), BF16)] * 3,
        compiler_params=_params("parallel", "parallel"), name=name)(h, w4, w4)


def _ffn_dact(name, dxb, wd4, layer, g, u, tm_t=512):
    T = dxb.shape[0]
    r_sh = wd4.shape[2]
    tm, tn = _tile(T, tm_t), _tile(r_sh, 1408)
    per = r_sh // tn

    def body(dx_ref, w_ref, g_ref, u_ref, o_ref):
        da = 0.5 * lax.dot_general(dx_ref[...], w_ref[...], _DIMS["nt"], preferred_element_type=F32)
        gv, uv = g_ref[...].astype(F32), u_ref[...].astype(F32)
        sg = _sigmoid(gv)
        o_ref[0, 0] = (da * uv * sg * (1.0 + gv * (1.0 - sg))).astype(BF16)
        o_ref[1, 0] = (da * gv * sg).astype(BF16)

    t = pl.BlockSpec((tm, tn), lambda j, i: (i, j))
    return pl.pallas_call(
        body, grid=(DFF // tn, T // tm),
        in_specs=[pl.BlockSpec((tm, D), lambda j, i: (i, 0)),
                  pl.BlockSpec((None, None, tn, D), lambda j, i: (j // per, layer, j % per, 0)), t, t],
        out_specs=pl.BlockSpec((2, 1, tm, tn), lambda j, i: (0, 0, i, j)), out_shape=S((2, 1, T, DFF), BF16),
        compiler_params=_params("parallel", "parallel"), name=name)(dxb, wd4, g, u)


def _gelu(x):
    return 0.5 * x * (1.0 + lax.erf(x * 0.7071067811865476))


def _causal(n):
    return lax.broadcasted_iota(jnp.int32, (n, n), 0) >= lax.broadcasted_iota(jnp.int32, (n, n), 1)


def _gmlp_math(u_raw, v_raw, lng, lnb, ws, bs):
    causal = _causal(CH)
    outs = []
    for g in range(GM_G):
        u, v = _gelu(u_raw[g]), _gelu(v_raw[g])
        mu = jnp.mean(v, -1, keepdims=True)
        var = jnp.mean(jnp.square(v - mu), -1, keepdims=True)
        vn = (v - mu) * lax.rsqrt(var + EPS) * lng[g] + lnb[g]
        wm = jnp.where(causal, ws[g], 0.0)
        s = jnp.dot(wm.astype(BF16), vn.astype(BF16), preferred_element_type=F32) + bs[g]
        outs.append(u * s)
    return outs


def _gmlp_load(proj_ref, lng_ref, lnb_ref, ws_ref, bs_ref):
    sl = lambda g, off: slice(off + g * GM_GD, off + (g + 1) * GM_GD)
    u_raw = [proj_ref[:, sl(g, 0)].astype(F32) for g in range(GM_G)]
    v_raw = [proj_ref[:, sl(g, D)].astype(F32) for g in range(GM_G)]
    lng = [lng_ref[:, sl(g, 0)] for g in range(GM_G)]
    lnb = [lnb_ref[:, sl(g, 0)] for g in range(GM_G)]
    ws = [ws_ref[g] for g in range(GM_G)]
    bs = [bs_ref[g] for g in range(GM_G)]
    return u_raw, v_raw, lng, lnb, ws, bs


_GM_PAR = lambda: [pl.BlockSpec((1, D), lambda i: (0, 0)), pl.BlockSpec((1, D), lambda i: (0, 0)),
                   pl.BlockSpec((GM_G, CH, CH), lambda i: (0, 0, 0)), pl.BlockSpec((GM_G, CH, 1), lambda i: (0, 0, 0))]


def _gmlp_fwd(name, proj, lng, lnb, ws, bs):
    T = proj.shape[0]

    def body(proj_ref, lng_ref, lnb_ref, ws_ref, bs_ref, o_ref):
        outs = _gmlp_math(*_gmlp_load(proj_ref, lng_ref, lnb_ref, ws_ref, bs_ref))
        for g in range(GM_G):
            o_ref[:, g * GM_GD:(g + 1) * GM_GD] = outs[g].astype(BF16)

    return pl.pallas_call(
        body, grid=(T // CH,), in_specs=[pl.BlockSpec((CH, 2 * D), lambda i: (i, 0))] + _GM_PAR(),
        out_specs=pl.BlockSpec((CH, D), lambda i: (i, 0)), out_shape=S((T, 2 * D), BF16),
        compiler_params=_params("parallel"), name=name)(proj, lng, lnb, ws, bs)


def _acc_store(first, ref, idx, val):
    @pl.when(first)
    def _():
        ref[idx] = val

    @pl.when(jnp.logical_not(first))
    def _():
        ref[idx] += val


def _gmlp_bwd(name, proj, lng, lnb, ws, bs, dmix, dproj):
    T = proj.shape[0]

    def body(proj_ref, lng_ref, lnb_ref, ws_ref, bs_ref, dmix_ref, _, dproj_ref, dlng_ref, dlnb_ref, dws_ref, dbs_ref):
        first = pl.program_id(0) == 0
        prim = _gmlp_load(proj_ref, lng_ref, lnb_ref, ws_ref, bs_ref)
        _, vjp = jax.vjp(_gmlp_math, *prim)
        du, dv, dlng, dlnb, dws, dbs = vjp([dmix_ref[:, g * GM_GD:(g + 1) * GM_GD].astype(F32) for g in range(GM_G)])
        for g in range(GM_G):
            sl = slice(g * GM_GD, (g + 1) * GM_GD)
            dproj_ref[:, sl] = du[g].astype(BF16)
            dproj_ref[:, D + g * GM_GD:D + (g + 1) * GM_GD] = dv[g].astype(BF16)
            _acc_store(first, dlng_ref, (slice(None), sl), dlng[g])
            _acc_store(first, dlnb_ref, (slice(None), sl), dlnb[g])
            _acc_store(first, dws_ref, g, dws[g])
            _acc_store(first, dbs_ref, g, dbs[g])

    par = _GM_PAR()
    return pl.pallas_call(
        body, grid=(T // CH,),
        in_specs=[pl.BlockSpec((CH, 2 * D), lambda i: (i, 0))] + par +
                 [pl.BlockSpec((CH, D), lambda i: (i, 0)), pl.BlockSpec(memory_space=pl.ANY)],
        out_specs=[pl.BlockSpec((CH, 2 * D), lambda i: (i, 0))] + par,
        out_shape=[S(dproj.shape, BF16), S((1, D), F32), S((1, D), F32), S((GM_G, CH, CH), F32), S((GM_G, CH, 1), F32)],
        input_output_aliases={6: 0}, compiler_params=_params("arbitrary"), name=name)(proj, lng, lnb, ws, bs, dmix, dproj)


CONV_TT = 256
HALO = 8


def _shift_rows(cur, halo_after, s):
    if s == 0:
        return cur
    n = cur.shape[0]
    return pltpu.roll(jnp.concatenate([cur, halo_after], 0), s, 0)[:n]


def _conv_fwd(name, proj, w, b):
    T = proj.shape[0]
    tt = _tile(T, CONV_TT)
    hb = tt // HALO

    def body(x_ref, halo_ref, w_ref, b_ref, y_ref, xc_ref):
        i = pl.program_id(0)
        x = x_ref[...].astype(F32)
        halo = halo_ref[...].astype(F32) * (i > 0).astype(F32)
        y = b_ref[...] + w_ref[3:4, :] * x
        for s in (1, 2, 3):
            y = y + w_ref[3 - s:4 - s, :] * _shift_rows(x, halo, s)
        y_ref[...] = y.astype(BF16)
        xc_ref[...] = (y * _sigmoid(y)).astype(BF16)

    o = pl.BlockSpec((tt, CONV_C), lambda i: (i, 0))
    return pl.pallas_call(
        body, grid=(T // tt,),
        in_specs=[pl.BlockSpec((tt, CONV_C), lambda i: (i, 2)),
                  pl.BlockSpec((HALO, CONV_C), lambda i: (jnp.maximum(i * hb - 1, 0), 2)),
                  pl.BlockSpec((4, CONV_C), lambda i: (0, 0)), pl.BlockSpec((1, CONV_C), lambda i: (0, 0))],
        out_specs=[o, o], out_shape=[S((T, CONV_C), BF16)] * 2,
        compiler_params=_params("parallel"), name=name)(proj, proj, w, b)


def _conv_bwd(name, proj, ypre, dxc, w, dproj):
    T = proj.shape[0]
    tt = _tile(T, CONV_TT)
    hb = tt // HALO
    nt = T // tt

    def dsilu(y):
        sg = _sigmoid(y)
        return sg * (1.0 + y * (1.0 - sg))

    def body(x_ref, xh_ref, y_ref, yn_ref, d_ref, dn_ref, w_ref, _, dproj_ref, dw_ref, db_ref):
        i = pl.program_id(0)
        first = i == 0
        x = x_ref[...].astype(F32)
        halo = xh_ref[...].astype(F32) * (i > 0).astype(F32)
        dy = d_ref[...].astype(F32) * dsilu(y_ref[...].astype(F32))
        dyn = dn_ref[...].astype(F32) * dsilu(yn_ref[...].astype(F32)) * (i < nt - 1).astype(F32)
        ext = jnp.concatenate([dy, dyn], 0)
        dx = w_ref[3:4, :] * dy
        _acc_store(first, dw_ref, (slice(3, 4), slice(None)), jnp.sum(x * dy, 0, keepdims=True))
        for s in (1, 2, 3):
            dx = dx + w_ref[3 - s:4 - s, :] * pltpu.roll(ext, tt + HALO - s, 0)[:tt]
            _acc_store(first, dw_ref, (slice(3 - s, 4 - s), slice(None)),
                       jnp.sum(_shift_rows(x, halo, s) * dy, 0, keepdims=True))
        _acc_store(first, db_ref, (slice(None), slice(None)), jnp.sum(dy, 0, keepdims=True))
        dproj_ref[...] = dx.astype(BF16)

    cur = pl.BlockSpec((tt, CONV_C), lambda i: (i, 0))
    nxt = pl.BlockSpec((HALO, CONV_C), lambda i: (jnp.minimum((i + 1) * hb, T // HALO - 1), 0))
    return pl.pallas_call(
        body, grid=(nt,),
        in_specs=[pl.BlockSpec((tt, CONV_C), lambda i: (i, 2)),
                  pl.BlockSpec((HALO, CONV_C), lambda i: (jnp.maximum(i * hb - 1, 0), 2)),
                  cur, nxt, cur, nxt, pl.BlockSpec((4, CONV_C), lambda i: (0, 0)), pl.BlockSpec(memory_space=pl.ANY)],
        out_specs=[pl.BlockSpec((tt, CONV_C), lambda i: (i, 2)), pl.BlockSpec((4, CONV_C), lambda i: (0, 0)),
                   pl.BlockSpec((1, CONV_C), lambda i: (0, 0))],
        out_shape=[S(dproj.shape, BF16), S((4, CONV_C), F32), S((1, CONV_C), F32)],
        input_output_aliases={7: 0}, compiler_params=_params("arbitrary"), name=name)(proj, proj, ypre, ypre, dxc, dxc, w, dproj)


def _softplus(x):
    return jnp.maximum(x, 0.0) + jnp.log(1.0 + jnp.exp(-jnp.abs(x)))


def _ssd_math(x, Bm, Cm, dtr, z, prev, dtb, alog, dsk, nrm):
    hi = lax.Precision.HIGHEST
    causal = _causal(CH)
    tri = causal.astype(F32)
    lane = lax.broadcasted_iota(jnp.int32, (1, LANE), 1)
    sub = lax.broadcasted_iota(jnp.int32, (LANE, 1), 0)
    dt = _softplus(dtr + dtb)
    a = dt * (-jnp.exp(alog))
    a_cs = jnp.dot(tri, a, preferred_element_type=F32, precision=hi)
    a_csT = lax.dot_general(a, tri, (((0,), (1,)), ((), ())), preferred_element_type=F32, precision=hi)
    a_last = jnp.sum(a, 0, keepdims=True)
    gw = HPG * HD
    outs, new = [], []
    for g in range(NG):
        spread = (lax.broadcasted_iota(jnp.int32, (LANE, gw), 0)
                  == g * HPG + lax.broadcasted_iota(jnp.int32, (LANE, gw), 1) // HD).astype(F32)
        to_lanes = lambda v: jnp.dot(v, spread, preferred_element_type=F32, precision=hi)
        col_e, dt_e, last_e, dsk_e = to_lanes(a_cs), to_lanes(dt), to_lanes(a_last), to_lanes(dsk)
        last_r = lax.dot_general(spread, a_last, (((0,), (1,)), ((), ())), preferred_element_type=F32, precision=hi)
        cb = lax.dot_general(Cm[g].astype(BF16), Bm[g].astype(BF16), _DIMS["nt"], preferred_element_type=F32)
        xg = jnp.concatenate(x[g * HPG:(g + 1) * HPG], 1)
        yd = []
        for h in range(g * HPG, (g + 1) * HPG):
            ohl = (lane == h).astype(F32)
            col = jnp.sum(a_cs * ohl, 1, keepdims=True)
            row = jnp.sum(a_csT * (sub == h).astype(F32), 0, keepdims=True)
            dtc = jnp.sum(dt * ohl, 1, keepdims=True)
            lmat = jnp.where(causal, jnp.exp(jnp.where(causal, col - row, 0.0)), 0.0)
            yd.append(jnp.dot((cb * lmat).astype(BF16), (x[h] * dtc).astype(BF16), preferred_element_type=F32))
        y = jnp.concatenate(yd, 1)
        y = y + jnp.exp(col_e) * lax.dot_general(Cm[g].astype(BF16), prev[g].astype(BF16), _DIMS["nt"],
                                                 preferred_element_type=F32)
        st = lax.dot_general((xg * dt_e * jnp.exp(last_e - col_e)).astype(BF16), Bm[g].astype(BF16), _DIMS["tn"],
                             preferred_element_type=F32)
        new.append(prev[g] * jnp.exp(last_r) + st)
        yg = (y + xg * dsk_e) * (z[g] * _sigmoid(z[g]))
        yg = yg * lax.rsqrt(jnp.mean(yg * yg, -1, keepdims=True) + EPS)
        outs.append(yg * nrm[g])
    return outs, new


def _ssd_load(xc_ref, dtr_ref, z_ref, state_ref, dtb_ref, alog_ref, dsk_ref, nrm_ref):
    gw = HPG * HD
    x = [xc_ref[:, h * HD:(h + 1) * HD].astype(F32) for h in range(NH)]
    Bm = [xc_ref[:, D + g * NS:D + (g + 1) * NS].astype(F32) for g in range(NG)]
    Cm = [xc_ref[:, D + NG * NS + g * NS:D + NG * NS + (g + 1) * NS].astype(F32) for g in range(NG)]
    z = [z_ref[:, g * gw:(g + 1) * gw].astype(F32) for g in range(NG)]
    prev = [state_ref[g * gw:(g + 1) * gw, :] for g in range(NG)]
    nrm = [nrm_ref[:, g * gw:(g + 1) * gw] for g in range(NG)]
    return x, Bm, Cm, dtr_ref[...], z, prev, dtb_ref[...], alog_ref[...], dsk_ref[...], nrm


_SSD_PAR = lambda: [pl.BlockSpec((1, LANE), lambda c: (0, 0))] * 3 + [pl.BlockSpec((1, D), lambda c: (0, 0))]


def _ssd_fwd(name, xc, dtr, proj, dtb, alog, dsk, nrm, mix):
    T = xc.shape[0]
    nc = T // CH

    def body(xc_ref, dtr_ref, z_ref, dtb_ref, alog_ref, dsk_ref, nrm_ref, _, mix_ref, prev_ref, state):
        @pl.when(pl.program_id(0) == 0)
        def _():
            state[...] = jnp.zeros_like(state)

        prev_ref[...] = state[...]
        outs, new = _ssd_math(*_ssd_load(xc_ref, dtr_ref, z_ref, state, dtb_ref, alog_ref, dsk_ref, nrm_ref))
        for g in range(NG):
            mix_ref[:, g * 512:(g + 1) * 512] = outs[g].astype(BF16)
            state[g * 512:(g + 1) * 512, :] = new[g]

    return pl.pallas_call(
        body, grid=(nc,),
        in_specs=[pl.BlockSpec((CH, CONV_C), lambda c: (c, 0)), pl.BlockSpec((CH, LANE), lambda c: (c, 0)),
                  pl.BlockSpec((CH, D), lambda c: (c, 2))] + _SSD_PAR() + [pl.BlockSpec(memory_space=pl.ANY)],
        out_specs=[pl.BlockSpec((CH, D), lambda c: (c, 1)), pl.BlockSpec((None, NH * HD, NS), lambda c: (c, 0, 0))],
        out_shape=[S(mix.shape, BF16), S((nc, NH * HD, NS), F32)],
        scratch_shapes=[pltpu.VMEM((NH * HD, NS), F32)], input_output_aliases={7: 0},
        compiler_params=_params("arbitrary"), name=name)(xc, dtr, proj, dtb, alog, dsk, nrm, mix)


def _ssd_bwd(name, xc, dtr, proj, prevs, dtb, alog, dsk, nrm, dmix, dproj):
    T = xc.shape[0]
    nc = T // CH
    rev = lambda c: nc - 1 - c

    def body(xc_ref, dtr_ref, z_ref, prev_ref, dtb_ref, alog_ref, dsk_ref, nrm_ref, dmix_ref, _,
             dproj_ref, dxc_ref, ddtr_ref, ddtb_ref, dalog_ref, ddsk_ref, dnrm_ref, dstate):
        first = pl.program_id(0) == 0

        @pl.when(first)
        def _():
            dstate[...] = jnp.zeros_like(dstate)

        prim = _ssd_load(xc_ref, dtr_ref, z_ref, prev_ref, dtb_ref, alog_ref, dsk_ref, nrm_ref)
        _, vjp = jax.vjp(_ssd_math, *prim)
        douts = [dmix_ref[:, g * 512:(g + 1) * 512].astype(F32) for g in range(NG)]
        dnew = [dstate[g * 512:(g + 1) * 512, :] for g in range(NG)]
        dx, dB, dC, ddtr, dz, dprev, ddtb, dalog, ddsk, dnrm = vjp((douts, dnew))
        for h in range(NH):
            dxc_ref[:, h * HD:(h + 1) * HD] = dx[h].astype(BF16)
        for g in range(NG):
            dstate[g * 512:(g + 1) * 512, :] = dprev[g]
            dxc_ref[:, D + g * NS:D + (g + 1) * NS] = dB[g].astype(BF16)
            dxc_ref[:, D + NG * NS + g * NS:D + NG * NS + (g + 1) * NS] = dC[g].astype(BF16)
            dproj_ref[:, g * 512:(g + 1) * 512] = dz[g].astype(BF16)
            _acc_store(first, dnrm_ref, (slice(None), slice(g * 512, (g + 1) * 512)), dnrm[g])
        ddtr_ref[...] = ddtr
        _acc_store(first, ddtb_ref, (slice(None), slice(None)), ddtb)
        _acc_store(first, dalog_ref, (slice(None), slice(None)), dalog)
        _acc_store(first, ddsk_ref, (slice(None), slice(None)), ddsk)

    vec = pl.BlockSpec((1, LANE), lambda c: (0, 0))
    return pl.pallas_call(
        body, grid=(nc,),
        in_specs=[pl.BlockSpec((CH, CONV_C), lambda c: (rev(c), 0)), pl.BlockSpec((CH, LANE), lambda c: (rev(c), 0)),
                  pl.BlockSpec((CH, D), lambda c: (rev(c), 2)),
                  pl.BlockSpec((None, NH * HD, NS), lambda c: (rev(c), 0, 0))] + _SSD_PAR() +
                 [pl.BlockSpec((CH, D), lambda c: (rev(c), 1)), pl.BlockSpec(memory_space=pl.ANY)],
        out_specs=[pl.BlockSpec((CH, D), lambda c: (rev(c), 2)), pl.BlockSpec((CH, CONV_C), lambda c: (rev(c), 0)),
                   pl.BlockSpec((CH, LANE), lambda c: (rev(c), 0)), vec, vec, vec, pl.BlockSpec((1, D), lambda c: (0, 0))],
        out_shape=[S(dproj.shape, BF16), S((T, CONV_C), BF16), S((T, LANE), F32), S((1, LANE), F32), S((1, LANE), F32),
                   S((1, LANE), F32), S((1, D), F32)],
        scratch_shapes=[pltpu.VMEM((NH * HD, NS), F32)], input_output_aliases={9: 0},
        compiler_params=_params("arbitrary"), name=name)(xc, dtr, proj, prevs, dtb, alog, dsk, nrm, dmix, dproj)


def _rope(x, c, s, sign):
    W = x.shape[1]
    reps = W // LANE
    C, Sg = jnp.tile(c, (1, reps)), jnp.tile(s, (1, reps))
    lane = lax.broadcasted_iota(jnp.int32, x.shape, 1) % AHD
    up, dn = pltpu.roll(x, W - ROT // 2, 1), pltpu.roll(x, ROT // 2, 1)
    sw = jnp.where(lane < ROT // 2, up, jnp.where(lane < ROT, dn, 0.0))
    return x * C + sign * sw * Sg


def _rope_fwd(name, qkv, cos, sin):
    T = qkv.shape[0]
    tt = _tile(T, 256)
    KV = AKV * AHD

    def body(x_ref, c_ref, s_ref, o_ref):
        c, s = c_ref[...], s_ref[...]
        o_ref[:, :D] = _rope(x_ref[:, :D], c, s, 1.0).astype(BF16)
        o_ref[:, D:D + KV] = _rope(x_ref[:, D:D + KV], c, s, 1.0).astype(BF16)
        o_ref[:, D + KV:] = x_ref[:, D + KV:].astype(BF16)

    tab = pl.BlockSpec((tt, LANE), lambda i: (i, 0))
    return pl.pallas_call(
        body, grid=(T // tt,), in_specs=[pl.BlockSpec((tt, ODD_IN), lambda i: (i, 0)), tab, tab],
        out_specs=pl.BlockSpec((tt, ODD_IN), lambda i: (i, 0)), out_shape=S((T, ODD_IN), BF16),
        compiler_params=_params("parallel"), name=name)(qkv, cos, sin)


def _rope_bwd(name, dq, dkv_cur, dkv_prev, cos, sin):
    T = dq.shape[0]
    nb = T // CH
    KV = AKV * AHD

    def body(dq_ref, cur_ref, nxt_ref, c_ref, s_ref, o_ref, db_ref):
        n = pl.program_id(0)
        c, s = c_ref[...], s_ref[...]
        dkv = cur_ref[...] + nxt_ref[...] * (n < nb - 1).astype(F32)
        o_ref[:, :D] = _rope(dq_ref[...].astype(F32), c, s, -1.0).astype(BF16)
        o_ref[:, D:D + KV] = _rope(dkv[:, :KV], c, s, -1.0).astype(BF16)
        o_ref[:, D + KV:] = dkv[:, KV:].astype(BF16)
        _acc_store(n == 0, db_ref, (slice(None), slice(None)), jnp.sum(o_ref[...].astype(F32), 0, keepdims=True))

    tab = pl.BlockSpec((CH, LANE), lambda n: (n, 0))
    return pl.pallas_call(
        body, grid=(nb,),
        in_specs=[pl.BlockSpec((CH, D), lambda n: (n, 0)), pl.BlockSpec((CH, 2 * KV), lambda n: (n, 0)),
                  pl.BlockSpec((CH, 2 * KV), lambda n: (jnp.minimum(n + 1, nb - 1), 0)), tab, tab],
        out_specs=[pl.BlockSpec((CH, ODD_IN), lambda n: (n, 0)), pl.BlockSpec((1, ODD_IN), lambda n: (0, 0))],
        out_shape=[S((T, ODD_IN), BF16), S((1, ODD_IN), F32)],
        compiler_params=_params("arbitrary"), name=name)(dq, dkv_cur, dkv_prev, cos, sin)


def _swa_math(q, kp, kc, vp, vc, snk, mask):
    outs = []
    for k in range(AKV):
        K = jnp.concatenate([kp[k], kc[k]], 0).astype(BF16)
        V = jnp.concatenate([vp[k], vc[k]], 0).astype(BF16)
        s = lax.dot_general(q[k].astype(BF16), K, _DIMS["nt"], preferred_element_type=F32) * ATT_SCALE
        s = jnp.where(mask, s, -jnp.inf)
        m = lax.stop_gradient(jnp.maximum(jnp.max(s, -1, keepdims=True), snk[k]))
        p = jnp.exp(s - m)
        pr = p / (jnp.sum(p, -1, keepdims=True) + jnp.exp(snk[k] - m))
        outs.append(jnp.dot(pr.astype(BF16), V, preferred_element_type=F32))
    return outs


def _stack_heads(ref, k):
    return jnp.concatenate([ref[:, (k * AREP + r) * AHD:(k * AREP + r + 1) * AHD].astype(F32) for r in range(AREP)], 0)


def _swa_load(q_ref, cur_ref, prv_ref, snk_ref):
    KV = AKV * AHD
    q = [_stack_heads(q_ref, k) for k in range(AKV)]
    kc = [cur_ref[:, k * AHD:(k + 1) * AHD].astype(F32) for k in range(AKV)]
    vc = [cur_ref[:, KV + k * AHD:KV + (k + 1) * AHD].astype(F32) for k in range(AKV)]
    kp = [prv_ref[:, k * AHD:(k + 1) * AHD].astype(F32) for k in range(AKV)]
    vp = [prv_ref[:, KV + k * AHD:KV + (k + 1) * AHD].astype(F32) for k in range(AKV)]
    snk = [jnp.concatenate([jnp.broadcast_to(snk_ref[:, k * AREP + r:k * AREP + r + 1], (CH, 1)) for r in range(AREP)], 0)
           for k in range(AKV)]
    return q, kp, kc, vp, vc, snk


def _swa_mask(n):
    iq = lax.broadcasted_iota(jnp.int32, (AREP * CH, 2 * CH), 0) % CH
    js = lax.broadcasted_iota(jnp.int32, (AREP * CH, 2 * CH), 1)
    rel = iq + CH - js
    return (rel >= 0) & (rel < CH) & ((n > 0) | (js >= CH))


def _swa_specs(T):
    KV = AKV * AHD
    return [pl.BlockSpec((CH, D), lambda n: (n, 0)), pl.BlockSpec((CH, 2 * KV), lambda n: (n, D // (2 * KV))),
            pl.BlockSpec((CH, 2 * KV), lambda n: (jnp.maximum(n - 1, 0), D // (2 * KV))),
            pl.BlockSpec((1, LANE), lambda n: (0, 0))]


def _swa_fwd(name, qkvr, snk):
    T = qkvr.shape[0]

    def body(q_ref, cur_ref, prv_ref, snk_ref, o_ref):
        outs = _swa_math(*_swa_load(q_ref, cur_ref, prv_ref, snk_ref), _swa_mask(pl.program_id(0)))
        for h in range(AH):
            k, r = divmod(h, AREP)
            o_ref[:, h * AHD:(h + 1) * AHD] = outs[k][r * CH:(r + 1) * CH].astype(BF16)

    return pl.pallas_call(
        body, grid=(T // CH,), in_specs=_swa_specs(T), out_specs=pl.BlockSpec((CH, D), lambda n: (n, 0)),
        out_shape=S((T, D), BF16), compiler_params=_params("parallel"), name=name)(qkvr, qkvr, qkvr, snk)


def _swa_bwd(name, qkvr, snk, do):
    T = qkvr.shape[0]
    KV = AKV * AHD

    def body(q_ref, cur_ref, prv_ref, snk_ref, do_ref, dq_ref, dcur_ref, dprv_ref, dsnk_ref):
        n = pl.program_id(0)

        @pl.when(n == 0)
        def _():
            dsnk_ref[...] = jnp.zeros_like(dsnk_ref)

        prim = _swa_load(q_ref, cur_ref, prv_ref, snk_ref)
        mask = _swa_mask(n)
        _, vjp = jax.vjp(lambda *p: _swa_math(*p, mask), *prim)
        dq, dkp, dkc, dvp, dvc, dsnk = vjp([_stack_heads(do_ref, k) for k in range(AKV)])
        for h in range(AH):
            k, r = divmod(h, AREP)
            dq_ref[:, h * AHD:(h + 1) * AHD] = dq[k][r * CH:(r + 1) * CH].astype(BF16)
            dsnk_ref[:, h:h + 1] += jnp.sum(dsnk[k][r * CH:(r + 1) * CH], 0, keepdims=True)
        for k in range(AKV):
            dcur_ref[:, k * AHD:(k + 1) * AHD] = dkc[k]
            dcur_ref[:, KV + k * AHD:KV + (k + 1) * AHD] = dvc[k]
            dprv_ref[:, k * AHD:(k + 1) * AHD] = dkp[k]
            dprv_ref[:, KV + k * AHD:KV + (k + 1) * AHD] = dvp[k]

    kv = pl.BlockSpec((CH, 2 * KV), lambda n: (n, 0))
    return pl.pallas_call(
        body, grid=(T // CH,), in_specs=_swa_specs(T) + [pl.BlockSpec((CH, D), lambda n: (n, 0))],
        out_specs=[pl.BlockSpec((CH, D), lambda n: (n, 0)), kv, kv, pl.BlockSpec((1, LANE), lambda n: (0, 0))],
        out_shape=[S((T, D), BF16), S((T, 2 * KV), F32), S((T, 2 * KV), F32), S((1, LANE), F32)],
        compiler_params=_params("arbitrary"), name=name)(qkvr, qkvr, qkvr, snk, do)


def _xat_math(q, k, v):
    outs = []
    for h in range(XH):
        s = lax.dot_general(q[h].astype(BF16), k[h].astype(BF16), _DIMS["nt"], preferred_element_type=F32) * X_SCALE
        m = lax.stop_gradient(jnp.max(s, -1, keepdims=True))
        p = jnp.exp(s - m)
        pr = p / jnp.sum(p, -1, keepdims=True)
        outs.append(jnp.dot(pr.astype(BF16), v[h].astype(BF16), preferred_element_type=F32))
    return outs


def _xat_load(q_ref, kv_ref):
    q = [q_ref[:, h * XHD:(h + 1) * XHD].astype(F32) for h in range(XH)]
    k = [kv_ref[:, h * XHD:(h + 1) * XHD].astype(F32) for h in range(XH)]
    v = [kv_ref[:, XW + h * XHD:XW + (h + 1) * XHD].astype(F32) for h in range(XH)]
    return q, k, v


def _xat_fwd(name, q, kv):
    T, M = q.shape[0], kv.shape[0]
    tt = _tile(T, 512)

    def body(q_ref, kv_ref, o_ref):
        outs = _xat_math(*_xat_load(q_ref, kv_ref))
        for h in range(XH):
            o_ref[:, h * XHD:(h + 1) * XHD] = outs[h].astype(BF16)

    return pl.pallas_call(
        body, grid=(T // tt,),
        in_specs=[pl.BlockSpec((tt, XW), lambda i: (i, 0)), pl.BlockSpec((M, 2 * XW), lambda i: (0, 0))],
        out_specs=pl.BlockSpec((tt, XW), lambda i: (i, 0)), out_shape=S((T, XW), BF16),
        compiler_params=_params("parallel"), name=name)(q, kv)


def _xat_bwd(name, q, kv, do):
    T, M = q.shape[0], kv.shape[0]
    tt = _tile(T, 512)

    def body(q_ref, kv_ref, do_ref, dq_ref, dkv_ref):
        first = pl.program_id(0) == 0
        _, vjp = jax.vjp(_xat_math, *_xat_load(q_ref, kv_ref))
        dq, dk, dv = vjp([do_ref[:, h * XHD:(h + 1) * XHD].astype(F32) for h in range(XH)])
        for h in range(XH):
            sl = slice(h * XHD, (h + 1) * XHD)
            dq_ref[:, sl] = dq[h].astype(BF16)
            _acc_store(first, dkv_ref, (slice(None), sl), dk[h])
            _acc_store(first, dkv_ref, (slice(None), slice(XW + h * XHD, XW + (h + 1) * XHD)), dv[h])

    qs = pl.BlockSpec((tt, XW), lambda i: (i, 0))
    kvs = pl.BlockSpec((M, 2 * XW), lambda i: (0, 0))
    return pl.pallas_call(
        body, grid=(T // tt,), in_specs=[qs, kvs, qs], out_specs=[qs, kvs],
        out_shape=[S((T, XW), BF16), S((M, 2 * XW), F32)],
        compiler_params=_params("arbitrary"), name=name)(q, kv, do)


def _loss_head(name, x, gain, target):
    T = x.shape[0]
    tt = _tile(T, 512)

    def body(x_ref, g_ref, t_ref, l_ref, dx_ref, dxb_ref, dg_ref):
        first = pl.program_id(0) == 0
        xv, g = x_ref[...], g_ref[...]
        r = lax.rsqrt(jnp.mean(xv * xv, -1, keepdims=True) + EPS)
        xh = xv * r
        e = xh * g - t_ref[...]
        part = 0.5 * jnp.sum(jnp.mean(e * e, -1, keepdims=True), (0, 1), keepdims=True)
        _acc_store(first, l_ref, (slice(None), slice(None)), jnp.broadcast_to(part, (1, LANE)))
        dy = e * (1.0 / D)
        dxh = dy * g
        dx = r * (dxh - xh * jnp.mean(dxh * xh, -1, keepdims=True))
        dx_ref[...] = dx
        dxb_ref[...] = dx.astype(BF16)
        _acc_store(first, dg_ref, (slice(None), slice(None)), jnp.sum(dy * xh, 0, keepdims=True))

    row = pl.BlockSpec((tt, D), lambda i: (i, 0))
    vec = pl.BlockSpec((1, D), lambda i: (0, 0))
    return pl.pallas_call(
        body, grid=(T // tt,), in_specs=[row, vec, row],
        out_specs=[pl.BlockSpec((1, LANE), lambda i: (0, 0)), row, row, vec],
        out_shape=[S((1, LANE), F32), S((T, D), F32), S((T, D), BF16), S((1, D), F32)],
        compiler_params=_params("arbitrary"), name=name)(x, gain, target)


def _out_proj(name, a, b, x, next_gain, scale=1.0, tk_t=2048, plain=(1024, 1024)):
    if next_gain is None:
        return _mm(name, "nn", a, b, F32, res=x, scale=scale, tm_t=plain[0], tn_t=plain[1], tk_t=tk_t), None
    return _mm(name, "nn", a, b, F32, res=x, scale=scale, norm_gain=next_gain, tm_t=512, tn_t=D, tk_t=min(tk_t, 2048))


def _ffn_fwd(tag, x, h, gain, wgu4, get_wd, next_gain):
    g, u, a = _ffn_up(f"{tag}_up", h, wgu4, 0)
    wd = get_wd(a).reshape(1, 1, DFF, D)
    x_new, _ = _out_proj(f"{tag}_down", a, Op(wd, "r"), x, None, 0.5, 2816)
    h_next = None if next_gain is None else _rms_fwd(f"{tag}_nextnorm", x_new, next_gain)
    return x_new, h_next, (x, gain, h, g, u, a)


def _ffn_bwd(tag, saved, dx, dxb, wgu4, wd4, put):
    x, gain, h, g, u, a = saved
    dgu = _ffn_dact(f"{tag}_dact", dxb, wd4, 0, g, u, 1024)
    dwd = _mm(f"{tag}_dwd", "tn", a, dxb, BF16, out=("r", 4, 1, 0), scale=0.5, tm_t=1408, tn_t=1024, tk_t=2048)
    dwgu = _mm(f"{tag}_dwgu", "tn", h, Op(dgu, "c"), BF16, out=("c", 4, 1, 0), tm_t=1024, tn_t=256, tk_t=8192)
    tok = put(dwgu, dwd)
    dh = _mm(f"{tag}_dh", "nt", Op(dgu, "c"), Op(wgu4, "c"), BF16, bias=jnp.zeros((1, D), F32) + tok, tk_t=2816)
    dx, dxb, dgain = _rms_bwd(f"{tag}_dnorm", x, gain, dh, dx)
    return dx, dxb, dgain


def _xattn_fwd(tag, x, hq, mem, gq, gm, wxq4, wxkv4, wxo4, next_gain):
    mn = _rms_fwd(f"{tag}_normm", mem, gm)
    q = _mm(f"{tag}_q", "nn", hq, Op(wxq4, "r"), BF16)
    kv = _mm(f"{tag}_kv", "nn", mn, Op(wxkv4, "r"), BF16)
    o = _xat_fwd(f"{tag}_att", q, kv)
    wxo = jnp.transpose(wxo4[:, 0], (1, 0, 2)).reshape(XW, D)
    x_new, h_next = _out_proj(f"{tag}_o", o, wxo, x, next_gain)
    return x_new, h_next, (x, mem, gq, gm, hq, mn, q, kv, o)


def _xattn_bwd(tag, saved, dx, dxb, wxq4, wxkv4, wxo4, put):
    x, mem, gq, gm, hq, mn, q, kv, o = saved
    dwxo = _mm(f"{tag}_dwo", "tn", o, dxb, BF16, out=("c", 4, 1, 0))
    do = _mm(f"{tag}_do", "nt", dxb, Op(wxo4, "c"), BF16)
    dq, dkv = _xat_bwd(f"{tag}_datt", q, kv, do)
    dwxq = _mm(f"{tag}_dwq", "tn", hq, dq, BF16, out=("r", 4, 1, 0))
    dwxkv = _mm(f"{tag}_dwkv", "tn", mn, dkv, BF16, out=("r", 4, 1, 0))
    tok = put(dwxq, dwxkv, dwxo)
    dhq = _mm(f"{tag}_dhq", "nt", dq, Op(wxq4, "r"), BF16, bias=jnp.zeros((1, D), F32) + tok)
    dmn = _mm(f"{tag}_dmn", "nt", dkv, Op(wxkv4, "r"), BF16)
    _, _, dgm = _rms_bwd(f"{tag}_dnormm", mem, gm, dmn)
    dx, dxb, dgq = _rms_bwd(f"{tag}_dnormq", x, gq, dhq, dx)
    return dx, dxb, dgq, dgm


def _even_fwd(tag, x, h, gain, w_main, w_dt, p, wout4, next_gain):
    proj = _mm(f"{tag}_in", "nn", h, w_main, BF16)
    dtr = _mm(f"{tag}_indt", "nn", h, w_dt, F32)
    mix = _gmlp_fwd(f"{tag}_gmlp", proj, p["lng"], p["lnb"], p["ws"], p["bs"])
    ypre, xc = _conv_fwd(f"{tag}_conv", proj, p["cw"], p["cb"])
    mix, prevs = _ssd_fwd(f"{tag}_ssd", xc, dtr, proj, p["dtb"], p["alog"], p["dsk"], p["nrm"], mix)
    x_new, h_next = _out_proj(f"{tag}_out", mix, Op(wout4.reshape(1, 1, 2 * D, D), "r"), x, next_gain)
    return x_new, h_next, (x, gain, h, proj, dtr, mix, ypre, xc, prevs)


def _even_bwd(tag, saved, dx, dxb, w_main, w_dt, p, wout4, put):
    x, gain, h, proj, dtr, mix, ypre, xc, prevs = saved
    T = x.shape[0]
    dwout = _mm(f"{tag}_dwout", "tn", mix, dxb, BF16, out=("r", 4, 1, 0))
    dmix = _mm(f"{tag}_dmix", "nt", dxb, Op(wout4, "r", 0), BF16)
    dproj = lax.empty((T, EVEN_MAIN), BF16)
    dproj, dlng, dlnb, dws, dbs = _gmlp_bwd(f"{tag}_dgmlp", proj, p["lng"], p["lnb"], p["ws"], p["bs"], dmix, dproj)
    dproj, dxc, ddtr, ddtb, dalog, ddsk, dnrm = _ssd_bwd(
        f"{tag}_dssd", xc, dtr, proj, prevs, p["dtb"], p["alog"], p["dsk"], p["nrm"], dmix, dproj)
    dproj, dcw, dcb = _conv_bwd(f"{tag}_dconv", proj, ypre, dxc, p["cw"], dproj)
    dw_main = _mm(f"{tag}_dwin", "tn", h, dproj, BF16, tm_t=1024, tn_t=256, tk_t=8192)
    dw_dt = _mm(f"{tag}_dwdt", "tn", h, ddtr, BF16)
    tok = put(dw_main, dw_dt, dwout)
    dh = _mm(f"{tag}_dh1", "nt", ddtr, w_dt + tok.astype(BF16), F32)
    dh = _mm(f"{tag}_dh2", "nt", dproj, w_main, BF16, res=dh)
    dx, dxb, dgain = _rms_bwd(f"{tag}_dnorm", x, gain, dh, dx)
    small = dict(lng=dlng, lnb=dlnb, ws=dws, bs=dbs, cw=dcw, cb=dcb, dtb=ddtb, alog=dalog, dsk=ddsk, nrm=dnrm)
    return dx, dxb, dgain, small


def _odd_fwd(tag, x, h, gain, wqkv4, bqkv, snk, wo4, cos, sin, next_gain):
    qkv = _mm(f"{tag}_qkv", "nn", h, Op(wqkv4, "c", 0), F32, bias=bqkv, tn_t=640)
    qkvr = _rope_fwd(f"{tag}_rope", qkv, cos, sin)
    o = _swa_fwd(f"{tag}_swa", qkvr, snk)
    x_new, h_next = _out_proj(f"{tag}_o", o, Op(wo4.reshape(1, 1, D, D), "r"), x, next_gain)
    return x_new, h_next, (x, gain, h, qkvr, o)


def _odd_bwd(tag, saved, dx, dxb, wqkv4, snk, wo4, cos, sin, put):
    x, gain, h, qkvr, o = saved
    dwo = _mm(f"{tag}_dwo", "tn", o, dxb, BF16, out=("r", 4, 1, 0))
    do = _mm(f"{tag}_do", "nt", dxb, Op(wo4, "r", 0), BF16)
    dq, dcur, dprv, dsnk = _swa_bwd(f"{tag}_dswa", qkvr, snk, do)
    dqkv, dbias = _rope_bwd(f"{tag}_drope", dq, dcur, dprv, cos, sin)
    dwqkv = _mm(f"{tag}_dwqkv", "tn", h, dqkv, BF16, out=("c", 4, 1, 0), tn_t=640)
    tok = put(dwqkv, dwo)
    dh = _mm(f"{tag}_dh", "nt", dqkv, Op(wqkv4, "c", 0), BF16, bias=jnp.zeros((1, D), F32) + tok, tk_t=640)
    dx, dxb, dgain = _rms_bwd(f"{tag}_dnorm", x, gain, dh, dx)
    return dx, dxb, dgain, dbias, dsnk


def _row(v):
    return v.reshape(1, -1).astype(F32)


def _pad_lanes(v, n=LANE):
    v = v.reshape(1, -1).astype(F32)
    return jnp.pad(v, ((0, 0), (0, n - v.shape[1])))


def _local_step(x, mem, positions, target, getw, P, putg):
    inv_freq = ROPE_THETA ** (-jnp.arange(0, ROT, 2, dtype=F32) / ROT)
    ang = positions.astype(F32)[:, None] * inv_freq
    cos8, sin8 = jnp.cos(ang), jnp.sin(ang)
    ones, zeros = jnp.ones((x.shape[0], AHD - ROT), F32), jnp.zeros((x.shape[0], AHD - ROT), F32)
    cos = jnp.tile(jnp.concatenate([cos8, cos8, ones], 1), (1, 2))
    sin = jnp.tile(jnp.concatenate([-sin8, sin8, zeros], 1), (1, 2))

    snk = _pad_lanes(P["sinks"])
    W = {}

    def w(name, layer, after):
        if (name, layer) not in W:
            W[name, layer] = getw(name, layer, after)
        return W[name, layer]

    saved = []
    h = _rms_fwd("l0_ffn1_norm", x, _row(P["norm_ffn1"][0]))
    for i in range(2):
        x, h, s1 = _ffn_fwd(f"l{i}_ffn1", x, h, _row(P["norm_ffn1"][i]), w("w_ffn1_gu", i, x),
                            functools.partial(w, "w_ffn1_down", i), _row(P["norm_mix"][i]))
        if i == 0:
            ev = dict(lng=_row(P["gm_ln_g"]), lnb=_row(P["gm_ln_b"]), ws=P["gm_ws"].reshape(GM_G, CH, CH),
                      bs=P["gm_bs"].reshape(GM_G, CH, 1), cw=w("conv_w", 0, x), cb=_row(P["conv_b"]),
                      dtb=_pad_lanes(P["dt_bias"]), alog=_pad_lanes(P["a_log"]), dsk=_pad_lanes(P["d_skip"]),
                      nrm=_row(P["ssd_norm"]))
            w_main, w_dt = w("w_in_even", 0, x)
            x, h, s2 = _even_fwd("l0_mix", x, h, _row(P["norm_mix"][0]), w_main, w_dt, ev, w("w_out_even", 0, x),
                                 _row(P["norm_xq"][0]))
        else:
            x, h, s2 = _odd_fwd("l1_mix", x, h, _row(P["norm_mix"][1]), w("w_qkv", 0, x), w("b_qkv", 0, x), snk,
                                w("w_o_odd", 0, x), cos, sin, _row(P["norm_xq"][1]))
        x, h, s3 = _xattn_fwd(f"l{i}_xat", x, h, mem, _row(P["norm_xq"][i]), _row(P["norm_mem"][i]),
                              w("w_xq", i, x), w("w_xkv", i, x), w("w_xo", i, x), _row(P["norm_ffn2"][i]))
        x, h, s4 = _ffn_fwd(f"l{i}_ffn2", x, h, _row(P["norm_ffn2"][i]), w("w_ffn2_gu", i, x),
                            functools.partial(w, "w_ffn2_down", i), _row(P["norm_ffn1"][1]) if i == 0 else None)
        saved.append((s1, s2, s3, s4))

    loss, dx, dxb, d_final = _loss_head("loss_head", x, _row(P["final_norm"]), target)

    sm = {}
    dn = {k: [None, None] for k in ("norm_ffn1", "norm_mix", "norm_xq", "norm_mem", "norm_ffn2")}
    for i in (1, 0):
        s1, s2, s3, s4 = saved[i]
        dx, dxb, dn["norm_ffn2"][i] = _ffn_bwd(
            f"l{i}_ffn2", s4, dx, dxb, W["w_ffn2_gu", i], W["w_ffn2_down", i],
            lambda dwgu, dwd, i=i: putg({("w_ffn2_gu", i): dwgu, ("w_ffn2_down", i): dwd}))
        dx, dxb, dn["norm_xq"][i], dn["norm_mem"][i] = _xattn_bwd(
            f"l{i}_xat", s3, dx, dxb, W["w_xq", i], W["w_xkv", i], W["w_xo", i],
            lambda dwxq, dwxkv, dwxo, i=i: putg({("w_xq", i): dwxq, ("w_xkv", i): dwxkv, ("w_xo", i): dwxo}))
        if i == 0:
            dx, dxb, dn["norm_mix"][0], sm_even = _even_bwd(
                "l0_mix", s2, dx, dxb, w_main, w_dt, ev, W["w_out_even", 0],
                lambda dw_main, dw_dt, dwout: putg({("w_in_even", 0): (dw_main, dw_dt), ("w_out_even", 0): dwout}))
        else:
            dx, dxb, dn["norm_mix"][1], sm["b_qkv"], sm["sinks"] = _odd_bwd(
                "l1_mix", s2, dx, dxb, W["w_qkv", 0], snk, W["w_o_odd", 0], cos, sin,
                lambda dwqkv, dwo: putg({("w_qkv", 0): dwqkv, ("w_o_odd", 0): dwo}))
        dx, dxb, dn["norm_ffn1"][i] = _ffn_bwd(
            f"l{i}_ffn1", s1, dx, dxb, W["w_ffn1_gu", i], W["w_ffn1_down", i],
            lambda dwgu, dwd, i=i: putg({("w_ffn1_gu", i): dwgu, ("w_ffn1_down", i): dwd}))
    for k, v in dn.items():
        sm[k] = jnp.concatenate(v, 0)
    sm.update(gm_ln_g=sm_even["lng"], gm_ln_b=sm_even["lnb"], gm_ws=sm_even["ws"], gm_bs=sm_even["bs"],
              conv_w=sm_even["cw"], conv_b=sm_even["cb"], dt_bias=sm_even["dtb"][:, :NH], a_log=sm_even["alog"][:, :NH],
              d_skip=sm_even["dsk"][:, :NH], ssd_norm=sm_even["nrm"], sinks=sm["sinks"][:, :AH], final_norm=d_final)
    return loss[0, 0], dx, sm


def _chip_peers():
    x, y, c = lax.axis_index("x"), lax.axis_index("y"), lax.axis_index("c")
    return 2 * x + y, [((1 - x, y, c), 2 * (1 - x) + y), ((x, 1 - y, c), 2 * x + (1 - y)),
                       ((1 - x, 1 - y, c), 2 * (1 - x) + (1 - y))]


def _any_specs(n):
    return [pl.BlockSpec(memory_space=pl.ANY)] * n


_HBM = pl.BlockSpec(memory_space=pltpu.HBM)
_SEM = pl.BlockSpec(memory_space=pltpu.SEMAPHORE)
_EFFECT = pltpu.SideEffectType.DATAFLOW_SIDE_EFFECTING


def _own_slot(piece, chip):
    zone = lax.empty((4,) + piece.shape, piece.dtype)
    return lax.dynamic_update_slice(zone, piece[None], (chip,) + (0,) * piece.ndim)


def _chip_copies(srcs, lands, ssems, rsems, mode="chips"):
    c = lax.axis_index("c")
    sib = (lax.axis_index("x"), lax.axis_index("y"), 1 - c)
    if mode == "sibling":
        return [pltpu.make_async_remote_copy(src_ref=srcs[i], dst_ref=lands[i], send_sem=ssems[i].at[0],
                                             recv_sem=rsems[i].at[0], device_id=sib, device_id_type=MESH)
                for i in range(len(lands))]
    me, peers = _chip_peers()
    if mode == "half":
        return [pltpu.make_async_remote_copy(src_ref=lands[i].at[me, c], dst_ref=lands[i].at[me, c], send_sem=ssems[i].at[j],
                                             recv_sem=rsems[i].at[j], device_id=dev, device_id_type=MESH)
                for i in range(len(lands)) for j, (dev, _) in enumerate(peers)]
    if mode == "pass_on":
        return [pltpu.make_async_remote_copy(src_ref=lands[i].at[chip, c], dst_ref=lands[i].at[chip, c],
                                             send_sem=ssems[i].at[j], recv_sem=rsems[i].at[j], device_id=sib, device_id_type=MESH)
                for i in range(len(lands)) for j, (_, chip) in enumerate(peers)]
    return [pltpu.make_async_remote_copy(src_ref=lands[i].at[me] if srcs[i] is None else srcs[i].at[chip],
                                         dst_ref=lands[i].at[me], send_sem=ssems[i].at[j], recv_sem=rsems[i].at[j],
                                         device_id=dev, device_id_type=MESH)
            for i in range(len(lands)) for j, (dev, chip) in enumerate(peers)]


def _exchange_start(name, srcs, lands, mode="chips"):
    n = len(lands)
    ns = 0 if srcs is None else n

    def body(*refs):
        src_refs = [None] * n if srcs is None else refs[:n]
        land_refs = refs[ns:ns + n]
        ssems, rsems = refs[ns + n:ns + 2 * n], refs[ns + 2 * n:ns + 3 * n]
        token = refs[2 * ns + 4 * n]
        for cp in _chip_copies(src_refs, land_refs, ssems, rsems, mode):
            cp.start()
        token[...] = jnp.zeros_like(token)

    ins = ([] if srcs is None else list(srcs)) + list(lands)
    res = pl.pallas_call(
        body, name=name,
        out_shape=[pltpu.SemaphoreType.DMA((1 if mode == "sibling" else 3,))] * (2 * n) + [pltpu.HBM(a.shape, a.dtype) for a in ins]
        + [S((8, LANE), F32)],
        in_specs=[_HBM] * (ns + n),
        out_specs=[_SEM] * (2 * n) + [_HBM] * (ns + n) + [pl.BlockSpec(memory_space=pltpu.VMEM)],
        input_output_aliases={i: 2 * n + i for i in range(ns + n)},
        compiler_params=pltpu.CompilerParams(has_side_effects=_EFFECT),
    )(*[pltpu.with_memory_space_constraint(a, pltpu.HBM) for a in ins])
    items = [(res[i], res[n + i], None if srcs is None else res[2 * n + i], res[2 * n + ns + i]) for i in range(n)]
    return items, res[2 * n + ns + n]


def _exchange_wait(name, item, after, mode="chips", whole=False):
    ssem, rsem, src, land = item
    ns = 0 if src is None else 1

    def body(*refs):
        src_ref = refs[0] if ns else None
        land_ref, ssem_ref, rsem_ref = refs[ns], refs[ns + 1], refs[ns + 2]
        for cp in _chip_copies([src_ref], [land_ref], [ssem_ref], [rsem_ref], mode):
            cp.wait_send()
            cp.wait_recv()

    ins = ([src] if ns else []) + [land]
    res = pl.pallas_call(
        body, name=name, out_shape=[pltpu.HBM(a.shape, a.dtype) for a in ins],
        in_specs=[_HBM] * (ns + 1) + [_SEM, _SEM, pl.BlockSpec(memory_space=pl.ANY)], out_specs=[_HBM] * (ns + 1),
        input_output_aliases={i: i for i in range(ns + 1)}, compiler_params=pltpu.CompilerParams(has_side_effects=_EFFECT),
    )(*ins, ssem, rsem, after)
    return res if whole else res[ns]


def _gather_all(name, v, after):
    def body(v_ref, _, o_ref, ssem, rsem, lsem):
        x, y, c = lax.axis_index("x"), lax.axis_index("y"), lax.axis_index("c")
        me = 4 * x + 2 * y + c
        loc = pltpu.make_async_copy(v_ref, o_ref.at[me], lsem)
        loc.start()
        copies = []
        for k in range(1, 8):
            fx, fy, fc = (k >> 2) & 1, (k >> 1) & 1, k & 1
            dev = (x ^ fx, y ^ fy, c ^ fc)
            cp = pltpu.make_async_remote_copy(src_ref=v_ref, dst_ref=o_ref.at[me], send_sem=ssem.at[k - 1],
                                              recv_sem=rsem.at[k - 1], device_id=dev, device_id_type=MESH)
            cp.start()
            copies.append(cp)
        for cp in copies:
            cp.wait()
        loc.wait()

    return pl.pallas_call(
        body, in_specs=_any_specs(2), out_specs=pl.BlockSpec(memory_space=pl.ANY), out_shape=S((8,) + v.shape, v.dtype),
        scratch_shapes=[pltpu.SemaphoreType.DMA((7,)), pltpu.SemaphoreType.DMA((7,)), pltpu.SemaphoreType.DMA(())],
        compiler_params=pltpu.CompilerParams(has_side_effects=True), name=name)(v, after)


def _row_tile(R, row_bytes, budget=4 << 20):
    if R * row_bytes <= budget or R % 16:
        return R
    t = max(16, budget // row_bytes // 16 * 16)
    while R % t:
        t -= 16
    return t


def _sum_slots(name, r, n):
    _, R, C = r.shape
    tr = _row_tile(R, C * (n * r.dtype.itemsize + 4))

    def body(r_ref, o_ref):
        acc = r_ref[0].astype(F32)
        for j in range(1, n):
            acc = acc + r_ref[j].astype(F32)
        o_ref[...] = acc

    return pl.pallas_call(
        body, grid=(R // tr,), in_specs=[pl.BlockSpec((n, tr, C), lambda i: (0, i, 0))],
        out_specs=pl.BlockSpec((tr, C), lambda i: (i, 0)), out_shape=S((R, C), F32),
        compiler_params=_params("parallel"), name=name)(r)


def _adamw(name, w, m, v, layer, g1, g2=None, into=None):
    nl, R, C = w.shape
    tr = _row_tile(R, C * 4 * 9)
    two, has_into = g2 is not None, into is not None

    def body(w_ref, m_ref, v_ref, g1_ref, *rest):
        rest = list(rest)
        g = g1_ref[...]
        if two:
            g = g + rest.pop(0)[...]
        g_ref, d_ref, nm_ref, nv_ref = rest[-4:]
        mn = B1 * m_ref[...] + (1.0 - B1) * g
        vn = B2 * v_ref[...] + (1.0 - B2) * jnp.square(g)
        m_hat = mn / (1.0 - B1 ** STEP)
        v_hat = vn / (1.0 - B2 ** STEP)
        g_ref[...] = g
        d_ref[...] = -LR * (m_hat / (jnp.sqrt(v_hat) + AEPS) + WD * w_ref[...])
        nm_ref[...] = mn
        nv_ref[...] = vn

    blk = pl.BlockSpec((tr, C), lambda i: (i, 0))
    lay = pl.BlockSpec((None, tr, C), lambda i: (layer, i, 0))
    args = [w, m, v, g1] + ([g2] if two else []) + (list(into) if has_into else [])
    in_specs = [lay] * 3 + [blk] * (2 if two else 1) + (_any_specs(4) if has_into else [])
    aliases = {len(args) - 4 + t: t for t in range(4)} if has_into else {}
    return pl.pallas_call(
        body, grid=(R // tr,), in_specs=in_specs, out_specs=[lay] * 4, out_shape=[S((nl, R, C), F32)] * 4,
        input_output_aliases=aliases, compiler_params=_params("parallel"), name=name)(*args)


_USE_ORDER = [("w_ffn1_gu", 0), ("w_ffn1_down", 0), ("conv_w", 0), ("w_in_even", 0), ("w_out_even", 0), ("w_xq", 0),
              ("w_xkv", 0), ("w_xo", 0), ("w_ffn2_gu", 0), ("w_ffn2_down", 0), ("w_ffn1_gu", 1), ("w_ffn1_down", 1),
              ("w_qkv", 0), ("b_qkv", 0), ("w_o_odd", 0), ("w_xq", 1), ("w_xkv", 1), ("w_xo", 1), ("w_ffn2_gu", 1),
              ("w_ffn2_down", 1)]
_SMALL = ["norm_ffn1", "norm_mix", "gm_ln_g", "gm_ln_b", "gm_ws", "gm_bs", "conv_w", "conv_b", "dt_bias", "a_log",
          "d_skip", "ssd_norm", "b_qkv", "sinks", "norm_xq", "norm_mem", "norm_ffn2", "final_norm"]
_WEIGHTS = ["norm_ffn1", "w_ffn1_gu", "w_ffn1_down", "norm_mix", "w_in_even", "gm_ln_g", "gm_ln_b", "gm_ws", "gm_bs",
            "conv_w", "conv_b", "dt_bias", "a_log", "d_skip", "ssd_norm", "w_out_even", "w_qkv", "b_qkv", "sinks",
            "w_o_odd", "norm_xq", "norm_mem", "w_xq", "w_xkv", "w_xo", "norm_ffn2", "w_ffn2_gu", "w_ffn2_down",
            "final_norm"]


def _pack(arrs):
    rows = []
    for a in arrs:
        f = a.reshape(-1).astype(F32)
        pad = (-f.shape[0]) % LANE
        rows.append(jnp.pad(f, (0, pad)).reshape(-1, LANE))
    out = jnp.concatenate(rows, 0)
    pad = (-out.shape[0]) % 8
    return jnp.pad(out, ((0, pad), (0, 0)))


def _unpack(packed, shapes):
    outs, r = [], 0
    for shp in shapes:
        n = math.prod(shp)
        nr = -(-n // LANE)
        outs.append(packed[r:r + nr].reshape(-1)[:n].reshape(shp))
        r += nr
    return outs


def kernel(x, mem, positions, norm_ffn1, w_ffn1_gu, w_ffn1_down, norm_mix, w_in_even, gm_ln_g, gm_ln_b, gm_ws, gm_bs, conv_w, conv_b, dt_bias, a_log, d_skip, ssd_norm, w_out_even, w_qkv, b_qkv, sinks, w_o_odd, norm_xq, norm_mem, w_xq, w_xkv, w_xo, norm_ffn2, w_ffn2_gu, w_ffn2_down, final_norm, loss_target, m_norm_ffn1, m_w_ffn1_gu, m_w_ffn1_down, m_norm_mix, m_w_in_even, m_gm_ln_g, m_gm_ln_b, m_gm_ws, m_gm_bs, m_conv_w, m_conv_b, m_dt_bias, m_a_log, m_d_skip, m_ssd_norm, m_w_out_even, m_w_qkv, m_b_qkv, m_sinks, m_w_o_odd, m_norm_xq, m_norm_mem, m_w_xq, m_w_xkv, m_w_xo, m_norm_ffn2, m_w_ffn2_gu, m_w_ffn2_down, m_final_norm, v_norm_ffn1, v_w_ffn1_gu, v_w_ffn1_down, v_norm_mix, v_w_in_even, v_gm_ln_g, v_gm_ln_b, v_gm_ws, v_gm_bs, v_conv_w, v_conv_b, v_dt_bias, v_a_log, v_d_skip, v_ssd_norm, v_w_out_even, v_w_qkv, v_b_qkv, v_sinks, v_w_o_odd, v_norm_xq, v_norm_mem, v_w_xq, v_w_xkv, v_w_xo, v_norm_ffn2, v_w_ffn2_gu, v_w_ffn2_down, v_final_norm):
    a = dict(locals())
    w = {k: a[k] for k in _WEIGHTS}
    mom = {k: a["m_" + k] for k in _WEIGHTS}
    var = {k: a["v_" + k] for k in _WEIGHTS}
    chip = 2 * lax.axis_index("x") + lax.axis_index("y")

    shards = [w[k][i:i + 1] if k in ("conv_w", "b_qkv") else w[k][i:i + 1].astype(BF16) for k, i in _USE_ORDER]
    half = shards[0].reshape((2, shards[0].shape[1] // 2) + shards[0].shape[2:])
    (first,), first_started = _exchange_start("gather_start_first", None, [_own_slot(half, chip)], "half")
    small = _USE_ORDER.index(("conv_w", 0))
    shards[small] = shards[small] + first_started[0, 0]
    rest, rest_started = _exchange_start("gather_start_rest", None, [_own_slot(s, chip) for s in shards[1:]])
    pending = dict(zip(_USE_ORDER[1:], rest))

    def getw(name, layer, after):
        if (name, layer) == _USE_ORDER[0]:
            zone = _exchange_wait("gather_wait_first_half", first, after, "half")
            (passed,), _ = _exchange_start("gather_pass_on_first", None, [zone], "pass_on")
            return _exchange_wait("gather_wait_first_passed", passed, after, "pass_on").reshape((4,) + shards[0].shape)
        got = _exchange_wait(f"gather_wait_{name}_{layer}", pending.pop((name, layer)), after)
        if name == "w_in_even":
            w_in = jnp.transpose(got[:, 0], (1, 0, 2)).reshape(D, EVEN_IN)
            return w_in[:, :EVEN_MAIN], jnp.pad(w_in[:, EVEN_MAIN:], ((0, 0), (0, LANE - (EVEN_IN - EVEN_MAIN))))
        if name == "conv_w":
            return jnp.transpose(got[:, 0], (1, 0, 2)).reshape(4, CONV_C)
        if name == "b_qkv":
            return got.reshape(1, ODD_IN)
        return got

    sent = []

    def putg(grads):
        names, arrs = [], []
        for (name, layer), g in grads.items():
            if name == "w_in_even":
                dw_in = jnp.concatenate([g[0], g[1][:, :EVEN_IN - EVEN_MAIN]], 1)
                g = jnp.transpose(dw_in.reshape(D, 4, EVEN_IN // 4), (1, 0, 2)).reshape(4, 1, D, EVEN_IN // 4)
            names.append((name, layer))
            arrs.append(g)
        own = [_own_slot(lax.dynamic_index_in_dim(g, chip, 0, keepdims=False), chip) for g in arrs]
        its, tok = _exchange_start(f"scatter_start_{names[0][0]}_{names[0][1]}", arrs, own)
        sent.append(list(zip(names, its)))
        return tok[0, 0]

    P = {k: w[k] for k in _SMALL}
    P["norm_ffn1"] = P["norm_ffn1"] + rest_started[0, 0]
    loss, grad_x, sm = _local_step(x[0], mem[0], positions[0], loss_target[0], getw, P, putg)
    loss = lax.psum(loss, ("x", "y", "c"))

    out = {}

    def update(groups, after):
        flying = []
        for grp in groups:
            part = []
            for (name, layer), it in grp:
                r = _exchange_wait(f"scatter_wait_{name}_{layer}", it, after)
                part.append(_sum_slots(f"sum_{name}_{layer}", r.reshape(4, -1, r.shape[-1]), 4))
            name0, layer0 = grp[0][0]
            its, after = _exchange_start(f"swap_start_{name0}_{layer0}", part, [lax.empty(p.shape, p.dtype) for p in part],
                                         "sibling")
            flying += [(nm, it) for (nm, _), it in zip(grp, its)]
        for (name, layer), it in flying:
            p1, p2 = _exchange_wait(f"swap_wait_{name}_{layer}", it, after, "sibling", whole=True)
            out[name] = _adamw(f"adamw_{name}_{layer}", w[name], mom[name], var[name], layer, p1, p2, out.get(name))
            after = out[name][0]

    update(sent[:-1], grad_x)
    done_a = out["w_out_even"][0]
    update(sent[-1:], done_a)

    full_shapes = {k: w[k].shape for k in _SMALL}
    full_shapes["conv_w"], full_shapes["b_qkv"] = (1, 4, CONV_C), (1, ODD_IN)
    packed = _pack([sm[k] for k in _SMALL])
    total = _sum_slots("sum_small", _gather_all("gather_small", packed, done_a), 8)
    gs = dict(zip(_SMALL, _unpack(total, [full_shapes[k] for k in _SMALL])))
    gs["conv_w"] = lax.dynamic_slice_in_dim(gs["conv_w"], chip * (CONV_C // 4), CONV_C // 4, 2)
    gs["b_qkv"] = lax.dynamic_slice_in_dim(gs["b_qkv"], chip * (ODD_IN // 4), ODD_IN // 4, 1)
    res = _adamw("adamw_small", _pack([w[k] for k in _SMALL])[None], _pack([mom[k] for k in _SMALL])[None],
                 _pack([var[k] for k in _SMALL])[None], 0, _pack([gs[k] for k in _SMALL]))
    shapes = [w[k].shape for k in _SMALL]
    for k, g, d, nm, nv in zip(_SMALL, *[_unpack(r[0], shapes) for r in res]):
        out[k] = [g, d, nm, nv]

    return (loss, grad_x[None], *[out[k][0] for k in _WEIGHTS], *[out[k][1] for k in _WEIGHTS],
            *[out[k][2] for k in _WEIGHTS], *[out[k][3] for k in _WEIGHTS])
```

```python
import functools
import math

import jax
import jax.numpy as jnp
from jax import lax
from jax.experimental import pallas as pl
from jax.experimental.pallas import tpu as pltpu

F32, BF16 = jnp.float32, jnp.bfloat16
S = jax.ShapeDtypeStruct
MESH = pl.DeviceIdType.MESH

D = 2048
DFF = 5632
EPS = 1e-5
CH = 128
GM_G, GM_GD = 4, 512
NH, HD, NG, HPG, NS = 32, 64, 4, 8, 128
CONV_C = 3072
EVEN_MAIN, EVEN_IN = 9216, 9248
AH, AKV, AREP, AHD = 32, 4, 8, 64
ODD_IN = 2560
XH, XHD, XW = 4, 128, 512
ATT_SCALE = AHD ** -0.5
X_SCALE = XHD ** -0.5
ROPE_THETA = 500000.0
ROT = 16
LR, B1, B2, AEPS, WD, STEP = 0.001, 0.9, 0.999, 1e-08, 0.01, 10
LANE = 128
VMEM_LIMIT_V7X = 56 * 1024 * 1024


def _params(*sem):
    return pltpu.CompilerParams(dimension_semantics=sem, vmem_limit_bytes=VMEM_LIMIT_V7X)


def _tile(dim, target):
    if dim <= target:
        return dim
    t = (target // LANE) * LANE
    while t > LANE and dim % t:
        t -= LANE
    assert dim % t == 0, (dim, target)
    return t


class Op:
    def __init__(self, arr, kind=None, layer=0):
        self.arr, self.kind, self.layer = arr, kind, layer
        if kind is None:
            self.R, self.C = arr.shape
        else:
            L = arr.shape[0]
            self.R = arr.shape[2] * (L if kind == "r" else 1)
            self.C = arr.shape[3] * (L if kind == "c" else 1)

    def unit(self, axis):
        if self.kind == "r" and axis == 0:
            return self.arr.shape[2]
        if self.kind == "c" and axis == 1:
            return self.arr.shape[3]
        return (self.R, self.C)[axis]

    def spec(self, tr, tc, pick):
        if self.kind is None:
            return pl.BlockSpec((tr, tc), lambda i, j, k: pick(i, j, k))
        l = self.layer
        if self.kind == "c":
            per = self.arr.shape[3] // tc
            return pl.BlockSpec((None, None, tr, tc),
                                lambda i, j, k: (pick(i, j, k)[1] // per, l, pick(i, j, k)[0], pick(i, j, k)[1] % per))
        per = self.arr.shape[2] // tr
        return pl.BlockSpec((None, None, tr, tc),
                            lambda i, j, k: (pick(i, j, k)[0] // per, l, pick(i, j, k)[0] % per, pick(i, j, k)[1]))


_DIMS = {"nn": (((1,), (0,)), ((), ())), "nt": (((1,), (1,)), ((), ())), "tn": (((0,), (0,)), ((), ()))}
_PICK_A = {"nn": lambda i, j, k: (i, k), "nt": lambda i, j, k: (i, k), "tn": lambda i, j, k: (k, i)}
_PICK_B = {"nn": lambda i, j, k: (k, j), "nt": lambda i, j, k: (j, k), "tn": lambda i, j, k: (k, j)}


def _mm(name, mode, a, b, out_dtype, *, out=None, res=None, bias=None, scale=1.0, norm_gain=None,
        tm_t=1024, tn_t=1024, tk_t=2048):
    if not isinstance(a, Op):
        a = Op(a)
    if not isinstance(b, Op):
        b = Op(b)
    if mode == "nn":
        M, K, N = a.R, a.C, b.C
        assert b.R == K
        um, uk, un = a.unit(0), math.gcd(a.unit(1), b.unit(0)), b.unit(1)
    elif mode == "nt":
        M, K, N = a.R, a.C, b.R
        assert b.C == K
        um, uk, un = a.unit(0), math.gcd(a.unit(1), b.unit(1)), b.unit(0)
    else:
        K, M, N = a.R, a.C, b.C
        assert b.R == K
        um, uk, un = a.unit(1), math.gcd(a.unit(0), b.unit(0)), b.unit(1)
    if out is not None:
        okind, oL, olayers, olayer = out
        if okind == "c":
            un = math.gcd(un, N // oL)
        else:
            um = math.gcd(um, M // oL)
    tm, tn, tk = _tile(um, tm_t), _tile(un, tn_t), _tile(uk, tk_t)
    gi, gj, gk = M // tm, N // tn, K // tk
    a_blk = (tm, tk) if mode != "tn" else (tk, tm)
    b_blk = {"nn": (tk, tn), "nt": (tn, tk), "tn": (tk, tn)}[mode]
    in_specs = [a.spec(*a_blk, _PICK_A[mode]), b.spec(*b_blk, _PICK_B[mode])]
    args = [a.arr, b.arr]
    if res is not None:
        in_specs.append(pl.BlockSpec((tm, tn), lambda i, j, k: (i, j)))
        args.append(res)
    if bias is not None:
        in_specs.append(pl.BlockSpec((1, tn), lambda i, j, k: (0, j)))
        args.append(bias)
    if out is None:
        out_shape = S((M, N), out_dtype)
        out_spec = pl.BlockSpec((tm, tn), lambda i, j, k: (i, j))
    else:
        shp = (oL, olayers, M, N // oL) if okind == "c" else (oL, olayers, M // oL, N)
        out_shape = S(shp, out_dtype)
        out_spec = Op(out_shape, okind, olayer).spec(tm, tn, lambda i, j, k: (i, j))
    has_res, has_bias, has_norm = res is not None, bias is not None, norm_gain is not None
    if has_norm:
        assert out is None and tn == N
        in_specs.append(pl.BlockSpec((1, N), lambda i, j, k: (0, 0)))
        args.append(norm_gain)
        out_shape = [out_shape, S((M, N), BF16)]
        out_spec = [out_spec, pl.BlockSpec((tm, tn), lambda i, j, k: (i, j))]
    dims = _DIMS[mode]

    def body(a_ref, b_ref, *rest):
        rest = list(rest)
        res_ref = rest.pop(0) if has_res else None
        bias_ref = rest.pop(0) if has_bias else None
        gain_ref = rest.pop(0) if has_norm else None
        o_ref = rest.pop(0)
        h_ref = rest.pop(0) if has_norm else None
        part = lax.dot_general(a_ref[...].astype(BF16), b_ref[...].astype(BF16), dims, preferred_element_type=F32)

        def finish(r):
            if scale != 1.0:
                r = r * scale
            if has_bias:
                r = r + bias_ref[...]
            if has_res:
                r = r + res_ref[...]
            o_ref[...] = r.astype(o_ref.dtype)
            if has_norm:
                h_ref[...] = (r * lax.rsqrt(jnp.mean(r * r, -1, keepdims=True) + EPS) * gain_ref[...]).astype(BF16)

        if gk == 1:
            finish(part)
            return
        acc, = rest
        k = pl.program_id(2)

        @pl.when(k == 0)
        def _():
            acc[...] = part

        @pl.when((k > 0) & (k < gk - 1))
        def _():
            acc[...] += part

        @pl.when(k == gk - 1)
        def _():
            finish(acc[...] + part)

    return pl.pallas_call(
        body, grid=(gi, gj, gk), in_specs=in_specs, out_specs=out_spec, out_shape=out_shape,
        scratch_shapes=[pltpu.VMEM((tm, tn), F32)] if gk > 1 else [],
        compiler_params=_params("parallel", "parallel", "arbitrary"), name=name)(*args)


def _rms_fwd(name, x, gain):
    T = x.shape[0]
    tt = _tile(T, 512)

    def body(x_ref, g_ref, o_ref):
        xv = x_ref[...]
        r = lax.rsqrt(jnp.mean(xv * xv, -1, keepdims=True) + EPS)
        o_ref[...] = (xv * r * g_ref[...]).astype(BF16)

    return pl.pallas_call(
        body, grid=(T // tt,),
        in_specs=[pl.BlockSpec((tt, D), lambda i: (i, 0)), pl.BlockSpec((1, D), lambda i: (0, 0))],
        out_specs=pl.BlockSpec((tt, D), lambda i: (i, 0)), out_shape=S((T, D), BF16),
        compiler_params=_params("parallel"), name=name)(x, gain)


def _rms_bwd(name, x, gain, dh, dx_in=None):
    T = x.shape[0]
    tt = _tile(T, 512)
    has_in = dx_in is not None

    def body(x_ref, g_ref, dh_ref, *rest):
        rest = list(rest)
        dxin_ref = rest.pop(0) if has_in else None
        dx_ref, dxb_ref, dg_ref = rest
        xv = x_ref[...]
        r = lax.rsqrt(jnp.mean(xv * xv, -1, keepdims=True) + EPS)
        xh = xv * r
        dy = dh_ref[...].astype(F32)
        dxh = dy * g_ref[...]
        dx = r * (dxh - xh * jnp.mean(dxh * xh, -1, keepdims=True))
        if has_in:
            dx = dx + dxin_ref[...]
        dx_ref[...] = dx
        dxb_ref[...] = dx.astype(BF16)
        part = jnp.sum(dy * xh, 0, keepdims=True)

        @pl.when(pl.program_id(0) == 0)
        def _():
            dg_ref[...] = part

        @pl.when(pl.program_id(0) > 0)
        def _():
            dg_ref[...] += part

    row = pl.BlockSpec((tt, D), lambda i: (i, 0))
    vec = pl.BlockSpec((1, D), lambda i: (0, 0))
    args = [x, gain, dh] + ([dx_in] if has_in else [])
    return pl.pallas_call(
        body, grid=(T // tt,), in_specs=[row, vec, row] + ([row] if has_in else []),
        out_specs=[row, row, vec], out_shape=[S((T, D), F32), S((T, D), BF16), S((1, D), F32)],
        compiler_params=_params("arbitrary"), name=name)(*args)


def _sigmoid(x):
    return 0.5 * jnp.tanh(0.5 * x) + 0.5


def _ffn_up(name, h, w4, layer, tm_t=512, tn_t=1408):
    T = h.shape[0]
    n_sh = w4.shape[3]
    tm, tn = _tile(T, tm_t), _tile(n_sh, tn_t)
    per = n_sh // tn

    def body(h_ref, wg_ref, wu_ref, g_ref, u_ref, a_ref):
        hv = h_ref[...]
        g = jnp.dot(hv, wg_ref[...], preferred_element_type=F32)
        u = jnp.dot(hv, wu_ref[...], preferred_element_type=F32)
        g_ref[...] = g.astype(BF16)
        u_ref[...] = u.astype(BF16)
        a_ref[...] = (g * _sigmoid(g) * u).astype(BF16)

    o = pl.BlockSpec((tm, tn), lambda j, i: (i, j))
    return pl.pallas_call(
        body, grid=(DFF // tn, T // tm),
        in_specs=[pl.BlockSpec((tm, D), lambda j, i: (i, 0)),
                  pl.BlockSpec((None, None, D, tn), lambda j, i: (j // per, layer, 0, j % per)),
                  pl.BlockSpec((None, None, D, tn), lambda j, i: (2 + j // per, layer, 0, j % per))],
        out_specs=[o, o, o], out_shape=[S((T, DFF), BF16)] * 3,
        compiler_params=_params("parallel", "parallel"), name=name)(h, w4, w4)


def _ffn_dact(name, dxb, wd4, layer, g, u, tm_t=512):
    T = dxb.shape[0]
    r_sh = wd4.shape[2]
    tm, tn = _tile(T, tm_t), _tile(r_sh, 1408)
    per = r_sh // tn

    def body(dx_ref, w_ref, g_ref, u_ref, o_ref):
        da = 0.5 * lax.dot_general(dx_ref[...], w_ref[...], _DIMS["nt"], preferred_element_type=F32)
        gv, uv = g_ref[...].astype(F32), u_ref[...].astype(F32)
        sg = _sigmoid(gv)
        o_ref[0, 0] = (da * uv * sg * (1.0 + gv * (1.0 - sg))).astype(BF16)
        o_ref[1, 0] = (da * gv * sg).astype(BF16)

    t = pl.BlockSpec((tm, tn), lambda j, i: (i, j))
    return pl.pallas_call(
        body, grid=(DFF // tn, T // tm),
        in_specs=[pl.BlockSpec((tm, D), lambda j, i: (i, 0)),
                  pl.BlockSpec((None, None, tn, D), lambda j, i: (j // per, layer, j % per, 0)), t, t],
        out_specs=pl.BlockSpec((2, 1, tm, tn), lambda j, i: (0, 0, i, j)), out_shape=S((2, 1, T, DFF), BF16),
        compiler_params=_params("parallel", "parallel"), name=name)(dxb, wd4, g, u)


def _gelu(x):
    return 0.5 * x * (1.0 + lax.erf(x * 0.7071067811865476))


def _causal(n):
    return lax.broadcasted_iota(jnp.int32, (n, n), 0) >= lax.broadcasted_iota(jnp.int32, (n, n), 1)


def _gmlp_math(u_raw, v_raw, lng, lnb, ws, bs):
    causal = _causal(CH)
    outs = []
    for g in range(GM_G):
        u, v = _gelu(u_raw[g]), _gelu(v_raw[g])
        mu = jnp.mean(v, -1, keepdims=True)
        var = jnp.mean(jnp.square(v - mu), -1, keepdims=True)
        vn = (v - mu) * lax.rsqrt(var + EPS) * lng[g] + lnb[g]
        wm = jnp.where(causal, ws[g], 0.0)
        s = jnp.dot(wm.astype(BF16), vn.astype(BF16), preferred_element_type=F32) + bs[g]
        outs.append(u * s)
    return outs


def _gmlp_load(proj_ref, lng_ref, lnb_ref, ws_ref, bs_ref):
    sl = lambda g, off: slice(off + g * GM_GD, off + (g + 1) * GM_GD)
    u_raw = [proj_ref[:, sl(g, 0)].astype(F32) for g in range(GM_G)]
    v_raw = [proj_ref[:, sl(g, D)].astype(F32) for g in range(GM_G)]
    lng = [lng_ref[:, sl(g, 0)] for g in range(GM_G)]
    lnb = [lnb_ref[:, sl(g, 0)] for g in range(GM_G)]
    ws = [ws_ref[g] for g in range(GM_G)]
    bs = [bs_ref[g] for g in range(GM_G)]
    return u_raw, v_raw, lng, lnb, ws, bs


_GM_PAR = lambda: [pl.BlockSpec((1, D), lambda i: (0, 0)), pl.BlockSpec((1, D), lambda i: (0, 0)),
                   pl.BlockSpec((GM_G, CH, CH), lambda i: (0, 0, 0)), pl.BlockSpec((GM_G, CH, 1), lambda i: (0, 0, 0))]


def _gmlp_fwd(name, proj, lng, lnb, ws, bs):
    T = proj.shape[0]

    def body(proj_ref, lng_ref, lnb_ref, ws_ref, bs_ref, o_ref):
        outs = _gmlp_math(*_gmlp_load(proj_ref, lng_ref, lnb_ref, ws_ref, bs_ref))
        for g in range(GM_G):
            o_ref[:, g * GM_GD:(g + 1) * GM_GD] = outs[g].astype(BF16)

    return pl.pallas_call(
        body, grid=(T // CH,), in_specs=[pl.BlockSpec((CH, 2 * D), lambda i: (i, 0))] + _GM_PAR(),
        out_specs=pl.BlockSpec((CH, D), lambda i: (i, 0)), out_shape=S((T, 2 * D), BF16),
        compiler_params=_params("parallel"), name=name)(proj, lng, lnb, ws, bs)


def _acc_store(first, ref, idx, val):
    @pl.when(first)
    def _():
        ref[idx] = val

    @pl.when(jnp.logical_not(first))
    def _():
        ref[idx] += val


def _gmlp_bwd(name, proj, lng, lnb, ws, bs, dmix, dproj):
    T = proj.shape[0]

    def body(proj_ref, lng_ref, lnb_ref, ws_ref, bs_ref, dmix_ref, _, dproj_ref, dlng_ref, dlnb_ref, dws_ref, dbs_ref):
        first = pl.program_id(0) == 0
        prim = _gmlp_load(proj_ref, lng_ref, lnb_ref, ws_ref, bs_ref)
        _, vjp = jax.vjp(_gmlp_math, *prim)
        du, dv, dlng, dlnb, dws, dbs = vjp([dmix_ref[:, g * GM_GD:(g + 1) * GM_GD].astype(F32) for g in range(GM_G)])
        for g in range(GM_G):
            sl = slice(g * GM_GD, (g + 1) * GM_GD)
            dproj_ref[:, sl] = du[g].astype(BF16)
            dproj_ref[:, D + g * GM_GD:D + (g + 1) * GM_GD] = dv[g].astype(BF16)
            _acc_store(first, dlng_ref, (slice(None), sl), dlng[g])
            _acc_store(first, dlnb_ref, (slice(None), sl), dlnb[g])
            _acc_store(first, dws_ref, g, dws[g])
            _acc_store(first, dbs_ref, g, dbs[g])

    par = _GM_PAR()
    return pl.pallas_call(
        body, grid=(T // CH,),
        in_specs=[pl.BlockSpec((CH, 2 * D), lambda i: (i, 0))] + par +
                 [pl.BlockSpec((CH, D), lambda i: (i, 0)), pl.BlockSpec(memory_space=pl.ANY)],
        out_specs=[pl.BlockSpec((CH, 2 * D), lambda i: (i, 0))] + par,
        out_shape=[S(dproj.shape, BF16), S((1, D), F32), S((1, D), F32), S((GM_G, CH, CH), F32), S((GM_G, CH, 1), F32)],
        input_output_aliases={6: 0}, compiler_params=_params("arbitrary"), name=name)(proj, lng, lnb, ws, bs, dmix, dproj)


CONV_TT = 256
HALO = 8


def _shift_rows(cur, halo_after, s):
    if s == 0:
        return cur
    n = cur.shape[0]
    return pltpu.roll(jnp.concatenate([cur, halo_after], 0), s, 0)[:n]


def _conv_fwd(name, proj, w, b):
    T = proj.shape[0]
    tt = _tile(T, CONV_TT)
    hb = tt // HALO

    def body(x_ref, halo_ref, w_ref, b_ref, y_ref, xc_ref):
        i = pl.program_id(0)
        x = x_ref[...].astype(F32)
        halo = halo_ref[...].astype(F32) * (i > 0).astype(F32)
        y = b_ref[...] + w_ref[3:4, :] * x
        for s in (1, 2, 3):
            y = y + w_ref[3 - s:4 - s, :] * _shift_rows(x, halo, s)
        y_ref[...] = y.astype(BF16)
        xc_ref[...] = (y * _sigmoid(y)).astype(BF16)

    o = pl.BlockSpec((tt, CONV_C), lambda i: (i, 0))
    return pl.pallas_call(
        body, grid=(T // tt,),
        in_specs=[pl.BlockSpec((tt, CONV_C), lambda i: (i, 2)),
                  pl.BlockSpec((HALO, CONV_C), lambda i: (jnp.maximum(i * hb - 1, 0), 2)),
                  pl.BlockSpec((4, CONV_C), lambda i: (0, 0)), pl.BlockSpec((1, CONV_C), lambda i: (0, 0))],
        out_specs=[o, o], out_shape=[S((T, CONV_C), BF16)] * 2,
        compiler_params=_params("parallel"), name=name)(proj, proj, w, b)


def _conv_bwd(name, proj, ypre, dxc, w, dproj):
    T = proj.shape[0]
    tt = _tile(T, CONV_TT)
    hb = tt // HALO
    nt = T // tt

    def dsilu(y):
        sg = _sigmoid(y)
        return sg * (1.0 + y * (1.0 - sg))

    def body(x_ref, xh_ref, y_ref, yn_ref, d_ref, dn_ref, w_ref, _, dproj_ref, dw_ref, db_ref):
        i = pl.program_id(0)
        first = i == 0
        x = x_ref[...].astype(F32)
        halo = xh_ref[...].astype(F32) * (i > 0).astype(F32)
        dy = d_ref[...].astype(F32) * dsilu(y_ref[...].astype(F32))
        dyn = dn_ref[...].astype(F32) * dsilu(yn_ref[...].astype(F32)) * (i < nt - 1).astype(F32)
        ext = jnp.concatenate([dy, dyn], 0)
        dx = w_ref[3:4, :] * dy
        _acc_store(first, dw_ref, (slice(3, 4), slice(None)), jnp.sum(x * dy, 0, keepdims=True))
        for s in (1, 2, 3):
            dx = dx + w_ref[3 - s:4 - s, :] * pltpu.roll(ext, tt + HALO - s, 0)[:tt]
            _acc_store(first, dw_ref, (slice(3 - s, 4 - s), slice(None)),
                       jnp.sum(_shift_rows(x, halo, s) * dy, 0, keepdims=True))
        _acc_store(first, db_ref, (slice(None), slice(None)), jnp.sum(dy, 0, keepdims=True))
        dproj_ref[...] = dx.astype(BF16)

    cur = pl.BlockSpec((tt, CONV_C), lambda i: (i, 0))
    nxt = pl.BlockSpec((HALO, CONV_C), lambda i: (jnp.minimum((i + 1) * hb, T // HALO - 1), 0))
    return pl.pallas_call(
        body, grid=(nt,),
        in_specs=[pl.BlockSpec((tt, CONV_C), lambda i: (i, 2)),
                  pl.BlockSpec((HALO, CONV_C), lambda i: (jnp.maximum(i * hb - 1, 0), 2)),
                  cur, nxt, cur, nxt, pl.BlockSpec((4, CONV_C), lambda i: (0, 0)), pl.BlockSpec(memory_space=pl.ANY)],
        out_specs=[pl.BlockSpec((tt, CONV_C), lambda i: (i, 2)), pl.BlockSpec((4, CONV_C), lambda i: (0, 0)),
                   pl.BlockSpec((1, CONV_C), lambda i: (0, 0))],
        out_shape=[S(dproj.shape, BF16), S((4, CONV_C), F32), S((1, CONV_C), F32)],
        input_output_aliases={7: 0}, compiler_params=_params("arbitrary"), name=name)(proj, proj, ypre, ypre, dxc, dxc, w, dproj)


def _softplus(x):
    return jnp.maximum(x, 0.0) + jnp.log(1.0 + jnp.exp(-jnp.abs(x)))


def _ssd_math(x, Bm, Cm, dtr, z, prev, dtb, alog, dsk, nrm):
    hi = lax.Precision.HIGHEST
    causal = _causal(CH)
    tri = causal.astype(F32)
    lane = lax.broadcasted_iota(jnp.int32, (1, LANE), 1)
    sub = lax.broadcasted_iota(jnp.int32, (LANE, 1), 0)
    dt = _softplus(dtr + dtb)
    a = dt * (-jnp.exp(alog))
    a_cs = jnp.dot(tri, a, preferred_element_type=F32, precision=hi)
    a_csT = lax.dot_general(a, tri, (((0,), (1,)), ((), ())), preferred_element_type=F32, precision=hi)
    a_last = jnp.sum(a, 0, keepdims=True)
    gw = HPG * HD
    outs, new = [], []
    for g in range(NG):
        spread = (lax.broadcasted_iota(jnp.int32, (LANE, gw), 0)
                  == g * HPG + lax.broadcasted_iota(jnp.int32, (LANE, gw), 1) // HD).astype(F32)
        to_lanes = lambda v: jnp.dot(v, spread, preferred_element_type=F32, precision=hi)
        col_e, dt_e, last_e, dsk_e = to_lanes(a_cs), to_lanes(dt), to_lanes(a_last), to_lanes(dsk)
        last_r = lax.dot_general(spread, a_last, (((0,), (1,)), ((), ())), preferred_element_type=F32, precision=hi)
        cb = lax.dot_general(Cm[g].astype(BF16), Bm[g].astype(BF16), _DIMS["nt"], preferred_element_type=F32)
        xg = jnp.concatenate(x[g * HPG:(g + 1) * HPG], 1)
        yd = []
        for h in range(g * HPG, (g + 1) * HPG):
            ohl = (lane == h).astype(F32)
            col = jnp.sum(a_cs * ohl, 1, keepdims=True)
            row = jnp.sum(a_csT * (sub == h).astype(F32), 0, keepdims=True)
            dtc = jnp.sum(dt * ohl, 1, keepdims=True)
            lmat = jnp.where(causal, jnp.exp(jnp.where(causal, col - row, 0.0)), 0.0)
            yd.append(jnp.dot((cb * lmat).astype(BF16), (x[h] * dtc).astype(BF16), preferred_element_type=F32))
        y = jnp.concatenate(yd, 1)
        y = y + jnp.exp(col_e) * lax.dot_general(Cm[g].astype(BF16), prev[g].astype(BF16), _DIMS["nt"],
                                                 preferred_element_type=F32)
        st = lax.dot_general((xg * dt_e * jnp.exp(last_e - col_e)).astype(BF16), Bm[g].astype(BF16), _DIMS["tn"],
                             preferred_element_type=F32)
        new.append(prev[g] * jnp.exp(last_r) + st)
        yg = (y + xg * dsk_e) * (z[g] * _sigmoid(z[g]))
        yg = yg * lax.rsqrt(jnp.mean(yg * yg, -1, keepdims=True) + EPS)
        outs.append(yg * nrm[g])
    return outs, new


def _ssd_load(xc_ref, dtr_ref, z_ref, state_ref, dtb_ref, alog_ref, dsk_ref, nrm_ref):
    gw = HPG * HD
    x = [xc_ref[:, h * HD:(h + 1) * HD].astype(F32) for h in range(NH)]
    Bm = [xc_ref[:, D + g * NS:D + (g + 1) * NS].astype(F32) for g in range(NG)]
    Cm = [xc_ref[:, D + NG * NS + g * NS:D + NG * NS + (g + 1) * NS].astype(F32) for g in range(NG)]
    z = [z_ref[:, g * gw:(g + 1) * gw].astype(F32) for g in range(NG)]
    prev = [state_ref[g * gw:(g + 1) * gw, :] for g in range(NG)]
    nrm = [nrm_ref[:, g * gw:(g + 1) * gw] for g in range(NG)]
    return x, Bm, Cm, dtr_ref[...], z, prev, dtb_ref[...], alog_ref[...], dsk_ref[...], nrm


_SSD_PAR = lambda: [pl.BlockSpec((1, LANE), lambda c: (0, 0))] * 3 + [pl.BlockSpec((1, D), lambda c: (0, 0))]


def _ssd_fwd(name, xc, dtr, proj, dtb, alog, dsk, nrm, mix):
    T = xc.shape[0]
    nc = T // CH

    def body(xc_ref, dtr_ref, z_ref, dtb_ref, alog_ref, dsk_ref, nrm_ref, _, mix_ref, prev_ref, state):
        @pl.when(pl.program_id(0) == 0)
        def _():
            state[...] = jnp.zeros_like(state)

        prev_ref[...] = state[...]
        outs, new = _ssd_math(*_ssd_load(xc_ref, dtr_ref, z_ref, state, dtb_ref, alog_ref, dsk_ref, nrm_ref))
        for g in range(NG):
            mix_ref[:, g * 512:(g + 1) * 512] = outs[g].astype(BF16)
            state[g * 512:(g + 1) * 512, :] = new[g]

    return pl.pallas_call(
        body, grid=(nc,),
        in_specs=[pl.BlockSpec((CH, CONV_C), lambda c: (c, 0)), pl.BlockSpec((CH, LANE), lambda c: (c, 0)),
                  pl.BlockSpec((CH, D), lambda c: (c, 2))] + _SSD_PAR() + [pl.BlockSpec(memory_space=pl.ANY)],
        out_specs=[pl.BlockSpec((CH, D), lambda c: (c, 1)), pl.BlockSpec((None, NH * HD, NS), lambda c: (c, 0, 0))],
        out_shape=[S(mix.shape, BF16), S((nc, NH * HD, NS), F32)],
        scratch_shapes=[pltpu.VMEM((NH * HD, NS), F32)], input_output_aliases={7: 0},
        compiler_params=_params("arbitrary"), name=name)(xc, dtr, proj, dtb, alog, dsk, nrm, mix)


def _ssd_bwd(name, xc, dtr, proj, prevs, dtb, alog, dsk, nrm, dmix, dproj):
    T = xc.shape[0]
    nc = T // CH
    rev = lambda c: nc - 1 - c

    def body(xc_ref, dtr_ref, z_ref, prev_ref, dtb_ref, alog_ref, dsk_ref, nrm_ref, dmix_ref, _,
             dproj_ref, dxc_ref, ddtr_ref, ddtb_ref, dalog_ref, ddsk_ref, dnrm_ref, dstate):
        first = pl.program_id(0) == 0

        @pl.when(first)
        def _():
            dstate[...] = jnp.zeros_like(dstate)

        prim = _ssd_load(xc_ref, dtr_ref, z_ref, prev_ref, dtb_ref, alog_ref, dsk_ref, nrm_ref)
        _, vjp = jax.vjp(_ssd_math, *prim)
        douts = [dmix_ref[:, g * 512:(g + 1) * 512].astype(F32) for g in range(NG)]
        dnew = [dstate[g * 512:(g + 1) * 512, :] for g in range(NG)]
        dx, dB, dC, ddtr, dz, dprev, ddtb, dalog, ddsk, dnrm = vjp((douts, dnew))
        for h in range(NH):
            dxc_ref[:, h * HD:(h + 1) * HD] = dx[h].astype(BF16)
        for g in range(NG):
            dstate[g * 512:(g + 1) * 512, :] = dprev[g]
            dxc_ref[:, D + g * NS:D + (g + 1) * NS] = dB[g].astype(BF16)
            dxc_ref[:, D + NG * NS + g * NS:D + NG * NS + (g + 1) * NS] = dC[g].astype(BF16)
            dproj_ref[:, g * 512:(g + 1) * 512] = dz[g].astype(BF16)
            _acc_store(first, dnrm_ref, (slice(None), slice(g * 512, (g + 1) * 512)), dnrm[g])
        ddtr_ref[...] = ddtr
        _acc_store(first, ddtb_ref, (slice(None), slice(None)), ddtb)
        _acc_store(first, dalog_ref, (slice(None), slice(None)), dalog)
        _acc_store(first, ddsk_ref, (slice(None), slice(None)), ddsk)

    vec = pl.BlockSpec((1, LANE), lambda c: (0, 0))
    return pl.pallas_call(
        body, grid=(nc,),
        in_specs=[pl.BlockSpec((CH, CONV_C), lambda c: (rev(c), 0)), pl.BlockSpec((CH, LANE), lambda c: (rev(c), 0)),
                  pl.BlockSpec((CH, D), lambda c: (rev(c), 2)),
                  pl.BlockSpec((None, NH * HD, NS), lambda c: (rev(c), 0, 0))] + _SSD_PAR() +
                 [pl.BlockSpec((CH, D), lambda c: (rev(c), 1)), pl.BlockSpec(memory_space=pl.ANY)],
        out_specs=[pl.BlockSpec((CH, D), lambda c: (rev(c), 2)), pl.BlockSpec((CH, CONV_C), lambda c: (rev(c), 0)),
                   pl.BlockSpec((CH, LANE), lambda c: (rev(c), 0)), vec, vec, vec, pl.BlockSpec((1, D), lambda c: (0, 0))],
        out_shape=[S(dproj.shape, BF16), S((T, CONV_C), BF16), S((T, LANE), F32), S((1, LANE), F32), S((1, LANE), F32),
                   S((1, LANE), F32), S((1, D), F32)],
        scratch_shapes=[pltpu.VMEM((NH * HD, NS), F32)], input_output_aliases={9: 0},
        compiler_params=_params("arbitrary"), name=name)(xc, dtr, proj, prevs, dtb, alog, dsk, nrm, dmix, dproj)


def _rope(x, c, s, sign):
    W = x.shape[1]
    reps = W // LANE
    C, Sg = jnp.tile(c, (1, reps)), jnp.tile(s, (1, reps))
    lane = lax.broadcasted_iota(jnp.int32, x.shape, 1) % AHD
    up, dn = pltpu.roll(x, W - ROT // 2, 1), pltpu.roll(x, ROT // 2, 1)
    sw = jnp.where(lane < ROT // 2, up, jnp.where(lane < ROT, dn, 0.0))
    return x * C + sign * sw * Sg


def _rope_fwd(name, qkv, cos, sin):
    T = qkv.shape[0]
    tt = _tile(T, 256)
    KV = AKV * AHD

    def body(x_ref, c_ref, s_ref, o_ref):
        c, s = c_ref[...], s_ref[...]
        o_ref[:, :D] = _rope(x_ref[:, :D], c, s, 1.0).astype(BF16)
        o_ref[:, D:D + KV] = _rope(x_ref[:, D:D + KV], c, s, 1.0).astype(BF16)
        o_ref[:, D + KV:] = x_ref[:, D + KV:].astype(BF16)

    tab = pl.BlockSpec((tt, LANE), lambda i: (i, 0))
    return pl.pallas_call(
        body, grid=(T // tt,), in_specs=[pl.BlockSpec((tt, ODD_IN), lambda i: (i, 0)), tab, tab],
        out_specs=pl.BlockSpec((tt, ODD_IN), lambda i: (i, 0)), out_shape=S((T, ODD_IN), BF16),
        compiler_params=_params("parallel"), name=name)(qkv, cos, sin)


def _rope_bwd(name, dq, dkv_cur, dkv_prev, cos, sin):
    T = dq.shape[0]
    nb = T // CH
    KV = AKV * AHD

    def body(dq_ref, cur_ref, nxt_ref, c_ref, s_ref, o_ref, db_ref):
        n = pl.program_id(0)
        c, s = c_ref[...], s_ref[...]
        dkv = cur_ref[...] + nxt_ref[...] * (n < nb - 1).astype(F32)
        o_ref[:, :D] = _rope(dq_ref[...].astype(F32), c, s, -1.0).astype(BF16)
        o_ref[:, D:D + KV] = _rope(dkv[:, :KV], c, s, -1.0).astype(BF16)
        o_ref[:, D + KV:] = dkv[:, KV:].astype(BF16)
        _acc_store(n == 0, db_ref, (slice(None), slice(None)), jnp.sum(o_ref[...].astype(F32), 0, keepdims=True))

    tab = pl.BlockSpec((CH, LANE), lambda n: (n, 0))
    return pl.pallas_call(
        body, grid=(nb,),
        in_specs=[pl.BlockSpec((CH, D), lambda n: (n, 0)), pl.BlockSpec((CH, 2 * KV), lambda n: (n, 0)),
                  pl.BlockSpec((CH, 2 * KV), lambda n: (jnp.minimum(n + 1, nb - 1), 0)), tab, tab],
        out_specs=[pl.BlockSpec((CH, ODD_IN), lambda n: (n, 0)), pl.BlockSpec((1, ODD_IN), lambda n: (0, 0))],
        out_shape=[S((T, ODD_IN), BF16), S((1, ODD_IN), F32)],
        compiler_params=_params("arbitrary"), name=name)(dq, dkv_cur, dkv_prev, cos, sin)


def _swa_math(q, kp, kc, vp, vc, snk, mask):
    outs = []
    for k in range(AKV):
        K = jnp.concatenate([kp[k], kc[k]], 0).astype(BF16)
        V = jnp.concatenate([vp[k], vc[k]], 0).astype(BF16)
        s = lax.dot_general(q[k].astype(BF16), K, _DIMS["nt"], preferred_element_type=F32) * ATT_SCALE
        s = jnp.where(mask, s, -jnp.inf)
        m = lax.stop_gradient(jnp.maximum(jnp.max(s, -1, keepdims=True), snk[k]))
        p = jnp.exp(s - m)
        pr = p / (jnp.sum(p, -1, keepdims=True) + jnp.exp(snk[k] - m))
        outs.append(jnp.dot(pr.astype(BF16), V, preferred_element_type=F32))
    return outs


def _stack_heads(ref, k):
    return jnp.concatenate([ref[:, (k * AREP + r) * AHD:(k * AREP + r + 1) * AHD].astype(F32) for r in range(AREP)], 0)


def _swa_load(q_ref, cur_ref, prv_ref, snk_ref):
    KV = AKV * AHD
    q = [_stack_heads(q_ref, k) for k in range(AKV)]
    kc = [cur_ref[:, k * AHD:(k + 1) * AHD].astype(F32) for k in range(AKV)]
    vc = [cur_ref[:, KV + k * AHD:KV + (k + 1) * AHD].astype(F32) for k in range(AKV)]
    kp = [prv_ref[:, k * AHD:(k + 1) * AHD].astype(F32) for k in range(AKV)]
    vp = [prv_ref[:, KV + k * AHD:KV + (k + 1) * AHD].astype(F32) for k in range(AKV)]
    snk = [jnp.concatenate([jnp.broadcast_to(snk_ref[:, k * AREP + r:k * AREP + r + 1], (CH, 1)) for r in range(AREP)], 0)
           for k in range(AKV)]
    return q, kp, kc, vp, vc, snk


def _swa_mask(n):
    iq = lax.broadcasted_iota(jnp.int32, (AREP * CH, 2 * CH), 0) % CH
    js = lax.broadcasted_iota(jnp.int32, (AREP * CH, 2 * CH), 1)
    rel = iq + CH - js
    return (rel >= 0) & (rel < CH) & ((n > 0) | (js >= CH))


def _swa_specs(T):
    KV = AKV * AHD
    return [pl.BlockSpec((CH, D), lambda n: (n, 0)), pl.BlockSpec((CH, 2 * KV), lambda n: (n, D // (2 * KV))),
            pl.BlockSpec((CH, 2 * KV), lambda n: (jnp.maximum(n - 1, 0), D // (2 * KV))),
            pl.BlockSpec((1, LANE), lambda n: (0, 0))]


def _swa_fwd(name, qkvr, snk):
    T = qkvr.shape[0]

    def body(q_ref, cur_ref, prv_ref, snk_ref, o_ref):
        outs = _swa_math(*_swa_load(q_ref, cur_ref, prv_ref, snk_ref), _swa_mask(pl.program_id(0)))
        for h in range(AH):
            k, r = divmod(h, AREP)
            o_ref[:, h * AHD:(h + 1) * AHD] = outs[k][r * CH:(r + 1) * CH].astype(BF16)

    return pl.pallas_call(
        body, grid=(T // CH,), in_specs=_swa_specs(T), out_specs=pl.BlockSpec((CH, D), lambda n: (n, 0)),
        out_shape=S((T, D), BF16), compiler_params=_params("parallel"), name=name)(qkvr, qkvr, qkvr, snk)


def _swa_bwd(name, qkvr, snk, do):
    T = qkvr.shape[0]
    KV = AKV * AHD

    def body(q_ref, cur_ref, prv_ref, snk_ref, do_ref, dq_ref, dcur_ref, dprv_ref, dsnk_ref):
        n = pl.program_id(0)

        @pl.when(n == 0)
        def _():
            dsnk_ref[...] = jnp.zeros_like(dsnk_ref)

        prim = _swa_load(q_ref, cur_ref, prv_ref, snk_ref)
        mask = _swa_mask(n)
        _, vjp = jax.vjp(lambda *p: _swa_math(*p, mask), *prim)
        dq, dkp, dkc, dvp, dvc, dsnk = vjp([_stack_heads(do_ref, k) for k in range(AKV)])
        for h in range(AH):
            k, r = divmod(h, AREP)
            dq_ref[:, h * AHD:(h + 1) * AHD] = dq[k][r * CH:(r + 1) * CH].astype(BF16)
            dsnk_ref[:, h:h + 1] += jnp.sum(dsnk[k][r * CH:(r + 1) * CH], 0, keepdims=True)
        for k in range(AKV):
            dcur_ref[:, k * AHD:(k + 1) * AHD] = dkc[k]
            dcur_ref[:, KV + k * AHD:KV + (k + 1) * AHD] = dvc[k]
            dprv_ref[:, k * AHD:(k + 1) * AHD] = dkp[k]
            dprv_ref[:, KV + k * AHD:KV + (k + 1) * AHD] = dvp[k]

    kv = pl.BlockSpec((CH, 2 * KV), lambda n: (n, 0))
    return pl.pallas_call(
        body, grid=(T // CH,), in_specs=_swa_specs(T) + [pl.BlockSpec((CH, D), lambda n: (n, 0))],
        out_specs=[pl.BlockSpec((CH, D), lambda n: (n, 0)), kv, kv, pl.BlockSpec((1, LANE), lambda n: (0, 0))],
        out_shape=[S((T, D), BF16), S((T, 2 * KV), F32), S((T, 2 * KV), F32), S((1, LANE), F32)],
        compiler_params=_params("arbitrary"), name=name)(qkvr, qkvr, qkvr, snk, do)


def _xat_math(q, k, v):
    outs = []
    for h in range(XH):
        s = lax.dot_general(q[h].astype(BF16), k[h].astype(BF16), _DIMS["nt"], preferred_element_type=F32) * X_SCALE
        m = lax.stop_gradient(jnp.max(s, -1, keepdims=True))
        p = jnp.exp(s - m)
        pr = p / jnp.sum(p, -1, keepdims=True)
        outs.append(jnp.dot(pr.astype(BF16), v[h].astype(BF16), preferred_element_type=F32))
    return outs


def _xat_load(q_ref, kv_ref):
    q = [q_ref[:, h * XHD:(h + 1) * XHD].astype(F32) for h in range(XH)]
    k = [kv_ref[:, h * XHD:(h + 1) * XHD].astype(F32) for h in range(XH)]
    v = [kv_ref[:, XW + h * XHD:XW + (h + 1) * XHD].astype(F32) for h in range(XH)]
    return q, k, v


def _xat_fwd(name, q, kv):
    T, M = q.shape[0], kv.shape[0]
    tt = _tile(T, 512)

    def body(q_ref, kv_ref, o_ref):
        outs = _xat_math(*_xat_load(q_ref, kv_ref))
        for h in range(XH):
            o_ref[:, h * XHD:(h + 1) * XHD] = outs[h].astype(BF16)

    return pl.pallas_call(
        body, grid=(T // tt,),
        in_specs=[pl.BlockSpec((tt, XW), lambda i: (i, 0)), pl.BlockSpec((M, 2 * XW), lambda i: (0, 0))],
        out_specs=pl.BlockSpec((tt, XW), lambda i: (i, 0)), out_shape=S((T, XW), BF16),
        compiler_params=_params("parallel"), name=name)(q, kv)


def _xat_bwd(name, q, kv, do):
    T, M = q.shape[0], kv.shape[0]
    tt = _tile(T, 512)

    def body(q_ref, kv_ref, do_ref, dq_ref, dkv_ref):
        first = pl.program_id(0) == 0
        _, vjp = jax.vjp(_xat_math, *_xat_load(q_ref, kv_ref))
        dq, dk, dv = vjp([do_ref[:, h * XHD:(h + 1) * XHD].astype(F32) for h in range(XH)])
        for h in range(XH):
            sl = slice(h * XHD, (h + 1) * XHD)
            dq_ref[:, sl] = dq[h].astype(BF16)
            _acc_store(first, dkv_ref, (slice(None), sl), dk[h])
            _acc_store(first, dkv_ref, (slice(None), slice(XW + h * XHD, XW + (h + 1) * XHD)), dv[h])

    qs = pl.BlockSpec((tt, XW), lambda i: (i, 0))
    kvs = pl.BlockSpec((M, 2 * XW), lambda i: (0, 0))
    return pl.pallas_call(
        body, grid=(T // tt,), in_specs=[qs, kvs, qs], out_specs=[qs, kvs],
        out_shape=[S((T, XW), BF16), S((M, 2 * XW), F32)],
        compiler_params=_params("arbitrary"), name=name)(q, kv, do)


def _loss_head(name, x, gain, target):
    T = x.shape[0]
    tt = _tile(T, 512)

    def body(x_ref, g_ref, t_ref, l_ref, dx_ref, dxb_ref, dg_ref):
        first = pl.program_id(0) == 0
        xv, g = x_ref[...], g_ref[...]
        r = lax.rsqrt(jnp.mean(xv * xv, -1, keepdims=True) + EPS)
        xh = xv * r
        e = xh * g - t_ref[...]
        part = 0.5 * jnp.sum(jnp.mean(e * e, -1, keepdims=True), (0, 1), keepdims=True)
        _acc_store(first, l_ref, (slice(None), slice(None)), jnp.broadcast_to(part, (1, LANE)))
        dy = e * (1.0 / D)
        dxh = dy * g
        dx = r * (dxh - xh * jnp.mean(dxh * xh, -1, keepdims=True))
        dx_ref[...] = dx
        dxb_ref[...] = dx.astype(BF16)
        _acc_store(first, dg_ref, (slice(None), slice(None)), jnp.sum(dy * xh, 0, keepdims=True))

    row = pl.BlockSpec((tt, D), lambda i: (i, 0))
    vec = pl.BlockSpec((1, D), lambda i: (0, 0))
    return pl.pallas_call(
        body, grid=(T // tt,), in_specs=[row, vec, row],
        out_specs=[pl.BlockSpec((1, LANE), lambda i: (0, 0)), row, row, vec],
        out_shape=[S((1, LANE), F32), S((T, D), F32), S((T, D), BF16), S((1, D), F32)],
        compiler_params=_params("arbitrary"), name=name)(x, gain, target)


def _out_proj(name, a, b, x, next_gain, scale=1.0, tk_t=2048, plain=(1024, 1024)):
    if next_gain is None:
        return _mm(name, "nn", a, b, F32, res=x, scale=scale, tm_t=plain[0], tn_t=plain[1], tk_t=tk_t), None
    return _mm(name, "nn", a, b, F32, res=x, scale=scale, norm_gain=next_gain, tm_t=512, tn_t=D, tk_t=min(tk_t, 2048))


def _ffn_fwd(tag, x, h, gain, wgu4, get_wd, next_gain):
    g, u, a = _ffn_up(f"{tag}_up", h, wgu4, 0)
    wd = get_wd(a).reshape(1, 1, DFF, D)
    x_new, _ = _out_proj(f"{tag}_down", a, Op(wd, "r"), x, None, 0.5, 2816)
    h_next = None if next_gain is None else _rms_fwd(f"{tag}_nextnorm", x_new, next_gain)
    return x_new, h_next, (x, gain, h, g, u, a)


def _ffn_bwd(tag, saved, dx, dxb, wgu4, wd4, put):
    x, gain, h, g, u, a = saved
    dgu = _ffn_dact(f"{tag}_dact", dxb, wd4, 0, g, u, 1024)
    dwd = _mm(f"{tag}_dwd", "tn", a, dxb, BF16, out=("r", 4, 1, 0), scale=0.5, tm_t=1408, tn_t=1024, tk_t=2048)
    dwgu = _mm(f"{tag}_dwgu", "tn", h, Op(dgu, "c"), BF16, out=("c", 4, 1, 0), tm_t=1024, tn_t=256, tk_t=8192)
    tok = put(dwgu, dwd)
    dh = _mm(f"{tag}_dh", "nt", Op(dgu, "c"), Op(wgu4, "c"), BF16, bias=jnp.zeros((1, D), F32) + tok, tk_t=2816)
    dx, dxb, dgain = _rms_bwd(f"{tag}_dnorm", x, gain, dh, dx)
    return dx, dxb, dgain


def _xattn_fwd(tag, x, hq, mem, gq, gm, wxq4, wxkv4, wxo4, next_gain):
    mn = _rms_fwd(f"{tag}_normm", mem, gm)
    q = _mm(f"{tag}_q", "nn", hq, Op(wxq4, "r"), BF16)
    kv = _mm(f"{tag}_kv", "nn", mn, Op(wxkv4, "r"), BF16)
    o = _xat_fwd(f"{tag}_att", q, kv)
    wxo = jnp.transpose(wxo4[:, 0], (1, 0, 2)).reshape(XW, D)
    x_new, h_next = _out_proj(f"{tag}_o", o, wxo, x, next_gain)
    return x_new, h_next, (x, mem, gq, gm, hq, mn, q, kv, o)


def _xattn_bwd(tag, saved, dx, dxb, wxq4, wxkv4, wxo4, put):
    x, mem, gq, gm, hq, mn, q, kv, o = saved
    dwxo = _mm(f"{tag}_dwo", "tn", o, dxb, BF16, out=("c", 4, 1, 0))
    do = _mm(f"{tag}_do", "nt", dxb, Op(wxo4, "c"), BF16)
    dq, dkv = _xat_bwd(f"{tag}_datt", q, kv, do)
    dwxq = _mm(f"{tag}_dwq", "tn", hq, dq, BF16, out=("r", 4, 1, 0))
    dwxkv = _mm(f"{tag}_dwkv", "tn", mn, dkv, BF16, out=("r", 4, 1, 0))
    tok = put(dwxq, dwxkv, dwxo)
    dhq = _mm(f"{tag}_dhq", "nt", dq, Op(wxq4, "r"), BF16, bias=jnp.zeros((1, D), F32) + tok)
    dmn = _mm(f"{tag}_dmn", "nt", dkv, Op(wxkv4, "r"), BF16)
    _, _, dgm = _rms_bwd(f"{tag}_dnormm", mem, gm, dmn)
    dx, dxb, dgq = _rms_bwd(f"{tag}_dnormq", x, gq, dhq, dx)
    return dx, dxb, dgq, dgm


def _even_fwd(tag, x, h, gain, w_main, w_dt, p, wout4, next_gain):
    proj = _mm(f"{tag}_in", "nn", h, w_main, BF16)
    dtr = _mm(f"{tag}_indt", "nn", h, w_dt, F32)
    mix = _gmlp_fwd(f"{tag}_gmlp", proj, p["lng"], p["lnb"], p["ws"], p["bs"])
    ypre, xc = _conv_fwd(f"{tag}_conv", proj, p["cw"], p["cb"])
    mix, prevs = _ssd_fwd(f"{tag}_ssd", xc, dtr, proj, p["dtb"], p["alog"], p["dsk"], p["nrm"], mix)
    x_new, h_next = _out_proj(f"{tag}_out", mix, Op(wout4.reshape(1, 1, 2 * D, D), "r"), x, next_gain)
    return x_new, h_next, (x, gain, h, proj, dtr, mix, ypre, xc, prevs)


def _even_bwd(tag, saved, dx, dxb, w_main, w_dt, p, wout4, put):
    x, gain, h, proj, dtr, mix, ypre, xc, prevs = saved
    T = x.shape[0]
    dwout = _mm(f"{tag}_dwout", "tn", mix, dxb, BF16, out=("r", 4, 1, 0))
    dmix = _mm(f"{tag}_dmix", "nt", dxb, Op(wout4, "r", 0), BF16)
    dproj = lax.empty((T, EVEN_MAIN), BF16)
    dproj, dlng, dlnb, dws, dbs = _gmlp_bwd(f"{tag}_dgmlp", proj, p["lng"], p["lnb"], p["ws"], p["bs"], dmix, dproj)
    dproj, dxc, ddtr, ddtb, dalog, ddsk, dnrm = _ssd_bwd(
        f"{tag}_dssd", xc, dtr, proj, prevs, p["dtb"], p["alog"], p["dsk"], p["nrm"], dmix, dproj)
    dproj, dcw, dcb = _conv_bwd(f"{tag}_dconv", proj, ypre, dxc, p["cw"], dproj)
    dw_main = _mm(f"{tag}_dwin", "tn", h, dproj, BF16, tm_t=1024, tn_t=256, tk_t=8192)
    dw_dt = _mm(f"{tag}_dwdt", "tn", h, ddtr, BF16)
    tok = put(dw_main, dw_dt, dwout)
    dh = _mm(f"{tag}_dh1", "nt", ddtr, w_dt + tok.astype(BF16), F32)
    dh = _mm(f"{tag}_dh2", "nt", dproj, w_main, BF16, res=dh, tk_t=3072)
    dx, dxb, dgain = _rms_bwd(f"{tag}_dnorm", x, gain, dh, dx)
    small = dict(lng=dlng, lnb=dlnb, ws=dws, bs=dbs, cw=dcw, cb=dcb, dtb=ddtb, alog=dalog, dsk=ddsk, nrm=dnrm)
    return dx, dxb, dgain, small


def _odd_fwd(tag, x, h, gain, wqkv4, bqkv, snk, wo4, cos, sin, next_gain):
    wqkv = jnp.transpose(wqkv4[:, 0], (1, 0, 2)).reshape(D, ODD_IN)
    qkv = _mm(f"{tag}_qkv", "nn", h, wqkv, F32, bias=bqkv, tn_t=1280)
    qkvr = _rope_fwd(f"{tag}_rope", qkv, cos, sin)
    o = _swa_fwd(f"{tag}_swa", qkvr, snk)
    x_new, h_next = _out_proj(f"{tag}_o", o, Op(wo4.reshape(1, 1, D, D), "r"), x, next_gain)
    return x_new, h_next, (x, gain, h, qkvr, o, wqkv)


def _odd_bwd(tag, saved, dx, dxb, wqkv4, snk, wo4, cos, sin, put):
    x, gain, h, qkvr, o, wqkv = saved
    dwo = _mm(f"{tag}_dwo", "tn", o, dxb, BF16, out=("r", 4, 1, 0))
    do = _mm(f"{tag}_do", "nt", dxb, Op(wo4, "r", 0), BF16)
    dq, dcur, dprv, dsnk = _swa_bwd(f"{tag}_dswa", qkvr, snk, do)
    dqkv, dbias = _rope_bwd(f"{tag}_drope", dq, dcur, dprv, cos, sin)
    dwqkv = _mm(f"{tag}_dwqkv", "tn", h, dqkv, BF16, out=("c", 4, 1, 0), tn_t=640)
    tok = put(dwqkv, dwo)
    dh = _mm(f"{tag}_dh", "nt", dqkv, wqkv, BF16, bias=jnp.zeros((1, D), F32) + tok, tk_t=ODD_IN)
    dx, dxb, dgain = _rms_bwd(f"{tag}_dnorm", x, gain, dh, dx)
    return dx, dxb, dgain, dbias, dsnk


def _row(v):
    return v.reshape(1, -1).astype(F32)


def _pad_lanes(v, n=LANE):
    v = v.reshape(1, -1).astype(F32)
    return jnp.pad(v, ((0, 0), (0, n - v.shape[1])))


def _local_step(x, mem, positions, target, getw, P, putg):
    inv_freq = ROPE_THETA ** (-jnp.arange(0, ROT, 2, dtype=F32) / ROT)
    ang = positions.astype(F32)[:, None] * inv_freq
    cos8, sin8 = jnp.cos(ang), jnp.sin(ang)
    ones, zeros = jnp.ones((x.shape[0], AHD - ROT), F32), jnp.zeros((x.shape[0], AHD - ROT), F32)
    cos = jnp.tile(jnp.concatenate([cos8, cos8, ones], 1), (1, 2))
    sin = jnp.tile(jnp.concatenate([-sin8, sin8, zeros], 1), (1, 2))

    snk = _pad_lanes(P["sinks"])
    W = {}

    def w(name, layer, after):
        if (name, layer) not in W:
            W[name, layer] = getw(name, layer, after)
        return W[name, layer]

    saved = []
    h = _rms_fwd("l0_ffn1_norm", x, _row(P["norm_ffn1"][0]))
    for i in range(2):
        x, h, s1 = _ffn_fwd(f"l{i}_ffn1", x, h, _row(P["norm_ffn1"][i]), w("w_ffn1_gu", i, x),
                            functools.partial(w, "w_ffn1_down", i), _row(P["norm_mix"][i]))
        if i == 0:
            ev = dict(lng=_row(P["gm_ln_g"]), lnb=_row(P["gm_ln_b"]), ws=P["gm_ws"].reshape(GM_G, CH, CH),
                      bs=P["gm_bs"].reshape(GM_G, CH, 1), cw=w("conv_w", 0, x), cb=_row(P["conv_b"]),
                      dtb=_pad_lanes(P["dt_bias"]), alog=_pad_lanes(P["a_log"]), dsk=_pad_lanes(P["d_skip"]),
                      nrm=_row(P["ssd_norm"]))
            w_main, w_dt = w("w_in_even", 0, x)
            x, h, s2 = _even_fwd("l0_mix", x, h, _row(P["norm_mix"][0]), w_main, w_dt, ev, w("w_out_even", 0, x),
                                 _row(P["norm_xq"][0]))
        else:
            x, h, s2 = _odd_fwd("l1_mix", x, h, _row(P["norm_mix"][1]), w("w_qkv", 0, x), w("b_qkv", 0, x), snk,
                                w("w_o_odd", 0, x), cos, sin, _row(P["norm_xq"][1]))
        x, h, s3 = _xattn_fwd(f"l{i}_xat", x, h, mem, _row(P["norm_xq"][i]), _row(P["norm_mem"][i]),
                              w("w_xq", i, x), w("w_xkv", i, x), w("w_xo", i, x), _row(P["norm_ffn2"][i]))
        x, h, s4 = _ffn_fwd(f"l{i}_ffn2", x, h, _row(P["norm_ffn2"][i]), w("w_ffn2_gu", i, x),
                            functools.partial(w, "w_ffn2_down", i), _row(P["norm_ffn1"][1]) if i == 0 else None)
        saved.append((s1, s2, s3, s4))

    loss, dx, dxb, d_final = _loss_head("loss_head", x, _row(P["final_norm"]), target)

    sm = {}
    dn = {k: [None, None] for k in ("norm_ffn1", "norm_mix", "norm_xq", "norm_mem", "norm_ffn2")}
    for i in (1, 0):
        s1, s2, s3, s4 = saved[i]
        dx, dxb, dn["norm_ffn2"][i] = _ffn_bwd(
            f"l{i}_ffn2", s4, dx, dxb, W["w_ffn2_gu", i], W["w_ffn2_down", i],
            lambda dwgu, dwd, i=i: putg({("w_ffn2_gu", i): dwgu, ("w_ffn2_down", i): dwd}))
        dx, dxb, dn["norm_xq"][i], dn["norm_mem"][i] = _xattn_bwd(
            f"l{i}_xat", s3, dx, dxb, W["w_xq", i], W["w_xkv", i], W["w_xo", i],
            lambda dwxq, dwxkv, dwxo, i=i: putg({("w_xq", i): dwxq, ("w_xkv", i): dwxkv, ("w_xo", i): dwxo}))
        if i == 0:
            dx, dxb, dn["norm_mix"][0], sm_even = _even_bwd(
                "l0_mix", s2, dx, dxb, w_main, w_dt, ev, W["w_out_even", 0],
                lambda dw_main, dw_dt, dwout: putg({("w_in_even", 0): (dw_main, dw_dt), ("w_out_even", 0): dwout}))
        else:
            dx, dxb, dn["norm_mix"][1], sm["b_qkv"], sm["sinks"] = _odd_bwd(
                "l1_mix", s2, dx, dxb, W["w_qkv", 0], snk, W["w_o_odd", 0], cos, sin,
                lambda dwqkv, dwo: putg({("w_qkv", 0): dwqkv, ("w_o_odd", 0): dwo}))
        dx, dxb, dn["norm_ffn1"][i] = _ffn_bwd(
            f"l{i}_ffn1", s1, dx, dxb, W["w_ffn1_gu", i], W["w_ffn1_down", i],
            lambda dwgu, dwd, i=i: putg({("w_ffn1_gu", i): dwgu, ("w_ffn1_down", i): dwd}))
    for k, v in dn.items():
        sm[k] = jnp.concatenate(v, 0)
    sm.update(gm_ln_g=sm_even["lng"], gm_ln_b=sm_even["lnb"], gm_ws=sm_even["ws"], gm_bs=sm_even["bs"],
              conv_w=sm_even["cw"], conv_b=sm_even["cb"], dt_bias=sm_even["dtb"][:, :NH], a_log=sm_even["alog"][:, :NH],
              d_skip=sm_even["dsk"][:, :NH], ssd_norm=sm_even["nrm"], sinks=sm["sinks"][:, :AH], final_norm=d_final)
    return loss[0, 0], dx, sm


def _chip_peers():
    x, y, c = lax.axis_index("x"), lax.axis_index("y"), lax.axis_index("c")
    return 2 * x + y, [((1 - x, y, c), 2 * (1 - x) + y), ((x, 1 - y, c), 2 * x + (1 - y)),
                       ((1 - x, 1 - y, c), 2 * (1 - x) + (1 - y))]


def _any_specs(n):
    return [pl.BlockSpec(memory_space=pl.ANY)] * n


_HBM = pl.BlockSpec(memory_space=pltpu.HBM)
_SEM = pl.BlockSpec(memory_space=pltpu.SEMAPHORE)
_EFFECT = pltpu.SideEffectType.DATAFLOW_SIDE_EFFECTING


def _own_slot(piece, chip):
    zone = lax.empty((4,) + piece.shape, piece.dtype)
    return lax.dynamic_update_slice(zone, piece[None], (chip,) + (0,) * piece.ndim)


def _chip_copies(srcs, lands, ssems, rsems, mode="chips"):
    c = lax.axis_index("c")
    sib = (lax.axis_index("x"), lax.axis_index("y"), 1 - c)
    if mode == "sibling":
        return [pltpu.make_async_remote_copy(src_ref=srcs[i], dst_ref=lands[i], send_sem=ssems[i].at[0],
                                             recv_sem=rsems[i].at[0], device_id=sib, device_id_type=MESH)
                for i in range(len(lands))]
    me, peers = _chip_peers()
    if mode == "half":
        return [pltpu.make_async_remote_copy(src_ref=lands[i].at[me, c], dst_ref=lands[i].at[me, c], send_sem=ssems[i].at[j],
                                             recv_sem=rsems[i].at[j], device_id=dev, device_id_type=MESH)
                for i in range(len(lands)) for j, (dev, _) in enumerate(peers)]
    if mode == "pass_on":
        return [pltpu.make_async_remote_copy(src_ref=lands[i].at[chip, c], dst_ref=lands[i].at[chip, c],
                                             send_sem=ssems[i].at[j], recv_sem=rsems[i].at[j], device_id=sib, device_id_type=MESH)
                for i in range(len(lands)) for j, (_, chip) in enumerate(peers)]
    return [pltpu.make_async_remote_copy(src_ref=lands[i].at[me] if srcs[i] is None else srcs[i].at[chip],
                                         dst_ref=lands[i].at[me], send_sem=ssems[i].at[j], recv_sem=rsems[i].at[j],
                                         device_id=dev, device_id_type=MESH)
            for i in range(len(lands)) for j, (dev, chip) in enumerate(peers)]


def _exchange_start(name, srcs, lands, mode="chips"):
    n = len(lands)
    ns = 0 if srcs is None else n

    def body(*refs):
        src_refs = [None] * n if srcs is None else refs[:n]
        land_refs = refs[ns:ns + n]
        ssems, rsems = refs[ns + n:ns + 2 * n], refs[ns + 2 * n:ns + 3 * n]
        token = refs[2 * ns + 4 * n]
        for cp in _chip_copies(src_refs, land_refs, ssems, rsems, mode):
            cp.start()
        token[...] = jnp.zeros_like(token)

    ins = ([] if srcs is None else list(srcs)) + list(lands)
    res = pl.pallas_call(
        body, name=name,
        out_shape=[pltpu.SemaphoreType.DMA((1 if mode == "sibling" else 3,))] * (2 * n) + [pltpu.HBM(a.shape, a.dtype) for a in ins]
        + [S((8, LANE), F32)],
        in_specs=[_HBM] * (ns + n),
        out_specs=[_SEM] * (2 * n) + [_HBM] * (ns + n) + [pl.BlockSpec(memory_space=pltpu.VMEM)],
        input_output_aliases={i: 2 * n + i for i in range(ns + n)},
        compiler_params=pltpu.CompilerParams(has_side_effects=_EFFECT),
    )(*[pltpu.with_memory_space_constraint(a, pltpu.HBM) for a in ins])
    items = [(res[i], res[n + i], None if srcs is None else res[2 * n + i], res[2 * n + ns + i]) for i in range(n)]
    return items, res[2 * n + ns + n]


def _exchange_wait(name, item, after, mode="chips", whole=False):
    ssem, rsem, src, land = item
    ns = 0 if src is None else 1

    def body(*refs):
        src_ref = refs[0] if ns else None
        land_ref, ssem_ref, rsem_ref = refs[ns], refs[ns + 1], refs[ns + 2]
        for cp in _chip_copies([src_ref], [land_ref], [ssem_ref], [rsem_ref], mode):
            cp.wait_send()
            cp.wait_recv()

    ins = ([src] if ns else []) + [land]
    res = pl.pallas_call(
        body, name=name, out_shape=[pltpu.HBM(a.shape, a.dtype) for a in ins],
        in_specs=[_HBM] * (ns + 1) + [_SEM, _SEM, pl.BlockSpec(memory_space=pl.ANY)], out_specs=[_HBM] * (ns + 1),
        input_output_aliases={i: i for i in range(ns + 1)}, compiler_params=pltpu.CompilerParams(has_side_effects=_EFFECT),
    )(*ins, ssem, rsem, after)
    return res if whole else res[ns]


def _gather_all(name, v, after):
    def body(v_ref, _, o_ref, ssem, rsem, lsem):
        x, y, c = lax.axis_index("x"), lax.axis_index("y"), lax.axis_index("c")
        me = 4 * x + 2 * y + c
        loc = pltpu.make_async_copy(v_ref, o_ref.at[me], lsem)
        loc.start()
        copies = []
        for k in range(1, 8):
            fx, fy, fc = (k >> 2) & 1, (k >> 1) & 1, k & 1
            dev = (x ^ fx, y ^ fy, c ^ fc)
            cp = pltpu.make_async_remote_copy(src_ref=v_ref, dst_ref=o_ref.at[me], send_sem=ssem.at[k - 1],
                                              recv_sem=rsem.at[k - 1], device_id=dev, device_id_type=MESH)
            cp.start()
            copies.append(cp)
        for cp in copies:
            cp.wait()
        loc.wait()

    return pl.pallas_call(
        body, in_specs=_any_specs(2), out_specs=pl.BlockSpec(memory_space=pl.ANY), out_shape=S((8,) + v.shape, v.dtype),
        scratch_shapes=[pltpu.SemaphoreType.DMA((7,)), pltpu.SemaphoreType.DMA((7,)), pltpu.SemaphoreType.DMA(())],
        compiler_params=pltpu.CompilerParams(has_side_effects=True), name=name)(v, after)


def _row_tile(R, row_bytes, budget=4 << 20):
    if R * row_bytes <= budget or R % 16:
        return R
    t = max(16, budget // row_bytes // 16 * 16)
    while R % t:
        t -= 16
    return t


def _sum_slots(name, r, n):
    _, R, C = r.shape
    tr = _row_tile(R, C * (n * r.dtype.itemsize + 4))

    def body(r_ref, o_ref):
        acc = r_ref[0].astype(F32)
        for j in range(1, n):
            acc = acc + r_ref[j].astype(F32)
        o_ref[...] = acc

    return pl.pallas_call(
        body, grid=(R // tr,), in_specs=[pl.BlockSpec((n, tr, C), lambda i: (0, i, 0))],
        out_specs=pl.BlockSpec((tr, C), lambda i: (i, 0)), out_shape=S((R, C), F32),
        compiler_params=_params("parallel"), name=name)(r)


def _adamw(name, w, m, v, layer, g1, g2=None, into=None):
    nl, R, C = w.shape
    tr = _row_tile(R, C * 4 * 9)
    two, has_into = g2 is not None, into is not None

    def body(w_ref, m_ref, v_ref, g1_ref, *rest):
        rest = list(rest)
        g = g1_ref[...]
        if two:
            g = g + rest.pop(0)[...]
        g_ref, d_ref, nm_ref, nv_ref = rest[-4:]
        mn = B1 * m_ref[...] + (1.0 - B1) * g
        vn = B2 * v_ref[...] + (1.0 - B2) * jnp.square(g)
        m_hat = mn / (1.0 - B1 ** STEP)
        v_hat = vn / (1.0 - B2 ** STEP)
        g_ref[...] = g
        d_ref[...] = -LR * (m_hat / (jnp.sqrt(v_hat) + AEPS) + WD * w_ref[...])
        nm_ref[...] = mn
        nv_ref[...] = vn

    blk = pl.BlockSpec((tr, C), lambda i: (i, 0))
    lay = pl.BlockSpec((None, tr, C), lambda i: (layer, i, 0))
    args = [w, m, v, g1] + ([g2] if two else []) + (list(into) if has_into else [])
    in_specs = [lay] * 3 + [blk] * (2 if two else 1) + (_any_specs(4) if has_into else [])
    aliases = {len(args) - 4 + t: t for t in range(4)} if has_into else {}
    return pl.pallas_call(
        body, grid=(R // tr,), in_specs=in_specs, out_specs=[lay] * 4, out_shape=[S((nl, R, C), F32)] * 4,
        input_output_aliases=aliases, compiler_params=_params("parallel"), name=name)(*args)


_USE_ORDER = [("w_ffn1_gu", 0), ("w_ffn1_down", 0), ("conv_w", 0), ("w_in_even", 0), ("w_out_even", 0), ("w_xq", 0),
              ("w_xkv", 0), ("w_xo", 0), ("w_ffn2_gu", 0), ("w_ffn2_down", 0), ("w_ffn1_gu", 1), ("w_ffn1_down", 1),
              ("w_qkv", 0), ("b_qkv", 0), ("w_o_odd", 0), ("w_xq", 1), ("w_xkv", 1), ("w_xo", 1), ("w_ffn2_gu", 1),
              ("w_ffn2_down", 1)]
_SMALL = ["norm_ffn1", "norm_mix", "gm_ln_g", "gm_ln_b", "gm_ws", "gm_bs", "conv_w", "conv_b", "dt_bias", "a_log",
          "d_skip", "ssd_norm", "b_qkv", "sinks", "norm_xq", "norm_mem", "norm_ffn2", "final_norm"]
_WEIGHTS = ["norm_ffn1", "w_ffn1_gu", "w_ffn1_down", "norm_mix", "w_in_even", "gm_ln_g", "gm_ln_b", "gm_ws", "gm_bs",
            "conv_w", "conv_b", "dt_bias", "a_log", "d_skip", "ssd_norm", "w_out_even", "w_qkv", "b_qkv", "sinks",
            "w_o_odd", "norm_xq", "norm_mem", "w_xq", "w_xkv", "w_xo", "norm_ffn2", "w_ffn2_gu", "w_ffn2_down",
            "final_norm"]


def _pack(arrs):
    rows = []
    for a in arrs:
        f = a.reshape(-1).astype(F32)
        pad = (-f.shape[0]) % LANE
        rows.append(jnp.pad(f, (0, pad)).reshape(-1, LANE))
    out = jnp.concatenate(rows, 0)
    pad = (-out.shape[0]) % 8
    return jnp.pad(out, ((0, pad), (0, 0)))


def _unpack(packed, shapes):
    outs, r = [], 0
    for shp in shapes:
        n = math.prod(shp)
        nr = -(-n // LANE)
        outs.append(packed[r:r + nr].reshape(-1)[:n].reshape(shp))
        r += nr
    return outs


def kernel(x, mem, positions, norm_ffn1, w_ffn1_gu, w_ffn1_down, norm_mix, w_in_even, gm_ln_g, gm_ln_b, gm_ws, gm_bs, conv_w, conv_b, dt_bias, a_log, d_skip, ssd_norm, w_out_even, w_qkv, b_qkv, sinks, w_o_odd, norm_xq, norm_mem, w_xq, w_xkv, w_xo, norm_ffn2, w_ffn2_gu, w_ffn2_down, final_norm, loss_target, m_norm_ffn1, m_w_ffn1_gu, m_w_ffn1_down, m_norm_mix, m_w_in_even, m_gm_ln_g, m_gm_ln_b, m_gm_ws, m_gm_bs, m_conv_w, m_conv_b, m_dt_bias, m_a_log, m_d_skip, m_ssd_norm, m_w_out_even, m_w_qkv, m_b_qkv, m_sinks, m_w_o_odd, m_norm_xq, m_norm_mem, m_w_xq, m_w_xkv, m_w_xo, m_norm_ffn2, m_w_ffn2_gu, m_w_ffn2_down, m_final_norm, v_norm_ffn1, v_w_ffn1_gu, v_w_ffn1_down, v_norm_mix, v_w_in_even, v_gm_ln_g, v_gm_ln_b, v_gm_ws, v_gm_bs, v_conv_w, v_conv_b, v_dt_bias, v_a_log, v_d_skip, v_ssd_norm, v_w_out_even, v_w_qkv, v_b_qkv, v_sinks, v_w_o_odd, v_norm_xq, v_norm_mem, v_w_xq, v_w_xkv, v_w_xo, v_norm_ffn2, v_w_ffn2_gu, v_w_ffn2_down, v_final_norm):
    a = dict(locals())
    w = {k: a[k] for k in _WEIGHTS}
    mom = {k: a["m_" + k] for k in _WEIGHTS}
    var = {k: a["v_" + k] for k in _WEIGHTS}
    chip = 2 * lax.axis_index("x") + lax.axis_index("y")

    shards = [w[k][i:i + 1] if k in ("conv_w", "b_qkv") else w[k][i:i + 1].astype(BF16) for k, i in _USE_ORDER]
    half = shards[0].reshape((2, shards[0].shape[1] // 2) + shards[0].shape[2:])
    (first,), first_started = _exchange_start("gather_start_first", None, [_own_slot(half, chip)], "half")
    small = _USE_ORDER.index(("conv_w", 0))
    shards[small] = shards[small] + first_started[0, 0]
    rest, rest_started = _exchange_start("gather_start_rest", None, [_own_slot(s, chip) for s in shards[1:]])
    pending = dict(zip(_USE_ORDER[1:], rest))

    def getw(name, layer, after):
        if (name, layer) == _USE_ORDER[0]:
            zone = _exchange_wait("gather_wait_first_half", first, after, "half")
            (passed,), _ = _exchange_start("gather_pass_on_first", None, [zone], "pass_on")
            return _exchange_wait("gather_wait_first_passed", passed, after, "pass_on").reshape((4,) + shards[0].shape)
        got = _exchange_wait(f"gather_wait_{name}_{layer}", pending.pop((name, layer)), after)
        if name == "w_in_even":
            w_in = jnp.transpose(got[:, 0], (1, 0, 2)).reshape(D, EVEN_IN)
            return w_in[:, :EVEN_MAIN], jnp.pad(w_in[:, EVEN_MAIN:], ((0, 0), (0, LANE - (EVEN_IN - EVEN_MAIN))))
        if name == "conv_w":
            return jnp.transpose(got[:, 0], (1, 0, 2)).reshape(4, CONV_C)
        if name == "b_qkv":
            return got.reshape(1, ODD_IN)
        return got

    sent = []

    def putg(grads):
        names, arrs = [], []
        for (name, layer), g in grads.items():
            if name == "w_in_even":
                dw_in = jnp.concatenate([g[0], g[1][:, :EVEN_IN - EVEN_MAIN]], 1)
                g = jnp.transpose(dw_in.reshape(D, 4, EVEN_IN // 4), (1, 0, 2)).reshape(4, 1, D, EVEN_IN // 4)
            names.append((name, layer))
            arrs.append(g)
        own = [_own_slot(lax.dynamic_index_in_dim(g, chip, 0, keepdims=False), chip) for g in arrs]
        its, tok = _exchange_start(f"scatter_start_{names[0][0]}_{names[0][1]}", arrs, own)
        sent.append(list(zip(names, its)))
        return tok[0, 0]

    P = {k: w[k] for k in _SMALL}
    P["norm_ffn1"] = P["norm_ffn1"] + rest_started[0, 0]
    loss, grad_x, sm = _local_step(x[0], mem[0], positions[0], loss_target[0], getw, P, putg)
    loss = lax.psum(loss, ("x", "y", "c"))

    out = {}

    def update(groups, after):
        flying = []
        for grp in groups:
            part = []
            for (name, layer), it in grp:
                r = _exchange_wait(f"scatter_wait_{name}_{layer}", it, after)
                part.append(_sum_slots(f"sum_{name}_{layer}", r.reshape(4, -1, r.shape[-1]), 4))
            name0, layer0 = grp[0][0]
            its, after = _exchange_start(f"swap_start_{name0}_{layer0}", part, [lax.empty(p.shape, p.dtype) for p in part],
                                         "sibling")
            flying += [(nm, it) for (nm, _), it in zip(grp, its)]
        for (name, layer), it in flying:
            p1, p2 = _exchange_wait(f"swap_wait_{name}_{layer}", it, after, "sibling", whole=True)
            out[name] = _adamw(f"adamw_{name}_{layer}", w[name], mom[name], var[name], layer, p1, p2, out.get(name))
            after = out[name][0]

    update(sent[:-1], grad_x)
    done_a = out["w_out_even"][0]
    update(sent[-1:], done_a)

    full_shapes = {k: w[k].shape for k in _SMALL}
    full_shapes["conv_w"], full_shapes["b_qkv"] = (1, 4, CONV_C), (1, ODD_IN)
    packed = _pack([sm[k] for k in _SMALL])
    total = _sum_slots("sum_small", _gather_all("gather_small", packed, done_a), 8)
    gs = dict(zip(_SMALL, _unpack(total, [full_shapes[k] for k in _SMALL])))
    gs["conv_w"] = lax.dynamic_slice_in_dim(gs["conv_w"], chip * (CONV_C // 4), CONV_C // 4, 2)
    gs["b_qkv"] = lax.dynamic_slice_in_dim(gs["b_qkv"], chip * (ODD_IN // 4), ODD_IN // 4, 1)
    res = _adamw("adamw_small", _pack([w[k] for k in _SMALL])[None], _pack([mom[k] for k in _SMALL])[None],
                 _pack([var[k] for k in _SMALL])[None], 0, _pack([gs[k] for k in _SMALL]))
    shapes = [w[k].shape for k in _SMALL]
    for k, g, d, nm, nv in zip(_SMALL, *[_unpack(r[0], shapes) for r in res]):
        out[k] = [g, d, nm, nv]

    return (loss, grad_x[None], *[out[k][0] for k in _WEIGHTS], *[out[k][1] for k in _WEIGHTS],
            *[out[k][2] for k in _WEIGHTS], *[out[k][3] for k in _WEIGHTS])
```

```python
import functools
import math

import jax
import jax.numpy as jnp
from jax import lax
from jax.experimental import pallas as pl
from jax.experimental.pallas import tpu as pltpu

F32, BF16 = jnp.float32, jnp.bfloat16
S = jax.ShapeDtypeStruct
MESH = pl.DeviceIdType.MESH

D = 2048
DFF = 5632
EPS = 1e-5
CH = 128
GM_G, GM_GD = 4, 512
NH, HD, NG, HPG, NS = 32, 64, 4, 8, 128
CONV_C = 3072
EVEN_MAIN, EVEN_IN = 9216, 9248
AH, AKV, AREP, AHD = 32, 4, 8, 64
ODD_IN = 2560
XH, XHD, XW = 4, 128, 512
ATT_SCALE = AHD ** -0.5
X_SCALE = XHD ** -0.5
ROPE_THETA = 500000.0
ROT = 16
LR, B1, B2, AEPS, WD, STEP = 0.001, 0.9, 0.999, 1e-08, 0.01, 10
LANE = 128
VMEM_LIMIT_V7X = 56 * 1024 * 1024


def _params(*sem):
    return pltpu.CompilerParams(dimension_semantics=sem, vmem_limit_bytes=VMEM_LIMIT_V7X)


def _tile(dim, target):
    if dim <= target:
        return dim
    t = (target // LANE) * LANE
    while t > LANE and dim % t:
        t -= LANE
    assert dim % t == 0, (dim, target)
    return t


class Op:
    def __init__(self, arr, kind=None, layer=0):
        self.arr, self.kind, self.layer = arr, kind, layer
        if kind is None:
            self.R, self.C = arr.shape
        else:
            L = arr.shape[0]
            self.R = arr.shape[2] * (L if kind == "r" else 1)
            self.C = arr.shape[3] * (L if kind == "c" else 1)

    def unit(self, axis):
        if self.kind == "r" and axis == 0:
            return self.arr.shape[2]
        if self.kind == "c" and axis == 1:
            return self.arr.shape[3]
        return (self.R, self.C)[axis]

    def spec(self, tr, tc, pick):
        if self.kind is None:
            return pl.BlockSpec((tr, tc), lambda i, j, k: pick(i, j, k))
        l = self.layer
        if self.kind == "c":
            per = self.arr.shape[3] // tc
            return pl.BlockSpec((None, None, tr, tc),
                                lambda i, j, k: (pick(i, j, k)[1] // per, l, pick(i, j, k)[0], pick(i, j, k)[1] % per))
        per = self.arr.shape[2] // tr
        return pl.BlockSpec((None, None, tr, tc),
                            lambda i, j, k: (pick(i, j, k)[0] // per, l, pick(i, j, k)[0] % per, pick(i, j, k)[1]))


_DIMS = {"nn": (((1,), (0,)), ((), ())), "nt": (((1,), (1,)), ((), ())), "tn": (((0,), (0,)), ((), ()))}
_PICK_A = {"nn": lambda i, j, k: (i, k), "nt": lambda i, j, k: (i, k), "tn": lambda i, j, k: (k, i)}
_PICK_B = {"nn": lambda i, j, k: (k, j), "nt": lambda i, j, k: (j, k), "tn": lambda i, j, k: (k, j)}


def _mm(name, mode, a, b, out_dtype, *, out=None, res=None, bias=None, scale=1.0, norm_gain=None,
        tm_t=1024, tn_t=1024, tk_t=2048):
    if not isinstance(a, Op):
        a = Op(a)
    if not isinstance(b, Op):
        b = Op(b)
    if mode == "nn":
        M, K, N = a.R, a.C, b.C
        assert b.R == K
        um, uk, un = a.unit(0), math.gcd(a.unit(1), b.unit(0)), b.unit(1)
    elif mode == "nt":
        M, K, N = a.R, a.C, b.R
        assert b.C == K
        um, uk, un = a.unit(0), math.gcd(a.unit(1), b.unit(1)), b.unit(0)
    else:
        K, M, N = a.R, a.C, b.C
        assert b.R == K
        um, uk, un = a.unit(1), math.gcd(a.unit(0), b.unit(0)), b.unit(1)
    if out is not None:
        okind, oL, olayers, olayer = out
        if okind == "c":
            un = math.gcd(un, N // oL)
        else:
            um = math.gcd(um, M // oL)
    tm, tn, tk = _tile(um, tm_t), _tile(un, tn_t), _tile(uk, tk_t)
    gi, gj, gk = M // tm, N // tn, K // tk
    a_blk = (tm, tk) if mode != "tn" else (tk, tm)
    b_blk = {"nn": (tk, tn), "nt": (tn, tk), "tn": (tk, tn)}[mode]
    in_specs = [a.spec(*a_blk, _PICK_A[mode]), b.spec(*b_blk, _PICK_B[mode])]
    args = [a.arr, b.arr]
    if res is not None:
        in_specs.append(pl.BlockSpec((tm, tn), lambda i, j, k: (i, j)))
        args.append(res)
    if bias is not None:
        in_specs.append(pl.BlockSpec((1, tn), lambda i, j, k: (0, j)))
        args.append(bias)
    if out is None:
        out_shape = S((M, N), out_dtype)
        out_spec = pl.BlockSpec((tm, tn), lambda i, j, k: (i, j))
    else:
        shp = (oL, olayers, M, N // oL) if okind == "c" else (oL, olayers, M // oL, N)
        out_shape = S(shp, out_dtype)
        out_spec = Op(out_shape, okind, olayer).spec(tm, tn, lambda i, j, k: (i, j))
    has_res, has_bias, has_norm = res is not None, bias is not None, norm_gain is not None
    if has_norm:
        assert out is None and tn == N
        in_specs.append(pl.BlockSpec((1, N), lambda i, j, k: (0, 0)))
        args.append(norm_gain)
        out_shape = [out_shape, S((M, N), BF16)]
        out_spec = [out_spec, pl.BlockSpec((tm, tn), lambda i, j, k: (i, j))]
    dims = _DIMS[mode]

    def body(a_ref, b_ref, *rest):
        rest = list(rest)
        res_ref = rest.pop(0) if has_res else None
        bias_ref = rest.pop(0) if has_bias else None
        gain_ref = rest.pop(0) if has_norm else None
        o_ref = rest.pop(0)
        h_ref = rest.pop(0) if has_norm else None
        part = lax.dot_general(a_ref[...].astype(BF16), b_ref[...].astype(BF16), dims, preferred_element_type=F32)

        def finish(r):
            if scale != 1.0:
                r = r * scale
            if has_bias:
                r = r + bias_ref[...]
            if has_res:
                r = r + res_ref[...]
            o_ref[...] = r.astype(o_ref.dtype)
            if has_norm:
                h_ref[...] = (r * lax.rsqrt(jnp.mean(r * r, -1, keepdims=True) + EPS) * gain_ref[...]).astype(BF16)

        if gk == 1:
            finish(part)
            return
        acc, = rest
        k = pl.program_id(2)

        @pl.when(k == 0)
        def _():
            acc[...] = part

        @pl.when((k > 0) & (k < gk - 1))
        def _():
            acc[...] += part

        @pl.when(k == gk - 1)
        def _():
            finish(acc[...] + part)

    return pl.pallas_call(
        body, grid=(gi, gj, gk), in_specs=in_specs, out_specs=out_spec, out_shape=out_shape,
        scratch_shapes=[pltpu.VMEM((tm, tn), F32)] if gk > 1 else [],
        compiler_params=_params("parallel", "parallel", "arbitrary"), name=name)(*args)


def _rms_fwd(name, x, gain):
    T = x.shape[0]
    tt = _tile(T, 512)

    def body(x_ref, g_ref, o_ref):
        xv = x_ref[...]
        r = lax.rsqrt(jnp.mean(xv * xv, -1, keepdims=True) + EPS)
        o_ref[...] = (xv * r * g_ref[...]).astype(BF16)

    return pl.pallas_call(
        body, grid=(T // tt,),
        in_specs=[pl.BlockSpec((tt, D), lambda i: (i, 0)), pl.BlockSpec((1, D), lambda i: (0, 0))],
        out_specs=pl.BlockSpec((tt, D), lambda i: (i, 0)), out_shape=S((T, D), BF16),
        compiler_params=_params("parallel"), name=name)(x, gain)


def _rms_bwd(name, x, gain, dh, dx_in=None):
    T = x.shape[0]
    tt = _tile(T, 512)
    has_in = dx_in is not None

    def body(x_ref, g_ref, dh_ref, *rest):
        rest = list(rest)
        dxin_ref = rest.pop(0) if has_in else None
        dx_ref, dxb_ref, dg_ref = rest
        xv = x_ref[...]
        r = lax.rsqrt(jnp.mean(xv * xv, -1, keepdims=True) + EPS)
        xh = xv * r
        dy = dh_ref[...].astype(F32)
        dxh = dy * g_ref[...]
        dx = r * (dxh - xh * jnp.mean(dxh * xh, -1, keepdims=True))
        if has_in:
            dx = dx + dxin_ref[...]
        dx_ref[...] = dx
        dxb_ref[...] = dx.astype(BF16)
        part = jnp.sum(dy * xh, 0, keepdims=True)

        @pl.when(pl.program_id(0) == 0)
        def _():
            dg_ref[...] = part

        @pl.when(pl.program_id(0) > 0)
        def _():
            dg_ref[...] += part

    row = pl.BlockSpec((tt, D), lambda i: (i, 0))
    vec = pl.BlockSpec((1, D), lambda i: (0, 0))
    args = [x, gain, dh] + ([dx_in] if has_in else [])
    return pl.pallas_call(
        body, grid=(T // tt,), in_specs=[row, vec, row] + ([row] if has_in else []),
        out_specs=[row, row, vec], out_shape=[S((T, D), F32), S((T, D), BF16), S((1, D), F32)],
        compiler_params=_params("arbitrary"), name=name)(*args)


def _sigmoid(x):
    return 0.5 * jnp.tanh(0.5 * x) + 0.5


def _ffn_up(name, h, w4, layer, tm_t=512, tn_t=1408):
    T = h.shape[0]
    n_sh = w4.shape[3]
    tm, tn = _tile(T, tm_t), _tile(n_sh, tn_t)
    per = n_sh // tn

    def body(h_ref, wg_ref, wu_ref, g_ref, u_ref, a_ref):
        hv = h_ref[...]
        g = jnp.dot(hv, wg_ref[...], preferred_element_type=F32)
        u = jnp.dot(hv, wu_ref[...], preferred_element_type=F32)
        g_ref[...] = g.astype(BF16)
        u_ref[...] = u.astype(BF16)
        a_ref[...] = (g * _sigmoid(g) * u).astype(BF16)

    o = pl.BlockSpec((tm, tn), lambda j, i: (i, j))
    return pl.pallas_call(
        body, grid=(DFF // tn, T // tm),
        in_specs=[pl.BlockSpec((tm, D), lambda j, i: (i, 0)),
                  pl.BlockSpec((None, None, D, tn), lambda j, i: (j // per, layer, 0, j % per)),
                  pl.BlockSpec((None, None, D, tn), lambda j, i: (2 + j // per, layer, 0, j % per))],
        out_specs=[o, o, o], out_shape=[S((T, DFF), BF16)] * 3,
        compiler_params=_params("parallel", "parallel"), name=name)(h, w4, w4)


def _ffn_dact(name, dxb, wd4, layer, g, u, tm_t=512):
    T = dxb.shape[0]
    r_sh = wd4.shape[2]
    tm, tn = _tile(T, tm_t), _tile(r_sh, 1408)
    per = r_sh // tn

    def body(dx_ref, w_ref, g_ref, u_ref, o_ref):
        da = 0.5 * lax.dot_general(dx_ref[...], w_ref[...], _DIMS["nt"], preferred_element_type=F32)
        gv, uv = g_ref[...].astype(F32), u_ref[...].astype(F32)
        sg = _sigmoid(gv)
        o_ref[0, 0] = (da * uv * sg * (1.0 + gv * (1.0 - sg))).astype(BF16)
        o_ref[1, 0] = (da * gv * sg).astype(BF16)

    t = pl.BlockSpec((tm, tn), lambda j, i: (i, j))
    return pl.pallas_call(
        body, grid=(DFF // tn, T // tm),
        in_specs=[pl.BlockSpec((tm, D), lambda j, i: (i, 0)),
                  pl.BlockSpec((None, None, tn, D), lambda j, i: (j // per, layer, j % per, 0)), t, t],
        out_specs=pl.BlockSpec((2, 1, tm, tn), lambda j, i: (0, 0, i, j)), out_shape=S((2, 1, T, DFF), BF16),
        compiler_params=_params("parallel", "parallel"), name=name)(dxb, wd4, g, u)


def _gelu(x):
    return 0.5 * x * (1.0 + lax.erf(x * 0.7071067811865476))


def _causal(n):
    return lax.broadcasted_iota(jnp.int32, (n, n), 0) >= lax.broadcasted_iota(jnp.int32, (n, n), 1)


def _gmlp_math(u_raw, v_raw, lng, lnb, ws, bs):
    causal = _causal(CH)
    outs = []
    for g in range(GM_G):
        u, v = _gelu(u_raw[g]), _gelu(v_raw[g])
        mu = jnp.mean(v, -1, keepdims=True)
        var = jnp.mean(jnp.square(v - mu), -1, keepdims=True)
        vn = (v - mu) * lax.rsqrt(var + EPS) * lng[g] + lnb[g]
        wm = jnp.where(causal, ws[g], 0.0)
        s = jnp.dot(wm.astype(BF16), vn.astype(BF16), preferred_element_type=F32) + bs[g]
        outs.append(u * s)
    return outs


def _gmlp_load(proj_ref, lng_ref, lnb_ref, ws_ref, bs_ref):
    sl = lambda g, off: slice(off + g * GM_GD, off + (g + 1) * GM_GD)
    u_raw = [proj_ref[:, sl(g, 0)].astype(F32) for g in range(GM_G)]
    v_raw = [proj_ref[:, sl(g, D)].astype(F32) for g in range(GM_G)]
    lng = [lng_ref[:, sl(g, 0)] for g in range(GM_G)]
    lnb = [lnb_ref[:, sl(g, 0)] for g in range(GM_G)]
    ws = [ws_ref[g] for g in range(GM_G)]
    bs = [bs_ref[g] for g in range(GM_G)]
    return u_raw, v_raw, lng, lnb, ws, bs


_GM_PAR = lambda: [pl.BlockSpec((1, D), lambda i: (0, 0)), pl.BlockSpec((1, D), lambda i: (0, 0)),
                   pl.BlockSpec((GM_G, CH, CH), lambda i: (0, 0, 0)), pl.BlockSpec((GM_G, CH, 1), lambda i: (0, 0, 0))]


def _gmlp_fwd(name, proj, lng, lnb, ws, bs):
    T = proj.shape[0]

    def body(proj_ref, lng_ref, lnb_ref, ws_ref, bs_ref, o_ref):
        outs = _gmlp_math(*_gmlp_load(proj_ref, lng_ref, lnb_ref, ws_ref, bs_ref))
        for g in range(GM_G):
            o_ref[:, g * GM_GD:(g + 1) * GM_GD] = outs[g].astype(BF16)

    return pl.pallas_call(
        body, grid=(T // CH,), in_specs=[pl.BlockSpec((CH, 2 * D), lambda i: (i, 0))] + _GM_PAR(),
        out_specs=pl.BlockSpec((CH, D), lambda i: (i, 0)), out_shape=S((T, 2 * D), BF16),
        compiler_params=_params("parallel"), name=name)(proj, lng, lnb, ws, bs)


def _acc_store(first, ref, idx, val):
    @pl.when(first)
    def _():
        ref[idx] = val

    @pl.when(jnp.logical_not(first))
    def _():
        ref[idx] += val


def _gmlp_bwd(name, proj, lng, lnb, ws, bs, dmix, dproj):
    T = proj.shape[0]

    def body(proj_ref, lng_ref, lnb_ref, ws_ref, bs_ref, dmix_ref, _, dproj_ref, dlng_ref, dlnb_ref, dws_ref, dbs_ref):
        first = pl.program_id(0) == 0
        prim = _gmlp_load(proj_ref, lng_ref, lnb_ref, ws_ref, bs_ref)
        _, vjp = jax.vjp(_gmlp_math, *prim)
        du, dv, dlng, dlnb, dws, dbs = vjp([dmix_ref[:, g * GM_GD:(g + 1) * GM_GD].astype(F32) for g in range(GM_G)])
        for g in range(GM_G):
            sl = slice(g * GM_GD, (g + 1) * GM_GD)
            dproj_ref[:, sl] = du[g].astype(BF16)
            dproj_ref[:, D + g * GM_GD:D + (g + 1) * GM_GD] = dv[g].astype(BF16)
            _acc_store(first, dlng_ref, (slice(None), sl), dlng[g])
            _acc_store(first, dlnb_ref, (slice(None), sl), dlnb[g])
            _acc_store(first, dws_ref, g, dws[g])
            _acc_store(first, dbs_ref, g, dbs[g])

    par = _GM_PAR()
    return pl.pallas_call(
        body, grid=(T // CH,),
        in_specs=[pl.BlockSpec((CH, 2 * D), lambda i: (i, 0))] + par +
                 [pl.BlockSpec((CH, D), lambda i: (i, 0)), pl.BlockSpec(memory_space=pl.ANY)],
        out_specs=[pl.BlockSpec((CH, 2 * D), lambda i: (i, 0))] + par,
        out_shape=[S(dproj.shape, BF16), S((1, D), F32), S((1, D), F32), S((GM_G, CH, CH), F32), S((GM_G, CH, 1), F32)],
        input_output_aliases={6: 0}, compiler_params=_params("arbitrary"), name=name)(proj, lng, lnb, ws, bs, dmix, dproj)


CONV_TT = 256
HALO = 8


def _shift_rows(cur, halo_after, s):
    if s == 0:
        return cur
    n = cur.shape[0]
    return pltpu.roll(jnp.concatenate([cur, halo_after], 0), s, 0)[:n]


def _conv_fwd(name, proj, w, b):
    T = proj.shape[0]
    tt = _tile(T, CONV_TT)
    hb = tt // HALO

    def body(x_ref, halo_ref, w_ref, b_ref, y_ref, xc_ref):
        i = pl.program_id(0)
        x = x_ref[...].astype(F32)
        halo = halo_ref[...].astype(F32) * (i > 0).astype(F32)
        y = b_ref[...] + w_ref[3:4, :] * x
        for s in (1, 2, 3):
            y = y + w_ref[3 - s:4 - s, :] * _shift_rows(x, halo, s)
        y_ref[...] = y.astype(BF16)
        xc_ref[...] = (y * _sigmoid(y)).astype(BF16)

    o = pl.BlockSpec((tt, CONV_C), lambda i: (i, 0))
    return pl.pallas_call(
        body, grid=(T // tt,),
        in_specs=[pl.BlockSpec((tt, CONV_C), lambda i: (i, 2)),
                  pl.BlockSpec((HALO, CONV_C), lambda i: (jnp.maximum(i * hb - 1, 0), 2)),
                  pl.BlockSpec((4, CONV_C), lambda i: (0, 0)), pl.BlockSpec((1, CONV_C), lambda i: (0, 0))],
        out_specs=[o, o], out_shape=[S((T, CONV_C), BF16)] * 2,
        compiler_params=_params("parallel"), name=name)(proj, proj, w, b)


def _conv_bwd(name, proj, ypre, dxc, w, dproj):
    T = proj.shape[0]
    tt = _tile(T, CONV_TT)
    hb = tt // HALO
    nt = T // tt

    def dsilu(y):
        sg = _sigmoid(y)
        return sg * (1.0 + y * (1.0 - sg))

    def body(x_ref, xh_ref, y_ref, yn_ref, d_ref, dn_ref, w_ref, _, dproj_ref, dw_ref, db_ref):
        i = pl.program_id(0)
        first = i == 0
        x = x_ref[...].astype(F32)
        halo = xh_ref[...].astype(F32) * (i > 0).astype(F32)
        dy = d_ref[...].astype(F32) * dsilu(y_ref[...].astype(F32))
        dyn = dn_ref[...].astype(F32) * dsilu(yn_ref[...].astype(F32)) * (i < nt - 1).astype(F32)
        ext = jnp.concatenate([dy, dyn], 0)
        dx = w_ref[3:4, :] * dy
        _acc_store(first, dw_ref, (slice(3, 4), slice(None)), jnp.sum(x * dy, 0, keepdims=True))
        for s in (1, 2, 3):
            dx = dx + w_ref[3 - s:4 - s, :] * pltpu.roll(ext, tt + HALO - s, 0)[:tt]
            _acc_store(first, dw_ref, (slice(3 - s, 4 - s), slice(None)),
                       jnp.sum(_shift_rows(x, halo, s) * dy, 0, keepdims=True))
        _acc_store(first, db_ref, (slice(None), slice(None)), jnp.sum(dy, 0, keepdims=True))
        dproj_ref[...] = dx.astype(BF16)

    cur = pl.BlockSpec((tt, CONV_C), lambda i: (i, 0))
    nxt = pl.BlockSpec((HALO, CONV_C), lambda i: (jnp.minimum((i + 1) * hb, T // HALO - 1), 0))
    return pl.pallas_call(
        body, grid=(nt,),
        in_specs=[pl.BlockSpec((tt, CONV_C), lambda i: (i, 2)),
                  pl.BlockSpec((HALO, CONV_C), lambda i: (jnp.maximum(i * hb - 1, 0), 2)),
                  cur, nxt, cur, nxt, pl.BlockSpec((4, CONV_C), lambda i: (0, 0)), pl.BlockSpec(memory_space=pl.ANY)],
        out_specs=[pl.BlockSpec((tt, CONV_C), lambda i: (i, 2)), pl.BlockSpec((4, CONV_C), lambda i: (0, 0)),
                   pl.BlockSpec((1, CONV_C), lambda i: (0, 0))],
        out_shape=[S(dproj.shape, BF16), S((4, CONV_C), F32), S((1, CONV_C), F32)],
        input_output_aliases={7: 0}, compiler_params=_params("arbitrary"), name=name)(proj, proj, ypre, ypre, dxc, dxc, w, dproj)


def _softplus(x):
    return jnp.maximum(x, 0.0) + jnp.log(1.0 + jnp.exp(-jnp.abs(x)))


def _ssd_math(x, Bm, Cm, dtr, z, prev, dtb, alog, dsk, nrm):
    hi = lax.Precision.HIGHEST
    causal = _causal(CH)
    tri = causal.astype(F32)
    lane = lax.broadcasted_iota(jnp.int32, (1, LANE), 1)
    sub = lax.broadcasted_iota(jnp.int32, (LANE, 1), 0)
    dt = _softplus(dtr + dtb)
    a = dt * (-jnp.exp(alog))
    a_cs = jnp.dot(tri, a, preferred_element_type=F32, precision=hi)
    a_csT = lax.dot_general(a, tri, (((0,), (1,)), ((), ())), preferred_element_type=F32, precision=hi)
    a_last = jnp.sum(a, 0, keepdims=True)
    gw = HPG * HD
    outs, new = [], []
    for g in range(NG):
        spread = (lax.broadcasted_iota(jnp.int32, (LANE, gw), 0)
                  == g * HPG + lax.broadcasted_iota(jnp.int32, (LANE, gw), 1) // HD).astype(F32)
        to_lanes = lambda v: jnp.dot(v, spread, preferred_element_type=F32, precision=hi)
        col_e, dt_e, last_e, dsk_e = to_lanes(a_cs), to_lanes(dt), to_lanes(a_last), to_lanes(dsk)
        last_r = lax.dot_general(spread, a_last, (((0,), (1,)), ((), ())), preferred_element_type=F32, precision=hi)
        cb = lax.dot_general(Cm[g].astype(BF16), Bm[g].astype(BF16), _DIMS["nt"], preferred_element_type=F32)
        xg = jnp.concatenate(x[g * HPG:(g + 1) * HPG], 1)
        yd = []
        for h in range(g * HPG, (g + 1) * HPG):
            ohl = (lane == h).astype(F32)
            col = jnp.sum(a_cs * ohl, 1, keepdims=True)
            row = jnp.sum(a_csT * (sub == h).astype(F32), 0, keepdims=True)
            dtc = jnp.sum(dt * ohl, 1, keepdims=True)
            lmat = jnp.where(causal, jnp.exp(jnp.where(causal, col - row, 0.0)), 0.0)
            yd.append(jnp.dot((cb * lmat).astype(BF16), (x[h] * dtc).astype(BF16), preferred_element_type=F32))
        y = jnp.concatenate(yd, 1)
        y = y + jnp.exp(col_e) * lax.dot_general(Cm[g].astype(BF16), prev[g].astype(BF16), _DIMS["nt"],
                                                 preferred_element_type=F32)
        st = lax.dot_general((xg * dt_e * jnp.exp(last_e - col_e)).astype(BF16), Bm[g].astype(BF16), _DIMS["tn"],
                             preferred_element_type=F32)
        new.append(prev[g] * jnp.exp(last_r) + st)
        yg = (y + xg * dsk_e) * (z[g] * _sigmoid(z[g]))
        yg = yg * lax.rsqrt(jnp.mean(yg * yg, -1, keepdims=True) + EPS)
        outs.append(yg * nrm[g])
    return outs, new


def _ssd_load(xc_ref, dtr_ref, z_ref, state_ref, dtb_ref, alog_ref, dsk_ref, nrm_ref):
    gw = HPG * HD
    x = [xc_ref[:, h * HD:(h + 1) * HD].astype(F32) for h in range(NH)]
    Bm = [xc_ref[:, D + g * NS:D + (g + 1) * NS].astype(F32) for g in range(NG)]
    Cm = [xc_ref[:, D + NG * NS + g * NS:D + NG * NS + (g + 1) * NS].astype(F32) for g in range(NG)]
    z = [z_ref[:, g * gw:(g + 1) * gw].astype(F32) for g in range(NG)]
    prev = [state_ref[g * gw:(g + 1) * gw, :] for g in range(NG)]
    nrm = [nrm_ref[:, g * gw:(g + 1) * gw] for g in range(NG)]
    return x, Bm, Cm, dtr_ref[...], z, prev, dtb_ref[...], alog_ref[...], dsk_ref[...], nrm


_SSD_PAR = lambda: [pl.BlockSpec((1, LANE), lambda c: (0, 0))] * 3 + [pl.BlockSpec((1, D), lambda c: (0, 0))]


def _ssd_fwd(name, xc, dtr, proj, dtb, alog, dsk, nrm, mix):
    T = xc.shape[0]
    nc = T // CH

    def body(xc_ref, dtr_ref, z_ref, dtb_ref, alog_ref, dsk_ref, nrm_ref, _, mix_ref, prev_ref, state):
        @pl.when(pl.program_id(0) == 0)
        def _():
            state[...] = jnp.zeros_like(state)

        prev_ref[...] = state[...]
        outs, new = _ssd_math(*_ssd_load(xc_ref, dtr_ref, z_ref, state, dtb_ref, alog_ref, dsk_ref, nrm_ref))
        for g in range(NG):
            mix_ref[:, g * 512:(g + 1) * 512] = outs[g].astype(BF16)
            state[g * 512:(g + 1) * 512, :] = new[g]

    return pl.pallas_call(
        body, grid=(nc,),
        in_specs=[pl.BlockSpec((CH, CONV_C), lambda c: (c, 0)), pl.BlockSpec((CH, LANE), lambda c: (c, 0)),
                  pl.BlockSpec((CH, D), lambda c: (c, 2))] + _SSD_PAR() + [pl.BlockSpec(memory_space=pl.ANY)],
        out_specs=[pl.BlockSpec((CH, D), lambda c: (c, 1)), pl.BlockSpec((None, NH * HD, NS), lambda c: (c, 0, 0))],
        out_shape=[S(mix.shape, BF16), S((nc, NH * HD, NS), F32)],
        scratch_shapes=[pltpu.VMEM((NH * HD, NS), F32)], input_output_aliases={7: 0},
        compiler_params=_params("arbitrary"), name=name)(xc, dtr, proj, dtb, alog, dsk, nrm, mix)


def _ssd_bwd(name, xc, dtr, proj, prevs, dtb, alog, dsk, nrm, dmix, dproj):
    T = xc.shape[0]
    nc = T // CH
    rev = lambda c: nc - 1 - c

    def body(xc_ref, dtr_ref, z_ref, prev_ref, dtb_ref, alog_ref, dsk_ref, nrm_ref, dmix_ref, _,
             dproj_ref, dxc_ref, ddtr_ref, ddtb_ref, dalog_ref, ddsk_ref, dnrm_ref, dstate):
        first = pl.program_id(0) == 0

        @pl.when(first)
        def _():
            dstate[...] = jnp.zeros_like(dstate)

        prim = _ssd_load(xc_ref, dtr_ref, z_ref, prev_ref, dtb_ref, alog_ref, dsk_ref, nrm_ref)
        _, vjp = jax.vjp(_ssd_math, *prim)
        douts = [dmix_ref[:, g * 512:(g + 1) * 512].astype(F32) for g in range(NG)]
        dnew = [dstate[g * 512:(g + 1) * 512, :] for g in range(NG)]
        dx, dB, dC, ddtr, dz, dprev, ddtb, dalog, ddsk, dnrm = vjp((douts, dnew))
        for h in range(NH):
            dxc_ref[:, h * HD:(h + 1) * HD] = dx[h].astype(BF16)
        for g in range(NG):
            dstate[g * 512:(g + 1) * 512, :] = dprev[g]
            dxc_ref[:, D + g * NS:D + (g + 1) * NS] = dB[g].astype(BF16)
            dxc_ref[:, D + NG * NS + g * NS:D + NG * NS + (g + 1) * NS] = dC[g].astype(BF16)
            dproj_ref[:, g * 512:(g + 1) * 512] = dz[g].astype(BF16)
            _acc_store(first, dnrm_ref, (slice(None), slice(g * 512, (g + 1) * 512)), dnrm[g])
        ddtr_ref[...] = ddtr
        _acc_store(first, ddtb_ref, (slice(None), slice(None)), ddtb)
        _acc_store(first, dalog_ref, (slice(None), slice(None)), dalog)
        _acc_store(first, ddsk_ref, (slice(None), slice(None)), ddsk)

    vec = pl.BlockSpec((1, LANE), lambda c: (0, 0))
    return pl.pallas_call(
        body, grid=(nc,),
        in_specs=[pl.BlockSpec((CH, CONV_C), lambda c: (rev(c), 0)), pl.BlockSpec((CH, LANE), lambda c: (rev(c), 0)),
                  pl.BlockSpec((CH, D), lambda c: (rev(c), 2)),
                  pl.BlockSpec((None, NH * HD, NS), lambda c: (rev(c), 0, 0))] + _SSD_PAR() +
                 [pl.BlockSpec((CH, D), lambda c: (rev(c), 1)), pl.BlockSpec(memory_space=pl.ANY)],
        out_specs=[pl.BlockSpec((CH, D), lambda c: (rev(c), 2)), pl.BlockSpec((CH, CONV_C), lambda c: (rev(c), 0)),
                   pl.BlockSpec((CH, LANE), lambda c: (rev(c), 0)), vec, vec, vec, pl.BlockSpec((1, D), lambda c: (0, 0))],
        out_shape=[S(dproj.shape, BF16), S((T, CONV_C), BF16), S((T, LANE), F32), S((1, LANE), F32), S((1, LANE), F32),
                   S((1, LANE), F32), S((1, D), F32)],
        scratch_shapes=[pltpu.VMEM((NH * HD, NS), F32)], input_output_aliases={9: 0},
        compiler_params=_params("arbitrary"), name=name)(xc, dtr, proj, prevs, dtb, alog, dsk, nrm, dmix, dproj)


def _rope(x, c, s, sign):
    W = x.shape[1]
    reps = W // LANE
    C, Sg = jnp.tile(c, (1, reps)), jnp.tile(s, (1, reps))
    lane = lax.broadcasted_iota(jnp.int32, x.shape, 1) % AHD
    up, dn = pltpu.roll(x, W - ROT // 2, 1), pltpu.roll(x, ROT // 2, 1)
    sw = jnp.where(lane < ROT // 2, up, jnp.where(lane < ROT, dn, 0.0))
    return x * C + sign * sw * Sg


def _rope_fwd(name, qkv, cos, sin):
    T = qkv.shape[0]
    tt = _tile(T, 256)
    KV = AKV * AHD

    def body(x_ref, c_ref, s_ref, o_ref):
        c, s = c_ref[...], s_ref[...]
        o_ref[:, :D] = _rope(x_ref[:, :D], c, s, 1.0).astype(BF16)
        o_ref[:, D:D + KV] = _rope(x_ref[:, D:D + KV], c, s, 1.0).astype(BF16)
        o_ref[:, D + KV:] = x_ref[:, D + KV:].astype(BF16)

    tab = pl.BlockSpec((tt, LANE), lambda i: (i, 0))
    return pl.pallas_call(
        body, grid=(T // tt,), in_specs=[pl.BlockSpec((tt, ODD_IN), lambda i: (i, 0)), tab, tab],
        out_specs=pl.BlockSpec((tt, ODD_IN), lambda i: (i, 0)), out_shape=S((T, ODD_IN), BF16),
        compiler_params=_params("parallel"), name=name)(qkv, cos, sin)


def _rope_bwd(name, dq, dkv_cur, dkv_prev, cos, sin):
    T = dq.shape[0]
    nb = T // CH
    KV = AKV * AHD

    def body(dq_ref, cur_ref, nxt_ref, c_ref, s_ref, o_ref, db_ref):
        n = pl.program_id(0)
        c, s = c_ref[...], s_ref[...]
        dkv = cur_ref[...] + nxt_ref[...] * (n < nb - 1).astype(F32)
        o_ref[:, :D] = _rope(dq_ref[...].astype(F32), c, s, -1.0).astype(BF16)
        o_ref[:, D:D + KV] = _rope(dkv[:, :KV], c, s, -1.0).astype(BF16)
        o_ref[:, D + KV:] = dkv[:, KV:].astype(BF16)
        _acc_store(n == 0, db_ref, (slice(None), slice(None)), jnp.sum(o_ref[...].astype(F32), 0, keepdims=True))

    tab = pl.BlockSpec((CH, LANE), lambda n: (n, 0))
    return pl.pallas_call(
        body, grid=(nb,),
        in_specs=[pl.BlockSpec((CH, D), lambda n: (n, 0)), pl.BlockSpec((CH, 2 * KV), lambda n: (n, 0)),
                  pl.BlockSpec((CH, 2 * KV), lambda n: (jnp.minimum(n + 1, nb - 1), 0)), tab, tab],
        out_specs=[pl.BlockSpec((CH, ODD_IN), lambda n: (n, 0)), pl.BlockSpec((1, ODD_IN), lambda n: (0, 0))],
        out_shape=[S((T, ODD_IN), BF16), S((1, ODD_IN), F32)],
        compiler_params=_params("arbitrary"), name=name)(dq, dkv_cur, dkv_prev, cos, sin)


def _swa_math(q, kp, kc, vp, vc, snk, mask):
    outs = []
    for k in range(AKV):
        K = jnp.concatenate([kp[k], kc[k]], 0).astype(BF16)
        V = jnp.concatenate([vp[k], vc[k]], 0).astype(BF16)
        s = lax.dot_general(q[k].astype(BF16), K, _DIMS["nt"], preferred_element_type=F32) * ATT_SCALE
        s = jnp.where(mask, s, -jnp.inf)
        m = lax.stop_gradient(jnp.maximum(jnp.max(s, -1, keepdims=True), snk[k]))
        p = jnp.exp(s - m)
        pr = p / (jnp.sum(p, -1, keepdims=True) + jnp.exp(snk[k] - m))
        outs.append(jnp.dot(pr.astype(BF16), V, preferred_element_type=F32))
    return outs


def _stack_heads(ref, k):
    return jnp.concatenate([ref[:, (k * AREP + r) * AHD:(k * AREP + r + 1) * AHD].astype(F32) for r in range(AREP)], 0)


def _swa_load(q_ref, cur_ref, prv_ref, snk_ref):
    KV = AKV * AHD
    q = [_stack_heads(q_ref, k) for k in range(AKV)]
    kc = [cur_ref[:, k * AHD:(k + 1) * AHD].astype(F32) for k in range(AKV)]
    vc = [cur_ref[:, KV + k * AHD:KV + (k + 1) * AHD].astype(F32) for k in range(AKV)]
    kp = [prv_ref[:, k * AHD:(k + 1) * AHD].astype(F32) for k in range(AKV)]
    vp = [prv_ref[:, KV + k * AHD:KV + (k + 1) * AHD].astype(F32) for k in range(AKV)]
    snk = [jnp.concatenate([jnp.broadcast_to(snk_ref[:, k * AREP + r:k * AREP + r + 1], (CH, 1)) for r in range(AREP)], 0)
           for k in range(AKV)]
    return q, kp, kc, vp, vc, snk


def _swa_mask(n):
    iq = lax.broadcasted_iota(jnp.int32, (AREP * CH, 2 * CH), 0) % CH
    js = lax.broadcasted_iota(jnp.int32, (AREP * CH, 2 * CH), 1)
    rel = iq + CH - js
    return (rel >= 0) & (rel < CH) & ((n > 0) | (js >= CH))


def _swa_specs(T):
    KV = AKV * AHD
    return [pl.BlockSpec((CH, D), lambda n: (n, 0)), pl.BlockSpec((CH, 2 * KV), lambda n: (n, D // (2 * KV))),
            pl.BlockSpec((CH, 2 * KV), lambda n: (jnp.maximum(n - 1, 0), D // (2 * KV))),
            pl.BlockSpec((1, LANE), lambda n: (0, 0))]


def _swa_fwd(name, qkvr, snk):
    T = qkvr.shape[0]

    def body(q_ref, cur_ref, prv_ref, snk_ref, o_ref):
        outs = _swa_math(*_swa_load(q_ref, cur_ref, prv_ref, snk_ref), _swa_mask(pl.program_id(0)))
        for h in range(AH):
            k, r = divmod(h, AREP)
            o_ref[:, h * AHD:(h + 1) * AHD] = outs[k][r * CH:(r + 1) * CH].astype(BF16)

    return pl.pallas_call(
        body, grid=(T // CH,), in_specs=_swa_specs(T), out_specs=pl.BlockSpec((CH, D), lambda n: (n, 0)),
        out_shape=S((T, D), BF16), compiler_params=_params("parallel"), name=name)(qkvr, qkvr, qkvr, snk)


def _swa_bwd(name, qkvr, snk, do):
    T = qkvr.shape[0]
    KV = AKV * AHD

    def body(q_ref, cur_ref, prv_ref, snk_ref, do_ref, dq_ref, dcur_ref, dprv_ref, dsnk_ref):
        n = pl.program_id(0)

        @pl.when(n == 0)
        def _():
            dsnk_ref[...] = jnp.zeros_like(dsnk_ref)

        prim = _swa_load(q_ref, cur_ref, prv_ref, snk_ref)
        mask = _swa_mask(n)
        _, vjp = jax.vjp(lambda *p: _swa_math(*p, mask), *prim)
        dq, dkp, dkc, dvp, dvc, dsnk = vjp([_stack_heads(do_ref, k) for k in range(AKV)])
        for h in range(AH):
            k, r = divmod(h, AREP)
            dq_ref[:, h * AHD:(h + 1) * AHD] = dq[k][r * CH:(r + 1) * CH].astype(BF16)
            dsnk_ref[:, h:h + 1] += jnp.sum(dsnk[k][r * CH:(r + 1) * CH], 0, keepdims=True)
        for k in range(AKV):
            dcur_ref[:, k * AHD:(k + 1) * AHD] = dkc[k]
            dcur_ref[:, KV + k * AHD:KV + (k + 1) * AHD] = dvc[k]
            dprv_ref[:, k * AHD:(k + 1) * AHD] = dkp[k]
            dprv_ref[:, KV + k * AHD:KV + (k + 1) * AHD] = dvp[k]

    kv = pl.BlockSpec((CH, 2 * KV), lambda n: (n, 0))
    return pl.pallas_call(
        body, grid=(T // CH,), in_specs=_swa_specs(T) + [pl.BlockSpec((CH, D), lambda n: (n, 0))],
        out_specs=[pl.BlockSpec((CH, D), lambda n: (n, 0)), kv, kv, pl.BlockSpec((1, LANE), lambda n: (0, 0))],
        out_shape=[S((T, D), BF16), S((T, 2 * KV), F32), S((T, 2 * KV), F32), S((1, LANE), F32)],
        compiler_params=_params("arbitrary"), name=name)(qkvr, qkvr, qkvr, snk, do)


def _xat_math(q, k, v):
    outs = []
    for h in range(XH):
        s = lax.dot_general(q[h].astype(BF16), k[h].astype(BF16), _DIMS["nt"], preferred_element_type=F32) * X_SCALE
        m = lax.stop_gradient(jnp.max(s, -1, keepdims=True))
        p = jnp.exp(s - m)
        pr = p / jnp.sum(p, -1, keepdims=True)
        outs.append(jnp.dot(pr.astype(BF16), v[h].astype(BF16), preferred_element_type=F32))
    return outs


def _xat_load(q_ref, kv_ref):
    q = [q_ref[:, h * XHD:(h + 1) * XHD].astype(F32) for h in range(XH)]
    k = [kv_ref[:, h * XHD:(h + 1) * XHD].astype(F32) for h in range(XH)]
    v = [kv_ref[:, XW + h * XHD:XW + (h + 1) * XHD].astype(F32) for h in range(XH)]
    return q, k, v


def _xat_fwd(name, q, kv):
    T, M = q.shape[0], kv.shape[0]
    tt = _tile(T, 512)

    def body(q_ref, kv_ref, o_ref):
        outs = _xat_math(*_xat_load(q_ref, kv_ref))
        for h in range(XH):
            o_ref[:, h * XHD:(h + 1) * XHD] = outs[h].astype(BF16)

    return pl.pallas_call(
        body, grid=(T // tt,),
        in_specs=[pl.BlockSpec((tt, XW), lambda i: (i, 0)), pl.BlockSpec((M, 2 * XW), lambda i: (0, 0))],
        out_specs=pl.BlockSpec((tt, XW), lambda i: (i, 0)), out_shape=S((T, XW), BF16),
        compiler_params=_params("parallel"), name=name)(q, kv)


def _xat_bwd(name, q, kv, do):
    T, M = q.shape[0], kv.shape[0]
    tt = _tile(T, 512)

    def body(q_ref, kv_ref, do_ref, dq_ref, dkv_ref):
        first = pl.program_id(0) == 0
        _, vjp = jax.vjp(_xat_math, *_xat_load(q_ref, kv_ref))
        dq, dk, dv = vjp([do_ref[:, h * XHD:(h + 1) * XHD].astype(F32) for h in range(XH)])
        for h in range(XH):
            sl = slice(h * XHD, (h + 1) * XHD)
            dq_ref[:, sl] = dq[h].astype(BF16)
            _acc_store(first, dkv_ref, (slice(None), sl), dk[h])
            _acc_store(first, dkv_ref, (slice(None), slice(XW + h * XHD, XW + (h + 1) * XHD)), dv[h])

    qs = pl.BlockSpec((tt, XW), lambda i: (i, 0))
    kvs = pl.BlockSpec((M, 2 * XW), lambda i: (0, 0))
    return pl.pallas_call(
        body, grid=(T // tt,), in_specs=[qs, kvs, qs], out_specs=[qs, kvs],
        out_shape=[S((T, XW), BF16), S((M, 2 * XW), F32)],
        compiler_params=_params("arbitrary"), name=name)(q, kv, do)


def _loss_head(name, x, gain, target):
    T = x.shape[0]
    tt = _tile(T, 512)

    def body(x_ref, g_ref, t_ref, l_ref, dx_ref, dxb_ref, dg_ref):
        first = pl.program_id(0) == 0
        xv, g = x_ref[...], g_ref[...]
        r = lax.rsqrt(jnp.mean(xv * xv, -1, keepdims=True) + EPS)
        xh = xv * r
        e = xh * g - t_ref[...]
        part = 0.5 * jnp.sum(jnp.mean(e * e, -1, keepdims=True), (0, 1), keepdims=True)
        _acc_store(first, l_ref, (slice(None), slice(None)), jnp.broadcast_to(part, (1, LANE)))
        dy = e * (1.0 / D)
        dxh = dy * g
        dx = r * (dxh - xh * jnp.mean(dxh * xh, -1, keepdims=True))
        dx_ref[...] = dx
        dxb_ref[...] = dx.astype(BF16)
        _acc_store(first, dg_ref, (slice(None), slice(None)), jnp.sum(dy * xh, 0, keepdims=True))

    row = pl.BlockSpec((tt, D), lambda i: (i, 0))
    vec = pl.BlockSpec((1, D), lambda i: (0, 0))
    return pl.pallas_call(
        body, grid=(T // tt,), in_specs=[row, vec, row],
        out_specs=[pl.BlockSpec((1, LANE), lambda i: (0, 0)), row, row, vec],
        out_shape=[S((1, LANE), F32), S((T, D), F32), S((T, D), BF16), S((1, D), F32)],
        compiler_params=_params("arbitrary"), name=name)(x, gain, target)


def _out_proj(name, a, b, x, next_gain, scale=1.0, tk_t=2048, plain=(1024, 1024)):
    if next_gain is None:
        return _mm(name, "nn", a, b, F32, res=x, scale=scale, tm_t=plain[0], tn_t=plain[1], tk_t=tk_t), None
    return _mm(name, "nn", a, b, F32, res=x, scale=scale, norm_gain=next_gain, tm_t=512, tn_t=D, tk_t=min(tk_t, 2048))


def _ffn_fwd(tag, x, h, gain, wgu4, get_wd, next_gain):
    g, u, a = _ffn_up(f"{tag}_up", h, wgu4, 0)
    wd = get_wd(a).reshape(1, 1, DFF, D)
    x_new, _ = _out_proj(f"{tag}_down", a, Op(wd, "r"), x, None, 0.5, 2816)
    h_next = None if next_gain is None else _rms_fwd(f"{tag}_nextnorm", x_new, next_gain)
    return x_new, h_next, (x, gain, h, g, u, a)


def _ffn_bwd(tag, saved, dx, dxb, wgu4, wd4, put):
    x, gain, h, g, u, a = saved
    dgu = _ffn_dact(f"{tag}_dact", dxb, wd4, 0, g, u, 1024)
    dwd = _mm(f"{tag}_dwd", "tn", a, dxb, BF16, out=("r", 4, 1, 0), scale=0.5, tm_t=1408, tn_t=1024, tk_t=2048)
    dwgu = _mm(f"{tag}_dwgu", "tn", h, Op(dgu, "c"), BF16, out=("c", 4, 1, 0), tm_t=1024, tn_t=256, tk_t=8192)
    tok = put(dwgu, dwd)
    dh = _mm(f"{tag}_dh", "nt", Op(dgu, "c"), Op(wgu4, "c"), BF16, bias=jnp.zeros((1, D), F32) + tok, tk_t=2816)
    dx, dxb, dgain = _rms_bwd(f"{tag}_dnorm", x, gain, dh, dx)
    return dx, dxb, dgain


def _xattn_fwd(tag, x, hq, mem, gq, gm, wxq4, wxkv4, wxo4, next_gain):
    mn = _rms_fwd(f"{tag}_normm", mem, gm)
    q = _mm(f"{tag}_q", "nn", hq, Op(wxq4.reshape(1, 1, D, XW), "r"), BF16)
    kv = _mm(f"{tag}_kv", "nn", mn, Op(wxkv4, "r"), BF16)
    o = _xat_fwd(f"{tag}_att", q, kv)
    wxo = jnp.transpose(wxo4[:, 0], (1, 0, 2)).reshape(XW, D)
    x_new, h_next = _out_proj(f"{tag}_o", o, wxo, x, next_gain)
    return x_new, h_next, (x, mem, gq, gm, hq, mn, q, kv, o, wxo)


def _xattn_bwd(tag, saved, dx, dxb, wxq4, wxkv4, wxo4, put):
    x, mem, gq, gm, hq, mn, q, kv, o, wxo = saved
    dwxo = _mm(f"{tag}_dwo", "tn", o, dxb, BF16, out=("c", 4, 1, 0))
    do = _mm(f"{tag}_do", "nt", dxb, wxo, BF16)
    dq, dkv = _xat_bwd(f"{tag}_datt", q, kv, do)
    dwxq = _mm(f"{tag}_dwq", "tn", hq, dq, BF16, out=("r", 4, 1, 0))
    dwxkv = _mm(f"{tag}_dwkv", "tn", mn, dkv, BF16, out=("r", 4, 1, 0))
    tok = put(dwxq, dwxkv, dwxo)
    dhq = _mm(f"{tag}_dhq", "nt", dq, Op(wxq4.reshape(1, 1, D, XW), "r"), BF16, bias=jnp.zeros((1, D), F32) + tok)
    dmn = _mm(f"{tag}_dmn", "nt", dkv, Op(wxkv4, "r"), BF16)
    _, _, dgm = _rms_bwd(f"{tag}_dnormm", mem, gm, dmn)
    dx, dxb, dgq = _rms_bwd(f"{tag}_dnormq", x, gq, dhq, dx)
    return dx, dxb, dgq, dgm


def _even_fwd(tag, x, h, gain, w_main, w_dt, p, wout4, next_gain):
    proj = _mm(f"{tag}_in", "nn", h, w_main, BF16)
    dtr = _mm(f"{tag}_indt", "nn", h, w_dt, F32)
    mix = _gmlp_fwd(f"{tag}_gmlp", proj, p["lng"], p["lnb"], p["ws"], p["bs"])
    ypre, xc = _conv_fwd(f"{tag}_conv", proj, p["cw"], p["cb"])
    mix, prevs = _ssd_fwd(f"{tag}_ssd", xc, dtr, proj, p["dtb"], p["alog"], p["dsk"], p["nrm"], mix)
    x_new, h_next = _out_proj(f"{tag}_out", mix, Op(wout4.reshape(1, 1, 2 * D, D), "r"), x, next_gain)
    return x_new, h_next, (x, gain, h, proj, dtr, mix, ypre, xc, prevs)


def _even_bwd(tag, saved, dx, dxb, w_main, w_dt, p, wout4, put):
    x, gain, h, proj, dtr, mix, ypre, xc, prevs = saved
    T = x.shape[0]
    dwout = _mm(f"{tag}_dwout", "tn", mix, dxb, BF16, out=("r", 4, 1, 0))
    dmix = _mm(f"{tag}_dmix", "nt", dxb, Op(wout4, "r", 0), BF16)
    dproj = lax.empty((T, EVEN_MAIN), BF16)
    dproj, dlng, dlnb, dws, dbs = _gmlp_bwd(f"{tag}_dgmlp", proj, p["lng"], p["lnb"], p["ws"], p["bs"], dmix, dproj)
    dproj, dxc, ddtr, ddtb, dalog, ddsk, dnrm = _ssd_bwd(
        f"{tag}_dssd", xc, dtr, proj, prevs, p["dtb"], p["alog"], p["dsk"], p["nrm"], dmix, dproj)
    dproj, dcw, dcb = _conv_bwd(f"{tag}_dconv", proj, ypre, dxc, p["cw"], dproj)
    dw_main = _mm(f"{tag}_dwin", "tn", h, dproj, BF16, tm_t=1024, tn_t=256, tk_t=8192)
    dw_dt = _mm(f"{tag}_dwdt", "tn", h, ddtr, BF16)
    tok = put(dw_main, dw_dt, dwout)
    dh = _mm(f"{tag}_dh1", "nt", ddtr, w_dt + tok.astype(BF16), F32)
    dh = _mm(f"{tag}_dh2", "nt", dproj, w_main, BF16, res=dh, tk_t=3072)
    dx, dxb, dgain = _rms_bwd(f"{tag}_dnorm", x, gain, dh, dx)
    small = dict(lng=dlng, lnb=dlnb, ws=dws, bs=dbs, cw=dcw, cb=dcb, dtb=ddtb, alog=dalog, dsk=ddsk, nrm=dnrm)
    return dx, dxb, dgain, small


def _odd_fwd(tag, x, h, gain, wqkv4, bqkv, snk, wo4, cos, sin, next_gain):
    wqkv = jnp.transpose(wqkv4[:, 0], (1, 0, 2)).reshape(D, ODD_IN)
    qkv = _mm(f"{tag}_qkv", "nn", h, wqkv, F32, bias=bqkv, tn_t=1280)
    qkvr = _rope_fwd(f"{tag}_rope", qkv, cos, sin)
    o = _swa_fwd(f"{tag}_swa", qkvr, snk)
    x_new, h_next = _out_proj(f"{tag}_o", o, Op(wo4.reshape(1, 1, D, D), "r"), x, next_gain)
    return x_new, h_next, (x, gain, h, qkvr, o, wqkv)


def _odd_bwd(tag, saved, dx, dxb, wqkv4, snk, wo4, cos, sin, put):
    x, gain, h, qkvr, o, wqkv = saved
    dwo = _mm(f"{tag}_dwo", "tn", o, dxb, BF16, out=("r", 4, 1, 0))
    do = _mm(f"{tag}_do", "nt", dxb, Op(wo4.reshape(1, 1, D, D), "r"), BF16)
    dq, dcur, dprv, dsnk = _swa_bwd(f"{tag}_dswa", qkvr, snk, do)
    dqkv, dbias = _rope_bwd(f"{tag}_drope", dq, dcur, dprv, cos, sin)
    dwqkv = _mm(f"{tag}_dwqkv", "tn", h, dqkv, BF16, out=("c", 4, 1, 0), tn_t=640)
    tok = put(dwqkv, dwo)
    dh = _mm(f"{tag}_dh", "nt", dqkv, wqkv, BF16, bias=jnp.zeros((1, D), F32) + tok, tk_t=ODD_IN)
    dx, dxb, dgain = _rms_bwd(f"{tag}_dnorm", x, gain, dh, dx)
    return dx, dxb, dgain, dbias, dsnk


def _row(v):
    return v.reshape(1, -1).astype(F32)


def _pad_lanes(v, n=LANE):
    v = v.reshape(1, -1).astype(F32)
    return jnp.pad(v, ((0, 0), (0, n - v.shape[1])))


def _local_step(x, mem, positions, target, getw, P, putg):
    inv_freq = ROPE_THETA ** (-jnp.arange(0, ROT, 2, dtype=F32) / ROT)
    ang = positions.astype(F32)[:, None] * inv_freq
    cos8, sin8 = jnp.cos(ang), jnp.sin(ang)
    ones, zeros = jnp.ones((x.shape[0], AHD - ROT), F32), jnp.zeros((x.shape[0], AHD - ROT), F32)
    cos = jnp.tile(jnp.concatenate([cos8, cos8, ones], 1), (1, 2))
    sin = jnp.tile(jnp.concatenate([-sin8, sin8, zeros], 1), (1, 2))

    snk = _pad_lanes(P["sinks"])
    W = {}

    def w(name, layer, after):
        if (name, layer) not in W:
            W[name, layer] = getw(name, layer, after)
        return W[name, layer]

    saved = []
    h = _rms_fwd("l0_ffn1_norm", x, _row(P["norm_ffn1"][0]))
    for i in range(2):
        x, h, s1 = _ffn_fwd(f"l{i}_ffn1", x, h, _row(P["norm_ffn1"][i]), w("w_ffn1_gu", i, x),
                            functools.partial(w, "w_ffn1_down", i), _row(P["norm_mix"][i]))
        if i == 0:
            ev = dict(lng=_row(P["gm_ln_g"]), lnb=_row(P["gm_ln_b"]), ws=P["gm_ws"].reshape(GM_G, CH, CH),
                      bs=P["gm_bs"].reshape(GM_G, CH, 1), cw=w("conv_w", 0, x), cb=_row(P["conv_b"]),
                      dtb=_pad_lanes(P["dt_bias"]), alog=_pad_lanes(P["a_log"]), dsk=_pad_lanes(P["d_skip"]),
                      nrm=_row(P["ssd_norm"]))
            w_main, w_dt = w("w_in_even", 0, x)
            x, h, s2 = _even_fwd("l0_mix", x, h, _row(P["norm_mix"][0]), w_main, w_dt, ev, w("w_out_even", 0, x),
                                 _row(P["norm_xq"][0]))
        else:
            x, h, s2 = _odd_fwd("l1_mix", x, h, _row(P["norm_mix"][1]), w("w_qkv", 0, x), w("b_qkv", 0, x), snk,
                                w("w_o_odd", 0, x), cos, sin, _row(P["norm_xq"][1]))
        x, h, s3 = _xattn_fwd(f"l{i}_xat", x, h, mem, _row(P["norm_xq"][i]), _row(P["norm_mem"][i]),
                              w("w_xq", i, x), w("w_xkv", i, x), w("w_xo", i, x), _row(P["norm_ffn2"][i]))
        x, h, s4 = _ffn_fwd(f"l{i}_ffn2", x, h, _row(P["norm_ffn2"][i]), w("w_ffn2_gu", i, x),
                            functools.partial(w, "w_ffn2_down", i), _row(P["norm_ffn1"][1]) if i == 0 else None)
        saved.append((s1, s2, s3, s4))

    loss, dx, dxb, d_final = _loss_head("loss_head", x, _row(P["final_norm"]), target)

    sm = {}
    dn = {k: [None, None] for k in ("norm_ffn1", "norm_mix", "norm_xq", "norm_mem", "norm_ffn2")}
    for i in (1, 0):
        s1, s2, s3, s4 = saved[i]
        dx, dxb, dn["norm_ffn2"][i] = _ffn_bwd(
            f"l{i}_ffn2", s4, dx, dxb, W["w_ffn2_gu", i], W["w_ffn2_down", i],
            lambda dwgu, dwd, i=i: putg({("w_ffn2_gu", i): dwgu, ("w_ffn2_down", i): dwd}))
        dx, dxb, dn["norm_xq"][i], dn["norm_mem"][i] = _xattn_bwd(
            f"l{i}_xat", s3, dx, dxb, W["w_xq", i], W["w_xkv", i], W["w_xo", i],
            lambda dwxq, dwxkv, dwxo, i=i: putg({("w_xq", i): dwxq, ("w_xkv", i): dwxkv, ("w_xo", i): dwxo}))
        if i == 0:
            dx, dxb, dn["norm_mix"][0], sm_even = _even_bwd(
                "l0_mix", s2, dx, dxb, w_main, w_dt, ev, W["w_out_even", 0],
                lambda dw_main, dw_dt, dwout: putg({("w_in_even", 0): (dw_main, dw_dt), ("w_out_even", 0): dwout}))
        else:
            dx, dxb, dn["norm_mix"][1], sm["b_qkv"], sm["sinks"] = _odd_bwd(
                "l1_mix", s2, dx, dxb, W["w_qkv", 0], snk, W["w_o_odd", 0], cos, sin,
                lambda dwqkv, dwo: putg({("w_qkv", 0): dwqkv, ("w_o_odd", 0): dwo}))
        dx, dxb, dn["norm_ffn1"][i] = _ffn_bwd(
            f"l{i}_ffn1", s1, dx, dxb, W["w_ffn1_gu", i], W["w_ffn1_down", i],
            lambda dwgu, dwd, i=i: putg({("w_ffn1_gu", i): dwgu, ("w_ffn1_down", i): dwd}))
    for k, v in dn.items():
        sm[k] = jnp.concatenate(v, 0)
    sm.update(gm_ln_g=sm_even["lng"], gm_ln_b=sm_even["lnb"], gm_ws=sm_even["ws"], gm_bs=sm_even["bs"],
              conv_w=sm_even["cw"], conv_b=sm_even["cb"], dt_bias=sm_even["dtb"][:, :NH], a_log=sm_even["alog"][:, :NH],
              d_skip=sm_even["dsk"][:, :NH], ssd_norm=sm_even["nrm"], sinks=sm["sinks"][:, :AH], final_norm=d_final)
    return loss[0, 0], dx, sm


def _chip_peers():
    x, y, c = lax.axis_index("x"), lax.axis_index("y"), lax.axis_index("c")
    return 2 * x + y, [((1 - x, y, c), 2 * (1 - x) + y), ((x, 1 - y, c), 2 * x + (1 - y)),
                       ((1 - x, 1 - y, c), 2 * (1 - x) + (1 - y))]


def _any_specs(n):
    return [pl.BlockSpec(memory_space=pl.ANY)] * n


_HBM = pl.BlockSpec(memory_space=pltpu.HBM)
_SEM = pl.BlockSpec(memory_space=pltpu.SEMAPHORE)
_EFFECT = pltpu.SideEffectType.DATAFLOW_SIDE_EFFECTING


def _own_slot(piece, chip):
    zone = lax.empty((4,) + piece.shape, piece.dtype)
    return lax.dynamic_update_slice(zone, piece[None], (chip,) + (0,) * piece.ndim)


def _chip_copies(srcs, lands, ssems, rsems, mode="chips"):
    c = lax.axis_index("c")
    sib = (lax.axis_index("x"), lax.axis_index("y"), 1 - c)
    if mode == "sibling":
        return [pltpu.make_async_remote_copy(src_ref=srcs[i], dst_ref=lands[i], send_sem=ssems[i].at[0],
                                             recv_sem=rsems[i].at[0], device_id=sib, device_id_type=MESH)
                for i in range(len(lands))]
    me, peers = _chip_peers()
    if mode == "half":
        return [pltpu.make_async_remote_copy(src_ref=lands[i].at[me, c], dst_ref=lands[i].at[me, c], send_sem=ssems[i].at[j],
                                             recv_sem=rsems[i].at[j], device_id=dev, device_id_type=MESH)
                for i in range(len(lands)) for j, (dev, _) in enumerate(peers)]
    if mode == "pass_on":
        return [pltpu.make_async_remote_copy(src_ref=lands[i].at[chip, c], dst_ref=lands[i].at[chip, c],
                                             send_sem=ssems[i].at[j], recv_sem=rsems[i].at[j], device_id=sib, device_id_type=MESH)
                for i in range(len(lands)) for j, (_, chip) in enumerate(peers)]
    return [pltpu.make_async_remote_copy(src_ref=lands[i].at[me] if srcs[i] is None else srcs[i].at[chip],
                                         dst_ref=lands[i].at[me], send_sem=ssems[i].at[j], recv_sem=rsems[i].at[j],
                                         device_id=dev, device_id_type=MESH)
            for i in range(len(lands)) for j, (dev, chip) in enumerate(peers)]


def _exchange_start(name, srcs, lands, mode="chips"):
    n = len(lands)
    ns = 0 if srcs is None else n

    def body(*refs):
        src_refs = [None] * n if srcs is None else refs[:n]
        land_refs = refs[ns:ns + n]
        ssems, rsems = refs[ns + n:ns + 2 * n], refs[ns + 2 * n:ns + 3 * n]
        token = refs[2 * ns + 4 * n]
        for cp in _chip_copies(src_refs, land_refs, ssems, rsems, mode):
            cp.start()
        token[...] = jnp.zeros_like(token)

    ins = ([] if srcs is None else list(srcs)) + list(lands)
    res = pl.pallas_call(
        body, name=name,
        out_shape=[pltpu.SemaphoreType.DMA((1 if mode == "sibling" else 3,))] * (2 * n) + [pltpu.HBM(a.shape, a.dtype) for a in ins]
        + [S((8, LANE), F32)],
        in_specs=[_HBM] * (ns + n),
        out_specs=[_SEM] * (2 * n) + [_HBM] * (ns + n) + [pl.BlockSpec(memory_space=pltpu.VMEM)],
        input_output_aliases={i: 2 * n + i for i in range(ns + n)},
        compiler_params=pltpu.CompilerParams(has_side_effects=_EFFECT),
    )(*[pltpu.with_memory_space_constraint(a, pltpu.HBM) for a in ins])
    items = [(res[i], res[n + i], None if srcs is None else res[2 * n + i], res[2 * n + ns + i]) for i in range(n)]
    return items, res[2 * n + ns + n]


def _exchange_wait(name, item, after, mode="chips", whole=False):
    ssem, rsem, src, land = item
    ns = 0 if src is None else 1

    def body(*refs):
        src_ref = refs[0] if ns else None
        land_ref, ssem_ref, rsem_ref = refs[ns], refs[ns + 1], refs[ns + 2]
        for cp in _chip_copies([src_ref], [land_ref], [ssem_ref], [rsem_ref], mode):
            cp.wait_send()
            cp.wait_recv()

    ins = ([src] if ns else []) + [land]
    res = pl.pallas_call(
        body, name=name, out_shape=[pltpu.HBM(a.shape, a.dtype) for a in ins],
        in_specs=[_HBM] * (ns + 1) + [_SEM, _SEM, pl.BlockSpec(memory_space=pl.ANY)], out_specs=[_HBM] * (ns + 1),
        input_output_aliases={i: i for i in range(ns + 1)}, compiler_params=pltpu.CompilerParams(has_side_effects=_EFFECT),
    )(*ins, ssem, rsem, after)
    return res if whole else res[ns]


def _gather_all(name, v, after):
    def body(v_ref, _, o_ref, ssem, rsem, lsem):
        x, y, c = lax.axis_index("x"), lax.axis_index("y"), lax.axis_index("c")
        me = 4 * x + 2 * y + c
        loc = pltpu.make_async_copy(v_ref, o_ref.at[me], lsem)
        loc.start()
        copies = []
        for k in range(1, 8):
            fx, fy, fc = (k >> 2) & 1, (k >> 1) & 1, k & 1
            dev = (x ^ fx, y ^ fy, c ^ fc)
            cp = pltpu.make_async_remote_copy(src_ref=v_ref, dst_ref=o_ref.at[me], send_sem=ssem.at[k - 1],
                                              recv_sem=rsem.at[k - 1], device_id=dev, device_id_type=MESH)
            cp.start()
            copies.append(cp)
        for cp in copies:
            cp.wait()
        loc.wait()

    return pl.pallas_call(
        body, in_specs=_any_specs(2), out_specs=pl.BlockSpec(memory_space=pl.ANY), out_shape=S((8,) + v.shape, v.dtype),
        scratch_shapes=[pltpu.SemaphoreType.DMA((7,)), pltpu.SemaphoreType.DMA((7,)), pltpu.SemaphoreType.DMA(())],
        compiler_params=pltpu.CompilerParams(has_side_effects=True), name=name)(v, after)


def _row_tile(R, row_bytes, budget=4 << 20):
    if R * row_bytes <= budget or R % 16:
        return R
    t = max(16, budget // row_bytes // 16 * 16)
    while R % t:
        t -= 16
    return t


def _sum_slots(name, r, n):
    _, R, C = r.shape
    tr = _row_tile(R, C * (n * r.dtype.itemsize + 4))

    def body(r_ref, o_ref):
        acc = r_ref[0].astype(F32)
        for j in range(1, n):
            acc = acc + r_ref[j].astype(F32)
        o_ref[...] = acc

    return pl.pallas_call(
        body, grid=(R // tr,), in_specs=[pl.BlockSpec((n, tr, C), lambda i: (0, i, 0))],
        out_specs=pl.BlockSpec((tr, C), lambda i: (i, 0)), out_shape=S((R, C), F32),
        compiler_params=_params("parallel"), name=name)(r)


def _adamw(name, w, m, v, layer, g1, g2=None, into=None):
    nl, R, C = w.shape
    tr = _row_tile(R, C * 4 * 9)
    two, has_into = g2 is not None, into is not None

    def body(w_ref, m_ref, v_ref, g1_ref, *rest):
        rest = list(rest)
        g = g1_ref[...]
        if two:
            g = g + rest.pop(0)[...]
        g_ref, d_ref, nm_ref, nv_ref = rest[-4:]
        mn = B1 * m_ref[...] + (1.0 - B1) * g
        vn = B2 * v_ref[...] + (1.0 - B2) * jnp.square(g)
        m_hat = mn / (1.0 - B1 ** STEP)
        v_hat = vn / (1.0 - B2 ** STEP)
        g_ref[...] = g
        d_ref[...] = -LR * (m_hat / (jnp.sqrt(v_hat) + AEPS) + WD * w_ref[...])
        nm_ref[...] = mn
        nv_ref[...] = vn

    blk = pl.BlockSpec((tr, C), lambda i: (i, 0))
    lay = pl.BlockSpec((None, tr, C), lambda i: (layer, i, 0))
    args = [w, m, v, g1] + ([g2] if two else []) + (list(into) if has_into else [])
    in_specs = [lay] * 3 + [blk] * (2 if two else 1) + (_any_specs(4) if has_into else [])
    aliases = {len(args) - 4 + t: t for t in range(4)} if has_into else {}
    return pl.pallas_call(
        body, grid=(R // tr,), in_specs=in_specs, out_specs=[lay] * 4, out_shape=[S((nl, R, C), F32)] * 4,
        input_output_aliases=aliases, compiler_params=_params("parallel"), name=name)(*args)


_USE_ORDER = [("w_ffn1_gu", 0), ("w_ffn1_down", 0), ("conv_w", 0), ("w_in_even", 0), ("w_out_even", 0), ("w_xq", 0),
              ("w_xkv", 0), ("w_xo", 0), ("w_ffn2_gu", 0), ("w_ffn2_down", 0), ("w_ffn1_gu", 1), ("w_ffn1_down", 1),
              ("w_qkv", 0), ("b_qkv", 0), ("w_o_odd", 0), ("w_xq", 1), ("w_xkv", 1), ("w_xo", 1), ("w_ffn2_gu", 1),
              ("w_ffn2_down", 1)]
_SMALL = ["norm_ffn1", "norm_mix", "gm_ln_g", "gm_ln_b", "gm_ws", "gm_bs", "conv_w", "conv_b", "dt_bias", "a_log",
          "d_skip", "ssd_norm", "b_qkv", "sinks", "norm_xq", "norm_mem", "norm_ffn2", "final_norm"]
_WEIGHTS = ["norm_ffn1", "w_ffn1_gu", "w_ffn1_down", "norm_mix", "w_in_even", "gm_ln_g", "gm_ln_b", "gm_ws", "gm_bs",
            "conv_w", "conv_b", "dt_bias", "a_log", "d_skip", "ssd_norm", "w_out_even", "w_qkv", "b_qkv", "sinks",
            "w_o_odd", "norm_xq", "norm_mem", "w_xq", "w_xkv", "w_xo", "norm_ffn2", "w_ffn2_gu", "w_ffn2_down",
            "final_norm"]


def _pack(arrs):
    rows = []
    for a in arrs:
        f = a.reshape(-1).astype(F32)
        pad = (-f.shape[0]) % LANE
        rows.append(jnp.pad(f, (0, pad)).reshape(-1, LANE))
    out = jnp.concatenate(rows, 0)
    pad = (-out.shape[0]) % 8
    return jnp.pad(out, ((0, pad), (0, 0)))


def _unpack(packed, shapes):
    outs, r = [], 0
    for shp in shapes:
        n = math.prod(shp)
        nr = -(-n // LANE)
        outs.append(packed[r:r + nr].reshape(-1)[:n].reshape(shp))
        r += nr
    return outs


def kernel(x, mem, positions, norm_ffn1, w_ffn1_gu, w_ffn1_down, norm_mix, w_in_even, gm_ln_g, gm_ln_b, gm_ws, gm_bs, conv_w, conv_b, dt_bias, a_log, d_skip, ssd_norm, w_out_even, w_qkv, b_qkv, sinks, w_o_odd, norm_xq, norm_mem, w_xq, w_xkv, w_xo, norm_ffn2, w_ffn2_gu, w_ffn2_down, final_norm, loss_target, m_norm_ffn1, m_w_ffn1_gu, m_w_ffn1_down, m_norm_mix, m_w_in_even, m_gm_ln_g, m_gm_ln_b, m_gm_ws, m_gm_bs, m_conv_w, m_conv_b, m_dt_bias, m_a_log, m_d_skip, m_ssd_norm, m_w_out_even, m_w_qkv, m_b_qkv, m_sinks, m_w_o_odd, m_norm_xq, m_norm_mem, m_w_xq, m_w_xkv, m_w_xo, m_norm_ffn2, m_w_ffn2_gu, m_w_ffn2_down, m_final_norm, v_norm_ffn1, v_w_ffn1_gu, v_w_ffn1_down, v_norm_mix, v_w_in_even, v_gm_ln_g, v_gm_ln_b, v_gm_ws, v_gm_bs, v_conv_w, v_conv_b, v_dt_bias, v_a_log, v_d_skip, v_ssd_norm, v_w_out_even, v_w_qkv, v_b_qkv, v_sinks, v_w_o_odd, v_norm_xq, v_norm_mem, v_w_xq, v_w_xkv, v_w_xo, v_norm_ffn2, v_w_ffn2_gu, v_w_ffn2_down, v_final_norm):
    a = dict(locals())
    w = {k: a[k] for k in _WEIGHTS}
    mom = {k: a["m_" + k] for k in _WEIGHTS}
    var = {k: a["v_" + k] for k in _WEIGHTS}
    chip = 2 * lax.axis_index("x") + lax.axis_index("y")

    shards = [w[k][i:i + 1] if k in ("conv_w", "b_qkv") else w[k][i:i + 1].astype(BF16) for k, i in _USE_ORDER]
    half = shards[0].reshape((2, shards[0].shape[1] // 2) + shards[0].shape[2:])
    (first,), first_started = _exchange_start("gather_start_first", None, [_own_slot(half, chip)], "half")
    small = _USE_ORDER.index(("conv_w", 0))
    shards[small] = shards[small] + first_started[0, 0]
    rest, rest_started = _exchange_start("gather_start_rest", None, [_own_slot(s, chip) for s in shards[1:]])
    pending = dict(zip(_USE_ORDER[1:], rest))

    def getw(name, layer, after):
        if (name, layer) == _USE_ORDER[0]:
            zone = _exchange_wait("gather_wait_first_half", first, after, "half")
            (passed,), _ = _exchange_start("gather_pass_on_first", None, [zone], "pass_on")
            return _exchange_wait("gather_wait_first_passed", passed, after, "pass_on").reshape((4,) + shards[0].shape)
        got = _exchange_wait(f"gather_wait_{name}_{layer}", pending.pop((name, layer)), after)
        if name == "w_in_even":
            w_in = jnp.transpose(got[:, 0], (1, 0, 2)).reshape(D, EVEN_IN)
            return w_in[:, :EVEN_MAIN], jnp.pad(w_in[:, EVEN_MAIN:], ((0, 0), (0, LANE - (EVEN_IN - EVEN_MAIN))))
        if name == "conv_w":
            return jnp.transpose(got[:, 0], (1, 0, 2)).reshape(4, CONV_C)
        if name == "b_qkv":
            return got.reshape(1, ODD_IN)
        return got

    sent = []

    def putg(grads):
        names, arrs = [], []
        for (name, layer), g in grads.items():
            if name == "w_in_even":
                dw_in = jnp.concatenate([g[0], g[1][:, :EVEN_IN - EVEN_MAIN]], 1)
                g = jnp.transpose(dw_in.reshape(D, 4, EVEN_IN // 4), (1, 0, 2)).reshape(4, 1, D, EVEN_IN // 4)
            names.append((name, layer))
            arrs.append(g)
        own = [_own_slot(lax.dynamic_index_in_dim(g, chip, 0, keepdims=False), chip) for g in arrs]
        its, tok = _exchange_start(f"scatter_start_{names[0][0]}_{names[0][1]}", arrs, own)
        sent.append(list(zip(names, its)))
        return tok[0, 0]

    P = {k: w[k] for k in _SMALL}
    P["norm_ffn1"] = P["norm_ffn1"] + rest_started[0, 0]
    loss, grad_x, sm = _local_step(x[0], mem[0], positions[0], loss_target[0], getw, P, putg)
    loss = lax.psum(loss, ("x", "y", "c"))

    out = {}

    def update(groups, after):
        flying = []
        for grp in groups:
            part = []
            for (name, layer), it in grp:
                r = _exchange_wait(f"scatter_wait_{name}_{layer}", it, after)
                part.append(_sum_slots(f"sum_{name}_{layer}", r.reshape(4, -1, r.shape[-1]), 4))
            name0, layer0 = grp[0][0]
            its, after = _exchange_start(f"swap_start_{name0}_{layer0}", part, [lax.empty(p.shape, p.dtype) for p in part],
                                         "sibling")
            flying += [(nm, it) for (nm, _), it in zip(grp, its)]
        for (name, layer), it in flying:
            p1, p2 = _exchange_wait(f"swap_wait_{name}_{layer}", it, after, "sibling", whole=True)
            out[name] = _adamw(f"adamw_{name}_{layer}", w[name], mom[name], var[name], layer, p1, p2, out.get(name))
            after = out[name][0]

    update(sent[:-1], grad_x)
    done_a = out["w_out_even"][0]
    update(sent[-1:], done_a)

    full_shapes = {k: w[k].shape for k in _SMALL}
    full_shapes["conv_w"], full_shapes["b_qkv"] = (1, 4, CONV_C), (1, ODD_IN)
    packed = _pack([sm[k] for k in _SMALL])
    total = _sum_slots("sum_small", _gather_all("gather_small", packed, done_a), 8)
    gs = dict(zip(_SMALL, _unpack(total, [full_shapes[k] for k in _SMALL])))
    gs["conv_w"] = lax.dynamic_slice_in_dim(gs["conv_w"], chip * (CONV_C // 4), CONV_C // 4, 2)
    gs["b_qkv"] = lax.dynamic_slice_in_dim(gs["b_qkv"], chip * (ODD_IN // 4), ODD_IN // 4, 1)
    res = _adamw("adamw_small", _pack([w[k] for k in _SMALL])[None], _pack([mom[k] for k in _SMALL])[None],
                 _pack([var[k] for k in _SMALL])[None], 0, _pack([gs[k] for k in _SMALL]))
    shapes = [w[k].shape for k in _SMALL]
    for k, g, d, nm, nv in zip(_SMALL, *[_unpack(r[0], shapes) for r in res]):
        out[k] = [g, d, nm, nv]

    return (loss, grad_x[None], *[out[k][0] for k in _WEIGHTS], *[out[k][1] for k in _WEIGHTS],
            *[out[k][2] for k in _WEIGHTS], *[out[k][3] for k in _WEIGHTS])
```

```python
import functools
import math

import jax
import jax.numpy as jnp
from jax import lax
from jax.experimental import pallas as pl
from jax.experimental.pallas import tpu as pltpu

F32, BF16 = jnp.float32, jnp.bfloat16
S = jax.ShapeDtypeStruct
MESH = pl.DeviceIdType.MESH

D = 2048
DFF = 5632
EPS = 1e-5
CH = 128
GM_G, GM_GD = 4, 512
NH, HD, NG, HPG, NS = 32, 64, 4, 8, 128
CONV_C = 3072
EVEN_MAIN, EVEN_IN = 9216, 9248
AH, AKV, AREP, AHD = 32, 4, 8, 64
ODD_IN = 2560
XH, XHD, XW = 4, 128, 512
ATT_SCALE = AHD ** -0.5
X_SCALE = XHD ** -0.5
ROPE_THETA = 500000.0
ROT = 16
LR, B1, B2, AEPS, WD, STEP = 0.001, 0.9, 0.999, 1e-08, 0.01, 10
LANE = 128
VMEM_LIMIT_V7X = 56 * 1024 * 1024


def _params(*sem):
    return pltpu.CompilerParams(dimension_semantics=sem, vmem_limit_bytes=VMEM_LIMIT_V7X)


def _tile(dim, target):
    if dim <= target:
        return dim
    t = (target // LANE) * LANE
    while t > LANE and dim % t:
        t -= LANE
    assert dim % t == 0, (dim, target)
    return t


class Op:
    def __init__(self, arr, kind=None, layer=0):
        self.arr, self.kind, self.layer = arr, kind, layer
        if kind is None:
            self.R, self.C = arr.shape
        else:
            L = arr.shape[0]
            self.R = arr.shape[2] * (L if kind == "r" else 1)
            self.C = arr.shape[3] * (L if kind == "c" else 1)

    def unit(self, axis):
        if self.kind == "r" and axis == 0:
            return self.arr.shape[2]
        if self.kind == "c" and axis == 1:
            return self.arr.shape[3]
        return (self.R, self.C)[axis]

    def spec(self, tr, tc, pick):
        if self.kind is None:
            return pl.BlockSpec((tr, tc), lambda i, j, k: pick(i, j, k))
        l = self.layer
        if self.kind == "c":
            per = self.arr.shape[3] // tc
            return pl.BlockSpec((None, None, tr, tc),
                                lambda i, j, k: (pick(i, j, k)[1] // per, l, pick(i, j, k)[0], pick(i, j, k)[1] % per))
        per = self.arr.shape[2] // tr
        return pl.BlockSpec((None, None, tr, tc),
                            lambda i, j, k: (pick(i, j, k)[0] // per, l, pick(i, j, k)[0] % per, pick(i, j, k)[1]))


_DIMS = {"nn": (((1,), (0,)), ((), ())), "nt": (((1,), (1,)), ((), ())), "tn": (((0,), (0,)), ((), ()))}
_PICK_A = {"nn": lambda i, j, k: (i, k), "nt": lambda i, j, k: (i, k), "tn": lambda i, j, k: (k, i)}
_PICK_B = {"nn": lambda i, j, k: (k, j), "nt": lambda i, j, k: (j, k), "tn": lambda i, j, k: (k, j)}


def _mm(name, mode, a, b, out_dtype, *, out=None, res=None, bias=None, scale=1.0, norm_gain=None,
        tm_t=1024, tn_t=1024, tk_t=2048):
    if not isinstance(a, Op):
        a = Op(a)
    if not isinstance(b, Op):
        b = Op(b)
    if mode == "nn":
        M, K, N = a.R, a.C, b.C
        assert b.R == K
        um, uk, un = a.unit(0), math.gcd(a.unit(1), b.unit(0)), b.unit(1)
    elif mode == "nt":
        M, K, N = a.R, a.C, b.R
        assert b.C == K
        um, uk, un = a.unit(0), math.gcd(a.unit(1), b.unit(1)), b.unit(0)
    else:
        K, M, N = a.R, a.C, b.C
        assert b.R == K
        um, uk, un = a.unit(1), math.gcd(a.unit(0), b.unit(0)), b.unit(1)
    if out is not None:
        okind, oL, olayers, olayer = out
        if okind == "c":
            un = math.gcd(un, N // oL)
        else:
            um = math.gcd(um, M // oL)
    tm, tn, tk = _tile(um, tm_t), _tile(un, tn_t), _tile(uk, tk_t)
    gi, gj, gk = M // tm, N // tn, K // tk
    a_blk = (tm, tk) if mode != "tn" else (tk, tm)
    b_blk = {"nn": (tk, tn), "nt": (tn, tk), "tn": (tk, tn)}[mode]
    in_specs = [a.spec(*a_blk, _PICK_A[mode]), b.spec(*b_blk, _PICK_B[mode])]
    args = [a.arr, b.arr]
    if res is not None:
        in_specs.append(pl.BlockSpec((tm, tn), lambda i, j, k: (i, j)))
        args.append(res)
    if bias is not None:
        in_specs.append(pl.BlockSpec((1, tn), lambda i, j, k: (0, j)))
        args.append(bias)
    if out is None:
        out_shape = S((M, N), out_dtype)
        out_spec = pl.BlockSpec((tm, tn), lambda i, j, k: (i, j))
    else:
        shp = (oL, olayers, M, N // oL) if okind == "c" else (oL, olayers, M // oL, N)
        out_shape = S(shp, out_dtype)
        out_spec = Op(out_shape, okind, olayer).spec(tm, tn, lambda i, j, k: (i, j))
    has_res, has_bias, has_norm = res is not None, bias is not None, norm_gain is not None
    if has_norm:
        assert out is None and tn == N
        in_specs.append(pl.BlockSpec((1, N), lambda i, j, k: (0, 0)))
        args.append(norm_gain)
        out_shape = [out_shape, S((M, N), BF16)]
        out_spec = [out_spec, pl.BlockSpec((tm, tn), lambda i, j, k: (i, j))]
    dims = _DIMS[mode]

    def body(a_ref, b_ref, *rest):
        rest = list(rest)
        res_ref = rest.pop(0) if has_res else None
        bias_ref = rest.pop(0) if has_bias else None
        gain_ref = rest.pop(0) if has_norm else None
        o_ref = rest.pop(0)
        h_ref = rest.pop(0) if has_norm else None
        part = lax.dot_general(a_ref[...].astype(BF16), b_ref[...].astype(BF16), dims, preferred_element_type=F32)

        def finish(r):
            if scale != 1.0:
                r = r * scale
            if has_bias:
                r = r + bias_ref[...]
            if has_res:
                r = r + res_ref[...]
            o_ref[...] = r.astype(o_ref.dtype)
            if has_norm:
                h_ref[...] = (r * lax.rsqrt(jnp.mean(r * r, -1, keepdims=True) + EPS) * gain_ref[...]).astype(BF16)

        if gk == 1:
            finish(part)
            return
        acc, = rest
        k = pl.program_id(2)

        @pl.when(k == 0)
        def _():
            acc[...] = part

        @pl.when((k > 0) & (k < gk - 1))
        def _():
            acc[...] += part

        @pl.when(k == gk - 1)
        def _():
            finish(acc[...] + part)

    return pl.pallas_call(
        body, grid=(gi, gj, gk), in_specs=in_specs, out_specs=out_spec, out_shape=out_shape,
        scratch_shapes=[pltpu.VMEM((tm, tn), F32)] if gk > 1 else [],
        compiler_params=_params("parallel", "parallel", "arbitrary"), name=name)(*args)


def _rms_fwd(name, x, gain):
    T = x.shape[0]
    tt = _tile(T, 512)

    def body(x_ref, g_ref, o_ref):
        xv = x_ref[...]
        r = lax.rsqrt(jnp.mean(xv * xv, -1, keepdims=True) + EPS)
        o_ref[...] = (xv * r * g_ref[...]).astype(BF16)

    return pl.pallas_call(
        body, grid=(T // tt,),
        in_specs=[pl.BlockSpec((tt, D), lambda i: (i, 0)), pl.BlockSpec((1, D), lambda i: (0, 0))],
        out_specs=pl.BlockSpec((tt, D), lambda i: (i, 0)), out_shape=S((T, D), BF16),
        compiler_params=_params("parallel"), name=name)(x, gain)


RING = 3


def _rms_bwd_ring(name, x, gain, dh, dx_in, tt):
    T = x.shape[0]
    n = T // tt

    def body(x_hbm, g_ref, dh_hbm, dxin_hbm, dx_ref, dxb_ref, dg_ref, xb, hb, ib, sem):
        s = pl.program_id(0)

        def copies(step, slot):
            rows = pl.ds(pl.multiple_of(step * tt, tt), tt)
            return [pltpu.make_async_copy(x_hbm.at[rows], xb.at[slot], sem.at[0, slot]),
                    pltpu.make_async_copy(dh_hbm.at[rows], hb.at[slot], sem.at[1, slot]),
                    pltpu.make_async_copy(dxin_hbm.at[rows], ib.at[slot], sem.at[2, slot])]

        @pl.when(s == 0)
        def _():
            for st in range(min(RING - 1, n)):
                for cp in copies(st, st):
                    cp.start()

        @pl.when(s + RING - 1 < n)
        def _():
            for cp in copies(s + RING - 1, (s + RING - 1) % RING):
                cp.start()

        slot = s % RING
        for cp in copies(s, slot):
            cp.wait()
        xv = xb[slot]
        r = lax.rsqrt(jnp.mean(xv * xv, -1, keepdims=True) + EPS)
        xh = xv * r
        dy = hb[slot].astype(F32)
        dxh = dy * g_ref[...]
        dx = r * (dxh - xh * jnp.mean(dxh * xh, -1, keepdims=True)) + ib[slot]
        dx_ref[...] = dx
        dxb_ref[...] = dx.astype(BF16)
        _acc_store(s == 0, dg_ref, (slice(None), slice(None)), jnp.sum(dy * xh, 0, keepdims=True))

    row = pl.BlockSpec((tt, D), lambda i: (i, 0))
    vec = pl.BlockSpec((1, D), lambda i: (0, 0))
    hbm = pl.BlockSpec(memory_space=pl.ANY)
    return pl.pallas_call(
        body, grid=(n,), in_specs=[hbm, vec, hbm, hbm], out_specs=[row, row, vec],
        out_shape=[S((T, D), F32), S((T, D), BF16), S((1, D), F32)],
        scratch_shapes=[pltpu.VMEM((RING, tt, D), F32), pltpu.VMEM((RING, tt, D), BF16), pltpu.VMEM((RING, tt, D), F32),
                        pltpu.SemaphoreType.DMA((3, RING))],
        compiler_params=_params("arbitrary"), name=name)(x, gain, dh, dx_in)


def _rms_bwd(name, x, gain, dh, dx_in=None):
    T = x.shape[0]
    tt = _tile(T, 512)
    has_in = dx_in is not None
    if has_in:
        return _rms_bwd_ring(name, x, gain, dh, dx_in, _tile(T, 256))

    def body(x_ref, g_ref, dh_ref, *rest):
        rest = list(rest)
        dxin_ref = rest.pop(0) if has_in else None
        dx_ref, dxb_ref, dg_ref = rest
        xv = x_ref[...]
        r = lax.rsqrt(jnp.mean(xv * xv, -1, keepdims=True) + EPS)
        xh = xv * r
        dy = dh_ref[...].astype(F32)
        dxh = dy * g_ref[...]
        dx = r * (dxh - xh * jnp.mean(dxh * xh, -1, keepdims=True))
        if has_in:
            dx = dx + dxin_ref[...]
        dx_ref[...] = dx
        dxb_ref[...] = dx.astype(BF16)
        part = jnp.sum(dy * xh, 0, keepdims=True)

        @pl.when(pl.program_id(0) == 0)
        def _():
            dg_ref[...] = part

        @pl.when(pl.program_id(0) > 0)
        def _():
            dg_ref[...] += part

    row = pl.BlockSpec((tt, D), lambda i: (i, 0))
    vec = pl.BlockSpec((1, D), lambda i: (0, 0))
    args = [x, gain, dh] + ([dx_in] if has_in else [])
    return pl.pallas_call(
        body, grid=(T // tt,), in_specs=[row, vec, row] + ([row] if has_in else []),
        out_specs=[row, row, vec], out_shape=[S((T, D), F32), S((T, D), BF16), S((1, D), F32)],
        compiler_params=_params("arbitrary"), name=name)(*args)


def _sigmoid(x):
    return 0.5 * jnp.tanh(0.5 * x) + 0.5


def _ffn_up(name, h, w4, layer, tm_t=512, tn_t=1408):
    T = h.shape[0]
    n_sh = w4.shape[3]
    tm, tn = _tile(T, tm_t), _tile(n_sh, tn_t)
    per = n_sh // tn

    def body(h_ref, wg_ref, wu_ref, g_ref, u_ref, a_ref):
        hv = h_ref[...]
        g = jnp.dot(hv, wg_ref[...], preferred_element_type=F32)
        u = jnp.dot(hv, wu_ref[...], preferred_element_type=F32)
        g_ref[...] = g.astype(BF16)
        u_ref[...] = u.astype(BF16)
        a_ref[...] = (g * _sigmoid(g) * u).astype(BF16)

    o = pl.BlockSpec((tm, tn), lambda j, i: (i, j))
    return pl.pallas_call(
        body, grid=(DFF // tn, T // tm),
        in_specs=[pl.BlockSpec((tm, D), lambda j, i: (i, 0)),
                  pl.BlockSpec((None, None, D, tn), lambda j, i: (j // per, layer, 0, j % per)),
                  pl.BlockSpec((None, None, D, tn), lambda j, i: (2 + j // per, layer, 0, j % per))],
        out_specs=[o, o, o], out_shape=[S((T, DFF), BF16)] * 3,
        compiler_params=_params("parallel", "parallel"), name=name)(h, w4, w4)


def _ffn_dact(name, dxb, wd4, layer, g, u, tm_t=512):
    T = dxb.shape[0]
    r_sh = wd4.shape[2]
    tm, tn = _tile(T, tm_t), _tile(r_sh, 1408)
    per = r_sh // tn

    def body(dx_ref, w_ref, g_ref, u_ref, o_ref):
        da = 0.5 * lax.dot_general(dx_ref[...], w_ref[...], _DIMS["nt"], preferred_element_type=F32)
        gv, uv = g_ref[...].astype(F32), u_ref[...].astype(F32)
        sg = _sigmoid(gv)
        o_ref[0, 0] = (da * uv * sg * (1.0 + gv * (1.0 - sg))).astype(BF16)
        o_ref[1, 0] = (da * gv * sg).astype(BF16)

    t = pl.BlockSpec((tm, tn), lambda j, i: (i, j))
    return pl.pallas_call(
        body, grid=(DFF // tn, T // tm),
        in_specs=[pl.BlockSpec((tm, D), lambda j, i: (i, 0)),
                  pl.BlockSpec((None, None, tn, D), lambda j, i: (j // per, layer, j % per, 0)), t, t],
        out_specs=pl.BlockSpec((2, 1, tm, tn), lambda j, i: (0, 0, i, j)), out_shape=S((2, 1, T, DFF), BF16),
        compiler_params=_params("parallel", "parallel"), name=name)(dxb, wd4, g, u)


def _gelu(x):
    return 0.5 * x * (1.0 + lax.erf(x * 0.7071067811865476))


def _causal(n):
    return lax.broadcasted_iota(jnp.int32, (n, n), 0) >= lax.broadcasted_iota(jnp.int32, (n, n), 1)


def _gmlp_math(u_raw, v_raw, lng, lnb, ws, bs):
    causal = _causal(CH)
    outs = []
    for g in range(GM_G):
        u, v = _gelu(u_raw[g]), _gelu(v_raw[g])
        mu = jnp.mean(v, -1, keepdims=True)
        var = jnp.mean(jnp.square(v - mu), -1, keepdims=True)
        vn = (v - mu) * lax.rsqrt(var + EPS) * lng[g] + lnb[g]
        wm = jnp.where(causal, ws[g], 0.0)
        s = jnp.dot(wm.astype(BF16), vn.astype(BF16), preferred_element_type=F32) + bs[g]
        outs.append(u * s)
    return outs


def _gmlp_load(proj_ref, lng_ref, lnb_ref, ws_ref, bs_ref):
    sl = lambda g, off: slice(off + g * GM_GD, off + (g + 1) * GM_GD)
    u_raw = [proj_ref[:, sl(g, 0)].astype(F32) for g in range(GM_G)]
    v_raw = [proj_ref[:, sl(g, D)].astype(F32) for g in range(GM_G)]
    lng = [lng_ref[:, sl(g, 0)] for g in range(GM_G)]
    lnb = [lnb_ref[:, sl(g, 0)] for g in range(GM_G)]
    ws = [ws_ref[g] for g in range(GM_G)]
    bs = [bs_ref[g] for g in range(GM_G)]
    return u_raw, v_raw, lng, lnb, ws, bs


_GM_PAR = lambda: [pl.BlockSpec((1, D), lambda i: (0, 0)), pl.BlockSpec((1, D), lambda i: (0, 0)),
                   pl.BlockSpec((GM_G, CH, CH), lambda i: (0, 0, 0)), pl.BlockSpec((GM_G, CH, 1), lambda i: (0, 0, 0))]


def _gmlp_fwd(name, proj, lng, lnb, ws, bs):
    T = proj.shape[0]

    def body(proj_ref, lng_ref, lnb_ref, ws_ref, bs_ref, o_ref):
        outs = _gmlp_math(*_gmlp_load(proj_ref, lng_ref, lnb_ref, ws_ref, bs_ref))
        for g in range(GM_G):
            o_ref[:, g * GM_GD:(g + 1) * GM_GD] = outs[g].astype(BF16)

    return pl.pallas_call(
        body, grid=(T // CH,), in_specs=[pl.BlockSpec((CH, 2 * D), lambda i: (i, 0))] + _GM_PAR(),
        out_specs=pl.BlockSpec((CH, D), lambda i: (i, 0)), out_shape=S((T, 2 * D), BF16),
        compiler_params=_params("parallel"), name=name)(proj, lng, lnb, ws, bs)


def _acc_store(first, ref, idx, val):
    @pl.when(first)
    def _():
        ref[idx] = val

    @pl.when(jnp.logical_not(first))
    def _():
        ref[idx] += val


def _gmlp_bwd(name, proj, lng, lnb, ws, bs, dmix, dproj):
    T = proj.shape[0]

    def body(proj_ref, lng_ref, lnb_ref, ws_ref, bs_ref, dmix_ref, _, dproj_ref, dlng_ref, dlnb_ref, dws_ref, dbs_ref):
        first = pl.program_id(0) == 0
        prim = _gmlp_load(proj_ref, lng_ref, lnb_ref, ws_ref, bs_ref)
        _, vjp = jax.vjp(_gmlp_math, *prim)
        du, dv, dlng, dlnb, dws, dbs = vjp([dmix_ref[:, g * GM_GD:(g + 1) * GM_GD].astype(F32) for g in range(GM_G)])
        for g in range(GM_G):
            sl = slice(g * GM_GD, (g + 1) * GM_GD)
            dproj_ref[:, sl] = du[g].astype(BF16)
            dproj_ref[:, D + g * GM_GD:D + (g + 1) * GM_GD] = dv[g].astype(BF16)
            _acc_store(first, dlng_ref, (slice(None), sl), dlng[g])
            _acc_store(first, dlnb_ref, (slice(None), sl), dlnb[g])
            _acc_store(first, dws_ref, g, dws[g])
            _acc_store(first, dbs_ref, g, dbs[g])

    par = _GM_PAR()
    return pl.pallas_call(
        body, grid=(T // CH,),
        in_specs=[pl.BlockSpec((CH, 2 * D), lambda i: (i, 0))] + par +
                 [pl.BlockSpec((CH, D), lambda i: (i, 0)), pl.BlockSpec(memory_space=pl.ANY)],
        out_specs=[pl.BlockSpec((CH, 2 * D), lambda i: (i, 0))] + par,
        out_shape=[S(dproj.shape, BF16), S((1, D), F32), S((1, D), F32), S((GM_G, CH, CH), F32), S((GM_G, CH, 1), F32)],
        input_output_aliases={6: 0}, compiler_params=_params("arbitrary"), name=name)(proj, lng, lnb, ws, bs, dmix, dproj)


CONV_TT = 256
HALO = 8


def _shift_rows(cur, halo_after, s):
    if s == 0:
        return cur
    n = cur.shape[0]
    return pltpu.roll(jnp.concatenate([cur, halo_after], 0), s, 0)[:n]


def _conv_fwd(name, proj, w, b):
    T = proj.shape[0]
    tt = _tile(T, CONV_TT)
    hb = tt // HALO

    def body(x_ref, halo_ref, w_ref, b_ref, y_ref, xc_ref):
        i = pl.program_id(0)
        x = x_ref[...].astype(F32)
        halo = halo_ref[...].astype(F32) * (i > 0).astype(F32)
        y = b_ref[...] + w_ref[3:4, :] * x
        for s in (1, 2, 3):
            y = y + w_ref[3 - s:4 - s, :] * _shift_rows(x, halo, s)
        y_ref[...] = y.astype(BF16)
        xc_ref[...] = (y * _sigmoid(y)).astype(BF16)

    o = pl.BlockSpec((tt, CONV_C), lambda i: (i, 0))
    return pl.pallas_call(
        body, grid=(T // tt,),
        in_specs=[pl.BlockSpec((tt, CONV_C), lambda i: (i, 2)),
                  pl.BlockSpec((HALO, CONV_C), lambda i: (jnp.maximum(i * hb - 1, 0), 2)),
                  pl.BlockSpec((4, CONV_C), lambda i: (0, 0)), pl.BlockSpec((1, CONV_C), lambda i: (0, 0))],
        out_specs=[o, o], out_shape=[S((T, CONV_C), BF16)] * 2,
        compiler_params=_params("parallel"), name=name)(proj, proj, w, b)


def _conv_bwd(name, proj, ypre, dxc, w, dproj):
    T = proj.shape[0]
    tt = _tile(T, CONV_TT)
    hb = tt // HALO
    nt = T // tt

    def dsilu(y):
        sg = _sigmoid(y)
        return sg * (1.0 + y * (1.0 - sg))

    def body(x_ref, xh_ref, y_ref, yn_ref, d_ref, dn_ref, w_ref, _, dproj_ref, dw_ref, db_ref):
        i = pl.program_id(0)
        first = i == 0
        x = x_ref[...].astype(F32)
        halo = xh_ref[...].astype(F32) * (i > 0).astype(F32)
        dy = d_ref[...].astype(F32) * dsilu(y_ref[...].astype(F32))
        dyn = dn_ref[...].astype(F32) * dsilu(yn_ref[...].astype(F32)) * (i < nt - 1).astype(F32)
        ext = jnp.concatenate([dy, dyn], 0)
        dx = w_ref[3:4, :] * dy
        _acc_store(first, dw_ref, (slice(3, 4), slice(None)), jnp.sum(x * dy, 0, keepdims=True))
        for s in (1, 2, 3):
            dx = dx + w_ref[3 - s:4 - s, :] * pltpu.roll(ext, tt + HALO - s, 0)[:tt]
            _acc_store(first, dw_ref, (slice(3 - s, 4 - s), slice(None)),
                       jnp.sum(_shift_rows(x, halo, s) * dy, 0, keepdims=True))
        _acc_store(first, db_ref, (slice(None), slice(None)), jnp.sum(dy, 0, keepdims=True))
        dproj_ref[...] = dx.astype(BF16)

    cur = pl.BlockSpec((tt, CONV_C), lambda i: (i, 0))
    nxt = pl.BlockSpec((HALO, CONV_C), lambda i: (jnp.minimum((i + 1) * hb, T // HALO - 1), 0))
    return pl.pallas_call(
        body, grid=(nt,),
        in_specs=[pl.BlockSpec((tt, CONV_C), lambda i: (i, 2)),
                  pl.BlockSpec((HALO, CONV_C), lambda i: (jnp.maximum(i * hb - 1, 0), 2)),
                  cur, nxt, cur, nxt, pl.BlockSpec((4, CONV_C), lambda i: (0, 0)), pl.BlockSpec(memory_space=pl.ANY)],
        out_specs=[pl.BlockSpec((tt, CONV_C), lambda i: (i, 2)), pl.BlockSpec((4, CONV_C), lambda i: (0, 0)),
                   pl.BlockSpec((1, CONV_C), lambda i: (0, 0))],
        out_shape=[S(dproj.shape, BF16), S((4, CONV_C), F32), S((1, CONV_C), F32)],
        input_output_aliases={7: 0}, compiler_params=_params("arbitrary"), name=name)(proj, proj, ypre, ypre, dxc, dxc, w, dproj)


def _softplus(x):
    return jnp.maximum(x, 0.0) + jnp.log(1.0 + jnp.exp(-jnp.abs(x)))


def _ssd_math(x, Bm, Cm, dtr, z, prev, dtb, alog, dsk, nrm):
    hi = lax.Precision.HIGHEST
    causal = _causal(CH)
    tri = causal.astype(F32)
    lane = lax.broadcasted_iota(jnp.int32, (1, LANE), 1)
    sub = lax.broadcasted_iota(jnp.int32, (LANE, 1), 0)
    dt = _softplus(dtr + dtb)
    a = dt * (-jnp.exp(alog))
    a_cs = jnp.dot(tri, a, preferred_element_type=F32, precision=hi)
    a_csT = lax.dot_general(a, tri, (((0,), (1,)), ((), ())), preferred_element_type=F32, precision=hi)
    a_last = jnp.sum(a, 0, keepdims=True)
    gw = HPG * HD
    outs, new = [], []
    for g in range(NG):
        spread = (lax.broadcasted_iota(jnp.int32, (LANE, gw), 0)
                  == g * HPG + lax.broadcasted_iota(jnp.int32, (LANE, gw), 1) // HD).astype(F32)
        to_lanes = lambda v: jnp.dot(v, spread, preferred_element_type=F32, precision=hi)
        col_e, dt_e, last_e, dsk_e = to_lanes(a_cs), to_lanes(dt), to_lanes(a_last), to_lanes(dsk)
        last_r = lax.dot_general(spread, a_last, (((0,), (1,)), ((), ())), preferred_element_type=F32, precision=hi)
        cb = lax.dot_general(Cm[g].astype(BF16), Bm[g].astype(BF16), _DIMS["nt"], preferred_element_type=F32)
        xg = jnp.concatenate(x[g * HPG:(g + 1) * HPG], 1)
        yd = []
        for h in range(g * HPG, (g + 1) * HPG):
            ohl = (lane == h).astype(F32)
            col = jnp.sum(a_cs * ohl, 1, keepdims=True)
            row = jnp.sum(a_csT * (sub == h).astype(F32), 0, keepdims=True)
            dtc = jnp.sum(dt * ohl, 1, keepdims=True)
            lmat = jnp.where(causal, jnp.exp(jnp.where(causal, col - row, 0.0)), 0.0)
            yd.append(jnp.dot((cb * lmat).astype(BF16), (x[h] * dtc).astype(BF16), preferred_element_type=F32))
        y = jnp.concatenate(yd, 1)
        y = y + jnp.exp(col_e) * lax.dot_general(Cm[g].astype(BF16), prev[g].astype(BF16), _DIMS["nt"],
                                                 preferred_element_type=F32)
        st = lax.dot_general((xg * dt_e * jnp.exp(last_e - col_e)).astype(BF16), Bm[g].astype(BF16), _DIMS["tn"],
                             preferred_element_type=F32)
        new.append(prev[g] * jnp.exp(last_r) + st)
        yg = (y + xg * dsk_e) * (z[g] * _sigmoid(z[g]))
        yg = yg * lax.rsqrt(jnp.mean(yg * yg, -1, keepdims=True) + EPS)
        outs.append(yg * nrm[g])
    return outs, new


def _ssd_load(xc_ref, dtr_ref, z_ref, state_ref, dtb_ref, alog_ref, dsk_ref, nrm_ref):
    gw = HPG * HD
    x = [xc_ref[:, h * HD:(h + 1) * HD].astype(F32) for h in range(NH)]
    Bm = [xc_ref[:, D + g * NS:D + (g + 1) * NS].astype(F32) for g in range(NG)]
    Cm = [xc_ref[:, D + NG * NS + g * NS:D + NG * NS + (g + 1) * NS].astype(F32) for g in range(NG)]
    z = [z_ref[:, g * gw:(g + 1) * gw].astype(F32) for g in range(NG)]
    prev = [state_ref[g * gw:(g + 1) * gw, :] for g in range(NG)]
    nrm = [nrm_ref[:, g * gw:(g + 1) * gw] for g in range(NG)]
    return x, Bm, Cm, dtr_ref[...], z, prev, dtb_ref[...], alog_ref[...], dsk_ref[...], nrm


_SSD_PAR = lambda: [pl.BlockSpec((1, LANE), lambda c: (0, 0))] * 3 + [pl.BlockSpec((1, D), lambda c: (0, 0))]


def _ssd_fwd(name, xc, dtr, proj, dtb, alog, dsk, nrm, mix):
    T = xc.shape[0]
    nc = T // CH

    def body(xc_ref, dtr_ref, z_ref, dtb_ref, alog_ref, dsk_ref, nrm_ref, _, mix_ref, prev_ref, state):
        @pl.when(pl.program_id(0) == 0)
        def _():
            state[...] = jnp.zeros_like(state)

        prev_ref[...] = state[...]
        outs, new = _ssd_math(*_ssd_load(xc_ref, dtr_ref, z_ref, state, dtb_ref, alog_ref, dsk_ref, nrm_ref))
        for g in range(NG):
            mix_ref[:, g * 512:(g + 1) * 512] = outs[g].astype(BF16)
            state[g * 512:(g + 1) * 512, :] = new[g]

    return pl.pallas_call(
        body, grid=(nc,),
        in_specs=[pl.BlockSpec((CH, CONV_C), lambda c: (c, 0)), pl.BlockSpec((CH, LANE), lambda c: (c, 0)),
                  pl.BlockSpec((CH, D), lambda c: (c, 2))] + _SSD_PAR() + [pl.BlockSpec(memory_space=pl.ANY)],
        out_specs=[pl.BlockSpec((CH, D), lambda c: (c, 1)), pl.BlockSpec((None, NH * HD, NS), lambda c: (c, 0, 0))],
        out_shape=[S(mix.shape, BF16), S((nc, NH * HD, NS), F32)],
        scratch_shapes=[pltpu.VMEM((NH * HD, NS), F32)], input_output_aliases={7: 0},
        compiler_params=_params("arbitrary"), name=name)(xc, dtr, proj, dtb, alog, dsk, nrm, mix)


def _ssd_bwd(name, xc, dtr, proj, prevs, dtb, alog, dsk, nrm, dmix, dproj):
    T = xc.shape[0]
    nc = T // CH
    rev = lambda c: nc - 1 - c

    def body(xc_ref, dtr_ref, z_ref, prev_ref, dtb_ref, alog_ref, dsk_ref, nrm_ref, dmix_ref, _,
             dproj_ref, dxc_ref, ddtr_ref, ddtb_ref, dalog_ref, ddsk_ref, dnrm_ref, dstate):
        first = pl.program_id(0) == 0

        @pl.when(first)
        def _():
            dstate[...] = jnp.zeros_like(dstate)

        prim = _ssd_load(xc_ref, dtr_ref, z_ref, prev_ref, dtb_ref, alog_ref, dsk_ref, nrm_ref)
        _, vjp = jax.vjp(_ssd_math, *prim)
        douts = [dmix_ref[:, g * 512:(g + 1) * 512].astype(F32) for g in range(NG)]
        dnew = [dstate[g * 512:(g + 1) * 512, :] for g in range(NG)]
        dx, dB, dC, ddtr, dz, dprev, ddtb, dalog, ddsk, dnrm = vjp((douts, dnew))
        for h in range(NH):
            dxc_ref[:, h * HD:(h + 1) * HD] = dx[h].astype(BF16)
        for g in range(NG):
            dstate[g * 512:(g + 1) * 512, :] = dprev[g]
            dxc_ref[:, D + g * NS:D + (g + 1) * NS] = dB[g].astype(BF16)
            dxc_ref[:, D + NG * NS + g * NS:D + NG * NS + (g + 1) * NS] = dC[g].astype(BF16)
            dproj_ref[:, g * 512:(g + 1) * 512] = dz[g].astype(BF16)
            _acc_store(first, dnrm_ref, (slice(None), slice(g * 512, (g + 1) * 512)), dnrm[g])
        ddtr_ref[...] = ddtr
        _acc_store(first, ddtb_ref, (slice(None), slice(None)), ddtb)
        _acc_store(first, dalog_ref, (slice(None), slice(None)), dalog)
        _acc_store(first, ddsk_ref, (slice(None), slice(None)), ddsk)

    vec = pl.BlockSpec((1, LANE), lambda c: (0, 0))
    return pl.pallas_call(
        body, grid=(nc,),
        in_specs=[pl.BlockSpec((CH, CONV_C), lambda c: (rev(c), 0)), pl.BlockSpec((CH, LANE), lambda c: (rev(c), 0)),
                  pl.BlockSpec((CH, D), lambda c: (rev(c), 2)),
                  pl.BlockSpec((None, NH * HD, NS), lambda c: (rev(c), 0, 0))] + _SSD_PAR() +
                 [pl.BlockSpec((CH, D), lambda c: (rev(c), 1)), pl.BlockSpec(memory_space=pl.ANY)],
        out_specs=[pl.BlockSpec((CH, D), lambda c: (rev(c), 2)), pl.BlockSpec((CH, CONV_C), lambda c: (rev(c), 0)),
                   pl.BlockSpec((CH, LANE), lambda c: (rev(c), 0)), vec, vec, vec, pl.BlockSpec((1, D), lambda c: (0, 0))],
        out_shape=[S(dproj.shape, BF16), S((T, CONV_C), BF16), S((T, LANE), F32), S((1, LANE), F32), S((1, LANE), F32),
                   S((1, LANE), F32), S((1, D), F32)],
        scratch_shapes=[pltpu.VMEM((NH * HD, NS), F32)], input_output_aliases={9: 0},
        compiler_params=_params("arbitrary"), name=name)(xc, dtr, proj, prevs, dtb, alog, dsk, nrm, dmix, dproj)


def _rope(x, c, s, sign):
    W = x.shape[1]
    reps = W // LANE
    C, Sg = jnp.tile(c, (1, reps)), jnp.tile(s, (1, reps))
    lane = lax.broadcasted_iota(jnp.int32, x.shape, 1) % AHD
    up, dn = pltpu.roll(x, W - ROT // 2, 1), pltpu.roll(x, ROT // 2, 1)
    sw = jnp.where(lane < ROT // 2, up, jnp.where(lane < ROT, dn, 0.0))
    return x * C + sign * sw * Sg


def _rope_fwd(name, qkv, cos, sin):
    T = qkv.shape[0]
    tt = _tile(T, 256)
    KV = AKV * AHD

    def body(x_ref, c_ref, s_ref, o_ref):
        c, s = c_ref[...], s_ref[...]
        o_ref[:, :D] = _rope(x_ref[:, :D], c, s, 1.0).astype(BF16)
        o_ref[:, D:D + KV] = _rope(x_ref[:, D:D + KV], c, s, 1.0).astype(BF16)
        o_ref[:, D + KV:] = x_ref[:, D + KV:].astype(BF16)

    tab = pl.BlockSpec((tt, LANE), lambda i: (i, 0))
    return pl.pallas_call(
        body, grid=(T // tt,), in_specs=[pl.BlockSpec((tt, ODD_IN), lambda i: (i, 0)), tab, tab],
        out_specs=pl.BlockSpec((tt, ODD_IN), lambda i: (i, 0)), out_shape=S((T, ODD_IN), BF16),
        compiler_params=_params("parallel"), name=name)(qkv, cos, sin)


def _rope_bwd(name, dq, dkv_cur, dkv_prev, cos, sin):
    T = dq.shape[0]
    nb = T // CH
    KV = AKV * AHD

    def body(dq_ref, cur_ref, nxt_ref, c_ref, s_ref, o_ref, db_ref):
        n = pl.program_id(0)
        c, s = c_ref[...], s_ref[...]
        dkv = cur_ref[...] + nxt_ref[...] * (n < nb - 1).astype(F32)
        o_ref[:, :D] = _rope(dq_ref[...].astype(F32), c, s, -1.0).astype(BF16)
        o_ref[:, D:D + KV] = _rope(dkv[:, :KV], c, s, -1.0).astype(BF16)
        o_ref[:, D + KV:] = dkv[:, KV:].astype(BF16)
        _acc_store(n == 0, db_ref, (slice(None), slice(None)), jnp.sum(o_ref[...].astype(F32), 0, keepdims=True))

    tab = pl.BlockSpec((CH, LANE), lambda n: (n, 0))
    return pl.pallas_call(
        body, grid=(nb,),
        in_specs=[pl.BlockSpec((CH, D), lambda n: (n, 0)), pl.BlockSpec((CH, 2 * KV), lambda n: (n, 0)),
                  pl.BlockSpec((CH, 2 * KV), lambda n: (jnp.minimum(n + 1, nb - 1), 0)), tab, tab],
        out_specs=[pl.BlockSpec((CH, ODD_IN), lambda n: (n, 0)), pl.BlockSpec((1, ODD_IN), lambda n: (0, 0))],
        out_shape=[S((T, ODD_IN), BF16), S((1, ODD_IN), F32)],
        compiler_params=_params("arbitrary"), name=name)(dq, dkv_cur, dkv_prev, cos, sin)


def _swa_math(q, kp, kc, vp, vc, snk, mask):
    outs = []
    for k in range(AKV):
        K = jnp.concatenate([kp[k], kc[k]], 0).astype(BF16)
        V = jnp.concatenate([vp[k], vc[k]], 0).astype(BF16)
        s = lax.dot_general(q[k].astype(BF16), K, _DIMS["nt"], preferred_element_type=F32) * ATT_SCALE
        s = jnp.where(mask, s, -jnp.inf)
        m = lax.stop_gradient(jnp.maximum(jnp.max(s, -1, keepdims=True), snk[k]))
        p = jnp.exp(s - m)
        pr = p / (jnp.sum(p, -1, keepdims=True) + jnp.exp(snk[k] - m))
        outs.append(jnp.dot(pr.astype(BF16), V, preferred_element_type=F32))
    return outs


def _stack_heads(ref, k):
    return jnp.concatenate([ref[:, (k * AREP + r) * AHD:(k * AREP + r + 1) * AHD].astype(F32) for r in range(AREP)], 0)


def _swa_load(q_ref, cur_ref, prv_ref, snk_ref):
    KV = AKV * AHD
    q = [_stack_heads(q_ref, k) for k in range(AKV)]
    kc = [cur_ref[:, k * AHD:(k + 1) * AHD].astype(F32) for k in range(AKV)]
    vc = [cur_ref[:, KV + k * AHD:KV + (k + 1) * AHD].astype(F32) for k in range(AKV)]
    kp = [prv_ref[:, k * AHD:(k + 1) * AHD].astype(F32) for k in range(AKV)]
    vp = [prv_ref[:, KV + k * AHD:KV + (k + 1) * AHD].astype(F32) for k in range(AKV)]
    snk = [jnp.concatenate([jnp.broadcast_to(snk_ref[:, k * AREP + r:k * AREP + r + 1], (CH, 1)) for r in range(AREP)], 0)
           for k in range(AKV)]
    return q, kp, kc, vp, vc, snk


def _swa_mask(n):
    iq = lax.broadcasted_iota(jnp.int32, (AREP * CH, 2 * CH), 0) % CH
    js = lax.broadcasted_iota(jnp.int32, (AREP * CH, 2 * CH), 1)
    rel = iq + CH - js
    return (rel >= 0) & (rel < CH) & ((n > 0) | (js >= CH))


def _swa_specs(T):
    KV = AKV * AHD
    return [pl.BlockSpec((CH, D), lambda n: (n, 0)), pl.BlockSpec((CH, 2 * KV), lambda n: (n, D // (2 * KV))),
            pl.BlockSpec((CH, 2 * KV), lambda n: (jnp.maximum(n - 1, 0), D // (2 * KV))),
            pl.BlockSpec((1, LANE), lambda n: (0, 0))]


def _swa_fwd(name, qkvr, snk):
    T = qkvr.shape[0]

    def body(q_ref, cur_ref, prv_ref, snk_ref, o_ref):
        outs = _swa_math(*_swa_load(q_ref, cur_ref, prv_ref, snk_ref), _swa_mask(pl.program_id(0)))
        for h in range(AH):
            k, r = divmod(h, AREP)
            o_ref[:, h * AHD:(h + 1) * AHD] = outs[k][r * CH:(r + 1) * CH].astype(BF16)

    return pl.pallas_call(
        body, grid=(T // CH,), in_specs=_swa_specs(T), out_specs=pl.BlockSpec((CH, D), lambda n: (n, 0)),
        out_shape=S((T, D), BF16), compiler_params=_params("parallel"), name=name)(qkvr, qkvr, qkvr, snk)


def _swa_bwd(name, qkvr, snk, do):
    T = qkvr.shape[0]
    KV = AKV * AHD

    def body(q_ref, cur_ref, prv_ref, snk_ref, do_ref, dq_ref, dcur_ref, dprv_ref, dsnk_ref):
        n = pl.program_id(0)

        @pl.when(n == 0)
        def _():
            dsnk_ref[...] = jnp.zeros_like(dsnk_ref)

        prim = _swa_load(q_ref, cur_ref, prv_ref, snk_ref)
        mask = _swa_mask(n)
        _, vjp = jax.vjp(lambda *p: _swa_math(*p, mask), *prim)
        dq, dkp, dkc, dvp, dvc, dsnk = vjp([_stack_heads(do_ref, k) for k in range(AKV)])
        for h in range(AH):
            k, r = divmod(h, AREP)
            dq_ref[:, h * AHD:(h + 1) * AHD] = dq[k][r * CH:(r + 1) * CH].astype(BF16)
            dsnk_ref[:, h:h + 1] += jnp.sum(dsnk[k][r * CH:(r + 1) * CH], 0, keepdims=True)
        for k in range(AKV):
            dcur_ref[:, k * AHD:(k + 1) * AHD] = dkc[k]
            dcur_ref[:, KV + k * AHD:KV + (k + 1) * AHD] = dvc[k]
            dprv_ref[:, k * AHD:(k + 1) * AHD] = dkp[k]
            dprv_ref[:, KV + k * AHD:KV + (k + 1) * AHD] = dvp[k]

    kv = pl.BlockSpec((CH, 2 * KV), lambda n: (n, 0))
    return pl.pallas_call(
        body, grid=(T // CH,), in_specs=_swa_specs(T) + [pl.BlockSpec((CH, D), lambda n: (n, 0))],
        out_specs=[pl.BlockSpec((CH, D), lambda n: (n, 0)), kv, kv, pl.BlockSpec((1, LANE), lambda n: (0, 0))],
        out_shape=[S((T, D), BF16), S((T, 2 * KV), F32), S((T, 2 * KV), F32), S((1, LANE), F32)],
        compiler_params=_params("arbitrary"), name=name)(qkvr, qkvr, qkvr, snk, do)


def _xat_math(q, k, v):
    outs = []
    for h in range(XH):
        s = lax.dot_general(q[h].astype(BF16), k[h].astype(BF16), _DIMS["nt"], preferred_element_type=F32) * X_SCALE
        m = lax.stop_gradient(jnp.max(s, -1, keepdims=True))
        p = jnp.exp(s - m)
        pr = p / jnp.sum(p, -1, keepdims=True)
        outs.append(jnp.dot(pr.astype(BF16), v[h].astype(BF16), preferred_element_type=F32))
    return outs


def _xat_load(q_ref, kv_ref):
    q = [q_ref[:, h * XHD:(h + 1) * XHD].astype(F32) for h in range(XH)]
    k = [kv_ref[:, h * XHD:(h + 1) * XHD].astype(F32) for h in range(XH)]
    v = [kv_ref[:, XW + h * XHD:XW + (h + 1) * XHD].astype(F32) for h in range(XH)]
    return q, k, v


def _xat_fwd(name, q, kv):
    T, M = q.shape[0], kv.shape[0]
    tt = _tile(T, 512)

    def body(q_ref, kv_ref, o_ref):
        outs = _xat_math(*_xat_load(q_ref, kv_ref))
        for h in range(XH):
            o_ref[:, h * XHD:(h + 1) * XHD] = outs[h].astype(BF16)

    return pl.pallas_call(
        body, grid=(T // tt,),
        in_specs=[pl.BlockSpec((tt, XW), lambda i: (i, 0)), pl.BlockSpec((M, 2 * XW), lambda i: (0, 0))],
        out_specs=pl.BlockSpec((tt, XW), lambda i: (i, 0)), out_shape=S((T, XW), BF16),
        compiler_params=_params("parallel"), name=name)(q, kv)


def _xat_bwd(name, q, kv, do):
    T, M = q.shape[0], kv.shape[0]
    tt = _tile(T, 512)

    def body(q_ref, kv_ref, do_ref, dq_ref, dkv_ref):
        first = pl.program_id(0) == 0
        _, vjp = jax.vjp(_xat_math, *_xat_load(q_ref, kv_ref))
        dq, dk, dv = vjp([do_ref[:, h * XHD:(h + 1) * XHD].astype(F32) for h in range(XH)])
        for h in range(XH):
            sl = slice(h * XHD, (h + 1) * XHD)
            dq_ref[:, sl] = dq[h].astype(BF16)
            _acc_store(first, dkv_ref, (slice(None), sl), dk[h])
            _acc_store(first, dkv_ref, (slice(None), slice(XW + h * XHD, XW + (h + 1) * XHD)), dv[h])

    qs = pl.BlockSpec((tt, XW), lambda i: (i, 0))
    kvs = pl.BlockSpec((M, 2 * XW), lambda i: (0, 0))
    return pl.pallas_call(
        body, grid=(T // tt,), in_specs=[qs, kvs, qs], out_specs=[qs, kvs],
        out_shape=[S((T, XW), BF16), S((M, 2 * XW), F32)],
        compiler_params=_params("arbitrary"), name=name)(q, kv, do)


def _loss_head(name, x, gain, target):
    T = x.shape[0]
    tt = _tile(T, 512)

    def body(x_ref, g_ref, t_ref, l_ref, dx_ref, dxb_ref, dg_ref):
        first = pl.program_id(0) == 0
        xv, g = x_ref[...], g_ref[...]
        r = lax.rsqrt(jnp.mean(xv * xv, -1, keepdims=True) + EPS)
        xh = xv * r
        e = xh * g - t_ref[...]
        part = 0.5 * jnp.sum(jnp.mean(e * e, -1, keepdims=True), (0, 1), keepdims=True)
        _acc_store(first, l_ref, (slice(None), slice(None)), jnp.broadcast_to(part, (1, LANE)))
        dy = e * (1.0 / D)
        dxh = dy * g
        dx = r * (dxh - xh * jnp.mean(dxh * xh, -1, keepdims=True))
        dx_ref[...] = dx
        dxb_ref[...] = dx.astype(BF16)
        _acc_store(first, dg_ref, (slice(None), slice(None)), jnp.sum(dy * xh, 0, keepdims=True))

    row = pl.BlockSpec((tt, D), lambda i: (i, 0))
    vec = pl.BlockSpec((1, D), lambda i: (0, 0))
    return pl.pallas_call(
        body, grid=(T // tt,), in_specs=[row, vec, row],
        out_specs=[pl.BlockSpec((1, LANE), lambda i: (0, 0)), row, row, vec],
        out_shape=[S((1, LANE), F32), S((T, D), F32), S((T, D), BF16), S((1, D), F32)],
        compiler_params=_params("arbitrary"), name=name)(x, gain, target)


def _out_proj(name, a, b, x, next_gain, scale=1.0, tk_t=2048, plain=(1024, 1024)):
    if next_gain is None:
        return _mm(name, "nn", a, b, F32, res=x, scale=scale, tm_t=plain[0], tn_t=plain[1], tk_t=tk_t), None
    return _mm(name, "nn", a, b, F32, res=x, scale=scale, norm_gain=next_gain, tm_t=512, tn_t=D, tk_t=min(tk_t, 2048))


def _ffn_fwd(tag, x, h, gain, wgu4, get_wd, next_gain):
    g, u, a = _ffn_up(f"{tag}_up", h, wgu4, 0)
    wd = get_wd(a).reshape(1, 1, DFF, D)
    x_new, _ = _out_proj(f"{tag}_down", a, Op(wd, "r"), x, None, 0.5, 2816)
    h_next = None if next_gain is None else _rms_fwd(f"{tag}_nextnorm", x_new, next_gain)
    return x_new, h_next, (x, gain, h, g, u, a)


def _ffn_bwd(tag, saved, dx, dxb, wgu4, wd4, put):
    x, gain, h, g, u, a = saved
    dgu = _ffn_dact(f"{tag}_dact", dxb, wd4, 0, g, u, 1024)
    dwd = _mm(f"{tag}_dwd", "tn", a, dxb, BF16, out=("r", 4, 1, 0), scale=0.5, tm_t=1408, tn_t=1024, tk_t=2048)
    dwgu = _mm(f"{tag}_dwgu", "tn", h, Op(dgu, "c"), BF16, out=("c", 4, 1, 0), tm_t=1024, tn_t=256, tk_t=8192)
    tok = put(dwgu, dwd)
    dh = _mm(f"{tag}_dh", "nt", Op(dgu, "c"), Op(wgu4, "c"), BF16, bias=jnp.zeros((1, D), F32) + tok, tk_t=2816)
    dx, dxb, dgain = _rms_bwd(f"{tag}_dnorm", x, gain, dh, dx)
    return dx, dxb, dgain


def _xattn_fwd(tag, x, hq, mem, gq, gm, wxq4, wxkv4, wxo4, next_gain):
    mn = _rms_fwd(f"{tag}_normm", mem, gm)
    q = _mm(f"{tag}_q", "nn", hq, Op(wxq4.reshape(1, 1, D, XW), "r"), BF16)
    kv = _mm(f"{tag}_kv", "nn", mn, Op(wxkv4, "r"), BF16)
    o = _xat_fwd(f"{tag}_att", q, kv)
    wxo = jnp.transpose(wxo4[:, 0], (1, 0, 2)).reshape(XW, D)
    x_new, h_next = _out_proj(f"{tag}_o", o, wxo, x, next_gain)
    return x_new, h_next, (x, mem, gq, gm, hq, mn, q, kv, o, wxo)


def _xattn_bwd(tag, saved, dx, dxb, wxq4, wxkv4, wxo4, put):
    x, mem, gq, gm, hq, mn, q, kv, o, wxo = saved
    dwxo = _mm(f"{tag}_dwo", "tn", o, dxb, BF16, out=("c", 4, 1, 0))
    do = _mm(f"{tag}_do", "nt", dxb, wxo, BF16)
    dq, dkv = _xat_bwd(f"{tag}_datt", q, kv, do)
    dwxq = _mm(f"{tag}_dwq", "tn", hq, dq, BF16, out=("r", 4, 1, 0))
    dwxkv = _mm(f"{tag}_dwkv", "tn", mn, dkv, BF16, out=("r", 4, 1, 0))
    tok = put(dwxq, dwxkv, dwxo)
    dhq = _mm(f"{tag}_dhq", "nt", dq, Op(wxq4.reshape(1, 1, D, XW), "r"), BF16, bias=jnp.zeros((1, D), F32) + tok)
    dmn = _mm(f"{tag}_dmn", "nt", dkv, Op(wxkv4, "r"), BF16)
    _, _, dgm = _rms_bwd(f"{tag}_dnormm", mem, gm, dmn)
    dx, dxb, dgq = _rms_bwd(f"{tag}_dnormq", x, gq, dhq, dx)
    return dx, dxb, dgq, dgm


def _even_fwd(tag, x, h, gain, w_main, w_dt, p, wout4, next_gain):
    proj = _mm(f"{tag}_in", "nn", h, w_main, BF16)
    dtr = _mm(f"{tag}_indt", "nn", h, w_dt, F32)
    mix = _gmlp_fwd(f"{tag}_gmlp", proj, p["lng"], p["lnb"], p["ws"], p["bs"])
    ypre, xc = _conv_fwd(f"{tag}_conv", proj, p["cw"], p["cb"])
    mix, prevs = _ssd_fwd(f"{tag}_ssd", xc, dtr, proj, p["dtb"], p["alog"], p["dsk"], p["nrm"], mix)
    x_new, h_next = _out_proj(f"{tag}_out", mix, Op(wout4.reshape(1, 1, 2 * D, D), "r"), x, next_gain)
    return x_new, h_next, (x, gain, h, proj, dtr, mix, ypre, xc, prevs)


def _even_bwd(tag, saved, dx, dxb, w_main, w_dt, p, wout4, put):
    x, gain, h, proj, dtr, mix, ypre, xc, prevs = saved
    T = x.shape[0]
    dwout = _mm(f"{tag}_dwout", "tn", mix, dxb, BF16, out=("r", 4, 1, 0))
    dmix = _mm(f"{tag}_dmix", "nt", dxb, Op(wout4, "r", 0), BF16)
    dproj = lax.empty((T, EVEN_MAIN), BF16)
    dproj, dlng, dlnb, dws, dbs = _gmlp_bwd(f"{tag}_dgmlp", proj, p["lng"], p["lnb"], p["ws"], p["bs"], dmix, dproj)
    dproj, dxc, ddtr, ddtb, dalog, ddsk, dnrm = _ssd_bwd(
        f"{tag}_dssd", xc, dtr, proj, prevs, p["dtb"], p["alog"], p["dsk"], p["nrm"], dmix, dproj)
    dproj, dcw, dcb = _conv_bwd(f"{tag}_dconv", proj, ypre, dxc, p["cw"], dproj)
    dw_main = _mm(f"{tag}_dwin", "tn", h, dproj, BF16, tm_t=1024, tn_t=256, tk_t=8192)
    dw_dt = _mm(f"{tag}_dwdt", "tn", h, ddtr, BF16)
    tok = put(dw_main, dw_dt, dwout)
    dh = _mm(f"{tag}_dh1", "nt", ddtr, w_dt + tok.astype(BF16), F32)
    dh = _mm(f"{tag}_dh2", "nt", dproj, w_main, BF16, res=dh, tk_t=3072)
    dx, dxb, dgain = _rms_bwd(f"{tag}_dnorm", x, gain, dh, dx)
    small = dict(lng=dlng, lnb=dlnb, ws=dws, bs=dbs, cw=dcw, cb=dcb, dtb=ddtb, alog=dalog, dsk=ddsk, nrm=dnrm)
    return dx, dxb, dgain, small


def _odd_fwd(tag, x, h, gain, wqkv4, bqkv, snk, wo4, cos, sin, next_gain):
    wqkv = jnp.transpose(wqkv4[:, 0], (1, 0, 2)).reshape(D, ODD_IN)
    qkv = _mm(f"{tag}_qkv", "nn", h, wqkv, F32, bias=bqkv, tn_t=1280)
    qkvr = _rope_fwd(f"{tag}_rope", qkv, cos, sin)
    o = _swa_fwd(f"{tag}_swa", qkvr, snk)
    x_new, h_next = _out_proj(f"{tag}_o", o, Op(wo4.reshape(1, 1, D, D), "r"), x, next_gain)
    return x_new, h_next, (x, gain, h, qkvr, o, wqkv)


def _odd_bwd(tag, saved, dx, dxb, wqkv4, snk, wo4, cos, sin, put):
    x, gain, h, qkvr, o, wqkv = saved
    dwo = _mm(f"{tag}_dwo", "tn", o, dxb, BF16, out=("r", 4, 1, 0))
    do = _mm(f"{tag}_do", "nt", dxb, Op(wo4.reshape(1, 1, D, D), "r"), BF16)
    dq, dcur, dprv, dsnk = _swa_bwd(f"{tag}_dswa", qkvr, snk, do)
    dqkv, dbias = _rope_bwd(f"{tag}_drope", dq, dcur, dprv, cos, sin)
    dwqkv = _mm(f"{tag}_dwqkv", "tn", h, dqkv, BF16, out=("c", 4, 1, 0), tn_t=640)
    tok = put(dwqkv, dwo)
    dh = _mm(f"{tag}_dh", "nt", dqkv, wqkv, BF16, bias=jnp.zeros((1, D), F32) + tok, tk_t=ODD_IN)
    dx, dxb, dgain = _rms_bwd(f"{tag}_dnorm", x, gain, dh, dx)
    return dx, dxb, dgain, dbias, dsnk


def _row(v):
    return v.reshape(1, -1).astype(F32)


def _pad_lanes(v, n=LANE):
    v = v.reshape(1, -1).astype(F32)
    return jnp.pad(v, ((0, 0), (0, n - v.shape[1])))


def _local_step(x, mem, positions, target, getw, P, putg):
    inv_freq = ROPE_THETA ** (-jnp.arange(0, ROT, 2, dtype=F32) / ROT)
    ang = positions.astype(F32)[:, None] * inv_freq
    cos8, sin8 = jnp.cos(ang), jnp.sin(ang)
    ones, zeros = jnp.ones((x.shape[0], AHD - ROT), F32), jnp.zeros((x.shape[0], AHD - ROT), F32)
    cos = jnp.tile(jnp.concatenate([cos8, cos8, ones], 1), (1, 2))
    sin = jnp.tile(jnp.concatenate([-sin8, sin8, zeros], 1), (1, 2))

    snk = _pad_lanes(P["sinks"])
    W = {}

    def w(name, layer, after):
        if (name, layer) not in W:
            W[name, layer] = getw(name, layer, after)
        return W[name, layer]

    saved = []
    h = _rms_fwd("l0_ffn1_norm", x, _row(P["norm_ffn1"][0]))
    for i in range(2):
        x, h, s1 = _ffn_fwd(f"l{i}_ffn1", x, h, _row(P["norm_ffn1"][i]), w("w_ffn1_gu", i, x),
                            functools.partial(w, "w_ffn1_down", i), _row(P["norm_mix"][i]))
        if i == 0:
            ev = dict(lng=_row(P["gm_ln_g"]), lnb=_row(P["gm_ln_b"]), ws=P["gm_ws"].reshape(GM_G, CH, CH),
                      bs=P["gm_bs"].reshape(GM_G, CH, 1), cw=w("conv_w", 0, x), cb=_row(P["conv_b"]),
                      dtb=_pad_lanes(P["dt_bias"]), alog=_pad_lanes(P["a_log"]), dsk=_pad_lanes(P["d_skip"]),
                      nrm=_row(P["ssd_norm"]))
            w_main, w_dt = w("w_in_even", 0, x)
            x, h, s2 = _even_fwd("l0_mix", x, h, _row(P["norm_mix"][0]), w_main, w_dt, ev, w("w_out_even", 0, x),
                                 _row(P["norm_xq"][0]))
        else:
            x, h, s2 = _odd_fwd("l1_mix", x, h, _row(P["norm_mix"][1]), w("w_qkv", 0, x), w("b_qkv", 0, x), snk,
                                w("w_o_odd", 0, x), cos, sin, _row(P["norm_xq"][1]))
        x, h, s3 = _xattn_fwd(f"l{i}_xat", x, h, mem, _row(P["norm_xq"][i]), _row(P["norm_mem"][i]),
                              w("w_xq", i, x), w("w_xkv", i, x), w("w_xo", i, x), _row(P["norm_ffn2"][i]))
        x, h, s4 = _ffn_fwd(f"l{i}_ffn2", x, h, _row(P["norm_ffn2"][i]), w("w_ffn2_gu", i, x),
                            functools.partial(w, "w_ffn2_down", i), _row(P["norm_ffn1"][1]) if i == 0 else None)
        saved.append((s1, s2, s3, s4))

    loss, dx, dxb, d_final = _loss_head("loss_head", x, _row(P["final_norm"]), target)

    sm = {}
    dn = {k: [None, None] for k in ("norm_ffn1", "norm_mix", "norm_xq", "norm_mem", "norm_ffn2")}
    for i in (1, 0):
        s1, s2, s3, s4 = saved[i]
        dx, dxb, dn["norm_ffn2"][i] = _ffn_bwd(
            f"l{i}_ffn2", s4, dx, dxb, W["w_ffn2_gu", i], W["w_ffn2_down", i],
            lambda dwgu, dwd, i=i: putg({("w_ffn2_gu", i): dwgu, ("w_ffn2_down", i): dwd}))
        dx, dxb, dn["norm_xq"][i], dn["norm_mem"][i] = _xattn_bwd(
            f"l{i}_xat", s3, dx, dxb, W["w_xq", i], W["w_xkv", i], W["w_xo", i],
            lambda dwxq, dwxkv, dwxo, i=i: putg({("w_xq", i): dwxq, ("w_xkv", i): dwxkv, ("w_xo", i): dwxo}))
        if i == 0:
            dx, dxb, dn["norm_mix"][0], sm_even = _even_bwd(
                "l0_mix", s2, dx, dxb, w_main, w_dt, ev, W["w_out_even", 0],
                lambda dw_main, dw_dt, dwout: putg({("w_in_even", 0): (dw_main, dw_dt), ("w_out_even", 0): dwout}))
        else:
            dx, dxb, dn["norm_mix"][1], sm["b_qkv"], sm["sinks"] = _odd_bwd(
                "l1_mix", s2, dx, dxb, W["w_qkv", 0], snk, W["w_o_odd", 0], cos, sin,
                lambda dwqkv, dwo: putg({("w_qkv", 0): dwqkv, ("w_o_odd", 0): dwo}))
        dx, dxb, dn["norm_ffn1"][i] = _ffn_bwd(
            f"l{i}_ffn1", s1, dx, dxb, W["w_ffn1_gu", i], W["w_ffn1_down", i],
            lambda dwgu, dwd, i=i: putg({("w_ffn1_gu", i): dwgu, ("w_ffn1_down", i): dwd}))
    for k, v in dn.items():
        sm[k] = jnp.concatenate(v, 0)
    sm.update(gm_ln_g=sm_even["lng"], gm_ln_b=sm_even["lnb"], gm_ws=sm_even["ws"], gm_bs=sm_even["bs"],
              conv_w=sm_even["cw"], conv_b=sm_even["cb"], dt_bias=sm_even["dtb"][:, :NH], a_log=sm_even["alog"][:, :NH],
              d_skip=sm_even["dsk"][:, :NH], ssd_norm=sm_even["nrm"], sinks=sm["sinks"][:, :AH], final_norm=d_final)
    return loss[0, 0], dx, sm


def _chip_peers():
    x, y, c = lax.axis_index("x"), lax.axis_index("y"), lax.axis_index("c")
    return 2 * x + y, [((1 - x, y, c), 2 * (1 - x) + y), ((x, 1 - y, c), 2 * x + (1 - y)),
                       ((1 - x, 1 - y, c), 2 * (1 - x) + (1 - y))]


def _any_specs(n):
    return [pl.BlockSpec(memory_space=pl.ANY)] * n


_HBM = pl.BlockSpec(memory_space=pltpu.HBM)
_SEM = pl.BlockSpec(memory_space=pltpu.SEMAPHORE)
_EFFECT = pltpu.SideEffectType.DATAFLOW_SIDE_EFFECTING


def _own_slot(piece, chip):
    zone = lax.empty((4,) + piece.shape, piece.dtype)
    return lax.dynamic_update_slice(zone, piece[None], (chip,) + (0,) * piece.ndim)


def _chip_copies(srcs, lands, ssems, rsems, mode="chips"):
    c = lax.axis_index("c")
    sib = (lax.axis_index("x"), lax.axis_index("y"), 1 - c)
    if mode == "sibling":
        return [pltpu.make_async_remote_copy(src_ref=srcs[i], dst_ref=lands[i], send_sem=ssems[i].at[0],
                                             recv_sem=rsems[i].at[0], device_id=sib, device_id_type=MESH)
                for i in range(len(lands))]
    me, peers = _chip_peers()
    if mode == "half":
        return [pltpu.make_async_remote_copy(src_ref=lands[i].at[me, c], dst_ref=lands[i].at[me, c], send_sem=ssems[i].at[j],
                                             recv_sem=rsems[i].at[j], device_id=dev, device_id_type=MESH)
                for i in range(len(lands)) for j, (dev, _) in enumerate(peers)]
    if mode == "pass_on":
        return [pltpu.make_async_remote_copy(src_ref=lands[i].at[chip, c], dst_ref=lands[i].at[chip, c],
                                             send_sem=ssems[i].at[j], recv_sem=rsems[i].at[j], device_id=sib, device_id_type=MESH)
                for i in range(len(lands)) for j, (_, chip) in enumerate(peers)]
    return [pltpu.make_async_remote_copy(src_ref=lands[i].at[me] if srcs[i] is None else srcs[i].at[chip],
                                         dst_ref=lands[i].at[me], send_sem=ssems[i].at[j], recv_sem=rsems[i].at[j],
                                         device_id=dev, device_id_type=MESH)
            for i in range(len(lands)) for j, (dev, chip) in enumerate(peers)]


def _exchange_start(name, srcs, lands, mode="chips"):
    n = len(lands)
    ns = 0 if srcs is None else n

    def body(*refs):
        src_refs = [None] * n if srcs is None else refs[:n]
        land_refs = refs[ns:ns + n]
        ssems, rsems = refs[ns + n:ns + 2 * n], refs[ns + 2 * n:ns + 3 * n]
        token = refs[2 * ns + 4 * n]
        for cp in _chip_copies(src_refs, land_refs, ssems, rsems, mode):
            cp.start()
        token[...] = jnp.zeros_like(token)

    ins = ([] if srcs is None else list(srcs)) + list(lands)
    res = pl.pallas_call(
        body, name=name,
        out_shape=[pltpu.SemaphoreType.DMA((1 if mode == "sibling" else 3,))] * (2 * n) + [pltpu.HBM(a.shape, a.dtype) for a in ins]
        + [S((8, LANE), F32)],
        in_specs=[_HBM] * (ns + n),
        out_specs=[_SEM] * (2 * n) + [_HBM] * (ns + n) + [pl.BlockSpec(memory_space=pltpu.VMEM)],
        input_output_aliases={i: 2 * n + i for i in range(ns + n)},
        compiler_params=pltpu.CompilerParams(has_side_effects=_EFFECT),
    )(*[pltpu.with_memory_space_constraint(a, pltpu.HBM) for a in ins])
    items = [(res[i], res[n + i], None if srcs is None else res[2 * n + i], res[2 * n + ns + i]) for i in range(n)]
    return items, res[2 * n + ns + n]


def _exchange_wait(name, item, after, mode="chips", whole=False):
    ssem, rsem, src, land = item
    ns = 0 if src is None else 1

    def body(*refs):
        src_ref = refs[0] if ns else None
        land_ref, ssem_ref, rsem_ref = refs[ns], refs[ns + 1], refs[ns + 2]
        for cp in _chip_copies([src_ref], [land_ref], [ssem_ref], [rsem_ref], mode):
            cp.wait_send()
            cp.wait_recv()

    ins = ([src] if ns else []) + [land]
    res = pl.pallas_call(
        body, name=name, out_shape=[pltpu.HBM(a.shape, a.dtype) for a in ins],
        in_specs=[_HBM] * (ns + 1) + [_SEM, _SEM, pl.BlockSpec(memory_space=pl.ANY)], out_specs=[_HBM] * (ns + 1),
        input_output_aliases={i: i for i in range(ns + 1)}, compiler_params=pltpu.CompilerParams(has_side_effects=_EFFECT),
    )(*ins, ssem, rsem, after)
    return res if whole else res[ns]


def _gather_all(name, v, after):
    def body(v_ref, _, o_ref, ssem, rsem, lsem):
        x, y, c = lax.axis_index("x"), lax.axis_index("y"), lax.axis_index("c")
        me = 4 * x + 2 * y + c
        loc = pltpu.make_async_copy(v_ref, o_ref.at[me], lsem)
        loc.start()
        copies = []
        for k in range(1, 8):
            fx, fy, fc = (k >> 2) & 1, (k >> 1) & 1, k & 1
            dev = (x ^ fx, y ^ fy, c ^ fc)
            cp = pltpu.make_async_remote_copy(src_ref=v_ref, dst_ref=o_ref.at[me], send_sem=ssem.at[k - 1],
                                              recv_sem=rsem.at[k - 1], device_id=dev, device_id_type=MESH)
            cp.start()
            copies.append(cp)
        for cp in copies:
            cp.wait()
        loc.wait()

    return pl.pallas_call(
        body, in_specs=_any_specs(2), out_specs=pl.BlockSpec(memory_space=pl.ANY), out_shape=S((8,) + v.shape, v.dtype),
        scratch_shapes=[pltpu.SemaphoreType.DMA((7,)), pltpu.SemaphoreType.DMA((7,)), pltpu.SemaphoreType.DMA(())],
        compiler_params=pltpu.CompilerParams(has_side_effects=True), name=name)(v, after)


def _row_tile(R, row_bytes, budget=4 << 20):
    if R * row_bytes <= budget or R % 16:
        return R
    t = max(16, budget // row_bytes // 16 * 16)
    while R % t:
        t -= 16
    return t


def _sum_slots(name, r, n):
    _, R, C = r.shape
    tr = _row_tile(R, C * (n * r.dtype.itemsize + 4))

    def body(r_ref, o_ref):
        acc = r_ref[0].astype(F32)
        for j in range(1, n):
            acc = acc + r_ref[j].astype(F32)
        o_ref[...] = acc

    return pl.pallas_call(
        body, grid=(R // tr,), in_specs=[pl.BlockSpec((n, tr, C), lambda i: (0, i, 0))],
        out_specs=pl.BlockSpec((tr, C), lambda i: (i, 0)), out_shape=S((R, C), F32),
        compiler_params=_params("parallel"), name=name)(r)


def _adamw(name, w, m, v, layer, g1, g2=None, into=None):
    nl, R, C = w.shape
    tr = _row_tile(R, C * 4 * 9)
    two, has_into = g2 is not None, into is not None

    def body(w_ref, m_ref, v_ref, g1_ref, *rest):
        rest = list(rest)
        g = g1_ref[...]
        if two:
            g = g + rest.pop(0)[...]
        g_ref, d_ref, nm_ref, nv_ref = rest[-4:]
        mn = B1 * m_ref[...] + (1.0 - B1) * g
        vn = B2 * v_ref[...] + (1.0 - B2) * jnp.square(g)
        m_hat = mn / (1.0 - B1 ** STEP)
        v_hat = vn / (1.0 - B2 ** STEP)
        g_ref[...] = g
        d_ref[...] = -LR * (m_hat / (jnp.sqrt(v_hat) + AEPS) + WD * w_ref[...])
        nm_ref[...] = mn
        nv_ref[...] = vn

    blk = pl.BlockSpec((tr, C), lambda i: (i, 0))
    lay = pl.BlockSpec((None, tr, C), lambda i: (layer, i, 0))
    args = [w, m, v, g1] + ([g2] if two else []) + (list(into) if has_into else [])
    in_specs = [lay] * 3 + [blk] * (2 if two else 1) + (_any_specs(4) if has_into else [])
    aliases = {len(args) - 4 + t: t for t in range(4)} if has_into else {}
    return pl.pallas_call(
        body, grid=(R // tr,), in_specs=in_specs, out_specs=[lay] * 4, out_shape=[S((nl, R, C), F32)] * 4,
        input_output_aliases=aliases, compiler_params=_params("parallel"), name=name)(*args)


_USE_ORDER = [("w_ffn1_gu", 0), ("w_ffn1_down", 0), ("conv_w", 0), ("w_in_even", 0), ("w_out_even", 0), ("w_xq", 0),
              ("w_xkv", 0), ("w_xo", 0), ("w_ffn2_gu", 0), ("w_ffn2_down", 0), ("w_ffn1_gu", 1), ("w_ffn1_down", 1),
              ("w_qkv", 0), ("b_qkv", 0), ("w_o_odd", 0), ("w_xq", 1), ("w_xkv", 1), ("w_xo", 1), ("w_ffn2_gu", 1),
              ("w_ffn2_down", 1)]
_SMALL = ["norm_ffn1", "norm_mix", "gm_ln_g", "gm_ln_b", "gm_ws", "gm_bs", "conv_w", "conv_b", "dt_bias", "a_log",
          "d_skip", "ssd_norm", "b_qkv", "sinks", "norm_xq", "norm_mem", "norm_ffn2", "final_norm"]
_WEIGHTS = ["norm_ffn1", "w_ffn1_gu", "w_ffn1_down", "norm_mix", "w_in_even", "gm_ln_g", "gm_ln_b", "gm_ws", "gm_bs",
            "conv_w", "conv_b", "dt_bias", "a_log", "d_skip", "ssd_norm", "w_out_even", "w_qkv", "b_qkv", "sinks",
            "w_o_odd", "norm_xq", "norm_mem", "w_xq", "w_xkv", "w_xo", "norm_ffn2", "w_ffn2_gu", "w_ffn2_down",
            "final_norm"]


def _pack(arrs):
    rows = []
    for a in arrs:
        f = a.reshape(-1).astype(F32)
        pad = (-f.shape[0]) % LANE
        rows.append(jnp.pad(f, (0, pad)).reshape(-1, LANE))
    out = jnp.concatenate(rows, 0)
    pad = (-out.shape[0]) % 8
    return jnp.pad(out, ((0, pad), (0, 0)))


def _unpack(packed, shapes):
    outs, r = [], 0
    for shp in shapes:
        n = math.prod(shp)
        nr = -(-n // LANE)
        outs.append(packed[r:r + nr].reshape(-1)[:n].reshape(shp))
        r += nr
    return outs


def kernel(x, mem, positions, norm_ffn1, w_ffn1_gu, w_ffn1_down, norm_mix, w_in_even, gm_ln_g, gm_ln_b, gm_ws, gm_bs, conv_w, conv_b, dt_bias, a_log, d_skip, ssd_norm, w_out_even, w_qkv, b_qkv, sinks, w_o_odd, norm_xq, norm_mem, w_xq, w_xkv, w_xo, norm_ffn2, w_ffn2_gu, w_ffn2_down, final_norm, loss_target, m_norm_ffn1, m_w_ffn1_gu, m_w_ffn1_down, m_norm_mix, m_w_in_even, m_gm_ln_g, m_gm_ln_b, m_gm_ws, m_gm_bs, m_conv_w, m_conv_b, m_dt_bias, m_a_log, m_d_skip, m_ssd_norm, m_w_out_even, m_w_qkv, m_b_qkv, m_sinks, m_w_o_odd, m_norm_xq, m_norm_mem, m_w_xq, m_w_xkv, m_w_xo, m_norm_ffn2, m_w_ffn2_gu, m_w_ffn2_down, m_final_norm, v_norm_ffn1, v_w_ffn1_gu, v_w_ffn1_down, v_norm_mix, v_w_in_even, v_gm_ln_g, v_gm_ln_b, v_gm_ws, v_gm_bs, v_conv_w, v_conv_b, v_dt_bias, v_a_log, v_d_skip, v_ssd_norm, v_w_out_even, v_w_qkv, v_b_qkv, v_sinks, v_w_o_odd, v_norm_xq, v_norm_mem, v_w_xq, v_w_xkv, v_w_xo, v_norm_ffn2, v_w_ffn2_gu, v_w_ffn2_down, v_final_norm):
    a = dict(locals())
    w = {k: a[k] for k in _WEIGHTS}
    mom = {k: a["m_" + k] for k in _WEIGHTS}
    var = {k: a["v_" + k] for k in _WEIGHTS}
    chip = 2 * lax.axis_index("x") + lax.axis_index("y")

    shards = [w[k][i:i + 1] if k in ("conv_w", "b_qkv") else w[k][i:i + 1].astype(BF16) for k, i in _USE_ORDER]
    half = shards[0].reshape((2, shards[0].shape[1] // 2) + shards[0].shape[2:])
    (first,), first_started = _exchange_start("gather_start_first", None, [_own_slot(half, chip)], "half")
    small = _USE_ORDER.index(("conv_w", 0))
    shards[small] = shards[small] + first_started[0, 0]
    rest, rest_started = _exchange_start("gather_start_rest", None, [_own_slot(s, chip) for s in shards[1:]])
    pending = dict(zip(_USE_ORDER[1:], rest))

    def getw(name, layer, after):
        if (name, layer) == _USE_ORDER[0]:
            zone = _exchange_wait("gather_wait_first_half", first, after, "half")
            (passed,), _ = _exchange_start("gather_pass_on_first", None, [zone], "pass_on")
            return _exchange_wait("gather_wait_first_passed", passed, after, "pass_on").reshape((4,) + shards[0].shape)
        got = _exchange_wait(f"gather_wait_{name}_{layer}", pending.pop((name, layer)), after)
        if name == "w_in_even":
            w_in = jnp.transpose(got[:, 0], (1, 0, 2)).reshape(D, EVEN_IN)
            return w_in[:, :EVEN_MAIN], jnp.pad(w_in[:, EVEN_MAIN:], ((0, 0), (0, LANE - (EVEN_IN - EVEN_MAIN))))
        if name == "conv_w":
            return jnp.transpose(got[:, 0], (1, 0, 2)).reshape(4, CONV_C)
        if name == "b_qkv":
            return got.reshape(1, ODD_IN)
        return got

    sent = []

    def putg(grads):
        names, arrs = [], []
        for (name, layer), g in grads.items():
            if name == "w_in_even":
                dw_in = jnp.concatenate([g[0], g[1][:, :EVEN_IN - EVEN_MAIN]], 1)
                g = jnp.transpose(dw_in.reshape(D, 4, EVEN_IN // 4), (1, 0, 2)).reshape(4, 1, D, EVEN_IN // 4)
            names.append((name, layer))
            arrs.append(g)
        own = [_own_slot(lax.dynamic_index_in_dim(g, chip, 0, keepdims=False), chip) for g in arrs]
        its, tok = _exchange_start(f"scatter_start_{names[0][0]}_{names[0][1]}", arrs, own)
        sent.append(list(zip(names, its)))
        return tok[0, 0]

    P = {k: w[k] for k in _SMALL}
    P["norm_ffn1"] = P["norm_ffn1"] + rest_started[0, 0]
    loss, grad_x, sm = _local_step(x[0], mem[0], positions[0], loss_target[0], getw, P, putg)
    loss = lax.psum(loss, ("x", "y", "c"))

    out = {}

    def update(groups, after):
        flying = []
        for grp in groups:
            part = []
            for (name, layer), it in grp:
                r = _exchange_wait(f"scatter_wait_{name}_{layer}", it, after)
                part.append(_sum_slots(f"sum_{name}_{layer}", r.reshape(4, -1, r.shape[-1]), 4))
            name0, layer0 = grp[0][0]
            its, after = _exchange_start(f"swap_start_{name0}_{layer0}", part, [lax.empty(p.shape, p.dtype) for p in part],
                                         "sibling")
            flying += [(nm, it) for (nm, _), it in zip(grp, its)]
        for (name, layer), it in flying:
            p1, p2 = _exchange_wait(f"swap_wait_{name}_{layer}", it, after, "sibling", whole=True)
            out[name] = _adamw(f"adamw_{name}_{layer}", w[name], mom[name], var[name], layer, p1, p2, out.get(name))
            after = out[name][0]

    update(sent[:-1], grad_x)
    done_a = out["w_out_even"][0]
    update(sent[-1:], done_a)

    full_shapes = {k: w[k].shape for k in _SMALL}
    full_shapes["conv_w"], full_shapes["b_qkv"] = (1, 4, CONV_C), (1, ODD_IN)
    packed = _pack([sm[k] for k in _SMALL])
    total = _sum_slots("sum_small", _gather_all("gather_small", packed, done_a), 8)
    gs = dict(zip(_SMALL, _unpack(total, [full_shapes[k] for k in _SMALL])))
    gs["conv_w"] = lax.dynamic_slice_in_dim(gs["conv_w"], chip * (CONV_C // 4), CONV_C // 4, 2)
    gs["b_qkv"] = lax.dynamic_slice_in_dim(gs["b_qkv"], chip * (ODD_IN // 4), ODD_IN // 4, 1)
    res = _adamw("adamw_small", _pack([w[k] for k in _SMALL])[None], _pack([mom[k] for k in _SMALL])[None],
                 _pack([var[k] for k in _SMALL])[None], 0, _pack([gs[k] for k in _SMALL]))
    shapes = [w[k].shape for k in _SMALL]
    for k, g, d, nm, nv in zip(_SMALL, *[_unpack(r[0], shapes) for r in res]):
        out[k] = [g, d, nm, nv]

    return (loss, grad_x[None], *[out[k][0] for k in _WEIGHTS], *[out[k][1] for k in _WEIGHTS],
            *[out[k][2] for k in _WEIGHTS], *[out[k][3] for k in _WEIGHTS])
```
